```python
import math
import jax, jax.numpy as jnp
from jax import lax
import numpy as np

D_MODEL = 1024
BATCH = 8
SEQ = 2048
DEPTH = 4

NSA_HEADS = 8
NSA_KV_HEADS = 2
NSA_GROUP = NSA_HEADS // NSA_KV_HEADS
HEAD_DIM = 64
ROPE_DIMS = HEAD_DIM // 4
ROPE_THETA = 500000.0
CMP_BLOCK = 32
CMP_STRIDE = 16
CMP_HIDDEN = 2 * HEAD_DIM
SLC_BLOCK = 64
SLC_TOPK = 16
WINDOW = 512
Q_BLOCK = 128
SLC_Q_CHUNK = 32
NSA_WIDTH = NSA_HEADS * HEAD_DIM
KV_WIDTH = NSA_KV_HEADS * HEAD_DIM
S5_WIDTH = 256
S5_GROUP_CH = 16
S5_GROUPS = S5_WIDTH // S5_GROUP_CH
S5_STATE = 64
N_BRANCHES = 2
N_EXPERT_GROUPS = 4
EXPERTS_PER_GROUP = 8
N_EXPERTS = N_EXPERT_GROUPS * EXPERTS_PER_GROUP
EXPERT_TOP_K = 2
EXPERT_FF = 512
MOE_BLOCK = 128
RMS_EPS = 1e-6
NEG_BIG = -1e30
IN_COLS = NSA_WIDTH + 6 * KV_WIDTH + 3 * NSA_HEADS + S5_WIDTH + N_BRANCHES * D_MODEL

kernel_name = 'hybrid_nsa_s5_hiermoe_trunk'


def rmsnorm(x, gain):
    xf = x.astype(jnp.float32)
    xf = xf * lax.rsqrt(jnp.mean(xf * xf, axis=-1, keepdims=True) + RMS_EPS)
    return xf.astype(x.dtype) * gain


def rope_partial(x, pos):
    inv_freq = ROPE_THETA ** (-jnp.arange(0, ROPE_DIMS, 2, dtype=jnp.float32) / ROPE_DIMS)
    ang = pos.astype(jnp.float32)[..., None] * inv_freq
    cos = jnp.cos(ang).astype(x.dtype)
    sin = jnp.sin(ang).astype(x.dtype)
    half = ROPE_DIMS // 2
    x1 = x[..., :half]
    x2 = x[..., half:ROPE_DIMS]
    return jnp.concatenate([x1 * cos - x2 * sin, x2 * cos + x1 * sin, x[..., ROPE_DIMS:]], axis=-1)


def masked_softmax(s, mask):
    s = jnp.where(mask, s.astype(jnp.float32), NEG_BIG)
    return jnp.where(mask, jax.nn.softmax(s, axis=-1), 0.0)


def nsa_mixer(q, k_cmp, v_cmp, k_slc, v_slc, k_win, v_win, gate_logits, positions,
              q_gain, k_gain, ck_pos, ck_w1, ck_w2, cv_pos, cv_w1, cv_w2):
    B, S, _ = q.shape
    Hkv, G, dh = NSA_KV_HEADS, NSA_GROUP, HEAD_DIM
    scale = dh ** -0.5
    t_idx = jnp.arange(S)

    def heads(t, n):
        return t.reshape(B, S, n, dh).transpose(0, 2, 1, 3)

    pos = positions[:, None, :]
    qh = rope_partial(rmsnorm(heads(q, NSA_HEADS), q_gain), pos).reshape(B, Hkv, G, S, dh)

    n_cmp = (S - CMP_BLOCK) // CMP_STRIDE + 1
    blk_idx = np.arange(n_cmp)[:, None] * CMP_STRIDE + np.arange(CMP_BLOCK)[None, :]

    def compress(t, pos_emb, w1, w2):
        blocks = heads(t, Hkv)[:, :, blk_idx] + pos_emb
        flat = blocks.reshape(B, Hkv, n_cmp, CMP_BLOCK * dh)
        return jax.nn.gelu(flat @ w1) @ w2

    cmp_end = blk_idx[:, -1]
    kc = rope_partial(rmsnorm(compress(k_cmp, ck_pos, ck_w1, ck_w2), k_gain), positions[:, cmp_end][:, None, :])
    vc = compress(v_cmp, cv_pos, cv_w1, cv_w2)
    s_c = jnp.einsum('bhgsd,bhcd->bhgsc', qh, kc) * scale
    mask_c = jnp.asarray(cmp_end)[None, :] <= t_idx[:, None]
    p_c = masked_softmax(s_c, mask_c)
    o_c = jnp.einsum('bhgsc,bhcd->bhgsd', p_c.astype(vc.dtype), vc)

    n_slc = S // SLC_BLOCK
    top_k = min(SLC_TOPK, n_slc)
    cmp_start = np.arange(n_cmp) * CMP_STRIDE
    slc_start = np.arange(n_slc) * SLC_BLOCK
    overlap = ((cmp_start[:, None] < slc_start[None, :] + SLC_BLOCK)
               & (cmp_start[:, None] + CMP_BLOCK > slc_start[None, :])).astype(np.float32)
    imp = jnp.einsum('bhgsc,cn->bhsn', p_c, jnp.asarray(overlap))
    j = jnp.arange(n_slc)[None, :]
    cur = (t_idx // SLC_BLOCK)[:, None]
    forced = (j == 0) | (j == cur) | (j == cur - 1)
    imp = jnp.where(forced, jnp.inf, jnp.where(j > cur, -jnp.inf, imp))
    _, sel = lax.top_k(imp, top_k)

    ks_blk = rope_partial(rmsnorm(heads(k_slc, Hkv), k_gain), pos).reshape(B, Hkv, n_slc, SLC_BLOCK, dh)
    vs_blk = heads(v_slc, Hkv).reshape(B, Hkv, n_slc, SLC_BLOCK, dh)
    n_chunk = S // SLC_Q_CHUNK
    q_ch = qh.reshape(B, Hkv, G, n_chunk, SLC_Q_CHUNK, dh).transpose(3, 0, 1, 2, 4, 5)
    sel_ch = sel.reshape(B, Hkv, n_chunk, SLC_Q_CHUNK, top_k).transpose(2, 0, 1, 3, 4)
    t_ch = t_idx.reshape(n_chunk, SLC_Q_CHUNK)
    b_ix = jnp.arange(B)[:, None, None, None]
    h_ix = jnp.arange(Hkv)[None, :, None, None]

    def slc_chunk(args):
        qc, sc, tc = args
        kg = ks_blk[b_ix, h_ix, sc]
        vg = vs_blk[b_ix, h_ix, sc]
        C = tc.shape[0]
        s = jnp.einsum('bhgqd,bhqnkd->bhgqnk', qc, kg).reshape(B, Hkv, G, C, top_k * SLC_BLOCK) * scale
        key_pos = sc[..., None] * SLC_BLOCK + jnp.arange(SLC_BLOCK)
        mask = (key_pos <= tc[None, None, :, None, None]).reshape(B, Hkv, 1, C, top_k * SLC_BLOCK)
        p = masked_softmax(s, mask)
        return jnp.einsum('bhgqm,bhqmd->bhgqd', p.astype(vg.dtype), vg.reshape(B, Hkv, C, top_k * SLC_BLOCK, dh))

    o_s = lax.map(slc_chunk, (q_ch, sel_ch, t_ch))
    o_s = o_s.transpose(1, 2, 3, 0, 4, 5).reshape(B, Hkv, G, S, dh)

    nb = S // Q_BLOCK
    n_prev = WINDOW // Q_BLOCK

    def windows(t):
        tb = jnp.pad(t.reshape(B, Hkv, nb, Q_BLOCK, dh), ((0, 0), (0, 0), (n_prev, 0), (0, 0), (0, 0)))
        return jnp.concatenate([tb[:, :, i:i + nb] for i in range(n_prev + 1)], axis=3)

    kw = windows(rope_partial(rmsnorm(heads(k_win, Hkv), k_gain), pos))
    vw = windows(heads(v_win, Hkv))
    qw = qh.reshape(B, Hkv, G, nb, Q_BLOCK, dh)
    s_w = jnp.einsum('bhgnqd,bhnkd->bhgnqk', qw, kw) * scale
    q_pos = jnp.arange(nb)[:, None] * Q_BLOCK + jnp.arange(Q_BLOCK)[None, :]
    k_pos = (jnp.arange(nb)[:, None] - n_prev) * Q_BLOCK + jnp.arange((n_prev + 1) * Q_BLOCK)[None, :]
    diff = q_pos[:, :, None] - k_pos[:, None, :]
    mask_w = (diff >= 0) & (diff < WINDOW) & (k_pos[:, None, :] >= 0)
    p_w = masked_softmax(s_w, mask_w)
    o_w = jnp.einsum('bhgnqk,bhnkd->bhgnqd', p_w.astype(vw.dtype), vw).reshape(B, Hkv, G, S, dh)

    g = jax.nn.sigmoid(gate_logits.reshape(B, S, NSA_HEADS, 3).transpose(0, 2, 1, 3)).reshape(B, Hkv, G, S, 3)
    o = g[..., 0:1] * o_c + g[..., 1:2] * o_s + g[..., 2:3] * o_w
    return o.transpose(0, 3, 1, 2, 4).reshape(B, S, NSA_WIDTH)


def s5_mixer(u, a_re, a_im, log_dt, b_re, b_im, c_re, c_im, d_skip, w_glu):
    B, S, _ = u.shape
    f32 = jnp.float32
    lam_re = jnp.minimum(a_re.astype(f32), -1e-4)
    lam_im = a_im.astype(f32)
    dt = jnp.exp(log_dt.astype(f32))[:, None]
    mag = jnp.exp(lam_re * dt)
    ang = lam_im * dt
    lb_re = mag * jnp.cos(ang)
    lb_im = mag * jnp.sin(ang)
    den = lam_re * lam_re + lam_im * lam_im
    f_re = ((lb_re - 1.0) * lam_re + lb_im * lam_im) / den
    f_im = (lb_im * lam_re - (lb_re - 1.0) * lam_im) / den
    br = b_re.astype(f32)
    bi = b_im.astype(f32)
    bb_re = f_re[..., None] * br - f_im[..., None] * bi
    bb_im = f_re[..., None] * bi + f_im[..., None] * br
    ug = u.astype(f32).reshape(B, S, S5_GROUPS, S5_GROUP_CH)
    bu_re = jnp.einsum('bsgc,gpc->sbgp', ug, bb_re)
    bu_im = jnp.einsum('bsgc,gpc->sbgp', ug, bb_im)
    a_re_t = jnp.broadcast_to(lb_re, bu_re.shape)
    a_im_t = jnp.broadcast_to(lb_im, bu_im.shape)

    def combine(left, right):
        ar1, ai1, br1, bi1 = left
        ar2, ai2, br2, bi2 = right
        return (ar2 * ar1 - ai2 * ai1, ar2 * ai1 + ai2 * ar1,
                ar2 * br1 - ai2 * bi1 + br2, ar2 * bi1 + ai2 * br1 + bi2)

    _, _, x_re, x_im = lax.associative_scan(combine, (a_re_t, a_im_t, bu_re, bu_im), axis=0)
    y = (jnp.einsum('sbgp,gcp->bsgc', x_re, c_re.astype(f32))
         - jnp.einsum('sbgp,gcp->bsgc', x_im, c_im.astype(f32))).reshape(B, S, S5_WIDTH)
    y = y + d_skip.astype(f32) * ug.reshape(B, S, S5_WIDTH)
    y = jax.nn.gelu(y).astype(u.dtype)
    z = y @ w_glu
    return z[..., :S5_WIDTH] * jax.nn.sigmoid(z[..., S5_WIDTH:])


def hier_moe(h, w_rg, b_rg, w_re, b_re, w_gate, w_up, w_down):
    B, S, D = h.shape
    N = B * S
    ht = h.reshape(N, D)
    g_logits = (ht @ w_rg + b_rg).astype(jnp.float32)
    g_sel = jnp.argmax(g_logits, axis=-1)
    g_w = jnp.take_along_axis(jax.nn.softmax(g_logits, axis=-1), g_sel[:, None], axis=1)
    e_logits = (ht @ w_re + b_re).astype(jnp.float32).reshape(N, N_EXPERT_GROUPS, EXPERTS_PER_GROUP)
    e_logits = jnp.take_along_axis(e_logits, g_sel[:, None, None], axis=1)[:, 0]
    top_p, top_i = lax.top_k(jax.nn.softmax(e_logits, axis=-1), EXPERT_TOP_K)
    w = g_w * top_p / jnp.sum(top_p, axis=-1, keepdims=True)
    expert = g_sel[:, None].astype(jnp.int32) * EXPERTS_PER_GROUP + top_i

    A = N * EXPERT_TOP_K
    e_flat = expert.reshape(A)
    tok = jnp.repeat(jnp.arange(N, dtype=jnp.int32), EXPERT_TOP_K)
    order = jnp.argsort(e_flat)
    e_sorted = e_flat[order]
    tok_sorted = tok[order]
    w_sorted = w.reshape(A)[order]
    counts = jnp.zeros((N_EXPERTS,), jnp.int32).at[e_flat].add(1)
    starts = jnp.cumsum(counts) - counts
    padded = (counts + MOE_BLOCK - 1) // MOE_BLOCK * MOE_BLOCK
    pends = jnp.cumsum(padded)
    pstarts = pends - padded
    dest = pstarts[e_sorted] + (jnp.arange(A, dtype=jnp.int32) - starts[e_sorted])
    cap = A + N_EXPERTS * MOE_BLOCK
    n_blk = cap // MOE_BLOCK
    buf = jnp.zeros((cap, D), h.dtype).at[dest].set(ht[tok_sorted])
    blk_expert = jnp.minimum(jnp.searchsorted(pends, jnp.arange(n_blk) * MOE_BLOCK, side='right'), N_EXPERTS - 1)

    def expert_block(args):
        xb, e = args
        return (jax.nn.silu(xb @ w_gate[e]) * (xb @ w_up[e])) @ w_down[e]

    out_buf = lax.map(expert_block, (buf.reshape(n_blk, MOE_BLOCK, D), blk_expert)).reshape(cap, D)
    y_a = out_buf[dest] * w_sorted[:, None].astype(h.dtype)
    return jax.ops.segment_sum(y_a, tok_sorted, num_segments=N).reshape(B, S, D)


def setup_inputs(seed: int = 0) -> dict:
    key = jax.random.key(seed)
    keys = iter(jax.random.split(key, 48))
    f32 = jnp.float32
    L = DEPTH

    def nrm(shape, scale):
        return jax.random.normal(next(keys), shape, f32) * scale

    res_scale = (2.0 * DEPTH) ** -0.5
    x = nrm((BATCH, SEQ, D_MODEL), 1.0)
    start = jax.random.randint(next(keys), (BATCH, 1), 0, 4096, dtype=jnp.int32)
    positions = start + jnp.arange(SEQ, dtype=jnp.int32)[None, :]
    n_idx = jnp.arange(S5_STATE, dtype=f32)
    return {
        'x': x,
        'positions': positions,
        'norm_mix': 1.0 + nrm((L, D_MODEL), 0.02),
        'norm_ffn': 1.0 + nrm((L, D_MODEL), 0.02),
        'w_in': nrm((L, D_MODEL, IN_COLS), D_MODEL ** -0.5),
        'q_gain': 1.0 + nrm((L, HEAD_DIM), 0.02),
        'k_gain': 1.0 + nrm((L, HEAD_DIM), 0.02),
        'cmp_k_pos': nrm((L, CMP_BLOCK, HEAD_DIM), 0.1),
        'cmp_k_w1': nrm((L, CMP_BLOCK * HEAD_DIM, CMP_HIDDEN), (CMP_BLOCK * HEAD_DIM) ** -0.5),
        'cmp_k_w2': nrm((L, CMP_HIDDEN, HEAD_DIM), CMP_HIDDEN ** -0.5),
        'cmp_v_pos': nrm((L, CMP_BLOCK, HEAD_DIM), 0.1),
        'cmp_v_w1': nrm((L, CMP_BLOCK * HEAD_DIM, CMP_HIDDEN), (CMP_BLOCK * HEAD_DIM) ** -0.5),
        'cmp_v_w2': nrm((L, CMP_HIDDEN, HEAD_DIM), CMP_HIDDEN ** -0.5),
        's5_a_re': -0.5 + nrm((L, S5_GROUPS, S5_STATE), 0.01),
        's5_a_im': math.pi * n_idx + nrm((L, S5_GROUPS, S5_STATE), 0.01),
        's5_log_dt': jax.random.uniform(next(keys), (L, S5_GROUPS), f32, math.log(1e-3), math.log(1e-1)),
        's5_b_re': nrm((L, S5_GROUPS, S5_STATE, S5_GROUP_CH), (2.0 * S5_GROUP_CH) ** -0.5),
        's5_b_im': nrm((L, S5_GROUPS, S5_STATE, S5_GROUP_CH), (2.0 * S5_GROUP_CH) ** -0.5),
        's5_c_re': nrm((L, S5_GROUPS, S5_GROUP_CH, S5_STATE), (2.0 * S5_STATE) ** -0.5),
        's5_c_im': nrm((L, S5_GROUPS, S5_GROUP_CH, S5_STATE), (2.0 * S5_STATE) ** -0.5),
        's5_d': nrm((L, S5_WIDTH), 0.5),
        's5_glu': nrm((L, S5_WIDTH, 2 * S5_WIDTH), S5_WIDTH ** -0.5),
        'proj_nsa': nrm((L, NSA_WIDTH, D_MODEL), NSA_WIDTH ** -0.5),
        'proj_s5': nrm((L, S5_WIDTH, D_MODEL), S5_WIDTH ** -0.5),
        'w_out': nrm((L, D_MODEL, D_MODEL), D_MODEL ** -0.5 * res_scale),
        'router_group': nrm((L, D_MODEL, N_EXPERT_GROUPS), D_MODEL ** -0.5),
        'router_group_bias': nrm((L, N_EXPERT_GROUPS), 0.01),
        'router_expert': nrm((L, D_MODEL, N_EXPERTS), D_MODEL ** -0.5),
        'router_expert_bias': nrm((L, N_EXPERTS), 0.01),
        'expert_gate': nrm((L, N_EXPERTS, D_MODEL, EXPERT_FF), D_MODEL ** -0.5),
        'expert_up': nrm((L, N_EXPERTS, D_MODEL, EXPERT_FF), D_MODEL ** -0.5),
        'expert_down': nrm((L, N_EXPERTS, EXPERT_FF, D_MODEL), EXPERT_FF ** -0.5 * res_scale),
    }


def reference(x, positions, norm_mix, norm_ffn, w_in, q_gain, k_gain,
              cmp_k_pos, cmp_k_w1, cmp_k_w2, cmp_v_pos, cmp_v_w1, cmp_v_w2,
              s5_a_re, s5_a_im, s5_log_dt, s5_b_re, s5_b_im, s5_c_re, s5_c_im, s5_d, s5_glu,
              proj_nsa, proj_s5, w_out, router_group, router_group_bias, router_expert,
              router_expert_bias, expert_gate, expert_up, expert_down):
    sizes = (NSA_WIDTH, KV_WIDTH, KV_WIDTH, KV_WIDTH, KV_WIDTH, KV_WIDTH, KV_WIDTH,
             3 * NSA_HEADS, S5_WIDTH, D_MODEL, D_MODEL)
    split_at = np.cumsum(sizes)[:-1].tolist()
    for l in range(DEPTH):
        h = rmsnorm(x, norm_mix[l])
        z = h @ w_in[l]
        q, kc, vc, ks, vs, kw, vw, nsa_g, u, g_a, g_s = jnp.split(z, split_at, axis=-1)
        o_a = nsa_mixer(q, kc, vc, ks, vs, kw, vw, nsa_g, positions, q_gain[l], k_gain[l],
                        cmp_k_pos[l], cmp_k_w1[l], cmp_k_w2[l], cmp_v_pos[l], cmp_v_w1[l], cmp_v_w2[l])
        o_b = s5_mixer(u, s5_a_re[l], s5_a_im[l], s5_log_dt[l], s5_b_re[l], s5_b_im[l],
                       s5_c_re[l], s5_c_im[l], s5_d[l], s5_glu[l])
        merged = jax.nn.sigmoid(g_a) * (o_a @ proj_nsa[l]) + jax.nn.sigmoid(g_s) * (o_b @ proj_s5[l])
        x = x + merged @ w_out[l]
        x = x + hier_moe(rmsnorm(x, norm_ffn[l]), router_group[l], router_group_bias[l],
                         router_expert[l], router_expert_bias[l],
                         expert_gate[l], expert_up[l], expert_down[l])
    return x
```

```python
import functools
import math

import jax
import jax.numpy as jnp
import numpy as np
from jax import lax
from jax.experimental import pallas as pl
from jax.experimental.pallas import tpu as pltpu

F32 = jnp.float32
BF16 = jnp.bfloat16

D_MODEL = 1024
NSA_HEADS = 8
NSA_KV_HEADS = 2
NSA_GROUP = NSA_HEADS // NSA_KV_HEADS
HEAD_DIM = 64
ROPE_DIMS = HEAD_DIM // 4
ROPE_THETA = 500000.0
CMP_BLOCK = 32
CMP_STRIDE = 16
CMP_HIDDEN = 2 * HEAD_DIM
SLC_BLOCK = 64
SLC_TOPK = 16
WINDOW = 512
NSA_WIDTH = NSA_HEADS * HEAD_DIM
KV_WIDTH = NSA_KV_HEADS * HEAD_DIM
S5_WIDTH = 256
S5_GROUP_CH = 16
S5_GROUPS = S5_WIDTH // S5_GROUP_CH
S5_STATE = 64
S5_NSTATE = S5_GROUPS * S5_STATE
N_EXPERT_GROUPS = 4
EXPERTS_PER_GROUP = 8
N_EXPERTS = N_EXPERT_GROUPS * EXPERTS_PER_GROUP
EXPERT_TOP_K = 2
EXPERT_FF = 512
RMS_EPS = 1e-6
NEG_BIG = -1e30

LANES = 128
SUBLANES = 8
VMEM_LIMIT = 56 * 1024 * 1024

IN_TM = 512
ATT_TQ = 128
ATT_TK = 256
S5_T = 128
MRG_TM = 256
RT_TT = 512
MOE_BLK = 256
DSP_TT = 256

C_Q, C_KS, C_KW, C_KC, C_VC, C_VS, C_VW, C_U, C_G = 0, 512, 640, 768, 896, 1024, 1152, 1280, 1536
IN_W = 1664


def _cparams(*sem):
    return pltpu.CompilerParams(dimension_semantics=sem, vmem_limit_bytes=VMEM_LIMIT)


def _gelu(x):
    return 0.5 * x * (1.0 + jnp.tanh(math.sqrt(2.0 / math.pi) * (x + 0.044715 * (x * x * x))))


def _sigmoid(x):
    return 1.0 / (1.0 + jnp.exp(-x))


def _dot(a, b):
    return jnp.dot(a, b, preferred_element_type=F32)


def _dot_t(a, b):
    return lax.dot_general(a, b, (((1,), (1,)), ((), ())), preferred_element_type=F32)


def _split_bf16(x):
    hi = x.astype(BF16)
    lo = (x - hi.astype(F32)).astype(BF16)
    return hi, lo


def _rope_kernel(pos_ref, freq_ref, sign_ref, c_ref, s_ref):
    ang = pos_ref[...].astype(F32) * freq_ref[...]
    c_ref[...] = jnp.cos(ang)
    s_ref[...] = jnp.sin(ang) * sign_ref[...]


def _rope_tables(pos_col, inv_freq):
    n = pos_col.shape[0]
    lane = np.arange(LANES) % HEAD_DIM
    sel = np.zeros((LANES, ROPE_DIMS // 2), np.float32)
    for l in range(LANES):
        if lane[l] < ROPE_DIMS:
            sel[l, lane[l] % (ROPE_DIMS // 2)] = 1.0
    freq = (jnp.asarray(sel) * inv_freq[None, :]).sum(-1)[None, :]
    sign = np.where(lane < ROPE_DIMS // 2, -1.0, 1.0).astype(np.float32)[None, :]
    tm = 1024
    return pl.pallas_call(
        _rope_kernel,
        grid=(n // tm,),
        in_specs=[pl.BlockSpec((tm, 1), lambda i: (i, 0)),
                  pl.BlockSpec((1, LANES), lambda i: (0, 0)),
                  pl.BlockSpec((1, LANES), lambda i: (0, 0))],
        out_specs=[pl.BlockSpec((tm, LANES), lambda i: (i, 0))] * 2,
        out_shape=[jax.ShapeDtypeStruct((n, LANES), F32)] * 2,
        compiler_params=_cparams("parallel"),
        name="rope_tables",
    )(pos_col, freq, jnp.asarray(sign))


def _headnorm_rope(t, gain, c, s):
    lane = lax.broadcasted_iota(jnp.int32, t.shape, 1)
    lo = lane < HEAD_DIM
    sq = t * t
    s_lo = jnp.sum(jnp.where(lo, sq, 0.0), axis=-1, keepdims=True)
    s_hi = jnp.sum(jnp.where(lo, 0.0, sq), axis=-1, keepdims=True)
    ms = jnp.where(lo, s_lo, s_hi) * (1.0 / HEAD_DIM)
    tn = (t * lax.rsqrt(ms + RMS_EPS)) * gain
    half = ROPE_DIMS // 2
    partner = jnp.where((lane & (HEAD_DIM - 1)) < half,
                        pltpu.roll(tn, LANES - half, 1), pltpu.roll(tn, half, 1))
    return tn * c + partner * s


def _in_proj_kernel(x_ref, gain_ref, w_ref, c_ref, s_ref, qg_ref, kg_ref,
                    q_ref, ks_ref, kw_ref, kc_ref, vc_ref, vs_ref, vw_ref, u_ref, g_ref):
    x = x_ref[...]
    ms = jnp.mean(x * x, axis=-1, keepdims=True)
    h = (x * lax.rsqrt(ms + RMS_EPS)) * gain_ref[...]
    z = _dot(h.astype(BF16), w_ref[...])
    c = c_ref[...]
    s = s_ref[...]
    lane = lax.broadcasted_iota(jnp.int32, c.shape, 1)
    lo = lane < HEAD_DIM
    scale = HEAD_DIM ** -0.5
    for j in range(NSA_HEADS // 2):
        r = _headnorm_rope(z[:, C_Q + LANES * j:C_Q + LANES * (j + 1)], qg_ref[...], c, s) * scale
        rr = pltpu.roll(r, HEAD_DIM, 1)
        if j // (NSA_GROUP // 2) == 0:
            even, odd = jnp.where(lo, r, 0.0), jnp.where(lo, rr, 0.0)
        else:
            even, odd = jnp.where(lo, 0.0, rr), jnp.where(lo, 0.0, r)
        q_ref[:, 2 * LANES * j:2 * LANES * j + LANES] = even.astype(BF16)
        q_ref[:, 2 * LANES * j + LANES:2 * LANES * (j + 1)] = odd.astype(BF16)
    ks_ref[...] = _headnorm_rope(z[:, C_KS:C_KS + LANES], kg_ref[...], c, s).astype(BF16)
    kw_ref[...] = _headnorm_rope(z[:, C_KW:C_KW + LANES], kg_ref[...], c, s).astype(BF16)
    kc_ref[...] = z[:, C_KC:C_KC + LANES].astype(BF16)
    vc_ref[...] = z[:, C_VC:C_VC + LANES].astype(BF16)
    vs_ref[...] = z[:, C_VS:C_VS + LANES].astype(BF16)
    vw_ref[...] = z[:, C_VW:C_VW + LANES].astype(BF16)
    u_ref[...] = z[:, C_U:C_U + S5_WIDTH]
    g_ref[...] = _sigmoid(z[:, C_G:C_G + LANES])


def _in_proj(x2, gain, w1, ctab, stab, qg, kg, batch, seq):
    n = x2.shape[0]
    tm = IN_TM
    nt = seq // tm
    row = lambda b, i: (b * nt + i, 0)
    const = lambda b, i: (0, 0)
    tok = lambda w: pl.BlockSpec((tm, w), row)
    outs = [(n, 2 * NSA_WIDTH, BF16)] + [(n, LANES, BF16)] * 6
    return pl.pallas_call(
        _in_proj_kernel,
        grid=(batch, nt),
        in_specs=[tok(D_MODEL), pl.BlockSpec((1, D_MODEL), const), pl.BlockSpec((D_MODEL, IN_W), const),
                  tok(LANES), tok(LANES), pl.BlockSpec((1, LANES), const), pl.BlockSpec((1, LANES), const)],
        out_specs=[tok(2 * NSA_WIDTH)] + [tok(LANES)] * 6
        + [pl.BlockSpec((tm, S5_WIDTH), lambda b, i: (i, b)), tok(LANES)],
        out_shape=[jax.ShapeDtypeStruct(s[:2], s[2]) for s in outs]
        + [jax.ShapeDtypeStruct((seq, batch * S5_WIDTH), F32), jax.ShapeDtypeStruct((n, LANES), F32)],
        compiler_params=_cparams("parallel", "parallel"),
        name="in_proj",
    )(x2, gain, w1, ctab, stab, qg, kg)


def _compress_mlp(h_ref, w_ref, p_ref, w2_ref):
    w = w_ref[...]
    pre = _dot(h_ref[...], w)
    phi, plo = _split_bf16(p_ref[...])
    pb = _dot(phi, w) + _dot(plo, w)
    rows = pre.shape[0]
    out = None
    for h in range(NSA_KV_HEADS):
        top = pre[:, LANES * h:LANES * (h + 1)]
        bot = pre[:, LANES * (2 + h):LANES * (3 + h)]
        bias = pb[0:1, LANES * h:LANES * (h + 1)] + pb[1:2, LANES * (2 + h):LANES * (3 + h)]
        a = _gelu(top + pltpu.roll(bot, rows - 1, 0) + bias)
        o = _dot(a.astype(BF16), w2_ref[h])
        out = o if out is None else out + o
    return out


def _compress_kernel(hk_ref, hv_ref, wk_ref, wv_ref, pk_ref, pv_ref, w2k_ref, w2v_ref,
                     kg_ref, c_ref, s_ref, kc_ref, vc_ref):
    k = _compress_mlp(hk_ref, wk_ref, pk_ref, w2k_ref)
    kc_ref[...] = _headnorm_rope(k, kg_ref[...], c_ref[...], s_ref[...]).astype(BF16)
    vc_ref[...] = _compress_mlp(hv_ref, wv_ref, pv_ref, w2v_ref).astype(BF16)


def _compress(hk, hv, wk, wv, pk, pv, w2k, w2v, kg, ccmp, scmp, batch):
    rows = hk.shape[0] // batch
    width = hk.shape[1]
    const2 = lambda b: (0, 0)
    const3 = lambda b: (0, 0, 0)
    rowb = lambda w: pl.BlockSpec((rows, w), lambda b: (b, 0))
    return pl.pallas_call(
        _compress_kernel,
        grid=(batch,),
        in_specs=[rowb(width), rowb(width),
                  pl.BlockSpec(wk.shape, const2), pl.BlockSpec(wv.shape, const2),
                  pl.BlockSpec(pk.shape, const2), pl.BlockSpec(pv.shape, const2),
                  pl.BlockSpec(w2k.shape, const3), pl.BlockSpec(w2v.shape, const3),
                  pl.BlockSpec((1, LANES), const2), rowb(LANES), rowb(LANES)],
        out_specs=[rowb(LANES), rowb(LANES)],
        out_shape=[jax.ShapeDtypeStruct((hk.shape[0], LANES), BF16)] * 2,
        compiler_params=_cparams("parallel"),
        name="compress_kv",
    )(hk, hv, wk, wv, pk, pv, w2k, w2v, kg, ccmp, scmp)


def _compress_weights(pos_emb, w1, w2):
    half = CMP_BLOCK // 2
    w1r = w1.reshape(CMP_BLOCK, HEAD_DIM, CMP_HIDDEN)
    wexp = jnp.zeros((half, NSA_KV_HEADS, HEAD_DIM, 2 * NSA_KV_HEADS, CMP_HIDDEN), F32)
    for h in range(NSA_KV_HEADS):
        wexp = wexp.at[:, h, :, h].set(w1r[:half])
        wexp = wexp.at[:, h, :, NSA_KV_HEADS + h].set(w1r[half:])
    wexp = wexp.reshape(half * KV_WIDTH, 2 * NSA_KV_HEADS * CMP_HIDDEN).astype(BF16)
    pexp = jnp.broadcast_to(pos_emb.reshape(2, half, 1, HEAD_DIM), (2, half, NSA_KV_HEADS, HEAD_DIM))
    pexp = jnp.pad(pexp.reshape(2, half * KV_WIDTH), ((0, SUBLANES - 2), (0, 0)))
    w2e = jnp.zeros((NSA_KV_HEADS, CMP_HIDDEN, KV_WIDTH), F32)
    for h in range(NSA_KV_HEADS):
        w2e = w2e.at[h, :, HEAD_DIM * h:HEAD_DIM * (h + 1)].set(w2)
    return wexp, pexp, w2e.astype(BF16)


def _attn_kernel(q_ref, ks_ref, vs_ref, kw_ref, vw_ref, kc_ref, vc_ref, g_ref, e_ref, ov_ref,
                 o_ref, ms_ref, ls_ref, as_ref, mw_ref, lw_ref, aw_ref, *, n_cmp, n_slc, top_k):
    tq, tk, nh = ATT_TQ, ATT_TK, NSA_HEADS
    q0 = pl.program_id(1) * tq
    q8 = jnp.concatenate([q_ref[:, LANES * a:LANES * (a + 1)] for a in range(nh)], axis=0)
    t_row = q0 + lax.broadcasted_iota(jnp.int32, (tq, 1), 0)
    lane = lax.broadcasted_iota(jnp.int32, (tq, LANES), 1)

    s_c = _dot_t(q8, kc_ref[...]).reshape(nh, tq, LANES)
    mask_c = ((lane * CMP_STRIDE + (CMP_BLOCK - 1)) <= t_row) & (lane < n_cmp)
    sm = jnp.where(mask_c[None], s_c, NEG_BIG)
    p = jnp.where(mask_c[None], jnp.exp(sm - jnp.max(sm, axis=-1, keepdims=True)), 0.0)
    l = jnp.sum(p, axis=-1, keepdims=True)
    p_c = p * jnp.where(l > 0.0, 1.0 / l, 0.0)
    o_c = _dot(p_c.reshape(nh * tq, LANES).astype(BF16), vc_ref[...])

    cur = t_row >> int(math.log2(SLC_BLOCK))
    forced = (lane == 0) | (lane == cur) | (lane == cur - 1)
    selm = []
    for h in range(NSA_KV_HEADS):
        ps = p_c[NSA_GROUP * h]
        for g in range(1, NSA_GROUP):
            ps = ps + p_c[NSA_GROUP * h + g]
        hi, lo = _split_bf16(ps)
        imp = _dot(hi, ov_ref[...]) + _dot(lo, ov_ref[...])
        imp = jnp.where(forced, jnp.inf, jnp.where(lane > cur, -jnp.inf, imp))
        rank = jnp.zeros((tq, LANES), F32)
        for j in range(n_slc):
            col = imp[:, j:j + 1]
            rank = rank + jnp.where((col > imp) | ((col == imp) & (lane > j)), 1.0, 0.0)
        selm.append(jnp.where(rank < float(top_k), 1.0, 0.0).astype(BF16))

    def flash(k_ref, v_ref, m_ref, l_ref, a_ref, c_lo, c_hi, masked):
        m_ref[...] = jnp.full(m_ref.shape, NEG_BIG, F32)
        l_ref[...] = jnp.zeros(l_ref.shape, F32)
        a_ref[...] = jnp.zeros(a_ref.shape, F32)

        def body(c, carry):
            k0 = pl.multiple_of(c * tk, tk)
            s = _dot_t(q8, k_ref[pl.ds(k0, tk), :])
            kpos = k0 + lax.broadcasted_iota(jnp.int32, (tq, tk), 1)
            sm = masked(c, s, kpos)
            m_old = m_ref[...]
            m_new = jnp.maximum(m_old, jnp.max(sm, axis=-1, keepdims=True))
            pp = jnp.exp(sm - m_new)
            alpha = jnp.exp(m_old - m_new)
            l_ref[...] = alpha * l_ref[...] + jnp.sum(pp, axis=-1, keepdims=True)
            a_ref[...] = alpha * a_ref[...] + _dot(pp.astype(BF16), v_ref[pl.ds(k0, tk), :])
            m_ref[...] = m_new
            return carry

        lax.fori_loop(c_lo, c_hi, body, 0)

    def slc_mask(c, s, kpos):
        causal = kpos <= t_row
        e = e_ref[c]
        parts = []
        for h in range(NSA_KV_HEADS):
            keep = (_dot(selm[h], e) > 0.5) & causal
            parts.append(jnp.where(keep[None], s[NSA_GROUP * h * tq:NSA_GROUP * (h + 1) * tq]
                                   .reshape(NSA_GROUP, tq, tk), 2.0 * NEG_BIG))
        return jnp.concatenate(parts, axis=0).reshape(nh * tq, tk)

    def win_mask(c, s, kpos):
        keep = (kpos <= t_row) & ((t_row - kpos) < WINDOW)
        return jnp.where(keep[None], s.reshape(nh, tq, tk), 2.0 * NEG_BIG).reshape(nh * tq, tk)

    shift = int(math.log2(tk))
    c_hi = ((q0 + tq - 1) >> shift) + 1
    flash(ks_ref, vs_ref, ms_ref, ls_ref, as_ref, 0, c_hi, slc_mask)
    flash(kw_ref, vw_ref, mw_ref, lw_ref, aw_ref, jnp.maximum(q0 - (WINDOW - 1), 0) >> shift, c_hi, win_mask)
    o_s = as_ref[...] * (1.0 / ls_ref[...])
    o_w = aw_ref[...] * (1.0 / lw_ref[...])

    lo_half = lane < HEAD_DIM
    heads = []
    for a in range(nh):
        rows = slice(a * tq, (a + 1) * tq)
        heads.append(g_ref[:, 3 * a:3 * a + 1] * o_c[rows] + g_ref[:, 3 * a + 1:3 * a + 2] * o_s[rows]
                     + g_ref[:, 3 * a + 2:3 * a + 3] * o_w[rows])
    for j in range(nh // 2):
        even, odd = heads[2 * j], heads[2 * j + 1]
        if j // (NSA_GROUP // 2) == 0:
            pair = jnp.where(lo_half, even, pltpu.roll(odd, HEAD_DIM, 1))
        else:
            pair = jnp.where(lo_half, pltpu.roll(even, HEAD_DIM, 1), odd)
        o_ref[:, LANES * j:LANES * (j + 1)] = pair.astype(BF16)


def _pack_w_in(w):
    sizes = (NSA_WIDTH,) + (KV_WIDTH,) * 6 + (3 * NSA_HEADS, S5_WIDTH, D_MODEL, D_MODEL)
    o = np.cumsum((0,) + sizes)
    q, kc, vc, ks, vs, kw, vw, g, u, ga, gs = [w[:, o[i]:o[i + 1]] for i in range(len(sizes))]
    g = jnp.pad(g, ((0, 0), (0, LANES - 3 * NSA_HEADS)))
    w1 = jnp.concatenate([q, ks, kw, kc, vc, vs, vw, u, g], axis=1).astype(BF16)
    return w1, ga.astype(BF16), gs.astype(BF16)


def _nsa_branch(x2, gain, w1, ctab, stab, ccmp, scmp, q_gain, k_gain, cmp_k, cmp_v, batch, seq):
    qg = jnp.tile(q_gain, 2)[None, :]
    kg = jnp.tile(k_gain, 2)[None, :]
    q, ks, kw, kc_raw, vc_raw, vs, vw, u_tm, gates = _in_proj(x2, gain, w1, ctab, stab, qg, kg, batch, seq)
    half_rows = batch * seq // (CMP_BLOCK // 2)
    hk = kc_raw.reshape(half_rows, (CMP_BLOCK // 2) * KV_WIDTH)
    hv = vc_raw.reshape(half_rows, (CMP_BLOCK // 2) * KV_WIDTH)
    wk, pk, w2k = _compress_weights(*cmp_k)
    wv, pv, w2v = _compress_weights(*cmp_v)
    kc, vc = _compress(hk, hv, wk, wv, pk, pv, w2k, w2v, kg, ccmp, scmp, batch)
    o_a = _attention(q, ks, vs, kw, vw, kc, vc, gates, batch, seq)
    return o_a, u_tm


def _cmp_rope_tables(ctab, stab, batch, seq):
    def pick(t):
        t = t.reshape(batch, seq, LANES)[:, CMP_BLOCK - 1::CMP_STRIDE]
        t = jnp.pad(t, ((0, 0), (0, seq // CMP_STRIDE - t.shape[1]), (0, 0)))
        return t.reshape(batch * (seq // CMP_STRIDE), LANES)
    return pick(ctab), pick(stab)


def _s5_disc_kernel(are_ref, aim_ref, ldt_ref, lre_ref, lim_ref, fre_ref, fim_ref):
    lam_re = jnp.minimum(are_ref[...], -1e-4)
    lam_im = aim_ref[...]
    dt = jnp.exp(ldt_ref[...])
    mag = jnp.exp(lam_re * dt)
    ang = lam_im * dt
    lb_re = mag * jnp.cos(ang)
    lb_im = mag * jnp.sin(ang)
    den = lam_re * lam_re + lam_im * lam_im
    lre_ref[...] = lb_re
    lim_ref[...] = lb_im
    fre_ref[...] = ((lb_re - 1.0) * lam_re + lb_im * lam_im) / den
    fim_ref[...] = (lb_im * lam_re - (lb_re - 1.0) * lam_im) / den


def _s5_weights(a_re, a_im, log_dt, b_re, b_im, c_re, c_im):
    shp = jax.ShapeDtypeStruct((S5_GROUPS, S5_STATE), F32)
    lb_re, lb_im, f_re, f_im = pl.pallas_call(
        _s5_disc_kernel, out_shape=[shp] * 4, name="s5_discretise",
    )(a_re, a_im, log_dt[:, None])
    bb_re = f_re[..., None] * b_re - f_im[..., None] * b_im
    bb_im = f_re[..., None] * b_im + f_im[..., None] * b_re
    eye = jnp.eye(S5_GROUPS, dtype=F32)

    def in_map(bb):
        return jnp.einsum("gpc,gh->gchp", bb, eye).reshape(S5_WIDTH, S5_NSTATE)

    def out_map(cc):
        return jnp.einsum("gcp,gh->hpgc", cc, eye).reshape(S5_NSTATE, S5_WIDTH)

    w_b = jnp.concatenate([in_map(bb_re), in_map(bb_im)], axis=1).astype(BF16)
    w_c = jnp.concatenate([out_map(c_re), -out_map(c_im)], axis=0).astype(BF16)
    return lb_re.reshape(1, S5_NSTATE), lb_im.reshape(1, S5_NSTATE), w_b, w_c


def _s5_kernel(u_ref, wb_ref, lre_ref, lim_ref, wc_ref, d_ref, wg_ref, o_ref, x_ref, st_ref, *, batch):
    ns = S5_NSTATE

    @pl.when(pl.program_id(0) == 0)
    def _():
        st_ref[...] = jnp.zeros(st_ref.shape, F32)

    u = u_ref[...]
    x_ref[...] = _dot(u.astype(BF16), wb_ref[...])
    lre = jnp.broadcast_to(lre_ref[...], (batch, ns))
    lim = jnp.broadcast_to(lim_ref[...], (batch, ns))

    def step(t, carry):
        xr, xi = carry
        r0 = pl.multiple_of(t * batch, batch)
        nr = lre * xr - lim * xi + x_ref[pl.ds(r0, batch), 0:ns]
        ni = lre * xi + lim * xr + x_ref[pl.ds(r0, batch), ns:2 * ns]
        x_ref[pl.ds(r0, batch), 0:ns] = nr
        x_ref[pl.ds(r0, batch), ns:2 * ns] = ni
        return nr, ni

    steps = u.shape[0] // batch
    xr, xi = lax.fori_loop(0, steps, step, (st_ref[:, 0:ns], st_ref[:, ns:2 * ns]), unroll=4)
    st_ref[:, 0:ns] = xr
    st_ref[:, ns:2 * ns] = xi
    y = _dot(x_ref[...].astype(BF16), wc_ref[...]) + d_ref[...] * u
    z = _dot(_gelu(y).astype(BF16), wg_ref[...])
    o_ref[...] = (z[:, :S5_WIDTH] * _sigmoid(z[:, S5_WIDTH:])).astype(BF16)


def _s5(u_tm, lre, lim, w_b, w_c, d_skip, w_glu, batch, seq):
    rows = S5_T * batch
    const = lambda t: (0, 0)
    return pl.pallas_call(
        functools.partial(_s5_kernel, batch=batch),
        grid=(seq // S5_T,),
        in_specs=[pl.BlockSpec((rows, S5_WIDTH), lambda t: (t, 0)),
                  pl.BlockSpec(w_b.shape, const), pl.BlockSpec(lre.shape, const), pl.BlockSpec(lim.shape, const),
                  pl.BlockSpec(w_c.shape, const), pl.BlockSpec((1, S5_WIDTH), const),
                  pl.BlockSpec(w_glu.shape, const)],
        out_specs=pl.BlockSpec((rows, S5_WIDTH), lambda t: (t, 0)),
        out_shape=jax.ShapeDtypeStruct((seq * batch, S5_WIDTH), BF16),
        scratch_shapes=[pltpu.VMEM((rows, 2 * S5_NSTATE), F32), pltpu.VMEM((batch, 2 * S5_NSTATE), F32)],
        compiler_params=_cparams("arbitrary"),
        name="s5_scan",
    )(u_tm, w_b, lre, lim, w_c, d_skip, w_glu)


RT_GROUP_ROW = 0
RT_EXPERT_ROW = SUBLANES


def _rmsnorm(x, gain):
    return (x * lax.rsqrt(jnp.mean(x * x, axis=-1, keepdims=True) + RMS_EPS)) * gain


def _merge_kernel(x_ref, g1_ref, wga_ref, wgs_ref, oa_ref, ob_ref, pa_ref, ps_ref, wo_ref, g2_ref,
                  wrh_ref, wrl_ref, rb_ref, x1_ref, h2_ref, lg_ref):
    x = x_ref[...]
    h = _rmsnorm(x, g1_ref[...]).astype(BF16)
    gate_a = _sigmoid(_dot(h, wga_ref[...]))
    gate_s = _sigmoid(_dot(h, wgs_ref[...]))
    merged = gate_a * _dot(oa_ref[...], pa_ref[...]) + gate_s * _dot(ob_ref[...], ps_ref[...])
    x1 = x + _dot(merged.astype(BF16), wo_ref[...])
    x1_ref[...] = x1
    h2 = _rmsnorm(x1, g2_ref[...])
    h2_ref[...] = h2
    hi, lo = _split_bf16(h2)
    lg_ref[...] = (_dot_t(wrh_ref[...], hi) + _dot_t(wrh_ref[...], lo) + _dot_t(wrl_ref[...], hi)
                   + rb_ref[...])


def _router_weights(w_rg, b_rg, w_re, b_re):
    wt = jnp.zeros((LANES, D_MODEL), F32)
    wt = wt.at[RT_GROUP_ROW:RT_GROUP_ROW + N_EXPERT_GROUPS].set(w_rg.T)
    wt = wt.at[RT_EXPERT_ROW:RT_EXPERT_ROW + N_EXPERTS].set(w_re.T)
    bias = jnp.zeros((LANES, 1), F32)
    bias = bias.at[RT_GROUP_ROW:RT_GROUP_ROW + N_EXPERT_GROUPS, 0].set(b_rg)
    bias = bias.at[RT_EXPERT_ROW:RT_EXPERT_ROW + N_EXPERTS, 0].set(b_re)
    hi, lo = _split_bf16(wt)
    return hi, lo, bias


def _merge(x2, g1, w_ga, w_gs, o_a, ob_tm, p_a, p_s, w_o, g2, wrh, wrl, rb, batch, seq):
    n = x2.shape[0]
    tm = MRG_TM
    nt = seq // tm
    row = lambda b, i: (b * nt + i, 0)
    const = lambda b, i: (0, 0)
    tok = lambda w: pl.BlockSpec((tm, w), row)
    full = lambda a: pl.BlockSpec(a.shape, const)
    ob_v = ob_tm.reshape(seq, batch * S5_WIDTH)
    return pl.pallas_call(
        _merge_kernel,
        grid=(batch, nt),
        in_specs=[tok(D_MODEL), full(g1), full(w_ga), full(w_gs), tok(NSA_WIDTH),
                  pl.BlockSpec((tm, S5_WIDTH), lambda b, i: (i, b)), full(p_a), full(p_s), full(w_o),
                  full(g2), full(wrh), full(wrl), full(rb)],
        out_specs=[tok(D_MODEL), tok(D_MODEL), pl.BlockSpec((LANES, tm), lambda b, i: (0, b * nt + i))],
        out_shape=[jax.ShapeDtypeStruct((n, D_MODEL), F32), jax.ShapeDtypeStruct((n, D_MODEL), F32),
                   jax.ShapeDtypeStruct((LANES, n), F32)],
        compiler_params=_cparams("parallel", "parallel"),
        name="merge_out_proj",
    )(x2, g1, w_ga, w_gs, o_a, ob_v, p_a, p_s, w_o, g2, wrh, wrl, rb)


def _first_argmax(v, rows):
    vmax = jnp.max(v, axis=0, keepdims=True)
    idx = jnp.min(jnp.where(v == vmax, rows, SUBLANES), axis=0, keepdims=True)
    return vmax, idx


def _route_kernel(lg_ref, tri_ref, e_ref, w_ref, r_ref, cnt_ref, carry_ref):
    tt = lg_ref.shape[1]

    @pl.when(pl.program_id(0) == 0)
    def _():
        carry_ref[...] = jnp.zeros(carry_ref.shape, F32)

    rows = lax.broadcasted_iota(jnp.int32, (SUBLANES, tt), 0)
    gl = jnp.where(rows < N_EXPERT_GROUPS, lg_ref[RT_GROUP_ROW:RT_GROUP_ROW + SUBLANES, :], -jnp.inf)
    gmax, g_sel = _first_argmax(gl, rows)
    g_w = 1.0 / jnp.sum(jnp.exp(gl - gmax), axis=0, keepdims=True)

    el = jnp.zeros((EXPERTS_PER_GROUP, tt), F32)
    for g in range(N_EXPERT_GROUPS):
        r0 = RT_EXPERT_ROW + EXPERTS_PER_GROUP * g
        el = jnp.where(g_sel == g, lg_ref[r0:r0 + EXPERTS_PER_GROUP, :], el)
    ex = jnp.exp(el - jnp.max(el, axis=0, keepdims=True))
    p = ex / jnp.sum(ex, axis=0, keepdims=True)
    p1, i1 = _first_argmax(p, rows)
    p2, i2 = _first_argmax(jnp.where(rows == i1, -1.0, p), rows)
    e1 = g_sel * EXPERTS_PER_GROUP + i1
    e2 = g_sel * EXPERTS_PER_GROUP + i2
    scale = g_w / (p1 + p2)

    erow = lax.broadcasted_iota(jnp.int32, (N_EXPERTS, tt), 0)
    oh1 = jnp.where(erow == e1, 1.0, 0.0)
    oh2 = jnp.where(erow == e2, 1.0, 0.0)
    both = oh1 + oh2
    before = carry_ref[:, 0:1] + _dot(both.astype(BF16), tri_ref[...])
    carry_ref[...] = carry_ref[...] + jnp.sum(both, axis=1, keepdims=True)
    cnt_ref[...] = carry_ref[...]

    zi = jnp.zeros((SUBLANES - 2, tt), jnp.int32)
    e_ref[...] = jnp.concatenate([e1, e2, zi], axis=0)
    w_ref[...] = jnp.concatenate([p1 * scale, p2 * scale, zi.astype(F32)], axis=0)
    r_ref[...] = jnp.concatenate([jnp.sum(oh1 * before, axis=0, keepdims=True),
                                  jnp.sum(oh2 * before, axis=0, keepdims=True),
                                  zi.astype(F32)], axis=0).astype(jnp.int32)


def _route(logits_t):
    n = logits_t.shape[1]
    tt = RT_TT
    tri = jnp.asarray(np.triu(np.ones((tt, tt), np.float32), 1), BF16)
    blk = pl.BlockSpec((SUBLANES, tt), lambda i: (0, i))
    return pl.pallas_call(
        _route_kernel,
        grid=(n // tt,),
        in_specs=[pl.BlockSpec((LANES, tt), lambda i: (0, i)), pl.BlockSpec((tt, tt), lambda i: (0, 0))],
        out_specs=[blk, blk, blk, pl.BlockSpec((N_EXPERTS, LANES), lambda i: (0, 0))],
        out_shape=[jax.ShapeDtypeStruct((SUBLANES, n), jnp.int32), jax.ShapeDtypeStruct((SUBLANES, n), F32),
                   jax.ShapeDtypeStruct((SUBLANES, n), jnp.int32), jax.ShapeDtypeStruct((N_EXPERTS, LANES), F32)],
        scratch_shapes=[pltpu.VMEM((N_EXPERTS, LANES), F32)],
        compiler_params=_cparams("arbitrary"),
        name="moe_route",
    )(logits_t, tri)


def _dispatch_kernel(dest_ref, h_ref, zero_ref, buf_ref, sem):
    del zero_ref
    tt = dest_ref.shape[1]
    t0 = pl.program_id(0) * tt

    def row_copy(t, k):
        return pltpu.make_async_copy(h_ref.at[pl.ds(t0 + t, 1)], buf_ref.at[pl.ds(dest_ref[k, t], 1)], sem)

    def issue(t, c):
        for k in range(EXPERT_TOP_K):
            row_copy(t, k).start()
        return c

    def drain(t, c):
        for k in range(EXPERT_TOP_K):
            row_copy(t, k).wait()
        return c

    lax.fori_loop(0, tt, issue, 0, unroll=8)
    lax.fori_loop(0, tt, drain, 0, unroll=8)


def _dispatch(dest, h2, cap):
    n = h2.shape[0]
    tt = DSP_TT
    return pl.pallas_call(
        _dispatch_kernel,
        grid=(n // tt,),
        in_specs=[pl.BlockSpec((EXPERT_TOP_K, tt), lambda i: (0, i), memory_space=pltpu.SMEM),
                  pl.BlockSpec(memory_space=pl.ANY), pl.BlockSpec(memory_space=pl.ANY)],
        out_specs=pl.BlockSpec(memory_space=pl.ANY),
        out_shape=jax.ShapeDtypeStruct((cap, D_MODEL), h2.dtype),
        scratch_shapes=[pltpu.SemaphoreType.DMA],
        input_output_aliases={2: 0},
        compiler_params=_cparams("arbitrary"),
        name="moe_dispatch",
    )(dest, h2, jnp.zeros((cap, D_MODEL), h2.dtype))


def _expert_kernel(be_ref, nb_ref, x_ref, wg_ref, wu_ref, wd_ref, o_ref, wgb_ref, wub_ref, wdb_ref):
    i = pl.program_id(0)

    @pl.when(i < nb_ref[0])
    def _():
        @pl.when((i == 0) | (be_ref[i] != be_ref[jnp.maximum(i - 1, 0)]))
        def _():
            wgb_ref[...] = wg_ref[...].astype(BF16)
            wub_ref[...] = wu_ref[...].astype(BF16)
            wdb_ref[...] = wd_ref[...].astype(BF16)

        x = x_ref[...].astype(BF16)
        g = _dot(x, wgb_ref[...])
        u = _dot(x, wub_ref[...])
        o_ref[...] = _dot(((g * _sigmoid(g)) * u).astype(BF16), wdb_ref[...])

    @pl.when(i >= nb_ref[0])
    def _():
        o_ref[...] = jnp.zeros(o_ref.shape, o_ref.dtype)


def _experts(blk_expert, n_used, buf, w_gate, w_up, w_down, layer):
    cap = buf.shape[0]
    blk = MOE_BLK
    row = lambda i, be, nb: (jnp.minimum(i, nb[0] - 1), 0)
    wmap = lambda i, be, nb: (layer, be[jnp.minimum(i, nb[0] - 1)], 0, 0)
    return pl.pallas_call(
        _expert_kernel,
        grid_spec=pltpu.PrefetchScalarGridSpec(
            num_scalar_prefetch=2,
            grid=(cap // blk,),
            in_specs=[pl.BlockSpec((blk, D_MODEL), row),
                      pl.BlockSpec((None, None, D_MODEL, EXPERT_FF), wmap),
                      pl.BlockSpec((None, None, D_MODEL, EXPERT_FF), wmap),
                      pl.BlockSpec((None, None, EXPERT_FF, D_MODEL), wmap)],
            out_specs=pl.BlockSpec((blk, D_MODEL), lambda i, be, nb: (i, 0)),
            scratch_shapes=[pltpu.VMEM((D_MODEL, EXPERT_FF), BF16), pltpu.VMEM((D_MODEL, EXPERT_FF), BF16),
                            pltpu.VMEM((EXPERT_FF, D_MODEL), BF16)],
        ),
        out_shape=jax.ShapeDtypeStruct((cap, D_MODEL), F32),
        compiler_params=_cparams("arbitrary"),
        name="moe_experts",
    )(blk_expert, n_used, buf, w_gate, w_up, w_down)


def _combine_kernel(dest_ref, x1_ref, w_ref, ob_ref, o_ref, rows_ref, sem):
    tt = dest_ref.shape[1]

    def row_copy(t, k):
        return pltpu.make_async_copy(ob_ref.at[pl.ds(dest_ref[k, t], 1)], rows_ref.at[k, pl.ds(t, 1)], sem)

    def issue(t, c):
        for k in range(EXPERT_TOP_K):
            row_copy(t, k).start()
        return c

    def drain(t, c):
        for k in range(EXPERT_TOP_K):
            row_copy(t, k).wait()
        return c

    lax.fori_loop(0, tt, issue, 0, unroll=8)
    lax.fori_loop(0, tt, drain, 0, unroll=8)
    o_ref[...] = x1_ref[...] + w_ref[:, 0:1] * rows_ref[0] + w_ref[:, 1:2] * rows_ref[1]


def _combine(dest, x1, w_col, out_buf):
    n = x1.shape[0]
    tt = DSP_TT
    return pl.pallas_call(
        _combine_kernel,
        grid=(n // tt,),
        in_specs=[pl.BlockSpec((EXPERT_TOP_K, tt), lambda i: (0, i), memory_space=pltpu.SMEM),
                  pl.BlockSpec((tt, D_MODEL), lambda i: (i, 0)),
                  pl.BlockSpec((tt, EXPERT_TOP_K), lambda i: (i, 0)),
                  pl.BlockSpec(memory_space=pl.ANY)],
        out_specs=pl.BlockSpec((tt, D_MODEL), lambda i: (i, 0)),
        out_shape=jax.ShapeDtypeStruct((n, D_MODEL), F32),
        scratch_shapes=[pltpu.VMEM((EXPERT_TOP_K, tt, D_MODEL), out_buf.dtype), pltpu.SemaphoreType.DMA],
        compiler_params=_cparams("arbitrary"),
        name="moe_combine",
    )(dest, x1, w_col, out_buf)


def _moe(x1, h2, logits_t, w_gate, w_up, w_down, layer):
    n = x1.shape[0]
    e_idx, wts, rank, counts = _route(logits_t)
    counts = counts[:, 0].astype(jnp.int32)
    padded = (counts + MOE_BLK - 1) // MOE_BLK * MOE_BLK
    pends = jnp.cumsum(padded)
    pstarts = pends - padded
    dest = pstarts[e_idx[:EXPERT_TOP_K]] + rank[:EXPERT_TOP_K]
    cap = n * EXPERT_TOP_K + N_EXPERTS * MOE_BLK
    n_blk = cap // MOE_BLK
    blk_expert = jnp.minimum(jnp.searchsorted(pends, jnp.arange(n_blk, dtype=jnp.int32) * MOE_BLK, side="right"),
                             N_EXPERTS - 1).astype(jnp.int32)
    n_used = (pends[-1:] // MOE_BLK).astype(jnp.int32)
    buf = _dispatch(dest, h2, cap)
    out_buf = _experts(blk_expert, n_used, buf, w_gate, w_up, w_down, layer)
    return _combine(dest, x1, wts[:EXPERT_TOP_K].T, out_buf)


def kernel(x, positions, norm_mix, norm_ffn, w_in, q_gain, k_gain, cmp_k_pos, cmp_k_w1, cmp_k_w2, cmp_v_pos,
           cmp_v_w1, cmp_v_w2, s5_a_re, s5_a_im, s5_log_dt, s5_b_re, s5_b_im, s5_c_re, s5_c_im, s5_d, s5_glu,
           proj_nsa, proj_s5, w_out, router_group, router_group_bias, router_expert, router_expert_bias,
           expert_gate, expert_up, expert_down):
    batch, seq, _ = x.shape
    depth = w_in.shape[0]
    n = batch * seq
    inv_freq = ROPE_THETA ** (-jnp.arange(0, ROPE_DIMS, 2, dtype=jnp.float32) / ROPE_DIMS)
    ctab, stab = _rope_tables(positions.reshape(n, 1), inv_freq)
    ccmp, scmp = _cmp_rope_tables(ctab, stab, batch, seq)
    x2 = x.reshape(n, D_MODEL)
    for l in range(depth):
        w1, w_ga, w_gs = _pack_w_in(w_in[l])
        o_a, u_tm = _nsa_branch(x2, norm_mix[l][None], w1, ctab, stab, ccmp, scmp, q_gain[l], k_gain[l],
                                (cmp_k_pos[l], cmp_k_w1[l], cmp_k_w2[l]),
                                (cmp_v_pos[l], cmp_v_w1[l], cmp_v_w2[l]), batch, seq)
        lre, lim, w_b, w_c = _s5_weights(s5_a_re[l], s5_a_im[l], s5_log_dt[l], s5_b_re[l], s5_b_im[l],
                                         s5_c_re[l], s5_c_im[l])
        ob_tm = _s5(u_tm.reshape(seq * batch, S5_WIDTH), lre, lim, w_b, w_c, s5_d[l][None],
                    s5_glu[l].astype(BF16), batch, seq)
        wrh, wrl, rb = _router_weights(router_group[l], router_group_bias[l], router_expert[l],
                                       router_expert_bias[l])
        x1, h2, logits_t = _merge(x2, norm_mix[l][None], w_ga, w_gs, o_a, ob_tm, proj_nsa[l].astype(BF16),
                                  proj_s5[l].astype(BF16), w_out[l].astype(BF16), norm_ffn[l][None],
                                  wrh, wrl, rb, batch, seq)
        x2 = _moe(x1, h2, logits_t, expert_gate, expert_up, expert_down, l)
    return x2.reshape(batch, seq, D_MODEL)


def _attn_tables(seq):
    n_cmp = (seq - CMP_BLOCK) // CMP_STRIDE + 1
    n_slc = seq // SLC_BLOCK
    key_blk = np.arange(seq) // SLC_BLOCK
    expand = (np.arange(LANES)[:, None] == key_blk[None, :]).astype(np.float32)
    expand = expand.reshape(LANES, seq // ATT_TK, ATT_TK).transpose(1, 0, 2)
    cmp_start = np.arange(LANES) * CMP_STRIDE
    slc_start = np.arange(LANES) * SLC_BLOCK
    overlap = ((cmp_start[:, None] < slc_start[None, :] + SLC_BLOCK)
               & (cmp_start[:, None] + CMP_BLOCK > slc_start[None, :])
               & (np.arange(LANES)[:, None] < n_cmp) & (np.arange(LANES)[None, :] < n_slc))
    return (jnp.asarray(expand, BF16), jnp.asarray(overlap.astype(np.float32), BF16), n_cmp, n_slc)


def _attention(q, ks, vs, kw, vw, kc, vc, gates, batch, seq):
    expand, overlap, n_cmp, n_slc = _attn_tables(seq)
    tq = ATT_TQ
    nq = seq // tq
    rows = NSA_HEADS * tq
    tile = lambda w: pl.BlockSpec((tq, w), lambda b, i: (b * nq + i, 0))
    per_b = lambda r: pl.BlockSpec((r, LANES), lambda b, i: (b, 0))
    stat = pltpu.VMEM((rows, 1), F32)
    acc = pltpu.VMEM((rows, LANES), F32)
    return pl.pallas_call(
        functools.partial(_attn_kernel, n_cmp=n_cmp, n_slc=n_slc, top_k=min(SLC_TOPK, n_slc)),
        grid=(batch, nq),
        in_specs=[tile(2 * NSA_WIDTH), per_b(seq), per_b(seq), per_b(seq), per_b(seq),
                  per_b(LANES), per_b(LANES), tile(LANES),
                  pl.BlockSpec(expand.shape, lambda b, i: (0, 0, 0)),
                  pl.BlockSpec(overlap.shape, lambda b, i: (0, 0))],
        out_specs=tile(NSA_WIDTH),
        out_shape=jax.ShapeDtypeStruct((batch * seq, NSA_WIDTH), BF16),
        scratch_shapes=[stat, stat, acc, stat, stat, acc],
        compiler_params=_cparams("parallel", "parallel"),
        name="nsa_attention",
    )(q, ks, vs, kw, vw, kc, vc, gates, expand, overlap)
```

```python
import functools
import math

import jax
import jax.numpy as jnp
import numpy as np
from jax import lax
from jax.experimental import pallas as pl
from jax.experimental.pallas import tpu as pltpu

F32 = jnp.float32
BF16 = jnp.bfloat16

D_MODEL = 1024
NSA_HEADS = 8
NSA_KV_HEADS = 2
NSA_GROUP = NSA_HEADS // NSA_KV_HEADS
HEAD_DIM = 64
ROPE_DIMS = HEAD_DIM // 4
ROPE_THETA = 500000.0
CMP_BLOCK = 32
CMP_STRIDE = 16
CMP_HIDDEN = 2 * HEAD_DIM
SLC_BLOCK = 64
SLC_TOPK = 16
WINDOW = 512
NSA_WIDTH = NSA_HEADS * HEAD_DIM
KV_WIDTH = NSA_KV_HEADS * HEAD_DIM
S5_WIDTH = 256
S5_GROUP_CH = 16
S5_GROUPS = S5_WIDTH // S5_GROUP_CH
S5_STATE = 64
S5_NSTATE = S5_GROUPS * S5_STATE
N_EXPERT_GROUPS = 4
EXPERTS_PER_GROUP = 8
N_EXPERTS = N_EXPERT_GROUPS * EXPERTS_PER_GROUP
EXPERT_TOP_K = 2
EXPERT_FF = 512
RMS_EPS = 1e-6
NEG_BIG = -1e30

LANES = 128
SUBLANES = 8
VMEM_LIMIT = 56 * 1024 * 1024

IN_TM = 512
ATT_TQ = 128
ATT_TK = 512
ATT_WK = WINDOW + ATT_TQ
ATT_RB = 64
S5_T = 128
MRG_TM = 256
RT_TT = 512
MOE_BLK = 256
DSP_TT = 256

C_Q, C_KS, C_KW, C_KC, C_VC, C_VS, C_VW, C_U, C_G = 0, 512, 640, 768, 896, 1024, 1152, 1280, 1536
IN_W = 1664


def _cparams(*sem):
    return pltpu.CompilerParams(dimension_semantics=sem, vmem_limit_bytes=VMEM_LIMIT)


def _gelu(x):
    return 0.5 * x * (1.0 + jnp.tanh(math.sqrt(2.0 / math.pi) * (x + 0.044715 * (x * x * x))))


def _sigmoid(x):
    return 1.0 / (1.0 + jnp.exp(-x))


def _dot(a, b):
    return jnp.dot(a, b, preferred_element_type=F32)


def _dot_t(a, b):
    return lax.dot_general(a, b, (((1,), (1,)), ((), ())), preferred_element_type=F32)


def _split_bf16(x):
    hi = x.astype(BF16)
    lo = (x - hi.astype(F32)).astype(BF16)
    return hi, lo


def _rope_kernel(pos_ref, freq_ref, sign_ref, c_ref, s_ref):
    ang = pos_ref[...].astype(F32) * freq_ref[...]
    c_ref[...] = jnp.cos(ang)
    s_ref[...] = jnp.sin(ang) * sign_ref[...]


def _rope_tables(pos_col, inv_freq):
    n = pos_col.shape[0]
    lane = np.arange(LANES) % HEAD_DIM
    sel = np.zeros((LANES, ROPE_DIMS // 2), np.float32)
    for l in range(LANES):
        if lane[l] < ROPE_DIMS:
            sel[l, lane[l] % (ROPE_DIMS // 2)] = 1.0
    freq = (jnp.asarray(sel) * inv_freq[None, :]).sum(-1)[None, :]
    sign = np.where(lane < ROPE_DIMS // 2, -1.0, 1.0).astype(np.float32)[None, :]
    tm = 1024
    return pl.pallas_call(
        _rope_kernel,
        grid=(n // tm,),
        in_specs=[pl.BlockSpec((tm, 1), lambda i: (i, 0)),
                  pl.BlockSpec((1, LANES), lambda i: (0, 0)),
                  pl.BlockSpec((1, LANES), lambda i: (0, 0))],
        out_specs=[pl.BlockSpec((tm, LANES), lambda i: (i, 0))] * 2,
        out_shape=[jax.ShapeDtypeStruct((n, LANES), F32)] * 2,
        compiler_params=_cparams("parallel"),
        name="rope_tables",
    )(pos_col, freq, jnp.asarray(sign))


def _headnorm_rope(t, gain, c, s):
    lane = lax.broadcasted_iota(jnp.int32, t.shape, 1)
    lo = lane < HEAD_DIM
    sq = t * t
    s_lo = jnp.sum(jnp.where(lo, sq, 0.0), axis=-1, keepdims=True)
    s_hi = jnp.sum(jnp.where(lo, 0.0, sq), axis=-1, keepdims=True)
    ms = jnp.where(lo, s_lo, s_hi) * (1.0 / HEAD_DIM)
    tn = (t * lax.rsqrt(ms + RMS_EPS)) * gain
    half = ROPE_DIMS // 2
    partner = jnp.where((lane & (HEAD_DIM - 1)) < half,
                        pltpu.roll(tn, LANES - half, 1), pltpu.roll(tn, half, 1))
    return tn * c + partner * s


def _in_proj_kernel(x_ref, gain_ref, w_ref, c_ref, s_ref, qg_ref, kg_ref,
                    q_ref, ks_ref, kw_ref, kc_ref, vc_ref, vs_ref, vw_ref, u_ref, g_ref):
    x = x_ref[...]
    ms = jnp.mean(x * x, axis=-1, keepdims=True)
    h = (x * lax.rsqrt(ms + RMS_EPS)) * gain_ref[...]
    z = _dot(h.astype(BF16), w_ref[...])
    c = c_ref[...]
    s = s_ref[...]
    lane = lax.broadcasted_iota(jnp.int32, c.shape, 1)
    lo = lane < HEAD_DIM
    scale = HEAD_DIM ** -0.5
    for j in range(NSA_HEADS // 2):
        r = _headnorm_rope(z[:, C_Q + LANES * j:C_Q + LANES * (j + 1)], qg_ref[...], c, s) * scale
        rr = pltpu.roll(r, HEAD_DIM, 1)
        if j // (NSA_GROUP // 2) == 0:
            even, odd = jnp.where(lo, r, 0.0), jnp.where(lo, rr, 0.0)
        else:
            even, odd = jnp.where(lo, 0.0, rr), jnp.where(lo, 0.0, r)
        q_ref[:, 2 * LANES * j:2 * LANES * j + LANES] = even.astype(BF16)
        q_ref[:, 2 * LANES * j + LANES:2 * LANES * (j + 1)] = odd.astype(BF16)
    ks_ref[...] = _headnorm_rope(z[:, C_KS:C_KS + LANES], kg_ref[...], c, s).astype(BF16)
    kw_ref[...] = _headnorm_rope(z[:, C_KW:C_KW + LANES], kg_ref[...], c, s).astype(BF16)
    kc_ref[...] = z[:, C_KC:C_KC + LANES].astype(BF16)
    vc_ref[...] = z[:, C_VC:C_VC + LANES].astype(BF16)
    vs_ref[...] = z[:, C_VS:C_VS + LANES].astype(BF16)
    vw_ref[...] = z[:, C_VW:C_VW + LANES].astype(BF16)
    u_ref[...] = z[:, C_U:C_U + S5_WIDTH]
    g_ref[...] = _sigmoid(z[:, C_G:C_G + LANES])


def _in_proj(x2, gain, w1, ctab, stab, qg, kg, batch, seq):
    n = x2.shape[0]
    tm = IN_TM
    nt = seq // tm
    row = lambda b, i: (b * nt + i, 0)
    const = lambda b, i: (0, 0)
    tok = lambda w: pl.BlockSpec((tm, w), row)
    outs = [(n, 2 * NSA_WIDTH, BF16)] + [(n, LANES, BF16)] * 6
    return pl.pallas_call(
        _in_proj_kernel,
        grid=(batch, nt),
        in_specs=[tok(D_MODEL), pl.BlockSpec((1, D_MODEL), const), pl.BlockSpec((D_MODEL, IN_W), const),
                  tok(LANES), tok(LANES), pl.BlockSpec((1, LANES), const), pl.BlockSpec((1, LANES), const)],
        out_specs=[tok(2 * NSA_WIDTH)] + [tok(LANES)] * 6
        + [pl.BlockSpec((tm, S5_WIDTH), lambda b, i: (i, b)), tok(LANES)],
        out_shape=[jax.ShapeDtypeStruct(s[:2], s[2]) for s in outs]
        + [jax.ShapeDtypeStruct((seq, batch * S5_WIDTH), F32), jax.ShapeDtypeStruct((n, LANES), F32)],
        compiler_params=_cparams("parallel", "parallel"),
        name="in_proj",
    )(x2, gain, w1, ctab, stab, qg, kg)


def _compress_mlp(h_ref, w_ref, p_ref, w2_ref):
    w = w_ref[...]
    pre = _dot(h_ref[...], w)
    phi, plo = _split_bf16(p_ref[...])
    pb = _dot(phi, w) + _dot(plo, w)
    rows = pre.shape[0]
    out = None
    for h in range(NSA_KV_HEADS):
        top = pre[:, LANES * h:LANES * (h + 1)]
        bot = pre[:, LANES * (2 + h):LANES * (3 + h)]
        bias = pb[0:1, LANES * h:LANES * (h + 1)] + pb[1:2, LANES * (2 + h):LANES * (3 + h)]
        a = _gelu(top + pltpu.roll(bot, rows - 1, 0) + bias)
        o = _dot(a.astype(BF16), w2_ref[h])
        out = o if out is None else out + o
    return out


def _compress_kernel(hk_ref, hv_ref, wk_ref, wv_ref, pk_ref, pv_ref, w2k_ref, w2v_ref,
                     kg_ref, c_ref, s_ref, kc_ref, vc_ref):
    k = _compress_mlp(hk_ref, wk_ref, pk_ref, w2k_ref)
    kc_ref[...] = _headnorm_rope(k, kg_ref[...], c_ref[...], s_ref[...]).astype(BF16)
    vc_ref[...] = _compress_mlp(hv_ref, wv_ref, pv_ref, w2v_ref).astype(BF16)


def _compress(hk, hv, wk, wv, pk, pv, w2k, w2v, kg, ccmp, scmp, batch):
    rows = hk.shape[0] // batch
    width = hk.shape[1]
    const2 = lambda b: (0, 0)
    const3 = lambda b: (0, 0, 0)
    rowb = lambda w: pl.BlockSpec((rows, w), lambda b: (b, 0))
    return pl.pallas_call(
        _compress_kernel,
        grid=(batch,),
        in_specs=[rowb(width), rowb(width),
                  pl.BlockSpec(wk.shape, const2), pl.BlockSpec(wv.shape, const2),
                  pl.BlockSpec(pk.shape, const2), pl.BlockSpec(pv.shape, const2),
                  pl.BlockSpec(w2k.shape, const3), pl.BlockSpec(w2v.shape, const3),
                  pl.BlockSpec((1, LANES), const2), rowb(LANES), rowb(LANES)],
        out_specs=[rowb(LANES), rowb(LANES)],
        out_shape=[jax.ShapeDtypeStruct((hk.shape[0], LANES), BF16)] * 2,
        compiler_params=_cparams("parallel"),
        name="compress_kv",
    )(hk, hv, wk, wv, pk, pv, w2k, w2v, kg, ccmp, scmp)


def _compress_weights(pos_emb, w1, w2):
    half = CMP_BLOCK // 2
    w1r = w1.reshape(CMP_BLOCK, HEAD_DIM, CMP_HIDDEN)
    wexp = jnp.zeros((half, NSA_KV_HEADS, HEAD_DIM, 2 * NSA_KV_HEADS, CMP_HIDDEN), F32)
    for h in range(NSA_KV_HEADS):
        wexp = wexp.at[:, h, :, h].set(w1r[:half])
        wexp = wexp.at[:, h, :, NSA_KV_HEADS + h].set(w1r[half:])
    wexp = wexp.reshape(half * KV_WIDTH, 2 * NSA_KV_HEADS * CMP_HIDDEN).astype(BF16)
    pexp = jnp.broadcast_to(pos_emb.reshape(2, half, 1, HEAD_DIM), (2, half, NSA_KV_HEADS, HEAD_DIM))
    pexp = jnp.pad(pexp.reshape(2, half * KV_WIDTH), ((0, SUBLANES - 2), (0, 0)))
    w2e = jnp.zeros((NSA_KV_HEADS, CMP_HIDDEN, KV_WIDTH), F32)
    for h in range(NSA_KV_HEADS):
        w2e = w2e.at[h, :, HEAD_DIM * h:HEAD_DIM * (h + 1)].set(w2)
    return wexp, pexp, w2e.astype(BF16)


def _attn_kernel(q_ref, ks_ref, vs_ref, kw_ref, vw_ref, kc_ref, vc_ref, g_ref, et_ref, ovt_ref,
                 o_ref, qc_ref, s_ref, p_ref, m_ref, al_ref, oc_ref, as_ref, aw_ref, *, n_cmp, n_slc, top_k):
    tq, tk, wk, rb, nh = ATT_TQ, ATT_TK, ATT_WK, ATT_RB, NSA_HEADS
    rows, grp_rows = nh * tq, NSA_GROUP * tq
    q0 = pl.program_id(1) * tq
    for a in range(nh):
        qc_ref[a * tq:(a + 1) * tq, 0:LANES] = q_ref[:, LANES * a:LANES * (a + 1)]
    q8 = qc_ref[:, 0:LANES]
    lane = lax.broadcasted_iota(jnp.int32, (tq, LANES), 1)
    t_row = q0 + lax.broadcasted_iota(jnp.int32, (tq, LANES), 0)

    s_c = _dot_t(q8, kc_ref[...]).reshape(nh, tq, LANES)
    mask_c = ((lane * CMP_STRIDE + (CMP_BLOCK - 1)) <= t_row) & (lane < n_cmp)
    sm = jnp.where(mask_c[None], s_c, NEG_BIG)
    p = jnp.where(mask_c[None], jnp.exp(sm - jnp.max(sm, axis=-1, keepdims=True)), 0.0)
    l = jnp.sum(p, axis=-1, keepdims=True)
    p_c = p * jnp.where(l > 0.0, 1.0 / l, 0.0)
    oc_ref[...] = _dot(p_c.reshape(rows, LANES).astype(BF16), vc_ref[...])

    blk = lax.broadcasted_iota(jnp.int32, (n_slc, tq), 0)
    cur = (q0 + lax.broadcasted_iota(jnp.int32, (n_slc, tq), 1)) >> int(math.log2(SLC_BLOCK))
    forced = (blk == 0) | (blk == cur) | (blk == cur - 1)
    for h in range(NSA_KV_HEADS):
        ps = p_c[NSA_GROUP * h]
        for g in range(1, NSA_GROUP):
            ps = ps + p_c[NSA_GROUP * h + g]
        hi, lo = _split_bf16(ps)
        imp = (_dot_t(ovt_ref[...], hi) + _dot_t(ovt_ref[...], lo))[0:n_slc]
        imp = jnp.where(forced, jnp.inf, jnp.where(blk > cur, -jnp.inf, imp))
        rank = jnp.zeros((n_slc, tq), F32)
        for i in range(n_slc):
            row = imp[i:i + 1, :]
            rank = rank + jnp.where((row > imp) | ((row == imp) & (blk > i)), 1.0, 0.0)
        bias_t = jnp.where(rank < float(top_k), 0.0, NEG_BIG)
        bias = jnp.concatenate([bias_t, jnp.zeros((LANES - n_slc, tq), F32)], axis=0).T.astype(BF16)
        for g in range(NSA_GROUP):
            a = NSA_GROUP * h + g
            qc_ref[a * tq:(a + 1) * tq, LANES:2 * LANES] = bias

    def softmax_pass(width, k0, mask, online):
        def block(r, carry):
            r0 = r * rb
            s = s_ref[pl.ds(r0, rb), 0:width]
            if mask is not None:
                t = q0 + ((r0 + lax.broadcasted_iota(jnp.int32, (rb, width), 0)) & (tq - 1))
                kpos = k0 + lax.broadcasted_iota(jnp.int32, (rb, width), 1)
                keep = kpos <= t
                if mask == "window":
                    keep = keep & ((t - kpos) < WINDOW)
                s = jnp.where(keep, s, NEG_BIG)
            m_new = jnp.broadcast_to(jnp.max(s, axis=1, keepdims=True), (rb, LANES))
            if online:
                m_old = m_ref[pl.ds(r0, rb), :]
                m_new = jnp.maximum(m_old, m_new)
                al_ref[pl.ds(r0, rb), :] = jnp.exp(m_old - m_new)
                m_ref[pl.ds(r0, rb), :] = m_new
            pp = jnp.exp(s - jnp.concatenate([m_new] * (width // LANES), axis=1))
            p_ref[pl.ds(r0, rb), 0:width] = pp.astype(BF16)
            return carry

        for r in range(rows // rb):
            block(r, 0)

    def pv(acc_ref, v, width, online):
        lane_v = lax.broadcasted_iota(jnp.int32, v.shape, 1)
        for h in range(NSA_KV_HEADS):
            rs = slice(h * grp_rows, (h + 1) * grp_rows)
            own = (lane_v < HEAD_DIM) if h == 0 else (lane_v >= HEAD_DIM)
            upd = _dot(p_ref[rs, 0:width], jnp.where(own, v, jnp.ones_like(v)))
            acc_ref[rs, :] = (al_ref[rs, :] * acc_ref[rs, :] + upd) if online else upd

    m_ref[...] = jnp.full(m_ref.shape, 0.1 * NEG_BIG, F32)
    as_ref[...] = jnp.zeros(as_ref.shape, F32)

    def slc_chunk(c, diagonal):
        k0 = pl.multiple_of(c * tk, tk)
        kcat = jnp.concatenate([ks_ref[pl.ds(k0, tk), :], et_ref[pl.ds(k0, tk), :]], axis=1)
        s_ref[:, 0:tk] = _dot_t(qc_ref[...], kcat)
        softmax_pass(tk, k0, "causal" if diagonal else None, True)
        pv(as_ref, vs_ref[pl.ds(k0, tk), :], tk, True)

    c_last = q0 >> int(math.log2(tk))

    def full_chunk(c, carry):
        slc_chunk(c, False)
        return carry

    lax.fori_loop(0, c_last, full_chunk, 0)
    slc_chunk(c_last, True)

    w0 = pl.multiple_of(jnp.maximum(q0 - WINDOW, 0), LANES)
    s_ref[:, 0:wk] = _dot_t(q8, kw_ref[pl.ds(w0, wk), :])
    softmax_pass(wk, w0, "window", False)
    pv(aw_ref, vw_ref[pl.ds(w0, wk), :], wk, False)

    lo_half = lane < HEAD_DIM
    for j in range(nh // 2):
        comb = []
        for a in (2 * j, 2 * j + 1):
            rs = slice(a * tq, (a + 1) * tq)
            o_s, o_w = as_ref[rs, :], aw_ref[rs, :]
            comb.append(g_ref[:, 3 * a:3 * a + 1] * oc_ref[rs, :]
                        + g_ref[:, 3 * a + 1:3 * a + 2] * (o_s * (1.0 / pltpu.roll(o_s, HEAD_DIM, 1)))
                        + g_ref[:, 3 * a + 2:3 * a + 3] * (o_w * (1.0 / pltpu.roll(o_w, HEAD_DIM, 1))))
        even, odd = comb
        if j // (NSA_GROUP // 2) == 0:
            pair = jnp.where(lo_half, even, pltpu.roll(odd, HEAD_DIM, 1))
        else:
            pair = jnp.where(lo_half, pltpu.roll(even, HEAD_DIM, 1), odd)
        o_ref[:, LANES * j:LANES * (j + 1)] = pair.astype(BF16)


def _pack_w_in(w):
    sizes = (NSA_WIDTH,) + (KV_WIDTH,) * 6 + (3 * NSA_HEADS, S5_WIDTH, D_MODEL, D_MODEL)
    o = np.cumsum((0,) + sizes)
    q, kc, vc, ks, vs, kw, vw, g, u, ga, gs = [w[:, o[i]:o[i + 1]] for i in range(len(sizes))]
    g = jnp.pad(g, ((0, 0), (0, LANES - 3 * NSA_HEADS)))
    w1 = jnp.concatenate([q, ks, kw, kc, vc, vs, vw, u, g], axis=1).astype(BF16)
    return w1, ga.astype(BF16), gs.astype(BF16)


def _nsa_branch(x2, gain, w1, ctab, stab, ccmp, scmp, q_gain, k_gain, cmp_k, cmp_v, batch, seq):
    qg = jnp.tile(q_gain, 2)[None, :]
    kg = jnp.tile(k_gain, 2)[None, :]
    q, ks, kw, kc_raw, vc_raw, vs, vw, u_tm, gates = _in_proj(x2, gain, w1, ctab, stab, qg, kg, batch, seq)
    half_rows = batch * seq // (CMP_BLOCK // 2)
    hk = kc_raw.reshape(half_rows, (CMP_BLOCK // 2) * KV_WIDTH)
    hv = vc_raw.reshape(half_rows, (CMP_BLOCK // 2) * KV_WIDTH)
    wk, pk, w2k = _compress_weights(*cmp_k)
    wv, pv, w2v = _compress_weights(*cmp_v)
    kc, vc = _compress(hk, hv, wk, wv, pk, pv, w2k, w2v, kg, ccmp, scmp, batch)
    o_a = _attention(q, ks, vs, kw, vw, kc, vc, gates, batch, seq)
    return o_a, u_tm


def _cmp_rope_tables(ctab, stab, batch, seq):
    def pick(t):
        t = t.reshape(batch, seq, LANES)[:, CMP_BLOCK - 1::CMP_STRIDE]
        t = jnp.pad(t, ((0, 0), (0, seq // CMP_STRIDE - t.shape[1]), (0, 0)))
        return t.reshape(batch * (seq // CMP_STRIDE), LANES)
    return pick(ctab), pick(stab)


def _s5_disc_kernel(are_ref, aim_ref, ldt_ref, lre_ref, lim_ref, fre_ref, fim_ref):
    lam_re = jnp.minimum(are_ref[...], -1e-4)
    lam_im = aim_ref[...]
    dt = jnp.exp(ldt_ref[...])
    mag = jnp.exp(lam_re * dt)
    ang = lam_im * dt
    lb_re = mag * jnp.cos(ang)
    lb_im = mag * jnp.sin(ang)
    den = lam_re * lam_re + lam_im * lam_im
    lre_ref[...] = lb_re
    lim_ref[...] = lb_im
    fre_ref[...] = ((lb_re - 1.0) * lam_re + lb_im * lam_im) / den
    fim_ref[...] = (lb_im * lam_re - (lb_re - 1.0) * lam_im) / den


def _s5_weights(a_re, a_im, log_dt, b_re, b_im, c_re, c_im):
    shp = jax.ShapeDtypeStruct((S5_GROUPS, S5_STATE), F32)
    lb_re, lb_im, f_re, f_im = pl.pallas_call(
        _s5_disc_kernel, out_shape=[shp] * 4, name="s5_discretise",
    )(a_re, a_im, log_dt[:, None])
    bb_re = f_re[..., None] * b_re - f_im[..., None] * b_im
    bb_im = f_re[..., None] * b_im + f_im[..., None] * b_re
    eye = jnp.eye(S5_GROUPS, dtype=F32)

    def in_map(bb):
        return jnp.einsum("gpc,gh->gchp", bb, eye).reshape(S5_WIDTH, S5_NSTATE)

    def out_map(cc):
        return jnp.einsum("gcp,gh->hpgc", cc, eye).reshape(S5_NSTATE, S5_WIDTH)

    w_b = jnp.concatenate([in_map(bb_re), in_map(bb_im)], axis=1).astype(BF16)
    w_c = jnp.concatenate([out_map(c_re), -out_map(c_im)], axis=0).astype(BF16)
    return lb_re.reshape(1, S5_NSTATE), lb_im.reshape(1, S5_NSTATE), w_b, w_c


def _s5_kernel(u_ref, wb_ref, lre_ref, lim_ref, wc_ref, d_ref, wg_ref, o_ref, x_ref, st_ref, *, batch):
    ns = S5_NSTATE

    @pl.when(pl.program_id(0) == 0)
    def _():
        st_ref[...] = jnp.zeros(st_ref.shape, F32)

    u = u_ref[...]
    x_ref[...] = _dot(u.astype(BF16), wb_ref[...])
    lre = jnp.broadcast_to(lre_ref[...], (batch, ns))
    lim = jnp.broadcast_to(lim_ref[...], (batch, ns))

    def step(t, carry):
        xr, xi = carry
        r0 = pl.multiple_of(t * batch, batch)
        nr = lre * xr - lim * xi + x_ref[pl.ds(r0, batch), 0:ns]
        ni = lre * xi + lim * xr + x_ref[pl.ds(r0, batch), ns:2 * ns]
        x_ref[pl.ds(r0, batch), 0:ns] = nr
        x_ref[pl.ds(r0, batch), ns:2 * ns] = ni
        return nr, ni

    steps = u.shape[0] // batch
    xr, xi = lax.fori_loop(0, steps, step, (st_ref[:, 0:ns], st_ref[:, ns:2 * ns]), unroll=4)
    st_ref[:, 0:ns] = xr
    st_ref[:, ns:2 * ns] = xi
    y = _dot(x_ref[...].astype(BF16), wc_ref[...]) + d_ref[...] * u
    z = _dot(_gelu(y).astype(BF16), wg_ref[...])
    o_ref[...] = (z[:, :S5_WIDTH] * _sigmoid(z[:, S5_WIDTH:])).astype(BF16)


def _s5(u_tm, lre, lim, w_b, w_c, d_skip, w_glu, batch, seq):
    rows = S5_T * batch
    const = lambda t: (0, 0)
    return pl.pallas_call(
        functools.partial(_s5_kernel, batch=batch),
        grid=(seq // S5_T,),
        in_specs=[pl.BlockSpec((rows, S5_WIDTH), lambda t: (t, 0)),
                  pl.BlockSpec(w_b.shape, const), pl.BlockSpec(lre.shape, const), pl.BlockSpec(lim.shape, const),
                  pl.BlockSpec(w_c.shape, const), pl.BlockSpec((1, S5_WIDTH), const),
                  pl.BlockSpec(w_glu.shape, const)],
        out_specs=pl.BlockSpec((rows, S5_WIDTH), lambda t: (t, 0)),
        out_shape=jax.ShapeDtypeStruct((seq * batch, S5_WIDTH), BF16),
        scratch_shapes=[pltpu.VMEM((rows, 2 * S5_NSTATE), F32), pltpu.VMEM((batch, 2 * S5_NSTATE), F32)],
        compiler_params=_cparams("arbitrary"),
        name="s5_scan",
    )(u_tm, w_b, lre, lim, w_c, d_skip, w_glu)


RT_GROUP_ROW = 0
RT_EXPERT_ROW = SUBLANES


def _rmsnorm(x, gain):
    return (x * lax.rsqrt(jnp.mean(x * x, axis=-1, keepdims=True) + RMS_EPS)) * gain


def _merge_kernel(x_ref, g1_ref, wga_ref, wgs_ref, oa_ref, ob_ref, pa_ref, ps_ref, wo_ref, g2_ref,
                  wrh_ref, wrl_ref, rb_ref, x1_ref, h2_ref, lg_ref):
    x = x_ref[...]
    h = _rmsnorm(x, g1_ref[...]).astype(BF16)
    gate_a = _sigmoid(_dot(h, wga_ref[...]))
    gate_s = _sigmoid(_dot(h, wgs_ref[...]))
    merged = gate_a * _dot(oa_ref[...], pa_ref[...]) + gate_s * _dot(ob_ref[...], ps_ref[...])
    x1 = x + _dot(merged.astype(BF16), wo_ref[...])
    x1_ref[...] = x1
    h2 = _rmsnorm(x1, g2_ref[...])
    h2_ref[...] = h2
    hi, lo = _split_bf16(h2)
    lg_ref[...] = (_dot_t(wrh_ref[...], hi) + _dot_t(wrh_ref[...], lo) + _dot_t(wrl_ref[...], hi)
                   + rb_ref[...])


def _router_weights(w_rg, b_rg, w_re, b_re):
    wt = jnp.zeros((LANES, D_MODEL), F32)
    wt = wt.at[RT_GROUP_ROW:RT_GROUP_ROW + N_EXPERT_GROUPS].set(w_rg.T)
    wt = wt.at[RT_EXPERT_ROW:RT_EXPERT_ROW + N_EXPERTS].set(w_re.T)
    bias = jnp.zeros((LANES, 1), F32)
    bias = bias.at[RT_GROUP_ROW:RT_GROUP_ROW + N_EXPERT_GROUPS, 0].set(b_rg)
    bias = bias.at[RT_EXPERT_ROW:RT_EXPERT_ROW + N_EXPERTS, 0].set(b_re)
    hi, lo = _split_bf16(wt)
    return hi, lo, bias


def _merge(x2, g1, w_ga, w_gs, o_a, ob_tm, p_a, p_s, w_o, g2, wrh, wrl, rb, batch, seq):
    n = x2.shape[0]
    tm = MRG_TM
    nt = seq // tm
    row = lambda b, i: (b * nt + i, 0)
    const = lambda b, i: (0, 0)
    tok = lambda w: pl.BlockSpec((tm, w), row)
    full = lambda a: pl.BlockSpec(a.shape, const)
    ob_v = ob_tm.reshape(seq, batch * S5_WIDTH)
    return pl.pallas_call(
        _merge_kernel,
        grid=(batch, nt),
        in_specs=[tok(D_MODEL), full(g1), full(w_ga), full(w_gs), tok(NSA_WIDTH),
                  pl.BlockSpec((tm, S5_WIDTH), lambda b, i: (i, b)), full(p_a), full(p_s), full(w_o),
                  full(g2), full(wrh), full(wrl), full(rb)],
        out_specs=[tok(D_MODEL), tok(D_MODEL), pl.BlockSpec((LANES, tm), lambda b, i: (0, b * nt + i))],
        out_shape=[jax.ShapeDtypeStruct((n, D_MODEL), F32), jax.ShapeDtypeStruct((n, D_MODEL), F32),
                   jax.ShapeDtypeStruct((LANES, n), F32)],
        compiler_params=_cparams("parallel", "parallel"),
        name="merge_out_proj",
    )(x2, g1, w_ga, w_gs, o_a, ob_v, p_a, p_s, w_o, g2, wrh, wrl, rb)


def _first_argmax(v, rows):
    vmax = jnp.max(v, axis=0, keepdims=True)
    idx = jnp.min(jnp.where(v == vmax, rows, SUBLANES), axis=0, keepdims=True)
    return vmax, idx


def _route_kernel(lg_ref, tri_ref, e_ref, w_ref, r_ref, cnt_ref, carry_ref):
    tt = lg_ref.shape[1]

    @pl.when(pl.program_id(0) == 0)
    def _():
        carry_ref[...] = jnp.zeros(carry_ref.shape, F32)

    rows = lax.broadcasted_iota(jnp.int32, (SUBLANES, tt), 0)
    gl = jnp.where(rows < N_EXPERT_GROUPS, lg_ref[RT_GROUP_ROW:RT_GROUP_ROW + SUBLANES, :], -jnp.inf)
    gmax, g_sel = _first_argmax(gl, rows)
    g_w = 1.0 / jnp.sum(jnp.exp(gl - gmax), axis=0, keepdims=True)

    el = jnp.zeros((EXPERTS_PER_GROUP, tt), F32)
    for g in range(N_EXPERT_GROUPS):
        r0 = RT_EXPERT_ROW + EXPERTS_PER_GROUP * g
        el = jnp.where(g_sel == g, lg_ref[r0:r0 + EXPERTS_PER_GROUP, :], el)
    ex = jnp.exp(el - jnp.max(el, axis=0, keepdims=True))
    p = ex / jnp.sum(ex, axis=0, keepdims=True)
    p1, i1 = _first_argmax(p, rows)
    p2, i2 = _first_argmax(jnp.where(rows == i1, -1.0, p), rows)
    e1 = g_sel * EXPERTS_PER_GROUP + i1
    e2 = g_sel * EXPERTS_PER_GROUP + i2
    scale = g_w / (p1 + p2)

    erow = lax.broadcasted_iota(jnp.int32, (N_EXPERTS, tt), 0)
    oh1 = jnp.where(erow == e1, 1.0, 0.0)
    oh2 = jnp.where(erow == e2, 1.0, 0.0)
    both = oh1 + oh2
    before = carry_ref[:, 0:1] + _dot(both.astype(BF16), tri_ref[...])
    carry_ref[...] = carry_ref[...] + jnp.sum(both, axis=1, keepdims=True)
    cnt_ref[...] = carry_ref[...]

    zi = jnp.zeros((SUBLANES - 2, tt), jnp.int32)
    e_ref[...] = jnp.concatenate([e1, e2, zi], axis=0)
    w_ref[...] = jnp.concatenate([p1 * scale, p2 * scale, zi.astype(F32)], axis=0)
    r_ref[...] = jnp.concatenate([jnp.sum(oh1 * before, axis=0, keepdims=True),
                                  jnp.sum(oh2 * before, axis=0, keepdims=True),
                                  zi.astype(F32)], axis=0).astype(jnp.int32)


def _route(logits_t):
    n = logits_t.shape[1]
    tt = RT_TT
    tri = jnp.asarray(np.triu(np.ones((tt, tt), np.float32), 1), BF16)
    blk = pl.BlockSpec((SUBLANES, tt), lambda i: (0, i))
    return pl.pallas_call(
        _route_kernel,
        grid=(n // tt,),
        in_specs=[pl.BlockSpec((LANES, tt), lambda i: (0, i)), pl.BlockSpec((tt, tt), lambda i: (0, 0))],
        out_specs=[blk, blk, blk, pl.BlockSpec((N_EXPERTS, LANES), lambda i: (0, 0))],
        out_shape=[jax.ShapeDtypeStruct((SUBLANES, n), jnp.int32), jax.ShapeDtypeStruct((SUBLANES, n), F32),
                   jax.ShapeDtypeStruct((SUBLANES, n), jnp.int32), jax.ShapeDtypeStruct((N_EXPERTS, LANES), F32)],
        scratch_shapes=[pltpu.VMEM((N_EXPERTS, LANES), F32)],
        compiler_params=_cparams("arbitrary"),
        name="moe_route",
    )(logits_t, tri)


def _dispatch_kernel(dest_ref, h_ref, zero_ref, buf_ref, sem):
    del zero_ref
    tt = dest_ref.shape[1]
    t0 = pl.program_id(0) * tt

    def row_copy(t, k):
        return pltpu.make_async_copy(h_ref.at[pl.ds(t0 + t, 1)], buf_ref.at[pl.ds(dest_ref[k, t], 1)], sem)

    def issue(t, c):
        for k in range(EXPERT_TOP_K):
            row_copy(t, k).start()
        return c

    def drain(t, c):
        for k in range(EXPERT_TOP_K):
            row_copy(t, k).wait()
        return c

    lax.fori_loop(0, tt, issue, 0, unroll=8)
    lax.fori_loop(0, tt, drain, 0, unroll=8)


def _dispatch(dest, h2, cap):
    n = h2.shape[0]
    tt = DSP_TT
    return pl.pallas_call(
        _dispatch_kernel,
        grid=(n // tt,),
        in_specs=[pl.BlockSpec((EXPERT_TOP_K, tt), lambda i: (0, i), memory_space=pltpu.SMEM),
                  pl.BlockSpec(memory_space=pl.ANY), pl.BlockSpec(memory_space=pl.ANY)],
        out_specs=pl.BlockSpec(memory_space=pl.ANY),
        out_shape=jax.ShapeDtypeStruct((cap, D_MODEL), h2.dtype),
        scratch_shapes=[pltpu.SemaphoreType.DMA],
        input_output_aliases={2: 0},
        compiler_params=_cparams("arbitrary"),
        name="moe_dispatch",
    )(dest, h2, jnp.zeros((cap, D_MODEL), h2.dtype))


def _expert_kernel(be_ref, nb_ref, x_ref, wg_ref, wu_ref, wd_ref, o_ref, wgb_ref, wub_ref, wdb_ref):
    i = pl.program_id(0)

    @pl.when(i < nb_ref[0])
    def _():
        @pl.when((i == 0) | (be_ref[i] != be_ref[jnp.maximum(i - 1, 0)]))
        def _():
            wgb_ref[...] = wg_ref[...].astype(BF16)
            wub_ref[...] = wu_ref[...].astype(BF16)
            wdb_ref[...] = wd_ref[...].astype(BF16)

        x = x_ref[...].astype(BF16)
        g = _dot(x, wgb_ref[...])
        u = _dot(x, wub_ref[...])
        o_ref[...] = _dot(((g * _sigmoid(g)) * u).astype(BF16), wdb_ref[...])

    @pl.when(i >= nb_ref[0])
    def _():
        o_ref[...] = jnp.zeros(o_ref.shape, o_ref.dtype)


def _experts(blk_expert, n_used, buf, w_gate, w_up, w_down, layer):
    cap = buf.shape[0]
    blk = MOE_BLK
    row = lambda i, be, nb: (jnp.minimum(i, nb[0] - 1), 0)
    wmap = lambda i, be, nb: (layer, be[jnp.minimum(i, nb[0] - 1)], 0, 0)
    return pl.pallas_call(
        _expert_kernel,
        grid_spec=pltpu.PrefetchScalarGridSpec(
            num_scalar_prefetch=2,
            grid=(cap // blk,),
            in_specs=[pl.BlockSpec((blk, D_MODEL), row),
                      pl.BlockSpec((None, None, D_MODEL, EXPERT_FF), wmap),
                      pl.BlockSpec((None, None, D_MODEL, EXPERT_FF), wmap),
                      pl.BlockSpec((None, None, EXPERT_FF, D_MODEL), wmap)],
            out_specs=pl.BlockSpec((blk, D_MODEL), lambda i, be, nb: (i, 0)),
            scratch_shapes=[pltpu.VMEM((D_MODEL, EXPERT_FF), BF16), pltpu.VMEM((D_MODEL, EXPERT_FF), BF16),
                            pltpu.VMEM((EXPERT_FF, D_MODEL), BF16)],
        ),
        out_shape=jax.ShapeDtypeStruct((cap, D_MODEL), F32),
        compiler_params=_cparams("arbitrary"),
        name="moe_experts",
    )(blk_expert, n_used, buf, w_gate, w_up, w_down)


def _combine_kernel(dest_ref, x1_ref, w_ref, ob_ref, o_ref, rows_ref, sem):
    tt = dest_ref.shape[1]

    def row_copy(t, k):
        return pltpu.make_async_copy(ob_ref.at[pl.ds(dest_ref[k, t], 1)], rows_ref.at[k, pl.ds(t, 1)], sem)

    def issue(t, c):
        for k in range(EXPERT_TOP_K):
            row_copy(t, k).start()
        return c

    def drain(t, c):
        for k in range(EXPERT_TOP_K):
            row_copy(t, k).wait()
        return c

    lax.fori_loop(0, tt, issue, 0, unroll=8)
    lax.fori_loop(0, tt, drain, 0, unroll=8)
    o_ref[...] = x1_ref[...] + w_ref[:, 0:1] * rows_ref[0] + w_ref[:, 1:2] * rows_ref[1]


def _combine(dest, x1, w_col, out_buf):
    n = x1.shape[0]
    tt = DSP_TT
    return pl.pallas_call(
        _combine_kernel,
        grid=(n // tt,),
        in_specs=[pl.BlockSpec((EXPERT_TOP_K, tt), lambda i: (0, i), memory_space=pltpu.SMEM),
                  pl.BlockSpec((tt, D_MODEL), lambda i: (i, 0)),
                  pl.BlockSpec((tt, EXPERT_TOP_K), lambda i: (i, 0)),
                  pl.BlockSpec(memory_space=pl.ANY)],
        out_specs=pl.BlockSpec((tt, D_MODEL), lambda i: (i, 0)),
        out_shape=jax.ShapeDtypeStruct((n, D_MODEL), F32),
        scratch_shapes=[pltpu.VMEM((EXPERT_TOP_K, tt, D_MODEL), out_buf.dtype), pltpu.SemaphoreType.DMA],
        compiler_params=_cparams("arbitrary"),
        name="moe_combine",
    )(dest, x1, w_col, out_buf)


def _moe(x1, h2, logits_t, w_gate, w_up, w_down, layer):
    n = x1.shape[0]
    e_idx, wts, rank, counts = _route(logits_t)
    counts = counts[:, 0].astype(jnp.int32)
    padded = (counts + MOE_BLK - 1) // MOE_BLK * MOE_BLK
    pends = jnp.cumsum(padded)
    pstarts = pends - padded
    dest = pstarts[e_idx[:EXPERT_TOP_K]] + rank[:EXPERT_TOP_K]
    cap = n * EXPERT_TOP_K + N_EXPERTS * MOE_BLK
    n_blk = cap // MOE_BLK
    blk_expert = jnp.minimum(jnp.searchsorted(pends, jnp.arange(n_blk, dtype=jnp.int32) * MOE_BLK, side="right"),
                             N_EXPERTS - 1).astype(jnp.int32)
    n_used = (pends[-1:] // MOE_BLK).astype(jnp.int32)
    buf = _dispatch(dest, h2, cap)
    out_buf = _experts(blk_expert, n_used, buf, w_gate, w_up, w_down, layer)
    return _combine(dest, x1, wts[:EXPERT_TOP_K].T, out_buf)


def kernel(x, positions, norm_mix, norm_ffn, w_in, q_gain, k_gain, cmp_k_pos, cmp_k_w1, cmp_k_w2, cmp_v_pos,
           cmp_v_w1, cmp_v_w2, s5_a_re, s5_a_im, s5_log_dt, s5_b_re, s5_b_im, s5_c_re, s5_c_im, s5_d, s5_glu,
           proj_nsa, proj_s5, w_out, router_group, router_group_bias, router_expert, router_expert_bias,
           expert_gate, expert_up, expert_down):
    batch, seq, _ = x.shape
    depth = w_in.shape[0]
    n = batch * seq
    inv_freq = ROPE_THETA ** (-jnp.arange(0, ROPE_DIMS, 2, dtype=jnp.float32) / ROPE_DIMS)
    ctab, stab = _rope_tables(positions.reshape(n, 1), inv_freq)
    ccmp, scmp = _cmp_rope_tables(ctab, stab, batch, seq)
    x2 = x.reshape(n, D_MODEL)
    for l in range(depth):
        w1, w_ga, w_gs = _pack_w_in(w_in[l])
        o_a, u_tm = _nsa_branch(x2, norm_mix[l][None], w1, ctab, stab, ccmp, scmp, q_gain[l], k_gain[l],
                                (cmp_k_pos[l], cmp_k_w1[l], cmp_k_w2[l]),
                                (cmp_v_pos[l], cmp_v_w1[l], cmp_v_w2[l]), batch, seq)
        lre, lim, w_b, w_c = _s5_weights(s5_a_re[l], s5_a_im[l], s5_log_dt[l], s5_b_re[l], s5_b_im[l],
                                         s5_c_re[l], s5_c_im[l])
        ob_tm = _s5(u_tm.reshape(seq * batch, S5_WIDTH), lre, lim, w_b, w_c, s5_d[l][None],
                    s5_glu[l].astype(BF16), batch, seq)
        wrh, wrl, rb = _router_weights(router_group[l], router_group_bias[l], router_expert[l],
                                       router_expert_bias[l])
        x1, h2, logits_t = _merge(x2, norm_mix[l][None], w_ga, w_gs, o_a, ob_tm, proj_nsa[l].astype(BF16),
                                  proj_s5[l].astype(BF16), w_out[l].astype(BF16), norm_ffn[l][None],
                                  wrh, wrl, rb, batch, seq)
        x2 = _moe(x1, h2, logits_t, expert_gate, expert_up, expert_down, l)
    return x2.reshape(batch, seq, D_MODEL)


def _attn_tables(seq):
    n_cmp = (seq - CMP_BLOCK) // CMP_STRIDE + 1
    n_slc = seq // SLC_BLOCK
    key_blk = np.arange(seq) // SLC_BLOCK
    expand_t = (key_blk[:, None] == np.arange(LANES)[None, :]).astype(np.float32)
    cmp_start = np.arange(LANES) * CMP_STRIDE
    slc_start = np.arange(LANES) * SLC_BLOCK
    overlap_t = ((cmp_start[None, :] < slc_start[:, None] + SLC_BLOCK)
                 & (cmp_start[None, :] + CMP_BLOCK > slc_start[:, None])
                 & (np.arange(LANES)[None, :] < n_cmp) & (np.arange(LANES)[:, None] < n_slc))
    return (jnp.asarray(expand_t, BF16), jnp.asarray(overlap_t.astype(np.float32), BF16), n_cmp, n_slc)


def _attention(q, ks, vs, kw, vw, kc, vc, gates, batch, seq):
    expand_t, overlap_t, n_cmp, n_slc = _attn_tables(seq)
    tq = ATT_TQ
    nq = seq // tq
    rows = NSA_HEADS * tq
    width = max(ATT_TK, ATT_WK)
    tile = lambda w: pl.BlockSpec((tq, w), lambda b, i: (b * nq + i, 0))
    per_b = lambda r: pl.BlockSpec((r, LANES), lambda b, i: (b, 0))
    f32_rows = pltpu.VMEM((rows, LANES), F32)
    return pl.pallas_call(
        functools.partial(_attn_kernel, n_cmp=n_cmp, n_slc=n_slc, top_k=min(SLC_TOPK, n_slc)),
        grid=(batch, nq),
        in_specs=[tile(2 * NSA_WIDTH), per_b(seq), per_b(seq), per_b(seq), per_b(seq),
                  per_b(LANES), per_b(LANES), tile(LANES),
                  pl.BlockSpec(expand_t.shape, lambda b, i: (0, 0)),
                  pl.BlockSpec(overlap_t.shape, lambda b, i: (0, 0))],
        out_specs=tile(NSA_WIDTH),
        out_shape=jax.ShapeDtypeStruct((batch * seq, NSA_WIDTH), BF16),
        scratch_shapes=[pltpu.VMEM((rows, 2 * LANES), BF16),
                        pltpu.VMEM((rows, width), F32), pltpu.VMEM((rows, width), BF16),
                        f32_rows, f32_rows, f32_rows, f32_rows, f32_rows],
        compiler_params=_cparams("parallel", "parallel"),
        name="nsa_attention",
    )(q, ks, vs, kw, vw, kc, vc, gates, expand_t, overlap_t)
```

```python
import functools
import math

import jax
import jax.numpy as jnp
import numpy as np
from jax import lax
from jax.experimental import pallas as pl
from jax.experimental.pallas import tpu as pltpu

F32 = jnp.float32
BF16 = jnp.bfloat16

D_MODEL = 1024
NSA_HEADS = 8
NSA_KV_HEADS = 2
NSA_GROUP = NSA_HEADS // NSA_KV_HEADS
HEAD_DIM = 64
ROPE_DIMS = HEAD_DIM // 4
ROPE_THETA = 500000.0
CMP_BLOCK = 32
CMP_STRIDE = 16
CMP_HIDDEN = 2 * HEAD_DIM
SLC_BLOCK = 64
SLC_TOPK = 16
WINDOW = 512
NSA_WIDTH = NSA_HEADS * HEAD_DIM
KV_WIDTH = NSA_KV_HEADS * HEAD_DIM
S5_WIDTH = 256
S5_GROUP_CH = 16
S5_GROUPS = S5_WIDTH // S5_GROUP_CH
S5_STATE = 64
S5_NSTATE = S5_GROUPS * S5_STATE
N_EXPERT_GROUPS = 4
EXPERTS_PER_GROUP = 8
N_EXPERTS = N_EXPERT_GROUPS * EXPERTS_PER_GROUP
EXPERT_TOP_K = 2
EXPERT_FF = 512
RMS_EPS = 1e-6
NEG_BIG = -1e30

LANES = 128
SUBLANES = 8
VMEM_LIMIT = 56 * 1024 * 1024

IN_TM = 512
ATT_TQ = 128
ATT_TK = 512
ATT_WK = WINDOW + ATT_TQ
ATT_RB = 64
S5_T = 128
MRG_TM = 256
RT_TT = 512
MOE_BLK = 256
DSP_TT = 256

C_Q, C_KS, C_KW, C_KC, C_VC, C_VS, C_VW, C_U, C_G = 0, 512, 640, 768, 896, 1024, 1152, 1280, 1536
IN_W = 1664


def _cparams(*sem):
    return pltpu.CompilerParams(dimension_semantics=sem, vmem_limit_bytes=VMEM_LIMIT)


def _gelu(x):
    return 0.5 * x * (1.0 + jnp.tanh(math.sqrt(2.0 / math.pi) * (x + 0.044715 * (x * x * x))))


def _sigmoid(x):
    return 1.0 / (1.0 + jnp.exp(-x))


def _dot(a, b):
    return jnp.dot(a, b, preferred_element_type=F32)


def _dot_t(a, b):
    return lax.dot_general(a, b, (((1,), (1,)), ((), ())), preferred_element_type=F32)


def _split_bf16(x):
    hi = x.astype(BF16)
    lo = (x - hi.astype(F32)).astype(BF16)
    return hi, lo


def _rope_kernel(pos_ref, freq_ref, sign_ref, c_ref, s_ref):
    ang = pos_ref[...].astype(F32) * freq_ref[...]
    c_ref[...] = jnp.cos(ang)
    s_ref[...] = jnp.sin(ang) * sign_ref[...]


def _rope_tables(pos_col, inv_freq):
    n = pos_col.shape[0]
    lane = np.arange(LANES) % HEAD_DIM
    sel = np.zeros((LANES, ROPE_DIMS // 2), np.float32)
    for l in range(LANES):
        if lane[l] < ROPE_DIMS:
            sel[l, lane[l] % (ROPE_DIMS // 2)] = 1.0
    freq = (jnp.asarray(sel) * inv_freq[None, :]).sum(-1)[None, :]
    sign = np.where(lane < ROPE_DIMS // 2, -1.0, 1.0).astype(np.float32)[None, :]
    tm = 1024
    return pl.pallas_call(
        _rope_kernel,
        grid=(n // tm,),
        in_specs=[pl.BlockSpec((tm, 1), lambda i: (i, 0)),
                  pl.BlockSpec((1, LANES), lambda i: (0, 0)),
                  pl.BlockSpec((1, LANES), lambda i: (0, 0))],
        out_specs=[pl.BlockSpec((tm, LANES), lambda i: (i, 0))] * 2,
        out_shape=[jax.ShapeDtypeStruct((n, LANES), F32)] * 2,
        compiler_params=_cparams("parallel"),
        name="rope_tables",
    )(pos_col, freq, jnp.asarray(sign))


def _headnorm_rope(t, gain, c, s):
    lane = lax.broadcasted_iota(jnp.int32, t.shape, 1)
    lo = lane < HEAD_DIM
    sq = t * t
    s_lo = jnp.sum(jnp.where(lo, sq, 0.0), axis=-1, keepdims=True)
    s_hi = jnp.sum(jnp.where(lo, 0.0, sq), axis=-1, keepdims=True)
    ms = jnp.where(lo, s_lo, s_hi) * (1.0 / HEAD_DIM)
    tn = (t * lax.rsqrt(ms + RMS_EPS)) * gain
    half = ROPE_DIMS // 2
    partner = jnp.where((lane & (HEAD_DIM - 1)) < half,
                        pltpu.roll(tn, LANES - half, 1), pltpu.roll(tn, half, 1))
    return tn * c + partner * s


def _in_proj_kernel(x_ref, gain_ref, w_ref, c_ref, s_ref, qg_ref, kg_ref,
                    q_ref, ks_ref, kw_ref, kc_ref, vc_ref, vs_ref, vw_ref, u_ref, g_ref):
    x = x_ref[...]
    ms = jnp.mean(x * x, axis=-1, keepdims=True)
    h = (x * lax.rsqrt(ms + RMS_EPS)) * gain_ref[...]
    z = _dot(h.astype(BF16), w_ref[...])
    c = c_ref[...]
    s = s_ref[...]
    lane = lax.broadcasted_iota(jnp.int32, c.shape, 1)
    lo = lane < HEAD_DIM
    scale = HEAD_DIM ** -0.5
    for j in range(NSA_HEADS // 2):
        r = _headnorm_rope(z[:, C_Q + LANES * j:C_Q + LANES * (j + 1)], qg_ref[...], c, s) * scale
        rr = pltpu.roll(r, HEAD_DIM, 1)
        if j // (NSA_GROUP // 2) == 0:
            even, odd = jnp.where(lo, r, 0.0), jnp.where(lo, rr, 0.0)
        else:
            even, odd = jnp.where(lo, 0.0, rr), jnp.where(lo, 0.0, r)
        q_ref[:, 2 * LANES * j:2 * LANES * j + LANES] = even.astype(BF16)
        q_ref[:, 2 * LANES * j + LANES:2 * LANES * (j + 1)] = odd.astype(BF16)
    ks_ref[...] = _headnorm_rope(z[:, C_KS:C_KS + LANES], kg_ref[...], c, s).astype(BF16)
    kw_ref[...] = _headnorm_rope(z[:, C_KW:C_KW + LANES], kg_ref[...], c, s).astype(BF16)
    kc_ref[...] = z[:, C_KC:C_KC + LANES].astype(BF16)
    vc_ref[...] = z[:, C_VC:C_VC + LANES].astype(BF16)
    vs_ref[...] = z[:, C_VS:C_VS + LANES].astype(BF16)
    vw_ref[...] = z[:, C_VW:C_VW + LANES].astype(BF16)
    u_ref[...] = z[:, C_U:C_U + S5_WIDTH]
    g_ref[...] = _sigmoid(z[:, C_G:C_G + LANES])


def _in_proj(x2, gain, w1, ctab, stab, qg, kg, batch, seq):
    n = x2.shape[0]
    tm = IN_TM
    nt = seq // tm
    row = lambda b, i: (b * nt + i, 0)
    const = lambda b, i: (0, 0)
    tok = lambda w: pl.BlockSpec((tm, w), row)
    outs = [(n, 2 * NSA_WIDTH, BF16)] + [(n, LANES, BF16)] * 6
    return pl.pallas_call(
        _in_proj_kernel,
        grid=(batch, nt),
        in_specs=[tok(D_MODEL), pl.BlockSpec((1, D_MODEL), const), pl.BlockSpec((D_MODEL, IN_W), const),
                  tok(LANES), tok(LANES), pl.BlockSpec((1, LANES), const), pl.BlockSpec((1, LANES), const)],
        out_specs=[tok(2 * NSA_WIDTH)] + [tok(LANES)] * 6
        + [pl.BlockSpec((tm, S5_WIDTH), lambda b, i: (i, b)), tok(LANES)],
        out_shape=[jax.ShapeDtypeStruct(s[:2], s[2]) for s in outs]
        + [jax.ShapeDtypeStruct((seq, batch * S5_WIDTH), F32), jax.ShapeDtypeStruct((n, LANES), F32)],
        compiler_params=_cparams("parallel", "parallel"),
        name="in_proj",
    )(x2, gain, w1, ctab, stab, qg, kg)


def _compress_mlp(h_ref, w_ref, p_ref, w2_ref):
    w = w_ref[...]
    pre = _dot(h_ref[...], w)
    phi, plo = _split_bf16(p_ref[...])
    pb = _dot(phi, w) + _dot(plo, w)
    rows = pre.shape[0]
    out = None
    for h in range(NSA_KV_HEADS):
        top = pre[:, LANES * h:LANES * (h + 1)]
        bot = pre[:, LANES * (2 + h):LANES * (3 + h)]
        bias = pb[0:1, LANES * h:LANES * (h + 1)] + pb[1:2, LANES * (2 + h):LANES * (3 + h)]
        a = _gelu(top + pltpu.roll(bot, rows - 1, 0) + bias)
        o = _dot(a.astype(BF16), w2_ref[h])
        out = o if out is None else out + o
    return out


def _compress_kernel(hk_ref, hv_ref, wk_ref, wv_ref, pk_ref, pv_ref, w2k_ref, w2v_ref,
                     kg_ref, c_ref, s_ref, kc_ref, vc_ref):
    k = _compress_mlp(hk_ref, wk_ref, pk_ref, w2k_ref)
    kc_ref[...] = _headnorm_rope(k, kg_ref[...], c_ref[...], s_ref[...]).astype(BF16)
    vc_ref[...] = _compress_mlp(hv_ref, wv_ref, pv_ref, w2v_ref).astype(BF16)


def _compress(hk, hv, wk, wv, pk, pv, w2k, w2v, kg, ccmp, scmp, batch):
    rows = hk.shape[0] // batch
    width = hk.shape[1]
    const2 = lambda b: (0, 0)
    const3 = lambda b: (0, 0, 0)
    rowb = lambda w: pl.BlockSpec((rows, w), lambda b: (b, 0))
    return pl.pallas_call(
        _compress_kernel,
        grid=(batch,),
        in_specs=[rowb(width), rowb(width),
                  pl.BlockSpec(wk.shape, const2), pl.BlockSpec(wv.shape, const2),
                  pl.BlockSpec(pk.shape, const2), pl.BlockSpec(pv.shape, const2),
                  pl.BlockSpec(w2k.shape, const3), pl.BlockSpec(w2v.shape, const3),
                  pl.BlockSpec((1, LANES), const2), rowb(LANES), rowb(LANES)],
        out_specs=[rowb(LANES), rowb(LANES)],
        out_shape=[jax.ShapeDtypeStruct((hk.shape[0], LANES), BF16)] * 2,
        compiler_params=_cparams("parallel"),
        name="compress_kv",
    )(hk, hv, wk, wv, pk, pv, w2k, w2v, kg, ccmp, scmp)


def _compress_weights(pos_emb, w1, w2):
    half = CMP_BLOCK // 2
    eye = jnp.eye(NSA_KV_HEADS, dtype=F32)
    w1r = w1.reshape(2, half, HEAD_DIM, CMP_HIDDEN)
    wexp = jnp.einsum("sldn,ph->lpdshn", w1r, eye)
    wexp = wexp.reshape(half * KV_WIDTH, 2 * NSA_KV_HEADS * CMP_HIDDEN).astype(BF16)
    pexp = jnp.broadcast_to(pos_emb.reshape(2, half, 1, HEAD_DIM), (2, half, NSA_KV_HEADS, HEAD_DIM))
    pexp = jnp.pad(pexp.reshape(2, half * KV_WIDTH), ((0, SUBLANES - 2), (0, 0)))
    w2e = jnp.einsum("cd,ph->pchd", w2, eye).reshape(NSA_KV_HEADS, CMP_HIDDEN, KV_WIDTH)
    return wexp, pexp, w2e.astype(BF16)


def _attn_kernel(q_ref, ks_ref, vs_ref, kw_ref, vw_ref, kc_ref, vc_ref, g_ref, et_ref, ovt_ref,
                 o_ref, qc_ref, s_ref, p_ref, m_ref, al_ref, oc_ref, as_ref, aw_ref, *, n_cmp, n_slc, top_k):
    tq, tk, wk, rb, nh = ATT_TQ, ATT_TK, ATT_WK, ATT_RB, NSA_HEADS
    rows, grp_rows = nh * tq, NSA_GROUP * tq
    q0 = pl.program_id(1) * tq
    for a in range(nh):
        qc_ref[a * tq:(a + 1) * tq, 0:LANES] = q_ref[:, LANES * a:LANES * (a + 1)]
    q8 = qc_ref[:, 0:LANES]
    lane = lax.broadcasted_iota(jnp.int32, (tq, LANES), 1)
    t_row = q0 + lax.broadcasted_iota(jnp.int32, (tq, LANES), 0)

    s_c = _dot_t(q8, kc_ref[...]).reshape(nh, tq, LANES)
    mask_c = ((lane * CMP_STRIDE + (CMP_BLOCK - 1)) <= t_row) & (lane < n_cmp)
    sm = jnp.where(mask_c[None], s_c, NEG_BIG)
    p = jnp.where(mask_c[None], jnp.exp(sm - jnp.max(sm, axis=-1, keepdims=True)), 0.0)
    l = jnp.sum(p, axis=-1, keepdims=True)
    p_c = p * jnp.where(l > 0.0, 1.0 / l, 0.0)
    oc_ref[...] = _dot(p_c.reshape(rows, LANES).astype(BF16), vc_ref[...])

    blk = lax.broadcasted_iota(jnp.int32, (n_slc, tq), 0)
    cur = (q0 + lax.broadcasted_iota(jnp.int32, (n_slc, tq), 1)) >> int(math.log2(SLC_BLOCK))
    forced = (blk == 0) | (blk == cur) | (blk == cur - 1)
    for h in range(NSA_KV_HEADS):
        ps = p_c[NSA_GROUP * h]
        for g in range(1, NSA_GROUP):
            ps = ps + p_c[NSA_GROUP * h + g]
        hi, lo = _split_bf16(ps)
        imp = (_dot_t(ovt_ref[...], hi) + _dot_t(ovt_ref[...], lo))[0:n_slc]
        imp = jnp.where(forced, jnp.inf, jnp.where(blk > cur, -jnp.inf, imp))
        rank = jnp.zeros((n_slc, tq), F32)
        for i in range(n_slc):
            row = imp[i:i + 1, :]
            rank = rank + jnp.where((row > imp) | ((row == imp) & (blk > i)), 1.0, 0.0)
        bias_t = jnp.where(rank < float(top_k), 0.0, NEG_BIG)
        bias = jnp.concatenate([bias_t, jnp.zeros((LANES - n_slc, tq), F32)], axis=0).T.astype(BF16)
        for g in range(NSA_GROUP):
            a = NSA_GROUP * h + g
            qc_ref[a * tq:(a + 1) * tq, LANES:2 * LANES] = bias

    def softmax_pass(width, k0, mask, online):
        def block(r, carry):
            r0 = r * rb
            s = s_ref[pl.ds(r0, rb), 0:width]
            if mask is not None:
                t = q0 + ((r0 + lax.broadcasted_iota(jnp.int32, (rb, width), 0)) & (tq - 1))
                kpos = k0 + lax.broadcasted_iota(jnp.int32, (rb, width), 1)
                keep = kpos <= t
                if mask == "window":
                    keep = keep & ((t - kpos) < WINDOW)
                s = jnp.where(keep, s, NEG_BIG)
            m_new = jnp.broadcast_to(jnp.max(s, axis=1, keepdims=True), (rb, LANES))
            if online:
                m_old = m_ref[pl.ds(r0, rb), :]
                m_new = jnp.maximum(m_old, m_new)
                al_ref[pl.ds(r0, rb), :] = jnp.exp(m_old - m_new)
                m_ref[pl.ds(r0, rb), :] = m_new
            pp = jnp.exp(s - jnp.concatenate([m_new] * (width // LANES), axis=1))
            p_ref[pl.ds(r0, rb), 0:width] = pp.astype(BF16)
            return carry

        for r in range(rows // rb):
            block(r, 0)

    def pv(acc_ref, v, width, online):
        lane_v = lax.broadcasted_iota(jnp.int32, v.shape, 1)
        for h in range(NSA_KV_HEADS):
            rs = slice(h * grp_rows, (h + 1) * grp_rows)
            own = (lane_v < HEAD_DIM) if h == 0 else (lane_v >= HEAD_DIM)
            upd = _dot(p_ref[rs, 0:width], jnp.where(own, v, jnp.ones_like(v)))
            acc_ref[rs, :] = (al_ref[rs, :] * acc_ref[rs, :] + upd) if online else upd

    m_ref[...] = jnp.full(m_ref.shape, 0.1 * NEG_BIG, F32)
    as_ref[...] = jnp.zeros(as_ref.shape, F32)

    def slc_chunk(c, diagonal):
        k0 = pl.multiple_of(c * tk, tk)
        kcat = jnp.concatenate([ks_ref[pl.ds(k0, tk), :], et_ref[pl.ds(k0, tk), :]], axis=1)
        s_ref[:, 0:tk] = _dot_t(qc_ref[...], kcat)
        softmax_pass(tk, k0, "causal" if diagonal else None, True)
        pv(as_ref, vs_ref[pl.ds(k0, tk), :], tk, True)

    c_last = q0 >> int(math.log2(tk))

    def full_chunk(c, carry):
        slc_chunk(c, False)
        return carry

    lax.fori_loop(0, c_last, full_chunk, 0)
    slc_chunk(c_last, True)

    w0 = pl.multiple_of(jnp.maximum(q0 - WINDOW, 0), LANES)
    s_ref[:, 0:wk] = _dot_t(q8, kw_ref[pl.ds(w0, wk), :])
    softmax_pass(wk, w0, "window", False)
    pv(aw_ref, vw_ref[pl.ds(w0, wk), :], wk, False)

    lo_half = lane < HEAD_DIM
    for j in range(nh // 2):
        comb = []
        for a in (2 * j, 2 * j + 1):
            rs = slice(a * tq, (a + 1) * tq)
            o_s, o_w = as_ref[rs, :], aw_ref[rs, :]
            comb.append(g_ref[:, 3 * a:3 * a + 1] * oc_ref[rs, :]
                        + g_ref[:, 3 * a + 1:3 * a + 2] * (o_s * (1.0 / pltpu.roll(o_s, HEAD_DIM, 1)))
                        + g_ref[:, 3 * a + 2:3 * a + 3] * (o_w * (1.0 / pltpu.roll(o_w, HEAD_DIM, 1))))
        even, odd = comb
        if j // (NSA_GROUP // 2) == 0:
            pair = jnp.where(lo_half, even, pltpu.roll(odd, HEAD_DIM, 1))
        else:
            pair = jnp.where(lo_half, pltpu.roll(even, HEAD_DIM, 1), odd)
        o_ref[:, LANES * j:LANES * (j + 1)] = pair.astype(BF16)


def _pack_w_in(w):
    sizes = (NSA_WIDTH,) + (KV_WIDTH,) * 6 + (3 * NSA_HEADS, S5_WIDTH, D_MODEL, D_MODEL)
    o = np.cumsum((0,) + sizes)
    q, kc, vc, ks, vs, kw, vw, g, u, ga, gs = [w[:, o[i]:o[i + 1]] for i in range(len(sizes))]
    g = jnp.pad(g, ((0, 0), (0, LANES - 3 * NSA_HEADS)))
    w1 = jnp.concatenate([q, ks, kw, kc, vc, vs, vw, u, g], axis=1).astype(BF16)
    return w1, ga.astype(BF16), gs.astype(BF16)


def _nsa_branch(x2, gain, w1, ctab, stab, ccmp, scmp, q_gain, k_gain, cmp_k, cmp_v, batch, seq):
    qg = jnp.tile(q_gain, 2)[None, :]
    kg = jnp.tile(k_gain, 2)[None, :]
    q, ks, kw, kc_raw, vc_raw, vs, vw, u_tm, gates = _in_proj(x2, gain, w1, ctab, stab, qg, kg, batch, seq)
    half_rows = batch * seq // (CMP_BLOCK // 2)
    hk = kc_raw.reshape(half_rows, (CMP_BLOCK // 2) * KV_WIDTH)
    hv = vc_raw.reshape(half_rows, (CMP_BLOCK // 2) * KV_WIDTH)
    wk, pk, w2k = _compress_weights(*cmp_k)
    wv, pv, w2v = _compress_weights(*cmp_v)
    kc, vc = _compress(hk, hv, wk, wv, pk, pv, w2k, w2v, kg, ccmp, scmp, batch)
    o_a = _attention(q, ks, vs, kw, vw, kc, vc, gates, batch, seq)
    return o_a, u_tm


def _cmp_rope_tables(ctab, stab, batch, seq):
    def pick(t):
        t = t.reshape(batch, seq, LANES)[:, CMP_BLOCK - 1::CMP_STRIDE]
        t = jnp.pad(t, ((0, 0), (0, seq // CMP_STRIDE - t.shape[1]), (0, 0)))
        return t.reshape(batch * (seq // CMP_STRIDE), LANES)
    return pick(ctab), pick(stab)


def _s5_disc_kernel(are_ref, aim_ref, ldt_ref, lre_ref, lim_ref, fre_ref, fim_ref):
    lam_re = jnp.minimum(are_ref[...], -1e-4)
    lam_im = aim_ref[...]
    dt = jnp.exp(ldt_ref[...])
    mag = jnp.exp(lam_re * dt)
    ang = lam_im * dt
    lb_re = mag * jnp.cos(ang)
    lb_im = mag * jnp.sin(ang)
    den = lam_re * lam_re + lam_im * lam_im
    lre_ref[...] = lb_re
    lim_ref[...] = lb_im
    fre_ref[...] = ((lb_re - 1.0) * lam_re + lb_im * lam_im) / den
    fim_ref[...] = (lb_im * lam_re - (lb_re - 1.0) * lam_im) / den


def _s5_weights(a_re, a_im, log_dt, b_re, b_im, c_re, c_im):
    shp = jax.ShapeDtypeStruct((S5_GROUPS, S5_STATE), F32)
    lb_re, lb_im, f_re, f_im = pl.pallas_call(
        _s5_disc_kernel, out_shape=[shp] * 4, name="s5_discretise",
    )(a_re, a_im, log_dt[:, None])
    bb_re = f_re[..., None] * b_re - f_im[..., None] * b_im
    bb_im = f_re[..., None] * b_im + f_im[..., None] * b_re
    eye = jnp.eye(S5_GROUPS, dtype=F32)

    def in_map(bb):
        return jnp.einsum("gpc,gh->gchp", bb, eye).reshape(S5_WIDTH, S5_NSTATE)

    def out_map(cc):
        return jnp.einsum("gcp,gh->hpgc", cc, eye).reshape(S5_NSTATE, S5_WIDTH)

    w_b = jnp.concatenate([in_map(bb_re), in_map(bb_im)], axis=1).astype(BF16)
    w_c = jnp.concatenate([out_map(c_re), -out_map(c_im)], axis=0).astype(BF16)
    return lb_re.reshape(1, S5_NSTATE), lb_im.reshape(1, S5_NSTATE), w_b, w_c


def _s5_kernel(u_ref, wb_ref, lre_ref, lim_ref, wc_ref, d_ref, wg_ref, o_ref, x_ref, st_ref, *, batch):
    ns = S5_NSTATE

    @pl.when(pl.program_id(0) == 0)
    def _():
        st_ref[...] = jnp.zeros(st_ref.shape, F32)

    u = u_ref[...]
    x_ref[...] = _dot(u.astype(BF16), wb_ref[...])
    lre = jnp.broadcast_to(lre_ref[...], (batch, ns))
    lim = jnp.broadcast_to(lim_ref[...], (batch, ns))

    def step(t, carry):
        xr, xi = carry
        r0 = pl.multiple_of(t * batch, batch)
        nr = lre * xr - lim * xi + x_ref[pl.ds(r0, batch), 0:ns]
        ni = lre * xi + lim * xr + x_ref[pl.ds(r0, batch), ns:2 * ns]
        x_ref[pl.ds(r0, batch), 0:ns] = nr
        x_ref[pl.ds(r0, batch), ns:2 * ns] = ni
        return nr, ni

    steps = u.shape[0] // batch
    xr, xi = lax.fori_loop(0, steps, step, (st_ref[:, 0:ns], st_ref[:, ns:2 * ns]), unroll=4)
    st_ref[:, 0:ns] = xr
    st_ref[:, ns:2 * ns] = xi
    y = _dot(x_ref[...].astype(BF16), wc_ref[...]) + d_ref[...] * u
    z = _dot(_gelu(y).astype(BF16), wg_ref[...])
    o_ref[...] = (z[:, :S5_WIDTH] * _sigmoid(z[:, S5_WIDTH:])).astype(BF16)


def _s5(u_tm, lre, lim, w_b, w_c, d_skip, w_glu, batch, seq):
    rows = S5_T * batch
    const = lambda t: (0, 0)
    return pl.pallas_call(
        functools.partial(_s5_kernel, batch=batch),
        grid=(seq // S5_T,),
        in_specs=[pl.BlockSpec((rows, S5_WIDTH), lambda t: (t, 0)),
                  pl.BlockSpec(w_b.shape, const), pl.BlockSpec(lre.shape, const), pl.BlockSpec(lim.shape, const),
                  pl.BlockSpec(w_c.shape, const), pl.BlockSpec((1, S5_WIDTH), const),
                  pl.BlockSpec(w_glu.shape, const)],
        out_specs=pl.BlockSpec((rows, S5_WIDTH), lambda t: (t, 0)),
        out_shape=jax.ShapeDtypeStruct((seq * batch, S5_WIDTH), BF16),
        scratch_shapes=[pltpu.VMEM((rows, 2 * S5_NSTATE), F32), pltpu.VMEM((batch, 2 * S5_NSTATE), F32)],
        compiler_params=_cparams("arbitrary"),
        name="s5_scan",
    )(u_tm, w_b, lre, lim, w_c, d_skip, w_glu)


RT_GROUP_ROW = 0
RT_EXPERT_ROW = SUBLANES


def _rmsnorm(x, gain):
    return (x * lax.rsqrt(jnp.mean(x * x, axis=-1, keepdims=True) + RMS_EPS)) * gain


def _merge_kernel(x_ref, g1_ref, wga_ref, wgs_ref, oa_ref, ob_ref, pa_ref, ps_ref, wo_ref, g2_ref,
                  wrh_ref, wrl_ref, rb_ref, x1_ref, h2_ref, lg_ref):
    x = x_ref[...]
    h = _rmsnorm(x, g1_ref[...]).astype(BF16)
    gate_a = _sigmoid(_dot(h, wga_ref[...]))
    gate_s = _sigmoid(_dot(h, wgs_ref[...]))
    merged = gate_a * _dot(oa_ref[...], pa_ref[...]) + gate_s * _dot(ob_ref[...], ps_ref[...])
    x1 = x + _dot(merged.astype(BF16), wo_ref[...])
    x1_ref[...] = x1
    h2 = _rmsnorm(x1, g2_ref[...])
    h2_ref[...] = h2
    hi, lo = _split_bf16(h2)
    lg_ref[...] = (_dot_t(wrh_ref[...], hi) + _dot_t(wrh_ref[...], lo) + _dot_t(wrl_ref[...], hi)
                   + rb_ref[...])


def _router_weights(w_rg, b_rg, w_re, b_re):
    def rows(group_part, expert_part):
        return jnp.concatenate([
            group_part, jnp.zeros((RT_EXPERT_ROW - N_EXPERT_GROUPS,) + group_part.shape[1:], F32),
            expert_part, jnp.zeros((LANES - RT_EXPERT_ROW - N_EXPERTS,) + group_part.shape[1:], F32)], axis=0)

    hi, lo = _split_bf16(rows(w_rg.T, w_re.T))
    return hi, lo, rows(b_rg[:, None], b_re[:, None])


def _merge(x2, g1, w_ga, w_gs, o_a, ob_tm, p_a, p_s, w_o, g2, wrh, wrl, rb, batch, seq):
    n = x2.shape[0]
    tm = MRG_TM
    nt = seq // tm
    row = lambda b, i: (b * nt + i, 0)
    const = lambda b, i: (0, 0)
    tok = lambda w: pl.BlockSpec((tm, w), row)
    full = lambda a: pl.BlockSpec(a.shape, const)
    ob_v = ob_tm.reshape(seq, batch * S5_WIDTH)
    return pl.pallas_call(
        _merge_kernel,
        grid=(batch, nt),
        in_specs=[tok(D_MODEL), full(g1), full(w_ga), full(w_gs), tok(NSA_WIDTH),
                  pl.BlockSpec((tm, S5_WIDTH), lambda b, i: (i, b)), full(p_a), full(p_s), full(w_o),
                  full(g2), full(wrh), full(wrl), full(rb)],
        out_specs=[tok(D_MODEL), tok(D_MODEL), pl.BlockSpec((LANES, tm), lambda b, i: (0, b * nt + i))],
        out_shape=[jax.ShapeDtypeStruct((n, D_MODEL), F32), jax.ShapeDtypeStruct((n, D_MODEL), F32),
                   jax.ShapeDtypeStruct((LANES, n), F32)],
        compiler_params=_cparams("parallel", "parallel"),
        name="merge_out_proj",
    )(x2, g1, w_ga, w_gs, o_a, ob_v, p_a, p_s, w_o, g2, wrh, wrl, rb)


def _first_argmax(v, rows):
    vmax = jnp.max(v, axis=0, keepdims=True)
    idx = jnp.min(jnp.where(v == vmax, rows, SUBLANES), axis=0, keepdims=True)
    return vmax, idx


def _route_kernel(lg_ref, tri_ref, e_ref, w_ref, r_ref, cnt_ref, carry_ref):
    tt = lg_ref.shape[1]

    @pl.when(pl.program_id(0) == 0)
    def _():
        carry_ref[...] = jnp.zeros(carry_ref.shape, F32)

    rows = lax.broadcasted_iota(jnp.int32, (SUBLANES, tt), 0)
    gl = jnp.where(rows < N_EXPERT_GROUPS, lg_ref[RT_GROUP_ROW:RT_GROUP_ROW + SUBLANES, :], -jnp.inf)
    gmax, g_sel = _first_argmax(gl, rows)
    g_w = 1.0 / jnp.sum(jnp.exp(gl - gmax), axis=0, keepdims=True)

    el = jnp.zeros((EXPERTS_PER_GROUP, tt), F32)
    for g in range(N_EXPERT_GROUPS):
        r0 = RT_EXPERT_ROW + EXPERTS_PER_GROUP * g
        el = jnp.where(g_sel == g, lg_ref[r0:r0 + EXPERTS_PER_GROUP, :], el)
    ex = jnp.exp(el - jnp.max(el, axis=0, keepdims=True))
    p = ex / jnp.sum(ex, axis=0, keepdims=True)
    p1, i1 = _first_argmax(p, rows)
    p2, i2 = _first_argmax(jnp.where(rows == i1, -1.0, p), rows)
    e1 = g_sel * EXPERTS_PER_GROUP + i1
    e2 = g_sel * EXPERTS_PER_GROUP + i2
    scale = g_w / (p1 + p2)

    erow = lax.broadcasted_iota(jnp.int32, (N_EXPERTS, tt), 0)
    oh1 = jnp.where(erow == e1, 1.0, 0.0)
    oh2 = jnp.where(erow == e2, 1.0, 0.0)
    both = oh1 + oh2
    before = carry_ref[:, 0:1] + _dot(both.astype(BF16), tri_ref[...])
    carry_ref[...] = carry_ref[...] + jnp.sum(both, axis=1, keepdims=True)
    cnt_ref[...] = carry_ref[...]

    zi = jnp.zeros((SUBLANES - 2, tt), jnp.int32)
    e_ref[...] = jnp.concatenate([e1, e2, zi], axis=0)
    w_ref[...] = jnp.concatenate([p1 * scale, p2 * scale, zi.astype(F32)], axis=0)
    r_ref[...] = jnp.concatenate([jnp.sum(oh1 * before, axis=0, keepdims=True),
                                  jnp.sum(oh2 * before, axis=0, keepdims=True),
                                  zi.astype(F32)], axis=0).astype(jnp.int32)


def _route(logits_t):
    n = logits_t.shape[1]
    tt = RT_TT
    tri = jnp.asarray(np.triu(np.ones((tt, tt), np.float32), 1), BF16)
    blk = pl.BlockSpec((SUBLANES, tt), lambda i: (0, i))
    return pl.pallas_call(
        _route_kernel,
        grid=(n // tt,),
        in_specs=[pl.BlockSpec((LANES, tt), lambda i: (0, i)), pl.BlockSpec((tt, tt), lambda i: (0, 0))],
        out_specs=[blk, blk, blk, pl.BlockSpec((N_EXPERTS, LANES), lambda i: (0, 0))],
        out_shape=[jax.ShapeDtypeStruct((SUBLANES, n), jnp.int32), jax.ShapeDtypeStruct((SUBLANES, n), F32),
                   jax.ShapeDtypeStruct((SUBLANES, n), jnp.int32), jax.ShapeDtypeStruct((N_EXPERTS, LANES), F32)],
        scratch_shapes=[pltpu.VMEM((N_EXPERTS, LANES), F32)],
        compiler_params=_cparams("arbitrary"),
        name="moe_route",
    )(logits_t, tri)


def _dispatch_kernel(dest_ref, h_ref, zero_ref, buf_ref, sem):
    del zero_ref
    tt = dest_ref.shape[1]

    def row_copy(t, k):
        return pltpu.make_async_copy(h_ref.at[pl.ds(t, 1)], buf_ref.at[pl.ds(dest_ref[k, t], 1)], sem)

    def issue(t, c):
        for k in range(EXPERT_TOP_K):
            row_copy(t, k).start()
        return c

    def drain(t, c):
        for k in range(EXPERT_TOP_K):
            row_copy(t, k).wait()
        return c

    lax.fori_loop(0, tt, issue, 0, unroll=8)
    lax.fori_loop(0, tt, drain, 0, unroll=8)


def _dispatch(dest, h2, cap):
    n = h2.shape[0]
    tt = DSP_TT
    return pl.pallas_call(
        _dispatch_kernel,
        grid=(n // tt,),
        in_specs=[pl.BlockSpec((EXPERT_TOP_K, tt), lambda i: (0, i), memory_space=pltpu.SMEM),
                  pl.BlockSpec((tt, D_MODEL), lambda i: (i, 0)), pl.BlockSpec(memory_space=pl.ANY)],
        out_specs=pl.BlockSpec(memory_space=pl.ANY),
        out_shape=jax.ShapeDtypeStruct((cap, D_MODEL), h2.dtype),
        scratch_shapes=[pltpu.SemaphoreType.DMA],
        input_output_aliases={2: 0},
        compiler_params=_cparams("arbitrary"),
        name="moe_dispatch",
    )(dest, h2, jnp.zeros((cap, D_MODEL), h2.dtype))


def _expert_kernel(be_ref, nb_ref, x_ref, wg_ref, wu_ref, wd_ref, o_ref, wgb_ref, wub_ref, wdb_ref):
    i = pl.program_id(0)

    @pl.when(i < nb_ref[0])
    def _():
        @pl.when((i == 0) | (be_ref[i] != be_ref[jnp.maximum(i - 1, 0)]))
        def _():
            wgb_ref[...] = wg_ref[...].astype(BF16)
            wub_ref[...] = wu_ref[...].astype(BF16)
            wdb_ref[...] = wd_ref[...].astype(BF16)

        x = x_ref[...].astype(BF16)
        g = _dot(x, wgb_ref[...])
        u = _dot(x, wub_ref[...])
        o_ref[...] = _dot(((g * _sigmoid(g)) * u).astype(BF16), wdb_ref[...])

    @pl.when(i >= nb_ref[0])
    def _():
        o_ref[...] = jnp.zeros(o_ref.shape, o_ref.dtype)


def _experts(blk_expert, n_used, buf, w_gate, w_up, w_down, layer):
    cap = buf.shape[0]
    blk = MOE_BLK
    row = lambda i, be, nb: (jnp.minimum(i, nb[0] - 1), 0)
    wmap = lambda i, be, nb: (layer, be[jnp.minimum(i, nb[0] - 1)], 0, 0)
    return pl.pallas_call(
        _expert_kernel,
        grid_spec=pltpu.PrefetchScalarGridSpec(
            num_scalar_prefetch=2,
            grid=(cap // blk,),
            in_specs=[pl.BlockSpec((blk, D_MODEL), row),
                      pl.BlockSpec((None, None, D_MODEL, EXPERT_FF), wmap),
                      pl.BlockSpec((None, None, D_MODEL, EXPERT_FF), wmap),
                      pl.BlockSpec((None, None, EXPERT_FF, D_MODEL), wmap)],
            out_specs=pl.BlockSpec((blk, D_MODEL), lambda i, be, nb: (i, 0)),
            scratch_shapes=[pltpu.VMEM((D_MODEL, EXPERT_FF), BF16), pltpu.VMEM((D_MODEL, EXPERT_FF), BF16),
                            pltpu.VMEM((EXPERT_FF, D_MODEL), BF16)],
        ),
        out_shape=jax.ShapeDtypeStruct((cap, D_MODEL), F32),
        compiler_params=_cparams("arbitrary"),
        name="moe_experts",
    )(blk_expert, n_used, buf, w_gate, w_up, w_down)


def _combine_kernel(dest_ref, x1_ref, w_ref, ob_ref, o_ref, rows_ref, sem):
    tt = dest_ref.shape[1]

    def row_copy(t, k):
        return pltpu.make_async_copy(ob_ref.at[pl.ds(dest_ref[k, t], 1)], rows_ref.at[k, pl.ds(t, 1)], sem)

    def issue(t, c):
        for k in range(EXPERT_TOP_K):
            row_copy(t, k).start()
        return c

    def drain(t, c):
        for k in range(EXPERT_TOP_K):
            row_copy(t, k).wait()
        return c

    lax.fori_loop(0, tt, issue, 0, unroll=8)
    lax.fori_loop(0, tt, drain, 0, unroll=8)
    o_ref[...] = x1_ref[...] + w_ref[:, 0:1] * rows_ref[0] + w_ref[:, 1:2] * rows_ref[1]


def _combine(dest, x1, w_col, out_buf):
    n = x1.shape[0]
    tt = DSP_TT
    return pl.pallas_call(
        _combine_kernel,
        grid=(n // tt,),
        in_specs=[pl.BlockSpec((EXPERT_TOP_K, tt), lambda i: (0, i), memory_space=pltpu.SMEM),
                  pl.BlockSpec((tt, D_MODEL), lambda i: (i, 0)),
                  pl.BlockSpec((tt, EXPERT_TOP_K), lambda i: (i, 0)),
                  pl.BlockSpec(memory_space=pl.ANY)],
        out_specs=pl.BlockSpec((tt, D_MODEL), lambda i: (i, 0)),
        out_shape=jax.ShapeDtypeStruct((n, D_MODEL), F32),
        scratch_shapes=[pltpu.VMEM((EXPERT_TOP_K, tt, D_MODEL), out_buf.dtype), pltpu.SemaphoreType.DMA],
        compiler_params=_cparams("arbitrary"),
        name="moe_combine",
    )(dest, x1, w_col, out_buf)


def _moe(x1, h2, logits_t, w_gate, w_up, w_down, layer):
    n = x1.shape[0]
    e_idx, wts, rank, counts = _route(logits_t)
    counts = counts[:, 0].astype(jnp.int32)
    padded = (counts + MOE_BLK - 1) // MOE_BLK * MOE_BLK
    pends = jnp.cumsum(padded)
    pstarts = pends - padded
    experts = jnp.arange(N_EXPERTS, dtype=jnp.int32)
    start_of = jnp.sum(jnp.where(e_idx[None, :EXPERT_TOP_K] == experts[:, None, None],
                                 pstarts[:, None, None], 0), axis=0)
    dest = start_of + rank[:EXPERT_TOP_K]
    cap = n * EXPERT_TOP_K + N_EXPERTS * MOE_BLK
    n_blk = cap // MOE_BLK
    blk_row = jnp.arange(n_blk, dtype=jnp.int32) * MOE_BLK
    blk_expert = jnp.minimum(jnp.sum((pends[None, :] <= blk_row[:, None]).astype(jnp.int32), axis=1),
                             N_EXPERTS - 1)
    n_used = (pends[-1:] // MOE_BLK).astype(jnp.int32)
    buf = _dispatch(dest, h2, cap)
    out_buf = _experts(blk_expert, n_used, buf, w_gate, w_up, w_down, layer)
    return _combine(dest, x1, wts[:EXPERT_TOP_K].T, out_buf)


def kernel(x, positions, norm_mix, norm_ffn, w_in, q_gain, k_gain, cmp_k_pos, cmp_k_w1, cmp_k_w2, cmp_v_pos,
           cmp_v_w1, cmp_v_w2, s5_a_re, s5_a_im, s5_log_dt, s5_b_re, s5_b_im, s5_c_re, s5_c_im, s5_d, s5_glu,
           proj_nsa, proj_s5, w_out, router_group, router_group_bias, router_expert, router_expert_bias,
           expert_gate, expert_up, expert_down):
    batch, seq, _ = x.shape
    depth = w_in.shape[0]
    n = batch * seq
    inv_freq = ROPE_THETA ** (-jnp.arange(0, ROPE_DIMS, 2, dtype=jnp.float32) / ROPE_DIMS)
    ctab, stab = _rope_tables(positions.reshape(n, 1), inv_freq)
    ccmp, scmp = _cmp_rope_tables(ctab, stab, batch, seq)
    x2 = x.reshape(n, D_MODEL)
    for l in range(depth):
        w1, w_ga, w_gs = _pack_w_in(w_in[l])
        o_a, u_tm = _nsa_branch(x2, norm_mix[l][None], w1, ctab, stab, ccmp, scmp, q_gain[l], k_gain[l],
                                (cmp_k_pos[l], cmp_k_w1[l], cmp_k_w2[l]),
                                (cmp_v_pos[l], cmp_v_w1[l], cmp_v_w2[l]), batch, seq)
        lre, lim, w_b, w_c = _s5_weights(s5_a_re[l], s5_a_im[l], s5_log_dt[l], s5_b_re[l], s5_b_im[l],
                                         s5_c_re[l], s5_c_im[l])
        ob_tm = _s5(u_tm.reshape(seq * batch, S5_WIDTH), lre, lim, w_b, w_c, s5_d[l][None],
                    s5_glu[l].astype(BF16), batch, seq)
        wrh, wrl, rb = _router_weights(router_group[l], router_group_bias[l], router_expert[l],
                                       router_expert_bias[l])
        x1, h2, logits_t = _merge(x2, norm_mix[l][None], w_ga, w_gs, o_a, ob_tm, proj_nsa[l].astype(BF16),
                                  proj_s5[l].astype(BF16), w_out[l].astype(BF16), norm_ffn[l][None],
                                  wrh, wrl, rb, batch, seq)
        x2 = _moe(x1, h2, logits_t, expert_gate, expert_up, expert_down, l)
    return x2.reshape(batch, seq, D_MODEL)


def _attn_tables(seq):
    n_cmp = (seq - CMP_BLOCK) // CMP_STRIDE + 1
    n_slc = seq // SLC_BLOCK
    key_blk = np.arange(seq) // SLC_BLOCK
    expand_t = (key_blk[:, None] == np.arange(LANES)[None, :]).astype(np.float32)
    cmp_start = np.arange(LANES) * CMP_STRIDE
    slc_start = np.arange(LANES) * SLC_BLOCK
    overlap_t = ((cmp_start[None, :] < slc_start[:, None] + SLC_BLOCK)
                 & (cmp_start[None, :] + CMP_BLOCK > slc_start[:, None])
                 & (np.arange(LANES)[None, :] < n_cmp) & (np.arange(LANES)[:, None] < n_slc))
    return (jnp.asarray(expand_t, BF16), jnp.asarray(overlap_t.astype(np.float32), BF16), n_cmp, n_slc)


def _attention(q, ks, vs, kw, vw, kc, vc, gates, batch, seq):
    expand_t, overlap_t, n_cmp, n_slc = _attn_tables(seq)
    tq = ATT_TQ
    nq = seq // tq
    rows = NSA_HEADS * tq
    width = max(ATT_TK, ATT_WK)
    tile = lambda w: pl.BlockSpec((tq, w), lambda b, i: (b * nq + i, 0))
    per_b = lambda r: pl.BlockSpec((r, LANES), lambda b, i: (b, 0))
    f32_rows = pltpu.VMEM((rows, LANES), F32)
    return pl.pallas_call(
        functools.partial(_attn_kernel, n_cmp=n_cmp, n_slc=n_slc, top_k=min(SLC_TOPK, n_slc)),
        grid=(batch, nq),
        in_specs=[tile(2 * NSA_WIDTH), per_b(seq), per_b(seq), per_b(seq), per_b(seq),
                  per_b(LANES), per_b(LANES), tile(LANES),
                  pl.BlockSpec(expand_t.shape, lambda b, i: (0, 0)),
                  pl.BlockSpec(overlap_t.shape, lambda b, i: (0, 0))],
        out_specs=tile(NSA_WIDTH),
        out_shape=jax.ShapeDtypeStruct((batch * seq, NSA_WIDTH), BF16),
        scratch_shapes=[pltpu.VMEM((rows, 2 * LANES), BF16),
                        pltpu.VMEM((rows, width), F32), pltpu.VMEM((rows, width), BF16),
                        f32_rows, f32_rows, f32_rows, f32_rows, f32_rows],
        compiler_params=_cparams("parallel", "parallel"),
        name="nsa_attention",
    )(q, ks, vs, kw, vw, kc, vc, gates, expand_t, overlap_t)
```

```python
import functools
import math

import jax
import jax.numpy as jnp
import numpy as np
from jax import lax
from jax.experimental import pallas as pl
from jax.experimental.pallas import tpu as pltpu

F32 = jnp.float32
BF16 = jnp.bfloat16

D_MODEL = 1024
NSA_HEADS = 8
NSA_KV_HEADS = 2
NSA_GROUP = NSA_HEADS // NSA_KV_HEADS
HEAD_DIM = 64
ROPE_DIMS = HEAD_DIM // 4
ROPE_THETA = 500000.0
CMP_BLOCK = 32
CMP_STRIDE = 16
CMP_HIDDEN = 2 * HEAD_DIM
SLC_BLOCK = 64
SLC_TOPK = 16
WINDOW = 512
NSA_WIDTH = NSA_HEADS * HEAD_DIM
KV_WIDTH = NSA_KV_HEADS * HEAD_DIM
S5_WIDTH = 256
S5_GROUP_CH = 16
S5_GROUPS = S5_WIDTH // S5_GROUP_CH
S5_STATE = 64
S5_NSTATE = S5_GROUPS * S5_STATE
N_EXPERT_GROUPS = 4
EXPERTS_PER_GROUP = 8
N_EXPERTS = N_EXPERT_GROUPS * EXPERTS_PER_GROUP
EXPERT_TOP_K = 2
EXPERT_FF = 512
RMS_EPS = 1e-6
NEG_BIG = -1e30

LANES = 128
SUBLANES = 8
VMEM_LIMIT = 56 * 1024 * 1024

IN_TM = 512
ATT_TQ = 128
ATT_TK = 512
ATT_WK = WINDOW + ATT_TQ
ATT_RB = 64
S5_T = 128
MRG_TM = 512
RT_TT = 512
MOE_BLK = 256
DSP_TT = 1024

C_Q, C_KS, C_KW, C_KC, C_VC, C_VS, C_VW, C_U, C_G = 0, 512, 640, 768, 896, 1024, 1152, 1280, 1536
IN_W = 1664


def _cparams(*sem):
    return pltpu.CompilerParams(dimension_semantics=sem, vmem_limit_bytes=VMEM_LIMIT)


def _gelu(x):
    return 0.5 * x * (1.0 + jnp.tanh(math.sqrt(2.0 / math.pi) * (x + 0.044715 * (x * x * x))))


def _sigmoid(x):
    return 1.0 / (1.0 + jnp.exp(-x))


def _dot(a, b):
    return jnp.dot(a, b, preferred_element_type=F32)


def _dot_t(a, b):
    return lax.dot_general(a, b, (((1,), (1,)), ((), ())), preferred_element_type=F32)


def _split_bf16(x):
    hi = x.astype(BF16)
    lo = (x - hi.astype(F32)).astype(BF16)
    return hi, lo


def _pack_rows(x):
    k = x.shape[1] // 2
    hi = lax.bitcast_convert_type(x[:, :k].astype(jnp.bfloat16).astype(F32), jnp.int32)
    lo = lax.bitcast_convert_type(x[:, k:].astype(jnp.bfloat16).astype(F32), jnp.int32)
    return hi | lax.shift_right_logical(lo, 16)


def _unpack_rows(w):
    a = lax.bitcast_convert_type(w & jnp.int32(-65536), F32)
    b = lax.bitcast_convert_type(lax.shift_left(w, 16), F32)
    return a, b


def _rope_kernel(pos_ref, freq_ref, sign_ref, c_ref, s_ref):
    ang = pos_ref[...].astype(F32) * freq_ref[...]
    c_ref[...] = jnp.cos(ang)
    s_ref[...] = jnp.sin(ang) * sign_ref[...]


def _rope_tables(pos_col, inv_freq):
    n = pos_col.shape[0]
    lane = np.arange(LANES) % HEAD_DIM
    sel = np.zeros((LANES, ROPE_DIMS // 2), np.float32)
    for l in range(LANES):
        if lane[l] < ROPE_DIMS:
            sel[l, lane[l] % (ROPE_DIMS // 2)] = 1.0
    freq = (jnp.asarray(sel) * inv_freq[None, :]).sum(-1)[None, :]
    sign = np.where(lane < ROPE_DIMS // 2, -1.0, 1.0).astype(np.float32)[None, :]
    tm = 1024
    return pl.pallas_call(
        _rope_kernel,
        grid=(n // tm,),
        in_specs=[pl.BlockSpec((tm, 1), lambda i: (i, 0)),
                  pl.BlockSpec((1, LANES), lambda i: (0, 0)),
                  pl.BlockSpec((1, LANES), lambda i: (0, 0))],
        out_specs=[pl.BlockSpec((tm, LANES), lambda i: (i, 0))] * 2,
        out_shape=[jax.ShapeDtypeStruct((n, LANES), F32)] * 2,
        compiler_params=_cparams("parallel"),
        name="rope_tables",
    )(pos_col, freq, jnp.asarray(sign))


def _headnorm_rope(t, gain, c, s):
    lane = lax.broadcasted_iota(jnp.int32, t.shape, 1)
    lo = lane < HEAD_DIM
    sq = t * t
    s_lo = jnp.sum(jnp.where(lo, sq, 0.0), axis=-1, keepdims=True)
    s_hi = jnp.sum(jnp.where(lo, 0.0, sq), axis=-1, keepdims=True)
    ms = jnp.where(lo, s_lo, s_hi) * (1.0 / HEAD_DIM)
    tn = (t * lax.rsqrt(ms + RMS_EPS)) * gain
    half = ROPE_DIMS // 2
    partner = jnp.where((lane & (HEAD_DIM - 1)) < half,
                        pltpu.roll(tn, LANES - half, 1), pltpu.roll(tn, half, 1))
    return tn * c + partner * s


def _in_proj_kernel(x_ref, gain_ref, w_ref, c_ref, s_ref, qg_ref, kg_ref,
                    q_ref, ks_ref, kw_ref, kc_ref, vc_ref, vs_ref, vw_ref, u_ref, g_ref):
    x = x_ref[...]
    ms = jnp.mean(x * x, axis=-1, keepdims=True)
    h = (x * lax.rsqrt(ms + RMS_EPS)) * gain_ref[...]
    z = _dot(h.astype(BF16), w_ref[...])
    c = c_ref[...]
    s = s_ref[...]
    lane = lax.broadcasted_iota(jnp.int32, c.shape, 1)
    lo = lane < HEAD_DIM
    scale = HEAD_DIM ** -0.5
    for j in range(NSA_HEADS // 2):
        r = _headnorm_rope(z[:, C_Q + LANES * j:C_Q + LANES * (j + 1)], qg_ref[...], c, s) * scale
        rr = pltpu.roll(r, HEAD_DIM, 1)
        if j // (NSA_GROUP // 2) == 0:
            even, odd = jnp.where(lo, r, 0.0), jnp.where(lo, rr, 0.0)
        else:
            even, odd = jnp.where(lo, 0.0, rr), jnp.where(lo, 0.0, r)
        q_ref[:, 2 * LANES * j:2 * LANES * j + LANES] = even.astype(BF16)
        q_ref[:, 2 * LANES * j + LANES:2 * LANES * (j + 1)] = odd.astype(BF16)
    ks_ref[...] = _headnorm_rope(z[:, C_KS:C_KS + LANES], kg_ref[...], c, s).astype(BF16)
    kw_ref[...] = _headnorm_rope(z[:, C_KW:C_KW + LANES], kg_ref[...], c, s).astype(BF16)
    kc_ref[...] = z[:, C_KC:C_KC + LANES].astype(BF16)
    vc_ref[...] = z[:, C_VC:C_VC + LANES].astype(BF16)
    vs_ref[...] = z[:, C_VS:C_VS + LANES].astype(BF16)
    vw_ref[...] = z[:, C_VW:C_VW + LANES].astype(BF16)
    u_ref[...] = z[:, C_U:C_U + S5_WIDTH]
    g_ref[...] = _sigmoid(z[:, C_G:C_G + LANES])


def _in_proj(x2, gain, w1, ctab, stab, qg, kg, batch, seq):
    n = x2.shape[0]
    tm = IN_TM
    nt = seq // tm
    row = lambda b, i: (b * nt + i, 0)
    const = lambda b, i: (0, 0)
    tok = lambda w: pl.BlockSpec((tm, w), row)
    outs = [(n, 2 * NSA_WIDTH, BF16)] + [(n, LANES, BF16)] * 6
    return pl.pallas_call(
        _in_proj_kernel,
        grid=(batch, nt),
        in_specs=[tok(D_MODEL), pl.BlockSpec((1, D_MODEL), const), pl.BlockSpec((D_MODEL, IN_W), const),
                  tok(LANES), tok(LANES), pl.BlockSpec((1, LANES), const), pl.BlockSpec((1, LANES), const)],
        out_specs=[tok(2 * NSA_WIDTH)] + [tok(LANES)] * 6
        + [pl.BlockSpec((tm, S5_WIDTH), lambda b, i: (i, b)), tok(LANES)],
        out_shape=[jax.ShapeDtypeStruct(s[:2], s[2]) for s in outs]
        + [jax.ShapeDtypeStruct((seq, batch * S5_WIDTH), F32), jax.ShapeDtypeStruct((n, LANES), F32)],
        compiler_params=_cparams("parallel", "parallel"),
        name="in_proj",
    )(x2, gain, w1, ctab, stab, qg, kg)


def _compress_mlp(h_ref, w_ref, p_ref, w2_ref):
    w = w_ref[...]
    pre = _dot(h_ref[...], w)
    phi, plo = _split_bf16(p_ref[...])
    pb = _dot(phi, w) + _dot(plo, w)
    rows = pre.shape[0]
    out = None
    for h in range(NSA_KV_HEADS):
        top = pre[:, LANES * h:LANES * (h + 1)]
        bot = pre[:, LANES * (2 + h):LANES * (3 + h)]
        bias = pb[0:1, LANES * h:LANES * (h + 1)] + pb[1:2, LANES * (2 + h):LANES * (3 + h)]
        a = _gelu(top + pltpu.roll(bot, rows - 1, 0) + bias)
        o = _dot(a.astype(BF16), w2_ref[h])
        out = o if out is None else out + o
    return out


def _compress_kernel(hk_ref, hv_ref, wk_ref, wv_ref, pk_ref, pv_ref, w2k_ref, w2v_ref,
                     kg_ref, c_ref, s_ref, kc_ref, vc_ref):
    k = _compress_mlp(hk_ref, wk_ref, pk_ref, w2k_ref)
    kc_ref[...] = _headnorm_rope(k, kg_ref[...], c_ref[...], s_ref[...]).astype(BF16)
    vc_ref[...] = _compress_mlp(hv_ref, wv_ref, pv_ref, w2v_ref).astype(BF16)


def _compress(hk, hv, wk, wv, pk, pv, w2k, w2v, kg, ccmp, scmp, batch):
    rows = hk.shape[0] // batch
    width = hk.shape[1]
    const2 = lambda b: (0, 0)
    const3 = lambda b: (0, 0, 0)
    rowb = lambda w: pl.BlockSpec((rows, w), lambda b: (b, 0))
    return pl.pallas_call(
        _compress_kernel,
        grid=(batch,),
        in_specs=[rowb(width), rowb(width),
                  pl.BlockSpec(wk.shape, const2), pl.BlockSpec(wv.shape, const2),
                  pl.BlockSpec(pk.shape, const2), pl.BlockSpec(pv.shape, const2),
                  pl.BlockSpec(w2k.shape, const3), pl.BlockSpec(w2v.shape, const3),
                  pl.BlockSpec((1, LANES), const2), rowb(LANES), rowb(LANES)],
        out_specs=[rowb(LANES), rowb(LANES)],
        out_shape=[jax.ShapeDtypeStruct((hk.shape[0], LANES), BF16)] * 2,
        compiler_params=_cparams("parallel"),
        name="compress_kv",
    )(hk, hv, wk, wv, pk, pv, w2k, w2v, kg, ccmp, scmp)


def _compress_weights(pos_emb, w1, w2):
    half = CMP_BLOCK // 2
    eye = jnp.eye(NSA_KV_HEADS, dtype=F32)
    w1r = w1.reshape(2, half, HEAD_DIM, CMP_HIDDEN)
    wexp = jnp.einsum("sldn,ph->lpdshn", w1r, eye)
    wexp = wexp.reshape(half * KV_WIDTH, 2 * NSA_KV_HEADS * CMP_HIDDEN).astype(BF16)
    pexp = jnp.broadcast_to(pos_emb.reshape(2, half, 1, HEAD_DIM), (2, half, NSA_KV_HEADS, HEAD_DIM))
    pexp = jnp.pad(pexp.reshape(2, half * KV_WIDTH), ((0, SUBLANES - 2), (0, 0)))
    w2e = jnp.einsum("cd,ph->pchd", w2, eye).reshape(NSA_KV_HEADS, CMP_HIDDEN, KV_WIDTH)
    return wexp, pexp, w2e.astype(BF16)


def _attn_kernel(q_ref, ks_ref, vs_ref, kw_ref, vw_ref, kc_ref, vc_ref, g_ref, et_ref, ovt_ref,
                 o_ref, qc_ref, s_ref, p_ref, m_ref, al_ref, oc_ref, as_ref, aw_ref, *, n_cmp, n_slc, top_k):
    tq, tk, wk, rb, nh = ATT_TQ, ATT_TK, ATT_WK, ATT_RB, NSA_HEADS
    rows, grp_rows = nh * tq, NSA_GROUP * tq
    q0 = pl.program_id(1) * tq
    for a in range(nh):
        qc_ref[a * tq:(a + 1) * tq, 0:LANES] = q_ref[:, LANES * a:LANES * (a + 1)]
    q8 = qc_ref[:, 0:LANES]
    lane = lax.broadcasted_iota(jnp.int32, (tq, LANES), 1)
    t_row = q0 + lax.broadcasted_iota(jnp.int32, (tq, LANES), 0)

    s_c = _dot_t(q8, kc_ref[...]).reshape(nh, tq, LANES)
    mask_c = ((lane * CMP_STRIDE + (CMP_BLOCK - 1)) <= t_row) & (lane < n_cmp)
    sm = jnp.where(mask_c[None], s_c, NEG_BIG)
    p = jnp.where(mask_c[None], jnp.exp(sm - jnp.max(sm, axis=-1, keepdims=True)), 0.0)
    l = jnp.sum(p, axis=-1, keepdims=True)
    p_c = p * jnp.where(l > 0.0, 1.0 / l, 0.0)
    oc_ref[...] = _dot(p_c.reshape(rows, LANES).astype(BF16), vc_ref[...])

    blk = lax.broadcasted_iota(jnp.int32, (n_slc, tq), 0)
    cur = (q0 + lax.broadcasted_iota(jnp.int32, (n_slc, tq), 1)) >> int(math.log2(SLC_BLOCK))
    forced = (blk == 0) | (blk == cur) | (blk == cur - 1)
    for h in range(NSA_KV_HEADS):
        ps = p_c[NSA_GROUP * h]
        for g in range(1, NSA_GROUP):
            ps = ps + p_c[NSA_GROUP * h + g]
        hi, lo = _split_bf16(ps)
        imp = (_dot_t(ovt_ref[...], hi) + _dot_t(ovt_ref[...], lo))[0:n_slc]
        imp = jnp.where(forced, jnp.inf, jnp.where(blk > cur, -jnp.inf, imp))
        rank = jnp.zeros((n_slc, tq), F32)
        for i in range(n_slc):
            row = imp[i:i + 1, :]
            rank = rank + jnp.where((row > imp) | ((row == imp) & (blk > i)), 1.0, 0.0)
        bias_t = jnp.where(rank < float(top_k), 0.0, NEG_BIG)
        bias = jnp.concatenate([bias_t, jnp.zeros((LANES - n_slc, tq), F32)], axis=0).T.astype(BF16)
        for g in range(NSA_GROUP):
            a = NSA_GROUP * h + g
            qc_ref[a * tq:(a + 1) * tq, LANES:2 * LANES] = bias

    def softmax_pass(width, k0, mask, online):
        def block(r, carry):
            r0 = r * rb
            s = s_ref[pl.ds(r0, rb), 0:width]
            if mask is not None:
                t = q0 + ((r0 + lax.broadcasted_iota(jnp.int32, (rb, width), 0)) & (tq - 1))
                kpos = k0 + lax.broadcasted_iota(jnp.int32, (rb, width), 1)
                keep = kpos <= t
                if mask == "window":
                    keep = keep & ((t - kpos) < WINDOW)
                s = jnp.where(keep, s, NEG_BIG)
            m_new = jnp.broadcast_to(jnp.max(s, axis=1, keepdims=True), (rb, LANES))
            if online:
                m_old = m_ref[pl.ds(r0, rb), :]
                m_new = jnp.maximum(m_old, m_new)
                al_ref[pl.ds(r0, rb), :] = jnp.exp(m_old - m_new)
                m_ref[pl.ds(r0, rb), :] = m_new
            pp = jnp.exp(s - jnp.concatenate([m_new] * (width // LANES), axis=1))
            p_ref[pl.ds(r0, rb), 0:width] = pp.astype(BF16)
            return carry

        for r in range(rows // rb):
            block(r, 0)

    def pv(acc_ref, v, width, online):
        lane_v = lax.broadcasted_iota(jnp.int32, v.shape, 1)
        for h in range(NSA_KV_HEADS):
            rs = slice(h * grp_rows, (h + 1) * grp_rows)
            own = (lane_v < HEAD_DIM) if h == 0 else (lane_v >= HEAD_DIM)
            upd = _dot(p_ref[rs, 0:width], jnp.where(own, v, jnp.ones_like(v)))
            acc_ref[rs, :] = (al_ref[rs, :] * acc_ref[rs, :] + upd) if online else upd

    m_ref[...] = jnp.full(m_ref.shape, 0.1 * NEG_BIG, F32)
    as_ref[...] = jnp.zeros(as_ref.shape, F32)

    def slc_chunk(c, diagonal):
        k0 = pl.multiple_of(c * tk, tk)
        kcat = jnp.concatenate([ks_ref[pl.ds(k0, tk), :], et_ref[pl.ds(k0, tk), :]], axis=1)
        s_ref[:, 0:tk] = _dot_t(qc_ref[...], kcat)
        softmax_pass(tk, k0, "causal" if diagonal else None, True)
        pv(as_ref, vs_ref[pl.ds(k0, tk), :], tk, True)

    c_last = q0 >> int(math.log2(tk))

    def full_chunk(c, carry):
        slc_chunk(c, False)
        return carry

    lax.fori_loop(0, c_last, full_chunk, 0)
    slc_chunk(c_last, True)

    w0 = pl.multiple_of(jnp.maximum(q0 - WINDOW, 0), LANES)
    s_ref[:, 0:wk] = _dot_t(q8, kw_ref[pl.ds(w0, wk), :])
    softmax_pass(wk, w0, "window", False)
    pv(aw_ref, vw_ref[pl.ds(w0, wk), :], wk, False)

    lo_half = lane < HEAD_DIM
    for j in range(nh // 2):
        comb = []
        for a in (2 * j, 2 * j + 1):
            rs = slice(a * tq, (a + 1) * tq)
            o_s, o_w = as_ref[rs, :], aw_ref[rs, :]
            comb.append(g_ref[:, 3 * a:3 * a + 1] * oc_ref[rs, :]
                        + g_ref[:, 3 * a + 1:3 * a + 2] * (o_s * (1.0 / pltpu.roll(o_s, HEAD_DIM, 1)))
                        + g_ref[:, 3 * a + 2:3 * a + 3] * (o_w * (1.0 / pltpu.roll(o_w, HEAD_DIM, 1))))
        even, odd = comb
        if j // (NSA_GROUP // 2) == 0:
            pair = jnp.where(lo_half, even, pltpu.roll(odd, HEAD_DIM, 1))
        else:
            pair = jnp.where(lo_half, pltpu.roll(even, HEAD_DIM, 1), odd)
        o_ref[:, LANES * j:LANES * (j + 1)] = pair.astype(BF16)


def _pack_w_in(w):
    sizes = (NSA_WIDTH,) + (KV_WIDTH,) * 6 + (3 * NSA_HEADS, S5_WIDTH, D_MODEL, D_MODEL)
    o = np.cumsum((0,) + sizes)
    q, kc, vc, ks, vs, kw, vw, g, u, ga, gs = [w[:, o[i]:o[i + 1]] for i in range(len(sizes))]
    g = jnp.pad(g, ((0, 0), (0, LANES - 3 * NSA_HEADS)))
    w1 = jnp.concatenate([q, ks, kw, kc, vc, vs, vw, u, g], axis=1).astype(BF16)
    return w1, ga.astype(BF16), gs.astype(BF16)


def _nsa_branch(x2, gain, w1, ctab, stab, ccmp, scmp, q_gain, k_gain, cmp_k, cmp_v, batch, seq):
    qg = jnp.tile(q_gain, 2)[None, :]
    kg = jnp.tile(k_gain, 2)[None, :]
    q, ks, kw, kc_raw, vc_raw, vs, vw, u_tm, gates = _in_proj(x2, gain, w1, ctab, stab, qg, kg, batch, seq)
    half_rows = batch * seq // (CMP_BLOCK // 2)
    hk = kc_raw.reshape(half_rows, (CMP_BLOCK // 2) * KV_WIDTH)
    hv = vc_raw.reshape(half_rows, (CMP_BLOCK // 2) * KV_WIDTH)
    wk, pk, w2k = _compress_weights(*cmp_k)
    wv, pv, w2v = _compress_weights(*cmp_v)
    kc, vc = _compress(hk, hv, wk, wv, pk, pv, w2k, w2v, kg, ccmp, scmp, batch)
    o_a = _attention(q, ks, vs, kw, vw, kc, vc, gates, batch, seq)
    return o_a, u_tm


def _cmp_rope_tables(ctab, stab, batch, seq):
    def pick(t):
        t = t.reshape(batch, seq, LANES)[:, CMP_BLOCK - 1::CMP_STRIDE]
        t = jnp.pad(t, ((0, 0), (0, seq // CMP_STRIDE - t.shape[1]), (0, 0)))
        return t.reshape(batch * (seq // CMP_STRIDE), LANES)
    return pick(ctab), pick(stab)


def _s5_disc_kernel(are_ref, aim_ref, ldt_ref, lre_ref, lim_ref, fre_ref, fim_ref):
    lam_re = jnp.minimum(are_ref[...], -1e-4)
    lam_im = aim_ref[...]
    dt = jnp.exp(ldt_ref[...])
    mag = jnp.exp(lam_re * dt)
    ang = lam_im * dt
    lb_re = mag * jnp.cos(ang)
    lb_im = mag * jnp.sin(ang)
    den = lam_re * lam_re + lam_im * lam_im
    lre_ref[...] = lb_re
    lim_ref[...] = lb_im
    fre_ref[...] = ((lb_re - 1.0) * lam_re + lb_im * lam_im) / den
    fim_ref[...] = (lb_im * lam_re - (lb_re - 1.0) * lam_im) / den


def _s5_weights(a_re, a_im, log_dt, b_re, b_im, c_re, c_im):
    shp = jax.ShapeDtypeStruct((S5_GROUPS, S5_STATE), F32)
    lb_re, lb_im, f_re, f_im = pl.pallas_call(
        _s5_disc_kernel, out_shape=[shp] * 4, name="s5_discretise",
    )(a_re, a_im, log_dt[:, None])
    bb_re = f_re[..., None] * b_re - f_im[..., None] * b_im
    bb_im = f_re[..., None] * b_im + f_im[..., None] * b_re
    eye = jnp.eye(S5_GROUPS, dtype=F32)

    def in_map(bb):
        return jnp.einsum("gpc,gh->gchp", bb, eye).reshape(S5_WIDTH, S5_NSTATE)

    def out_map(cc):
        return jnp.einsum("gcp,gh->hpgc", cc, eye).reshape(S5_NSTATE, S5_WIDTH)

    w_b = jnp.concatenate([in_map(bb_re), in_map(bb_im)], axis=1).astype(BF16)
    w_c = jnp.concatenate([out_map(c_re), -out_map(c_im)], axis=0).astype(BF16)
    return lb_re.reshape(1, S5_NSTATE), lb_im.reshape(1, S5_NSTATE), w_b, w_c


def _s5_kernel(u_ref, wb_ref, lre_ref, lim_ref, wc_ref, d_ref, wg_ref, o_ref, x_ref, st_ref, *, batch):
    ns = S5_NSTATE

    @pl.when(pl.program_id(0) == 0)
    def _():
        st_ref[...] = jnp.zeros(st_ref.shape, F32)

    u = u_ref[...]
    x_ref[...] = _dot(u.astype(BF16), wb_ref[...])
    lre = jnp.broadcast_to(lre_ref[...], (batch, ns))
    lim = jnp.broadcast_to(lim_ref[...], (batch, ns))

    def step(t, carry):
        xr, xi = carry
        r0 = pl.multiple_of(t * batch, batch)
        nr = lre * xr - lim * xi + x_ref[pl.ds(r0, batch), 0:ns]
        ni = lre * xi + lim * xr + x_ref[pl.ds(r0, batch), ns:2 * ns]
        x_ref[pl.ds(r0, batch), 0:ns] = nr
        x_ref[pl.ds(r0, batch), ns:2 * ns] = ni
        return nr, ni

    steps = u.shape[0] // batch
    xr, xi = lax.fori_loop(0, steps, step, (st_ref[:, 0:ns], st_ref[:, ns:2 * ns]), unroll=4)
    st_ref[:, 0:ns] = xr
    st_ref[:, ns:2 * ns] = xi
    y = _dot(x_ref[...].astype(BF16), wc_ref[...]) + d_ref[...] * u
    z = _dot(_gelu(y).astype(BF16), wg_ref[...])
    o_ref[...] = (z[:, :S5_WIDTH] * _sigmoid(z[:, S5_WIDTH:])).astype(BF16)


def _s5(u_tm, lre, lim, w_b, w_c, d_skip, w_glu, batch, seq):
    rows = S5_T * batch
    const = lambda t: (0, 0)
    return pl.pallas_call(
        functools.partial(_s5_kernel, batch=batch),
        grid=(seq // S5_T,),
        in_specs=[pl.BlockSpec((rows, S5_WIDTH), lambda t: (t, 0)),
                  pl.BlockSpec(w_b.shape, const), pl.BlockSpec(lre.shape, const), pl.BlockSpec(lim.shape, const),
                  pl.BlockSpec(w_c.shape, const), pl.BlockSpec((1, S5_WIDTH), const),
                  pl.BlockSpec(w_glu.shape, const)],
        out_specs=pl.BlockSpec((rows, S5_WIDTH), lambda t: (t, 0)),
        out_shape=jax.ShapeDtypeStruct((seq * batch, S5_WIDTH), BF16),
        scratch_shapes=[pltpu.VMEM((rows, 2 * S5_NSTATE), F32), pltpu.VMEM((batch, 2 * S5_NSTATE), F32)],
        compiler_params=_cparams("arbitrary"),
        name="s5_scan",
    )(u_tm, w_b, lre, lim, w_c, d_skip, w_glu)


RT_GROUP_ROW = 0
RT_EXPERT_ROW = SUBLANES


def _rmsnorm(x, gain):
    return (x * lax.rsqrt(jnp.mean(x * x, axis=-1, keepdims=True) + RMS_EPS)) * gain


def _merge_kernel(x_ref, g1_ref, wga_ref, wgs_ref, oa_ref, ob_ref, pa_ref, ps_ref, wo_ref, g2_ref,
                  wrh_ref, wrl_ref, rb_ref, x1_ref, h2_ref, lg_ref):
    x = x_ref[...]
    h = _rmsnorm(x, g1_ref[...]).astype(BF16)
    gate_a = _sigmoid(_dot(h, wga_ref[...]))
    gate_s = _sigmoid(_dot(h, wgs_ref[...]))
    merged = gate_a * _dot(oa_ref[...], pa_ref[...]) + gate_s * _dot(ob_ref[...], ps_ref[...])
    x1 = x + _dot(merged.astype(BF16), wo_ref[...])
    x1_ref[...] = x1
    h2 = _rmsnorm(x1, g2_ref[...])
    h2_ref[...] = _pack_rows(h2)
    hi, lo = _split_bf16(h2)
    lg_ref[...] = (_dot_t(wrh_ref[...], hi) + _dot_t(wrh_ref[...], lo) + _dot_t(wrl_ref[...], hi)
                   + rb_ref[...])


def _router_weights(w_rg, b_rg, w_re, b_re):
    def rows(group_part, expert_part):
        return jnp.concatenate([
            group_part, jnp.zeros((RT_EXPERT_ROW - N_EXPERT_GROUPS,) + group_part.shape[1:], F32),
            expert_part, jnp.zeros((LANES - RT_EXPERT_ROW - N_EXPERTS,) + group_part.shape[1:], F32)], axis=0)

    hi, lo = _split_bf16(rows(w_rg.T, w_re.T))
    return hi, lo, rows(b_rg[:, None], b_re[:, None])


def _merge(x2, g1, w_ga, w_gs, o_a, ob_tm, p_a, p_s, w_o, g2, wrh, wrl, rb, batch, seq):
    n = x2.shape[0]
    tm = MRG_TM
    nt = seq // tm
    row = lambda b, i: (b * nt + i, 0)
    const = lambda b, i: (0, 0)
    tok = lambda w: pl.BlockSpec((tm, w), row)
    full = lambda a: pl.BlockSpec(a.shape, const)
    ob_v = ob_tm.reshape(seq, batch * S5_WIDTH)
    return pl.pallas_call(
        _merge_kernel,
        grid=(batch, nt),
        in_specs=[tok(D_MODEL), full(g1), full(w_ga), full(w_gs), tok(NSA_WIDTH),
                  pl.BlockSpec((tm, S5_WIDTH), lambda b, i: (i, b)), full(p_a), full(p_s), full(w_o),
                  full(g2), full(wrh), full(wrl), full(rb)],
        out_specs=[tok(D_MODEL), tok(D_MODEL // 2), pl.BlockSpec((LANES, tm), lambda b, i: (0, b * nt + i))],
        out_shape=[jax.ShapeDtypeStruct((n, D_MODEL), F32), jax.ShapeDtypeStruct((n, D_MODEL // 2), jnp.int32),
                   jax.ShapeDtypeStruct((LANES, n), F32)],
        compiler_params=_cparams("parallel", "parallel"),
        name="merge_out_proj",
    )(x2, g1, w_ga, w_gs, o_a, ob_v, p_a, p_s, w_o, g2, wrh, wrl, rb)


def _first_argmax(v, rows):
    vmax = jnp.max(v, axis=0, keepdims=True)
    idx = jnp.min(jnp.where(v == vmax, rows, SUBLANES), axis=0, keepdims=True)
    return vmax, idx


def _route_kernel(lg_ref, tri_ref, e_ref, w_ref, r_ref, cnt_ref, carry_ref):
    tt = lg_ref.shape[1]

    @pl.when(pl.program_id(0) == 0)
    def _():
        carry_ref[...] = jnp.zeros(carry_ref.shape, F32)

    rows = lax.broadcasted_iota(jnp.int32, (SUBLANES, tt), 0)
    gl = jnp.where(rows < N_EXPERT_GROUPS, lg_ref[RT_GROUP_ROW:RT_GROUP_ROW + SUBLANES, :], -jnp.inf)
    gmax, g_sel = _first_argmax(gl, rows)
    g_w = 1.0 / jnp.sum(jnp.exp(gl - gmax), axis=0, keepdims=True)

    el = jnp.zeros((EXPERTS_PER_GROUP, tt), F32)
    for g in range(N_EXPERT_GROUPS):
        r0 = RT_EXPERT_ROW + EXPERTS_PER_GROUP * g
        el = jnp.where(g_sel == g, lg_ref[r0:r0 + EXPERTS_PER_GROUP, :], el)
    ex = jnp.exp(el - jnp.max(el, axis=0, keepdims=True))
    p = ex / jnp.sum(ex, axis=0, keepdims=True)
    p1, i1 = _first_argmax(p, rows)
    p2, i2 = _first_argmax(jnp.where(rows == i1, -1.0, p), rows)
    e1 = g_sel * EXPERTS_PER_GROUP + i1
    e2 = g_sel * EXPERTS_PER_GROUP + i2
    scale = g_w / (p1 + p2)

    erow = lax.broadcasted_iota(jnp.int32, (N_EXPERTS, tt), 0)
    oh1 = jnp.where(erow == e1, 1.0, 0.0)
    oh2 = jnp.where(erow == e2, 1.0, 0.0)
    both = oh1 + oh2
    before = carry_ref[:, 0:1] + _dot(both.astype(BF16), tri_ref[...])
    carry_ref[...] = carry_ref[...] + jnp.sum(both, axis=1, keepdims=True)
    cnt_ref[...] = carry_ref[...]

    zi = jnp.zeros((SUBLANES - 2, tt), jnp.int32)
    e_ref[...] = jnp.concatenate([e1, e2, zi], axis=0)
    w_ref[...] = jnp.concatenate([p1 * scale, p2 * scale, zi.astype(F32)], axis=0)
    r_ref[...] = jnp.concatenate([jnp.sum(oh1 * before, axis=0, keepdims=True),
                                  jnp.sum(oh2 * before, axis=0, keepdims=True),
                                  zi.astype(F32)], axis=0).astype(jnp.int32)


def _route(logits_t):
    n = logits_t.shape[1]
    tt = RT_TT
    tri = jnp.asarray(np.triu(np.ones((tt, tt), np.float32), 1), BF16)
    blk = pl.BlockSpec((SUBLANES, tt), lambda i: (0, i))
    return pl.pallas_call(
        _route_kernel,
        grid=(n // tt,),
        in_specs=[pl.BlockSpec((LANES, tt), lambda i: (0, i)), pl.BlockSpec((tt, tt), lambda i: (0, 0))],
        out_specs=[blk, blk, blk, pl.BlockSpec((N_EXPERTS, LANES), lambda i: (0, 0))],
        out_shape=[jax.ShapeDtypeStruct((SUBLANES, n), jnp.int32), jax.ShapeDtypeStruct((SUBLANES, n), F32),
                   jax.ShapeDtypeStruct((SUBLANES, n), jnp.int32), jax.ShapeDtypeStruct((N_EXPERTS, LANES), F32)],
        scratch_shapes=[pltpu.VMEM((N_EXPERTS, LANES), F32)],
        compiler_params=_cparams("arbitrary"),
        name="moe_route",
    )(logits_t, tri)


def _dispatch_kernel(d0_ref, d1_ref, h_ref, zero_ref, buf_ref, sem):
    del zero_ref
    tt = h_ref.shape[0]

    def row_copy(t, d_ref):
        return pltpu.make_async_copy(h_ref.at[pl.ds(t, 1)], buf_ref.at[pl.ds(d_ref[t], 1)], sem)

    def issue(t, c):
        row_copy(t, d0_ref).start()
        row_copy(t, d1_ref).start()
        return c

    def drain(t, c):
        row_copy(t, d0_ref).wait()
        row_copy(t, d1_ref).wait()
        return c

    lax.fori_loop(0, tt, issue, 0, unroll=8)
    lax.fori_loop(0, tt, drain, 0, unroll=8)


def _dispatch(dest0, dest1, h2p, cap):
    n, width = h2p.shape
    tt = DSP_TT
    slot = pl.BlockSpec((tt,), lambda i: (i,), memory_space=pltpu.SMEM)
    return pl.pallas_call(
        _dispatch_kernel,
        grid=(n // tt,),
        in_specs=[slot, slot, pl.BlockSpec((tt, width), lambda i: (i, 0)), pl.BlockSpec(memory_space=pl.ANY)],
        out_specs=pl.BlockSpec(memory_space=pl.ANY),
        out_shape=jax.ShapeDtypeStruct((cap, width), h2p.dtype),
        scratch_shapes=[pltpu.SemaphoreType.DMA],
        input_output_aliases={3: 0},
        compiler_params=_cparams("arbitrary"),
        name="moe_dispatch",
    )(dest0, dest1, h2p, jnp.zeros((cap, width), h2p.dtype))


def _expert_kernel(be_ref, nb_ref, x_ref, wg_ref, wu_ref, wd_ref, o_ref, wgb_ref, wub_ref, wdb_ref):
    i = pl.program_id(0)
    half = D_MODEL // 2

    @pl.when(i < nb_ref[0])
    def _():
        @pl.when((i == 0) | (be_ref[i] != be_ref[jnp.maximum(i - 1, 0)]))
        def _():
            wgb_ref[...] = wg_ref[...].astype(BF16)
            wub_ref[...] = wu_ref[...].astype(BF16)
            wdb_ref[...] = wd_ref[...].astype(BF16)

        xa, xb = _unpack_rows(x_ref[...])
        xa, xb = xa.astype(BF16), xb.astype(BF16)
        g = _dot(xa, wgb_ref[0:half, :]) + _dot(xb, wgb_ref[half:, :])
        u = _dot(xa, wub_ref[0:half, :]) + _dot(xb, wub_ref[half:, :])
        o_ref[...] = _pack_rows(_dot(((g * _sigmoid(g)) * u).astype(BF16), wdb_ref[...]))

    @pl.when(i >= nb_ref[0])
    def _():
        o_ref[...] = jnp.zeros(o_ref.shape, o_ref.dtype)


def _experts(blk_expert, n_used, buf, w_gate, w_up, w_down, layer):
    cap, width = buf.shape
    blk = MOE_BLK
    row = lambda i, be, nb: (jnp.minimum(i, nb[0] - 1), 0)
    wmap = lambda i, be, nb: (layer, be[jnp.minimum(i, nb[0] - 1)], 0, 0)
    return pl.pallas_call(
        _expert_kernel,
        grid_spec=pltpu.PrefetchScalarGridSpec(
            num_scalar_prefetch=2,
            grid=(cap // blk,),
            in_specs=[pl.BlockSpec((blk, width), row),
                      pl.BlockSpec((None, None, D_MODEL, EXPERT_FF), wmap),
                      pl.BlockSpec((None, None, D_MODEL, EXPERT_FF), wmap),
                      pl.BlockSpec((None, None, EXPERT_FF, D_MODEL), wmap)],
            out_specs=pl.BlockSpec((blk, width), lambda i, be, nb: (i, 0)),
            scratch_shapes=[pltpu.VMEM((D_MODEL, EXPERT_FF), BF16), pltpu.VMEM((D_MODEL, EXPERT_FF), BF16),
                            pltpu.VMEM((EXPERT_FF, D_MODEL), BF16)],
        ),
        out_shape=jax.ShapeDtypeStruct((cap, width), buf.dtype),
        compiler_params=_cparams("arbitrary"),
        name="moe_experts",
    )(blk_expert, n_used, buf, w_gate, w_up, w_down)


def _combine_kernel(d0_ref, d1_ref, x1_ref, w_ref, ob_ref, o_ref, rows_ref, sem):
    tt, half = x1_ref.shape[0], x1_ref.shape[1] // 2

    def row_copy(t, k, d_ref):
        return pltpu.make_async_copy(ob_ref.at[pl.ds(d_ref[t], 1)], rows_ref.at[k, pl.ds(t, 1)], sem)

    def issue(t, c):
        row_copy(t, 0, d0_ref).start()
        row_copy(t, 1, d1_ref).start()
        return c

    def drain(t, c):
        row_copy(t, 0, d0_ref).wait()
        row_copy(t, 1, d1_ref).wait()
        return c

    lax.fori_loop(0, tt, issue, 0, unroll=8)
    lax.fori_loop(0, tt, drain, 0, unroll=8)
    a0, b0 = _unpack_rows(rows_ref[0])
    a1, b1 = _unpack_rows(rows_ref[1])
    w0, w1 = w_ref[:, 0:1], w_ref[:, 1:2]
    o_ref[:, 0:half] = x1_ref[:, 0:half] + w0 * a0 + w1 * a1
    o_ref[:, half:] = x1_ref[:, half:] + w0 * b0 + w1 * b1


def _combine(dest0, dest1, x1, w_col, out_buf):
    n = x1.shape[0]
    tt = DSP_TT
    slot = pl.BlockSpec((tt,), lambda i: (i,), memory_space=pltpu.SMEM)
    return pl.pallas_call(
        _combine_kernel,
        grid=(n // tt,),
        in_specs=[slot, slot, pl.BlockSpec((tt, D_MODEL), lambda i: (i, 0)),
                  pl.BlockSpec((tt, EXPERT_TOP_K), lambda i: (i, 0)),
                  pl.BlockSpec(memory_space=pl.ANY)],
        out_specs=pl.BlockSpec((tt, D_MODEL), lambda i: (i, 0)),
        out_shape=jax.ShapeDtypeStruct((n, D_MODEL), F32),
        scratch_shapes=[pltpu.VMEM((EXPERT_TOP_K, tt, out_buf.shape[1]), out_buf.dtype),
                        pltpu.SemaphoreType.DMA],
        compiler_params=_cparams("arbitrary"),
        name="moe_combine",
    )(dest0, dest1, x1, w_col, out_buf)


def _moe(x1, h2, logits_t, w_gate, w_up, w_down, layer):
    n = x1.shape[0]
    e_idx, wts, rank, counts = _route(logits_t)
    counts = counts[:, 0].astype(jnp.int32)
    padded = (counts + MOE_BLK - 1) // MOE_BLK * MOE_BLK
    pends = jnp.cumsum(padded)
    pstarts = pends - padded
    experts = jnp.arange(N_EXPERTS, dtype=jnp.int32)
    start_of = jnp.sum(jnp.where(e_idx[None, :EXPERT_TOP_K] == experts[:, None, None],
                                 pstarts[:, None, None], 0), axis=0)
    dest = start_of + rank[:EXPERT_TOP_K]
    cap = n * EXPERT_TOP_K + N_EXPERTS * MOE_BLK
    n_blk = cap // MOE_BLK
    blk_row = jnp.arange(n_blk, dtype=jnp.int32) * MOE_BLK
    blk_expert = jnp.minimum(jnp.sum((pends[None, :] <= blk_row[:, None]).astype(jnp.int32), axis=1),
                             N_EXPERTS - 1)
    n_used = (pends[-1:] // MOE_BLK).astype(jnp.int32)
    buf = _dispatch(dest[0], dest[1], h2, cap)
    out_buf = _experts(blk_expert, n_used, buf, w_gate, w_up, w_down, layer)
    return _combine(dest[0], dest[1], x1, wts[:EXPERT_TOP_K].T, out_buf)


def kernel(x, positions, norm_mix, norm_ffn, w_in, q_gain, k_gain, cmp_k_pos, cmp_k_w1, cmp_k_w2, cmp_v_pos,
           cmp_v_w1, cmp_v_w2, s5_a_re, s5_a_im, s5_log_dt, s5_b_re, s5_b_im, s5_c_re, s5_c_im, s5_d, s5_glu,
           proj_nsa, proj_s5, w_out, router_group, router_group_bias, router_expert, router_expert_bias,
           expert_gate, expert_up, expert_down):
    batch, seq, _ = x.shape
    depth = w_in.shape[0]
    n = batch * seq
    inv_freq = ROPE_THETA ** (-jnp.arange(0, ROPE_DIMS, 2, dtype=jnp.float32) / ROPE_DIMS)
    ctab, stab = _rope_tables(positions.reshape(n, 1), inv_freq)
    ccmp, scmp = _cmp_rope_tables(ctab, stab, batch, seq)
    x2 = x.reshape(n, D_MODEL)
    for l in range(depth):
        w1, w_ga, w_gs = _pack_w_in(w_in[l])
        o_a, u_tm = _nsa_branch(x2, norm_mix[l][None], w1, ctab, stab, ccmp, scmp, q_gain[l], k_gain[l],
                                (cmp_k_pos[l], cmp_k_w1[l], cmp_k_w2[l]),
                                (cmp_v_pos[l], cmp_v_w1[l], cmp_v_w2[l]), batch, seq)
        lre, lim, w_b, w_c = _s5_weights(s5_a_re[l], s5_a_im[l], s5_log_dt[l], s5_b_re[l], s5_b_im[l],
                                         s5_c_re[l], s5_c_im[l])
        ob_tm = _s5(u_tm.reshape(seq * batch, S5_WIDTH), lre, lim, w_b, w_c, s5_d[l][None],
                    s5_glu[l].astype(BF16), batch, seq)
        wrh, wrl, rb = _router_weights(router_group[l], router_group_bias[l], router_expert[l],
                                       router_expert_bias[l])
        x1, h2, logits_t = _merge(x2, norm_mix[l][None], w_ga, w_gs, o_a, ob_tm, proj_nsa[l].astype(BF16),
                                  proj_s5[l].astype(BF16), w_out[l].astype(BF16), norm_ffn[l][None],
                                  wrh, wrl, rb, batch, seq)
        x2 = _moe(x1, h2, logits_t, expert_gate, expert_up, expert_down, l)
    return x2.reshape(batch, seq, D_MODEL)


def _attn_tables(seq):
    n_cmp = (seq - CMP_BLOCK) // CMP_STRIDE + 1
    n_slc = seq // SLC_BLOCK
    key_blk = np.arange(seq) // SLC_BLOCK
    expand_t = (key_blk[:, None] == np.arange(LANES)[None, :]).astype(np.float32)
    cmp_start = np.arange(LANES) * CMP_STRIDE
    slc_start = np.arange(LANES) * SLC_BLOCK
    overlap_t = ((cmp_start[None, :] < slc_start[:, None] + SLC_BLOCK)
                 & (cmp_start[None, :] + CMP_BLOCK > slc_start[:, None])
                 & (np.arange(LANES)[None, :] < n_cmp) & (np.arange(LANES)[:, None] < n_slc))
    return (jnp.asarray(expand_t, BF16), jnp.asarray(overlap_t.astype(np.float32), BF16), n_cmp, n_slc)


def _attention(q, ks, vs, kw, vw, kc, vc, gates, batch, seq):
    expand_t, overlap_t, n_cmp, n_slc = _attn_tables(seq)
    tq = ATT_TQ
    nq = seq // tq
    rows = NSA_HEADS * tq
    width = max(ATT_TK, ATT_WK)
    tile = lambda w: pl.BlockSpec((tq, w), lambda b, i: (b * nq + i, 0))
    per_b = lambda r: pl.BlockSpec((r, LANES), lambda b, i: (b, 0))
    f32_rows = pltpu.VMEM((rows, LANES), F32)
    return pl.pallas_call(
        functools.partial(_attn_kernel, n_cmp=n_cmp, n_slc=n_slc, top_k=min(SLC_TOPK, n_slc)),
        grid=(batch, nq),
        in_specs=[tile(2 * NSA_WIDTH), per_b(seq), per_b(seq), per_b(seq), per_b(seq),
                  per_b(LANES), per_b(LANES), tile(LANES),
                  pl.BlockSpec(expand_t.shape, lambda b, i: (0, 0)),
                  pl.BlockSpec(overlap_t.shape, lambda b, i: (0, 0))],
        out_specs=tile(NSA_WIDTH),
        out_shape=jax.ShapeDtypeStruct((batch * seq, NSA_WIDTH), BF16),
        scratch_shapes=[pltpu.VMEM((rows, 2 * LANES), BF16),
                        pltpu.VMEM((rows, width), F32), pltpu.VMEM((rows, width), BF16),
                        f32_rows, f32_rows, f32_rows, f32_rows, f32_rows],
        compiler_params=_cparams("parallel", "parallel"),
        name="nsa_attention",
    )(q, ks, vs, kw, vw, kc, vc, gates, expand_t, overlap_t)
```

```python
import functools
import math

import jax
import jax.numpy as jnp
import numpy as np
from jax import lax
from jax.experimental import pallas as pl
from jax.experimental.pallas import tpu as pltpu
from jax.experimental.pallas import tpu_sc as plsc

F32 = jnp.float32
BF16 = jnp.bfloat16

D_MODEL = 1024
NSA_HEADS = 8
NSA_KV_HEADS = 2
NSA_GROUP = NSA_HEADS // NSA_KV_HEADS
HEAD_DIM = 64
ROPE_DIMS = HEAD_DIM // 4
ROPE_THETA = 500000.0
CMP_BLOCK = 32
CMP_STRIDE = 16
CMP_HIDDEN = 2 * HEAD_DIM
SLC_BLOCK = 64
SLC_TOPK = 16
WINDOW = 512
NSA_WIDTH = NSA_HEADS * HEAD_DIM
KV_WIDTH = NSA_KV_HEADS * HEAD_DIM
S5_WIDTH = 256
S5_GROUP_CH = 16
S5_GROUPS = S5_WIDTH // S5_GROUP_CH
S5_STATE = 64
S5_NSTATE = S5_GROUPS * S5_STATE
N_EXPERT_GROUPS = 4
EXPERTS_PER_GROUP = 8
N_EXPERTS = N_EXPERT_GROUPS * EXPERTS_PER_GROUP
EXPERT_TOP_K = 2
EXPERT_FF = 512
RMS_EPS = 1e-6
NEG_BIG = -1e30

LANES = 128
SUBLANES = 8
VMEM_LIMIT = 56 * 1024 * 1024

IN_TM = 512
ATT_TQ = 128
ATT_TK = 512
ATT_WK = WINDOW + ATT_TQ
ATT_RB = 64
S5_T = 128
MRG_TM = 512
RT_TT = 512
MOE_BLK = 256
DSP_TT = 1024

C_Q, C_KS, C_KW, C_KC, C_VC, C_VS, C_VW, C_U, C_G = 0, 512, 640, 768, 896, 1024, 1152, 1280, 1536
IN_W = 1664


def _cparams(*sem):
    return pltpu.CompilerParams(dimension_semantics=sem, vmem_limit_bytes=VMEM_LIMIT)


def _gelu(x):
    return 0.5 * x * (1.0 + jnp.tanh(math.sqrt(2.0 / math.pi) * (x + 0.044715 * (x * x * x))))


def _sigmoid(x):
    return 1.0 / (1.0 + jnp.exp(-x))


def _dot(a, b):
    return jnp.dot(a, b, preferred_element_type=F32)


def _dot_t(a, b):
    return lax.dot_general(a, b, (((1,), (1,)), ((), ())), preferred_element_type=F32)


def _split_bf16(x):
    hi = x.astype(BF16)
    lo = (x - hi.astype(F32)).astype(BF16)
    return hi, lo


def _pack_rows(x):
    k = x.shape[1] // 2
    hi = lax.bitcast_convert_type(x[:, :k].astype(jnp.bfloat16).astype(F32), jnp.int32)
    lo = lax.bitcast_convert_type(x[:, k:].astype(jnp.bfloat16).astype(F32), jnp.int32)
    return hi | lax.shift_right_logical(lo, 16)


def _unpack_rows(w):
    a = lax.bitcast_convert_type(w & jnp.int32(-65536), F32)
    b = lax.bitcast_convert_type(lax.shift_left(w, 16), F32)
    return a, b


def _rope_kernel(pos_ref, freq_ref, sign_ref, c_ref, s_ref):
    ang = pos_ref[...].astype(F32) * freq_ref[...]
    c_ref[...] = jnp.cos(ang)
    s_ref[...] = jnp.sin(ang) * sign_ref[...]


def _rope_tables(pos_col, inv_freq):
    n = pos_col.shape[0]
    lane = np.arange(LANES) % HEAD_DIM
    sel = np.zeros((LANES, ROPE_DIMS // 2), np.float32)
    for l in range(LANES):
        if lane[l] < ROPE_DIMS:
            sel[l, lane[l] % (ROPE_DIMS // 2)] = 1.0
    freq = (jnp.asarray(sel) * inv_freq[None, :]).sum(-1)[None, :]
    sign = np.where(lane < ROPE_DIMS // 2, -1.0, 1.0).astype(np.float32)[None, :]
    tm = 1024
    return pl.pallas_call(
        _rope_kernel,
        grid=(n // tm,),
        in_specs=[pl.BlockSpec((tm, 1), lambda i: (i, 0)),
                  pl.BlockSpec((1, LANES), lambda i: (0, 0)),
                  pl.BlockSpec((1, LANES), lambda i: (0, 0))],
        out_specs=[pl.BlockSpec((tm, LANES), lambda i: (i, 0))] * 2,
        out_shape=[jax.ShapeDtypeStruct((n, LANES), F32)] * 2,
        compiler_params=_cparams("parallel"),
        name="rope_tables",
    )(pos_col, freq, jnp.asarray(sign))


def _headnorm_rope(t, gain, c, s):
    lane = lax.broadcasted_iota(jnp.int32, t.shape, 1)
    lo = lane < HEAD_DIM
    sq = t * t
    s_lo = jnp.sum(jnp.where(lo, sq, 0.0), axis=-1, keepdims=True)
    s_hi = jnp.sum(jnp.where(lo, 0.0, sq), axis=-1, keepdims=True)
    ms = jnp.where(lo, s_lo, s_hi) * (1.0 / HEAD_DIM)
    tn = (t * lax.rsqrt(ms + RMS_EPS)) * gain
    half = ROPE_DIMS // 2
    partner = jnp.where((lane & (HEAD_DIM - 1)) < half,
                        pltpu.roll(tn, LANES - half, 1), pltpu.roll(tn, half, 1))
    return tn * c + partner * s


def _in_proj_kernel(x_ref, gain_ref, w_ref, c_ref, s_ref, qg_ref, kg_ref,
                    q_ref, ks_ref, kw_ref, kc_ref, vc_ref, vs_ref, vw_ref, u_ref, g_ref):
    x = x_ref[...]
    ms = jnp.mean(x * x, axis=-1, keepdims=True)
    h = (x * lax.rsqrt(ms + RMS_EPS)) * gain_ref[...]
    z = _dot(h.astype(BF16), w_ref[...])
    c = c_ref[...]
    s = s_ref[...]
    lane = lax.broadcasted_iota(jnp.int32, c.shape, 1)
    lo = lane < HEAD_DIM
    scale = HEAD_DIM ** -0.5 * math.log2(math.e)
    for j in range(NSA_HEADS // 2):
        r = _headnorm_rope(z[:, C_Q + LANES * j:C_Q + LANES * (j + 1)], qg_ref[...], c, s) * scale
        rr = pltpu.roll(r, HEAD_DIM, 1)
        if j // (NSA_GROUP // 2) == 0:
            even, odd = jnp.where(lo, r, 0.0), jnp.where(lo, rr, 0.0)
        else:
            even, odd = jnp.where(lo, 0.0, rr), jnp.where(lo, 0.0, r)
        q_ref[:, 2 * LANES * j:2 * LANES * j + LANES] = even.astype(BF16)
        q_ref[:, 2 * LANES * j + LANES:2 * LANES * (j + 1)] = odd.astype(BF16)
    ks_ref[...] = _headnorm_rope(z[:, C_KS:C_KS + LANES], kg_ref[...], c, s).astype(BF16)
    kw_ref[...] = _headnorm_rope(z[:, C_KW:C_KW + LANES], kg_ref[...], c, s).astype(BF16)
    kc_ref[...] = z[:, C_KC:C_KC + LANES].astype(BF16)
    vc_ref[...] = z[:, C_VC:C_VC + LANES].astype(BF16)
    vs_ref[...] = z[:, C_VS:C_VS + LANES].astype(BF16)
    vw_ref[...] = z[:, C_VW:C_VW + LANES].astype(BF16)
    u_ref[...] = z[:, C_U:C_U + S5_WIDTH]
    g_ref[...] = _sigmoid(z[:, C_G:C_G + LANES])


def _in_proj(x2, gain, w1, ctab, stab, qg, kg, batch, seq):
    n = x2.shape[0]
    tm = IN_TM
    nt = seq // tm
    row = lambda b, i: (b * nt + i, 0)
    const = lambda b, i: (0, 0)
    tok = lambda w: pl.BlockSpec((tm, w), row)
    outs = [(n, 2 * NSA_WIDTH, BF16)] + [(n, LANES, BF16)] * 6
    return pl.pallas_call(
        _in_proj_kernel,
        grid=(batch, nt),
        in_specs=[tok(D_MODEL), pl.BlockSpec((1, D_MODEL), const), pl.BlockSpec((D_MODEL, IN_W), const),
                  tok(LANES), tok(LANES), pl.BlockSpec((1, LANES), const), pl.BlockSpec((1, LANES), const)],
        out_specs=[tok(2 * NSA_WIDTH)] + [tok(LANES)] * 6
        + [pl.BlockSpec((tm, S5_WIDTH), lambda b, i: (i, b)), tok(LANES)],
        out_shape=[jax.ShapeDtypeStruct(s[:2], s[2]) for s in outs]
        + [jax.ShapeDtypeStruct((seq, batch * S5_WIDTH), F32), jax.ShapeDtypeStruct((n, LANES), F32)],
        compiler_params=_cparams("parallel", "parallel"),
        name="in_proj",
    )(x2, gain, w1, ctab, stab, qg, kg)


def _compress_mlp(h_ref, w_ref, p_ref, w2_ref):
    w = w_ref[...]
    pre = _dot(h_ref[...], w)
    phi, plo = _split_bf16(p_ref[...])
    pb = _dot(phi, w) + _dot(plo, w)
    rows = pre.shape[0]
    out = None
    for h in range(NSA_KV_HEADS):
        top = pre[:, LANES * h:LANES * (h + 1)]
        bot = pre[:, LANES * (2 + h):LANES * (3 + h)]
        bias = pb[0:1, LANES * h:LANES * (h + 1)] + pb[1:2, LANES * (2 + h):LANES * (3 + h)]
        a = _gelu(top + pltpu.roll(bot, rows - 1, 0) + bias)
        o = _dot(a.astype(BF16), w2_ref[h])
        out = o if out is None else out + o
    return out


def _compress_kernel(hk_ref, hv_ref, wk_ref, wv_ref, pk_ref, pv_ref, w2k_ref, w2v_ref,
                     kg_ref, c_ref, s_ref, kc_ref, vc_ref):
    k = _compress_mlp(hk_ref, wk_ref, pk_ref, w2k_ref)
    kc_ref[...] = _headnorm_rope(k, kg_ref[...], c_ref[...], s_ref[...]).astype(BF16)
    vc_ref[...] = _compress_mlp(hv_ref, wv_ref, pv_ref, w2v_ref).astype(BF16)


def _compress(hk, hv, wk, wv, pk, pv, w2k, w2v, kg, ccmp, scmp, batch):
    rows = hk.shape[0] // batch
    width = hk.shape[1]
    const2 = lambda b: (0, 0)
    const3 = lambda b: (0, 0, 0)
    rowb = lambda w: pl.BlockSpec((rows, w), lambda b: (b, 0))
    return pl.pallas_call(
        _compress_kernel,
        grid=(batch,),
        in_specs=[rowb(width), rowb(width),
                  pl.BlockSpec(wk.shape, const2), pl.BlockSpec(wv.shape, const2),
                  pl.BlockSpec(pk.shape, const2), pl.BlockSpec(pv.shape, const2),
                  pl.BlockSpec(w2k.shape, const3), pl.BlockSpec(w2v.shape, const3),
                  pl.BlockSpec((1, LANES), const2), rowb(LANES), rowb(LANES)],
        out_specs=[rowb(LANES), rowb(LANES)],
        out_shape=[jax.ShapeDtypeStruct((hk.shape[0], LANES), BF16)] * 2,
        compiler_params=_cparams("parallel"),
        name="compress_kv",
    )(hk, hv, wk, wv, pk, pv, w2k, w2v, kg, ccmp, scmp)


def _compress_weights(pos_emb, w1, w2):
    half = CMP_BLOCK // 2
    eye = jnp.eye(NSA_KV_HEADS, dtype=F32)
    w1r = w1.reshape(2, half, HEAD_DIM, CMP_HIDDEN)
    wexp = jnp.einsum("sldn,ph->lpdshn", w1r, eye)
    wexp = wexp.reshape(half * KV_WIDTH, 2 * NSA_KV_HEADS * CMP_HIDDEN).astype(BF16)
    pexp = jnp.broadcast_to(pos_emb.reshape(2, half, 1, HEAD_DIM), (2, half, NSA_KV_HEADS, HEAD_DIM))
    pexp = jnp.pad(pexp.reshape(2, half * KV_WIDTH), ((0, SUBLANES - 2), (0, 0)))
    w2e = jnp.einsum("cd,ph->pchd", w2, eye).reshape(NSA_KV_HEADS, CMP_HIDDEN, KV_WIDTH)
    return wexp, pexp, w2e.astype(BF16)


def _attn_kernel(q_ref, ks_ref, vs_ref, kw_ref, vw_ref, kc_ref, vc_ref, g_ref, et_ref, ovt_ref, oh_ref, wb_ref,
                 cb_ref, o_ref, qc_ref, s_ref, p_ref, sw_ref, pw_ref, m_ref, al_ref, oc_ref, as_ref, aw_ref,
                 *, n_cmp, n_slc, top_k):
    tq, tk, wk, rb, nh = ATT_TQ, ATT_TK, ATT_WK, ATT_RB, NSA_HEADS
    rows, grp_rows = nh * tq, NSA_GROUP * tq
    q0 = pl.program_id(1) * tq
    for a in range(nh):
        qc_ref[a * tq:(a + 1) * tq, 0:LANES] = q_ref[:, LANES * a:LANES * (a + 1)]
    q8 = qc_ref[:, 0:LANES]
    lane = lax.broadcasted_iota(jnp.int32, (tq, LANES), 1)
    t_row = q0 + lax.broadcasted_iota(jnp.int32, (tq, LANES), 0)

    s_c = _dot_t(q8, kc_ref[...]).reshape(nh, tq, LANES)
    mask_c = ((lane * CMP_STRIDE + (CMP_BLOCK - 1)) <= t_row) & (lane < n_cmp)
    sm = jnp.where(mask_c[None], s_c, NEG_BIG)
    p = jnp.where(mask_c[None], jnp.exp2(sm - jnp.max(sm, axis=-1, keepdims=True)), 0.0)
    l = jnp.sum(p, axis=-1, keepdims=True)
    p_c = p * jnp.where(l > 0.0, 1.0 / l, 0.0)
    oc_ref[...] = _dot(p_c.reshape(rows, LANES).astype(BF16), vc_ref[...])

    blk = lax.broadcasted_iota(jnp.int32, (n_slc, tq), 0)
    cur = (q0 + lax.broadcasted_iota(jnp.int32, (n_slc, tq), 1)) >> int(math.log2(SLC_BLOCK))
    forced = (blk == 0) | (blk == cur) | (blk == cur - 1)
    for h in range(NSA_KV_HEADS):
        ps = p_c[NSA_GROUP * h]
        for g in range(1, NSA_GROUP):
            ps = ps + p_c[NSA_GROUP * h + g]
        hi, lo = _split_bf16(ps)
        imp = (_dot_t(ovt_ref[...], hi) + _dot_t(ovt_ref[...], lo))[0:n_slc]
        imp = jnp.where(forced, jnp.inf, jnp.where(blk > cur, -jnp.inf, imp))
        rank = jnp.zeros((n_slc, tq), F32)
        for i in range(n_slc):
            row = imp[i:i + 1, :]
            rank = rank + jnp.where((row > imp) | ((row == imp) & (blk > i)), 1.0, 0.0)
        bias_t = jnp.where(rank < float(top_k), 0.0, NEG_BIG)
        bias = jnp.concatenate([bias_t, jnp.zeros((LANES - n_slc, tq), F32)], axis=0).T.astype(BF16)
        for g in range(NSA_GROUP):
            a = NSA_GROUP * h + g
            qc_ref[a * tq:(a + 1) * tq, LANES:2 * LANES] = bias

    def softmax_pass(s_buf, p_buf, width, causal_case, online):
        for r in range(rows // rb):
            r0 = r * rb
            s = s_buf[pl.ds(r0, rb), 0:width]
            if causal_case is not None:
                s = s + cb_ref[causal_case, pl.ds(r0 % tq, rb), :]
            m_new = jnp.broadcast_to(jnp.max(s, axis=1, keepdims=True), (rb, LANES))
            if online:
                m_old = m_ref[pl.ds(r0, rb), :]
                m_new = jnp.maximum(m_old, m_new)
                al_ref[pl.ds(r0, rb), :] = jnp.exp2(m_old - m_new)
                m_ref[pl.ds(r0, rb), :] = m_new
            pp = jnp.exp2(s - jnp.concatenate([m_new] * (width // LANES), axis=1))
            p_buf[pl.ds(r0, rb), 0:width] = pp.astype(BF16)

    def pv(acc_ref, p_buf, v, width, online):
        lane_v = lax.broadcasted_iota(jnp.int32, v.shape, 1)
        for h in range(NSA_KV_HEADS):
            rs = slice(h * grp_rows, (h + 1) * grp_rows)
            own = (lane_v < HEAD_DIM) if h == 0 else (lane_v >= HEAD_DIM)
            upd = _dot(p_buf[rs, 0:width], jnp.where(own, v, jnp.ones_like(v)))
            acc_ref[rs, :] = (al_ref[rs, :] * acc_ref[rs, :] + upd) if online else upd

    m_ref[...] = jnp.full(m_ref.shape, 0.1 * NEG_BIG, F32)
    as_ref[...] = jnp.zeros(as_ref.shape, F32)

    def slc_chunk(c, diagonal):
        k0 = pl.multiple_of(c * tk, tk)
        kcat = jnp.concatenate([ks_ref[pl.ds(k0, tk), :], et_ref[pl.ds(k0, tk), :]], axis=1)
        s_ref[...] = _dot_t(qc_ref[...], kcat)
        softmax_pass(s_ref, p_ref, tk, ((q0 - k0) >> int(math.log2(tq))) if diagonal else None, True)
        pv(as_ref, p_ref, vs_ref[pl.ds(k0, tk), :], tk, True)

    c_last = q0 >> int(math.log2(tk))

    def full_chunk(c, carry):
        slc_chunk(c, False)
        return carry

    lax.fori_loop(0, c_last, full_chunk, 0)
    slc_chunk(c_last, True)

    w0 = pl.multiple_of(jnp.maximum(q0 - WINDOW, 0), LANES)
    w_case = jnp.minimum(q0 >> int(math.log2(tq)), WINDOW // tq)
    kcat_w = jnp.concatenate([kw_ref[pl.ds(w0, wk), :], wb_ref[w_case]], axis=1)
    sw_ref[...] = _dot_t(jnp.concatenate([q8, oh_ref[...]], axis=1), kcat_w)
    softmax_pass(sw_ref, pw_ref, wk, None, False)
    pv(aw_ref, pw_ref, vw_ref[pl.ds(w0, wk), :], wk, False)

    lo_half = lane < HEAD_DIM
    for j in range(nh // 2):
        comb = []
        for a in (2 * j, 2 * j + 1):
            rs = slice(a * tq, (a + 1) * tq)
            o_s, o_w = as_ref[rs, :], aw_ref[rs, :]
            comb.append(g_ref[:, 3 * a:3 * a + 1] * oc_ref[rs, :]
                        + g_ref[:, 3 * a + 1:3 * a + 2] * (o_s * (1.0 / pltpu.roll(o_s, HEAD_DIM, 1)))
                        + g_ref[:, 3 * a + 2:3 * a + 3] * (o_w * (1.0 / pltpu.roll(o_w, HEAD_DIM, 1))))
        even, odd = comb
        if j // (NSA_GROUP // 2) == 0:
            pair = jnp.where(lo_half, even, pltpu.roll(odd, HEAD_DIM, 1))
        else:
            pair = jnp.where(lo_half, pltpu.roll(even, HEAD_DIM, 1), odd)
        o_ref[:, LANES * j:LANES * (j + 1)] = pair.astype(BF16)


def _pack_w_in(w):
    sizes = (NSA_WIDTH,) + (KV_WIDTH,) * 6 + (3 * NSA_HEADS, S5_WIDTH, D_MODEL, D_MODEL)
    o = np.cumsum((0,) + sizes)
    q, kc, vc, ks, vs, kw, vw, g, u, ga, gs = [w[:, o[i]:o[i + 1]] for i in range(len(sizes))]
    g = jnp.pad(g, ((0, 0), (0, LANES - 3 * NSA_HEADS)))
    w1 = jnp.concatenate([q, ks, kw, kc, vc, vs, vw, u, g], axis=1).astype(BF16)
    return w1, ga.astype(BF16), gs.astype(BF16)


def _nsa_branch(x2, gain, w1, ctab, stab, ccmp, scmp, q_gain, k_gain, cmp_k, cmp_v, batch, seq):
    qg = jnp.tile(q_gain, 2)[None, :]
    kg = jnp.tile(k_gain, 2)[None, :]
    q, ks, kw, kc_raw, vc_raw, vs, vw, u_tm, gates = _in_proj(x2, gain, w1, ctab, stab, qg, kg, batch, seq)
    half_rows = batch * seq // (CMP_BLOCK // 2)
    hk = kc_raw.reshape(half_rows, (CMP_BLOCK // 2) * KV_WIDTH)
    hv = vc_raw.reshape(half_rows, (CMP_BLOCK // 2) * KV_WIDTH)
    wk, pk, w2k = _compress_weights(*cmp_k)
    wv, pv, w2v = _compress_weights(*cmp_v)
    kc, vc = _compress(hk, hv, wk, wv, pk, pv, w2k, w2v, kg, ccmp, scmp, batch)
    o_a = _attention(q, ks, vs, kw, vw, kc, vc, gates, batch, seq)
    return o_a, u_tm


def _cmp_rope_tables(ctab, stab, batch, seq):
    def pick(t):
        t = t.reshape(batch, seq, LANES)[:, CMP_BLOCK - 1::CMP_STRIDE]
        t = jnp.pad(t, ((0, 0), (0, seq // CMP_STRIDE - t.shape[1]), (0, 0)))
        return t.reshape(batch * (seq // CMP_STRIDE), LANES)
    return pick(ctab), pick(stab)


def _s5_disc_kernel(are_ref, aim_ref, ldt_ref, lre_ref, lim_ref, fre_ref, fim_ref):
    lam_re = jnp.minimum(are_ref[...], -1e-4)
    lam_im = aim_ref[...]
    dt = jnp.exp(ldt_ref[...])
    mag = jnp.exp(lam_re * dt)
    ang = lam_im * dt
    lb_re = mag * jnp.cos(ang)
    lb_im = mag * jnp.sin(ang)
    den = lam_re * lam_re + lam_im * lam_im
    lre_ref[...] = lb_re
    lim_ref[...] = lb_im
    fre_ref[...] = ((lb_re - 1.0) * lam_re + lb_im * lam_im) / den
    fim_ref[...] = (lb_im * lam_re - (lb_re - 1.0) * lam_im) / den


def _s5_weights(a_re, a_im, log_dt, b_re, b_im, c_re, c_im):
    shp = jax.ShapeDtypeStruct((S5_GROUPS, S5_STATE), F32)
    lb_re, lb_im, f_re, f_im = pl.pallas_call(
        _s5_disc_kernel, out_shape=[shp] * 4, name="s5_discretise",
    )(a_re, a_im, log_dt[:, None])
    bb_re = f_re[..., None] * b_re - f_im[..., None] * b_im
    bb_im = f_re[..., None] * b_im + f_im[..., None] * b_re
    eye = jnp.eye(S5_GROUPS, dtype=F32)

    def in_map(bb):
        return jnp.einsum("gpc,gh->gchp", bb, eye).reshape(S5_WIDTH, S5_NSTATE)

    def out_map(cc):
        return jnp.einsum("gcp,gh->hpgc", cc, eye).reshape(S5_NSTATE, S5_WIDTH)

    w_b = jnp.concatenate([in_map(bb_re), in_map(bb_im)], axis=1).astype(BF16)
    w_c = jnp.concatenate([out_map(c_re), -out_map(c_im)], axis=0).astype(BF16)
    return lb_re.reshape(1, S5_NSTATE), lb_im.reshape(1, S5_NSTATE), w_b, w_c


def _s5_kernel(u_ref, wb_ref, lre_ref, lim_ref, wc_ref, d_ref, wg_ref, o_ref, x_ref, st_ref, *, batch):
    ns = S5_NSTATE

    @pl.when(pl.program_id(0) == 0)
    def _():
        st_ref[...] = jnp.zeros(st_ref.shape, F32)

    u = u_ref[...]
    x_ref[...] = _dot(u.astype(BF16), wb_ref[...])
    lre = jnp.broadcast_to(lre_ref[...], (batch, ns))
    lim = jnp.broadcast_to(lim_ref[...], (batch, ns))

    def step(t, carry):
        xr, xi = carry
        r0 = pl.multiple_of(t * batch, batch)
        nr = lre * xr - lim * xi + x_ref[pl.ds(r0, batch), 0:ns]
        ni = lre * xi + lim * xr + x_ref[pl.ds(r0, batch), ns:2 * ns]
        x_ref[pl.ds(r0, batch), 0:ns] = nr
        x_ref[pl.ds(r0, batch), ns:2 * ns] = ni
        return nr, ni

    steps = u.shape[0] // batch
    xr, xi = lax.fori_loop(0, steps, step, (st_ref[:, 0:ns], st_ref[:, ns:2 * ns]), unroll=4)
    st_ref[:, 0:ns] = xr
    st_ref[:, ns:2 * ns] = xi
    y = _dot(x_ref[...].astype(BF16), wc_ref[...]) + d_ref[...] * u
    z = _dot(_gelu(y).astype(BF16), wg_ref[...])
    o_ref[...] = (z[:, :S5_WIDTH] * _sigmoid(z[:, S5_WIDTH:])).astype(BF16)


def _s5(u_tm, lre, lim, w_b, w_c, d_skip, w_glu, batch, seq):
    rows = S5_T * batch
    const = lambda t: (0, 0)
    return pl.pallas_call(
        functools.partial(_s5_kernel, batch=batch),
        grid=(seq // S5_T,),
        in_specs=[pl.BlockSpec((rows, S5_WIDTH), lambda t: (t, 0)),
                  pl.BlockSpec(w_b.shape, const), pl.BlockSpec(lre.shape, const), pl.BlockSpec(lim.shape, const),
                  pl.BlockSpec(w_c.shape, const), pl.BlockSpec((1, S5_WIDTH), const),
                  pl.BlockSpec(w_glu.shape, const)],
        out_specs=pl.BlockSpec((rows, S5_WIDTH), lambda t: (t, 0)),
        out_shape=jax.ShapeDtypeStruct((seq * batch, S5_WIDTH), BF16),
        scratch_shapes=[pltpu.VMEM((rows, 2 * S5_NSTATE), F32), pltpu.VMEM((batch, 2 * S5_NSTATE), F32)],
        compiler_params=_cparams("arbitrary"),
        name="s5_scan",
    )(u_tm, w_b, lre, lim, w_c, d_skip, w_glu)


RT_GROUP_ROW = 0
RT_EXPERT_ROW = SUBLANES


def _rmsnorm(x, gain):
    return (x * lax.rsqrt(jnp.mean(x * x, axis=-1, keepdims=True) + RMS_EPS)) * gain


def _merge_kernel(x_ref, g1_ref, wga_ref, wgs_ref, oa_ref, ob_ref, pa_ref, ps_ref, wo_ref, g2_ref,
                  wrh_ref, wrl_ref, rb_ref, x1_ref, h2_ref, lg_ref):
    x = x_ref[...]
    h = _rmsnorm(x, g1_ref[...]).astype(BF16)
    gate_a = _sigmoid(_dot(h, wga_ref[...]))
    gate_s = _sigmoid(_dot(h, wgs_ref[...]))
    merged = gate_a * _dot(oa_ref[...], pa_ref[...]) + gate_s * _dot(ob_ref[...], ps_ref[...])
    x1 = x + _dot(merged.astype(BF16), wo_ref[...])
    x1_ref[...] = x1
    h2 = _rmsnorm(x1, g2_ref[...])
    h2_ref[...] = _pack_rows(h2)
    hi, lo = _split_bf16(h2)
    lg_ref[...] = (_dot_t(wrh_ref[...], hi) + _dot_t(wrh_ref[...], lo) + _dot_t(wrl_ref[...], hi)
                   + rb_ref[...])


def _router_weights(w_rg, b_rg, w_re, b_re):
    def rows(group_part, expert_part):
        return jnp.concatenate([
            group_part, jnp.zeros((RT_EXPERT_ROW - N_EXPERT_GROUPS,) + group_part.shape[1:], F32),
            expert_part, jnp.zeros((LANES - RT_EXPERT_ROW - N_EXPERTS,) + group_part.shape[1:], F32)], axis=0)

    hi, lo = _split_bf16(rows(w_rg.T, w_re.T))
    return hi, lo, rows(b_rg[:, None], b_re[:, None])


def _merge(x2, g1, w_ga, w_gs, o_a, ob_tm, p_a, p_s, w_o, g2, wrh, wrl, rb, batch, seq):
    n = x2.shape[0]
    tm = MRG_TM
    nt = seq // tm
    row = lambda b, i: (b * nt + i, 0)
    const = lambda b, i: (0, 0)
    tok = lambda w: pl.BlockSpec((tm, w), row)
    full = lambda a: pl.BlockSpec(a.shape, const)
    ob_v = ob_tm.reshape(seq, batch * S5_WIDTH)
    return pl.pallas_call(
        _merge_kernel,
        grid=(batch, nt),
        in_specs=[tok(D_MODEL), full(g1), full(w_ga), full(w_gs), tok(NSA_WIDTH),
                  pl.BlockSpec((tm, S5_WIDTH), lambda b, i: (i, b)), full(p_a), full(p_s), full(w_o),
                  full(g2), full(wrh), full(wrl), full(rb)],
        out_specs=[tok(D_MODEL), tok(D_MODEL // 2), pl.BlockSpec((LANES, tm), lambda b, i: (0, b * nt + i))],
        out_shape=[jax.ShapeDtypeStruct((n, D_MODEL), F32), jax.ShapeDtypeStruct((n, D_MODEL // 2), jnp.int32),
                   jax.ShapeDtypeStruct((LANES, n), F32)],
        compiler_params=_cparams("parallel", "parallel"),
        name="merge_out_proj",
    )(x2, g1, w_ga, w_gs, o_a, ob_v, p_a, p_s, w_o, g2, wrh, wrl, rb)


def _first_argmax(v, rows):
    vmax = jnp.max(v, axis=0, keepdims=True)
    idx = jnp.min(jnp.where(v == vmax, rows, SUBLANES), axis=0, keepdims=True)
    return vmax, idx


def _route_kernel(lg_ref, tri_ref, e_ref, w_ref, r_ref, cnt_ref, carry_ref):
    tt = lg_ref.shape[1]

    @pl.when(pl.program_id(0) == 0)
    def _():
        carry_ref[...] = jnp.zeros(carry_ref.shape, F32)

    rows = lax.broadcasted_iota(jnp.int32, (SUBLANES, tt), 0)
    gl = jnp.where(rows < N_EXPERT_GROUPS, lg_ref[RT_GROUP_ROW:RT_GROUP_ROW + SUBLANES, :], -jnp.inf)
    gmax, g_sel = _first_argmax(gl, rows)
    g_w = 1.0 / jnp.sum(jnp.exp(gl - gmax), axis=0, keepdims=True)

    el = jnp.zeros((EXPERTS_PER_GROUP, tt), F32)
    for g in range(N_EXPERT_GROUPS):
        r0 = RT_EXPERT_ROW + EXPERTS_PER_GROUP * g
        el = jnp.where(g_sel == g, lg_ref[r0:r0 + EXPERTS_PER_GROUP, :], el)
    ex = jnp.exp(el - jnp.max(el, axis=0, keepdims=True))
    p = ex / jnp.sum(ex, axis=0, keepdims=True)
    p1, i1 = _first_argmax(p, rows)
    p2, i2 = _first_argmax(jnp.where(rows == i1, -1.0, p), rows)
    e1 = g_sel * EXPERTS_PER_GROUP + i1
    e2 = g_sel * EXPERTS_PER_GROUP + i2
    scale = g_w / (p1 + p2)

    erow = lax.broadcasted_iota(jnp.int32, (N_EXPERTS, tt), 0)
    oh1 = jnp.where(erow == e1, 1.0, 0.0)
    oh2 = jnp.where(erow == e2, 1.0, 0.0)
    both = oh1 + oh2
    before = carry_ref[:, 0:1] + _dot(both.astype(BF16), tri_ref[...])
    carry_ref[...] = carry_ref[...] + jnp.sum(both, axis=1, keepdims=True)
    cnt_ref[...] = carry_ref[...]

    zi = jnp.zeros((SUBLANES - 2, tt), jnp.int32)
    e_ref[...] = jnp.concatenate([e1, e2, zi], axis=0)
    w_ref[...] = jnp.concatenate([p1 * scale, p2 * scale, zi.astype(F32)], axis=0)
    r_ref[...] = jnp.concatenate([jnp.sum(oh1 * before, axis=0, keepdims=True),
                                  jnp.sum(oh2 * before, axis=0, keepdims=True),
                                  zi.astype(F32)], axis=0).astype(jnp.int32)


def _route(logits_t):
    n = logits_t.shape[1]
    tt = RT_TT
    tri = jnp.asarray(np.triu(np.ones((tt, tt), np.float32), 1), BF16)
    blk = pl.BlockSpec((SUBLANES, tt), lambda i: (0, i))
    return pl.pallas_call(
        _route_kernel,
        grid=(n // tt,),
        in_specs=[pl.BlockSpec((LANES, tt), lambda i: (0, i)), pl.BlockSpec((tt, tt), lambda i: (0, 0))],
        out_specs=[blk, blk, blk, pl.BlockSpec((N_EXPERTS, LANES), lambda i: (0, 0))],
        out_shape=[jax.ShapeDtypeStruct((SUBLANES, n), jnp.int32), jax.ShapeDtypeStruct((SUBLANES, n), F32),
                   jax.ShapeDtypeStruct((SUBLANES, n), jnp.int32), jax.ShapeDtypeStruct((N_EXPERTS, LANES), F32)],
        scratch_shapes=[pltpu.VMEM((N_EXPERTS, LANES), F32)],
        compiler_params=_cparams("arbitrary"),
        name="moe_route",
    )(logits_t, tri)


def _dispatch_kernel(d0_ref, d1_ref, h_ref, zero_ref, buf_ref, sem):
    del zero_ref
    tt = h_ref.shape[0]

    def row_copy(t, d_ref):
        return pltpu.make_async_copy(h_ref.at[pl.ds(t, 1)], buf_ref.at[pl.ds(d_ref[t], 1)], sem)

    def issue(t, c):
        row_copy(t, d0_ref).start()
        row_copy(t, d1_ref).start()
        return c

    def drain(t, c):
        row_copy(t, d0_ref).wait()
        row_copy(t, d1_ref).wait()
        return c

    lax.fori_loop(0, tt, issue, 0, unroll=8)
    lax.fori_loop(0, tt, drain, 0, unroll=8)


def _dispatch(dest0, dest1, h2p, cap):
    n, width = h2p.shape
    tt = DSP_TT
    slot = pl.BlockSpec((tt,), lambda i: (i,), memory_space=pltpu.SMEM)
    return pl.pallas_call(
        _dispatch_kernel,
        grid=(n // tt,),
        in_specs=[slot, slot, pl.BlockSpec((tt, width), lambda i: (i, 0)), pl.BlockSpec(memory_space=pl.ANY)],
        out_specs=pl.BlockSpec(memory_space=pl.ANY),
        out_shape=jax.ShapeDtypeStruct((cap, width), h2p.dtype),
        scratch_shapes=[pltpu.SemaphoreType.DMA],
        input_output_aliases={3: 0},
        compiler_params=_cparams("arbitrary"),
        name="moe_dispatch",
    )(dest0, dest1, h2p, jnp.zeros((cap, width), h2p.dtype))


def _expert_kernel(be_ref, nb_ref, x_ref, wg_ref, wu_ref, wd_ref, o_ref, wgb_ref, wub_ref, wdb_ref):
    i = pl.program_id(0)
    half = D_MODEL // 2

    @pl.when(i < nb_ref[0])
    def _():
        @pl.when((i == 0) | (be_ref[i] != be_ref[jnp.maximum(i - 1, 0)]))
        def _():
            wgb_ref[...] = wg_ref[...].astype(BF16)
            wub_ref[...] = wu_ref[...].astype(BF16)
            wdb_ref[...] = wd_ref[...].astype(BF16)

        xa, xb = _unpack_rows(x_ref[...])
        xa, xb = xa.astype(BF16), xb.astype(BF16)
        g = _dot(xa, wgb_ref[0:half, :]) + _dot(xb, wgb_ref[half:, :])
        u = _dot(xa, wub_ref[0:half, :]) + _dot(xb, wub_ref[half:, :])
        o_ref[...] = _pack_rows(_dot(((g * _sigmoid(g)) * u).astype(BF16), wdb_ref[...]))

    @pl.when(i >= nb_ref[0])
    def _():
        o_ref[...] = jnp.zeros(o_ref.shape, o_ref.dtype)


def _experts(blk_expert, n_used, buf, w_gate, w_up, w_down, layer):
    cap, width = buf.shape
    blk = MOE_BLK
    row = lambda i, be, nb: (jnp.minimum(i, nb[0] - 1), 0)
    wmap = lambda i, be, nb: (layer, be[jnp.minimum(i, nb[0] - 1)], 0, 0)
    return pl.pallas_call(
        _expert_kernel,
        grid_spec=pltpu.PrefetchScalarGridSpec(
            num_scalar_prefetch=2,
            grid=(cap // blk,),
            in_specs=[pl.BlockSpec((blk, width), row),
                      pl.BlockSpec((None, None, D_MODEL, EXPERT_FF), wmap),
                      pl.BlockSpec((None, None, D_MODEL, EXPERT_FF), wmap),
                      pl.BlockSpec((None, None, EXPERT_FF, D_MODEL), wmap)],
            out_specs=pl.BlockSpec((blk, width), lambda i, be, nb: (i, 0)),
            scratch_shapes=[pltpu.VMEM((D_MODEL, EXPERT_FF), BF16), pltpu.VMEM((D_MODEL, EXPERT_FF), BF16),
                            pltpu.VMEM((EXPERT_FF, D_MODEL), BF16)],
        ),
        out_shape=jax.ShapeDtypeStruct((cap, width), buf.dtype),
        compiler_params=_cparams("arbitrary"),
        name="moe_experts",
    )(blk_expert, n_used, buf, w_gate, w_up, w_down)


def _combine_kernel(d0_ref, d1_ref, x1_ref, w_ref, ob_ref, o_ref, rows_ref, sem):
    tt, half = x1_ref.shape[0], x1_ref.shape[1] // 2

    def row_copy(t, k, d_ref):
        return pltpu.make_async_copy(ob_ref.at[pl.ds(d_ref[t], 1)], rows_ref.at[k, pl.ds(t, 1)], sem)

    def issue(t, c):
        row_copy(t, 0, d0_ref).start()
        row_copy(t, 1, d1_ref).start()
        return c

    def drain(t, c):
        row_copy(t, 0, d0_ref).wait()
        row_copy(t, 1, d1_ref).wait()
        return c

    lax.fori_loop(0, tt, issue, 0, unroll=8)
    lax.fori_loop(0, tt, drain, 0, unroll=8)
    a0, b0 = _unpack_rows(rows_ref[0])
    a1, b1 = _unpack_rows(rows_ref[1])
    w0, w1 = w_ref[:, 0:1], w_ref[:, 1:2]
    o_ref[:, 0:half] = x1_ref[:, 0:half] + w0 * a0 + w1 * a1
    o_ref[:, half:] = x1_ref[:, half:] + w0 * b0 + w1 * b1


def _combine(dest0, dest1, x1, w_col, out_buf):
    n = x1.shape[0]
    tt = DSP_TT
    slot = pl.BlockSpec((tt,), lambda i: (i,), memory_space=pltpu.SMEM)
    return pl.pallas_call(
        _combine_kernel,
        grid=(n // tt,),
        in_specs=[slot, slot, pl.BlockSpec((tt, D_MODEL), lambda i: (i, 0)),
                  pl.BlockSpec((tt, EXPERT_TOP_K), lambda i: (i, 0)),
                  pl.BlockSpec(memory_space=pl.ANY)],
        out_specs=pl.BlockSpec((tt, D_MODEL), lambda i: (i, 0)),
        out_shape=jax.ShapeDtypeStruct((n, D_MODEL), F32),
        scratch_shapes=[pltpu.VMEM((EXPERT_TOP_K, tt, out_buf.shape[1]), out_buf.dtype),
                        pltpu.SemaphoreType.DMA],
        compiler_params=_cparams("arbitrary"),
        name="moe_combine",
    )(dest0, dest1, x1, w_col, out_buf)


SC_CH = 64


def _sc_mesh():
    return plsc.VectorSubcoreMesh(core_axis_name="c", subcore_axis_name="s")


def _sc_worker(info):
    return lax.axis_index("s") * info.num_cores + lax.axis_index("c")


def _sc_invert(dest_flat, cap, n):
    info = plsc.get_sparse_core_info()
    lanes, workers = info.num_lanes, info.num_cores * info.num_subcores
    per_w = cap // workers
    n_assign = dest_flat.shape[0]
    chunk = 2048

    def body(dest_hbm, src_hbm, src_v, dest_v):
        lo = _sc_worker(info) * per_w
        lane = lax.iota(jnp.int32, lanes)

        @pl.loop(0, per_w // lanes)
        def _(j):
            src_v[pl.ds(j * lanes, lanes)] = (lo + j * lanes + lane) & (n - 1)

        @pl.loop(0, n_assign // chunk)
        def _(c):
            pltpu.sync_copy(dest_hbm.at[pl.ds(c * chunk, chunk)], dest_v)

            @pl.loop(0, chunk // lanes)
            def _(j):
                rel = dest_v[pl.ds(j * lanes, lanes)] - lo
                mine = (rel >= 0) & (rel < per_w)
                token = (c * chunk + j * lanes + lane) & (n - 1)
                plsc.store_scatter(src_v, [jnp.where(mine, rel, 0)], token, mask=mine)

        pltpu.sync_copy(src_v, src_hbm.at[pl.ds(lo, per_w)])

    return pl.kernel(
        body, out_type=jax.ShapeDtypeStruct((cap,), jnp.int32), mesh=_sc_mesh(),
        scratch_types=[pltpu.VMEM((per_w,), jnp.int32), pltpu.VMEM((chunk,), jnp.int32)],
        compiler_params=pltpu.CompilerParams(needs_layout_passes=False), name="sc_invert_slots",
    )(dest_flat)


def _sc_gather_rows(table, idx):
    info = plsc.get_sparse_core_info()
    workers = info.num_cores * info.num_subcores
    rows, width = idx.shape[0], table.shape[1]
    per_w = rows // workers

    def body(table_hbm, idx_hbm, out_hbm, idx_v, rows_v, sem):
        base = _sc_worker(info) * per_w
        pltpu.sync_copy(idx_hbm.at[pl.ds(base, per_w)], idx_v)

        @pl.loop(0, per_w // SC_CH)
        def _(j):
            pltpu.async_copy(table_hbm.at[idx_v.at[pl.ds(j * SC_CH, SC_CH)]], rows_v, sem).wait()
            pltpu.sync_copy(rows_v, out_hbm.at[pl.ds(base + j * SC_CH, SC_CH)])

    return pl.kernel(
        body, out_type=jax.ShapeDtypeStruct((rows, width), table.dtype), mesh=_sc_mesh(),
        scratch_types=[pltpu.VMEM((per_w,), jnp.int32), pltpu.VMEM((SC_CH, width), table.dtype),
                       pltpu.SemaphoreType.DMA],
        name="sc_gather_rows",
    )(table, idx)


def _blend_kernel(x1_ref, w_ref, g0_ref, g1_ref, o_ref):
    half = x1_ref.shape[1] // 2
    a0, b0 = _unpack_rows(g0_ref[...])
    a1, b1 = _unpack_rows(g1_ref[...])
    w0, w1 = w_ref[:, 0:1], w_ref[:, 1:2]
    o_ref[:, 0:half] = x1_ref[:, 0:half] + w0 * a0 + w1 * a1
    o_ref[:, half:] = x1_ref[:, half:] + w0 * b0 + w1 * b1


def _blend(x1, w_col, gathered):
    n = x1.shape[0]
    tt = DSP_TT
    nt = n // tt
    width = gathered.shape[1]
    return pl.pallas_call(
        _blend_kernel,
        grid=(nt,),
        in_specs=[pl.BlockSpec((tt, D_MODEL), lambda i: (i, 0)),
                  pl.BlockSpec((tt, EXPERT_TOP_K), lambda i: (i, 0)),
                  pl.BlockSpec((tt, width), lambda i: (i, 0)),
                  pl.BlockSpec((tt, width), lambda i: (nt + i, 0))],
        out_specs=pl.BlockSpec((tt, D_MODEL), lambda i: (i, 0)),
        out_shape=jax.ShapeDtypeStruct((n, D_MODEL), F32),
        compiler_params=_cparams("parallel"),
        name="moe_blend",
    )(x1, w_col, gathered, gathered)


def _moe(x1, h2, logits_t, w_gate, w_up, w_down, layer):
    n = x1.shape[0]
    e_idx, wts, rank, counts = _route(logits_t)
    counts = counts[:, 0].astype(jnp.int32)
    padded = (counts + MOE_BLK - 1) // MOE_BLK * MOE_BLK
    pends = jnp.cumsum(padded)
    pstarts = pends - padded
    experts = jnp.arange(N_EXPERTS, dtype=jnp.int32)
    start_of = jnp.sum(jnp.where(e_idx[None, :EXPERT_TOP_K] == experts[:, None, None],
                                 pstarts[:, None, None], 0), axis=0)
    dest = start_of + rank[:EXPERT_TOP_K]
    cap = n * EXPERT_TOP_K + N_EXPERTS * MOE_BLK
    n_blk = cap // MOE_BLK
    blk_row = jnp.arange(n_blk, dtype=jnp.int32) * MOE_BLK
    blk_expert = jnp.minimum(jnp.sum((pends[None, :] <= blk_row[:, None]).astype(jnp.int32), axis=1),
                             N_EXPERTS - 1)
    n_used = (pends[-1:] // MOE_BLK).astype(jnp.int32)
    dest_flat = dest.reshape(EXPERT_TOP_K * n)
    buf = _sc_gather_rows(h2, _sc_invert(dest_flat, cap, n))
    out_buf = _experts(blk_expert, n_used, buf, w_gate, w_up, w_down, layer)
    return _blend(x1, wts[:EXPERT_TOP_K].T, _sc_gather_rows(out_buf, dest_flat))


def kernel(x, positions, norm_mix, norm_ffn, w_in, q_gain, k_gain, cmp_k_pos, cmp_k_w1, cmp_k_w2, cmp_v_pos,
           cmp_v_w1, cmp_v_w2, s5_a_re, s5_a_im, s5_log_dt, s5_b_re, s5_b_im, s5_c_re, s5_c_im, s5_d, s5_glu,
           proj_nsa, proj_s5, w_out, router_group, router_group_bias, router_expert, router_expert_bias,
           expert_gate, expert_up, expert_down):
    batch, seq, _ = x.shape
    depth = w_in.shape[0]
    n = batch * seq
    inv_freq = ROPE_THETA ** (-jnp.arange(0, ROPE_DIMS, 2, dtype=jnp.float32) / ROPE_DIMS)
    ctab, stab = _rope_tables(positions.reshape(n, 1), inv_freq)
    ccmp, scmp = _cmp_rope_tables(ctab, stab, batch, seq)
    x2 = x.reshape(n, D_MODEL)
    for l in range(depth):
        w1, w_ga, w_gs = _pack_w_in(w_in[l])
        o_a, u_tm = _nsa_branch(x2, norm_mix[l][None], w1, ctab, stab, ccmp, scmp, q_gain[l], k_gain[l],
                                (cmp_k_pos[l], cmp_k_w1[l], cmp_k_w2[l]),
                                (cmp_v_pos[l], cmp_v_w1[l], cmp_v_w2[l]), batch, seq)
        lre, lim, w_b, w_c = _s5_weights(s5_a_re[l], s5_a_im[l], s5_log_dt[l], s5_b_re[l], s5_b_im[l],
                                         s5_c_re[l], s5_c_im[l])
        ob_tm = _s5(u_tm.reshape(seq * batch, S5_WIDTH), lre, lim, w_b, w_c, s5_d[l][None],
                    s5_glu[l].astype(BF16), batch, seq)
        wrh, wrl, rb = _router_weights(router_group[l], router_group_bias[l], router_expert[l],
                                       router_expert_bias[l])
        x1, h2, logits_t = _merge(x2, norm_mix[l][None], w_ga, w_gs, o_a, ob_tm, proj_nsa[l].astype(BF16),
                                  proj_s5[l].astype(BF16), w_out[l].astype(BF16), norm_ffn[l][None],
                                  wrh, wrl, rb, batch, seq)
        x2 = _moe(x1, h2, logits_t, expert_gate, expert_up, expert_down, l)
    return x2.reshape(batch, seq, D_MODEL)


def _attn_tables(seq):
    n_cmp = (seq - CMP_BLOCK) // CMP_STRIDE + 1
    n_slc = seq // SLC_BLOCK
    key_blk = np.arange(seq) // SLC_BLOCK
    expand_t = (key_blk[:, None] == np.arange(LANES)[None, :]).astype(np.float32)
    cmp_start = np.arange(LANES) * CMP_STRIDE
    slc_start = np.arange(LANES) * SLC_BLOCK
    overlap_t = ((cmp_start[None, :] < slc_start[:, None] + SLC_BLOCK)
                 & (cmp_start[None, :] + CMP_BLOCK > slc_start[:, None])
                 & (np.arange(LANES)[None, :] < n_cmp) & (np.arange(LANES)[:, None] < n_slc))
    return (jnp.asarray(expand_t, BF16), jnp.asarray(overlap_t.astype(np.float32), BF16), n_cmp, n_slc)


def _attn_mask_tables():
    tq, tk, wk = ATT_TQ, ATT_TK, ATT_WK
    ql = np.arange(tq)
    onehot = np.tile(np.eye(tq, dtype=np.float32), (NSA_HEADS, 1))
    k = np.arange(wk)[:, None]
    band = []
    for c in range(WINDOW // tq + 1):
        if c < WINDOW // tq:
            keep = k <= c * tq + ql[None, :]
        else:
            keep = (k > ql[None, :]) & (k <= ql[None, :] + WINDOW)
        band.append(np.where(keep, 0.0, NEG_BIG))
    kl = np.arange(tk)[None, :]
    causal = [np.where(kl <= c * tq + ql[:, None], 0.0, NEG_BIG) for c in range(tk // tq)]
    return (jnp.asarray(onehot, BF16), jnp.asarray(np.stack(band), BF16),
            jnp.asarray(np.stack(causal), F32))


def _attention(q, ks, vs, kw, vw, kc, vc, gates, batch, seq):
    expand_t, overlap_t, n_cmp, n_slc = _attn_tables(seq)
    onehot, band, causal = _attn_mask_tables()
    tq = ATT_TQ
    nq = seq // tq
    rows = NSA_HEADS * tq
    tile = lambda w: pl.BlockSpec((tq, w), lambda b, i: (b * nq + i, 0))
    per_b = lambda r: pl.BlockSpec((r, LANES), lambda b, i: (b, 0))
    f32_rows = pltpu.VMEM((rows, LANES), F32)
    return pl.pallas_call(
        functools.partial(_attn_kernel, n_cmp=n_cmp, n_slc=n_slc, top_k=min(SLC_TOPK, n_slc)),
        grid=(batch, nq),
        in_specs=[tile(2 * NSA_WIDTH), per_b(seq), per_b(seq), per_b(seq), per_b(seq),
                  per_b(LANES), per_b(LANES), tile(LANES),
                  pl.BlockSpec(expand_t.shape, lambda b, i: (0, 0)),
                  pl.BlockSpec(overlap_t.shape, lambda b, i: (0, 0)),
                  pl.BlockSpec(onehot.shape, lambda b, i: (0, 0)),
                  pl.BlockSpec(band.shape, lambda b, i: (0, 0, 0)),
                  pl.BlockSpec(causal.shape, lambda b, i: (0, 0, 0))],
        out_specs=tile(NSA_WIDTH),
        out_shape=jax.ShapeDtypeStruct((batch * seq, NSA_WIDTH), BF16),
        scratch_shapes=[pltpu.VMEM((rows, 2 * LANES), BF16),
                        pltpu.VMEM((rows, ATT_TK), F32), pltpu.VMEM((rows, ATT_TK), BF16),
                        pltpu.VMEM((rows, ATT_WK), F32), pltpu.VMEM((rows, ATT_WK), BF16),
                        f32_rows, f32_rows, f32_rows, f32_rows, f32_rows],
        compiler_params=_cparams("parallel", "parallel"),
        name="nsa_attention",
    )(q, ks, vs, kw, vw, kc, vc, gates, expand_t, overlap_t, onehot, band, causal)
```

```python
import functools
import math

import jax
import jax.numpy as jnp
import numpy as np
from jax import lax
from jax.experimental import pallas as pl
from jax.experimental.pallas import tpu as pltpu
from jax.experimental.pallas import tpu_sc as plsc

F32 = jnp.float32
BF16 = jnp.bfloat16

D_MODEL = 1024
NSA_HEADS = 8
NSA_KV_HEADS = 2
NSA_GROUP = NSA_HEADS // NSA_KV_HEADS
HEAD_DIM = 64
ROPE_DIMS = HEAD_DIM // 4
ROPE_THETA = 500000.0
CMP_BLOCK = 32
CMP_STRIDE = 16
CMP_HIDDEN = 2 * HEAD_DIM
SLC_BLOCK = 64
SLC_TOPK = 16
WINDOW = 512
NSA_WIDTH = NSA_HEADS * HEAD_DIM
KV_WIDTH = NSA_KV_HEADS * HEAD_DIM
S5_WIDTH = 256
S5_GROUP_CH = 16
S5_GROUPS = S5_WIDTH // S5_GROUP_CH
S5_STATE = 64
S5_NSTATE = S5_GROUPS * S5_STATE
N_EXPERT_GROUPS = 4
EXPERTS_PER_GROUP = 8
N_EXPERTS = N_EXPERT_GROUPS * EXPERTS_PER_GROUP
EXPERT_TOP_K = 2
EXPERT_FF = 512
RMS_EPS = 1e-6
NEG_BIG = -1e30

LANES = 128
SUBLANES = 8
VMEM_LIMIT = 56 * 1024 * 1024

IN_TM = 512
ATT_TQ = 128
ATT_TK = 512
ATT_WK = WINDOW + ATT_TQ
ATT_RB = 128
S5_T = 128
MRG_TM = 512
RT_TT = 512
MOE_BLK = 512
DSP_TT = 1024

C_Q, C_KS, C_KW, C_KC, C_VC, C_VS, C_VW, C_U, C_G = 0, 512, 640, 768, 896, 1024, 1152, 1280, 1536
IN_W = 1664


def _cparams(*sem):
    return pltpu.CompilerParams(dimension_semantics=sem, vmem_limit_bytes=VMEM_LIMIT)


def _gelu(x):
    return 0.5 * x * (1.0 + jnp.tanh(math.sqrt(2.0 / math.pi) * (x + 0.044715 * (x * x * x))))


def _sigmoid(x):
    return 1.0 / (1.0 + jnp.exp(-x))


def _dot(a, b):
    return jnp.dot(a, b, preferred_element_type=F32)


def _dot_t(a, b):
    return lax.dot_general(a, b, (((1,), (1,)), ((), ())), preferred_element_type=F32)


def _split_bf16(x):
    hi = x.astype(BF16)
    lo = (x - hi.astype(F32)).astype(BF16)
    return hi, lo


def _pack_rows(x):
    k = x.shape[1] // 2
    hi = lax.bitcast_convert_type(x[:, :k].astype(jnp.bfloat16).astype(F32), jnp.int32)
    lo = lax.bitcast_convert_type(x[:, k:].astype(jnp.bfloat16).astype(F32), jnp.int32)
    return hi | lax.shift_right_logical(lo, 16)


def _unpack_rows(w):
    a = lax.bitcast_convert_type(w & jnp.int32(-65536), F32)
    b = lax.bitcast_convert_type(lax.shift_left(w, 16), F32)
    return a, b


def _rope_kernel(pos_ref, freq_ref, sign_ref, c_ref, s_ref):
    ang = pos_ref[...].astype(F32) * freq_ref[...]
    c_ref[...] = jnp.cos(ang)
    s_ref[...] = jnp.sin(ang) * sign_ref[...]


def _rope_tables(pos_col, inv_freq):
    n = pos_col.shape[0]
    lane = np.arange(LANES) % HEAD_DIM
    sel = np.zeros((LANES, ROPE_DIMS // 2), np.float32)
    for l in range(LANES):
        if lane[l] < ROPE_DIMS:
            sel[l, lane[l] % (ROPE_DIMS // 2)] = 1.0
    freq = (jnp.asarray(sel) * inv_freq[None, :]).sum(-1)[None, :]
    sign = np.where(lane < ROPE_DIMS // 2, -1.0, 1.0).astype(np.float32)[None, :]
    tm = 1024
    return pl.pallas_call(
        _rope_kernel,
        grid=(n // tm,),
        in_specs=[pl.BlockSpec((tm, 1), lambda i: (i, 0)),
                  pl.BlockSpec((1, LANES), lambda i: (0, 0)),
                  pl.BlockSpec((1, LANES), lambda i: (0, 0))],
        out_specs=[pl.BlockSpec((tm, LANES), lambda i: (i, 0))] * 2,
        out_shape=[jax.ShapeDtypeStruct((n, LANES), F32)] * 2,
        compiler_params=_cparams("parallel"),
        name="rope_tables",
    )(pos_col, freq, jnp.asarray(sign))


def _headnorm_rope(t, gain, c, s):
    lane = lax.broadcasted_iota(jnp.int32, t.shape, 1)
    lo = lane < HEAD_DIM
    sq = t * t
    s_lo = jnp.sum(jnp.where(lo, sq, 0.0), axis=-1, keepdims=True)
    s_hi = jnp.sum(jnp.where(lo, 0.0, sq), axis=-1, keepdims=True)
    ms = jnp.where(lo, s_lo, s_hi) * (1.0 / HEAD_DIM)
    tn = (t * lax.rsqrt(ms + RMS_EPS)) * gain
    half = ROPE_DIMS // 2
    partner = jnp.where((lane & (HEAD_DIM - 1)) < half,
                        pltpu.roll(tn, LANES - half, 1), pltpu.roll(tn, half, 1))
    return tn * c + partner * s


def _in_proj_kernel(x_ref, gain_ref, w_ref, c_ref, s_ref, qg_ref, kg_ref,
                    q_ref, ks_ref, kw_ref, kc_ref, vc_ref, vs_ref, vw_ref, u_ref, g_ref):
    x = x_ref[...]
    ms = jnp.mean(x * x, axis=-1, keepdims=True)
    h = (x * lax.rsqrt(ms + RMS_EPS)) * gain_ref[...]
    z = _dot(h.astype(BF16), w_ref[...])
    c = c_ref[...]
    s = s_ref[...]
    lane = lax.broadcasted_iota(jnp.int32, c.shape, 1)
    lo = lane < HEAD_DIM
    scale = HEAD_DIM ** -0.5 * math.log2(math.e)
    for j in range(NSA_HEADS // 2):
        r = _headnorm_rope(z[:, C_Q + LANES * j:C_Q + LANES * (j + 1)], qg_ref[...], c, s) * scale
        rr = pltpu.roll(r, HEAD_DIM, 1)
        if j // (NSA_GROUP // 2) == 0:
            even, odd = jnp.where(lo, r, 0.0), jnp.where(lo, rr, 0.0)
        else:
            even, odd = jnp.where(lo, 0.0, rr), jnp.where(lo, 0.0, r)
        q_ref[:, 2 * LANES * j:2 * LANES * j + LANES] = even.astype(BF16)
        q_ref[:, 2 * LANES * j + LANES:2 * LANES * (j + 1)] = odd.astype(BF16)
    ks_ref[...] = _headnorm_rope(z[:, C_KS:C_KS + LANES], kg_ref[...], c, s).astype(BF16)
    kw_ref[...] = _headnorm_rope(z[:, C_KW:C_KW + LANES], kg_ref[...], c, s).astype(BF16)
    kc_ref[...] = z[:, C_KC:C_KC + LANES].astype(BF16)
    vc_ref[...] = z[:, C_VC:C_VC + LANES].astype(BF16)
    vs_ref[...] = z[:, C_VS:C_VS + LANES].astype(BF16)
    vw_ref[...] = z[:, C_VW:C_VW + LANES].astype(BF16)
    u_ref[...] = z[:, C_U:C_U + S5_WIDTH]
    g_ref[...] = _sigmoid(z[:, C_G:C_G + LANES])


def _in_proj(x2, gain, w1, ctab, stab, qg, kg, batch, seq):
    n = x2.shape[0]
    tm = IN_TM
    nt = seq // tm
    row = lambda b, i: (b * nt + i, 0)
    const = lambda b, i: (0, 0)
    tok = lambda w: pl.BlockSpec((tm, w), row)
    outs = [(n, 2 * NSA_WIDTH, BF16)] + [(n, LANES, BF16)] * 6
    return pl.pallas_call(
        _in_proj_kernel,
        grid=(batch, nt),
        in_specs=[tok(D_MODEL), pl.BlockSpec((1, D_MODEL), const), pl.BlockSpec((D_MODEL, IN_W), const),
                  tok(LANES), tok(LANES), pl.BlockSpec((1, LANES), const), pl.BlockSpec((1, LANES), const)],
        out_specs=[tok(2 * NSA_WIDTH)] + [tok(LANES)] * 6
        + [pl.BlockSpec((tm, S5_WIDTH), lambda b, i: (i, b)), tok(LANES)],
        out_shape=[jax.ShapeDtypeStruct(s[:2], s[2]) for s in outs]
        + [jax.ShapeDtypeStruct((seq, batch * S5_WIDTH), F32), jax.ShapeDtypeStruct((n, LANES), F32)],
        compiler_params=_cparams("parallel", "parallel"),
        name="in_proj",
    )(x2, gain, w1, ctab, stab, qg, kg)


def _compress_mlp(h_ref, w_ref, p_ref, w2_ref):
    w = w_ref[...]
    pre = _dot(h_ref[...], w)
    phi, plo = _split_bf16(p_ref[...])
    pb = _dot(phi, w) + _dot(plo, w)
    rows = pre.shape[0]
    out = None
    for h in range(NSA_KV_HEADS):
        top = pre[:, LANES * h:LANES * (h + 1)]
        bot = pre[:, LANES * (2 + h):LANES * (3 + h)]
        bias = pb[0:1, LANES * h:LANES * (h + 1)] + pb[1:2, LANES * (2 + h):LANES * (3 + h)]
        a = _gelu(top + pltpu.roll(bot, rows - 1, 0) + bias)
        o = _dot(a.astype(BF16), w2_ref[h])
        out = o if out is None else out + o
    return out


def _compress_kernel(hk_ref, hv_ref, wk_ref, wv_ref, pk_ref, pv_ref, w2k_ref, w2v_ref,
                     kg_ref, c_ref, s_ref, kc_ref, vc_ref):
    k = _compress_mlp(hk_ref, wk_ref, pk_ref, w2k_ref)
    kc_ref[...] = _headnorm_rope(k, kg_ref[...], c_ref[...], s_ref[...]).astype(BF16)
    vc_ref[...] = _compress_mlp(hv_ref, wv_ref, pv_ref, w2v_ref).astype(BF16)


def _compress(hk, hv, wk, wv, pk, pv, w2k, w2v, kg, ccmp, scmp, batch):
    rows = hk.shape[0] // batch
    width = hk.shape[1]
    const2 = lambda b: (0, 0)
    const3 = lambda b: (0, 0, 0)
    rowb = lambda w: pl.BlockSpec((rows, w), lambda b: (b, 0))
    return pl.pallas_call(
        _compress_kernel,
        grid=(batch,),
        in_specs=[rowb(width), rowb(width),
                  pl.BlockSpec(wk.shape, const2), pl.BlockSpec(wv.shape, const2),
                  pl.BlockSpec(pk.shape, const2), pl.BlockSpec(pv.shape, const2),
                  pl.BlockSpec(w2k.shape, const3), pl.BlockSpec(w2v.shape, const3),
                  pl.BlockSpec((1, LANES), const2), rowb(LANES), rowb(LANES)],
        out_specs=[rowb(LANES), rowb(LANES)],
        out_shape=[jax.ShapeDtypeStruct((hk.shape[0], LANES), BF16)] * 2,
        compiler_params=_cparams("parallel"),
        name="compress_kv",
    )(hk, hv, wk, wv, pk, pv, w2k, w2v, kg, ccmp, scmp)


def _compress_weights(pos_emb, w1, w2):
    half = CMP_BLOCK // 2
    eye = jnp.eye(NSA_KV_HEADS, dtype=F32)
    w1r = w1.reshape(2, half, HEAD_DIM, CMP_HIDDEN)
    wexp = jnp.einsum("sldn,ph->lpdshn", w1r, eye)
    wexp = wexp.reshape(half * KV_WIDTH, 2 * NSA_KV_HEADS * CMP_HIDDEN).astype(BF16)
    pexp = jnp.broadcast_to(pos_emb.reshape(2, half, 1, HEAD_DIM), (2, half, NSA_KV_HEADS, HEAD_DIM))
    pexp = jnp.pad(pexp.reshape(2, half * KV_WIDTH), ((0, SUBLANES - 2), (0, 0)))
    w2e = jnp.einsum("cd,ph->pchd", w2, eye).reshape(NSA_KV_HEADS, CMP_HIDDEN, KV_WIDTH)
    return wexp, pexp, w2e.astype(BF16)


def _attn_kernel(q_ref, ks_ref, vs_ref, kw_ref, vw_ref, kc_ref, vc_ref, g_ref, et_ref, ovt_ref, oh_ref, wb_ref,
                 cb_ref, o_ref, qc_ref, s_ref, p_ref, sw_ref, pw_ref, m_ref, al_ref, oc_ref, as_ref, aw_ref,
                 *, n_cmp, n_slc, top_k):
    tq, tk, wk, rb, nh = ATT_TQ, ATT_TK, ATT_WK, ATT_RB, NSA_HEADS
    rows, grp_rows = nh * tq, NSA_GROUP * tq
    q0 = pl.program_id(1) * tq
    for a in range(nh):
        qc_ref[a * tq:(a + 1) * tq, 0:LANES] = q_ref[:, LANES * a:LANES * (a + 1)]
    q8 = qc_ref[:, 0:LANES]
    lane = lax.broadcasted_iota(jnp.int32, (tq, LANES), 1)
    t_row = q0 + lax.broadcasted_iota(jnp.int32, (tq, LANES), 0)

    s_c = _dot_t(q8, kc_ref[...]).reshape(nh, tq, LANES)
    mask_c = ((lane * CMP_STRIDE + (CMP_BLOCK - 1)) <= t_row) & (lane < n_cmp)
    sm = jnp.where(mask_c[None], s_c, NEG_BIG)
    p = jnp.where(mask_c[None], jnp.exp2(sm - jnp.max(sm, axis=-1, keepdims=True)), 0.0)
    l = jnp.sum(p, axis=-1, keepdims=True)
    p_c = p * jnp.where(l > 0.0, 1.0 / l, 0.0)
    oc_ref[...] = _dot(p_c.reshape(rows, LANES).astype(BF16), vc_ref[...])

    blk = lax.broadcasted_iota(jnp.int32, (n_slc, tq), 0)
    cur = (q0 + lax.broadcasted_iota(jnp.int32, (n_slc, tq), 1)) >> int(math.log2(SLC_BLOCK))
    forced = (blk == 0) | (blk == cur) | (blk == cur - 1)
    for h in range(NSA_KV_HEADS):
        ps = p_c[NSA_GROUP * h]
        for g in range(1, NSA_GROUP):
            ps = ps + p_c[NSA_GROUP * h + g]
        hi, lo = _split_bf16(ps)
        imp = (_dot_t(ovt_ref[...], hi) + _dot_t(ovt_ref[...], lo))[0:n_slc]
        imp = jnp.where(forced, jnp.inf, jnp.where(blk > cur, -jnp.inf, imp))
        rank = jnp.zeros((n_slc, tq), F32)
        for i in range(n_slc):
            row = imp[i:i + 1, :]
            rank = rank + jnp.where((row > imp) | ((row == imp) & (blk > i)), 1.0, 0.0)
        bias_t = jnp.where(rank < float(top_k), 0.0, NEG_BIG)
        bias = jnp.concatenate([bias_t, jnp.zeros((LANES - n_slc, tq), F32)], axis=0).T.astype(BF16)
        for g in range(NSA_GROUP):
            a = NSA_GROUP * h + g
            qc_ref[a * tq:(a + 1) * tq, LANES:2 * LANES] = bias

    def softmax_pass(s_buf, p_buf, width, causal_case, online):
        for r in range(rows // rb):
            r0 = r * rb
            s = s_buf[pl.ds(r0, rb), 0:width]
            if causal_case is not None:
                s = s + cb_ref[causal_case, pl.ds(r0 % tq, rb), :]
            m_new = jnp.broadcast_to(jnp.max(s, axis=1, keepdims=True), (rb, LANES))
            if online:
                m_old = m_ref[pl.ds(r0, rb), :]
                m_new = jnp.maximum(m_old, m_new)
                al_ref[pl.ds(r0, rb), :] = jnp.exp2(m_old - m_new)
                m_ref[pl.ds(r0, rb), :] = m_new
            pp = jnp.exp2(s - jnp.concatenate([m_new] * (width // LANES), axis=1))
            p_buf[pl.ds(r0, rb), 0:width] = pp.astype(BF16)

    def pv(acc_ref, p_buf, v, width, online):
        lane_v = lax.broadcasted_iota(jnp.int32, v.shape, 1)
        for h in range(NSA_KV_HEADS):
            rs = slice(h * grp_rows, (h + 1) * grp_rows)
            own = (lane_v < HEAD_DIM) if h == 0 else (lane_v >= HEAD_DIM)
            upd = _dot(p_buf[rs, 0:width], jnp.where(own, v, jnp.ones_like(v)))
            acc_ref[rs, :] = (al_ref[rs, :] * acc_ref[rs, :] + upd) if online else upd

    m_ref[...] = jnp.full(m_ref.shape, 0.1 * NEG_BIG, F32)
    as_ref[...] = jnp.zeros(as_ref.shape, F32)

    def slc_chunk(c, diagonal):
        k0 = pl.multiple_of(c * tk, tk)
        kcat = jnp.concatenate([ks_ref[pl.ds(k0, tk), :], et_ref[pl.ds(k0, tk), :]], axis=1)
        s_ref[...] = _dot_t(qc_ref[...], kcat)
        softmax_pass(s_ref, p_ref, tk, ((q0 - k0) >> int(math.log2(tq))) if diagonal else None, True)
        pv(as_ref, p_ref, vs_ref[pl.ds(k0, tk), :], tk, True)

    c_last = q0 >> int(math.log2(tk))

    def full_chunk(c, carry):
        slc_chunk(c, False)
        return carry

    lax.fori_loop(0, c_last, full_chunk, 0)
    slc_chunk(c_last, True)

    w0 = pl.multiple_of(jnp.maximum(q0 - WINDOW, 0), LANES)
    w_case = jnp.minimum(q0 >> int(math.log2(tq)), WINDOW // tq)
    kcat_w = jnp.concatenate([kw_ref[pl.ds(w0, wk), :], wb_ref[w_case]], axis=1)
    sw_ref[...] = _dot_t(jnp.concatenate([q8, oh_ref[...]], axis=1), kcat_w)
    softmax_pass(sw_ref, pw_ref, wk, None, False)
    pv(aw_ref, pw_ref, vw_ref[pl.ds(w0, wk), :], wk, False)

    lo_half = lane < HEAD_DIM
    for j in range(nh // 2):
        comb = []
        for a in (2 * j, 2 * j + 1):
            rs = slice(a * tq, (a + 1) * tq)
            o_s, o_w = as_ref[rs, :], aw_ref[rs, :]
            comb.append(g_ref[:, 3 * a:3 * a + 1] * oc_ref[rs, :]
                        + g_ref[:, 3 * a + 1:3 * a + 2] * (o_s * (1.0 / pltpu.roll(o_s, HEAD_DIM, 1)))
                        + g_ref[:, 3 * a + 2:3 * a + 3] * (o_w * (1.0 / pltpu.roll(o_w, HEAD_DIM, 1))))
        even, odd = comb
        if j // (NSA_GROUP // 2) == 0:
            pair = jnp.where(lo_half, even, pltpu.roll(odd, HEAD_DIM, 1))
        else:
            pair = jnp.where(lo_half, pltpu.roll(even, HEAD_DIM, 1), odd)
        o_ref[:, LANES * j:LANES * (j + 1)] = pair.astype(BF16)


def _pack_w_in(w):
    sizes = (NSA_WIDTH,) + (KV_WIDTH,) * 6 + (3 * NSA_HEADS, S5_WIDTH, D_MODEL, D_MODEL)
    o = np.cumsum((0,) + sizes)
    q, kc, vc, ks, vs, kw, vw, g, u, ga, gs = [w[:, o[i]:o[i + 1]] for i in range(len(sizes))]
    g = jnp.pad(g, ((0, 0), (0, LANES - 3 * NSA_HEADS)))
    w1 = jnp.concatenate([q, ks, kw, kc, vc, vs, vw, u, g], axis=1).astype(BF16)
    return w1, ga.astype(BF16), gs.astype(BF16)


def _nsa_branch(x2, gain, w1, ctab, stab, ccmp, scmp, q_gain, k_gain, cmp_k, cmp_v, batch, seq):
    qg = jnp.tile(q_gain, 2)[None, :]
    kg = jnp.tile(k_gain, 2)[None, :]
    q, ks, kw, kc_raw, vc_raw, vs, vw, u_tm, gates = _in_proj(x2, gain, w1, ctab, stab, qg, kg, batch, seq)
    half_rows = batch * seq // (CMP_BLOCK // 2)
    hk = kc_raw.reshape(half_rows, (CMP_BLOCK // 2) * KV_WIDTH)
    hv = vc_raw.reshape(half_rows, (CMP_BLOCK // 2) * KV_WIDTH)
    wk, pk, w2k = _compress_weights(*cmp_k)
    wv, pv, w2v = _compress_weights(*cmp_v)
    kc, vc = _compress(hk, hv, wk, wv, pk, pv, w2k, w2v, kg, ccmp, scmp, batch)
    o_a = _attention(q, ks, vs, kw, vw, kc, vc, gates, batch, seq)
    return o_a, u_tm


def _cmp_rope_tables(ctab, stab, batch, seq):
    def pick(t):
        t = t.reshape(batch, seq, LANES)[:, CMP_BLOCK - 1::CMP_STRIDE]
        t = jnp.pad(t, ((0, 0), (0, seq // CMP_STRIDE - t.shape[1]), (0, 0)))
        return t.reshape(batch * (seq // CMP_STRIDE), LANES)
    return pick(ctab), pick(stab)


def _s5_disc_kernel(are_ref, aim_ref, ldt_ref, lre_ref, lim_ref, fre_ref, fim_ref):
    lam_re = jnp.minimum(are_ref[...], -1e-4)
    lam_im = aim_ref[...]
    dt = jnp.exp(ldt_ref[...])
    mag = jnp.exp(lam_re * dt)
    ang = lam_im * dt
    lb_re = mag * jnp.cos(ang)
    lb_im = mag * jnp.sin(ang)
    den = lam_re * lam_re + lam_im * lam_im
    lre_ref[...] = lb_re
    lim_ref[...] = lb_im
    fre_ref[...] = ((lb_re - 1.0) * lam_re + lb_im * lam_im) / den
    fim_ref[...] = (lb_im * lam_re - (lb_re - 1.0) * lam_im) / den


def _s5_weights(a_re, a_im, log_dt, b_re, b_im, c_re, c_im):
    shp = jax.ShapeDtypeStruct((S5_GROUPS, S5_STATE), F32)
    lb_re, lb_im, f_re, f_im = pl.pallas_call(
        _s5_disc_kernel, out_shape=[shp] * 4, name="s5_discretise",
    )(a_re, a_im, log_dt[:, None])
    bb_re = f_re[..., None] * b_re - f_im[..., None] * b_im
    bb_im = f_re[..., None] * b_im + f_im[..., None] * b_re
    eye = jnp.eye(S5_GROUPS, dtype=F32)

    def in_map(bb):
        return jnp.einsum("gpc,gh->gchp", bb, eye).reshape(S5_WIDTH, S5_NSTATE)

    def out_map(cc):
        return jnp.einsum("gcp,gh->hpgc", cc, eye).reshape(S5_NSTATE, S5_WIDTH)

    w_b = jnp.concatenate([in_map(bb_re), in_map(bb_im)], axis=1).astype(BF16)
    w_c = jnp.concatenate([out_map(c_re), -out_map(c_im)], axis=0).astype(BF16)
    return lb_re.reshape(1, S5_NSTATE), lb_im.reshape(1, S5_NSTATE), w_b, w_c


def _s5_kernel(u_ref, wb_ref, lre_ref, lim_ref, wc_ref, d_ref, wg_ref, o_ref, x_ref, st_ref, *, batch):
    ns = S5_NSTATE

    @pl.when(pl.program_id(0) == 0)
    def _():
        st_ref[...] = jnp.zeros(st_ref.shape, F32)

    u = u_ref[...]
    x_ref[...] = _dot(u.astype(BF16), wb_ref[...])
    lre = jnp.broadcast_to(lre_ref[...], (batch, ns))
    lim = jnp.broadcast_to(lim_ref[...], (batch, ns))

    def step(t, carry):
        xr, xi = carry
        r0 = pl.multiple_of(t * batch, batch)
        nr = lre * xr - lim * xi + x_ref[pl.ds(r0, batch), 0:ns]
        ni = lre * xi + lim * xr + x_ref[pl.ds(r0, batch), ns:2 * ns]
        x_ref[pl.ds(r0, batch), 0:ns] = nr
        x_ref[pl.ds(r0, batch), ns:2 * ns] = ni
        return nr, ni

    steps = u.shape[0] // batch
    xr, xi = lax.fori_loop(0, steps, step, (st_ref[:, 0:ns], st_ref[:, ns:2 * ns]), unroll=4)
    st_ref[:, 0:ns] = xr
    st_ref[:, ns:2 * ns] = xi
    y = _dot(x_ref[...].astype(BF16), wc_ref[...]) + d_ref[...] * u
    z = _dot(_gelu(y).astype(BF16), wg_ref[...])
    o_ref[...] = (z[:, :S5_WIDTH] * _sigmoid(z[:, S5_WIDTH:])).astype(BF16)


def _s5(u_tm, lre, lim, w_b, w_c, d_skip, w_glu, batch, seq):
    rows = S5_T * batch
    const = lambda t: (0, 0)
    return pl.pallas_call(
        functools.partial(_s5_kernel, batch=batch),
        grid=(seq // S5_T,),
        in_specs=[pl.BlockSpec((rows, S5_WIDTH), lambda t: (t, 0)),
                  pl.BlockSpec(w_b.shape, const), pl.BlockSpec(lre.shape, const), pl.BlockSpec(lim.shape, const),
                  pl.BlockSpec(w_c.shape, const), pl.BlockSpec((1, S5_WIDTH), const),
                  pl.BlockSpec(w_glu.shape, const)],
        out_specs=pl.BlockSpec((rows, S5_WIDTH), lambda t: (t, 0)),
        out_shape=jax.ShapeDtypeStruct((seq * batch, S5_WIDTH), BF16),
        scratch_shapes=[pltpu.VMEM((rows, 2 * S5_NSTATE), F32), pltpu.VMEM((batch, 2 * S5_NSTATE), F32)],
        compiler_params=_cparams("arbitrary"),
        name="s5_scan",
    )(u_tm, w_b, lre, lim, w_c, d_skip, w_glu)


RT_GROUP_ROW = 0
RT_EXPERT_ROW = SUBLANES


def _rmsnorm(x, gain):
    return (x * lax.rsqrt(jnp.mean(x * x, axis=-1, keepdims=True) + RMS_EPS)) * gain


def _merge_kernel(x_ref, g1_ref, wga_ref, wgs_ref, oa_ref, ob_ref, pa_ref, ps_ref, wo_ref, g2_ref,
                  wrh_ref, wrl_ref, rb_ref, x1_ref, h2_ref, lg_ref):
    x = x_ref[...]
    h = _rmsnorm(x, g1_ref[...]).astype(BF16)
    gate_a = _sigmoid(_dot(h, wga_ref[...]))
    gate_s = _sigmoid(_dot(h, wgs_ref[...]))
    merged = gate_a * _dot(oa_ref[...], pa_ref[...]) + gate_s * _dot(ob_ref[...], ps_ref[...])
    x1 = x + _dot(merged.astype(BF16), wo_ref[...])
    x1_ref[...] = x1
    h2 = _rmsnorm(x1, g2_ref[...])
    h2_ref[...] = _pack_rows(h2)
    hi, lo = _split_bf16(h2)
    lg_ref[...] = (_dot_t(wrh_ref[...], hi) + _dot_t(wrh_ref[...], lo) + _dot_t(wrl_ref[...], hi)
                   + rb_ref[...])


def _router_weights(w_rg, b_rg, w_re, b_re):
    def rows(group_part, expert_part):
        return jnp.concatenate([
            group_part, jnp.zeros((RT_EXPERT_ROW - N_EXPERT_GROUPS,) + group_part.shape[1:], F32),
            expert_part, jnp.zeros((LANES - RT_EXPERT_ROW - N_EXPERTS,) + group_part.shape[1:], F32)], axis=0)

    hi, lo = _split_bf16(rows(w_rg.T, w_re.T))
    return hi, lo, rows(b_rg[:, None], b_re[:, None])


def _merge(x2, g1, w_ga, w_gs, o_a, ob_tm, p_a, p_s, w_o, g2, wrh, wrl, rb, batch, seq):
    n = x2.shape[0]
    tm = MRG_TM
    nt = seq // tm
    row = lambda b, i: (b * nt + i, 0)
    const = lambda b, i: (0, 0)
    tok = lambda w: pl.BlockSpec((tm, w), row)
    full = lambda a: pl.BlockSpec(a.shape, const)
    ob_v = ob_tm.reshape(seq, batch * S5_WIDTH)
    return pl.pallas_call(
        _merge_kernel,
        grid=(batch, nt),
        in_specs=[tok(D_MODEL), full(g1), full(w_ga), full(w_gs), tok(NSA_WIDTH),
                  pl.BlockSpec((tm, S5_WIDTH), lambda b, i: (i, b)), full(p_a), full(p_s), full(w_o),
                  full(g2), full(wrh), full(wrl), full(rb)],
        out_specs=[tok(D_MODEL), tok(D_MODEL // 2), pl.BlockSpec((LANES, tm), lambda b, i: (0, b * nt + i))],
        out_shape=[jax.ShapeDtypeStruct((n, D_MODEL), F32), jax.ShapeDtypeStruct((n, D_MODEL // 2), jnp.int32),
                   jax.ShapeDtypeStruct((LANES, n), F32)],
        compiler_params=_cparams("parallel", "parallel"),
        name="merge_out_proj",
    )(x2, g1, w_ga, w_gs, o_a, ob_v, p_a, p_s, w_o, g2, wrh, wrl, rb)


def _first_argmax(v, rows):
    vmax = jnp.max(v, axis=0, keepdims=True)
    idx = jnp.min(jnp.where(v == vmax, rows, SUBLANES), axis=0, keepdims=True)
    return vmax, idx


def _route_kernel(lg_ref, tri_ref, e_ref, w_ref, r_ref, cnt_ref, carry_ref):
    tt = lg_ref.shape[1]

    @pl.when(pl.program_id(0) == 0)
    def _():
        carry_ref[...] = jnp.zeros(carry_ref.shape, F32)

    rows = lax.broadcasted_iota(jnp.int32, (SUBLANES, tt), 0)
    gl = jnp.where(rows < N_EXPERT_GROUPS, lg_ref[RT_GROUP_ROW:RT_GROUP_ROW + SUBLANES, :], -jnp.inf)
    gmax, g_sel = _first_argmax(gl, rows)
    g_w = 1.0 / jnp.sum(jnp.exp(gl - gmax), axis=0, keepdims=True)

    el = jnp.zeros((EXPERTS_PER_GROUP, tt), F32)
    for g in range(N_EXPERT_GROUPS):
        r0 = RT_EXPERT_ROW + EXPERTS_PER_GROUP * g
        el = jnp.where(g_sel == g, lg_ref[r0:r0 + EXPERTS_PER_GROUP, :], el)
    ex = jnp.exp(el - jnp.max(el, axis=0, keepdims=True))
    p = ex / jnp.sum(ex, axis=0, keepdims=True)
    p1, i1 = _first_argmax(p, rows)
    p2, i2 = _first_argmax(jnp.where(rows == i1, -1.0, p), rows)
    e1 = g_sel * EXPERTS_PER_GROUP + i1
    e2 = g_sel * EXPERTS_PER_GROUP + i2
    scale = g_w / (p1 + p2)

    erow = lax.broadcasted_iota(jnp.int32, (N_EXPERTS, tt), 0)
    oh1 = jnp.where(erow == e1, 1.0, 0.0)
    oh2 = jnp.where(erow == e2, 1.0, 0.0)
    both = oh1 + oh2
    before = carry_ref[:, 0:1] + _dot(both.astype(BF16), tri_ref[...])
    carry_ref[...] = carry_ref[...] + jnp.sum(both, axis=1, keepdims=True)
    cnt_ref[...] = carry_ref[...]

    zi = jnp.zeros((SUBLANES - 2, tt), jnp.int32)
    e_ref[...] = jnp.concatenate([e1, e2, zi], axis=0)
    w_ref[...] = jnp.concatenate([p1 * scale, p2 * scale, zi.astype(F32)], axis=0)
    r_ref[...] = jnp.concatenate([jnp.sum(oh1 * before, axis=0, keepdims=True),
                                  jnp.sum(oh2 * before, axis=0, keepdims=True),
                                  zi.astype(F32)], axis=0).astype(jnp.int32)


def _route(logits_t):
    n = logits_t.shape[1]
    tt = RT_TT
    tri = jnp.asarray(np.triu(np.ones((tt, tt), np.float32), 1), BF16)
    blk = pl.BlockSpec((SUBLANES, tt), lambda i: (0, i))
    return pl.pallas_call(
        _route_kernel,
        grid=(n // tt,),
        in_specs=[pl.BlockSpec((LANES, tt), lambda i: (0, i)), pl.BlockSpec((tt, tt), lambda i: (0, 0))],
        out_specs=[blk, blk, blk, pl.BlockSpec((N_EXPERTS, LANES), lambda i: (0, 0))],
        out_shape=[jax.ShapeDtypeStruct((SUBLANES, n), jnp.int32), jax.ShapeDtypeStruct((SUBLANES, n), F32),
                   jax.ShapeDtypeStruct((SUBLANES, n), jnp.int32), jax.ShapeDtypeStruct((N_EXPERTS, LANES), F32)],
        scratch_shapes=[pltpu.VMEM((N_EXPERTS, LANES), F32)],
        compiler_params=_cparams("arbitrary"),
        name="moe_route",
    )(logits_t, tri)


def _expert_kernel(be_ref, nb_ref, x_ref, wg_ref, wu_ref, wd_ref, o_ref, wgb_ref, wub_ref, wdb_ref):
    i = pl.program_id(0)
    half = D_MODEL // 2

    @pl.when(i < nb_ref[0])
    def _():
        @pl.when((i == 0) | (be_ref[i] != be_ref[jnp.maximum(i - 1, 0)]))
        def _():
            wgb_ref[...] = wg_ref[...].astype(BF16)
            wub_ref[...] = wu_ref[...].astype(BF16)
            wdb_ref[...] = wd_ref[...].astype(BF16)

        xa, xb = _unpack_rows(x_ref[...])
        xa, xb = xa.astype(BF16), xb.astype(BF16)
        g = _dot(xa, wgb_ref[0:half, :]) + _dot(xb, wgb_ref[half:, :])
        u = _dot(xa, wub_ref[0:half, :]) + _dot(xb, wub_ref[half:, :])
        o_ref[...] = _pack_rows(_dot(((g * _sigmoid(g)) * u).astype(BF16), wdb_ref[...]))

    @pl.when(i >= nb_ref[0])
    def _():
        o_ref[...] = jnp.zeros(o_ref.shape, o_ref.dtype)


def _experts(blk_expert, n_used, buf, w_gate, w_up, w_down, layer):
    cap, width = buf.shape
    blk = MOE_BLK
    row = lambda i, be, nb: (jnp.minimum(i, nb[0] - 1), 0)
    wmap = lambda i, be, nb: (layer, be[jnp.minimum(i, nb[0] - 1)], 0, 0)
    return pl.pallas_call(
        _expert_kernel,
        grid_spec=pltpu.PrefetchScalarGridSpec(
            num_scalar_prefetch=2,
            grid=(cap // blk,),
            in_specs=[pl.BlockSpec((blk, width), row),
                      pl.BlockSpec((None, None, D_MODEL, EXPERT_FF), wmap),
                      pl.BlockSpec((None, None, D_MODEL, EXPERT_FF), wmap),
                      pl.BlockSpec((None, None, EXPERT_FF, D_MODEL), wmap)],
            out_specs=pl.BlockSpec((blk, width), lambda i, be, nb: (i, 0)),
            scratch_shapes=[pltpu.VMEM((D_MODEL, EXPERT_FF), BF16), pltpu.VMEM((D_MODEL, EXPERT_FF), BF16),
                            pltpu.VMEM((EXPERT_FF, D_MODEL), BF16)],
        ),
        out_shape=jax.ShapeDtypeStruct((cap, width), buf.dtype),
        compiler_params=_cparams("arbitrary"),
        name="moe_experts",
    )(blk_expert, n_used, buf, w_gate, w_up, w_down)


SC_CH = 64


def _sc_mesh():
    return plsc.VectorSubcoreMesh(core_axis_name="c", subcore_axis_name="s")


def _sc_worker(info):
    return lax.axis_index("s") * info.num_cores + lax.axis_index("c")


def _sc_invert(dest_flat, cap, n):
    info = plsc.get_sparse_core_info()
    lanes, workers = info.num_lanes, info.num_cores * info.num_subcores
    per_w = cap // workers
    n_assign = dest_flat.shape[0]
    chunk = 2048

    def body(dest_hbm, src_hbm, src_v, dest_v):
        lo = _sc_worker(info) * per_w
        lane = lax.iota(jnp.int32, lanes)

        @pl.loop(0, per_w // lanes)
        def _(j):
            src_v[pl.ds(j * lanes, lanes)] = (lo + j * lanes + lane) & (n - 1)

        @pl.loop(0, n_assign // chunk)
        def _(c):
            pltpu.sync_copy(dest_hbm.at[pl.ds(c * chunk, chunk)], dest_v)

            @pl.loop(0, chunk // lanes)
            def _(j):
                rel = dest_v[pl.ds(j * lanes, lanes)] - lo
                mine = (rel >= 0) & (rel < per_w)
                token = (c * chunk + j * lanes + lane) & (n - 1)
                plsc.store_scatter(src_v, [jnp.where(mine, rel, 0)], token, mask=mine)

        pltpu.sync_copy(src_v, src_hbm.at[pl.ds(lo, per_w)])

    return pl.kernel(
        body, out_type=jax.ShapeDtypeStruct((cap,), jnp.int32), mesh=_sc_mesh(),
        scratch_types=[pltpu.VMEM((per_w,), jnp.int32), pltpu.VMEM((chunk,), jnp.int32)],
        compiler_params=pltpu.CompilerParams(needs_layout_passes=False), name="sc_invert_slots",
    )(dest_flat)


def _sc_gather_rows(table, idx):
    info = plsc.get_sparse_core_info()
    workers = info.num_cores * info.num_subcores
    rows, width = idx.shape[0], table.shape[1]
    per_w = rows // workers

    n_chunks = per_w // SC_CH
    assert per_w % (2 * SC_CH) == 0

    def body(table_hbm, idx_hbm, out_hbm, idx_v, rows_a, rows_b, sem_a, sem_b):
        base = _sc_worker(info) * per_w
        pltpu.sync_copy(idx_hbm.at[pl.ds(base, per_w)], idx_v)

        def gather(j, buf, sem):
            return pltpu.make_async_copy(table_hbm.at[idx_v.at[pl.ds(j * SC_CH, SC_CH)]], buf, sem)

        def write(j, buf):
            pltpu.sync_copy(buf, out_hbm.at[pl.ds(base + j * SC_CH, SC_CH)])

        gather(0, rows_a, sem_a).start()

        @pl.loop(0, n_chunks, step=2)
        def _(j):
            gather(j + 1, rows_b, sem_b).start()
            gather(j, rows_a, sem_a).wait()
            write(j, rows_a)

            @pl.when(j + 2 < n_chunks)
            def _():
                gather(j + 2, rows_a, sem_a).start()

            gather(j + 1, rows_b, sem_b).wait()
            write(j + 1, rows_b)

    chunk_buf = pltpu.VMEM((SC_CH, width), table.dtype)
    return pl.kernel(
        body, out_type=jax.ShapeDtypeStruct((rows, width), table.dtype), mesh=_sc_mesh(),
        scratch_types=[pltpu.VMEM((per_w,), jnp.int32), chunk_buf, chunk_buf,
                       pltpu.SemaphoreType.DMA, pltpu.SemaphoreType.DMA],
        name="sc_gather_rows",
    )(table, idx)


def _blend_kernel(x1_ref, w_ref, g0_ref, g1_ref, o_ref):
    half = x1_ref.shape[1] // 2
    a0, b0 = _unpack_rows(g0_ref[...])
    a1, b1 = _unpack_rows(g1_ref[...])
    w0, w1 = w_ref[:, 0:1], w_ref[:, 1:2]
    o_ref[:, 0:half] = x1_ref[:, 0:half] + w0 * a0 + w1 * a1
    o_ref[:, half:] = x1_ref[:, half:] + w0 * b0 + w1 * b1


def _blend(x1, w_col, gathered):
    n = x1.shape[0]
    tt = DSP_TT
    nt = n // tt
    width = gathered.shape[1]
    return pl.pallas_call(
        _blend_kernel,
        grid=(nt,),
        in_specs=[pl.BlockSpec((tt, D_MODEL), lambda i: (i, 0)),
                  pl.BlockSpec((tt, EXPERT_TOP_K), lambda i: (i, 0)),
                  pl.BlockSpec((tt, width), lambda i: (i, 0)),
                  pl.BlockSpec((tt, width), lambda i: (nt + i, 0))],
        out_specs=pl.BlockSpec((tt, D_MODEL), lambda i: (i, 0)),
        out_shape=jax.ShapeDtypeStruct((n, D_MODEL), F32),
        compiler_params=_cparams("parallel"),
        name="moe_blend",
    )(x1, w_col, gathered, gathered)


def _moe(x1, h2, logits_t, w_gate, w_up, w_down, layer):
    n = x1.shape[0]
    e_idx, wts, rank, counts = _route(logits_t)
    counts = counts[:, 0].astype(jnp.int32)
    padded = (counts + MOE_BLK - 1) // MOE_BLK * MOE_BLK
    pends = jnp.cumsum(padded)
    pstarts = pends - padded
    experts = jnp.arange(N_EXPERTS, dtype=jnp.int32)
    start_of = jnp.sum(jnp.where(e_idx[None, :EXPERT_TOP_K] == experts[:, None, None],
                                 pstarts[:, None, None], 0), axis=0)
    dest = start_of + rank[:EXPERT_TOP_K]
    cap = n * EXPERT_TOP_K + N_EXPERTS * MOE_BLK
    n_blk = cap // MOE_BLK
    blk_row = jnp.arange(n_blk, dtype=jnp.int32) * MOE_BLK
    blk_expert = jnp.minimum(jnp.sum((pends[None, :] <= blk_row[:, None]).astype(jnp.int32), axis=1),
                             N_EXPERTS - 1)
    n_used = (pends[-1:] // MOE_BLK).astype(jnp.int32)
    dest_flat = dest.reshape(EXPERT_TOP_K * n)
    buf = _sc_gather_rows(h2, _sc_invert(dest_flat, cap, n))
    out_buf = _experts(blk_expert, n_used, buf, w_gate, w_up, w_down, layer)
    return _blend(x1, wts[:EXPERT_TOP_K].T, _sc_gather_rows(out_buf, dest_flat))


def kernel(x, positions, norm_mix, norm_ffn, w_in, q_gain, k_gain, cmp_k_pos, cmp_k_w1, cmp_k_w2, cmp_v_pos,
           cmp_v_w1, cmp_v_w2, s5_a_re, s5_a_im, s5_log_dt, s5_b_re, s5_b_im, s5_c_re, s5_c_im, s5_d, s5_glu,
           proj_nsa, proj_s5, w_out, router_group, router_group_bias, router_expert, router_expert_bias,
           expert_gate, expert_up, expert_down):
    batch, seq, _ = x.shape
    depth = w_in.shape[0]
    n = batch * seq
    inv_freq = ROPE_THETA ** (-jnp.arange(0, ROPE_DIMS, 2, dtype=jnp.float32) / ROPE_DIMS)
    ctab, stab = _rope_tables(positions.reshape(n, 1), inv_freq)
    ccmp, scmp = _cmp_rope_tables(ctab, stab, batch, seq)
    x2 = x.reshape(n, D_MODEL)
    for l in range(depth):
        w1, w_ga, w_gs = _pack_w_in(w_in[l])
        o_a, u_tm = _nsa_branch(x2, norm_mix[l][None], w1, ctab, stab, ccmp, scmp, q_gain[l], k_gain[l],
                                (cmp_k_pos[l], cmp_k_w1[l], cmp_k_w2[l]),
                                (cmp_v_pos[l], cmp_v_w1[l], cmp_v_w2[l]), batch, seq)
        lre, lim, w_b, w_c = _s5_weights(s5_a_re[l], s5_a_im[l], s5_log_dt[l], s5_b_re[l], s5_b_im[l],
                                         s5_c_re[l], s5_c_im[l])
        ob_tm = _s5(u_tm.reshape(seq * batch, S5_WIDTH), lre, lim, w_b, w_c, s5_d[l][None],
                    s5_glu[l].astype(BF16), batch, seq)
        wrh, wrl, rb = _router_weights(router_group[l], router_group_bias[l], router_expert[l],
                                       router_expert_bias[l])
        x1, h2, logits_t = _merge(x2, norm_mix[l][None], w_ga, w_gs, o_a, ob_tm, proj_nsa[l].astype(BF16),
                                  proj_s5[l].astype(BF16), w_out[l].astype(BF16), norm_ffn[l][None],
                                  wrh, wrl, rb, batch, seq)
        x2 = _moe(x1, h2, logits_t, expert_gate, expert_up, expert_down, l)
    return x2.reshape(batch, seq, D_MODEL)


def _attn_tables(seq):
    n_cmp = (seq - CMP_BLOCK) // CMP_STRIDE + 1
    n_slc = seq // SLC_BLOCK
    key_blk = np.arange(seq) // SLC_BLOCK
    expand_t = (key_blk[:, None] == np.arange(LANES)[None, :]).astype(np.float32)
    cmp_start = np.arange(LANES) * CMP_STRIDE
    slc_start = np.arange(LANES) * SLC_BLOCK
    overlap_t = ((cmp_start[None, :] < slc_start[:, None] + SLC_BLOCK)
                 & (cmp_start[None, :] + CMP_BLOCK > slc_start[:, None])
                 & (np.arange(LANES)[None, :] < n_cmp) & (np.arange(LANES)[:, None] < n_slc))
    return (jnp.asarray(expand_t, BF16), jnp.asarray(overlap_t.astype(np.float32), BF16), n_cmp, n_slc)


def _attn_mask_tables():
    tq, tk, wk = ATT_TQ, ATT_TK, ATT_WK
    ql = np.arange(tq)
    onehot = np.tile(np.eye(tq, dtype=np.float32), (NSA_HEADS, 1))
    k = np.arange(wk)[:, None]
    band = []
    for c in range(WINDOW // tq + 1):
        if c < WINDOW // tq:
            keep = k <= c * tq + ql[None, :]
        else:
            keep = (k > ql[None, :]) & (k <= ql[None, :] + WINDOW)
        band.append(np.where(keep, 0.0, NEG_BIG))
    kl = np.arange(tk)[None, :]
    causal = [np.where(kl <= c * tq + ql[:, None], 0.0, NEG_BIG) for c in range(tk // tq)]
    return (jnp.asarray(onehot, BF16), jnp.asarray(np.stack(band), BF16),
            jnp.asarray(np.stack(causal), F32))


def _attention(q, ks, vs, kw, vw, kc, vc, gates, batch, seq):
    expand_t, overlap_t, n_cmp, n_slc = _attn_tables(seq)
    onehot, band, causal = _attn_mask_tables()
    tq = ATT_TQ
    nq = seq // tq
    rows = NSA_HEADS * tq
    tile = lambda w: pl.BlockSpec((tq, w), lambda b, i: (b * nq + i, 0))
    per_b = lambda r: pl.BlockSpec((r, LANES), lambda b, i: (b, 0))
    f32_rows = pltpu.VMEM((rows, LANES), F32)
    return pl.pallas_call(
        functools.partial(_attn_kernel, n_cmp=n_cmp, n_slc=n_slc, top_k=min(SLC_TOPK, n_slc)),
        grid=(batch, nq),
        in_specs=[tile(2 * NSA_WIDTH), per_b(seq), per_b(seq), per_b(seq), per_b(seq),
                  per_b(LANES), per_b(LANES), tile(LANES),
                  pl.BlockSpec(expand_t.shape, lambda b, i: (0, 0)),
                  pl.BlockSpec(overlap_t.shape, lambda b, i: (0, 0)),
                  pl.BlockSpec(onehot.shape, lambda b, i: (0, 0)),
                  pl.BlockSpec(band.shape, lambda b, i: (0, 0, 0)),
                  pl.BlockSpec(causal.shape, lambda b, i: (0, 0, 0))],
        out_specs=tile(NSA_WIDTH),
        out_shape=jax.ShapeDtypeStruct((batch * seq, NSA_WIDTH), BF16),
        scratch_shapes=[pltpu.VMEM((rows, 2 * LANES), BF16),
                        pltpu.VMEM((rows, ATT_TK), F32), pltpu.VMEM((rows, ATT_TK), BF16),
                        pltpu.VMEM((rows, ATT_WK), F32), pltpu.VMEM((rows, ATT_WK), BF16),
                        f32_rows, f32_rows, f32_rows, f32_rows, f32_rows],
        compiler_params=_cparams("parallel", "parallel"),
        name="nsa_attention",
    )(q, ks, vs, kw, vw, kc, vc, gates, expand_t, overlap_t, onehot, band, causal)
```

```python
import functools
import math

import jax
import jax.numpy as jnp
import numpy as np
from jax import lax
from jax.experimental import pallas as pl
from jax.experimental.pallas import tpu as pltpu
from jax.experimental.pallas import tpu_sc as plsc

F32 = jnp.float32
BF16 = jnp.bfloat16

D_MODEL = 1024
NSA_HEADS = 8
NSA_KV_HEADS = 2
NSA_GROUP = NSA_HEADS // NSA_KV_HEADS
HEAD_DIM = 64
ROPE_DIMS = HEAD_DIM // 4
ROPE_THETA = 500000.0
CMP_BLOCK = 32
CMP_STRIDE = 16
CMP_HIDDEN = 2 * HEAD_DIM
SLC_BLOCK = 64
SLC_TOPK = 16
WINDOW = 512
NSA_WIDTH = NSA_HEADS * HEAD_DIM
KV_WIDTH = NSA_KV_HEADS * HEAD_DIM
S5_WIDTH = 256
S5_GROUP_CH = 16
S5_GROUPS = S5_WIDTH // S5_GROUP_CH
S5_STATE = 64
S5_NSTATE = S5_GROUPS * S5_STATE
N_EXPERT_GROUPS = 4
EXPERTS_PER_GROUP = 8
N_EXPERTS = N_EXPERT_GROUPS * EXPERTS_PER_GROUP
EXPERT_TOP_K = 2
EXPERT_FF = 512
RMS_EPS = 1e-6
NEG_BIG = -1e30

LANES = 128
SUBLANES = 8
VMEM_LIMIT = 56 * 1024 * 1024

IN_TM = 512
ATT_TQ = 128
ATT_TK = 512
ATT_WK = WINDOW + ATT_TQ
ATT_RB = 128
S5_T = 128
MRG_TM = 512
RT_TT = 512
MOE_BLK = 512
DSP_TT = 1024

C_Q, C_KS, C_KW, C_KC, C_VC, C_VS, C_VW, C_U, C_G = 0, 512, 640, 768, 896, 1024, 1152, 1280, 1536
IN_W = 1664


def _cparams(*sem):
    return pltpu.CompilerParams(dimension_semantics=sem, vmem_limit_bytes=VMEM_LIMIT)


def _gelu(x):
    return 0.5 * x * (1.0 + jnp.tanh(math.sqrt(2.0 / math.pi) * (x + 0.044715 * (x * x * x))))


def _sigmoid(x):
    return 1.0 / (1.0 + jnp.exp(-x))


def _dot(a, b):
    return jnp.dot(a, b, preferred_element_type=F32)


def _dot_t(a, b):
    return lax.dot_general(a, b, (((1,), (1,)), ((), ())), preferred_element_type=F32)


def _split_bf16(x):
    hi = x.astype(BF16)
    lo = (x - hi.astype(F32)).astype(BF16)
    return hi, lo


def _pack_rows(x):
    k = x.shape[1] // 2
    hi = lax.bitcast_convert_type(x[:, :k].astype(jnp.bfloat16).astype(F32), jnp.int32)
    lo = lax.bitcast_convert_type(x[:, k:].astype(jnp.bfloat16).astype(F32), jnp.int32)
    return hi | lax.shift_right_logical(lo, 16)


def _unpack_rows(w):
    a = lax.bitcast_convert_type(w & jnp.int32(-65536), F32)
    b = lax.bitcast_convert_type(lax.shift_left(w, 16), F32)
    return a, b


def _rope_kernel(pos_ref, freq_ref, sign_ref, c_ref, s_ref):
    ang = pos_ref[...].astype(F32) * freq_ref[...]
    c_ref[...] = jnp.cos(ang)
    s_ref[...] = jnp.sin(ang) * sign_ref[...]


def _rope_tables(pos_col, inv_freq):
    n = pos_col.shape[0]
    lane = np.arange(LANES) % HEAD_DIM
    sel = np.zeros((LANES, ROPE_DIMS // 2), np.float32)
    for l in range(LANES):
        if lane[l] < ROPE_DIMS:
            sel[l, lane[l] % (ROPE_DIMS // 2)] = 1.0
    freq = (jnp.asarray(sel) * inv_freq[None, :]).sum(-1)[None, :]
    sign = np.where(lane < ROPE_DIMS // 2, -1.0, 1.0).astype(np.float32)[None, :]
    tm = 1024
    return pl.pallas_call(
        _rope_kernel,
        grid=(n // tm,),
        in_specs=[pl.BlockSpec((tm, 1), lambda i: (i, 0)),
                  pl.BlockSpec((1, LANES), lambda i: (0, 0)),
                  pl.BlockSpec((1, LANES), lambda i: (0, 0))],
        out_specs=[pl.BlockSpec((tm, LANES), lambda i: (i, 0))] * 2,
        out_shape=[jax.ShapeDtypeStruct((n, LANES), F32)] * 2,
        compiler_params=_cparams("parallel"),
        name="rope_tables",
    )(pos_col, freq, jnp.asarray(sign))


def _headnorm_rope(t, gain, c, s):
    lane = lax.broadcasted_iota(jnp.int32, t.shape, 1)
    lo = lane < HEAD_DIM
    sq = t * t
    s_lo = jnp.sum(jnp.where(lo, sq, 0.0), axis=-1, keepdims=True)
    s_hi = jnp.sum(jnp.where(lo, 0.0, sq), axis=-1, keepdims=True)
    ms = jnp.where(lo, s_lo, s_hi) * (1.0 / HEAD_DIM)
    tn = (t * lax.rsqrt(ms + RMS_EPS)) * gain
    half = ROPE_DIMS // 2
    partner = jnp.where((lane & (HEAD_DIM - 1)) < half,
                        pltpu.roll(tn, LANES - half, 1), pltpu.roll(tn, half, 1))
    return tn * c + partner * s


def _in_proj_kernel(x_ref, gain_ref, w_ref, c_ref, s_ref, qg_ref, kg_ref,
                    q_ref, ks_ref, kw_ref, kc_ref, vc_ref, vs_ref, vw_ref, u_ref, g_ref):
    x = x_ref[...]
    ms = jnp.mean(x * x, axis=-1, keepdims=True)
    h = (x * lax.rsqrt(ms + RMS_EPS)) * gain_ref[...]
    z = _dot(h.astype(BF16), w_ref[...])
    c = c_ref[...]
    s = s_ref[...]
    lane = lax.broadcasted_iota(jnp.int32, c.shape, 1)
    lo = lane < HEAD_DIM
    scale = HEAD_DIM ** -0.5 * math.log2(math.e)
    for j in range(NSA_HEADS // 2):
        r = _headnorm_rope(z[:, C_Q + LANES * j:C_Q + LANES * (j + 1)], qg_ref[...], c, s) * scale
        rr = pltpu.roll(r, HEAD_DIM, 1)
        if j // (NSA_GROUP // 2) == 0:
            even, odd = jnp.where(lo, r, 0.0), jnp.where(lo, rr, 0.0)
        else:
            even, odd = jnp.where(lo, 0.0, rr), jnp.where(lo, 0.0, r)
        q_ref[:, 2 * LANES * j:2 * LANES * j + LANES] = even.astype(BF16)
        q_ref[:, 2 * LANES * j + LANES:2 * LANES * (j + 1)] = odd.astype(BF16)
    ks_ref[...] = _headnorm_rope(z[:, C_KS:C_KS + LANES], kg_ref[...], c, s).astype(BF16)
    kw_ref[...] = _headnorm_rope(z[:, C_KW:C_KW + LANES], kg_ref[...], c, s).astype(BF16)
    kc_ref[...] = z[:, C_KC:C_KC + LANES].astype(BF16)
    vc_ref[...] = z[:, C_VC:C_VC + LANES].astype(BF16)
    vs_ref[...] = z[:, C_VS:C_VS + LANES].astype(BF16)
    vw_ref[...] = z[:, C_VW:C_VW + LANES].astype(BF16)
    u_ref[:, pl.ds(pl.program_id(1), 1), :] = z[:, C_U:C_U + S5_WIDTH][:, None, :]
    g_ref[...] = _sigmoid(z[:, C_G:C_G + LANES])


def _in_proj(x2, gain, w1, ctab, stab, qg, kg, batch, seq):
    n = x2.shape[0]
    tm = IN_TM
    nt = seq // tm
    row = lambda i, b: (b * nt + i, 0)
    const = lambda i, b: (0, 0)
    tok = lambda w: pl.BlockSpec((tm, w), row)
    outs = [(n, 2 * NSA_WIDTH, BF16)] + [(n, LANES, BF16)] * 6
    return pl.pallas_call(
        _in_proj_kernel,
        grid=(nt, batch),
        in_specs=[tok(D_MODEL), pl.BlockSpec((1, D_MODEL), const), pl.BlockSpec((D_MODEL, IN_W), const),
                  tok(LANES), tok(LANES), pl.BlockSpec((1, LANES), const), pl.BlockSpec((1, LANES), const)],
        out_specs=[tok(2 * NSA_WIDTH)] + [tok(LANES)] * 6
        + [pl.BlockSpec((tm, batch, S5_WIDTH), lambda i, b: (i, 0, 0)), tok(LANES)],
        out_shape=[jax.ShapeDtypeStruct(s[:2], s[2]) for s in outs]
        + [jax.ShapeDtypeStruct((seq, batch, S5_WIDTH), F32), jax.ShapeDtypeStruct((n, LANES), F32)],
        compiler_params=_cparams("parallel", "arbitrary"),
        name="in_proj",
    )(x2, gain, w1, ctab, stab, qg, kg)


def _compress_mlp(h_ref, w_ref, p_ref, w2_ref):
    w = w_ref[...]
    pre = _dot(h_ref[...], w)
    phi, plo = _split_bf16(p_ref[...])
    pb = _dot(phi, w) + _dot(plo, w)
    rows = pre.shape[0]
    out = None
    for h in range(NSA_KV_HEADS):
        top = pre[:, LANES * h:LANES * (h + 1)]
        bot = pre[:, LANES * (2 + h):LANES * (3 + h)]
        bias = pb[0:1, LANES * h:LANES * (h + 1)] + pb[1:2, LANES * (2 + h):LANES * (3 + h)]
        a = _gelu(top + pltpu.roll(bot, rows - 1, 0) + bias)
        o = _dot(a.astype(BF16), w2_ref[h])
        out = o if out is None else out + o
    return out


def _compress_kernel(hk_ref, hv_ref, wk_ref, wv_ref, pk_ref, pv_ref, w2k_ref, w2v_ref,
                     kg_ref, c_ref, s_ref, kc_ref, vc_ref):
    k = _compress_mlp(hk_ref, wk_ref, pk_ref, w2k_ref)
    kc_ref[...] = _headnorm_rope(k, kg_ref[...], c_ref[...], s_ref[...]).astype(BF16)
    vc_ref[...] = _compress_mlp(hv_ref, wv_ref, pv_ref, w2v_ref).astype(BF16)


def _compress(hk, hv, wk, wv, pk, pv, w2k, w2v, kg, ccmp, scmp, batch):
    rows = hk.shape[0] // batch
    width = hk.shape[1]
    const2 = lambda b: (0, 0)
    const3 = lambda b: (0, 0, 0)
    rowb = lambda w: pl.BlockSpec((rows, w), lambda b: (b, 0))
    return pl.pallas_call(
        _compress_kernel,
        grid=(batch,),
        in_specs=[rowb(width), rowb(width),
                  pl.BlockSpec(wk.shape, const2), pl.BlockSpec(wv.shape, const2),
                  pl.BlockSpec(pk.shape, const2), pl.BlockSpec(pv.shape, const2),
                  pl.BlockSpec(w2k.shape, const3), pl.BlockSpec(w2v.shape, const3),
                  pl.BlockSpec((1, LANES), const2), rowb(LANES), rowb(LANES)],
        out_specs=[rowb(LANES), rowb(LANES)],
        out_shape=[jax.ShapeDtypeStruct((hk.shape[0], LANES), BF16)] * 2,
        compiler_params=_cparams("parallel"),
        name="compress_kv",
    )(hk, hv, wk, wv, pk, pv, w2k, w2v, kg, ccmp, scmp)


def _compress_weights(pos_emb, w1, w2):
    half = CMP_BLOCK // 2
    eye = jnp.eye(NSA_KV_HEADS, dtype=F32)
    w1r = w1.reshape(2, half, HEAD_DIM, CMP_HIDDEN)
    wexp = jnp.einsum("sldn,ph->lpdshn", w1r, eye)
    wexp = wexp.reshape(half * KV_WIDTH, 2 * NSA_KV_HEADS * CMP_HIDDEN).astype(BF16)
    pexp = jnp.broadcast_to(pos_emb.reshape(2, half, 1, HEAD_DIM), (2, half, NSA_KV_HEADS, HEAD_DIM))
    pexp = jnp.pad(pexp.reshape(2, half * KV_WIDTH), ((0, SUBLANES - 2), (0, 0)))
    w2e = jnp.einsum("cd,ph->pchd", w2, eye).reshape(NSA_KV_HEADS, CMP_HIDDEN, KV_WIDTH)
    return wexp, pexp, w2e.astype(BF16)


def _attn_kernel(q_ref, ks_ref, vs_ref, kw_ref, vw_ref, kc_ref, vc_ref, g_ref, et_ref, ovt_ref, oh_ref, wb_ref,
                 cb_ref, o_ref, qc_ref, s_ref, p_ref, sw_ref, pw_ref, m_ref, al_ref, oc_ref, as_ref, aw_ref,
                 *, n_cmp, n_slc, top_k):
    tq, tk, wk, rb, nh = ATT_TQ, ATT_TK, ATT_WK, ATT_RB, NSA_HEADS
    rows, grp_rows = nh * tq, NSA_GROUP * tq
    q0 = pl.program_id(1) * tq
    for a in range(nh):
        qc_ref[a * tq:(a + 1) * tq, 0:LANES] = q_ref[:, LANES * a:LANES * (a + 1)]
    q8 = qc_ref[:, 0:LANES]
    lane = lax.broadcasted_iota(jnp.int32, (tq, LANES), 1)
    t_row = q0 + lax.broadcasted_iota(jnp.int32, (tq, LANES), 0)

    s_c = _dot_t(q8, kc_ref[...]).reshape(nh, tq, LANES)
    mask_c = ((lane * CMP_STRIDE + (CMP_BLOCK - 1)) <= t_row) & (lane < n_cmp)
    sm = jnp.where(mask_c[None], s_c, NEG_BIG)
    p = jnp.where(mask_c[None], jnp.exp2(sm - jnp.max(sm, axis=-1, keepdims=True)), 0.0)
    l = jnp.sum(p, axis=-1, keepdims=True)
    p_c = p * jnp.where(l > 0.0, 1.0 / l, 0.0)
    oc_ref[...] = _dot(p_c.reshape(rows, LANES).astype(BF16), vc_ref[...])

    blk = lax.broadcasted_iota(jnp.int32, (n_slc, tq), 0)
    cur = (q0 + lax.broadcasted_iota(jnp.int32, (n_slc, tq), 1)) >> int(math.log2(SLC_BLOCK))
    forced = (blk == 0) | (blk == cur) | (blk == cur - 1)
    for h in range(NSA_KV_HEADS):
        ps = p_c[NSA_GROUP * h]
        for g in range(1, NSA_GROUP):
            ps = ps + p_c[NSA_GROUP * h + g]
        hi, lo = _split_bf16(ps)
        imp = (_dot_t(ovt_ref[...], hi) + _dot_t(ovt_ref[...], lo))[0:n_slc]
        imp = jnp.where(forced, jnp.inf, jnp.where(blk > cur, -jnp.inf, imp))
        rank = jnp.zeros((n_slc, tq), F32)
        for i in range(n_slc):
            row = imp[i:i + 1, :]
            rank = rank + jnp.where((row > imp) | ((row == imp) & (blk > i)), 1.0, 0.0)
        bias_t = jnp.where(rank < float(top_k), 0.0, NEG_BIG)
        bias = jnp.concatenate([bias_t, jnp.zeros((LANES - n_slc, tq), F32)], axis=0).T.astype(BF16)
        for g in range(NSA_GROUP):
            a = NSA_GROUP * h + g
            qc_ref[a * tq:(a + 1) * tq, LANES:2 * LANES] = bias

    def softmax_pass(s_buf, p_buf, width, causal_case, online):
        for r in range(rows // rb):
            r0 = r * rb
            s = s_buf[pl.ds(r0, rb), 0:width]
            if causal_case is not None:
                s = s + cb_ref[causal_case, pl.ds(r0 % tq, rb), :]
            m_new = jnp.broadcast_to(jnp.max(s, axis=1, keepdims=True), (rb, LANES))
            if online:
                m_old = m_ref[pl.ds(r0, rb), :]
                m_new = jnp.maximum(m_old, m_new)
                al_ref[pl.ds(r0, rb), :] = jnp.exp2(m_old - m_new)
                m_ref[pl.ds(r0, rb), :] = m_new
            pp = jnp.exp2(s - jnp.concatenate([m_new] * (width // LANES), axis=1))
            p_buf[pl.ds(r0, rb), 0:width] = pp.astype(BF16)

    def pv(acc_ref, p_buf, v, width, online):
        lane_v = lax.broadcasted_iota(jnp.int32, v.shape, 1)
        for h in range(NSA_KV_HEADS):
            rs = slice(h * grp_rows, (h + 1) * grp_rows)
            own = (lane_v < HEAD_DIM) if h == 0 else (lane_v >= HEAD_DIM)
            upd = _dot(p_buf[rs, 0:width], jnp.where(own, v, jnp.ones_like(v)))
            acc_ref[rs, :] = (al_ref[rs, :] * acc_ref[rs, :] + upd) if online else upd

    m_ref[...] = jnp.full(m_ref.shape, 0.1 * NEG_BIG, F32)
    as_ref[...] = jnp.zeros(as_ref.shape, F32)

    def slc_chunk(c, diagonal):
        k0 = pl.multiple_of(c * tk, tk)
        kcat = jnp.concatenate([ks_ref[pl.ds(k0, tk), :], et_ref[pl.ds(k0, tk), :]], axis=1)
        s_ref[...] = _dot_t(qc_ref[...], kcat)
        softmax_pass(s_ref, p_ref, tk, ((q0 - k0) >> int(math.log2(tq))) if diagonal else None, True)
        pv(as_ref, p_ref, vs_ref[pl.ds(k0, tk), :], tk, True)

    c_last = q0 >> int(math.log2(tk))

    def full_chunk(c, carry):
        slc_chunk(c, False)
        return carry

    lax.fori_loop(0, c_last, full_chunk, 0)
    slc_chunk(c_last, True)

    w0 = pl.multiple_of(jnp.maximum(q0 - WINDOW, 0), LANES)
    w_case = jnp.minimum(q0 >> int(math.log2(tq)), WINDOW // tq)
    kcat_w = jnp.concatenate([kw_ref[pl.ds(w0, wk), :], wb_ref[w_case]], axis=1)
    sw_ref[...] = _dot_t(jnp.concatenate([q8, oh_ref[...]], axis=1), kcat_w)
    softmax_pass(sw_ref, pw_ref, wk, None, False)
    pv(aw_ref, pw_ref, vw_ref[pl.ds(w0, wk), :], wk, False)

    lo_half = lane < HEAD_DIM
    for j in range(nh // 2):
        comb = []
        for a in (2 * j, 2 * j + 1):
            rs = slice(a * tq, (a + 1) * tq)
            o_s, o_w = as_ref[rs, :], aw_ref[rs, :]
            comb.append(g_ref[:, 3 * a:3 * a + 1] * oc_ref[rs, :]
                        + g_ref[:, 3 * a + 1:3 * a + 2] * (o_s * (1.0 / pltpu.roll(o_s, HEAD_DIM, 1)))
                        + g_ref[:, 3 * a + 2:3 * a + 3] * (o_w * (1.0 / pltpu.roll(o_w, HEAD_DIM, 1))))
        even, odd = comb
        if j // (NSA_GROUP // 2) == 0:
            pair = jnp.where(lo_half, even, pltpu.roll(odd, HEAD_DIM, 1))
        else:
            pair = jnp.where(lo_half, pltpu.roll(even, HEAD_DIM, 1), odd)
        o_ref[:, LANES * j:LANES * (j + 1)] = pair.astype(BF16)


def _pack_w_in(w):
    sizes = (NSA_WIDTH,) + (KV_WIDTH,) * 6 + (3 * NSA_HEADS, S5_WIDTH, D_MODEL, D_MODEL)
    o = np.cumsum((0,) + sizes)
    q, kc, vc, ks, vs, kw, vw, g, u, ga, gs = [w[:, o[i]:o[i + 1]] for i in range(len(sizes))]
    g = jnp.pad(g, ((0, 0), (0, LANES - 3 * NSA_HEADS)))
    w1 = jnp.concatenate([q, ks, kw, kc, vc, vs, vw, u, g], axis=1).astype(BF16)
    return w1, ga.astype(BF16), gs.astype(BF16)


def _nsa_branch(x2, gain, w1, ctab, stab, ccmp, scmp, q_gain, k_gain, cmp_k, cmp_v, batch, seq):
    qg = jnp.tile(q_gain, 2)[None, :]
    kg = jnp.tile(k_gain, 2)[None, :]
    q, ks, kw, kc_raw, vc_raw, vs, vw, u_tm, gates = _in_proj(x2, gain, w1, ctab, stab, qg, kg, batch, seq)
    half_rows = batch * seq // (CMP_BLOCK // 2)
    hk = kc_raw.reshape(half_rows, (CMP_BLOCK // 2) * KV_WIDTH)
    hv = vc_raw.reshape(half_rows, (CMP_BLOCK // 2) * KV_WIDTH)
    wk, pk, w2k = _compress_weights(*cmp_k)
    wv, pv, w2v = _compress_weights(*cmp_v)
    kc, vc = _compress(hk, hv, wk, wv, pk, pv, w2k, w2v, kg, ccmp, scmp, batch)
    o_a = _attention(q, ks, vs, kw, vw, kc, vc, gates, batch, seq)
    return o_a, u_tm


def _cmp_rope_tables(ctab, stab, batch, seq):
    def pick(t):
        t = t.reshape(batch, seq, LANES)[:, CMP_BLOCK - 1::CMP_STRIDE]
        t = jnp.pad(t, ((0, 0), (0, seq // CMP_STRIDE - t.shape[1]), (0, 0)))
        return t.reshape(batch * (seq // CMP_STRIDE), LANES)
    return pick(ctab), pick(stab)


def _s5_disc_kernel(are_ref, aim_ref, ldt_ref, lre_ref, lim_ref, fre_ref, fim_ref):
    lam_re = jnp.minimum(are_ref[...], -1e-4)
    lam_im = aim_ref[...]
    dt = jnp.exp(ldt_ref[...])
    mag = jnp.exp(lam_re * dt)
    ang = lam_im * dt
    lb_re = mag * jnp.cos(ang)
    lb_im = mag * jnp.sin(ang)
    den = lam_re * lam_re + lam_im * lam_im
    lre_ref[...] = lb_re
    lim_ref[...] = lb_im
    fre_ref[...] = ((lb_re - 1.0) * lam_re + lb_im * lam_im) / den
    fim_ref[...] = (lb_im * lam_re - (lb_re - 1.0) * lam_im) / den


def _s5_weights(a_re, a_im, log_dt, b_re, b_im, c_re, c_im):
    shp = jax.ShapeDtypeStruct((S5_GROUPS, S5_STATE), F32)
    lb_re, lb_im, f_re, f_im = pl.pallas_call(
        _s5_disc_kernel, out_shape=[shp] * 4, name="s5_discretise",
    )(a_re, a_im, log_dt[:, None])
    bb_re = f_re[..., None] * b_re - f_im[..., None] * b_im
    bb_im = f_re[..., None] * b_im + f_im[..., None] * b_re
    eye = jnp.eye(S5_GROUPS, dtype=F32)

    def in_map(bb):
        return jnp.einsum("gpc,gh->gchp", bb, eye).reshape(S5_WIDTH, S5_NSTATE)

    def out_map(cc):
        return jnp.einsum("gcp,gh->hpgc", cc, eye).reshape(S5_NSTATE, S5_WIDTH)

    w_b = jnp.concatenate([in_map(bb_re), in_map(bb_im)], axis=1).astype(BF16)
    w_c = jnp.concatenate([out_map(c_re), -out_map(c_im)], axis=0).astype(BF16)
    return lb_re.reshape(1, S5_NSTATE), lb_im.reshape(1, S5_NSTATE), w_b, w_c


def _s5_kernel(u_ref, wb_ref, lre_ref, lim_ref, wc_ref, d_ref, wg_ref, o_ref, x_ref, st_ref, *, batch):
    ns = S5_NSTATE

    @pl.when(pl.program_id(0) == 0)
    def _():
        st_ref[...] = jnp.zeros(st_ref.shape, F32)

    u = u_ref[...].reshape(u_ref.shape[0] * batch, S5_WIDTH)
    x_ref[...] = _dot(u.astype(BF16), wb_ref[...])
    lre = jnp.broadcast_to(lre_ref[...], (batch, ns))
    lim = jnp.broadcast_to(lim_ref[...], (batch, ns))

    def step(t, carry):
        xr, xi = carry
        r0 = pl.multiple_of(t * batch, batch)
        nr = lre * xr - lim * xi + x_ref[pl.ds(r0, batch), 0:ns]
        ni = lre * xi + lim * xr + x_ref[pl.ds(r0, batch), ns:2 * ns]
        x_ref[pl.ds(r0, batch), 0:ns] = nr
        x_ref[pl.ds(r0, batch), ns:2 * ns] = ni
        return nr, ni

    steps = u.shape[0] // batch
    xr, xi = lax.fori_loop(0, steps, step, (st_ref[:, 0:ns], st_ref[:, ns:2 * ns]), unroll=4)
    st_ref[:, 0:ns] = xr
    st_ref[:, ns:2 * ns] = xi
    y = _dot(x_ref[...].astype(BF16), wc_ref[...]) + d_ref[...] * u
    z = _dot(_gelu(y).astype(BF16), wg_ref[...])
    o_ref[...] = (z[:, :S5_WIDTH] * _sigmoid(z[:, S5_WIDTH:])).reshape(o_ref.shape)


def _s5(u_tm, lre, lim, w_b, w_c, d_skip, w_glu, batch, seq):
    rows = S5_T * batch
    const = lambda t: (0, 0)
    return pl.pallas_call(
        functools.partial(_s5_kernel, batch=batch),
        grid=(seq // S5_T,),
        in_specs=[pl.BlockSpec((S5_T, batch, S5_WIDTH), lambda t: (t, 0, 0)),
                  pl.BlockSpec(w_b.shape, const), pl.BlockSpec(lre.shape, const), pl.BlockSpec(lim.shape, const),
                  pl.BlockSpec(w_c.shape, const), pl.BlockSpec((1, S5_WIDTH), const),
                  pl.BlockSpec(w_glu.shape, const)],
        out_specs=pl.BlockSpec((S5_T, batch, S5_WIDTH), lambda t: (t, 0, 0)),
        out_shape=jax.ShapeDtypeStruct((seq, batch, S5_WIDTH), F32),
        scratch_shapes=[pltpu.VMEM((rows, 2 * S5_NSTATE), F32), pltpu.VMEM((batch, 2 * S5_NSTATE), F32)],
        compiler_params=_cparams("arbitrary"),
        name="s5_scan",
    )(u_tm, w_b, lre, lim, w_c, d_skip, w_glu)


RT_GROUP_ROW = 0
RT_EXPERT_ROW = SUBLANES


def _rmsnorm(x, gain):
    return (x * lax.rsqrt(jnp.mean(x * x, axis=-1, keepdims=True) + RMS_EPS)) * gain


def _merge_kernel(x_ref, g1_ref, wga_ref, wgs_ref, oa_ref, ob_ref, pa_ref, ps_ref, wo_ref, g2_ref,
                  wrh_ref, wrl_ref, rb_ref, x1_ref, h2_ref, lg_ref):
    x = x_ref[...]
    h = _rmsnorm(x, g1_ref[...]).astype(BF16)
    gate_a = _sigmoid(_dot(h, wga_ref[...]))
    gate_s = _sigmoid(_dot(h, wgs_ref[...]))
    o_b = ob_ref[:, pl.ds(pl.program_id(1), 1), :].reshape(x.shape[0], S5_WIDTH)
    merged = gate_a * _dot(oa_ref[...], pa_ref[...]) + gate_s * _dot(o_b.astype(BF16), ps_ref[...])
    x1 = x + _dot(merged.astype(BF16), wo_ref[...])
    x1_ref[...] = x1
    h2 = _rmsnorm(x1, g2_ref[...])
    h2_ref[...] = _pack_rows(h2)
    hi, lo = _split_bf16(h2)
    lg_ref[...] = (_dot_t(wrh_ref[...], hi) + _dot_t(wrh_ref[...], lo) + _dot_t(wrl_ref[...], hi)
                   + rb_ref[...])


def _router_weights(w_rg, b_rg, w_re, b_re):
    def rows(group_part, expert_part):
        return jnp.concatenate([
            group_part, jnp.zeros((RT_EXPERT_ROW - N_EXPERT_GROUPS,) + group_part.shape[1:], F32),
            expert_part, jnp.zeros((LANES - RT_EXPERT_ROW - N_EXPERTS,) + group_part.shape[1:], F32)], axis=0)

    hi, lo = _split_bf16(rows(w_rg.T, w_re.T))
    return hi, lo, rows(b_rg[:, None], b_re[:, None])


def _merge(x2, g1, w_ga, w_gs, o_a, ob_tm, p_a, p_s, w_o, g2, wrh, wrl, rb, batch, seq):
    n = x2.shape[0]
    tm = MRG_TM
    nt = seq // tm
    row = lambda i, b: (b * nt + i, 0)
    const = lambda i, b: (0, 0)
    tok = lambda w: pl.BlockSpec((tm, w), row)
    full = lambda a: pl.BlockSpec(a.shape, const)
    return pl.pallas_call(
        _merge_kernel,
        grid=(nt, batch),
        in_specs=[tok(D_MODEL), full(g1), full(w_ga), full(w_gs), tok(NSA_WIDTH),
                  pl.BlockSpec((tm, batch, S5_WIDTH), lambda i, b: (i, 0, 0)), full(p_a), full(p_s), full(w_o),
                  full(g2), full(wrh), full(wrl), full(rb)],
        out_specs=[tok(D_MODEL), tok(D_MODEL // 2), pl.BlockSpec((LANES, tm), lambda i, b: (0, b * nt + i))],
        out_shape=[jax.ShapeDtypeStruct((n, D_MODEL), F32), jax.ShapeDtypeStruct((n, D_MODEL // 2), jnp.int32),
                   jax.ShapeDtypeStruct((LANES, n), F32)],
        compiler_params=_cparams("parallel", "parallel"),
        name="merge_out_proj",
    )(x2, g1, w_ga, w_gs, o_a, ob_tm, p_a, p_s, w_o, g2, wrh, wrl, rb)


def _first_argmax(v, rows):
    vmax = jnp.max(v, axis=0, keepdims=True)
    idx = jnp.min(jnp.where(v == vmax, rows, SUBLANES), axis=0, keepdims=True)
    return vmax, idx


def _route_kernel(lg_ref, tri_ref, e_ref, w_ref, r_ref, cnt_ref, carry_ref):
    tt = lg_ref.shape[1]

    @pl.when(pl.program_id(0) == 0)
    def _():
        carry_ref[...] = jnp.zeros(carry_ref.shape, F32)

    rows = lax.broadcasted_iota(jnp.int32, (SUBLANES, tt), 0)
    gl = jnp.where(rows < N_EXPERT_GROUPS, lg_ref[RT_GROUP_ROW:RT_GROUP_ROW + SUBLANES, :], -jnp.inf)
    gmax, g_sel = _first_argmax(gl, rows)
    g_w = 1.0 / jnp.sum(jnp.exp(gl - gmax), axis=0, keepdims=True)

    el = jnp.zeros((EXPERTS_PER_GROUP, tt), F32)
    for g in range(N_EXPERT_GROUPS):
        r0 = RT_EXPERT_ROW + EXPERTS_PER_GROUP * g
        el = jnp.where(g_sel == g, lg_ref[r0:r0 + EXPERTS_PER_GROUP, :], el)
    ex = jnp.exp(el - jnp.max(el, axis=0, keepdims=True))
    p = ex / jnp.sum(ex, axis=0, keepdims=True)
    p1, i1 = _first_argmax(p, rows)
    p2, i2 = _first_argmax(jnp.where(rows == i1, -1.0, p), rows)
    e1 = g_sel * EXPERTS_PER_GROUP + i1
    e2 = g_sel * EXPERTS_PER_GROUP + i2
    scale = g_w / (p1 + p2)

    erow = lax.broadcasted_iota(jnp.int32, (N_EXPERTS, tt), 0)
    oh1 = jnp.where(erow == e1, 1.0, 0.0)
    oh2 = jnp.where(erow == e2, 1.0, 0.0)
    both = oh1 + oh2
    before = carry_ref[:, 0:1] + _dot(both.astype(BF16), tri_ref[...])
    carry_ref[...] = carry_ref[...] + jnp.sum(both, axis=1, keepdims=True)
    cnt_ref[...] = carry_ref[...]

    zi = jnp.zeros((SUBLANES - 2, tt), jnp.int32)
    e_ref[...] = jnp.concatenate([e1, e2, zi], axis=0)
    w_ref[...] = jnp.concatenate([p1 * scale, p2 * scale, zi.astype(F32)], axis=0)
    r_ref[...] = jnp.concatenate([jnp.sum(oh1 * before, axis=0, keepdims=True),
                                  jnp.sum(oh2 * before, axis=0, keepdims=True),
                                  zi.astype(F32)], axis=0).astype(jnp.int32)


def _route(logits_t):
    n = logits_t.shape[1]
    tt = RT_TT
    tri = jnp.asarray(np.triu(np.ones((tt, tt), np.float32), 1), BF16)
    blk = pl.BlockSpec((SUBLANES, tt), lambda i: (0, i))
    return pl.pallas_call(
        _route_kernel,
        grid=(n // tt,),
        in_specs=[pl.BlockSpec((LANES, tt), lambda i: (0, i)), pl.BlockSpec((tt, tt), lambda i: (0, 0))],
        out_specs=[blk, blk, blk, pl.BlockSpec((N_EXPERTS, LANES), lambda i: (0, 0))],
        out_shape=[jax.ShapeDtypeStruct((SUBLANES, n), jnp.int32), jax.ShapeDtypeStruct((SUBLANES, n), F32),
                   jax.ShapeDtypeStruct((SUBLANES, n), jnp.int32), jax.ShapeDtypeStruct((N_EXPERTS, LANES), F32)],
        scratch_shapes=[pltpu.VMEM((N_EXPERTS, LANES), F32)],
        compiler_params=_cparams("arbitrary"),
        name="moe_route",
    )(logits_t, tri)


def _expert_kernel(be_ref, nb_ref, x_ref, wg_ref, wu_ref, wd_ref, o_ref, wgb_ref, wub_ref, wdb_ref):
    i = pl.program_id(0)
    half = D_MODEL // 2

    @pl.when(i < nb_ref[0])
    def _():
        @pl.when((i == 0) | (be_ref[i] != be_ref[jnp.maximum(i - 1, 0)]))
        def _():
            wgb_ref[...] = wg_ref[...].astype(BF16)
            wub_ref[...] = wu_ref[...].astype(BF16)
            wdb_ref[...] = wd_ref[...].astype(BF16)

        xa, xb = _unpack_rows(x_ref[...])
        xa, xb = xa.astype(BF16), xb.astype(BF16)
        g = _dot(xa, wgb_ref[0:half, :]) + _dot(xb, wgb_ref[half:, :])
        u = _dot(xa, wub_ref[0:half, :]) + _dot(xb, wub_ref[half:, :])
        o_ref[...] = _pack_rows(_dot(((g * _sigmoid(g)) * u).astype(BF16), wdb_ref[...]))

    @pl.when(i >= nb_ref[0])
    def _():
        o_ref[...] = jnp.zeros(o_ref.shape, o_ref.dtype)


def _experts(blk_expert, n_used, buf, w_gate, w_up, w_down, layer):
    cap, width = buf.shape
    blk = MOE_BLK
    row = lambda i, be, nb: (jnp.minimum(i, nb[0] - 1), 0)
    wmap = lambda i, be, nb: (layer, be[jnp.minimum(i, nb[0] - 1)], 0, 0)
    return pl.pallas_call(
        _expert_kernel,
        grid_spec=pltpu.PrefetchScalarGridSpec(
            num_scalar_prefetch=2,
            grid=(cap // blk,),
            in_specs=[pl.BlockSpec((blk, width), row),
                      pl.BlockSpec((None, None, D_MODEL, EXPERT_FF), wmap),
                      pl.BlockSpec((None, None, D_MODEL, EXPERT_FF), wmap),
                      pl.BlockSpec((None, None, EXPERT_FF, D_MODEL), wmap)],
            out_specs=pl.BlockSpec((blk, width), lambda i, be, nb: (i, 0)),
            scratch_shapes=[pltpu.VMEM((D_MODEL, EXPERT_FF), BF16), pltpu.VMEM((D_MODEL, EXPERT_FF), BF16),
                            pltpu.VMEM((EXPERT_FF, D_MODEL), BF16)],
        ),
        out_shape=jax.ShapeDtypeStruct((cap, width), buf.dtype),
        compiler_params=_cparams("arbitrary"),
        name="moe_experts",
    )(blk_expert, n_used, buf, w_gate, w_up, w_down)


SC_CH = 64


def _sc_mesh():
    return plsc.VectorSubcoreMesh(core_axis_name="c", subcore_axis_name="s")


def _sc_worker(info):
    return lax.axis_index("s") * info.num_cores + lax.axis_index("c")


def _sc_gather_rows(table, idx, rows=None):
    info = plsc.get_sparse_core_info()
    lanes, workers = info.num_lanes, info.num_cores * info.num_subcores
    invert = rows is not None
    rows = rows if invert else idx.shape[0]
    width = table.shape[1]
    per_w = rows // workers
    n_chunks = per_w // SC_CH
    assert per_w % (2 * SC_CH) == 0
    scan = 2048
    n_tok = table.shape[0]

    def body(table_hbm, idx_hbm, out_hbm, idx_v, rows_a, rows_b, sem_a, sem_b, *scan_v):
        base = _sc_worker(info) * per_w
        if invert:
            lane = lax.iota(jnp.int32, lanes)

            @pl.loop(0, per_w // lanes)
            def _(j):
                idx_v[pl.ds(j * lanes, lanes)] = (base + j * lanes + lane) & (n_tok - 1)

            @pl.loop(0, idx.shape[0] // scan)
            def _(c):
                pltpu.sync_copy(idx_hbm.at[pl.ds(c * scan, scan)], scan_v[0])

                @pl.loop(0, scan // lanes)
                def _(j):
                    rel = scan_v[0][pl.ds(j * lanes, lanes)] - base
                    mine = (rel >= 0) & (rel < per_w)
                    token = (c * scan + j * lanes + lane) & (n_tok - 1)
                    plsc.store_scatter(idx_v, [jnp.where(mine, rel, 0)], token, mask=mine)
        else:
            pltpu.sync_copy(idx_hbm.at[pl.ds(base, per_w)], idx_v)

        def gather(j, buf, sem):
            return pltpu.make_async_copy(table_hbm.at[idx_v.at[pl.ds(j * SC_CH, SC_CH)]], buf, sem)

        def write(j, buf):
            pltpu.sync_copy(buf, out_hbm.at[pl.ds(base + j * SC_CH, SC_CH)])

        gather(0, rows_a, sem_a).start()

        @pl.loop(0, n_chunks, step=2)
        def _(j):
            gather(j + 1, rows_b, sem_b).start()
            gather(j, rows_a, sem_a).wait()
            write(j, rows_a)

            @pl.when(j + 2 < n_chunks)
            def _():
                gather(j + 2, rows_a, sem_a).start()

            gather(j + 1, rows_b, sem_b).wait()
            write(j + 1, rows_b)

    chunk_buf = pltpu.VMEM((SC_CH, width), table.dtype)
    scratch = [pltpu.VMEM((per_w,), jnp.int32), chunk_buf, chunk_buf,
               pltpu.SemaphoreType.DMA, pltpu.SemaphoreType.DMA]
    return pl.kernel(
        body, out_type=jax.ShapeDtypeStruct((rows, width), table.dtype), mesh=_sc_mesh(),
        scratch_types=scratch + ([pltpu.VMEM((scan,), jnp.int32)] if invert else []),
        compiler_params=pltpu.CompilerParams(needs_layout_passes=False) if invert else None,
        name="sc_dispatch_rows" if invert else "sc_gather_rows",
    )(table, idx)


def _blend_kernel(x1_ref, w_ref, g0_ref, g1_ref, o_ref):
    half = x1_ref.shape[1] // 2
    a0, b0 = _unpack_rows(g0_ref[...])
    a1, b1 = _unpack_rows(g1_ref[...])
    w0, w1 = w_ref[:, 0:1], w_ref[:, 1:2]
    o_ref[:, 0:half] = x1_ref[:, 0:half] + w0 * a0 + w1 * a1
    o_ref[:, half:] = x1_ref[:, half:] + w0 * b0 + w1 * b1


def _blend(x1, w_col, gathered):
    n = x1.shape[0]
    tt = DSP_TT
    nt = n // tt
    width = gathered.shape[1]
    return pl.pallas_call(
        _blend_kernel,
        grid=(nt,),
        in_specs=[pl.BlockSpec((tt, D_MODEL), lambda i: (i, 0)),
                  pl.BlockSpec((tt, EXPERT_TOP_K), lambda i: (i, 0)),
                  pl.BlockSpec((tt, width), lambda i: (i, 0)),
                  pl.BlockSpec((tt, width), lambda i: (nt + i, 0))],
        out_specs=pl.BlockSpec((tt, D_MODEL), lambda i: (i, 0)),
        out_shape=jax.ShapeDtypeStruct((n, D_MODEL), F32),
        compiler_params=_cparams("parallel"),
        name="moe_blend",
    )(x1, w_col, gathered, gathered)


def _moe(x1, h2, logits_t, w_gate, w_up, w_down, layer):
    n = x1.shape[0]
    e_idx, wts, rank, counts = _route(logits_t)
    counts = counts[:, 0].astype(jnp.int32)
    padded = (counts + MOE_BLK - 1) // MOE_BLK * MOE_BLK
    pends = jnp.cumsum(padded)
    pstarts = pends - padded
    experts = jnp.arange(N_EXPERTS, dtype=jnp.int32)
    start_of = jnp.sum(jnp.where(e_idx[None, :EXPERT_TOP_K] == experts[:, None, None],
                                 pstarts[:, None, None], 0), axis=0)
    dest = start_of + rank[:EXPERT_TOP_K]
    cap = n * EXPERT_TOP_K + N_EXPERTS * MOE_BLK
    n_blk = cap // MOE_BLK
    blk_row = jnp.arange(n_blk, dtype=jnp.int32) * MOE_BLK
    blk_expert = jnp.minimum(jnp.sum((pends[None, :] <= blk_row[:, None]).astype(jnp.int32), axis=1),
                             N_EXPERTS - 1)
    n_used = (pends[-1:] // MOE_BLK).astype(jnp.int32)
    dest_flat = dest.reshape(EXPERT_TOP_K * n)
    buf = _sc_gather_rows(h2, dest_flat, rows=cap)
    out_buf = _experts(blk_expert, n_used, buf, w_gate, w_up, w_down, layer)
    return _blend(x1, wts[:EXPERT_TOP_K].T, _sc_gather_rows(out_buf, dest_flat))


def kernel(x, positions, norm_mix, norm_ffn, w_in, q_gain, k_gain, cmp_k_pos, cmp_k_w1, cmp_k_w2, cmp_v_pos,
           cmp_v_w1, cmp_v_w2, s5_a_re, s5_a_im, s5_log_dt, s5_b_re, s5_b_im, s5_c_re, s5_c_im, s5_d, s5_glu,
           proj_nsa, proj_s5, w_out, router_group, router_group_bias, router_expert, router_expert_bias,
           expert_gate, expert_up, expert_down):
    batch, seq, _ = x.shape
    depth = w_in.shape[0]
    n = batch * seq
    inv_freq = ROPE_THETA ** (-jnp.arange(0, ROPE_DIMS, 2, dtype=jnp.float32) / ROPE_DIMS)
    ctab, stab = _rope_tables(positions.reshape(n, 1), inv_freq)
    ccmp, scmp = _cmp_rope_tables(ctab, stab, batch, seq)
    x2 = x.reshape(n, D_MODEL)
    for l in range(depth):
        w1, w_ga, w_gs = _pack_w_in(w_in[l])
        o_a, u_tm = _nsa_branch(x2, norm_mix[l][None], w1, ctab, stab, ccmp, scmp, q_gain[l], k_gain[l],
                                (cmp_k_pos[l], cmp_k_w1[l], cmp_k_w2[l]),
                                (cmp_v_pos[l], cmp_v_w1[l], cmp_v_w2[l]), batch, seq)
        lre, lim, w_b, w_c = _s5_weights(s5_a_re[l], s5_a_im[l], s5_log_dt[l], s5_b_re[l], s5_b_im[l],
                                         s5_c_re[l], s5_c_im[l])
        ob_tm = _s5(u_tm, lre, lim, w_b, w_c, s5_d[l][None],
                    s5_glu[l].astype(BF16), batch, seq)
        wrh, wrl, rb = _router_weights(router_group[l], router_group_bias[l], router_expert[l],
                                       router_expert_bias[l])
        x1, h2, logits_t = _merge(x2, norm_mix[l][None], w_ga, w_gs, o_a, ob_tm, proj_nsa[l].astype(BF16),
                                  proj_s5[l].astype(BF16), w_out[l].astype(BF16), norm_ffn[l][None],
                                  wrh, wrl, rb, batch, seq)
        x2 = _moe(x1, h2, logits_t, expert_gate, expert_up, expert_down, l)
    return x2.reshape(batch, seq, D_MODEL)


def _attn_tables(seq):
    n_cmp = (seq - CMP_BLOCK) // CMP_STRIDE + 1
    n_slc = seq // SLC_BLOCK
    key_blk = np.arange(seq) // SLC_BLOCK
    expand_t = (key_blk[:, None] == np.arange(LANES)[None, :]).astype(np.float32)
    cmp_start = np.arange(LANES) * CMP_STRIDE
    slc_start = np.arange(LANES) * SLC_BLOCK
    overlap_t = ((cmp_start[None, :] < slc_start[:, None] + SLC_BLOCK)
                 & (cmp_start[None, :] + CMP_BLOCK > slc_start[:, None])
                 & (np.arange(LANES)[None, :] < n_cmp) & (np.arange(LANES)[:, None] < n_slc))
    return (jnp.asarray(expand_t, BF16), jnp.asarray(overlap_t.astype(np.float32), BF16), n_cmp, n_slc)


def _attn_mask_tables():
    tq, tk, wk = ATT_TQ, ATT_TK, ATT_WK
    ql = np.arange(tq)
    onehot = np.tile(np.eye(tq, dtype=np.float32), (NSA_HEADS, 1))
    k = np.arange(wk)[:, None]
    band = []
    for c in range(WINDOW // tq + 1):
        if c < WINDOW // tq:
            keep = k <= c * tq + ql[None, :]
        else:
            keep = (k > ql[None, :]) & (k <= ql[None, :] + WINDOW)
        band.append(np.where(keep, 0.0, NEG_BIG))
    kl = np.arange(tk)[None, :]
    causal = [np.where(kl <= c * tq + ql[:, None], 0.0, NEG_BIG) for c in range(tk // tq)]
    return (jnp.asarray(onehot, BF16), jnp.asarray(np.stack(band), BF16),
            jnp.asarray(np.stack(causal), F32))


def _attention(q, ks, vs, kw, vw, kc, vc, gates, batch, seq):
    expand_t, overlap_t, n_cmp, n_slc = _attn_tables(seq)
    onehot, band, causal = _attn_mask_tables()
    tq = ATT_TQ
    nq = seq // tq
    rows = NSA_HEADS * tq
    tile = lambda w: pl.BlockSpec((tq, w), lambda b, i: (b * nq + i, 0))
    per_b = lambda r: pl.BlockSpec((r, LANES), lambda b, i: (b, 0))
    f32_rows = pltpu.VMEM((rows, LANES), F32)
    return pl.pallas_call(
        functools.partial(_attn_kernel, n_cmp=n_cmp, n_slc=n_slc, top_k=min(SLC_TOPK, n_slc)),
        grid=(batch, nq),
        in_specs=[tile(2 * NSA_WIDTH), per_b(seq), per_b(seq), per_b(seq), per_b(seq),
                  per_b(LANES), per_b(LANES), tile(LANES),
                  pl.BlockSpec(expand_t.shape, lambda b, i: (0, 0)),
                  pl.BlockSpec(overlap_t.shape, lambda b, i: (0, 0)),
                  pl.BlockSpec(onehot.shape, lambda b, i: (0, 0)),
                  pl.BlockSpec(band.shape, lambda b, i: (0, 0, 0)),
                  pl.BlockSpec(causal.shape, lambda b, i: (0, 0, 0))],
        out_specs=tile(NSA_WIDTH),
        out_shape=jax.ShapeDtypeStruct((batch * seq, NSA_WIDTH), BF16),
        scratch_shapes=[pltpu.VMEM((rows, 2 * LANES), BF16),
                        pltpu.VMEM((rows, ATT_TK), F32), pltpu.VMEM((rows, ATT_TK), BF16),
                        pltpu.VMEM((rows, ATT_WK), F32), pltpu.VMEM((rows, ATT_WK), BF16),
                        f32_rows, f32_rows, f32_rows, f32_rows, f32_rows],
        compiler_params=_cparams("parallel", "parallel"),
        name="nsa_attention",
    )(q, ks, vs, kw, vw, kc, vc, gates, expand_t, overlap_t, onehot, band, causal)
```

```python
import functools
import math

import jax
import jax.numpy as jnp
import numpy as np
from jax import lax
from jax.experimental import pallas as pl
from jax.experimental.pallas import tpu as pltpu
from jax.experimental.pallas import tpu_sc as plsc

F32 = jnp.float32
BF16 = jnp.bfloat16

D_MODEL = 1024
NSA_HEADS = 8
NSA_KV_HEADS = 2
NSA_GROUP = NSA_HEADS // NSA_KV_HEADS
HEAD_DIM = 64
ROPE_DIMS = HEAD_DIM // 4
ROPE_THETA = 500000.0
CMP_BLOCK = 32
CMP_STRIDE = 16
CMP_HIDDEN = 2 * HEAD_DIM
SLC_BLOCK = 64
SLC_TOPK = 16
WINDOW = 512
NSA_WIDTH = NSA_HEADS * HEAD_DIM
KV_WIDTH = NSA_KV_HEADS * HEAD_DIM
S5_WIDTH = 256
S5_GROUP_CH = 16
S5_GROUPS = S5_WIDTH // S5_GROUP_CH
S5_STATE = 64
S5_NSTATE = S5_GROUPS * S5_STATE
N_EXPERT_GROUPS = 4
EXPERTS_PER_GROUP = 8
N_EXPERTS = N_EXPERT_GROUPS * EXPERTS_PER_GROUP
EXPERT_TOP_K = 2
EXPERT_FF = 512
RMS_EPS = 1e-6
NEG_BIG = -1e30

LANES = 128
SUBLANES = 8
VMEM_LIMIT = 56 * 1024 * 1024

IN_TM = 512
ATT_TQ = 128
ATT_TK = 512
ATT_WK = WINDOW + ATT_TQ
ATT_RB = 128
S5_T = 128
MRG_TM = 512
RT_TT = 512
MOE_BLK = 512
DSP_TT = 1024

C_Q, C_KS, C_KW, C_KC, C_VC, C_VS, C_VW, C_U, C_G = 0, 512, 640, 768, 896, 1024, 1152, 1280, 1536
IN_W = 1664


def _cparams(*sem):
    return pltpu.CompilerParams(dimension_semantics=sem, vmem_limit_bytes=VMEM_LIMIT)


def _gelu(x):
    return 0.5 * x * (1.0 + jnp.tanh(math.sqrt(2.0 / math.pi) * (x + 0.044715 * (x * x * x))))


def _sigmoid(x):
    return 1.0 / (1.0 + jnp.exp(-x))


def _dot(a, b):
    return jnp.dot(a, b, preferred_element_type=F32)


def _dot_t(a, b):
    return lax.dot_general(a, b, (((1,), (1,)), ((), ())), preferred_element_type=F32)


def _split_bf16(x):
    hi = x.astype(BF16)
    lo = (x - hi.astype(F32)).astype(BF16)
    return hi, lo


def _pack_rows(x):
    k = x.shape[1] // 2
    hi = lax.bitcast_convert_type(x[:, :k].astype(jnp.bfloat16).astype(F32), jnp.int32)
    lo = lax.bitcast_convert_type(x[:, k:].astype(jnp.bfloat16).astype(F32), jnp.int32)
    return hi | lax.shift_right_logical(lo, 16)


def _unpack_rows(w):
    a = lax.bitcast_convert_type(w & jnp.int32(-65536), F32)
    b = lax.bitcast_convert_type(lax.shift_left(w, 16), F32)
    return a, b


def _rope_kernel(pos_ref, freq_ref, sign_ref, c_ref, s_ref):
    ang = pos_ref[...].astype(F32) * freq_ref[...]
    c_ref[...] = jnp.cos(ang)
    s_ref[...] = jnp.sin(ang) * sign_ref[...]


def _rope_tables(pos_col, inv_freq):
    n = pos_col.shape[0]
    lane = np.arange(LANES) % HEAD_DIM
    sel = np.zeros((LANES, ROPE_DIMS // 2), np.float32)
    for l in range(LANES):
        if lane[l] < ROPE_DIMS:
            sel[l, lane[l] % (ROPE_DIMS // 2)] = 1.0
    freq = (jnp.asarray(sel) * inv_freq[None, :]).sum(-1)[None, :]
    sign = np.where(lane < ROPE_DIMS // 2, -1.0, 1.0).astype(np.float32)[None, :]
    tm = 1024
    return pl.pallas_call(
        _rope_kernel,
        grid=(n // tm,),
        in_specs=[pl.BlockSpec((tm, 1), lambda i: (i, 0)),
                  pl.BlockSpec((1, LANES), lambda i: (0, 0)),
                  pl.BlockSpec((1, LANES), lambda i: (0, 0))],
        out_specs=[pl.BlockSpec((tm, LANES), lambda i: (i, 0))] * 2,
        out_shape=[jax.ShapeDtypeStruct((n, LANES), F32)] * 2,
        compiler_params=_cparams("parallel"),
        name="rope_tables",
    )(pos_col, freq, jnp.asarray(sign))


def _headnorm_rope(t, gain, c, s):
    lane = lax.broadcasted_iota(jnp.int32, t.shape, 1)
    lo = lane < HEAD_DIM
    sq = t * t
    s_lo = jnp.sum(jnp.where(lo, sq, 0.0), axis=-1, keepdims=True)
    s_hi = jnp.sum(jnp.where(lo, 0.0, sq), axis=-1, keepdims=True)
    ms = jnp.where(lo, s_lo, s_hi) * (1.0 / HEAD_DIM)
    tn = (t * lax.rsqrt(ms + RMS_EPS)) * gain
    half = ROPE_DIMS // 2
    partner = jnp.where((lane & (HEAD_DIM - 1)) < half,
                        pltpu.roll(tn, LANES - half, 1), pltpu.roll(tn, half, 1))
    return tn * c + partner * s


def _in_proj_kernel(x_ref, gain_ref, w_ref, c_ref, s_ref, qg_ref, kg_ref,
                    q_ref, ks_ref, kw_ref, kc_ref, vc_ref, vs_ref, vw_ref, u_ref, g_ref):
    x = x_ref[...]
    ms = jnp.mean(x * x, axis=-1, keepdims=True)
    h = (x * lax.rsqrt(ms + RMS_EPS)) * gain_ref[...]
    z = _dot(h.astype(BF16), w_ref[...])
    c = c_ref[...]
    s = s_ref[...]
    lane = lax.broadcasted_iota(jnp.int32, c.shape, 1)
    lo = lane < HEAD_DIM
    scale = HEAD_DIM ** -0.5 * math.log2(math.e)
    for j in range(NSA_HEADS // 2):
        r = _headnorm_rope(z[:, C_Q + LANES * j:C_Q + LANES * (j + 1)], qg_ref[...], c, s) * scale
        rr = pltpu.roll(r, HEAD_DIM, 1)
        if j // (NSA_GROUP // 2) == 0:
            even, odd = jnp.where(lo, r, 0.0), jnp.where(lo, rr, 0.0)
        else:
            even, odd = jnp.where(lo, 0.0, rr), jnp.where(lo, 0.0, r)
        q_ref[:, 2 * LANES * j:2 * LANES * j + LANES] = even.astype(BF16)
        q_ref[:, 2 * LANES * j + LANES:2 * LANES * (j + 1)] = odd.astype(BF16)
    ks_ref[...] = _headnorm_rope(z[:, C_KS:C_KS + LANES], kg_ref[...], c, s).astype(BF16)
    kw_ref[...] = _headnorm_rope(z[:, C_KW:C_KW + LANES], kg_ref[...], c, s).astype(BF16)
    kc_ref[...] = z[:, C_KC:C_KC + LANES].astype(BF16)
    vc_ref[...] = z[:, C_VC:C_VC + LANES].astype(BF16)
    vs_ref[...] = z[:, C_VS:C_VS + LANES].astype(BF16)
    vw_ref[...] = z[:, C_VW:C_VW + LANES].astype(BF16)
    u_ref[:, pl.ds(pl.program_id(1), 1), :] = z[:, C_U:C_U + S5_WIDTH][:, None, :]
    g_ref[...] = _sigmoid(z[:, C_G:C_G + LANES])


def _in_proj(x2, gain, w1, ctab, stab, qg, kg, batch, seq):
    n = x2.shape[0]
    tm = IN_TM
    nt = seq // tm
    row = lambda i, b: (b * nt + i, 0)
    const = lambda i, b: (0, 0)
    tok = lambda w: pl.BlockSpec((tm, w), row)
    outs = [(n, 2 * NSA_WIDTH, BF16)] + [(n, LANES, BF16)] * 6
    return pl.pallas_call(
        _in_proj_kernel,
        grid=(nt, batch),
        in_specs=[tok(D_MODEL), pl.BlockSpec((1, D_MODEL), const), pl.BlockSpec((D_MODEL, IN_W), const),
                  tok(LANES), tok(LANES), pl.BlockSpec((1, LANES), const), pl.BlockSpec((1, LANES), const)],
        out_specs=[tok(2 * NSA_WIDTH)] + [tok(LANES)] * 6
        + [pl.BlockSpec((tm, batch, S5_WIDTH), lambda i, b: (i, 0, 0)), tok(LANES)],
        out_shape=[jax.ShapeDtypeStruct(s[:2], s[2]) for s in outs]
        + [jax.ShapeDtypeStruct((seq, batch, S5_WIDTH), F32), jax.ShapeDtypeStruct((n, LANES), F32)],
        compiler_params=_cparams("parallel", "arbitrary"),
        name="in_proj",
    )(x2, gain, w1, ctab, stab, qg, kg)


def _compress_mlp(h_ref, w_ref, p_ref, w2_ref):
    w = w_ref[...]
    pre = _dot(h_ref[...], w)
    phi, plo = _split_bf16(p_ref[...])
    pb = _dot(phi, w) + _dot(plo, w)
    rows = pre.shape[0]
    out = None
    for h in range(NSA_KV_HEADS):
        top = pre[:, LANES * h:LANES * (h + 1)]
        bot = pre[:, LANES * (2 + h):LANES * (3 + h)]
        bias = pb[0:1, LANES * h:LANES * (h + 1)] + pb[1:2, LANES * (2 + h):LANES * (3 + h)]
        a = _gelu(top + pltpu.roll(bot, rows - 1, 0) + bias)
        o = _dot(a.astype(BF16), w2_ref[h])
        out = o if out is None else out + o
    return out


def _compress_kernel(hk_ref, hv_ref, wk_ref, wv_ref, pk_ref, pv_ref, w2k_ref, w2v_ref,
                     kg_ref, c_ref, s_ref, kc_ref, vc_ref):
    k = _compress_mlp(hk_ref, wk_ref, pk_ref, w2k_ref)
    kc_ref[...] = _headnorm_rope(k, kg_ref[...], c_ref[...], s_ref[...]).astype(BF16)
    vc_ref[...] = _compress_mlp(hv_ref, wv_ref, pv_ref, w2v_ref).astype(BF16)


def _compress(hk, hv, wk, wv, pk, pv, w2k, w2v, kg, ccmp, scmp, batch):
    rows = hk.shape[0] // batch
    width = hk.shape[1]
    const2 = lambda b: (0, 0)
    const3 = lambda b: (0, 0, 0)
    rowb = lambda w: pl.BlockSpec((rows, w), lambda b: (b, 0))
    return pl.pallas_call(
        _compress_kernel,
        grid=(batch,),
        in_specs=[rowb(width), rowb(width),
                  pl.BlockSpec(wk.shape, const2), pl.BlockSpec(wv.shape, const2),
                  pl.BlockSpec(pk.shape, const2), pl.BlockSpec(pv.shape, const2),
                  pl.BlockSpec(w2k.shape, const3), pl.BlockSpec(w2v.shape, const3),
                  pl.BlockSpec((1, LANES), const2), rowb(LANES), rowb(LANES)],
        out_specs=[rowb(LANES), rowb(LANES)],
        out_shape=[jax.ShapeDtypeStruct((hk.shape[0], LANES), BF16)] * 2,
        compiler_params=_cparams("parallel"),
        name="compress_kv",
    )(hk, hv, wk, wv, pk, pv, w2k, w2v, kg, ccmp, scmp)


def _compress_weights(pos_emb, w1, w2):
    half = CMP_BLOCK // 2
    eye = jnp.eye(NSA_KV_HEADS, dtype=F32)
    w1r = w1.reshape(2, half, HEAD_DIM, CMP_HIDDEN)
    wexp = jnp.einsum("sldn,ph->lpdshn", w1r, eye)
    wexp = wexp.reshape(half * KV_WIDTH, 2 * NSA_KV_HEADS * CMP_HIDDEN).astype(BF16)
    pexp = jnp.broadcast_to(pos_emb.reshape(2, half, 1, HEAD_DIM), (2, half, NSA_KV_HEADS, HEAD_DIM))
    pexp = jnp.pad(pexp.reshape(2, half * KV_WIDTH), ((0, SUBLANES - 2), (0, 0)))
    w2e = jnp.einsum("cd,ph->pchd", w2, eye).reshape(NSA_KV_HEADS, CMP_HIDDEN, KV_WIDTH)
    return wexp, pexp, w2e.astype(BF16)


def _attn_kernel(q_ref, ks_ref, vs_ref, kw_ref, vw_ref, kc_ref, vc_ref, g_ref, et_ref, ovt_ref, oh_ref, wb_ref,
                 cb_ref, o_ref, qc_ref, s_ref, p_ref, sw_ref, pw_ref, m_ref, al_ref, oc_ref, as_ref, aw_ref,
                 *, n_cmp, n_slc, top_k):
    tq, tk, wk, rb, nh = ATT_TQ, ATT_TK, ATT_WK, ATT_RB, NSA_HEADS
    rows, grp_rows = nh * tq, NSA_GROUP * tq
    q0 = pl.program_id(1) * tq
    for a in range(nh):
        qc_ref[a * tq:(a + 1) * tq, 0:LANES] = q_ref[:, LANES * a:LANES * (a + 1)]
    q8 = qc_ref[:, 0:LANES]
    lane = lax.broadcasted_iota(jnp.int32, (tq, LANES), 1)
    t_row = q0 + lax.broadcasted_iota(jnp.int32, (tq, LANES), 0)

    s_c = _dot_t(q8, kc_ref[...]).reshape(nh, tq, LANES)
    mask_c = ((lane * CMP_STRIDE + (CMP_BLOCK - 1)) <= t_row) & (lane < n_cmp)
    sm = jnp.where(mask_c[None], s_c, NEG_BIG)
    p = jnp.where(mask_c[None], jnp.exp2(sm - jnp.max(sm, axis=-1, keepdims=True)), 0.0)
    l = jnp.sum(p, axis=-1, keepdims=True)
    p_c = p * jnp.where(l > 0.0, 1.0 / l, 0.0)
    oc_ref[...] = _dot(p_c.reshape(rows, LANES).astype(BF16), vc_ref[...])

    blk = lax.broadcasted_iota(jnp.int32, (n_slc, tq), 0)
    cur = (q0 + lax.broadcasted_iota(jnp.int32, (n_slc, tq), 1)) >> int(math.log2(SLC_BLOCK))
    forced = (blk == 0) | (blk == cur) | (blk == cur - 1)
    for h in range(NSA_KV_HEADS):
        ps = p_c[NSA_GROUP * h]
        for g in range(1, NSA_GROUP):
            ps = ps + p_c[NSA_GROUP * h + g]
        hi, lo = _split_bf16(ps)
        imp = (_dot_t(ovt_ref[...], hi) + _dot_t(ovt_ref[...], lo))[0:n_slc]
        imp = jnp.where(forced, jnp.inf, jnp.where(blk > cur, -jnp.inf, imp))
        rank = jnp.zeros((n_slc, tq), F32)
        for i in range(n_slc):
            row = imp[i:i + 1, :]
            rank = rank + jnp.where((row > imp) | ((row == imp) & (blk > i)), 1.0, 0.0)
        bias_t = jnp.where(rank < float(top_k), 0.0, NEG_BIG)
        bias = jnp.concatenate([bias_t, jnp.zeros((LANES - n_slc, tq), F32)], axis=0).T.astype(BF16)
        for g in range(NSA_GROUP):
            a = NSA_GROUP * h + g
            qc_ref[a * tq:(a + 1) * tq, LANES:2 * LANES] = bias

    def softmax_pass(s_buf, p_buf, width, causal_case, online):
        for r in range(rows // rb):
            r0 = r * rb
            s = s_buf[pl.ds(r0, rb), 0:width]
            if causal_case is not None:
                s = s + cb_ref[causal_case, pl.ds(r0 % tq, rb), :]
            m_new = jnp.broadcast_to(jnp.max(s, axis=1, keepdims=True), (rb, LANES))
            if online:
                m_old = m_ref[pl.ds(r0, rb), :]
                m_new = jnp.maximum(m_old, m_new)
                al_ref[pl.ds(r0, rb), :] = jnp.exp2(m_old - m_new)
                m_ref[pl.ds(r0, rb), :] = m_new
            pp = jnp.exp2(s - jnp.concatenate([m_new] * (width // LANES), axis=1))
            p_buf[pl.ds(r0, rb), 0:width] = pp.astype(BF16)

    def pv(acc_ref, p_buf, v, width, online):
        lane_v = lax.broadcasted_iota(jnp.int32, v.shape, 1)
        for h in range(NSA_KV_HEADS):
            rs = slice(h * grp_rows, (h + 1) * grp_rows)
            own = (lane_v < HEAD_DIM) if h == 0 else (lane_v >= HEAD_DIM)
            upd = _dot(p_buf[rs, 0:width], jnp.where(own, v, jnp.ones_like(v)))
            acc_ref[rs, :] = (al_ref[rs, :] * acc_ref[rs, :] + upd) if online else upd

    m_ref[...] = jnp.full(m_ref.shape, 0.1 * NEG_BIG, F32)
    as_ref[...] = jnp.zeros(as_ref.shape, F32)

    def slc_chunk(c, diagonal):
        k0 = pl.multiple_of(c * tk, tk)
        kcat = jnp.concatenate([ks_ref[pl.ds(k0, tk), :], et_ref[pl.ds(k0, tk), :]], axis=1)
        s_ref[...] = _dot_t(qc_ref[...], kcat)
        softmax_pass(s_ref, p_ref, tk, ((q0 - k0) >> int(math.log2(tq))) if diagonal else None, True)
        pv(as_ref, p_ref, vs_ref[pl.ds(k0, tk), :], tk, True)

    c_last = q0 >> int(math.log2(tk))

    def full_chunk(c, carry):
        slc_chunk(c, False)
        return carry

    lax.fori_loop(0, c_last, full_chunk, 0)
    slc_chunk(c_last, True)

    w0 = pl.multiple_of(jnp.maximum(q0 - WINDOW, 0), LANES)
    w_case = jnp.minimum(q0 >> int(math.log2(tq)), WINDOW // tq)
    kcat_w = jnp.concatenate([kw_ref[pl.ds(w0, wk), :], wb_ref[w_case]], axis=1)
    sw_ref[...] = _dot_t(jnp.concatenate([q8, oh_ref[...]], axis=1), kcat_w)
    softmax_pass(sw_ref, pw_ref, wk, None, False)
    pv(aw_ref, pw_ref, vw_ref[pl.ds(w0, wk), :], wk, False)

    lo_half = lane < HEAD_DIM
    for j in range(nh // 2):
        comb = []
        for a in (2 * j, 2 * j + 1):
            rs = slice(a * tq, (a + 1) * tq)
            o_s, o_w = as_ref[rs, :], aw_ref[rs, :]
            comb.append(g_ref[:, 3 * a:3 * a + 1] * oc_ref[rs, :]
                        + g_ref[:, 3 * a + 1:3 * a + 2] * (o_s * (1.0 / pltpu.roll(o_s, HEAD_DIM, 1)))
                        + g_ref[:, 3 * a + 2:3 * a + 3] * (o_w * (1.0 / pltpu.roll(o_w, HEAD_DIM, 1))))
        even, odd = comb
        if j // (NSA_GROUP // 2) == 0:
            pair = jnp.where(lo_half, even, pltpu.roll(odd, HEAD_DIM, 1))
        else:
            pair = jnp.where(lo_half, pltpu.roll(even, HEAD_DIM, 1), odd)
        o_ref[:, LANES * j:LANES * (j + 1)] = pair.astype(BF16)


def _pack_w_in(w):
    sizes = (NSA_WIDTH,) + (KV_WIDTH,) * 6 + (3 * NSA_HEADS, S5_WIDTH, D_MODEL, D_MODEL)
    o = np.cumsum((0,) + sizes)
    q, kc, vc, ks, vs, kw, vw, g, u, ga, gs = [w[:, o[i]:o[i + 1]] for i in range(len(sizes))]
    g = jnp.pad(g, ((0, 0), (0, LANES - 3 * NSA_HEADS)))
    w1 = jnp.concatenate([q, ks, kw, kc, vc, vs, vw, u, g], axis=1).astype(BF16)
    return w1, ga.astype(BF16), gs.astype(BF16)


def _nsa_branch(x2, gain, w1, ctab, stab, ccmp, scmp, q_gain, k_gain, cmp_k, cmp_v, batch, seq):
    qg = jnp.tile(q_gain, 2)[None, :]
    kg = jnp.tile(k_gain, 2)[None, :]
    q, ks, kw, kc_raw, vc_raw, vs, vw, u_tm, gates = _in_proj(x2, gain, w1, ctab, stab, qg, kg, batch, seq)
    half_rows = batch * seq // (CMP_BLOCK // 2)
    hk = kc_raw.reshape(half_rows, (CMP_BLOCK // 2) * KV_WIDTH)
    hv = vc_raw.reshape(half_rows, (CMP_BLOCK // 2) * KV_WIDTH)
    wk, pk, w2k = _compress_weights(*cmp_k)
    wv, pv, w2v = _compress_weights(*cmp_v)
    kc, vc = _compress(hk, hv, wk, wv, pk, pv, w2k, w2v, kg, ccmp, scmp, batch)
    o_a = _attention(q, ks, vs, kw, vw, kc, vc, gates, batch, seq)
    return o_a, u_tm


def _cmp_rope_tables(ctab, stab, batch, seq):
    def pick(t):
        t = t.reshape(batch, seq, LANES)[:, CMP_BLOCK - 1::CMP_STRIDE]
        t = jnp.pad(t, ((0, 0), (0, seq // CMP_STRIDE - t.shape[1]), (0, 0)))
        return t.reshape(batch * (seq // CMP_STRIDE), LANES)
    return pick(ctab), pick(stab)


def _s5_disc_kernel(are_ref, aim_ref, ldt_ref, lre_ref, lim_ref, fre_ref, fim_ref):
    lam_re = jnp.minimum(are_ref[...], -1e-4)
    lam_im = aim_ref[...]
    dt = jnp.exp(ldt_ref[...])
    mag = jnp.exp(lam_re * dt)
    ang = lam_im * dt
    lb_re = mag * jnp.cos(ang)
    lb_im = mag * jnp.sin(ang)
    den = lam_re * lam_re + lam_im * lam_im
    lre_ref[...] = lb_re
    lim_ref[...] = lb_im
    fre_ref[...] = ((lb_re - 1.0) * lam_re + lb_im * lam_im) / den
    fim_ref[...] = (lb_im * lam_re - (lb_re - 1.0) * lam_im) / den


def _s5_weights(a_re, a_im, log_dt, b_re, b_im, c_re, c_im):
    shp = jax.ShapeDtypeStruct((S5_GROUPS, S5_STATE), F32)
    lb_re, lb_im, f_re, f_im = pl.pallas_call(
        _s5_disc_kernel, out_shape=[shp] * 4, name="s5_discretise",
    )(a_re, a_im, log_dt[:, None])
    bb_re = f_re[..., None] * b_re - f_im[..., None] * b_im
    bb_im = f_re[..., None] * b_im + f_im[..., None] * b_re
    eye = jnp.eye(S5_GROUPS, dtype=F32)

    def in_map(bb):
        return jnp.einsum("gpc,gh->gchp", bb, eye).reshape(S5_WIDTH, S5_NSTATE)

    def out_map(cc):
        return jnp.einsum("gcp,gh->hpgc", cc, eye).reshape(S5_NSTATE, S5_WIDTH)

    w_b = jnp.concatenate([in_map(bb_re), in_map(bb_im)], axis=1).astype(BF16)
    w_c = jnp.concatenate([out_map(c_re), -out_map(c_im)], axis=0).astype(BF16)
    return lb_re.reshape(1, S5_NSTATE), lb_im.reshape(1, S5_NSTATE), w_b, w_c


def _s5_kernel(u_ref, wb_ref, lre_ref, lim_ref, wc_ref, d_ref, wg_ref, o_ref, x_ref, st_ref, *, batch):
    ns = S5_NSTATE

    @pl.when(pl.program_id(0) == 0)
    def _():
        st_ref[...] = jnp.zeros(st_ref.shape, F32)

    u = u_ref[...].reshape(u_ref.shape[0] * batch, S5_WIDTH)
    x_ref[...] = _dot(u.astype(BF16), wb_ref[...])
    lre = jnp.broadcast_to(lre_ref[...], (batch, ns))
    lim = jnp.broadcast_to(lim_ref[...], (batch, ns))

    def step(t, carry):
        xr, xi = carry
        r0 = pl.multiple_of(t * batch, batch)
        nr = lre * xr - lim * xi + x_ref[pl.ds(r0, batch), 0:ns]
        ni = lre * xi + lim * xr + x_ref[pl.ds(r0, batch), ns:2 * ns]
        x_ref[pl.ds(r0, batch), 0:ns] = nr
        x_ref[pl.ds(r0, batch), ns:2 * ns] = ni
        return nr, ni

    steps = u.shape[0] // batch
    xr, xi = lax.fori_loop(0, steps, step, (st_ref[:, 0:ns], st_ref[:, ns:2 * ns]), unroll=4)
    st_ref[:, 0:ns] = xr
    st_ref[:, ns:2 * ns] = xi
    y = _dot(x_ref[...].astype(BF16), wc_ref[...]) + d_ref[...] * u
    z = _dot(_gelu(y).astype(BF16), wg_ref[...])
    o = z[:, :S5_WIDTH] * _sigmoid(z[:, S5_WIDTH:])
    for c in range(S5_WIDTH // LANES):
        o_ref[c] = o[:, c * LANES:(c + 1) * LANES]


def _s5(u_tm, lre, lim, w_b, w_c, d_skip, w_glu, batch, seq):
    rows = S5_T * batch
    const = lambda t: (0, 0)
    return pl.pallas_call(
        functools.partial(_s5_kernel, batch=batch),
        grid=(seq // S5_T,),
        in_specs=[pl.BlockSpec((S5_T, batch, S5_WIDTH), lambda t: (t, 0, 0)),
                  pl.BlockSpec(w_b.shape, const), pl.BlockSpec(lre.shape, const), pl.BlockSpec(lim.shape, const),
                  pl.BlockSpec(w_c.shape, const), pl.BlockSpec((1, S5_WIDTH), const),
                  pl.BlockSpec(w_glu.shape, const)],
        out_specs=pl.BlockSpec((S5_WIDTH // LANES, rows, LANES), lambda t: (0, t, 0)),
        out_shape=jax.ShapeDtypeStruct((S5_WIDTH // LANES, seq * batch, LANES), F32),
        scratch_shapes=[pltpu.VMEM((rows, 2 * S5_NSTATE), F32), pltpu.VMEM((batch, 2 * S5_NSTATE), F32)],
        compiler_params=_cparams("arbitrary"),
        name="s5_scan",
    )(u_tm, w_b, lre, lim, w_c, d_skip, w_glu)


RT_GROUP_ROW = 0
RT_EXPERT_ROW = SUBLANES


def _rmsnorm(x, gain):
    return (x * lax.rsqrt(jnp.mean(x * x, axis=-1, keepdims=True) + RMS_EPS)) * gain


def _merge_kernel(x_ref, g1_ref, wga_ref, wgs_ref, oa_ref, ob_ref, pa_ref, ps_ref, wo_ref, g2_ref,
                  wrh_ref, wrl_ref, rb_ref, x1_ref, h2_ref, lg_ref):
    x = x_ref[...]
    h = _rmsnorm(x, g1_ref[...]).astype(BF16)
    gate_a = _sigmoid(_dot(h, wga_ref[...]))
    gate_s = _sigmoid(_dot(h, wgs_ref[...]))
    rows_b = pl.ds(pl.program_id(1), x.shape[0], stride=ob_ref.shape[1] // x.shape[0])
    o_b = jnp.concatenate([ob_ref[c, rows_b, :] for c in range(S5_WIDTH // LANES)], axis=1)
    merged = gate_a * _dot(oa_ref[...], pa_ref[...]) + gate_s * _dot(o_b.astype(BF16), ps_ref[...])
    x1 = x + _dot(merged.astype(BF16), wo_ref[...])
    x1_ref[...] = x1
    h2 = _rmsnorm(x1, g2_ref[...])
    h2_ref[...] = _pack_rows(h2)
    hi, lo = _split_bf16(h2)
    lg_ref[...] = (_dot_t(wrh_ref[...], hi) + _dot_t(wrh_ref[...], lo) + _dot_t(wrl_ref[...], hi)
                   + rb_ref[...])


def _router_weights(w_rg, b_rg, w_re, b_re):
    def rows(group_part, expert_part):
        return jnp.concatenate([
            group_part, jnp.zeros((RT_EXPERT_ROW - N_EXPERT_GROUPS,) + group_part.shape[1:], F32),
            expert_part, jnp.zeros((LANES - RT_EXPERT_ROW - N_EXPERTS,) + group_part.shape[1:], F32)], axis=0)

    hi, lo = _split_bf16(rows(w_rg.T, w_re.T))
    return hi, lo, rows(b_rg[:, None], b_re[:, None])


def _merge(x2, g1, w_ga, w_gs, o_a, ob_tm, p_a, p_s, w_o, g2, wrh, wrl, rb, batch, seq):
    n = x2.shape[0]
    tm = MRG_TM
    nt = seq // tm
    row = lambda i, b: (b * nt + i, 0)
    const = lambda i, b: (0, 0)
    tok = lambda w: pl.BlockSpec((tm, w), row)
    full = lambda a: pl.BlockSpec(a.shape, const)
    return pl.pallas_call(
        _merge_kernel,
        grid=(nt, batch),
        in_specs=[tok(D_MODEL), full(g1), full(w_ga), full(w_gs), tok(NSA_WIDTH),
                  pl.BlockSpec((S5_WIDTH // LANES, tm * batch, LANES), lambda i, b: (0, i, 0)),
                  full(p_a), full(p_s), full(w_o),
                  full(g2), full(wrh), full(wrl), full(rb)],
        out_specs=[tok(D_MODEL), tok(D_MODEL // 2), pl.BlockSpec((LANES, tm), lambda i, b: (0, b * nt + i))],
        out_shape=[jax.ShapeDtypeStruct((n, D_MODEL), F32), jax.ShapeDtypeStruct((n, D_MODEL // 2), jnp.int32),
                   jax.ShapeDtypeStruct((LANES, n), F32)],
        compiler_params=_cparams("parallel", "parallel"),
        name="merge_out_proj",
    )(x2, g1, w_ga, w_gs, o_a, ob_tm, p_a, p_s, w_o, g2, wrh, wrl, rb)


def _first_argmax(v, rows):
    vmax = jnp.max(v, axis=0, keepdims=True)
    idx = jnp.min(jnp.where(v == vmax, rows, SUBLANES), axis=0, keepdims=True)
    return vmax, idx


def _route_kernel(lg_ref, tri_ref, e_ref, w_ref, r_ref, cnt_ref, carry_ref):
    tt = lg_ref.shape[1]

    @pl.when(pl.program_id(0) == 0)
    def _():
        carry_ref[...] = jnp.zeros(carry_ref.shape, F32)

    rows = lax.broadcasted_iota(jnp.int32, (SUBLANES, tt), 0)
    gl = jnp.where(rows < N_EXPERT_GROUPS, lg_ref[RT_GROUP_ROW:RT_GROUP_ROW + SUBLANES, :], -jnp.inf)
    gmax, g_sel = _first_argmax(gl, rows)
    g_w = 1.0 / jnp.sum(jnp.exp(gl - gmax), axis=0, keepdims=True)

    el = jnp.zeros((EXPERTS_PER_GROUP, tt), F32)
    for g in range(N_EXPERT_GROUPS):
        r0 = RT_EXPERT_ROW + EXPERTS_PER_GROUP * g
        el = jnp.where(g_sel == g, lg_ref[r0:r0 + EXPERTS_PER_GROUP, :], el)
    ex = jnp.exp(el - jnp.max(el, axis=0, keepdims=True))
    p = ex / jnp.sum(ex, axis=0, keepdims=True)
    p1, i1 = _first_argmax(p, rows)
    p2, i2 = _first_argmax(jnp.where(rows == i1, -1.0, p), rows)
    e1 = g_sel * EXPERTS_PER_GROUP + i1
    e2 = g_sel * EXPERTS_PER_GROUP + i2
    scale = g_w / (p1 + p2)

    erow = lax.broadcasted_iota(jnp.int32, (N_EXPERTS, tt), 0)
    oh1 = jnp.where(erow == e1, 1.0, 0.0)
    oh2 = jnp.where(erow == e2, 1.0, 0.0)
    both = oh1 + oh2
    before = carry_ref[:, 0:1] + _dot(both.astype(BF16), tri_ref[...])
    carry_ref[...] = carry_ref[...] + jnp.sum(both, axis=1, keepdims=True)
    cnt_ref[...] = carry_ref[...]

    zi = jnp.zeros((SUBLANES - 2, tt), jnp.int32)
    e_ref[...] = jnp.concatenate([e1, e2, zi], axis=0)
    w_ref[...] = jnp.concatenate([p1 * scale, p2 * scale, zi.astype(F32)], axis=0)
    r_ref[...] = jnp.concatenate([jnp.sum(oh1 * before, axis=0, keepdims=True),
                                  jnp.sum(oh2 * before, axis=0, keepdims=True),
                                  zi.astype(F32)], axis=0).astype(jnp.int32)


def _route(logits_t):
    n = logits_t.shape[1]
    tt = RT_TT
    tri = jnp.asarray(np.triu(np.ones((tt, tt), np.float32), 1), BF16)
    blk = pl.BlockSpec((SUBLANES, tt), lambda i: (0, i))
    return pl.pallas_call(
        _route_kernel,
        grid=(n // tt,),
        in_specs=[pl.BlockSpec((LANES, tt), lambda i: (0, i)), pl.BlockSpec((tt, tt), lambda i: (0, 0))],
        out_specs=[blk, blk, blk, pl.BlockSpec((N_EXPERTS, LANES), lambda i: (0, 0))],
        out_shape=[jax.ShapeDtypeStruct((SUBLANES, n), jnp.int32), jax.ShapeDtypeStruct((SUBLANES, n), F32),
                   jax.ShapeDtypeStruct((SUBLANES, n), jnp.int32), jax.ShapeDtypeStruct((N_EXPERTS, LANES), F32)],
        scratch_shapes=[pltpu.VMEM((N_EXPERTS, LANES), F32)],
        compiler_params=_cparams("arbitrary"),
        name="moe_route",
    )(logits_t, tri)


def _expert_kernel(be_ref, nb_ref, x_ref, wg_ref, wu_ref, wd_ref, o_ref, wgb_ref, wub_ref, wdb_ref):
    i = pl.program_id(0)
    half = D_MODEL // 2

    @pl.when(i < nb_ref[0])
    def _():
        @pl.when((i == 0) | (be_ref[i] != be_ref[jnp.maximum(i - 1, 0)]))
        def _():
            wgb_ref[...] = wg_ref[...].astype(BF16)
            wub_ref[...] = wu_ref[...].astype(BF16)
            wdb_ref[...] = wd_ref[...].astype(BF16)

        xa, xb = _unpack_rows(x_ref[...])
        xa, xb = xa.astype(BF16), xb.astype(BF16)
        g = _dot(xa, wgb_ref[0:half, :]) + _dot(xb, wgb_ref[half:, :])
        u = _dot(xa, wub_ref[0:half, :]) + _dot(xb, wub_ref[half:, :])
        o_ref[...] = _pack_rows(_dot(((g * _sigmoid(g)) * u).astype(BF16), wdb_ref[...]))

    @pl.when(i >= nb_ref[0])
    def _():
        o_ref[...] = jnp.zeros(o_ref.shape, o_ref.dtype)


def _experts(blk_expert, n_used, buf, w_gate, w_up, w_down, layer):
    cap, width = buf.shape
    blk = MOE_BLK
    row = lambda i, be, nb: (jnp.minimum(i, nb[0] - 1), 0)
    wmap = lambda i, be, nb: (layer, be[jnp.minimum(i, nb[0] - 1)], 0, 0)
    return pl.pallas_call(
        _expert_kernel,
        grid_spec=pltpu.PrefetchScalarGridSpec(
            num_scalar_prefetch=2,
            grid=(cap // blk,),
            in_specs=[pl.BlockSpec((blk, width), row),
                      pl.BlockSpec((None, None, D_MODEL, EXPERT_FF), wmap),
                      pl.BlockSpec((None, None, D_MODEL, EXPERT_FF), wmap),
                      pl.BlockSpec((None, None, EXPERT_FF, D_MODEL), wmap)],
            out_specs=pl.BlockSpec((blk, width), lambda i, be, nb: (i, 0)),
            scratch_shapes=[pltpu.VMEM((D_MODEL, EXPERT_FF), BF16), pltpu.VMEM((D_MODEL, EXPERT_FF), BF16),
                            pltpu.VMEM((EXPERT_FF, D_MODEL), BF16)],
        ),
        out_shape=jax.ShapeDtypeStruct((cap, width), buf.dtype),
        compiler_params=_cparams("arbitrary"),
        name="moe_experts",
    )(blk_expert, n_used, buf, w_gate, w_up, w_down)


SC_CH = 64


def _sc_mesh():
    return plsc.VectorSubcoreMesh(core_axis_name="c", subcore_axis_name="s")


def _sc_worker(info):
    return lax.axis_index("s") * info.num_cores + lax.axis_index("c")


def _sc_gather_rows(table, idx, rows=None):
    info = plsc.get_sparse_core_info()
    lanes, workers = info.num_lanes, info.num_cores * info.num_subcores
    invert = rows is not None
    rows = rows if invert else idx.shape[0]
    width = table.shape[1]
    per_w = rows // workers
    n_chunks = per_w // SC_CH
    assert per_w % (2 * SC_CH) == 0
    scan = 2048
    n_tok = table.shape[0]

    def body(table_hbm, idx_hbm, out_hbm, idx_v, rows_a, rows_b, sem_a, sem_b, *scan_v):
        base = _sc_worker(info) * per_w
        if invert:
            lane = lax.iota(jnp.int32, lanes)

            @pl.loop(0, per_w // lanes)
            def _(j):
                idx_v[pl.ds(j * lanes, lanes)] = (base + j * lanes + lane) & (n_tok - 1)

            @pl.loop(0, idx.shape[0] // scan)
            def _(c):
                pltpu.sync_copy(idx_hbm.at[pl.ds(c * scan, scan)], scan_v[0])

                @pl.loop(0, scan // lanes)
                def _(j):
                    rel = scan_v[0][pl.ds(j * lanes, lanes)] - base
                    mine = (rel >= 0) & (rel < per_w)
                    token = (c * scan + j * lanes + lane) & (n_tok - 1)
                    plsc.store_scatter(idx_v, [jnp.where(mine, rel, 0)], token, mask=mine)
        else:
            pltpu.sync_copy(idx_hbm.at[pl.ds(base, per_w)], idx_v)

        def gather(j, buf, sem):
            return pltpu.make_async_copy(table_hbm.at[idx_v.at[pl.ds(j * SC_CH, SC_CH)]], buf, sem)

        def write(j, buf):
            pltpu.sync_copy(buf, out_hbm.at[pl.ds(base + j * SC_CH, SC_CH)])

        gather(0, rows_a, sem_a).start()

        @pl.loop(0, n_chunks, step=2)
        def _(j):
            gather(j + 1, rows_b, sem_b).start()
            gather(j, rows_a, sem_a).wait()
            write(j, rows_a)

            @pl.when(j + 2 < n_chunks)
            def _():
                gather(j + 2, rows_a, sem_a).start()

            gather(j + 1, rows_b, sem_b).wait()
            write(j + 1, rows_b)

    chunk_buf = pltpu.VMEM((SC_CH, width), table.dtype)
    scratch = [pltpu.VMEM((per_w,), jnp.int32), chunk_buf, chunk_buf,
               pltpu.SemaphoreType.DMA, pltpu.SemaphoreType.DMA]
    return pl.kernel(
        body, out_type=jax.ShapeDtypeStruct((rows, width), table.dtype), mesh=_sc_mesh(),
        scratch_types=scratch + ([pltpu.VMEM((scan,), jnp.int32)] if invert else []),
        compiler_params=pltpu.CompilerParams(needs_layout_passes=False) if invert else None,
        name="sc_dispatch_rows" if invert else "sc_gather_rows",
    )(table, idx)


def _blend_kernel(x1_ref, w_ref, g0_ref, g1_ref, o_ref):
    half = x1_ref.shape[1] // 2
    a0, b0 = _unpack_rows(g0_ref[...])
    a1, b1 = _unpack_rows(g1_ref[...])
    w0, w1 = w_ref[:, 0:1], w_ref[:, 1:2]
    o_ref[:, 0:half] = x1_ref[:, 0:half] + w0 * a0 + w1 * a1
    o_ref[:, half:] = x1_ref[:, half:] + w0 * b0 + w1 * b1


def _blend(x1, w_col, gathered):
    n = x1.shape[0]
    tt = DSP_TT
    nt = n // tt
    width = gathered.shape[1]
    return pl.pallas_call(
        _blend_kernel,
        grid=(nt,),
        in_specs=[pl.BlockSpec((tt, D_MODEL), lambda i: (i, 0)),
                  pl.BlockSpec((tt, EXPERT_TOP_K), lambda i: (i, 0)),
                  pl.BlockSpec((tt, width), lambda i: (i, 0)),
                  pl.BlockSpec((tt, width), lambda i: (nt + i, 0))],
        out_specs=pl.BlockSpec((tt, D_MODEL), lambda i: (i, 0)),
        out_shape=jax.ShapeDtypeStruct((n, D_MODEL), F32),
        compiler_params=_cparams("parallel"),
        name="moe_blend",
    )(x1, w_col, gathered, gathered)


def _moe(x1, h2, logits_t, w_gate, w_up, w_down, layer):
    n = x1.shape[0]
    e_idx, wts, rank, counts = _route(logits_t)
    counts = counts[:, 0].astype(jnp.int32)
    padded = (counts + MOE_BLK - 1) // MOE_BLK * MOE_BLK
    pends = jnp.cumsum(padded)
    pstarts = pends - padded
    experts = jnp.arange(N_EXPERTS, dtype=jnp.int32)
    start_of = jnp.sum(jnp.where(e_idx[None, :EXPERT_TOP_K] == experts[:, None, None],
                                 pstarts[:, None, None], 0), axis=0)
    dest = start_of + rank[:EXPERT_TOP_K]
    cap = n * EXPERT_TOP_K + N_EXPERTS * MOE_BLK
    n_blk = cap // MOE_BLK
    blk_row = jnp.arange(n_blk, dtype=jnp.int32) * MOE_BLK
    blk_expert = jnp.minimum(jnp.sum((pends[None, :] <= blk_row[:, None]).astype(jnp.int32), axis=1),
                             N_EXPERTS - 1)
    n_used = (pends[-1:] // MOE_BLK).astype(jnp.int32)
    dest_flat = dest.reshape(EXPERT_TOP_K * n)
    buf = _sc_gather_rows(h2, dest_flat, rows=cap)
    out_buf = _experts(blk_expert, n_used, buf, w_gate, w_up, w_down, layer)
    return _blend(x1, wts[:EXPERT_TOP_K].T, _sc_gather_rows(out_buf, dest_flat))


def kernel(x, positions, norm_mix, norm_ffn, w_in, q_gain, k_gain, cmp_k_pos, cmp_k_w1, cmp_k_w2, cmp_v_pos,
           cmp_v_w1, cmp_v_w2, s5_a_re, s5_a_im, s5_log_dt, s5_b_re, s5_b_im, s5_c_re, s5_c_im, s5_d, s5_glu,
           proj_nsa, proj_s5, w_out, router_group, router_group_bias, router_expert, router_expert_bias,
           expert_gate, expert_up, expert_down):
    batch, seq, _ = x.shape
    depth = w_in.shape[0]
    n = batch * seq
    inv_freq = ROPE_THETA ** (-jnp.arange(0, ROPE_DIMS, 2, dtype=jnp.float32) / ROPE_DIMS)
    ctab, stab = _rope_tables(positions.reshape(n, 1), inv_freq)
    ccmp, scmp = _cmp_rope_tables(ctab, stab, batch, seq)
    x2 = x.reshape(n, D_MODEL)
    for l in range(depth):
        w1, w_ga, w_gs = _pack_w_in(w_in[l])
        o_a, u_tm = _nsa_branch(x2, norm_mix[l][None], w1, ctab, stab, ccmp, scmp, q_gain[l], k_gain[l],
                                (cmp_k_pos[l], cmp_k_w1[l], cmp_k_w2[l]),
                                (cmp_v_pos[l], cmp_v_w1[l], cmp_v_w2[l]), batch, seq)
        lre, lim, w_b, w_c = _s5_weights(s5_a_re[l], s5_a_im[l], s5_log_dt[l], s5_b_re[l], s5_b_im[l],
                                         s5_c_re[l], s5_c_im[l])
        ob_tm = _s5(u_tm, lre, lim, w_b, w_c, s5_d[l][None],
                    s5_glu[l].astype(BF16), batch, seq)
        wrh, wrl, rb = _router_weights(router_group[l], router_group_bias[l], router_expert[l],
                                       router_expert_bias[l])
        x1, h2, logits_t = _merge(x2, norm_mix[l][None], w_ga, w_gs, o_a, ob_tm, proj_nsa[l].astype(BF16),
                                  proj_s5[l].astype(BF16), w_out[l].astype(BF16), norm_ffn[l][None],
                                  wrh, wrl, rb, batch, seq)
        x2 = _moe(x1, h2, logits_t, expert_gate, expert_up, expert_down, l)
    return x2.reshape(batch, seq, D_MODEL)


def _attn_tables(seq):
    n_cmp = (seq - CMP_BLOCK) // CMP_STRIDE + 1
    n_slc = seq // SLC_BLOCK
    key_blk = np.arange(seq) // SLC_BLOCK
    expand_t = (key_blk[:, None] == np.arange(LANES)[None, :]).astype(np.float32)
    cmp_start = np.arange(LANES) * CMP_STRIDE
    slc_start = np.arange(LANES) * SLC_BLOCK
    overlap_t = ((cmp_start[None, :] < slc_start[:, None] + SLC_BLOCK)
                 & (cmp_start[None, :] + CMP_BLOCK > slc_start[:, None])
                 & (np.arange(LANES)[None, :] < n_cmp) & (np.arange(LANES)[:, None] < n_slc))
    return (jnp.asarray(expand_t, BF16), jnp.asarray(overlap_t.astype(np.float32), BF16), n_cmp, n_slc)


def _attn_mask_tables():
    tq, tk, wk = ATT_TQ, ATT_TK, ATT_WK
    ql = np.arange(tq)
    onehot = np.tile(np.eye(tq, dtype=np.float32), (NSA_HEADS, 1))
    k = np.arange(wk)[:, None]
    band = []
    for c in range(WINDOW // tq + 1):
        if c < WINDOW // tq:
            keep = k <= c * tq + ql[None, :]
        else:
            keep = (k > ql[None, :]) & (k <= ql[None, :] + WINDOW)
        band.append(np.where(keep, 0.0, NEG_BIG))
    kl = np.arange(tk)[None, :]
    causal = [np.where(kl <= c * tq + ql[:, None], 0.0, NEG_BIG) for c in range(tk // tq)]
    return (jnp.asarray(onehot, BF16), jnp.asarray(np.stack(band), BF16),
            jnp.asarray(np.stack(causal), F32))


def _attention(q, ks, vs, kw, vw, kc, vc, gates, batch, seq):
    expand_t, overlap_t, n_cmp, n_slc = _attn_tables(seq)
    onehot, band, causal = _attn_mask_tables()
    tq = ATT_TQ
    nq = seq // tq
    rows = NSA_HEADS * tq
    tile = lambda w: pl.BlockSpec((tq, w), lambda b, i: (b * nq + i, 0))
    per_b = lambda r: pl.BlockSpec((r, LANES), lambda b, i: (b, 0))
    f32_rows = pltpu.VMEM((rows, LANES), F32)
    return pl.pallas_call(
        functools.partial(_attn_kernel, n_cmp=n_cmp, n_slc=n_slc, top_k=min(SLC_TOPK, n_slc)),
        grid=(batch, nq),
        in_specs=[tile(2 * NSA_WIDTH), per_b(seq), per_b(seq), per_b(seq), per_b(seq),
                  per_b(LANES), per_b(LANES), tile(LANES),
                  pl.BlockSpec(expand_t.shape, lambda b, i: (0, 0)),
                  pl.BlockSpec(overlap_t.shape, lambda b, i: (0, 0)),
                  pl.BlockSpec(onehot.shape, lambda b, i: (0, 0)),
                  pl.BlockSpec(band.shape, lambda b, i: (0, 0, 0)),
                  pl.BlockSpec(causal.shape, lambda b, i: (0, 0, 0))],
        out_specs=tile(NSA_WIDTH),
        out_shape=jax.ShapeDtypeStruct((batch * seq, NSA_WIDTH), BF16),
        scratch_shapes=[pltpu.VMEM((rows, 2 * LANES), BF16),
                        pltpu.VMEM((rows, ATT_TK), F32), pltpu.VMEM((rows, ATT_TK), BF16),
                        pltpu.VMEM((rows, ATT_WK), F32), pltpu.VMEM((rows, ATT_WK), BF16),
                        f32_rows, f32_rows, f32_rows, f32_rows, f32_rows],
        compiler_params=_cparams("parallel", "parallel"),
        name="nsa_attention",
    )(q, ks, vs, kw, vw, kc, vc, gates, expand_t, overlap_t, onehot, band, causal)
```

```python
import functools
import math

import jax
import jax.numpy as jnp
import numpy as np
from jax import lax
from jax.experimental import pallas as pl
from jax.experimental.pallas import tpu as pltpu
from jax.experimental.pallas import tpu_sc as plsc

F32 = jnp.float32
BF16 = jnp.bfloat16

D_MODEL = 1024
NSA_HEADS = 8
NSA_KV_HEADS = 2
NSA_GROUP = NSA_HEADS // NSA_KV_HEADS
HEAD_DIM = 64
ROPE_DIMS = HEAD_DIM // 4
ROPE_THETA = 500000.0
CMP_BLOCK = 32
CMP_STRIDE = 16
CMP_HIDDEN = 2 * HEAD_DIM
SLC_BLOCK = 64
SLC_TOPK = 16
WINDOW = 512
NSA_WIDTH = NSA_HEADS * HEAD_DIM
KV_WIDTH = NSA_KV_HEADS * HEAD_DIM
S5_WIDTH = 256
S5_GROUP_CH = 16
S5_GROUPS = S5_WIDTH // S5_GROUP_CH
S5_STATE = 64
S5_NSTATE = S5_GROUPS * S5_STATE
N_EXPERT_GROUPS = 4
EXPERTS_PER_GROUP = 8
N_EXPERTS = N_EXPERT_GROUPS * EXPERTS_PER_GROUP
EXPERT_TOP_K = 2
EXPERT_FF = 512
RMS_EPS = 1e-6
NEG_BIG = -1e30

LANES = 128
SUBLANES = 8
VMEM_LIMIT = 56 * 1024 * 1024

IN_TM = 512
ATT_TQ = 128
ATT_TK = 512
ATT_WK = WINDOW + ATT_TQ
ATT_RB = 128
S5_T = 128
MRG_TM = 512
RT_TT = 512
MOE_BLK = 512
DSP_TT = 1024

C_Q, C_KS, C_KW, C_KC, C_VC, C_VS, C_VW, C_U, C_G = 0, 512, 640, 768, 896, 1024, 1152, 1280, 1536
IN_W = 1664


def _cparams(*sem):
    return pltpu.CompilerParams(dimension_semantics=sem, vmem_limit_bytes=VMEM_LIMIT)


def _gelu(x):
    return 0.5 * x * (1.0 + jnp.tanh(math.sqrt(2.0 / math.pi) * (x + 0.044715 * (x * x * x))))


def _sigmoid(x):
    return 1.0 / (1.0 + jnp.exp(-x))


def _dot(a, b):
    return jnp.dot(a, b, preferred_element_type=F32)


def _dot_t(a, b):
    return lax.dot_general(a, b, (((1,), (1,)), ((), ())), preferred_element_type=F32)


def _split_bf16(x):
    hi = x.astype(BF16)
    lo = (x - hi.astype(F32)).astype(BF16)
    return hi, lo


def _pack_rows(x):
    k = x.shape[1] // 2
    hi = lax.bitcast_convert_type(x[:, :k].astype(jnp.bfloat16).astype(F32), jnp.int32)
    lo = lax.bitcast_convert_type(x[:, k:].astype(jnp.bfloat16).astype(F32), jnp.int32)
    return hi | lax.shift_right_logical(lo, 16)


def _unpack_rows(w):
    a = lax.bitcast_convert_type(w & jnp.int32(-65536), F32)
    b = lax.bitcast_convert_type(lax.shift_left(w, 16), F32)
    return a, b


def _rope_kernel(pos_ref, freq_ref, sign_ref, c_ref, s_ref):
    ang = pos_ref[...].astype(F32) * freq_ref[...]
    c_ref[...] = jnp.cos(ang)
    s_ref[...] = jnp.sin(ang) * sign_ref[...]


def _rope_tables(pos_col, inv_freq):
    n = pos_col.shape[0]
    lane = np.arange(LANES) % HEAD_DIM
    sel = np.zeros((LANES, ROPE_DIMS // 2), np.float32)
    for l in range(LANES):
        if lane[l] < ROPE_DIMS:
            sel[l, lane[l] % (ROPE_DIMS // 2)] = 1.0
    freq = (jnp.asarray(sel) * inv_freq[None, :]).sum(-1)[None, :]
    sign = np.where(lane < ROPE_DIMS // 2, -1.0, 1.0).astype(np.float32)[None, :]
    tm = 1024
    return pl.pallas_call(
        _rope_kernel,
        grid=(n // tm,),
        in_specs=[pl.BlockSpec((tm, 1), lambda i: (i, 0)),
                  pl.BlockSpec((1, LANES), lambda i: (0, 0)),
                  pl.BlockSpec((1, LANES), lambda i: (0, 0))],
        out_specs=[pl.BlockSpec((tm, LANES), lambda i: (i, 0))] * 2,
        out_shape=[jax.ShapeDtypeStruct((n, LANES), F32)] * 2,
        compiler_params=_cparams("parallel"),
        name="rope_tables",
    )(pos_col, freq, jnp.asarray(sign))


def _headnorm_rope(t, gain, c, s):
    lane = lax.broadcasted_iota(jnp.int32, t.shape, 1)
    lo = lane < HEAD_DIM
    sq = t * t
    s_lo = jnp.sum(jnp.where(lo, sq, 0.0), axis=-1, keepdims=True)
    s_hi = jnp.sum(jnp.where(lo, 0.0, sq), axis=-1, keepdims=True)
    ms = jnp.where(lo, s_lo, s_hi) * (1.0 / HEAD_DIM)
    tn = (t * lax.rsqrt(ms + RMS_EPS)) * gain
    half = ROPE_DIMS // 2
    partner = jnp.where((lane & (HEAD_DIM - 1)) < half,
                        pltpu.roll(tn, LANES - half, 1), pltpu.roll(tn, half, 1))
    return tn * c + partner * s


def _in_proj_kernel(x_ref, gain_ref, w_ref, c_ref, s_ref, qg_ref, kg_ref,
                    q_ref, ks_ref, kw_ref, kc_ref, vc_ref, vs_ref, vw_ref, u_ref, g_ref):
    x = x_ref[...]
    ms = jnp.mean(x * x, axis=-1, keepdims=True)
    h = (x * lax.rsqrt(ms + RMS_EPS)) * gain_ref[...]
    z = _dot(h.astype(BF16), w_ref[...])
    c = c_ref[...]
    s = s_ref[...]
    lane = lax.broadcasted_iota(jnp.int32, c.shape, 1)
    lo = lane < HEAD_DIM
    scale = HEAD_DIM ** -0.5 * math.log2(math.e)
    for j in range(NSA_HEADS // 2):
        r = _headnorm_rope(z[:, C_Q + LANES * j:C_Q + LANES * (j + 1)], qg_ref[...], c, s) * scale
        rr = pltpu.roll(r, HEAD_DIM, 1)
        if j // (NSA_GROUP // 2) == 0:
            even, odd = jnp.where(lo, r, 0.0), jnp.where(lo, rr, 0.0)
        else:
            even, odd = jnp.where(lo, 0.0, rr), jnp.where(lo, 0.0, r)
        q_ref[:, 2 * LANES * j:2 * LANES * j + LANES] = even.astype(BF16)
        q_ref[:, 2 * LANES * j + LANES:2 * LANES * (j + 1)] = odd.astype(BF16)
    ks_ref[...] = _headnorm_rope(z[:, C_KS:C_KS + LANES], kg_ref[...], c, s).astype(BF16)
    kw_ref[...] = _headnorm_rope(z[:, C_KW:C_KW + LANES], kg_ref[...], c, s).astype(BF16)
    kc_ref[...] = z[:, C_KC:C_KC + LANES].astype(BF16)
    vc_ref[...] = z[:, C_VC:C_VC + LANES].astype(BF16)
    vs_ref[...] = z[:, C_VS:C_VS + LANES].astype(BF16)
    vw_ref[...] = z[:, C_VW:C_VW + LANES].astype(BF16)
    u_ref[:, pl.ds(pl.program_id(1), 1), :] = z[:, C_U:C_U + S5_WIDTH][:, None, :]
    g_ref[...] = _sigmoid(z[:, C_G:C_G + LANES])


def _in_proj(x2, gain, w1, ctab, stab, qg, kg, batch, seq):
    n = x2.shape[0]
    tm = IN_TM
    nt = seq // tm
    row = lambda i, b: (b * nt + i, 0)
    const = lambda i, b: (0, 0)
    tok = lambda w: pl.BlockSpec((tm, w), row)
    outs = [(n, 2 * NSA_WIDTH, BF16)] + [(n, LANES, BF16)] * 6
    return pl.pallas_call(
        _in_proj_kernel,
        grid=(nt, batch),
        in_specs=[tok(D_MODEL), pl.BlockSpec((1, D_MODEL), const), pl.BlockSpec((D_MODEL, IN_W), const),
                  tok(LANES), tok(LANES), pl.BlockSpec((1, LANES), const), pl.BlockSpec((1, LANES), const)],
        out_specs=[tok(2 * NSA_WIDTH)] + [tok(LANES)] * 6
        + [pl.BlockSpec((tm, batch, S5_WIDTH), lambda i, b: (i, 0, 0)), tok(LANES)],
        out_shape=[jax.ShapeDtypeStruct(s[:2], s[2]) for s in outs]
        + [jax.ShapeDtypeStruct((seq, batch, S5_WIDTH), F32), jax.ShapeDtypeStruct((n, LANES), F32)],
        compiler_params=_cparams("parallel", "arbitrary"),
        name="in_proj",
    )(x2, gain, w1, ctab, stab, qg, kg)


def _compress_mlp(h_ref, w_ref, p_ref, w2_ref):
    w = w_ref[...]
    pre = _dot(h_ref[...], w)
    phi, plo = _split_bf16(p_ref[...])
    pb = _dot(phi, w) + _dot(plo, w)
    rows = pre.shape[0]
    out = None
    for h in range(NSA_KV_HEADS):
        top = pre[:, LANES * h:LANES * (h + 1)]
        bot = pre[:, LANES * (2 + h):LANES * (3 + h)]
        bias = pb[0:1, LANES * h:LANES * (h + 1)] + pb[1:2, LANES * (2 + h):LANES * (3 + h)]
        a = _gelu(top + pltpu.roll(bot, rows - 1, 0) + bias)
        o = _dot(a.astype(BF16), w2_ref[h])
        out = o if out is None else out + o
    return out


def _compress_kernel(hk_ref, hv_ref, wk_ref, wv_ref, pk_ref, pv_ref, w2k_ref, w2v_ref,
                     kg_ref, c_ref, s_ref, kc_ref, vc_ref):
    k = _compress_mlp(hk_ref, wk_ref, pk_ref, w2k_ref)
    kc_ref[...] = _headnorm_rope(k, kg_ref[...], c_ref[...], s_ref[...]).astype(BF16)
    vc_ref[...] = _compress_mlp(hv_ref, wv_ref, pv_ref, w2v_ref).astype(BF16)


def _compress(hk, hv, wk, wv, pk, pv, w2k, w2v, kg, ccmp, scmp, batch):
    rows = hk.shape[0] // batch
    width = hk.shape[1]
    const2 = lambda b: (0, 0)
    const3 = lambda b: (0, 0, 0)
    rowb = lambda w: pl.BlockSpec((rows, w), lambda b: (b, 0))
    return pl.pallas_call(
        _compress_kernel,
        grid=(batch,),
        in_specs=[rowb(width), rowb(width),
                  pl.BlockSpec(wk.shape, const2), pl.BlockSpec(wv.shape, const2),
                  pl.BlockSpec(pk.shape, const2), pl.BlockSpec(pv.shape, const2),
                  pl.BlockSpec(w2k.shape, const3), pl.BlockSpec(w2v.shape, const3),
                  pl.BlockSpec((1, LANES), const2), rowb(LANES), rowb(LANES)],
        out_specs=[rowb(LANES), rowb(LANES)],
        out_shape=[jax.ShapeDtypeStruct((hk.shape[0], LANES), BF16)] * 2,
        compiler_params=_cparams("parallel"),
        name="compress_kv",
    )(hk, hv, wk, wv, pk, pv, w2k, w2v, kg, ccmp, scmp)


def _compress_weights(pos_emb, w1, w2):
    half = CMP_BLOCK // 2
    eye = jnp.eye(NSA_KV_HEADS, dtype=F32)
    w1r = w1.reshape(2, half, HEAD_DIM, CMP_HIDDEN)
    wexp = jnp.einsum("sldn,ph->lpdshn", w1r, eye)
    wexp = wexp.reshape(half * KV_WIDTH, 2 * NSA_KV_HEADS * CMP_HIDDEN).astype(BF16)
    pexp = jnp.broadcast_to(pos_emb.reshape(2, half, 1, HEAD_DIM), (2, half, NSA_KV_HEADS, HEAD_DIM))
    pexp = jnp.pad(pexp.reshape(2, half * KV_WIDTH), ((0, SUBLANES - 2), (0, 0)))
    w2e = jnp.einsum("cd,ph->pchd", w2, eye).reshape(NSA_KV_HEADS, CMP_HIDDEN, KV_WIDTH)
    return wexp, pexp, w2e.astype(BF16)


def _attn_kernel(q_ref, ks_ref, vs_ref, kw_ref, vw_ref, kc_ref, vc_ref, g_ref, et_ref, ovt_ref, oh_ref, wb_ref,
                 cb_ref, o_ref, qc_ref, s_ref, p_ref, sw_ref, pw_ref, m_ref, al_ref, oc_ref, as_ref, aw_ref,
                 *, n_cmp, n_slc, top_k):
    tq, tk, wk, rb, nh = ATT_TQ, ATT_TK, ATT_WK, ATT_RB, NSA_HEADS
    rows, grp_rows = nh * tq, NSA_GROUP * tq
    qk_rows = 2 * tq
    q0 = pl.program_id(1) * tq
    for a in range(nh):
        qc_ref[a * tq:(a + 1) * tq, 0:LANES] = q_ref[:, LANES * a:LANES * (a + 1)]
    q8 = qc_ref[:, 0:LANES]
    lane = lax.broadcasted_iota(jnp.int32, (tq, LANES), 1)
    t_row = q0 + lax.broadcasted_iota(jnp.int32, (tq, LANES), 0)

    s_c = _dot_t(q8, kc_ref[...]).reshape(nh, tq, LANES)
    mask_c = ((lane * CMP_STRIDE + (CMP_BLOCK - 1)) <= t_row) & (lane < n_cmp)
    sm = jnp.where(mask_c[None], s_c, NEG_BIG)
    p = jnp.where(mask_c[None], jnp.exp2(sm - jnp.max(sm, axis=-1, keepdims=True)), 0.0)
    l = jnp.sum(p, axis=-1, keepdims=True)
    p_c = p * jnp.where(l > 0.0, 1.0 / l, 0.0)
    oc_ref[...] = _dot(p_c.reshape(rows, LANES).astype(BF16), vc_ref[...])

    blk = lax.broadcasted_iota(jnp.int32, (n_slc, tq), 0)
    cur = (q0 + lax.broadcasted_iota(jnp.int32, (n_slc, tq), 1)) >> int(math.log2(SLC_BLOCK))
    forced = (blk == 0) | (blk == cur) | (blk == cur - 1)
    for h in range(NSA_KV_HEADS):
        ps = p_c[NSA_GROUP * h]
        for g in range(1, NSA_GROUP):
            ps = ps + p_c[NSA_GROUP * h + g]
        hi, lo = _split_bf16(ps)
        imp = (_dot_t(ovt_ref[...], hi) + _dot_t(ovt_ref[...], lo))[0:n_slc]
        imp = jnp.where(forced, jnp.inf, jnp.where(blk > cur, -jnp.inf, imp))
        rank = jnp.zeros((n_slc, tq), F32)
        for i in range(n_slc):
            row = imp[i:i + 1, :]
            rank = rank + jnp.where((row > imp) | ((row == imp) & (blk > i)), 1.0, 0.0)
        bias_t = jnp.where(rank < float(top_k), 0.0, NEG_BIG)
        bias = jnp.concatenate([bias_t, jnp.zeros((LANES - n_slc, tq), F32)], axis=0).T.astype(BF16)
        for g in range(NSA_GROUP):
            a = NSA_GROUP * h + g
            qc_ref[a * tq:(a + 1) * tq, LANES:2 * LANES] = bias

    def softmax_pass(s_buf, p_buf, width, causal_case, online):
        for r in range(rows // rb):
            r0 = r * rb
            s = s_buf[pl.ds(r0, rb), 0:width]
            if causal_case is not None:
                s = s + cb_ref[causal_case, pl.ds(r0 % tq, rb), :]
            m_new = jnp.broadcast_to(jnp.max(s, axis=1, keepdims=True), (rb, LANES))
            if online:
                m_old = m_ref[pl.ds(r0, rb), :]
                m_new = jnp.maximum(m_old, m_new)
                al_ref[pl.ds(r0, rb), :] = jnp.exp2(m_old - m_new)
                m_ref[pl.ds(r0, rb), :] = m_new
            pp = jnp.exp2(s - jnp.concatenate([m_new] * (width // LANES), axis=1))
            p_buf[pl.ds(r0, rb), 0:width] = pp.astype(BF16)

    def pv(acc_ref, p_buf, v, width, online):
        lane_v = lax.broadcasted_iota(jnp.int32, v.shape, 1)
        for h in range(NSA_KV_HEADS):
            rs = slice(h * grp_rows, (h + 1) * grp_rows)
            own = (lane_v < HEAD_DIM) if h == 0 else (lane_v >= HEAD_DIM)
            upd = _dot(p_buf[rs, 0:width], jnp.where(own, v, jnp.ones_like(v)))
            acc_ref[rs, :] = (al_ref[rs, :] * acc_ref[rs, :] + upd) if online else upd

    m_ref[...] = jnp.full(m_ref.shape, 0.1 * NEG_BIG, F32)
    as_ref[...] = jnp.zeros(as_ref.shape, F32)

    def slc_chunk(c, diagonal):
        k0 = pl.multiple_of(c * tk, tk)
        kcat = jnp.concatenate([ks_ref[pl.ds(k0, tk), :], et_ref[pl.ds(k0, tk), :]], axis=1)
        for g in range(rows // qk_rows):
            rs = slice(g * qk_rows, (g + 1) * qk_rows)
            s_ref[rs, :] = _dot_t(qc_ref[rs, :], kcat)
        softmax_pass(s_ref, p_ref, tk, ((q0 - k0) >> int(math.log2(tq))) if diagonal else None, True)
        pv(as_ref, p_ref, vs_ref[pl.ds(k0, tk), :], tk, True)

    c_last = q0 >> int(math.log2(tk))

    def full_chunk(c, carry):
        slc_chunk(c, False)
        return carry

    lax.fori_loop(0, c_last, full_chunk, 0)
    slc_chunk(c_last, True)

    w0 = pl.multiple_of(jnp.maximum(q0 - WINDOW, 0), LANES)
    w_case = jnp.minimum(q0 >> int(math.log2(tq)), WINDOW // tq)
    kcat_w = jnp.concatenate([kw_ref[pl.ds(w0, wk), :], wb_ref[w_case]], axis=1)
    for g in range(rows // qk_rows):
        rs = slice(g * qk_rows, (g + 1) * qk_rows)
        sw_ref[rs, :] = _dot_t(jnp.concatenate([qc_ref[rs, 0:LANES], oh_ref[rs, :]], axis=1), kcat_w)
    softmax_pass(sw_ref, pw_ref, wk, None, False)
    pv(aw_ref, pw_ref, vw_ref[pl.ds(w0, wk), :], wk, False)

    lo_half = lane < HEAD_DIM
    for j in range(nh // 2):
        comb = []
        for a in (2 * j, 2 * j + 1):
            rs = slice(a * tq, (a + 1) * tq)
            o_s, o_w = as_ref[rs, :], aw_ref[rs, :]
            comb.append(g_ref[:, 3 * a:3 * a + 1] * oc_ref[rs, :]
                        + g_ref[:, 3 * a + 1:3 * a + 2] * (o_s * (1.0 / pltpu.roll(o_s, HEAD_DIM, 1)))
                        + g_ref[:, 3 * a + 2:3 * a + 3] * (o_w * (1.0 / pltpu.roll(o_w, HEAD_DIM, 1))))
        even, odd = comb
        if j // (NSA_GROUP // 2) == 0:
            pair = jnp.where(lo_half, even, pltpu.roll(odd, HEAD_DIM, 1))
        else:
            pair = jnp.where(lo_half, pltpu.roll(even, HEAD_DIM, 1), odd)
        o_ref[:, LANES * j:LANES * (j + 1)] = pair.astype(BF16)


def _pack_w_in(w):
    sizes = (NSA_WIDTH,) + (KV_WIDTH,) * 6 + (3 * NSA_HEADS, S5_WIDTH, D_MODEL, D_MODEL)
    o = np.cumsum((0,) + sizes)
    q, kc, vc, ks, vs, kw, vw, g, u, ga, gs = [w[:, o[i]:o[i + 1]] for i in range(len(sizes))]
    g = jnp.pad(g, ((0, 0), (0, LANES - 3 * NSA_HEADS)))
    w1 = jnp.concatenate([q, ks, kw, kc, vc, vs, vw, u, g], axis=1).astype(BF16)
    return w1, ga.astype(BF16), gs.astype(BF16)


def _nsa_branch(x2, gain, w1, ctab, stab, ccmp, scmp, q_gain, k_gain, cmp_k, cmp_v, batch, seq):
    qg = jnp.tile(q_gain, 2)[None, :]
    kg = jnp.tile(k_gain, 2)[None, :]
    q, ks, kw, kc_raw, vc_raw, vs, vw, u_tm, gates = _in_proj(x2, gain, w1, ctab, stab, qg, kg, batch, seq)
    half_rows = batch * seq // (CMP_BLOCK // 2)
    hk = kc_raw.reshape(half_rows, (CMP_BLOCK // 2) * KV_WIDTH)
    hv = vc_raw.reshape(half_rows, (CMP_BLOCK // 2) * KV_WIDTH)
    wk, pk, w2k = _compress_weights(*cmp_k)
    wv, pv, w2v = _compress_weights(*cmp_v)
    kc, vc = _compress(hk, hv, wk, wv, pk, pv, w2k, w2v, kg, ccmp, scmp, batch)
    o_a = _attention(q, ks, vs, kw, vw, kc, vc, gates, batch, seq)
    return o_a, u_tm


def _cmp_rope_tables(ctab, stab, batch, seq):
    def pick(t):
        t = t.reshape(batch, seq, LANES)[:, CMP_BLOCK - 1::CMP_STRIDE]
        t = jnp.pad(t, ((0, 0), (0, seq // CMP_STRIDE - t.shape[1]), (0, 0)))
        return t.reshape(batch * (seq // CMP_STRIDE), LANES)
    return pick(ctab), pick(stab)


def _s5_disc_kernel(are_ref, aim_ref, ldt_ref, lre_ref, lim_ref, fre_ref, fim_ref):
    lam_re = jnp.minimum(are_ref[...], -1e-4)
    lam_im = aim_ref[...]
    dt = jnp.exp(ldt_ref[...])
    mag = jnp.exp(lam_re * dt)
    ang = lam_im * dt
    lb_re = mag * jnp.cos(ang)
    lb_im = mag * jnp.sin(ang)
    den = lam_re * lam_re + lam_im * lam_im
    lre_ref[...] = lb_re
    lim_ref[...] = lb_im
    fre_ref[...] = ((lb_re - 1.0) * lam_re + lb_im * lam_im) / den
    fim_ref[...] = (lb_im * lam_re - (lb_re - 1.0) * lam_im) / den


def _s5_weights(a_re, a_im, log_dt, b_re, b_im, c_re, c_im):
    shp = jax.ShapeDtypeStruct((S5_GROUPS, S5_STATE), F32)
    lb_re, lb_im, f_re, f_im = pl.pallas_call(
        _s5_disc_kernel, out_shape=[shp] * 4, name="s5_discretise",
    )(a_re, a_im, log_dt[:, None])
    bb_re = f_re[..., None] * b_re - f_im[..., None] * b_im
    bb_im = f_re[..., None] * b_im + f_im[..., None] * b_re
    eye = jnp.eye(S5_GROUPS, dtype=F32)

    def in_map(bb):
        return jnp.einsum("gpc,gh->gchp", bb, eye).reshape(S5_WIDTH, S5_NSTATE)

    def out_map(cc):
        return jnp.einsum("gcp,gh->hpgc", cc, eye).reshape(S5_NSTATE, S5_WIDTH)

    w_b = jnp.concatenate([in_map(bb_re), in_map(bb_im)], axis=1).astype(BF16)
    w_c = jnp.concatenate([out_map(c_re), -out_map(c_im)], axis=0).astype(BF16)
    return lb_re.reshape(1, S5_NSTATE), lb_im.reshape(1, S5_NSTATE), w_b, w_c


def _s5_kernel(u_ref, wb_ref, lre_ref, lim_ref, wc_ref, d_ref, wg_ref, o_ref, x_ref, st_ref, *, batch):
    ns = S5_NSTATE

    @pl.when(pl.program_id(0) == 0)
    def _():
        st_ref[...] = jnp.zeros(st_ref.shape, F32)

    u = u_ref[...].reshape(u_ref.shape[0] * batch, S5_WIDTH)
    x_ref[...] = _dot(u.astype(BF16), wb_ref[...])
    lre = jnp.broadcast_to(lre_ref[...], (batch, ns))
    lim = jnp.broadcast_to(lim_ref[...], (batch, ns))

    def step(t, carry):
        xr, xi = carry
        r0 = pl.multiple_of(t * batch, batch)
        nr = lre * xr - lim * xi + x_ref[pl.ds(r0, batch), 0:ns]
        ni = lre * xi + lim * xr + x_ref[pl.ds(r0, batch), ns:2 * ns]
        x_ref[pl.ds(r0, batch), 0:ns] = nr
        x_ref[pl.ds(r0, batch), ns:2 * ns] = ni
        return nr, ni

    steps = u.shape[0] // batch
    xr, xi = lax.fori_loop(0, steps, step, (st_ref[:, 0:ns], st_ref[:, ns:2 * ns]), unroll=4)
    st_ref[:, 0:ns] = xr
    st_ref[:, ns:2 * ns] = xi
    y = _dot(x_ref[...].astype(BF16), wc_ref[...]) + d_ref[...] * u
    z = _dot(_gelu(y).astype(BF16), wg_ref[...])
    o = z[:, :S5_WIDTH] * _sigmoid(z[:, S5_WIDTH:])
    for c in range(S5_WIDTH // LANES):
        o_ref[c] = o[:, c * LANES:(c + 1) * LANES]


def _s5(u_tm, lre, lim, w_b, w_c, d_skip, w_glu, batch, seq):
    rows = S5_T * batch
    const = lambda t: (0, 0)
    return pl.pallas_call(
        functools.partial(_s5_kernel, batch=batch),
        grid=(seq // S5_T,),
        in_specs=[pl.BlockSpec((S5_T, batch, S5_WIDTH), lambda t: (t, 0, 0)),
                  pl.BlockSpec(w_b.shape, const), pl.BlockSpec(lre.shape, const), pl.BlockSpec(lim.shape, const),
                  pl.BlockSpec(w_c.shape, const), pl.BlockSpec((1, S5_WIDTH), const),
                  pl.BlockSpec(w_glu.shape, const)],
        out_specs=pl.BlockSpec((S5_WIDTH // LANES, rows, LANES), lambda t: (0, t, 0)),
        out_shape=jax.ShapeDtypeStruct((S5_WIDTH // LANES, seq * batch, LANES), F32),
        scratch_shapes=[pltpu.VMEM((rows, 2 * S5_NSTATE), F32), pltpu.VMEM((batch, 2 * S5_NSTATE), F32)],
        compiler_params=_cparams("arbitrary"),
        name="s5_scan",
    )(u_tm, w_b, lre, lim, w_c, d_skip, w_glu)


RT_GROUP_ROW = 0
RT_EXPERT_ROW = SUBLANES


def _rmsnorm(x, gain):
    return (x * lax.rsqrt(jnp.mean(x * x, axis=-1, keepdims=True) + RMS_EPS)) * gain


def _merge_kernel(x_ref, g1_ref, wga_ref, wgs_ref, oa_ref, ob_ref, pa_ref, ps_ref, wo_ref, g2_ref,
                  wrh_ref, wrl_ref, rb_ref, x1_ref, h2_ref, lg_ref):
    x = x_ref[...]
    h = _rmsnorm(x, g1_ref[...]).astype(BF16)
    gate_a = _sigmoid(_dot(h, wga_ref[...]))
    gate_s = _sigmoid(_dot(h, wgs_ref[...]))
    rows_b = pl.ds(pl.program_id(1), x.shape[0], stride=ob_ref.shape[1] // x.shape[0])
    o_b = jnp.concatenate([ob_ref[c, rows_b, :] for c in range(S5_WIDTH // LANES)], axis=1)
    merged = gate_a * _dot(oa_ref[...], pa_ref[...]) + gate_s * _dot(o_b.astype(BF16), ps_ref[...])
    x1 = x + _dot(merged.astype(BF16), wo_ref[...])
    x1_ref[...] = x1
    h2 = _rmsnorm(x1, g2_ref[...])
    h2_ref[...] = _pack_rows(h2)
    hi, lo = _split_bf16(h2)
    lg_ref[...] = (_dot_t(wrh_ref[...], hi) + _dot_t(wrh_ref[...], lo) + _dot_t(wrl_ref[...], hi)
                   + rb_ref[...])


def _router_weights(w_rg, b_rg, w_re, b_re):
    def rows(group_part, expert_part):
        return jnp.concatenate([
            group_part, jnp.zeros((RT_EXPERT_ROW - N_EXPERT_GROUPS,) + group_part.shape[1:], F32),
            expert_part, jnp.zeros((LANES - RT_EXPERT_ROW - N_EXPERTS,) + group_part.shape[1:], F32)], axis=0)

    hi, lo = _split_bf16(rows(w_rg.T, w_re.T))
    return hi, lo, rows(b_rg[:, None], b_re[:, None])


def _merge(x2, g1, w_ga, w_gs, o_a, ob_tm, p_a, p_s, w_o, g2, wrh, wrl, rb, batch, seq):
    n = x2.shape[0]
    tm = MRG_TM
    nt = seq // tm
    row = lambda i, b: (b * nt + i, 0)
    const = lambda i, b: (0, 0)
    tok = lambda w: pl.BlockSpec((tm, w), row)
    full = lambda a: pl.BlockSpec(a.shape, const)
    return pl.pallas_call(
        _merge_kernel,
        grid=(nt, batch),
        in_specs=[tok(D_MODEL), full(g1), full(w_ga), full(w_gs), tok(NSA_WIDTH),
                  pl.BlockSpec((S5_WIDTH // LANES, tm * batch, LANES), lambda i, b: (0, i, 0)),
                  full(p_a), full(p_s), full(w_o),
                  full(g2), full(wrh), full(wrl), full(rb)],
        out_specs=[tok(D_MODEL), tok(D_MODEL // 2), pl.BlockSpec((LANES, tm), lambda i, b: (0, b * nt + i))],
        out_shape=[jax.ShapeDtypeStruct((n, D_MODEL), F32), jax.ShapeDtypeStruct((n, D_MODEL // 2), jnp.int32),
                   jax.ShapeDtypeStruct((LANES, n), F32)],
        compiler_params=_cparams("parallel", "parallel"),
        name="merge_out_proj",
    )(x2, g1, w_ga, w_gs, o_a, ob_tm, p_a, p_s, w_o, g2, wrh, wrl, rb)


def _first_argmax(v, rows):
    vmax = jnp.max(v, axis=0, keepdims=True)
    idx = jnp.min(jnp.where(v == vmax, rows, SUBLANES), axis=0, keepdims=True)
    return vmax, idx


def _route_kernel(lg_ref, tri_ref, e_ref, w_ref, r_ref, cnt_ref, carry_ref):
    tt = lg_ref.shape[1]

    @pl.when(pl.program_id(0) == 0)
    def _():
        carry_ref[...] = jnp.zeros(carry_ref.shape, F32)

    rows = lax.broadcasted_iota(jnp.int32, (SUBLANES, tt), 0)
    gl = jnp.where(rows < N_EXPERT_GROUPS, lg_ref[RT_GROUP_ROW:RT_GROUP_ROW + SUBLANES, :], -jnp.inf)
    gmax, g_sel = _first_argmax(gl, rows)
    g_w = 1.0 / jnp.sum(jnp.exp(gl - gmax), axis=0, keepdims=True)

    el = jnp.zeros((EXPERTS_PER_GROUP, tt), F32)
    for g in range(N_EXPERT_GROUPS):
        r0 = RT_EXPERT_ROW + EXPERTS_PER_GROUP * g
        el = jnp.where(g_sel == g, lg_ref[r0:r0 + EXPERTS_PER_GROUP, :], el)
    ex = jnp.exp(el - jnp.max(el, axis=0, keepdims=True))
    p = ex / jnp.sum(ex, axis=0, keepdims=True)
    p1, i1 = _first_argmax(p, rows)
    p2, i2 = _first_argmax(jnp.where(rows == i1, -1.0, p), rows)
    e1 = g_sel * EXPERTS_PER_GROUP + i1
    e2 = g_sel * EXPERTS_PER_GROUP + i2
    scale = g_w / (p1 + p2)

    erow = lax.broadcasted_iota(jnp.int32, (N_EXPERTS, tt), 0)
    oh1 = jnp.where(erow == e1, 1.0, 0.0)
    oh2 = jnp.where(erow == e2, 1.0, 0.0)
    both = oh1 + oh2
    before = carry_ref[:, 0:1] + _dot(both.astype(BF16), tri_ref[...])
    carry_ref[...] = carry_ref[...] + jnp.sum(both, axis=1, keepdims=True)
    cnt_ref[...] = carry_ref[...]

    zi = jnp.zeros((SUBLANES - 2, tt), jnp.int32)
    e_ref[...] = jnp.concatenate([e1, e2, zi], axis=0)
    w_ref[...] = jnp.concatenate([p1 * scale, p2 * scale, zi.astype(F32)], axis=0)
    r_ref[...] = jnp.concatenate([jnp.sum(oh1 * before, axis=0, keepdims=True),
                                  jnp.sum(oh2 * before, axis=0, keepdims=True),
                                  zi.astype(F32)], axis=0).astype(jnp.int32)


def _route(logits_t):
    n = logits_t.shape[1]
    tt = RT_TT
    tri = jnp.asarray(np.triu(np.ones((tt, tt), np.float32), 1), BF16)
    blk = pl.BlockSpec((SUBLANES, tt), lambda i: (0, i))
    return pl.pallas_call(
        _route_kernel,
        grid=(n // tt,),
        in_specs=[pl.BlockSpec((LANES, tt), lambda i: (0, i)), pl.BlockSpec((tt, tt), lambda i: (0, 0))],
        out_specs=[blk, blk, blk, pl.BlockSpec((N_EXPERTS, LANES), lambda i: (0, 0))],
        out_shape=[jax.ShapeDtypeStruct((SUBLANES, n), jnp.int32), jax.ShapeDtypeStruct((SUBLANES, n), F32),
                   jax.ShapeDtypeStruct((SUBLANES, n), jnp.int32), jax.ShapeDtypeStruct((N_EXPERTS, LANES), F32)],
        scratch_shapes=[pltpu.VMEM((N_EXPERTS, LANES), F32)],
        compiler_params=_cparams("arbitrary"),
        name="moe_route",
    )(logits_t, tri)


def _expert_kernel(be_ref, nb_ref, x_ref, wg_ref, wu_ref, wd_ref, o_ref, wgb_ref, wub_ref, wdb_ref):
    i = pl.program_id(0)
    half = D_MODEL // 2

    @pl.when(i < nb_ref[0])
    def _():
        @pl.when((i == 0) | (be_ref[i] != be_ref[jnp.maximum(i - 1, 0)]))
        def _():
            wgb_ref[...] = wg_ref[...].astype(BF16)
            wub_ref[...] = wu_ref[...].astype(BF16)
            wdb_ref[...] = wd_ref[...].astype(BF16)

        xa, xb = _unpack_rows(x_ref[...])
        xa, xb = xa.astype(BF16), xb.astype(BF16)
        g = _dot(xa, wgb_ref[0:half, :]) + _dot(xb, wgb_ref[half:, :])
        u = _dot(xa, wub_ref[0:half, :]) + _dot(xb, wub_ref[half:, :])
        o_ref[...] = _pack_rows(_dot(((g * _sigmoid(g)) * u).astype(BF16), wdb_ref[...]))

    @pl.when(i >= nb_ref[0])
    def _():
        o_ref[...] = jnp.zeros(o_ref.shape, o_ref.dtype)


def _experts(blk_expert, n_used, buf, w_gate, w_up, w_down, layer):
    cap, width = buf.shape
    blk = MOE_BLK
    row = lambda i, be, nb: (jnp.minimum(i, nb[0] - 1), 0)
    wmap = lambda i, be, nb: (layer, be[jnp.minimum(i, nb[0] - 1)], 0, 0)
    return pl.pallas_call(
        _expert_kernel,
        grid_spec=pltpu.PrefetchScalarGridSpec(
            num_scalar_prefetch=2,
            grid=(cap // blk,),
            in_specs=[pl.BlockSpec((blk, width), row),
                      pl.BlockSpec((None, None, D_MODEL, EXPERT_FF), wmap),
                      pl.BlockSpec((None, None, D_MODEL, EXPERT_FF), wmap),
                      pl.BlockSpec((None, None, EXPERT_FF, D_MODEL), wmap)],
            out_specs=pl.BlockSpec((blk, width), lambda i, be, nb: (i, 0)),
            scratch_shapes=[pltpu.VMEM((D_MODEL, EXPERT_FF), BF16), pltpu.VMEM((D_MODEL, EXPERT_FF), BF16),
                            pltpu.VMEM((EXPERT_FF, D_MODEL), BF16)],
        ),
        out_shape=jax.ShapeDtypeStruct((cap, width), buf.dtype),
        compiler_params=_cparams("arbitrary"),
        name="moe_experts",
    )(blk_expert, n_used, buf, w_gate, w_up, w_down)


SC_CH = 64


def _sc_mesh():
    return plsc.VectorSubcoreMesh(core_axis_name="c", subcore_axis_name="s")


def _sc_worker(info):
    return lax.axis_index("s") * info.num_cores + lax.axis_index("c")


def _sc_gather_rows(table, idx, rows=None):
    info = plsc.get_sparse_core_info()
    lanes, workers = info.num_lanes, info.num_cores * info.num_subcores
    invert = rows is not None
    rows = rows if invert else idx.shape[0]
    width = table.shape[1]
    per_w = rows // workers
    n_chunks = per_w // SC_CH
    assert per_w % (2 * SC_CH) == 0
    scan = 2048
    n_tok = table.shape[0]

    def body(table_hbm, idx_hbm, out_hbm, idx_v, rows_a, rows_b, sem_a, sem_b, *scan_v):
        base = _sc_worker(info) * per_w
        if invert:
            lane = lax.iota(jnp.int32, lanes)

            @pl.loop(0, per_w // lanes)
            def _(j):
                idx_v[pl.ds(j * lanes, lanes)] = (base + j * lanes + lane) & (n_tok - 1)

            @pl.loop(0, idx.shape[0] // scan)
            def _(c):
                pltpu.sync_copy(idx_hbm.at[pl.ds(c * scan, scan)], scan_v[0])

                @pl.loop(0, scan // lanes)
                def _(j):
                    rel = scan_v[0][pl.ds(j * lanes, lanes)] - base
                    mine = (rel >= 0) & (rel < per_w)
                    token = (c * scan + j * lanes + lane) & (n_tok - 1)
                    plsc.store_scatter(idx_v, [jnp.where(mine, rel, 0)], token, mask=mine)
        else:
            pltpu.sync_copy(idx_hbm.at[pl.ds(base, per_w)], idx_v)

        def gather(j, buf, sem):
            return pltpu.make_async_copy(table_hbm.at[idx_v.at[pl.ds(j * SC_CH, SC_CH)]], buf, sem)

        def write(j, buf):
            pltpu.sync_copy(buf, out_hbm.at[pl.ds(base + j * SC_CH, SC_CH)])

        gather(0, rows_a, sem_a).start()

        @pl.loop(0, n_chunks, step=2)
        def _(j):
            gather(j + 1, rows_b, sem_b).start()
            gather(j, rows_a, sem_a).wait()
            write(j, rows_a)

            @pl.when(j + 2 < n_chunks)
            def _():
                gather(j + 2, rows_a, sem_a).start()

            gather(j + 1, rows_b, sem_b).wait()
            write(j + 1, rows_b)

    chunk_buf = pltpu.VMEM((SC_CH, width), table.dtype)
    scratch = [pltpu.VMEM((per_w,), jnp.int32), chunk_buf, chunk_buf,
               pltpu.SemaphoreType.DMA, pltpu.SemaphoreType.DMA]
    return pl.kernel(
        body, out_type=jax.ShapeDtypeStruct((rows, width), table.dtype), mesh=_sc_mesh(),
        scratch_types=scratch + ([pltpu.VMEM((scan,), jnp.int32)] if invert else []),
        compiler_params=pltpu.CompilerParams(needs_layout_passes=False) if invert else None,
        name="sc_dispatch_rows" if invert else "sc_gather_rows",
    )(table, idx)


def _blend_kernel(x1_ref, w_ref, g0_ref, g1_ref, o_ref):
    half = x1_ref.shape[1] // 2
    a0, b0 = _unpack_rows(g0_ref[...])
    a1, b1 = _unpack_rows(g1_ref[...])
    w0, w1 = w_ref[:, 0:1], w_ref[:, 1:2]
    o_ref[:, 0:half] = x1_ref[:, 0:half] + w0 * a0 + w1 * a1
    o_ref[:, half:] = x1_ref[:, half:] + w0 * b0 + w1 * b1


def _blend(x1, w_col, gathered):
    n = x1.shape[0]
    tt = DSP_TT
    nt = n // tt
    width = gathered.shape[1]
    return pl.pallas_call(
        _blend_kernel,
        grid=(nt,),
        in_specs=[pl.BlockSpec((tt, D_MODEL), lambda i: (i, 0)),
                  pl.BlockSpec((tt, EXPERT_TOP_K), lambda i: (i, 0)),
                  pl.BlockSpec((tt, width), lambda i: (i, 0)),
                  pl.BlockSpec((tt, width), lambda i: (nt + i, 0))],
        out_specs=pl.BlockSpec((tt, D_MODEL), lambda i: (i, 0)),
        out_shape=jax.ShapeDtypeStruct((n, D_MODEL), F32),
        compiler_params=_cparams("parallel"),
        name="moe_blend",
    )(x1, w_col, gathered, gathered)


def _moe(x1, h2, logits_t, w_gate, w_up, w_down, layer):
    n = x1.shape[0]
    e_idx, wts, rank, counts = _route(logits_t)
    counts = counts[:, 0].astype(jnp.int32)
    padded = (counts + MOE_BLK - 1) // MOE_BLK * MOE_BLK
    pends = jnp.cumsum(padded)
    pstarts = pends - padded
    experts = jnp.arange(N_EXPERTS, dtype=jnp.int32)
    start_of = jnp.sum(jnp.where(e_idx[None, :EXPERT_TOP_K] == experts[:, None, None],
                                 pstarts[:, None, None], 0), axis=0)
    dest = start_of + rank[:EXPERT_TOP_K]
    cap = n * EXPERT_TOP_K + N_EXPERTS * MOE_BLK
    n_blk = cap // MOE_BLK
    blk_row = jnp.arange(n_blk, dtype=jnp.int32) * MOE_BLK
    blk_expert = jnp.minimum(jnp.sum((pends[None, :] <= blk_row[:, None]).astype(jnp.int32), axis=1),
                             N_EXPERTS - 1)
    n_used = (pends[-1:] // MOE_BLK).astype(jnp.int32)
    dest_flat = dest.reshape(EXPERT_TOP_K * n)
    buf = _sc_gather_rows(h2, dest_flat, rows=cap)
    out_buf = _experts(blk_expert, n_used, buf, w_gate, w_up, w_down, layer)
    return _blend(x1, wts[:EXPERT_TOP_K].T, _sc_gather_rows(out_buf, dest_flat))


def kernel(x, positions, norm_mix, norm_ffn, w_in, q_gain, k_gain, cmp_k_pos, cmp_k_w1, cmp_k_w2, cmp_v_pos,
           cmp_v_w1, cmp_v_w2, s5_a_re, s5_a_im, s5_log_dt, s5_b_re, s5_b_im, s5_c_re, s5_c_im, s5_d, s5_glu,
           proj_nsa, proj_s5, w_out, router_group, router_group_bias, router_expert, router_expert_bias,
           expert_gate, expert_up, expert_down):
    batch, seq, _ = x.shape
    depth = w_in.shape[0]
    n = batch * seq
    inv_freq = ROPE_THETA ** (-jnp.arange(0, ROPE_DIMS, 2, dtype=jnp.float32) / ROPE_DIMS)
    ctab, stab = _rope_tables(positions.reshape(n, 1), inv_freq)
    ccmp, scmp = _cmp_rope_tables(ctab, stab, batch, seq)
    x2 = x.reshape(n, D_MODEL)
    for l in range(depth):
        w1, w_ga, w_gs = _pack_w_in(w_in[l])
        o_a, u_tm = _nsa_branch(x2, norm_mix[l][None], w1, ctab, stab, ccmp, scmp, q_gain[l], k_gain[l],
                                (cmp_k_pos[l], cmp_k_w1[l], cmp_k_w2[l]),
                                (cmp_v_pos[l], cmp_v_w1[l], cmp_v_w2[l]), batch, seq)
        lre, lim, w_b, w_c = _s5_weights(s5_a_re[l], s5_a_im[l], s5_log_dt[l], s5_b_re[l], s5_b_im[l],
                                         s5_c_re[l], s5_c_im[l])
        ob_tm = _s5(u_tm, lre, lim, w_b, w_c, s5_d[l][None],
                    s5_glu[l].astype(BF16), batch, seq)
        wrh, wrl, rb = _router_weights(router_group[l], router_group_bias[l], router_expert[l],
                                       router_expert_bias[l])
        x1, h2, logits_t = _merge(x2, norm_mix[l][None], w_ga, w_gs, o_a, ob_tm, proj_nsa[l].astype(BF16),
                                  proj_s5[l].astype(BF16), w_out[l].astype(BF16), norm_ffn[l][None],
                                  wrh, wrl, rb, batch, seq)
        x2 = _moe(x1, h2, logits_t, expert_gate, expert_up, expert_down, l)
    return x2.reshape(batch, seq, D_MODEL)


def _attn_tables(seq):
    n_cmp = (seq - CMP_BLOCK) // CMP_STRIDE + 1
    n_slc = seq // SLC_BLOCK
    key_blk = np.arange(seq) // SLC_BLOCK
    expand_t = (key_blk[:, None] == np.arange(LANES)[None, :]).astype(np.float32)
    cmp_start = np.arange(LANES) * CMP_STRIDE
    slc_start = np.arange(LANES) * SLC_BLOCK
    overlap_t = ((cmp_start[None, :] < slc_start[:, None] + SLC_BLOCK)
                 & (cmp_start[None, :] + CMP_BLOCK > slc_start[:, None])
                 & (np.arange(LANES)[None, :] < n_cmp) & (np.arange(LANES)[:, None] < n_slc))
    return (jnp.asarray(expand_t, BF16), jnp.asarray(overlap_t.astype(np.float32), BF16), n_cmp, n_slc)


def _attn_mask_tables():
    tq, tk, wk = ATT_TQ, ATT_TK, ATT_WK
    ql = np.arange(tq)
    onehot = np.tile(np.eye(tq, dtype=np.float32), (NSA_HEADS, 1))
    k = np.arange(wk)[:, None]
    band = []
    for c in range(WINDOW // tq + 1):
        if c < WINDOW // tq:
            keep = k <= c * tq + ql[None, :]
        else:
            keep = (k > ql[None, :]) & (k <= ql[None, :] + WINDOW)
        band.append(np.where(keep, 0.0, NEG_BIG))
    kl = np.arange(tk)[None, :]
    causal = [np.where(kl <= c * tq + ql[:, None], 0.0, NEG_BIG) for c in range(tk // tq)]
    return (jnp.asarray(onehot, BF16), jnp.asarray(np.stack(band), BF16),
            jnp.asarray(np.stack(causal), F32))


def _attention(q, ks, vs, kw, vw, kc, vc, gates, batch, seq):
    expand_t, overlap_t, n_cmp, n_slc = _attn_tables(seq)
    onehot, band, causal = _attn_mask_tables()
    tq = ATT_TQ
    nq = seq // tq
    rows = NSA_HEADS * tq
    tile = lambda w: pl.BlockSpec((tq, w), lambda b, i: (b * nq + i, 0))
    per_b = lambda r: pl.BlockSpec((r, LANES), lambda b, i: (b, 0))
    f32_rows = pltpu.VMEM((rows, LANES), F32)
    return pl.pallas_call(
        functools.partial(_attn_kernel, n_cmp=n_cmp, n_slc=n_slc, top_k=min(SLC_TOPK, n_slc)),
        grid=(batch, nq),
        in_specs=[tile(2 * NSA_WIDTH), per_b(seq), per_b(seq), per_b(seq), per_b(seq),
                  per_b(LANES), per_b(LANES), tile(LANES),
                  pl.BlockSpec(expand_t.shape, lambda b, i: (0, 0)),
                  pl.BlockSpec(overlap_t.shape, lambda b, i: (0, 0)),
                  pl.BlockSpec(onehot.shape, lambda b, i: (0, 0)),
                  pl.BlockSpec(band.shape, lambda b, i: (0, 0, 0)),
                  pl.BlockSpec(causal.shape, lambda b, i: (0, 0, 0))],
        out_specs=tile(NSA_WIDTH),
        out_shape=jax.ShapeDtypeStruct((batch * seq, NSA_WIDTH), BF16),
        scratch_shapes=[pltpu.VMEM((rows, 2 * LANES), BF16),
                        pltpu.VMEM((rows, ATT_TK), F32), pltpu.VMEM((rows, ATT_TK), BF16),
                        pltpu.VMEM((rows, ATT_WK), F32), pltpu.VMEM((rows, ATT_WK), BF16),
                        f32_rows, f32_rows, f32_rows, f32_rows, f32_rows],
        compiler_params=_cparams("parallel", "parallel"),
        name="nsa_attention",
    )(q, ks, vs, kw, vw, kc, vc, gates, expand_t, overlap_t, onehot, band, causal)
```

```python
import functools
import math

import jax
import jax.numpy as jnp
import numpy as np
from jax import lax
from jax.experimental import pallas as pl
from jax.experimental.pallas import tpu as pltpu
from jax.experimental.pallas import tpu_sc as plsc

F32 = jnp.float32
BF16 = jnp.bfloat16

D_MODEL = 1024
NSA_HEADS = 8
NSA_KV_HEADS = 2
NSA_GROUP = NSA_HEADS // NSA_KV_HEADS
HEAD_DIM = 64
ROPE_DIMS = HEAD_DIM // 4
ROPE_THETA = 500000.0
CMP_BLOCK = 32
CMP_STRIDE = 16
CMP_HIDDEN = 2 * HEAD_DIM
SLC_BLOCK = 64
SLC_TOPK = 16
WINDOW = 512
NSA_WIDTH = NSA_HEADS * HEAD_DIM
KV_WIDTH = NSA_KV_HEADS * HEAD_DIM
S5_WIDTH = 256
S5_GROUP_CH = 16
S5_GROUPS = S5_WIDTH // S5_GROUP_CH
S5_STATE = 64
S5_NSTATE = S5_GROUPS * S5_STATE
N_EXPERT_GROUPS = 4
EXPERTS_PER_GROUP = 8
N_EXPERTS = N_EXPERT_GROUPS * EXPERTS_PER_GROUP
EXPERT_TOP_K = 2
EXPERT_FF = 512
RMS_EPS = 1e-6
NEG_BIG = -1e30

LANES = 128
SUBLANES = 8
VMEM_LIMIT = 56 * 1024 * 1024

IN_TM = 512
ATT_TQ = 128
ATT_TK = 512
ATT_WK = WINDOW + ATT_TQ
ATT_RB = 128
S5_T = 128
MRG_TM = 512
RT_TT = 512
MOE_BLK = 512
DSP_TT = 1024

C_Q, C_KS, C_KW, C_KC, C_VC, C_VS, C_VW, C_U, C_G = 0, 512, 640, 768, 896, 1024, 1152, 1280, 1536
IN_W = 1664


def _cparams(*sem):
    return pltpu.CompilerParams(dimension_semantics=sem, vmem_limit_bytes=VMEM_LIMIT)


def _gelu(x):
    return 0.5 * x * (1.0 + jnp.tanh(math.sqrt(2.0 / math.pi) * (x + 0.044715 * (x * x * x))))


def _sigmoid(x):
    return 1.0 / (1.0 + jnp.exp(-x))


def _dot(a, b):
    return jnp.dot(a, b, preferred_element_type=F32)


def _dot_t(a, b):
    return lax.dot_general(a, b, (((1,), (1,)), ((), ())), preferred_element_type=F32)


def _split_bf16(x):
    hi = x.astype(BF16)
    lo = (x - hi.astype(F32)).astype(BF16)
    return hi, lo


def _pack_rows(x):
    k = x.shape[1] // 2
    hi = lax.bitcast_convert_type(x[:, :k].astype(jnp.bfloat16).astype(F32), jnp.int32)
    lo = lax.bitcast_convert_type(x[:, k:].astype(jnp.bfloat16).astype(F32), jnp.int32)
    return hi | lax.shift_right_logical(lo, 16)


def _unpack_rows(w):
    a = lax.bitcast_convert_type(w & jnp.int32(-65536), F32)
    b = lax.bitcast_convert_type(lax.shift_left(w, 16), F32)
    return a, b


def _rope_kernel(pos_ref, freq_ref, sign_ref, c_ref, s_ref):
    ang = pos_ref[...].astype(F32) * freq_ref[...]
    c_ref[...] = jnp.cos(ang)
    s_ref[...] = jnp.sin(ang) * sign_ref[...]


def _rope_tables(pos_col, inv_freq):
    n = pos_col.shape[0]
    lane = np.arange(LANES) % HEAD_DIM
    sel = np.zeros((LANES, ROPE_DIMS // 2), np.float32)
    for l in range(LANES):
        if lane[l] < ROPE_DIMS:
            sel[l, lane[l] % (ROPE_DIMS // 2)] = 1.0
    freq = (jnp.asarray(sel) * inv_freq[None, :]).sum(-1)[None, :]
    sign = np.where(lane < ROPE_DIMS // 2, -1.0, 1.0).astype(np.float32)[None, :]
    tm = 1024
    return pl.pallas_call(
        _rope_kernel,
        grid=(n // tm,),
        in_specs=[pl.BlockSpec((tm, 1), lambda i: (i, 0)),
                  pl.BlockSpec((1, LANES), lambda i: (0, 0)),
                  pl.BlockSpec((1, LANES), lambda i: (0, 0))],
        out_specs=[pl.BlockSpec((tm, LANES), lambda i: (i, 0))] * 2,
        out_shape=[jax.ShapeDtypeStruct((n, LANES), F32)] * 2,
        compiler_params=_cparams("parallel"),
        name="rope_tables",
    )(pos_col, freq, jnp.asarray(sign))


def _headnorm_rope(t, gain, c, s):
    lane = lax.broadcasted_iota(jnp.int32, t.shape, 1)
    lo = lane < HEAD_DIM
    sq = t * t
    s_lo = jnp.sum(jnp.where(lo, sq, 0.0), axis=-1, keepdims=True)
    s_hi = jnp.sum(jnp.where(lo, 0.0, sq), axis=-1, keepdims=True)
    ms = jnp.where(lo, s_lo, s_hi) * (1.0 / HEAD_DIM)
    tn = (t * lax.rsqrt(ms + RMS_EPS)) * gain
    half = ROPE_DIMS // 2
    partner = jnp.where((lane & (HEAD_DIM - 1)) < half,
                        pltpu.roll(tn, LANES - half, 1), pltpu.roll(tn, half, 1))
    return tn * c + partner * s


def _in_proj_kernel(x_ref, gain_ref, w_ref, c_ref, s_ref, qg_ref, kg_ref,
                    q_ref, ks_ref, kw_ref, kc_ref, vc_ref, vs_ref, vw_ref, u_ref, g_ref):
    x = x_ref[...]
    ms = jnp.mean(x * x, axis=-1, keepdims=True)
    h = (x * lax.rsqrt(ms + RMS_EPS)) * gain_ref[...]
    z = _dot(h.astype(BF16), w_ref[...])
    c = c_ref[...]
    s = s_ref[...]
    lane = lax.broadcasted_iota(jnp.int32, c.shape, 1)
    lo = lane < HEAD_DIM
    scale = HEAD_DIM ** -0.5 * math.log2(math.e)
    for j in range(NSA_HEADS // 2):
        r = _headnorm_rope(z[:, C_Q + LANES * j:C_Q + LANES * (j + 1)], qg_ref[...], c, s) * scale
        rr = pltpu.roll(r, HEAD_DIM, 1)
        if j // (NSA_GROUP // 2) == 0:
            even, odd = jnp.where(lo, r, 0.0), jnp.where(lo, rr, 0.0)
        else:
            even, odd = jnp.where(lo, 0.0, rr), jnp.where(lo, 0.0, r)
        q_ref[:, 2 * LANES * j:2 * LANES * j + LANES] = even.astype(BF16)
        q_ref[:, 2 * LANES * j + LANES:2 * LANES * (j + 1)] = odd.astype(BF16)
    ks_ref[...] = _headnorm_rope(z[:, C_KS:C_KS + LANES], kg_ref[...], c, s).astype(BF16)
    kw_ref[...] = _headnorm_rope(z[:, C_KW:C_KW + LANES], kg_ref[...], c, s).astype(BF16)
    kc_ref[...] = z[:, C_KC:C_KC + LANES].astype(BF16)
    vc_ref[...] = z[:, C_VC:C_VC + LANES].astype(BF16)
    vs_ref[...] = z[:, C_VS:C_VS + LANES].astype(BF16)
    vw_ref[...] = z[:, C_VW:C_VW + LANES].astype(BF16)
    u_ref[:, pl.ds(pl.program_id(1), 1), :] = z[:, C_U:C_U + S5_WIDTH][:, None, :]
    g_ref[...] = _sigmoid(z[:, C_G:C_G + LANES])


def _in_proj(x2, gain, w1, ctab, stab, qg, kg, batch, seq):
    n = x2.shape[0]
    tm = IN_TM
    nt = seq // tm
    row = lambda i, b: (b * nt + i, 0)
    const = lambda i, b: (0, 0)
    tok = lambda w: pl.BlockSpec((tm, w), row)
    outs = [(n, 2 * NSA_WIDTH, BF16)] + [(n, LANES, BF16)] * 6
    return pl.pallas_call(
        _in_proj_kernel,
        grid=(nt, batch),
        in_specs=[tok(D_MODEL), pl.BlockSpec((1, D_MODEL), const), pl.BlockSpec((D_MODEL, IN_W), const),
                  tok(LANES), tok(LANES), pl.BlockSpec((1, LANES), const), pl.BlockSpec((1, LANES), const)],
        out_specs=[tok(2 * NSA_WIDTH)] + [tok(LANES)] * 6
        + [pl.BlockSpec((tm, batch, S5_WIDTH), lambda i, b: (i, 0, 0)), tok(LANES)],
        out_shape=[jax.ShapeDtypeStruct(s[:2], s[2]) for s in outs]
        + [jax.ShapeDtypeStruct((seq, batch, S5_WIDTH), F32), jax.ShapeDtypeStruct((n, LANES), F32)],
        compiler_params=_cparams("parallel", "arbitrary"),
        name="in_proj",
    )(x2, gain, w1, ctab, stab, qg, kg)


def _compress_mlp(h_ref, w_ref, p_ref, w2_ref):
    w = w_ref[...]
    pre = _dot(h_ref[...], w)
    phi, plo = _split_bf16(p_ref[...])
    pb = _dot(phi, w) + _dot(plo, w)
    rows = pre.shape[0]
    out = None
    for h in range(NSA_KV_HEADS):
        top = pre[:, LANES * h:LANES * (h + 1)]
        bot = pre[:, LANES * (2 + h):LANES * (3 + h)]
        bias = pb[0:1, LANES * h:LANES * (h + 1)] + pb[1:2, LANES * (2 + h):LANES * (3 + h)]
        a = _gelu(top + pltpu.roll(bot, rows - 1, 0) + bias)
        o = _dot(a.astype(BF16), w2_ref[h])
        out = o if out is None else out + o
    return out


def _compress_kernel(hk_ref, hv_ref, wk_ref, wv_ref, pk_ref, pv_ref, w2k_ref, w2v_ref,
                     kg_ref, c_ref, s_ref, kc_ref, vc_ref):
    k = _compress_mlp(hk_ref, wk_ref, pk_ref, w2k_ref)
    kc_ref[...] = _headnorm_rope(k, kg_ref[...], c_ref[...], s_ref[...]).astype(BF16)
    vc_ref[...] = _compress_mlp(hv_ref, wv_ref, pv_ref, w2v_ref).astype(BF16)


def _compress(hk, hv, wk, wv, pk, pv, w2k, w2v, kg, ccmp, scmp, batch):
    rows = hk.shape[0] // batch
    width = hk.shape[1]
    const2 = lambda b: (0, 0)
    const3 = lambda b: (0, 0, 0)
    rowb = lambda w: pl.BlockSpec((rows, w), lambda b: (b, 0))
    return pl.pallas_call(
        _compress_kernel,
        grid=(batch,),
        in_specs=[rowb(width), rowb(width),
                  pl.BlockSpec(wk.shape, const2), pl.BlockSpec(wv.shape, const2),
                  pl.BlockSpec(pk.shape, const2), pl.BlockSpec(pv.shape, const2),
                  pl.BlockSpec(w2k.shape, const3), pl.BlockSpec(w2v.shape, const3),
                  pl.BlockSpec((1, LANES), const2), rowb(LANES), rowb(LANES)],
        out_specs=[rowb(LANES), rowb(LANES)],
        out_shape=[jax.ShapeDtypeStruct((hk.shape[0], LANES), BF16)] * 2,
        compiler_params=_cparams("parallel"),
        name="compress_kv",
    )(hk, hv, wk, wv, pk, pv, w2k, w2v, kg, ccmp, scmp)


def _compress_weights(pos_emb, w1, w2):
    half = CMP_BLOCK // 2
    eye = jnp.eye(NSA_KV_HEADS, dtype=F32)
    w1r = w1.reshape(2, half, HEAD_DIM, CMP_HIDDEN)
    wexp = jnp.einsum("sldn,ph->lpdshn", w1r, eye)
    wexp = wexp.reshape(half * KV_WIDTH, 2 * NSA_KV_HEADS * CMP_HIDDEN).astype(BF16)
    pexp = jnp.broadcast_to(pos_emb.reshape(2, half, 1, HEAD_DIM), (2, half, NSA_KV_HEADS, HEAD_DIM))
    pexp = jnp.pad(pexp.reshape(2, half * KV_WIDTH), ((0, SUBLANES - 2), (0, 0)))
    w2e = jnp.einsum("cd,ph->pchd", w2, eye).reshape(NSA_KV_HEADS, CMP_HIDDEN, KV_WIDTH)
    return wexp, pexp, w2e.astype(BF16)


def _attn_kernel(q_ref, ks_ref, vs_ref, kw_ref, vw_ref, kc_ref, vc_ref, g_ref, et_ref, ovt_ref, oh_ref, wb_ref,
                 cb_ref, o_ref, qc_ref, s_ref, p_ref, sw_ref, pw_ref, m_ref, al_ref, oc_ref, as_ref, aw_ref,
                 *, n_cmp, n_slc, top_k):
    tq, tk, wk, rb, nh = ATT_TQ, ATT_TK, ATT_WK, ATT_RB, NSA_HEADS
    rows, grp_rows = nh * tq, NSA_GROUP * tq
    qk_rows = 2 * tq
    q0 = pl.program_id(1) * tq
    for a in range(nh):
        qc_ref[a * tq:(a + 1) * tq, 0:LANES] = q_ref[:, LANES * a:LANES * (a + 1)]
    q8 = qc_ref[:, 0:LANES]
    lane = lax.broadcasted_iota(jnp.int32, (tq, LANES), 1)
    t_row = q0 + lax.broadcasted_iota(jnp.int32, (tq, LANES), 0)

    s_c = _dot_t(q8, kc_ref[...]).reshape(nh, tq, LANES)
    mask_c = ((lane * CMP_STRIDE + (CMP_BLOCK - 1)) <= t_row) & (lane < n_cmp)
    sm = jnp.where(mask_c[None], s_c, NEG_BIG)
    p = jnp.where(mask_c[None], jnp.exp2(sm - jnp.max(sm, axis=-1, keepdims=True)), 0.0)
    l = jnp.sum(p, axis=-1, keepdims=True)
    p_c = p * jnp.where(l > 0.0, 1.0 / l, 0.0)
    oc_ref[...] = _dot(p_c.reshape(rows, LANES).astype(BF16), vc_ref[...])

    blk = lax.broadcasted_iota(jnp.int32, (n_slc, tq), 0)
    cur = (q0 + lax.broadcasted_iota(jnp.int32, (n_slc, tq), 1)) >> int(math.log2(SLC_BLOCK))
    forced = (blk == 0) | (blk == cur) | (blk == cur - 1)
    for h in range(NSA_KV_HEADS):
        ps = p_c[NSA_GROUP * h]
        for g in range(1, NSA_GROUP):
            ps = ps + p_c[NSA_GROUP * h + g]
        hi, lo = _split_bf16(ps)
        imp = (_dot_t(ovt_ref[...], hi) + _dot_t(ovt_ref[...], lo))[0:n_slc]
        imp = jnp.where(forced, jnp.inf, jnp.where(blk > cur, -jnp.inf, imp))
        rank = jnp.zeros((n_slc, tq), F32)
        for i in range(n_slc):
            row = imp[i:i + 1, :]
            rank = rank + jnp.where((row > imp) | ((row == imp) & (blk > i)), 1.0, 0.0)
        bias_t = jnp.where(rank < float(top_k), 0.0, NEG_BIG)
        bias = jnp.concatenate([bias_t, jnp.zeros((LANES - n_slc, tq), F32)], axis=0).T.astype(BF16)
        for g in range(NSA_GROUP):
            a = NSA_GROUP * h + g
            qc_ref[a * tq:(a + 1) * tq, LANES:2 * LANES] = bias

    def softmax_pass(s_buf, p_buf, width, causal_case, online):
        for r in range(rows // rb):
            r0 = r * rb
            s = s_buf[pl.ds(r0, rb), 0:width]
            if causal_case is not None:
                s = s + cb_ref[causal_case, pl.ds(r0 % tq, rb), :]
            m_new = jnp.broadcast_to(jnp.max(s, axis=1, keepdims=True), (rb, LANES))
            if online:
                m_old = m_ref[pl.ds(r0, rb), :]
                m_new = jnp.maximum(m_old, m_new)
                al_ref[pl.ds(r0, rb), :] = jnp.exp2(m_old - m_new)
                m_ref[pl.ds(r0, rb), :] = m_new
            pp = jnp.exp2(s - jnp.concatenate([m_new] * (width // LANES), axis=1))
            p_buf[pl.ds(r0, rb), 0:width] = pp.astype(BF16)

    def pv(acc_ref, p_buf, v, width, online):
        lane_v = lax.broadcasted_iota(jnp.int32, v.shape, 1)
        for h in range(NSA_KV_HEADS):
            rs = slice(h * grp_rows, (h + 1) * grp_rows)
            own = (lane_v < HEAD_DIM) if h == 0 else (lane_v >= HEAD_DIM)
            upd = _dot(p_buf[rs, 0:width], jnp.where(own, v, jnp.ones_like(v)))
            acc_ref[rs, :] = (al_ref[rs, :] * acc_ref[rs, :] + upd) if online else upd

    m_ref[...] = jnp.full(m_ref.shape, 0.1 * NEG_BIG, F32)
    as_ref[...] = jnp.zeros(as_ref.shape, F32)

    def slc_chunk(c, diagonal):
        k0 = pl.multiple_of(c * tk, tk)
        kcat = jnp.concatenate([ks_ref[pl.ds(k0, tk), :], et_ref[pl.ds(k0, tk), :]], axis=1)
        for g in range(rows // qk_rows):
            rs = slice(g * qk_rows, (g + 1) * qk_rows)
            s_ref[rs, :] = _dot_t(qc_ref[rs, :], kcat)
        softmax_pass(s_ref, p_ref, tk, ((q0 - k0) >> int(math.log2(tq))) if diagonal else None, True)
        pv(as_ref, p_ref, vs_ref[pl.ds(k0, tk), :], tk, True)

    c_last = q0 >> int(math.log2(tk))

    def full_chunk(c, carry):
        slc_chunk(c, False)
        return carry

    lax.fori_loop(0, c_last, full_chunk, 0)
    slc_chunk(c_last, True)

    w0 = pl.multiple_of(jnp.maximum(q0 - WINDOW, 0), LANES)
    w_case = jnp.minimum(q0 >> int(math.log2(tq)), WINDOW // tq)
    kcat_w = jnp.concatenate([kw_ref[pl.ds(w0, wk), :], wb_ref[w_case]], axis=1)
    for g in range(rows // qk_rows):
        rs = slice(g * qk_rows, (g + 1) * qk_rows)
        sw_ref[rs, :] = _dot_t(jnp.concatenate([qc_ref[rs, 0:LANES], oh_ref[rs, :]], axis=1), kcat_w)
    softmax_pass(sw_ref, pw_ref, wk, None, False)
    pv(aw_ref, pw_ref, vw_ref[pl.ds(w0, wk), :], wk, False)

    lo_half = lane < HEAD_DIM
    for j in range(nh // 2):
        comb = []
        for a in (2 * j, 2 * j + 1):
            rs = slice(a * tq, (a + 1) * tq)
            o_s, o_w = as_ref[rs, :], aw_ref[rs, :]
            comb.append(g_ref[:, 3 * a:3 * a + 1] * oc_ref[rs, :]
                        + g_ref[:, 3 * a + 1:3 * a + 2] * (o_s * (1.0 / pltpu.roll(o_s, HEAD_DIM, 1)))
                        + g_ref[:, 3 * a + 2:3 * a + 3] * (o_w * (1.0 / pltpu.roll(o_w, HEAD_DIM, 1))))
        even, odd = comb
        if j // (NSA_GROUP // 2) == 0:
            pair = jnp.where(lo_half, even, pltpu.roll(odd, HEAD_DIM, 1))
        else:
            pair = jnp.where(lo_half, pltpu.roll(even, HEAD_DIM, 1), odd)
        o_ref[:, LANES * j:LANES * (j + 1)] = pair.astype(BF16)


def _pack_w_in(w):
    sizes = (NSA_WIDTH,) + (KV_WIDTH,) * 6 + (3 * NSA_HEADS, S5_WIDTH, D_MODEL, D_MODEL)
    o = np.cumsum((0,) + sizes)
    q, kc, vc, ks, vs, kw, vw, g, u, ga, gs = [w[:, o[i]:o[i + 1]] for i in range(len(sizes))]
    g = jnp.pad(g, ((0, 0), (0, LANES - 3 * NSA_HEADS)))
    w1 = jnp.concatenate([q, ks, kw, kc, vc, vs, vw, u, g], axis=1).astype(BF16)
    return w1, ga.astype(BF16), gs.astype(BF16)


def _nsa_branch(x2, gain, w1, ctab, stab, ccmp, scmp, q_gain, k_gain, cmp_k, cmp_v, batch, seq):
    qg = jnp.tile(q_gain, 2)[None, :]
    kg = jnp.tile(k_gain, 2)[None, :]
    q, ks, kw, kc_raw, vc_raw, vs, vw, u_tm, gates = _in_proj(x2, gain, w1, ctab, stab, qg, kg, batch, seq)
    half_rows = batch * seq // (CMP_BLOCK // 2)
    hk = kc_raw.reshape(half_rows, (CMP_BLOCK // 2) * KV_WIDTH)
    hv = vc_raw.reshape(half_rows, (CMP_BLOCK // 2) * KV_WIDTH)
    wk, pk, w2k = _compress_weights(*cmp_k)
    wv, pv, w2v = _compress_weights(*cmp_v)
    kc, vc = _compress(hk, hv, wk, wv, pk, pv, w2k, w2v, kg, ccmp, scmp, batch)
    o_a = _attention(q, ks, vs, kw, vw, kc, vc, gates, batch, seq)
    return o_a, u_tm


def _cmp_rope_tables(ctab, stab, batch, seq):
    def pick(t):
        t = t.reshape(batch, seq, LANES)[:, CMP_BLOCK - 1::CMP_STRIDE]
        t = jnp.pad(t, ((0, 0), (0, seq // CMP_STRIDE - t.shape[1]), (0, 0)))
        return t.reshape(batch * (seq // CMP_STRIDE), LANES)
    return pick(ctab), pick(stab)


def _s5_disc_kernel(are_ref, aim_ref, ldt_ref, lre_ref, lim_ref, fre_ref, fim_ref):
    lam_re = jnp.minimum(are_ref[...], -1e-4)
    lam_im = aim_ref[...]
    dt = jnp.exp(ldt_ref[...])
    mag = jnp.exp(lam_re * dt)
    ang = lam_im * dt
    lb_re = mag * jnp.cos(ang)
    lb_im = mag * jnp.sin(ang)
    den = lam_re * lam_re + lam_im * lam_im
    lre_ref[...] = lb_re
    lim_ref[...] = lb_im
    fre_ref[...] = ((lb_re - 1.0) * lam_re + lb_im * lam_im) / den
    fim_ref[...] = (lb_im * lam_re - (lb_re - 1.0) * lam_im) / den


def _s5_weights(a_re, a_im, log_dt, b_re, b_im, c_re, c_im):
    shp = jax.ShapeDtypeStruct((S5_GROUPS, S5_STATE), F32)
    lb_re, lb_im, f_re, f_im = pl.pallas_call(
        _s5_disc_kernel, out_shape=[shp] * 4, name="s5_discretise",
    )(a_re, a_im, log_dt[:, None])
    bb_re = f_re[..., None] * b_re - f_im[..., None] * b_im
    bb_im = f_re[..., None] * b_im + f_im[..., None] * b_re
    eye = jnp.eye(S5_GROUPS, dtype=F32)

    def in_map(bb):
        return jnp.einsum("gpc,gh->gchp", bb, eye).reshape(S5_WIDTH, S5_NSTATE)

    def out_map(cc):
        return jnp.einsum("gcp,gh->hpgc", cc, eye).reshape(S5_NSTATE, S5_WIDTH)

    w_b = jnp.concatenate([in_map(bb_re), in_map(bb_im)], axis=1).astype(BF16)
    w_c = jnp.concatenate([out_map(c_re), -out_map(c_im)], axis=0).astype(BF16)
    return lb_re.reshape(1, S5_NSTATE), lb_im.reshape(1, S5_NSTATE), w_b, w_c


def _s5_kernel(u_ref, wb_ref, lre_ref, lim_ref, wc_ref, d_ref, wg_ref, o_ref, x_ref, st_ref, *, batch):
    ns = S5_NSTATE

    @pl.when(pl.program_id(0) == 0)
    def _():
        st_ref[...] = jnp.zeros(st_ref.shape, F32)

    u = u_ref[...].reshape(u_ref.shape[0] * batch, S5_WIDTH)
    x_ref[...] = _dot(u.astype(BF16), wb_ref[...])
    lre = jnp.broadcast_to(lre_ref[...], (batch, ns))
    lim = jnp.broadcast_to(lim_ref[...], (batch, ns))

    def step(t, carry):
        xr, xi = carry
        r0 = pl.multiple_of(t * batch, batch)
        nr = lre * xr - lim * xi + x_ref[pl.ds(r0, batch), 0:ns]
        ni = lre * xi + lim * xr + x_ref[pl.ds(r0, batch), ns:2 * ns]
        x_ref[pl.ds(r0, batch), 0:ns] = nr
        x_ref[pl.ds(r0, batch), ns:2 * ns] = ni
        return nr, ni

    steps = u.shape[0] // batch
    xr, xi = lax.fori_loop(0, steps, step, (st_ref[:, 0:ns], st_ref[:, ns:2 * ns]), unroll=4)
    st_ref[:, 0:ns] = xr
    st_ref[:, ns:2 * ns] = xi
    y = _dot(x_ref[...].astype(BF16), wc_ref[...]) + d_ref[...] * u
    z = _dot(_gelu(y).astype(BF16), wg_ref[...])
    o = z[:, :S5_WIDTH] * _sigmoid(z[:, S5_WIDTH:])
    for c in range(S5_WIDTH // LANES):
        o_ref[c] = o[:, c * LANES:(c + 1) * LANES]


def _s5(u_tm, lre, lim, w_b, w_c, d_skip, w_glu, batch, seq):
    rows = S5_T * batch
    const = lambda t: (0, 0)
    return pl.pallas_call(
        functools.partial(_s5_kernel, batch=batch),
        grid=(seq // S5_T,),
        in_specs=[pl.BlockSpec((S5_T, batch, S5_WIDTH), lambda t: (t, 0, 0)),
                  pl.BlockSpec(w_b.shape, const), pl.BlockSpec(lre.shape, const), pl.BlockSpec(lim.shape, const),
                  pl.BlockSpec(w_c.shape, const), pl.BlockSpec((1, S5_WIDTH), const),
                  pl.BlockSpec(w_glu.shape, const)],
        out_specs=pl.BlockSpec((S5_WIDTH // LANES, rows, LANES), lambda t: (0, t, 0)),
        out_shape=jax.ShapeDtypeStruct((S5_WIDTH // LANES, seq * batch, LANES), F32),
        scratch_shapes=[pltpu.VMEM((rows, 2 * S5_NSTATE), F32), pltpu.VMEM((batch, 2 * S5_NSTATE), F32)],
        compiler_params=_cparams("arbitrary"),
        name="s5_scan",
    )(u_tm, w_b, lre, lim, w_c, d_skip, w_glu)


RT_GROUP_ROW = 0
RT_EXPERT_ROW = SUBLANES


def _rmsnorm(x, gain):
    return (x * lax.rsqrt(jnp.mean(x * x, axis=-1, keepdims=True) + RMS_EPS)) * gain


def _merge_kernel(x_ref, g1_ref, wga_ref, wgs_ref, oa_ref, ob_ref, pa_ref, ps_ref, wo_ref, g2_ref,
                  wrh_ref, wrl_ref, rb_ref, x1_ref, h2_ref, lg_ref):
    x = x_ref[...]
    h = _rmsnorm(x, g1_ref[...]).astype(BF16)
    gate_a = _sigmoid(_dot(h, wga_ref[...]))
    gate_s = _sigmoid(_dot(h, wgs_ref[...]))
    rows_b = pl.ds(pl.program_id(1), x.shape[0], stride=ob_ref.shape[1] // x.shape[0])
    o_b = jnp.concatenate([ob_ref[c, rows_b, :] for c in range(S5_WIDTH // LANES)], axis=1)
    merged = gate_a * _dot(oa_ref[...], pa_ref[...]) + gate_s * _dot(o_b.astype(BF16), ps_ref[...])
    x1 = x + _dot(merged.astype(BF16), wo_ref[...])
    x1_ref[...] = x1
    h2 = _rmsnorm(x1, g2_ref[...])
    h2_ref[...] = _pack_rows(h2)
    hi, lo = _split_bf16(h2)
    lg_ref[...] = (_dot_t(wrh_ref[...], hi) + _dot_t(wrh_ref[...], lo) + _dot_t(wrl_ref[...], hi)
                   + rb_ref[...])


def _router_weights(w_rg, b_rg, w_re, b_re):
    def rows(group_part, expert_part):
        return jnp.concatenate([
            group_part, jnp.zeros((RT_EXPERT_ROW - N_EXPERT_GROUPS,) + group_part.shape[1:], F32),
            expert_part, jnp.zeros((LANES - RT_EXPERT_ROW - N_EXPERTS,) + group_part.shape[1:], F32)], axis=0)

    hi, lo = _split_bf16(rows(w_rg.T, w_re.T))
    return hi, lo, rows(b_rg[:, None], b_re[:, None])


def _merge(x2, g1, w_ga, w_gs, o_a, ob_tm, p_a, p_s, w_o, g2, wrh, wrl, rb, batch, seq):
    n = x2.shape[0]
    tm = MRG_TM
    nt = seq // tm
    row = lambda i, b: (b * nt + i, 0)
    const = lambda i, b: (0, 0)
    tok = lambda w: pl.BlockSpec((tm, w), row)
    full = lambda a: pl.BlockSpec(a.shape, const)
    return pl.pallas_call(
        _merge_kernel,
        grid=(nt, batch),
        in_specs=[tok(D_MODEL), full(g1), full(w_ga), full(w_gs), tok(NSA_WIDTH),
                  pl.BlockSpec((S5_WIDTH // LANES, tm * batch, LANES), lambda i, b: (0, i, 0)),
                  full(p_a), full(p_s), full(w_o),
                  full(g2), full(wrh), full(wrl), full(rb)],
        out_specs=[tok(D_MODEL), tok(D_MODEL // 2), pl.BlockSpec((LANES, tm), lambda i, b: (0, b * nt + i))],
        out_shape=[jax.ShapeDtypeStruct((n, D_MODEL), F32), jax.ShapeDtypeStruct((n, D_MODEL // 2), jnp.int32),
                   jax.ShapeDtypeStruct((LANES, n), F32)],
        compiler_params=_cparams("parallel", "parallel"),
        name="merge_out_proj",
    )(x2, g1, w_ga, w_gs, o_a, ob_tm, p_a, p_s, w_o, g2, wrh, wrl, rb)


def _first_argmax(v, rows):
    vmax = jnp.max(v, axis=0, keepdims=True)
    idx = jnp.min(jnp.where(v == vmax, rows, SUBLANES), axis=0, keepdims=True)
    return vmax, idx


def _route_kernel(lg_ref, tri_ref, e_ref, w_ref, r_ref, cnt_ref, carry_ref):
    tt = lg_ref.shape[1]

    @pl.when(pl.program_id(0) == 0)
    def _():
        carry_ref[...] = jnp.zeros(carry_ref.shape, F32)

    rows = lax.broadcasted_iota(jnp.int32, (SUBLANES, tt), 0)
    gl = jnp.where(rows < N_EXPERT_GROUPS, lg_ref[RT_GROUP_ROW:RT_GROUP_ROW + SUBLANES, :], -jnp.inf)
    gmax, g_sel = _first_argmax(gl, rows)
    g_w = 1.0 / jnp.sum(jnp.exp(gl - gmax), axis=0, keepdims=True)

    el = jnp.zeros((EXPERTS_PER_GROUP, tt), F32)
    for g in range(N_EXPERT_GROUPS):
        r0 = RT_EXPERT_ROW + EXPERTS_PER_GROUP * g
        el = jnp.where(g_sel == g, lg_ref[r0:r0 + EXPERTS_PER_GROUP, :], el)
    ex = jnp.exp(el - jnp.max(el, axis=0, keepdims=True))
    p = ex / jnp.sum(ex, axis=0, keepdims=True)
    p1, i1 = _first_argmax(p, rows)
    p2, i2 = _first_argmax(jnp.where(rows == i1, -1.0, p), rows)
    e1 = g_sel * EXPERTS_PER_GROUP + i1
    e2 = g_sel * EXPERTS_PER_GROUP + i2
    scale = g_w / (p1 + p2)

    erow = lax.broadcasted_iota(jnp.int32, (N_EXPERTS, tt), 0)
    oh1 = jnp.where(erow == e1, 1.0, 0.0)
    oh2 = jnp.where(erow == e2, 1.0, 0.0)
    both = oh1 + oh2
    before = carry_ref[:, 0:1] + _dot(both.astype(BF16), tri_ref[...])
    carry_ref[...] = carry_ref[...] + jnp.sum(both, axis=1, keepdims=True)
    cnt_ref[...] = carry_ref[...]

    zi = jnp.zeros((SUBLANES - 2, tt), jnp.int32)
    e_ref[...] = jnp.concatenate([e1, e2, zi], axis=0)
    w_ref[...] = jnp.concatenate([p1 * scale, p2 * scale, zi.astype(F32)], axis=0)
    r_ref[...] = jnp.concatenate([jnp.sum(oh1 * before, axis=0, keepdims=True),
                                  jnp.sum(oh2 * before, axis=0, keepdims=True),
                                  zi.astype(F32)], axis=0).astype(jnp.int32)


def _route(logits_t):
    n = logits_t.shape[1]
    tt = RT_TT
    tri = jnp.asarray(np.triu(np.ones((tt, tt), np.float32), 1), BF16)
    blk = pl.BlockSpec((SUBLANES, tt), lambda i: (0, i))
    return pl.pallas_call(
        _route_kernel,
        grid=(n // tt,),
        in_specs=[pl.BlockSpec((LANES, tt), lambda i: (0, i)), pl.BlockSpec((tt, tt), lambda i: (0, 0))],
        out_specs=[blk, blk, blk, pl.BlockSpec((N_EXPERTS, LANES), lambda i: (0, 0))],
        out_shape=[jax.ShapeDtypeStruct((SUBLANES, n), jnp.int32), jax.ShapeDtypeStruct((SUBLANES, n), F32),
                   jax.ShapeDtypeStruct((SUBLANES, n), jnp.int32), jax.ShapeDtypeStruct((N_EXPERTS, LANES), F32)],
        scratch_shapes=[pltpu.VMEM((N_EXPERTS, LANES), F32)],
        compiler_params=_cparams("arbitrary"),
        name="moe_route",
    )(logits_t, tri)


def _expert_kernel(be_ref, nb_ref, x_ref, wg_ref, wu_ref, wd_ref, o_ref, wgb_ref, wub_ref, wdb_ref):
    i = pl.program_id(0)
    half = D_MODEL // 2

    @pl.when(i < nb_ref[0])
    def _():
        @pl.when((i == 0) | (be_ref[i] != be_ref[jnp.maximum(i - 1, 0)]))
        def _():
            wgb_ref[...] = wg_ref[...].astype(BF16)
            wub_ref[...] = wu_ref[...].astype(BF16)
            wdb_ref[...] = wd_ref[...].astype(BF16)

        xa, xb = _unpack_rows(x_ref[...])
        xa, xb = xa.astype(BF16), xb.astype(BF16)
        g = _dot(xa, wgb_ref[0:half, :]) + _dot(xb, wgb_ref[half:, :])
        u = _dot(xa, wub_ref[0:half, :]) + _dot(xb, wub_ref[half:, :])
        o_ref[...] = _pack_rows(_dot(((g * _sigmoid(g)) * u).astype(BF16), wdb_ref[...]))

    @pl.when(i >= nb_ref[0])
    def _():
        o_ref[...] = jnp.zeros(o_ref.shape, o_ref.dtype)


def _experts(blk_expert, n_used, buf, w_gate, w_up, w_down, layer):
    cap, width = buf.shape
    blk = MOE_BLK
    row = lambda i, be, nb: (jnp.minimum(i, nb[0] - 1), 0)
    wmap = lambda i, be, nb: (layer, be[jnp.minimum(i, nb[0] - 1)], 0, 0)
    return pl.pallas_call(
        _expert_kernel,
        grid_spec=pltpu.PrefetchScalarGridSpec(
            num_scalar_prefetch=2,
            grid=(cap // blk,),
            in_specs=[pl.BlockSpec((blk, width), row),
                      pl.BlockSpec((None, None, D_MODEL, EXPERT_FF), wmap),
                      pl.BlockSpec((None, None, D_MODEL, EXPERT_FF), wmap),
                      pl.BlockSpec((None, None, EXPERT_FF, D_MODEL), wmap)],
            out_specs=pl.BlockSpec((blk, width), lambda i, be, nb: (i, 0)),
            scratch_shapes=[pltpu.VMEM((D_MODEL, EXPERT_FF), BF16), pltpu.VMEM((D_MODEL, EXPERT_FF), BF16),
                            pltpu.VMEM((EXPERT_FF, D_MODEL), BF16)],
        ),
        out_shape=jax.ShapeDtypeStruct((cap, width), buf.dtype),
        compiler_params=_cparams("arbitrary"),
        name="moe_experts",
    )(blk_expert, n_used, buf, w_gate, w_up, w_down)


SC_CH = 64


def _sc_mesh():
    return plsc.VectorSubcoreMesh(core_axis_name="c", subcore_axis_name="s")


def _sc_worker(info):
    return lax.axis_index("s") * info.num_cores + lax.axis_index("c")


def _sc_gather_rows(table, idx, rows=None):
    info = plsc.get_sparse_core_info()
    lanes, workers = info.num_lanes, info.num_cores * info.num_subcores
    invert = rows is not None
    rows = rows if invert else idx.shape[0]
    width = table.shape[1]
    per_w = rows // workers
    n_chunks = per_w // SC_CH
    assert per_w % (2 * SC_CH) == 0
    scan = 2048
    n_tok = table.shape[0]

    def body(table_hbm, idx_hbm, out_hbm, idx_v, rows_a, rows_b, sem_a, sem_b, *scan_v):
        base = _sc_worker(info) * per_w
        if invert:
            lane = lax.iota(jnp.int32, lanes)

            @pl.loop(0, per_w // lanes)
            def _(j):
                idx_v[pl.ds(j * lanes, lanes)] = (base + j * lanes + lane) & (n_tok - 1)

            @pl.loop(0, idx.shape[0] // scan)
            def _(c):
                pltpu.sync_copy(idx_hbm.at[pl.ds(c * scan, scan)], scan_v[0])

                @plsc.parallel_loop(0, scan // lanes, unroll=8)
                def _(j):
                    rel = scan_v[0][pl.ds(j * lanes, lanes)] - base
                    mine = (rel >= 0) & (rel < per_w)
                    token = (c * scan + j * lanes + lane) & (n_tok - 1)
                    plsc.store_scatter(idx_v, [jnp.where(mine, rel, 0)], token, mask=mine)
        else:
            pltpu.sync_copy(idx_hbm.at[pl.ds(base, per_w)], idx_v)

        def gather(j, buf, sem):
            return pltpu.make_async_copy(table_hbm.at[idx_v.at[pl.ds(j * SC_CH, SC_CH)]], buf, sem)

        def write(j, buf):
            pltpu.sync_copy(buf, out_hbm.at[pl.ds(base + j * SC_CH, SC_CH)])

        gather(0, rows_a, sem_a).start()

        @pl.loop(0, n_chunks, step=2)
        def _(j):
            gather(j + 1, rows_b, sem_b).start()
            gather(j, rows_a, sem_a).wait()
            write(j, rows_a)

            @pl.when(j + 2 < n_chunks)
            def _():
                gather(j + 2, rows_a, sem_a).start()

            gather(j + 1, rows_b, sem_b).wait()
            write(j + 1, rows_b)

    chunk_buf = pltpu.VMEM((SC_CH, width), table.dtype)
    scratch = [pltpu.VMEM((per_w,), jnp.int32), chunk_buf, chunk_buf,
               pltpu.SemaphoreType.DMA, pltpu.SemaphoreType.DMA]
    return pl.kernel(
        body, out_type=jax.ShapeDtypeStruct((rows, width), table.dtype), mesh=_sc_mesh(),
        scratch_types=scratch + ([pltpu.VMEM((scan,), jnp.int32)] if invert else []),
        compiler_params=pltpu.CompilerParams(needs_layout_passes=False) if invert else None,
        name="sc_dispatch_rows" if invert else "sc_gather_rows",
    )(table, idx)


def _blend_kernel(x1_ref, w_ref, g0_ref, g1_ref, o_ref):
    half = x1_ref.shape[1] // 2
    a0, b0 = _unpack_rows(g0_ref[...])
    a1, b1 = _unpack_rows(g1_ref[...])
    w0, w1 = w_ref[:, 0:1], w_ref[:, 1:2]
    o_ref[:, 0:half] = x1_ref[:, 0:half] + w0 * a0 + w1 * a1
    o_ref[:, half:] = x1_ref[:, half:] + w0 * b0 + w1 * b1


def _blend(x1, w_col, gathered):
    n = x1.shape[0]
    tt = DSP_TT
    nt = n // tt
    width = gathered.shape[1]
    return pl.pallas_call(
        _blend_kernel,
        grid=(nt,),
        in_specs=[pl.BlockSpec((tt, D_MODEL), lambda i: (i, 0)),
                  pl.BlockSpec((tt, EXPERT_TOP_K), lambda i: (i, 0)),
                  pl.BlockSpec((tt, width), lambda i: (i, 0)),
                  pl.BlockSpec((tt, width), lambda i: (nt + i, 0))],
        out_specs=pl.BlockSpec((tt, D_MODEL), lambda i: (i, 0)),
        out_shape=jax.ShapeDtypeStruct((n, D_MODEL), F32),
        compiler_params=_cparams("parallel"),
        name="moe_blend",
    )(x1, w_col, gathered, gathered)


def _moe(x1, h2, logits_t, w_gate, w_up, w_down, layer):
    n = x1.shape[0]
    e_idx, wts, rank, counts = _route(logits_t)
    counts = counts[:, 0].astype(jnp.int32)
    padded = (counts + MOE_BLK - 1) // MOE_BLK * MOE_BLK
    pends = jnp.cumsum(padded)
    pstarts = pends - padded
    experts = jnp.arange(N_EXPERTS, dtype=jnp.int32)
    start_of = jnp.sum(jnp.where(e_idx[None, :EXPERT_TOP_K] == experts[:, None, None],
                                 pstarts[:, None, None], 0), axis=0)
    dest = start_of + rank[:EXPERT_TOP_K]
    cap = n * EXPERT_TOP_K + N_EXPERTS * MOE_BLK
    n_blk = cap // MOE_BLK
    blk_row = jnp.arange(n_blk, dtype=jnp.int32) * MOE_BLK
    blk_expert = jnp.minimum(jnp.sum((pends[None, :] <= blk_row[:, None]).astype(jnp.int32), axis=1),
                             N_EXPERTS - 1)
    n_used = (pends[-1:] // MOE_BLK).astype(jnp.int32)
    dest_flat = dest.reshape(EXPERT_TOP_K * n)
    buf = _sc_gather_rows(h2, dest_flat, rows=cap)
    out_buf = _experts(blk_expert, n_used, buf, w_gate, w_up, w_down, layer)
    return _blend(x1, wts[:EXPERT_TOP_K].T, _sc_gather_rows(out_buf, dest_flat))


def kernel(x, positions, norm_mix, norm_ffn, w_in, q_gain, k_gain, cmp_k_pos, cmp_k_w1, cmp_k_w2, cmp_v_pos,
           cmp_v_w1, cmp_v_w2, s5_a_re, s5_a_im, s5_log_dt, s5_b_re, s5_b_im, s5_c_re, s5_c_im, s5_d, s5_glu,
           proj_nsa, proj_s5, w_out, router_group, router_group_bias, router_expert, router_expert_bias,
           expert_gate, expert_up, expert_down):
    batch, seq, _ = x.shape
    depth = w_in.shape[0]
    n = batch * seq
    inv_freq = ROPE_THETA ** (-jnp.arange(0, ROPE_DIMS, 2, dtype=jnp.float32) / ROPE_DIMS)
    ctab, stab = _rope_tables(positions.reshape(n, 1), inv_freq)
    ccmp, scmp = _cmp_rope_tables(ctab, stab, batch, seq)
    x2 = x.reshape(n, D_MODEL)
    for l in range(depth):
        w1, w_ga, w_gs = _pack_w_in(w_in[l])
        o_a, u_tm = _nsa_branch(x2, norm_mix[l][None], w1, ctab, stab, ccmp, scmp, q_gain[l], k_gain[l],
                                (cmp_k_pos[l], cmp_k_w1[l], cmp_k_w2[l]),
                                (cmp_v_pos[l], cmp_v_w1[l], cmp_v_w2[l]), batch, seq)
        lre, lim, w_b, w_c = _s5_weights(s5_a_re[l], s5_a_im[l], s5_log_dt[l], s5_b_re[l], s5_b_im[l],
                                         s5_c_re[l], s5_c_im[l])
        ob_tm = _s5(u_tm, lre, lim, w_b, w_c, s5_d[l][None],
                    s5_glu[l].astype(BF16), batch, seq)
        wrh, wrl, rb = _router_weights(router_group[l], router_group_bias[l], router_expert[l],
                                       router_expert_bias[l])
        x1, h2, logits_t = _merge(x2, norm_mix[l][None], w_ga, w_gs, o_a, ob_tm, proj_nsa[l].astype(BF16),
                                  proj_s5[l].astype(BF16), w_out[l].astype(BF16), norm_ffn[l][None],
                                  wrh, wrl, rb, batch, seq)
        x2 = _moe(x1, h2, logits_t, expert_gate, expert_up, expert_down, l)
    return x2.reshape(batch, seq, D_MODEL)


def _attn_tables(seq):
    n_cmp = (seq - CMP_BLOCK) // CMP_STRIDE + 1
    n_slc = seq // SLC_BLOCK
    key_blk = np.arange(seq) // SLC_BLOCK
    expand_t = (key_blk[:, None] == np.arange(LANES)[None, :]).astype(np.float32)
    cmp_start = np.arange(LANES) * CMP_STRIDE
    slc_start = np.arange(LANES) * SLC_BLOCK
    overlap_t = ((cmp_start[None, :] < slc_start[:, None] + SLC_BLOCK)
                 & (cmp_start[None, :] + CMP_BLOCK > slc_start[:, None])
                 & (np.arange(LANES)[None, :] < n_cmp) & (np.arange(LANES)[:, None] < n_slc))
    return (jnp.asarray(expand_t, BF16), jnp.asarray(overlap_t.astype(np.float32), BF16), n_cmp, n_slc)


def _attn_mask_tables():
    tq, tk, wk = ATT_TQ, ATT_TK, ATT_WK
    ql = np.arange(tq)
    onehot = np.tile(np.eye(tq, dtype=np.float32), (NSA_HEADS, 1))
    k = np.arange(wk)[:, None]
    band = []
    for c in range(WINDOW // tq + 1):
        if c < WINDOW // tq:
            keep = k <= c * tq + ql[None, :]
        else:
            keep = (k > ql[None, :]) & (k <= ql[None, :] + WINDOW)
        band.append(np.where(keep, 0.0, NEG_BIG))
    kl = np.arange(tk)[None, :]
    causal = [np.where(kl <= c * tq + ql[:, None], 0.0, NEG_BIG) for c in range(tk // tq)]
    return (jnp.asarray(onehot, BF16), jnp.asarray(np.stack(band), BF16),
            jnp.asarray(np.stack(causal), F32))


def _attention(q, ks, vs, kw, vw, kc, vc, gates, batch, seq):
    expand_t, overlap_t, n_cmp, n_slc = _attn_tables(seq)
    onehot, band, causal = _attn_mask_tables()
    tq = ATT_TQ
    nq = seq // tq
    rows = NSA_HEADS * tq
    tile = lambda w: pl.BlockSpec((tq, w), lambda b, i: (b * nq + i, 0))
    per_b = lambda r: pl.BlockSpec((r, LANES), lambda b, i: (b, 0))
    f32_rows = pltpu.VMEM((rows, LANES), F32)
    return pl.pallas_call(
        functools.partial(_attn_kernel, n_cmp=n_cmp, n_slc=n_slc, top_k=min(SLC_TOPK, n_slc)),
        grid=(batch, nq),
        in_specs=[tile(2 * NSA_WIDTH), per_b(seq), per_b(seq), per_b(seq), per_b(seq),
                  per_b(LANES), per_b(LANES), tile(LANES),
                  pl.BlockSpec(expand_t.shape, lambda b, i: (0, 0)),
                  pl.BlockSpec(overlap_t.shape, lambda b, i: (0, 0)),
                  pl.BlockSpec(onehot.shape, lambda b, i: (0, 0)),
                  pl.BlockSpec(band.shape, lambda b, i: (0, 0, 0)),
                  pl.BlockSpec(causal.shape, lambda b, i: (0, 0, 0))],
        out_specs=tile(NSA_WIDTH),
        out_shape=jax.ShapeDtypeStruct((batch * seq, NSA_WIDTH), BF16),
        scratch_shapes=[pltpu.VMEM((rows, 2 * LANES), BF16),
                        pltpu.VMEM((rows, ATT_TK), F32), pltpu.VMEM((rows, ATT_TK), BF16),
                        pltpu.VMEM((rows, ATT_WK), F32), pltpu.VMEM((rows, ATT_WK), BF16),
                        f32_rows, f32_rows, f32_rows, f32_rows, f32_rows],
        compiler_params=_cparams("parallel", "parallel"),
        name="nsa_attention",
    )(q, ks, vs, kw, vw, kc, vc, gates, expand_t, overlap_t, onehot, band, causal)
```

```python
import functools
import math

import jax
import jax.numpy as jnp
import numpy as np
from jax import lax
from jax.experimental import pallas as pl
from jax.experimental.pallas import tpu as pltpu
from jax.experimental.pallas import tpu_sc as plsc

F32 = jnp.float32
BF16 = jnp.bfloat16

D_MODEL = 1024
NSA_HEADS = 8
NSA_KV_HEADS = 2
NSA_GROUP = NSA_HEADS // NSA_KV_HEADS
HEAD_DIM = 64
ROPE_DIMS = HEAD_DIM // 4
ROPE_THETA = 500000.0
CMP_BLOCK = 32
CMP_STRIDE = 16
CMP_HIDDEN = 2 * HEAD_DIM
SLC_BLOCK = 64
SLC_TOPK = 16
WINDOW = 512
NSA_WIDTH = NSA_HEADS * HEAD_DIM
KV_WIDTH = NSA_KV_HEADS * HEAD_DIM
S5_WIDTH = 256
S5_GROUP_CH = 16
S5_GROUPS = S5_WIDTH // S5_GROUP_CH
S5_STATE = 64
S5_NSTATE = S5_GROUPS * S5_STATE
N_EXPERT_GROUPS = 4
EXPERTS_PER_GROUP = 8
N_EXPERTS = N_EXPERT_GROUPS * EXPERTS_PER_GROUP
EXPERT_TOP_K = 2
EXPERT_FF = 512
RMS_EPS = 1e-6
NEG_BIG = -1e30

LANES = 128
SUBLANES = 8
VMEM_LIMIT = 56 * 1024 * 1024

IN_TM = 512
ATT_TQ = 128
ATT_TK = 512
ATT_WK = WINDOW + ATT_TQ
ATT_RB = 128
S5_T = 128
MRG_TM = 512
RT_TT = 512
MOE_BLK = 512
DSP_TT = 1024

C_Q, C_KS, C_KW, C_KC, C_VC, C_VS, C_VW, C_U, C_G = 0, 512, 640, 768, 896, 1024, 1152, 1280, 1536
IN_W = 1664


def _cparams(*sem):
    return pltpu.CompilerParams(dimension_semantics=sem, vmem_limit_bytes=VMEM_LIMIT)


def _gelu(x):
    return 0.5 * x * (1.0 + jnp.tanh(math.sqrt(2.0 / math.pi) * (x + 0.044715 * (x * x * x))))


def _sigmoid(x):
    return 1.0 / (1.0 + jnp.exp(-x))


def _dot(a, b):
    return jnp.dot(a, b, preferred_element_type=F32)


def _dot_t(a, b):
    return lax.dot_general(a, b, (((1,), (1,)), ((), ())), preferred_element_type=F32)


def _split_bf16(x):
    hi = x.astype(BF16)
    lo = (x - hi.astype(F32)).astype(BF16)
    return hi, lo


def _pack_rows(x):
    k = x.shape[1] // 2
    hi = lax.bitcast_convert_type(x[:, :k].astype(jnp.bfloat16).astype(F32), jnp.int32)
    lo = lax.bitcast_convert_type(x[:, k:].astype(jnp.bfloat16).astype(F32), jnp.int32)
    return hi | lax.shift_right_logical(lo, 16)


def _unpack_rows(w):
    a = lax.bitcast_convert_type(w & jnp.int32(-65536), F32)
    b = lax.bitcast_convert_type(lax.shift_left(w, 16), F32)
    return a, b


def _rope_kernel(pos_ref, freq_ref, sign_ref, c_ref, s_ref):
    ang = pos_ref[...].astype(F32) * freq_ref[...]
    c_ref[...] = jnp.cos(ang)
    s_ref[...] = jnp.sin(ang) * sign_ref[...]


def _rope_tables(pos_col, inv_freq):
    n = pos_col.shape[0]
    lane = np.arange(LANES) % HEAD_DIM
    sel = np.zeros((LANES, ROPE_DIMS // 2), np.float32)
    for l in range(LANES):
        if lane[l] < ROPE_DIMS:
            sel[l, lane[l] % (ROPE_DIMS // 2)] = 1.0
    freq = (jnp.asarray(sel) * inv_freq[None, :]).sum(-1)[None, :]
    sign = np.where(lane < ROPE_DIMS // 2, -1.0, 1.0).astype(np.float32)[None, :]
    tm = 1024
    return pl.pallas_call(
        _rope_kernel,
        grid=(n // tm,),
        in_specs=[pl.BlockSpec((tm, 1), lambda i: (i, 0)),
                  pl.BlockSpec((1, LANES), lambda i: (0, 0)),
                  pl.BlockSpec((1, LANES), lambda i: (0, 0))],
        out_specs=[pl.BlockSpec((tm, LANES), lambda i: (i, 0))] * 2,
        out_shape=[jax.ShapeDtypeStruct((n, LANES), F32)] * 2,
        compiler_params=_cparams("parallel"),
        name="rope_tables",
    )(pos_col, freq, jnp.asarray(sign))


def _headnorm_rope(t, gain, c, s):
    lane = lax.broadcasted_iota(jnp.int32, t.shape, 1)
    lo = lane < HEAD_DIM
    sq = t * t
    s_lo = jnp.sum(jnp.where(lo, sq, 0.0), axis=-1, keepdims=True)
    s_hi = jnp.sum(jnp.where(lo, 0.0, sq), axis=-1, keepdims=True)
    ms = jnp.where(lo, s_lo, s_hi) * (1.0 / HEAD_DIM)
    tn = (t * lax.rsqrt(ms + RMS_EPS)) * gain
    half = ROPE_DIMS // 2
    partner = jnp.where((lane & (HEAD_DIM - 1)) < half,
                        pltpu.roll(tn, LANES - half, 1), pltpu.roll(tn, half, 1))
    return tn * c + partner * s


def _in_proj_kernel(x_ref, gain_ref, w_ref, c_ref, s_ref, qg_ref, kg_ref,
                    q_ref, ks_ref, kw_ref, kc_ref, vc_ref, vs_ref, vw_ref, u_ref, g_ref):
    x = x_ref[...]
    ms = jnp.mean(x * x, axis=-1, keepdims=True)
    h = (x * lax.rsqrt(ms + RMS_EPS)) * gain_ref[...]
    z = _dot(h.astype(BF16), w_ref[...])
    c = c_ref[...]
    s = s_ref[...]
    lane = lax.broadcasted_iota(jnp.int32, c.shape, 1)
    lo = lane < HEAD_DIM
    scale = HEAD_DIM ** -0.5 * math.log2(math.e)
    for j in range(NSA_HEADS // 2):
        r = _headnorm_rope(z[:, C_Q + LANES * j:C_Q + LANES * (j + 1)], qg_ref[...], c, s) * scale
        rr = pltpu.roll(r, HEAD_DIM, 1)
        if j // (NSA_GROUP // 2) == 0:
            even, odd = jnp.where(lo, r, 0.0), jnp.where(lo, rr, 0.0)
        else:
            even, odd = jnp.where(lo, 0.0, rr), jnp.where(lo, 0.0, r)
        q_ref[:, 2 * LANES * j:2 * LANES * j + LANES] = even.astype(BF16)
        q_ref[:, 2 * LANES * j + LANES:2 * LANES * (j + 1)] = odd.astype(BF16)
    ks_ref[...] = _headnorm_rope(z[:, C_KS:C_KS + LANES], kg_ref[...], c, s).astype(BF16)
    kw_ref[...] = _headnorm_rope(z[:, C_KW:C_KW + LANES], kg_ref[...], c, s).astype(BF16)
    kc_ref[...] = z[:, C_KC:C_KC + LANES].astype(BF16)
    vc_ref[...] = z[:, C_VC:C_VC + LANES].astype(BF16)
    vs_ref[...] = z[:, C_VS:C_VS + LANES].astype(BF16)
    vw_ref[...] = z[:, C_VW:C_VW + LANES].astype(BF16)
    u_ref[:, pl.ds(pl.program_id(1), 1), :] = z[:, C_U:C_U + S5_WIDTH][:, None, :]
    g_ref[...] = _sigmoid(z[:, C_G:C_G + LANES])


def _in_proj(x2, gain, w1, ctab, stab, qg, kg, batch, seq):
    n = x2.shape[0]
    tm = IN_TM
    nt = seq // tm
    row = lambda i, b: (b * nt + i, 0)
    const = lambda i, b: (0, 0)
    tok = lambda w: pl.BlockSpec((tm, w), row)
    outs = [(n, 2 * NSA_WIDTH, BF16)] + [(n, LANES, BF16)] * 6
    return pl.pallas_call(
        _in_proj_kernel,
        grid=(nt, batch),
        in_specs=[tok(D_MODEL), pl.BlockSpec((1, D_MODEL), const), pl.BlockSpec((D_MODEL, IN_W), const),
                  tok(LANES), tok(LANES), pl.BlockSpec((1, LANES), const), pl.BlockSpec((1, LANES), const)],
        out_specs=[tok(2 * NSA_WIDTH)] + [tok(LANES)] * 6
        + [pl.BlockSpec((tm, batch, S5_WIDTH), lambda i, b: (i, 0, 0)), tok(LANES)],
        out_shape=[jax.ShapeDtypeStruct(s[:2], s[2]) for s in outs]
        + [jax.ShapeDtypeStruct((seq, batch, S5_WIDTH), F32), jax.ShapeDtypeStruct((n, LANES), F32)],
        compiler_params=_cparams("parallel", "arbitrary"),
        name="in_proj",
    )(x2, gain, w1, ctab, stab, qg, kg)


def _compress_mlp(h_ref, w_ref, p_ref, w2_ref):
    w = w_ref[...]
    pre = _dot(h_ref[...], w)
    phi, plo = _split_bf16(p_ref[...])
    pb = _dot(phi, w) + _dot(plo, w)
    rows = pre.shape[0]
    out = None
    for h in range(NSA_KV_HEADS):
        top = pre[:, LANES * h:LANES * (h + 1)]
        bot = pre[:, LANES * (2 + h):LANES * (3 + h)]
        bias = pb[0:1, LANES * h:LANES * (h + 1)] + pb[1:2, LANES * (2 + h):LANES * (3 + h)]
        a = _gelu(top + pltpu.roll(bot, rows - 1, 0) + bias)
        o = _dot(a.astype(BF16), w2_ref[h])
        out = o if out is None else out + o
    return out


def _compress_kernel(hk_ref, hv_ref, wk_ref, wv_ref, pk_ref, pv_ref, w2k_ref, w2v_ref,
                     kg_ref, c_ref, s_ref, kc_ref, vc_ref):
    k = _compress_mlp(hk_ref, wk_ref, pk_ref, w2k_ref)
    kc_ref[...] = _headnorm_rope(k, kg_ref[...], c_ref[...], s_ref[...]).astype(BF16)
    vc_ref[...] = _compress_mlp(hv_ref, wv_ref, pv_ref, w2v_ref).astype(BF16)


def _compress(hk, hv, wk, wv, pk, pv, w2k, w2v, kg, ccmp, scmp, batch):
    rows = hk.shape[0] // batch
    width = hk.shape[1]
    const2 = lambda b: (0, 0)
    const3 = lambda b: (0, 0, 0)
    rowb = lambda w: pl.BlockSpec((rows, w), lambda b: (b, 0))
    return pl.pallas_call(
        _compress_kernel,
        grid=(batch,),
        in_specs=[rowb(width), rowb(width),
                  pl.BlockSpec(wk.shape, const2), pl.BlockSpec(wv.shape, const2),
                  pl.BlockSpec(pk.shape, const2), pl.BlockSpec(pv.shape, const2),
                  pl.BlockSpec(w2k.shape, const3), pl.BlockSpec(w2v.shape, const3),
                  pl.BlockSpec((1, LANES), const2), rowb(LANES), rowb(LANES)],
        out_specs=[rowb(LANES), rowb(LANES)],
        out_shape=[jax.ShapeDtypeStruct((hk.shape[0], LANES), BF16)] * 2,
        compiler_params=_cparams("parallel"),
        name="compress_kv",
    )(hk, hv, wk, wv, pk, pv, w2k, w2v, kg, ccmp, scmp)


def _compress_weights(pos_emb, w1, w2):
    half = CMP_BLOCK // 2
    eye = jnp.eye(NSA_KV_HEADS, dtype=F32)
    w1r = w1.reshape(2, half, HEAD_DIM, CMP_HIDDEN)
    wexp = jnp.einsum("sldn,ph->lpdshn", w1r, eye)
    wexp = wexp.reshape(half * KV_WIDTH, 2 * NSA_KV_HEADS * CMP_HIDDEN).astype(BF16)
    pexp = jnp.broadcast_to(pos_emb.reshape(2, half, 1, HEAD_DIM), (2, half, NSA_KV_HEADS, HEAD_DIM))
    pexp = jnp.pad(pexp.reshape(2, half * KV_WIDTH), ((0, SUBLANES - 2), (0, 0)))
    w2e = jnp.einsum("cd,ph->pchd", w2, eye).reshape(NSA_KV_HEADS, CMP_HIDDEN, KV_WIDTH)
    return wexp, pexp, w2e.astype(BF16)


def _attn_kernel(q_ref, ks_ref, vs_ref, kw_ref, vw_ref, kc_ref, vc_ref, g_ref, et_ref, ovt_ref, oh_ref, wb_ref,
                 cb_ref, o_ref, qc_ref, s_ref, p_ref, sw_ref, pw_ref, m_ref, al_ref, oc_ref, as_ref, aw_ref,
                 *, n_cmp, n_slc, top_k):
    tq, tk, wk, rb, nh = ATT_TQ, ATT_TK, ATT_WK, ATT_RB, NSA_HEADS
    rows, grp_rows = nh * tq, NSA_GROUP * tq
    qk_rows = 2 * tq
    q0 = pl.program_id(1) * tq
    for a in range(nh):
        qc_ref[a * tq:(a + 1) * tq, 0:LANES] = q_ref[:, LANES * a:LANES * (a + 1)]
    q8 = qc_ref[:, 0:LANES]
    lane = lax.broadcasted_iota(jnp.int32, (tq, LANES), 1)
    t_row = q0 + lax.broadcasted_iota(jnp.int32, (tq, LANES), 0)

    s_c = _dot_t(q8, kc_ref[...]).reshape(nh, tq, LANES)
    mask_c = ((lane * CMP_STRIDE + (CMP_BLOCK - 1)) <= t_row) & (lane < n_cmp)
    sm = jnp.where(mask_c[None], s_c, NEG_BIG)
    p = jnp.where(mask_c[None], jnp.exp2(sm - jnp.max(sm, axis=-1, keepdims=True)), 0.0)
    l = jnp.sum(p, axis=-1, keepdims=True)
    p_c = p * jnp.where(l > 0.0, 1.0 / l, 0.0)
    oc_ref[...] = _dot(p_c.reshape(rows, LANES).astype(BF16), vc_ref[...])

    blk = lax.broadcasted_iota(jnp.int32, (n_slc, tq), 0)
    cur = (q0 + lax.broadcasted_iota(jnp.int32, (n_slc, tq), 1)) >> int(math.log2(SLC_BLOCK))
    forced = (blk == 0) | (blk == cur) | (blk == cur - 1)
    for h in range(NSA_KV_HEADS):
        ps = p_c[NSA_GROUP * h]
        for g in range(1, NSA_GROUP):
            ps = ps + p_c[NSA_GROUP * h + g]
        hi, lo = _split_bf16(ps)
        imp = (_dot_t(ovt_ref[...], hi) + _dot_t(ovt_ref[...], lo))[0:n_slc]
        imp = jnp.where(forced, jnp.inf, jnp.where(blk > cur, -jnp.inf, imp))
        rank = jnp.zeros((n_slc, tq), F32)
        for i in range(n_slc):
            row = imp[i:i + 1, :]
            rank = rank + jnp.where((row > imp) | ((row == imp) & (blk > i)), 1.0, 0.0)
        bias_t = jnp.where(rank < float(top_k), 0.0, NEG_BIG)
        bias = jnp.concatenate([bias_t, jnp.zeros((LANES - n_slc, tq), F32)], axis=0).T.astype(BF16)
        for g in range(NSA_GROUP):
            a = NSA_GROUP * h + g
            qc_ref[a * tq:(a + 1) * tq, LANES:2 * LANES] = bias

    def softmax_pass(s_buf, p_buf, width, causal_case, online):
        for r in range(rows // rb):
            r0 = r * rb
            s = s_buf[pl.ds(r0, rb), 0:width]
            if causal_case is not None:
                s = s + cb_ref[causal_case, pl.ds(r0 % tq, rb), :]
            m_new = jnp.broadcast_to(jnp.max(s, axis=1, keepdims=True), (rb, LANES))
            if online:
                m_old = m_ref[pl.ds(r0, rb), :]
                m_new = jnp.maximum(m_old, m_new)
                al_ref[pl.ds(r0, rb), :] = jnp.exp2(m_old - m_new)
                m_ref[pl.ds(r0, rb), :] = m_new
            pp = jnp.exp2(s - jnp.concatenate([m_new] * (width // LANES), axis=1))
            p_buf[pl.ds(r0, rb), 0:width] = pp.astype(BF16)

    def pv(acc_ref, p_buf, v, width, online):
        lane_v = lax.broadcasted_iota(jnp.int32, v.shape, 1)
        for h in range(NSA_KV_HEADS):
            rs = slice(h * grp_rows, (h + 1) * grp_rows)
            own = (lane_v < HEAD_DIM) if h == 0 else (lane_v >= HEAD_DIM)
            upd = _dot(p_buf[rs, 0:width], jnp.where(own, v, jnp.ones_like(v)))
            acc_ref[rs, :] = (al_ref[rs, :] * acc_ref[rs, :] + upd) if online else upd

    m_ref[...] = jnp.full(m_ref.shape, 0.1 * NEG_BIG, F32)
    as_ref[...] = jnp.zeros(as_ref.shape, F32)

    def slc_chunk(c, diagonal):
        k0 = c * tk
        s_buf, p_buf = s_ref.at[c % 2], p_ref.at[c % 2]
        kcat = jnp.concatenate([ks_ref[pl.ds(k0, tk), :], et_ref[pl.ds(k0, tk), :]], axis=1)
        for g in range(rows // qk_rows):
            rs = slice(g * qk_rows, (g + 1) * qk_rows)
            s_buf[rs, :] = _dot_t(qc_ref[rs, :], kcat)
        softmax_pass(s_buf, p_buf, tk, ((q0 - k0) >> int(math.log2(tq))) if diagonal else None, True)
        pv(as_ref, p_buf, vs_ref[pl.ds(k0, tk), :], tk, True)

    def window_and_output():
        w0 = pl.multiple_of(jnp.maximum(q0 - WINDOW, 0), LANES)
        w_case = jnp.minimum(q0 >> int(math.log2(tq)), WINDOW // tq)
        kcat_w = jnp.concatenate([kw_ref[pl.ds(w0, wk), :], wb_ref[w_case]], axis=1)
        for g in range(rows // qk_rows):
            rs = slice(g * qk_rows, (g + 1) * qk_rows)
            sw_ref[rs, :] = _dot_t(jnp.concatenate([qc_ref[rs, 0:LANES], oh_ref[rs, :]], axis=1), kcat_w)
        softmax_pass(sw_ref, pw_ref, wk, None, False)
        pv(aw_ref, pw_ref, vw_ref[pl.ds(w0, wk), :], wk, False)

        lo_half = lane < HEAD_DIM
        for j in range(nh // 2):
            comb = []
            for a in (2 * j, 2 * j + 1):
                rs = slice(a * tq, (a + 1) * tq)
                o_s, o_w = as_ref[rs, :], aw_ref[rs, :]
                comb.append(g_ref[:, 3 * a:3 * a + 1] * oc_ref[rs, :]
                            + g_ref[:, 3 * a + 1:3 * a + 2] * (o_s * (1.0 / pltpu.roll(o_s, HEAD_DIM, 1)))
                            + g_ref[:, 3 * a + 2:3 * a + 3] * (o_w * (1.0 / pltpu.roll(o_w, HEAD_DIM, 1))))
            even, odd = comb
            if j // (NSA_GROUP // 2) == 0:
                pair = jnp.where(lo_half, even, pltpu.roll(odd, HEAD_DIM, 1))
            else:
                pair = jnp.where(lo_half, pltpu.roll(even, HEAD_DIM, 1), odd)
            o_ref[:, LANES * j:LANES * (j + 1)] = pair.astype(BF16)

    c_last = q0 >> int(math.log2(tk))
    for n_full in range(ks_ref.shape[0] // tk):
        @pl.when(c_last == n_full)
        def _():
            for c in range(n_full):
                slc_chunk(c, False)
            slc_chunk(n_full, True)
            window_and_output()


def _pack_w_in(w):
    sizes = (NSA_WIDTH,) + (KV_WIDTH,) * 6 + (3 * NSA_HEADS, S5_WIDTH, D_MODEL, D_MODEL)
    o = np.cumsum((0,) + sizes)
    q, kc, vc, ks, vs, kw, vw, g, u, ga, gs = [w[:, o[i]:o[i + 1]] for i in range(len(sizes))]
    g = jnp.pad(g, ((0, 0), (0, LANES - 3 * NSA_HEADS)))
    w1 = jnp.concatenate([q, ks, kw, kc, vc, vs, vw, u, g], axis=1).astype(BF16)
    return w1, ga.astype(BF16), gs.astype(BF16)


def _nsa_branch(x2, gain, w1, ctab, stab, ccmp, scmp, q_gain, k_gain, cmp_k, cmp_v, batch, seq):
    qg = jnp.tile(q_gain, 2)[None, :]
    kg = jnp.tile(k_gain, 2)[None, :]
    q, ks, kw, kc_raw, vc_raw, vs, vw, u_tm, gates = _in_proj(x2, gain, w1, ctab, stab, qg, kg, batch, seq)
    half_rows = batch * seq // (CMP_BLOCK // 2)
    hk = kc_raw.reshape(half_rows, (CMP_BLOCK // 2) * KV_WIDTH)
    hv = vc_raw.reshape(half_rows, (CMP_BLOCK // 2) * KV_WIDTH)
    wk, pk, w2k = _compress_weights(*cmp_k)
    wv, pv, w2v = _compress_weights(*cmp_v)
    kc, vc = _compress(hk, hv, wk, wv, pk, pv, w2k, w2v, kg, ccmp, scmp, batch)
    o_a = _attention(q, ks, vs, kw, vw, kc, vc, gates, batch, seq)
    return o_a, u_tm


def _cmp_rope_tables(ctab, stab, batch, seq):
    def pick(t):
        t = t.reshape(batch, seq, LANES)[:, CMP_BLOCK - 1::CMP_STRIDE]
        t = jnp.pad(t, ((0, 0), (0, seq // CMP_STRIDE - t.shape[1]), (0, 0)))
        return t.reshape(batch * (seq // CMP_STRIDE), LANES)
    return pick(ctab), pick(stab)


def _s5_disc_kernel(are_ref, aim_ref, ldt_ref, lre_ref, lim_ref, fre_ref, fim_ref):
    lam_re = jnp.minimum(are_ref[...], -1e-4)
    lam_im = aim_ref[...]
    dt = jnp.exp(ldt_ref[...])
    mag = jnp.exp(lam_re * dt)
    ang = lam_im * dt
    lb_re = mag * jnp.cos(ang)
    lb_im = mag * jnp.sin(ang)
    den = lam_re * lam_re + lam_im * lam_im
    lre_ref[...] = lb_re
    lim_ref[...] = lb_im
    fre_ref[...] = ((lb_re - 1.0) * lam_re + lb_im * lam_im) / den
    fim_ref[...] = (lb_im * lam_re - (lb_re - 1.0) * lam_im) / den


def _s5_weights(a_re, a_im, log_dt, b_re, b_im, c_re, c_im):
    shp = jax.ShapeDtypeStruct((S5_GROUPS, S5_STATE), F32)
    lb_re, lb_im, f_re, f_im = pl.pallas_call(
        _s5_disc_kernel, out_shape=[shp] * 4, name="s5_discretise",
    )(a_re, a_im, log_dt[:, None])
    bb_re = f_re[..., None] * b_re - f_im[..., None] * b_im
    bb_im = f_re[..., None] * b_im + f_im[..., None] * b_re
    eye = jnp.eye(S5_GROUPS, dtype=F32)

    def in_map(bb):
        return jnp.einsum("gpc,gh->gchp", bb, eye).reshape(S5_WIDTH, S5_NSTATE)

    def out_map(cc):
        return jnp.einsum("gcp,gh->hpgc", cc, eye).reshape(S5_NSTATE, S5_WIDTH)

    w_b = jnp.concatenate([in_map(bb_re), in_map(bb_im)], axis=1).astype(BF16)
    w_c = jnp.concatenate([out_map(c_re), -out_map(c_im)], axis=0).astype(BF16)
    return lb_re.reshape(1, S5_NSTATE), lb_im.reshape(1, S5_NSTATE), w_b, w_c


def _s5_kernel(u_ref, wb_ref, lre_ref, lim_ref, wc_ref, d_ref, wg_ref, o_ref, x_ref, st_ref, *, batch):
    ns = S5_NSTATE

    @pl.when(pl.program_id(0) == 0)
    def _():
        st_ref[...] = jnp.zeros(st_ref.shape, F32)

    u = u_ref[...].reshape(u_ref.shape[0] * batch, S5_WIDTH)
    x_ref[...] = _dot(u.astype(BF16), wb_ref[...])
    lre = jnp.broadcast_to(lre_ref[...], (batch, ns))
    lim = jnp.broadcast_to(lim_ref[...], (batch, ns))

    def step(t, carry):
        xr, xi = carry
        r0 = pl.multiple_of(t * batch, batch)
        nr = lre * xr - lim * xi + x_ref[pl.ds(r0, batch), 0:ns]
        ni = lre * xi + lim * xr + x_ref[pl.ds(r0, batch), ns:2 * ns]
        x_ref[pl.ds(r0, batch), 0:ns] = nr
        x_ref[pl.ds(r0, batch), ns:2 * ns] = ni
        return nr, ni

    steps = u.shape[0] // batch
    xr, xi = lax.fori_loop(0, steps, step, (st_ref[:, 0:ns], st_ref[:, ns:2 * ns]), unroll=4)
    st_ref[:, 0:ns] = xr
    st_ref[:, ns:2 * ns] = xi
    y = _dot(x_ref[...].astype(BF16), wc_ref[...]) + d_ref[...] * u
    z = _dot(_gelu(y).astype(BF16), wg_ref[...])
    o = z[:, :S5_WIDTH] * _sigmoid(z[:, S5_WIDTH:])
    for c in range(S5_WIDTH // LANES):
        o_ref[c] = o[:, c * LANES:(c + 1) * LANES]


def _s5(u_tm, lre, lim, w_b, w_c, d_skip, w_glu, batch, seq):
    rows = S5_T * batch
    const = lambda t: (0, 0)
    return pl.pallas_call(
        functools.partial(_s5_kernel, batch=batch),
        grid=(seq // S5_T,),
        in_specs=[pl.BlockSpec((S5_T, batch, S5_WIDTH), lambda t: (t, 0, 0)),
                  pl.BlockSpec(w_b.shape, const), pl.BlockSpec(lre.shape, const), pl.BlockSpec(lim.shape, const),
                  pl.BlockSpec(w_c.shape, const), pl.BlockSpec((1, S5_WIDTH), const),
                  pl.BlockSpec(w_glu.shape, const)],
        out_specs=pl.BlockSpec((S5_WIDTH // LANES, rows, LANES), lambda t: (0, t, 0)),
        out_shape=jax.ShapeDtypeStruct((S5_WIDTH // LANES, seq * batch, LANES), F32),
        scratch_shapes=[pltpu.VMEM((rows, 2 * S5_NSTATE), F32), pltpu.VMEM((batch, 2 * S5_NSTATE), F32)],
        compiler_params=_cparams("arbitrary"),
        name="s5_scan",
    )(u_tm, w_b, lre, lim, w_c, d_skip, w_glu)


RT_GROUP_ROW = 0
RT_EXPERT_ROW = SUBLANES


def _rmsnorm(x, gain):
    return (x * lax.rsqrt(jnp.mean(x * x, axis=-1, keepdims=True) + RMS_EPS)) * gain


def _merge_kernel(x_ref, g1_ref, wga_ref, wgs_ref, oa_ref, ob_ref, pa_ref, ps_ref, wo_ref, g2_ref,
                  wrh_ref, wrl_ref, rb_ref, x1_ref, h2_ref, lg_ref):
    x = x_ref[...]
    h = _rmsnorm(x, g1_ref[...]).astype(BF16)
    gate_a = _sigmoid(_dot(h, wga_ref[...]))
    gate_s = _sigmoid(_dot(h, wgs_ref[...]))
    rows_b = pl.ds(pl.program_id(1), x.shape[0], stride=ob_ref.shape[1] // x.shape[0])
    o_b = jnp.concatenate([ob_ref[c, rows_b, :] for c in range(S5_WIDTH // LANES)], axis=1)
    merged = gate_a * _dot(oa_ref[...], pa_ref[...]) + gate_s * _dot(o_b.astype(BF16), ps_ref[...])
    x1 = x + _dot(merged.astype(BF16), wo_ref[...])
    x1_ref[...] = x1
    h2 = _rmsnorm(x1, g2_ref[...])
    h2_ref[...] = _pack_rows(h2)
    hi, lo = _split_bf16(h2)
    lg_ref[...] = (_dot_t(wrh_ref[...], hi) + _dot_t(wrh_ref[...], lo) + _dot_t(wrl_ref[...], hi)
                   + rb_ref[...])


def _router_weights(w_rg, b_rg, w_re, b_re):
    def rows(group_part, expert_part):
        return jnp.concatenate([
            group_part, jnp.zeros((RT_EXPERT_ROW - N_EXPERT_GROUPS,) + group_part.shape[1:], F32),
            expert_part, jnp.zeros((LANES - RT_EXPERT_ROW - N_EXPERTS,) + group_part.shape[1:], F32)], axis=0)

    hi, lo = _split_bf16(rows(w_rg.T, w_re.T))
    return hi, lo, rows(b_rg[:, None], b_re[:, None])


def _merge(x2, g1, w_ga, w_gs, o_a, ob_tm, p_a, p_s, w_o, g2, wrh, wrl, rb, batch, seq):
    n = x2.shape[0]
    tm = MRG_TM
    nt = seq // tm
    row = lambda i, b: (b * nt + i, 0)
    const = lambda i, b: (0, 0)
    tok = lambda w: pl.BlockSpec((tm, w), row)
    full = lambda a: pl.BlockSpec(a.shape, const)
    return pl.pallas_call(
        _merge_kernel,
        grid=(nt, batch),
        in_specs=[tok(D_MODEL), full(g1), full(w_ga), full(w_gs), tok(NSA_WIDTH),
                  pl.BlockSpec((S5_WIDTH // LANES, tm * batch, LANES), lambda i, b: (0, i, 0)),
                  full(p_a), full(p_s), full(w_o),
                  full(g2), full(wrh), full(wrl), full(rb)],
        out_specs=[tok(D_MODEL), tok(D_MODEL // 2), pl.BlockSpec((LANES, tm), lambda i, b: (0, b * nt + i))],
        out_shape=[jax.ShapeDtypeStruct((n, D_MODEL), F32), jax.ShapeDtypeStruct((n, D_MODEL // 2), jnp.int32),
                   jax.ShapeDtypeStruct((LANES, n), F32)],
        compiler_params=_cparams("parallel", "parallel"),
        name="merge_out_proj",
    )(x2, g1, w_ga, w_gs, o_a, ob_tm, p_a, p_s, w_o, g2, wrh, wrl, rb)


def _first_argmax(v, rows):
    vmax = jnp.max(v, axis=0, keepdims=True)
    idx = jnp.min(jnp.where(v == vmax, rows, SUBLANES), axis=0, keepdims=True)
    return vmax, idx


def _route_kernel(lg_ref, tri_ref, e_ref, w_ref, r_ref, cnt_ref, carry_ref):
    tt = lg_ref.shape[1]

    @pl.when(pl.program_id(0) == 0)
    def _():
        carry_ref[...] = jnp.zeros(carry_ref.shape, F32)

    rows = lax.broadcasted_iota(jnp.int32, (SUBLANES, tt), 0)
    gl = jnp.where(rows < N_EXPERT_GROUPS, lg_ref[RT_GROUP_ROW:RT_GROUP_ROW + SUBLANES, :], -jnp.inf)
    gmax, g_sel = _first_argmax(gl, rows)
    g_w = 1.0 / jnp.sum(jnp.exp(gl - gmax), axis=0, keepdims=True)

    el = jnp.zeros((EXPERTS_PER_GROUP, tt), F32)
    for g in range(N_EXPERT_GROUPS):
        r0 = RT_EXPERT_ROW + EXPERTS_PER_GROUP * g
        el = jnp.where(g_sel == g, lg_ref[r0:r0 + EXPERTS_PER_GROUP, :], el)
    ex = jnp.exp(el - jnp.max(el, axis=0, keepdims=True))
    p = ex / jnp.sum(ex, axis=0, keepdims=True)
    p1, i1 = _first_argmax(p, rows)
    p2, i2 = _first_argmax(jnp.where(rows == i1, -1.0, p), rows)
    e1 = g_sel * EXPERTS_PER_GROUP + i1
    e2 = g_sel * EXPERTS_PER_GROUP + i2
    scale = g_w / (p1 + p2)

    erow = lax.broadcasted_iota(jnp.int32, (N_EXPERTS, tt), 0)
    oh1 = jnp.where(erow == e1, 1.0, 0.0)
    oh2 = jnp.where(erow == e2, 1.0, 0.0)
    both = oh1 + oh2
    before = carry_ref[:, 0:1] + _dot(both.astype(BF16), tri_ref[...])
    carry_ref[...] = carry_ref[...] + jnp.sum(both, axis=1, keepdims=True)
    cnt_ref[...] = carry_ref[...]

    zi = jnp.zeros((SUBLANES - 2, tt), jnp.int32)
    e_ref[...] = jnp.concatenate([e1, e2, zi], axis=0)
    w_ref[...] = jnp.concatenate([p1 * scale, p2 * scale, zi.astype(F32)], axis=0)
    r_ref[...] = jnp.concatenate([jnp.sum(oh1 * before, axis=0, keepdims=True),
                                  jnp.sum(oh2 * before, axis=0, keepdims=True),
                                  zi.astype(F32)], axis=0).astype(jnp.int32)


def _route(logits_t):
    n = logits_t.shape[1]
    tt = RT_TT
    tri = jnp.asarray(np.triu(np.ones((tt, tt), np.float32), 1), BF16)
    blk = pl.BlockSpec((SUBLANES, tt), lambda i: (0, i))
    return pl.pallas_call(
        _route_kernel,
        grid=(n // tt,),
        in_specs=[pl.BlockSpec((LANES, tt), lambda i: (0, i)), pl.BlockSpec((tt, tt), lambda i: (0, 0))],
        out_specs=[blk, blk, blk, pl.BlockSpec((N_EXPERTS, LANES), lambda i: (0, 0))],
        out_shape=[jax.ShapeDtypeStruct((SUBLANES, n), jnp.int32), jax.ShapeDtypeStruct((SUBLANES, n), F32),
                   jax.ShapeDtypeStruct((SUBLANES, n), jnp.int32), jax.ShapeDtypeStruct((N_EXPERTS, LANES), F32)],
        scratch_shapes=[pltpu.VMEM((N_EXPERTS, LANES), F32)],
        compiler_params=_cparams("arbitrary"),
        name="moe_route",
    )(logits_t, tri)


def _expert_kernel(be_ref, nb_ref, x_ref, wg_ref, wu_ref, wd_ref, o_ref, wgb_ref, wub_ref, wdb_ref):
    i = pl.program_id(0)
    half = D_MODEL // 2

    @pl.when(i < nb_ref[0])
    def _():
        @pl.when((i == 0) | (be_ref[i] != be_ref[jnp.maximum(i - 1, 0)]))
        def _():
            wgb_ref[...] = wg_ref[...].astype(BF16)
            wub_ref[...] = wu_ref[...].astype(BF16)
            wdb_ref[...] = wd_ref[...].astype(BF16)

        xa, xb = _unpack_rows(x_ref[...])
        xa, xb = xa.astype(BF16), xb.astype(BF16)
        g = _dot(xa, wgb_ref[0:half, :]) + _dot(xb, wgb_ref[half:, :])
        u = _dot(xa, wub_ref[0:half, :]) + _dot(xb, wub_ref[half:, :])
        o_ref[...] = _pack_rows(_dot(((g * _sigmoid(g)) * u).astype(BF16), wdb_ref[...]))

    @pl.when(i >= nb_ref[0])
    def _():
        o_ref[...] = jnp.zeros(o_ref.shape, o_ref.dtype)


def _experts(blk_expert, n_used, buf, w_gate, w_up, w_down, layer):
    cap, width = buf.shape
    blk = MOE_BLK
    row = lambda i, be, nb: (jnp.minimum(i, nb[0] - 1), 0)
    wmap = lambda i, be, nb: (layer, be[jnp.minimum(i, nb[0] - 1)], 0, 0)
    return pl.pallas_call(
        _expert_kernel,
        grid_spec=pltpu.PrefetchScalarGridSpec(
            num_scalar_prefetch=2,
            grid=(cap // blk,),
            in_specs=[pl.BlockSpec((blk, width), row),
                      pl.BlockSpec((None, None, D_MODEL, EXPERT_FF), wmap),
                      pl.BlockSpec((None, None, D_MODEL, EXPERT_FF), wmap),
                      pl.BlockSpec((None, None, EXPERT_FF, D_MODEL), wmap)],
            out_specs=pl.BlockSpec((blk, width), lambda i, be, nb: (i, 0)),
            scratch_shapes=[pltpu.VMEM((D_MODEL, EXPERT_FF), BF16), pltpu.VMEM((D_MODEL, EXPERT_FF), BF16),
                            pltpu.VMEM((EXPERT_FF, D_MODEL), BF16)],
        ),
        out_shape=jax.ShapeDtypeStruct((cap, width), buf.dtype),
        compiler_params=_cparams("arbitrary"),
        name="moe_experts",
    )(blk_expert, n_used, buf, w_gate, w_up, w_down)


SC_CH = 64


def _sc_mesh():
    return plsc.VectorSubcoreMesh(core_axis_name="c", subcore_axis_name="s")


def _sc_worker(info):
    return lax.axis_index("s") * info.num_cores + lax.axis_index("c")


def _sc_gather_rows(table, idx, rows=None):
    info = plsc.get_sparse_core_info()
    lanes, workers = info.num_lanes, info.num_cores * info.num_subcores
    invert = rows is not None
    rows = rows if invert else idx.shape[0]
    width = table.shape[1]
    per_w = rows // workers
    n_chunks = per_w // SC_CH
    assert per_w % (2 * SC_CH) == 0
    scan = 2048
    n_tok = table.shape[0]

    def body(table_hbm, idx_hbm, out_hbm, idx_v, rows_a, rows_b, sem_a, sem_b, *scan_v):
        base = _sc_worker(info) * per_w
        if invert:
            lane = lax.iota(jnp.int32, lanes)

            @pl.loop(0, per_w // lanes)
            def _(j):
                idx_v[pl.ds(j * lanes, lanes)] = (base + j * lanes + lane) & (n_tok - 1)

            @pl.loop(0, idx.shape[0] // scan)
            def _(c):
                pltpu.sync_copy(idx_hbm.at[pl.ds(c * scan, scan)], scan_v[0])

                @plsc.parallel_loop(0, scan // lanes, unroll=8)
                def _(j):
                    rel = scan_v[0][pl.ds(j * lanes, lanes)] - base
                    mine = (rel >= 0) & (rel < per_w)
                    token = (c * scan + j * lanes + lane) & (n_tok - 1)
                    plsc.store_scatter(idx_v, [jnp.where(mine, rel, 0)], token, mask=mine)
        else:
            pltpu.sync_copy(idx_hbm.at[pl.ds(base, per_w)], idx_v)

        def gather(j, buf, sem):
            return pltpu.make_async_copy(table_hbm.at[idx_v.at[pl.ds(j * SC_CH, SC_CH)]], buf, sem)

        def write(j, buf):
            pltpu.sync_copy(buf, out_hbm.at[pl.ds(base + j * SC_CH, SC_CH)])

        gather(0, rows_a, sem_a).start()

        @pl.loop(0, n_chunks, step=2)
        def _(j):
            gather(j + 1, rows_b, sem_b).start()
            gather(j, rows_a, sem_a).wait()
            write(j, rows_a)

            @pl.when(j + 2 < n_chunks)
            def _():
                gather(j + 2, rows_a, sem_a).start()

            gather(j + 1, rows_b, sem_b).wait()
            write(j + 1, rows_b)

    chunk_buf = pltpu.VMEM((SC_CH, width), table.dtype)
    scratch = [pltpu.VMEM((per_w,), jnp.int32), chunk_buf, chunk_buf,
               pltpu.SemaphoreType.DMA, pltpu.SemaphoreType.DMA]
    return pl.kernel(
        body, out_type=jax.ShapeDtypeStruct((rows, width), table.dtype), mesh=_sc_mesh(),
        scratch_types=scratch + ([pltpu.VMEM((scan,), jnp.int32)] if invert else []),
        compiler_params=pltpu.CompilerParams(needs_layout_passes=False) if invert else None,
        name="sc_dispatch_rows" if invert else "sc_gather_rows",
    )(table, idx)


def _blend_kernel(x1_ref, w_ref, g0_ref, g1_ref, o_ref):
    half = x1_ref.shape[1] // 2
    a0, b0 = _unpack_rows(g0_ref[...])
    a1, b1 = _unpack_rows(g1_ref[...])
    w0, w1 = w_ref[:, 0:1], w_ref[:, 1:2]
    o_ref[:, 0:half] = x1_ref[:, 0:half] + w0 * a0 + w1 * a1
    o_ref[:, half:] = x1_ref[:, half:] + w0 * b0 + w1 * b1


def _blend(x1, w_col, gathered):
    n = x1.shape[0]
    tt = DSP_TT
    nt = n // tt
    width = gathered.shape[1]
    return pl.pallas_call(
        _blend_kernel,
        grid=(nt,),
        in_specs=[pl.BlockSpec((tt, D_MODEL), lambda i: (i, 0)),
                  pl.BlockSpec((tt, EXPERT_TOP_K), lambda i: (i, 0)),
                  pl.BlockSpec((tt, width), lambda i: (i, 0)),
                  pl.BlockSpec((tt, width), lambda i: (nt + i, 0))],
        out_specs=pl.BlockSpec((tt, D_MODEL), lambda i: (i, 0)),
        out_shape=jax.ShapeDtypeStruct((n, D_MODEL), F32),
        compiler_params=_cparams("parallel"),
        name="moe_blend",
    )(x1, w_col, gathered, gathered)


def _moe(x1, h2, logits_t, w_gate, w_up, w_down, layer):
    n = x1.shape[0]
    e_idx, wts, rank, counts = _route(logits_t)
    counts = counts[:, 0].astype(jnp.int32)
    padded = (counts + MOE_BLK - 1) // MOE_BLK * MOE_BLK
    pends = jnp.cumsum(padded)
    pstarts = pends - padded
    experts = jnp.arange(N_EXPERTS, dtype=jnp.int32)
    start_of = jnp.sum(jnp.where(e_idx[None, :EXPERT_TOP_K] == experts[:, None, None],
                                 pstarts[:, None, None], 0), axis=0)
    dest = start_of + rank[:EXPERT_TOP_K]
    cap = n * EXPERT_TOP_K + N_EXPERTS * MOE_BLK
    n_blk = cap // MOE_BLK
    blk_row = jnp.arange(n_blk, dtype=jnp.int32) * MOE_BLK
    blk_expert = jnp.minimum(jnp.sum((pends[None, :] <= blk_row[:, None]).astype(jnp.int32), axis=1),
                             N_EXPERTS - 1)
    n_used = (pends[-1:] // MOE_BLK).astype(jnp.int32)
    dest_flat = dest.reshape(EXPERT_TOP_K * n)
    buf = _sc_gather_rows(h2, dest_flat, rows=cap)
    out_buf = _experts(blk_expert, n_used, buf, w_gate, w_up, w_down, layer)
    return _blend(x1, wts[:EXPERT_TOP_K].T, _sc_gather_rows(out_buf, dest_flat))


def kernel(x, positions, norm_mix, norm_ffn, w_in, q_gain, k_gain, cmp_k_pos, cmp_k_w1, cmp_k_w2, cmp_v_pos,
           cmp_v_w1, cmp_v_w2, s5_a_re, s5_a_im, s5_log_dt, s5_b_re, s5_b_im, s5_c_re, s5_c_im, s5_d, s5_glu,
           proj_nsa, proj_s5, w_out, router_group, router_group_bias, router_expert, router_expert_bias,
           expert_gate, expert_up, expert_down):
    batch, seq, _ = x.shape
    depth = w_in.shape[0]
    n = batch * seq
    inv_freq = ROPE_THETA ** (-jnp.arange(0, ROPE_DIMS, 2, dtype=jnp.float32) / ROPE_DIMS)
    ctab, stab = _rope_tables(positions.reshape(n, 1), inv_freq)
    ccmp, scmp = _cmp_rope_tables(ctab, stab, batch, seq)
    x2 = x.reshape(n, D_MODEL)
    for l in range(depth):
        w1, w_ga, w_gs = _pack_w_in(w_in[l])
        o_a, u_tm = _nsa_branch(x2, norm_mix[l][None], w1, ctab, stab, ccmp, scmp, q_gain[l], k_gain[l],
                                (cmp_k_pos[l], cmp_k_w1[l], cmp_k_w2[l]),
                                (cmp_v_pos[l], cmp_v_w1[l], cmp_v_w2[l]), batch, seq)
        lre, lim, w_b, w_c = _s5_weights(s5_a_re[l], s5_a_im[l], s5_log_dt[l], s5_b_re[l], s5_b_im[l],
                                         s5_c_re[l], s5_c_im[l])
        ob_tm = _s5(u_tm, lre, lim, w_b, w_c, s5_d[l][None],
                    s5_glu[l].astype(BF16), batch, seq)
        wrh, wrl, rb = _router_weights(router_group[l], router_group_bias[l], router_expert[l],
                                       router_expert_bias[l])
        x1, h2, logits_t = _merge(x2, norm_mix[l][None], w_ga, w_gs, o_a, ob_tm, proj_nsa[l].astype(BF16),
                                  proj_s5[l].astype(BF16), w_out[l].astype(BF16), norm_ffn[l][None],
                                  wrh, wrl, rb, batch, seq)
        x2 = _moe(x1, h2, logits_t, expert_gate, expert_up, expert_down, l)
    return x2.reshape(batch, seq, D_MODEL)


def _attn_tables(seq):
    n_cmp = (seq - CMP_BLOCK) // CMP_STRIDE + 1
    n_slc = seq // SLC_BLOCK
    key_blk = np.arange(seq) // SLC_BLOCK
    expand_t = (key_blk[:, None] == np.arange(LANES)[None, :]).astype(np.float32)
    cmp_start = np.arange(LANES) * CMP_STRIDE
    slc_start = np.arange(LANES) * SLC_BLOCK
    overlap_t = ((cmp_start[None, :] < slc_start[:, None] + SLC_BLOCK)
                 & (cmp_start[None, :] + CMP_BLOCK > slc_start[:, None])
                 & (np.arange(LANES)[None, :] < n_cmp) & (np.arange(LANES)[:, None] < n_slc))
    return (jnp.asarray(expand_t, BF16), jnp.asarray(overlap_t.astype(np.float32), BF16), n_cmp, n_slc)


def _attn_mask_tables():
    tq, tk, wk = ATT_TQ, ATT_TK, ATT_WK
    ql = np.arange(tq)
    onehot = np.tile(np.eye(tq, dtype=np.float32), (NSA_HEADS, 1))
    k = np.arange(wk)[:, None]
    band = []
    for c in range(WINDOW // tq + 1):
        if c < WINDOW // tq:
            keep = k <= c * tq + ql[None, :]
        else:
            keep = (k > ql[None, :]) & (k <= ql[None, :] + WINDOW)
        band.append(np.where(keep, 0.0, NEG_BIG))
    kl = np.arange(tk)[None, :]
    causal = [np.where(kl <= c * tq + ql[:, None], 0.0, NEG_BIG) for c in range(tk // tq)]
    return (jnp.asarray(onehot, BF16), jnp.asarray(np.stack(band), BF16),
            jnp.asarray(np.stack(causal), F32))


def _attention(q, ks, vs, kw, vw, kc, vc, gates, batch, seq):
    expand_t, overlap_t, n_cmp, n_slc = _attn_tables(seq)
    onehot, band, causal = _attn_mask_tables()
    tq = ATT_TQ
    nq = seq // tq
    rows = NSA_HEADS * tq
    tile = lambda w: pl.BlockSpec((tq, w), lambda b, i: (b * nq + i, 0))
    per_b = lambda r: pl.BlockSpec((r, LANES), lambda b, i: (b, 0))
    f32_rows = pltpu.VMEM((rows, LANES), F32)
    return pl.pallas_call(
        functools.partial(_attn_kernel, n_cmp=n_cmp, n_slc=n_slc, top_k=min(SLC_TOPK, n_slc)),
        grid=(batch, nq),
        in_specs=[tile(2 * NSA_WIDTH), per_b(seq), per_b(seq), per_b(seq), per_b(seq),
                  per_b(LANES), per_b(LANES), tile(LANES),
                  pl.BlockSpec(expand_t.shape, lambda b, i: (0, 0)),
                  pl.BlockSpec(overlap_t.shape, lambda b, i: (0, 0)),
                  pl.BlockSpec(onehot.shape, lambda b, i: (0, 0)),
                  pl.BlockSpec(band.shape, lambda b, i: (0, 0, 0)),
                  pl.BlockSpec(causal.shape, lambda b, i: (0, 0, 0))],
        out_specs=tile(NSA_WIDTH),
        out_shape=jax.ShapeDtypeStruct((batch * seq, NSA_WIDTH), BF16),
        scratch_shapes=[pltpu.VMEM((rows, 2 * LANES), BF16),
                        pltpu.VMEM((2, rows, ATT_TK), F32), pltpu.VMEM((2, rows, ATT_TK), BF16),
                        pltpu.VMEM((rows, ATT_WK), F32), pltpu.VMEM((rows, ATT_WK), BF16),
                        f32_rows, f32_rows, f32_rows, f32_rows, f32_rows],
        compiler_params=_cparams("parallel", "parallel"),
        name="nsa_attention",
    )(q, ks, vs, kw, vw, kc, vc, gates, expand_t, overlap_t, onehot, band, causal)
```

```python
import functools
import math

import jax
import jax.numpy as jnp
import numpy as np
from jax import lax
from jax.experimental import pallas as pl
from jax.experimental.pallas import tpu as pltpu
from jax.experimental.pallas import tpu_sc as plsc

F32 = jnp.float32
BF16 = jnp.bfloat16

D_MODEL = 1024
NSA_HEADS = 8
NSA_KV_HEADS = 2
NSA_GROUP = NSA_HEADS // NSA_KV_HEADS
HEAD_DIM = 64
ROPE_DIMS = HEAD_DIM // 4
ROPE_THETA = 500000.0
CMP_BLOCK = 32
CMP_STRIDE = 16
CMP_HIDDEN = 2 * HEAD_DIM
SLC_BLOCK = 64
SLC_TOPK = 16
WINDOW = 512
NSA_WIDTH = NSA_HEADS * HEAD_DIM
KV_WIDTH = NSA_KV_HEADS * HEAD_DIM
S5_WIDTH = 256
S5_GROUP_CH = 16
S5_GROUPS = S5_WIDTH // S5_GROUP_CH
S5_STATE = 64
S5_NSTATE = S5_GROUPS * S5_STATE
N_EXPERT_GROUPS = 4
EXPERTS_PER_GROUP = 8
N_EXPERTS = N_EXPERT_GROUPS * EXPERTS_PER_GROUP
EXPERT_TOP_K = 2
EXPERT_FF = 512
RMS_EPS = 1e-6
NEG_BIG = -1e30

LANES = 128
SUBLANES = 8
VMEM_LIMIT = 56 * 1024 * 1024

IN_TM = 512
ATT_TQ = 128
ATT_TK = 512
ATT_WK = WINDOW + ATT_TQ
ATT_RB = 128
S5_T = 128
MRG_TM = 512
RT_TT = 512
MOE_BLK = 512
DSP_TT = 1024

C_Q, C_KS, C_KW, C_KC, C_VC, C_VS, C_VW, C_U, C_G = 0, 512, 640, 768, 896, 1024, 1152, 1280, 1536
IN_W = 1664


def _cparams(*sem):
    return pltpu.CompilerParams(dimension_semantics=sem, vmem_limit_bytes=VMEM_LIMIT)


def _gelu(x):
    return 0.5 * x * (1.0 + jnp.tanh(math.sqrt(2.0 / math.pi) * (x + 0.044715 * (x * x * x))))


def _sigmoid(x):
    return 1.0 / (1.0 + jnp.exp(-x))


def _dot(a, b):
    return jnp.dot(a, b, preferred_element_type=F32)


def _dot_t(a, b):
    return lax.dot_general(a, b, (((1,), (1,)), ((), ())), preferred_element_type=F32)


def _split_bf16(x):
    hi = x.astype(BF16)
    lo = (x - hi.astype(F32)).astype(BF16)
    return hi, lo


def _pack_rows(x):
    k = x.shape[1] // 2
    hi = lax.bitcast_convert_type(x[:, :k].astype(jnp.bfloat16).astype(F32), jnp.int32)
    lo = lax.bitcast_convert_type(x[:, k:].astype(jnp.bfloat16).astype(F32), jnp.int32)
    return hi | lax.shift_right_logical(lo, 16)


def _unpack_rows(w):
    a = lax.bitcast_convert_type(w & jnp.int32(-65536), F32)
    b = lax.bitcast_convert_type(lax.shift_left(w, 16), F32)
    return a, b


def _rope_kernel(pos_ref, freq_ref, sign_ref, c_ref, s_ref):
    ang = pos_ref[...].astype(F32) * freq_ref[...]
    c_ref[...] = jnp.cos(ang)
    s_ref[...] = jnp.sin(ang) * sign_ref[...]


def _rope_tables(pos_col, inv_freq):
    n = pos_col.shape[0]
    lane = np.arange(LANES) % HEAD_DIM
    sel = np.zeros((LANES, ROPE_DIMS // 2), np.float32)
    for l in range(LANES):
        if lane[l] < ROPE_DIMS:
            sel[l, lane[l] % (ROPE_DIMS // 2)] = 1.0
    freq = (jnp.asarray(sel) * inv_freq[None, :]).sum(-1)[None, :]
    sign = np.where(lane < ROPE_DIMS // 2, -1.0, 1.0).astype(np.float32)[None, :]
    tm = 1024
    return pl.pallas_call(
        _rope_kernel,
        grid=(n // tm,),
        in_specs=[pl.BlockSpec((tm, 1), lambda i: (i, 0)),
                  pl.BlockSpec((1, LANES), lambda i: (0, 0)),
                  pl.BlockSpec((1, LANES), lambda i: (0, 0))],
        out_specs=[pl.BlockSpec((tm, LANES), lambda i: (i, 0))] * 2,
        out_shape=[jax.ShapeDtypeStruct((n, LANES), F32)] * 2,
        compiler_params=_cparams("parallel"),
        name="rope_tables",
    )(pos_col, freq, jnp.asarray(sign))


def _headnorm_rope(t, gain, c, s):
    lane = lax.broadcasted_iota(jnp.int32, t.shape, 1)
    lo = lane < HEAD_DIM
    sq = t * t
    s_lo = jnp.sum(jnp.where(lo, sq, 0.0), axis=-1, keepdims=True)
    s_hi = jnp.sum(jnp.where(lo, 0.0, sq), axis=-1, keepdims=True)
    ms = jnp.where(lo, s_lo, s_hi) * (1.0 / HEAD_DIM)
    tn = (t * lax.rsqrt(ms + RMS_EPS)) * gain
    half = ROPE_DIMS // 2
    partner = jnp.where((lane & (HEAD_DIM - 1)) < half,
                        pltpu.roll(tn, LANES - half, 1), pltpu.roll(tn, half, 1))
    return tn * c + partner * s


def _in_proj_kernel(x_ref, gain_ref, w_ref, c_ref, s_ref, qg_ref, kg_ref,
                    q_ref, ks_ref, kw_ref, kc_ref, vc_ref, vs_ref, vw_ref, u_ref, g_ref, raw_ref):
    x = x_ref[...]
    ms = jnp.mean(x * x, axis=-1, keepdims=True)
    h = (x * lax.rsqrt(ms + RMS_EPS)) * gain_ref[...]
    z = _dot(h.astype(BF16), w_ref[...])
    c = c_ref[...]
    s = s_ref[...]
    lane = lax.broadcasted_iota(jnp.int32, c.shape, 1)
    lo = lane < HEAD_DIM
    scale = HEAD_DIM ** -0.5 * math.log2(math.e)
    for j in range(NSA_HEADS // 2):
        r = _headnorm_rope(z[:, C_Q + LANES * j:C_Q + LANES * (j + 1)], qg_ref[...], c, s) * scale
        rr = pltpu.roll(r, HEAD_DIM, 1)
        if j // (NSA_GROUP // 2) == 0:
            even, odd = jnp.where(lo, r, 0.0), jnp.where(lo, rr, 0.0)
        else:
            even, odd = jnp.where(lo, 0.0, rr), jnp.where(lo, 0.0, r)
        q_ref[:, 2 * LANES * j:2 * LANES * j + LANES] = even.astype(BF16)
        q_ref[:, 2 * LANES * j + LANES:2 * LANES * (j + 1)] = odd.astype(BF16)
    ks_ref[...] = _headnorm_rope(z[:, C_KS:C_KS + LANES], kg_ref[...], c, s).astype(BF16)
    kw_ref[...] = _headnorm_rope(z[:, C_KW:C_KW + LANES], kg_ref[...], c, s).astype(BF16)
    half = CMP_BLOCK // 2
    for col, out in ((C_KC, kc_ref), (C_VC, vc_ref)):
        raw_ref[...] = z[:, col:col + LANES]
        for l in range(half):
            piece = raw_ref[pl.ds(l, x.shape[0] // half, stride=half), :]
            out[:, LANES * l:LANES * (l + 1)] = piece.astype(BF16)
    vs_ref[...] = z[:, C_VS:C_VS + LANES].astype(BF16)
    vw_ref[...] = z[:, C_VW:C_VW + LANES].astype(BF16)
    u_ref[:, pl.ds(pl.program_id(1), 1), :] = z[:, C_U:C_U + S5_WIDTH][:, None, :]
    g_ref[...] = _sigmoid(z[:, C_G:C_G + LANES])


def _in_proj(x2, gain, w1, ctab, stab, qg, kg, batch, seq):
    n = x2.shape[0]
    tm = IN_TM
    nt = seq // tm
    row = lambda i, b: (b * nt + i, 0)
    const = lambda i, b: (0, 0)
    tok = lambda w: pl.BlockSpec((tm, w), row)
    half = CMP_BLOCK // 2
    grouped = pl.BlockSpec((tm // half, half * KV_WIDTH), row)
    rows_bf16 = lambda r, w: jax.ShapeDtypeStruct((r, w), BF16)
    return pl.pallas_call(
        _in_proj_kernel,
        grid=(nt, batch),
        in_specs=[tok(D_MODEL), pl.BlockSpec((1, D_MODEL), const), pl.BlockSpec((D_MODEL, IN_W), const),
                  tok(LANES), tok(LANES), pl.BlockSpec((1, LANES), const), pl.BlockSpec((1, LANES), const)],
        out_specs=[tok(2 * NSA_WIDTH), tok(LANES), tok(LANES), grouped, grouped, tok(LANES), tok(LANES),
                   pl.BlockSpec((tm, batch, S5_WIDTH), lambda i, b: (i, 0, 0)), tok(LANES)],
        out_shape=[rows_bf16(n, 2 * NSA_WIDTH), rows_bf16(n, LANES), rows_bf16(n, LANES),
                   rows_bf16(n // half, half * KV_WIDTH), rows_bf16(n // half, half * KV_WIDTH),
                   rows_bf16(n, LANES), rows_bf16(n, LANES),
                   jax.ShapeDtypeStruct((seq, batch, S5_WIDTH), F32), jax.ShapeDtypeStruct((n, LANES), F32)],
        scratch_shapes=[pltpu.VMEM((tm, LANES), F32)],
        compiler_params=_cparams("parallel", "arbitrary"),
        name="in_proj",
    )(x2, gain, w1, ctab, stab, qg, kg)


def _compress_mlp(h_ref, w_ref, p_ref, w2_ref):
    w = w_ref[...]
    pre = _dot(h_ref[...], w)
    phi, plo = _split_bf16(p_ref[...])
    pb = _dot(phi, w) + _dot(plo, w)
    rows = pre.shape[0]
    out = None
    for h in range(NSA_KV_HEADS):
        top = pre[:, LANES * h:LANES * (h + 1)]
        bot = pre[:, LANES * (2 + h):LANES * (3 + h)]
        bias = pb[0:1, LANES * h:LANES * (h + 1)] + pb[1:2, LANES * (2 + h):LANES * (3 + h)]
        a = _gelu(top + pltpu.roll(bot, rows - 1, 0) + bias)
        o = _dot(a.astype(BF16), w2_ref[h])
        out = o if out is None else out + o
    return out


def _compress_kernel(hk_ref, hv_ref, wk_ref, wv_ref, pk_ref, pv_ref, w2k_ref, w2v_ref,
                     kg_ref, c_ref, s_ref, kc_ref, vc_ref):
    k = _compress_mlp(hk_ref, wk_ref, pk_ref, w2k_ref)
    kc_ref[...] = _headnorm_rope(k, kg_ref[...], c_ref[...], s_ref[...]).astype(BF16)
    vc_ref[...] = _compress_mlp(hv_ref, wv_ref, pv_ref, w2v_ref).astype(BF16)


def _compress(hk, hv, wk, wv, pk, pv, w2k, w2v, kg, ccmp, scmp, batch):
    rows = hk.shape[0] // batch
    width = hk.shape[1]
    const2 = lambda b: (0, 0)
    const3 = lambda b: (0, 0, 0)
    rowb = lambda w: pl.BlockSpec((rows, w), lambda b: (b, 0))
    return pl.pallas_call(
        _compress_kernel,
        grid=(batch,),
        in_specs=[rowb(width), rowb(width),
                  pl.BlockSpec(wk.shape, const2), pl.BlockSpec(wv.shape, const2),
                  pl.BlockSpec(pk.shape, const2), pl.BlockSpec(pv.shape, const2),
                  pl.BlockSpec(w2k.shape, const3), pl.BlockSpec(w2v.shape, const3),
                  pl.BlockSpec((1, LANES), const2), rowb(LANES), rowb(LANES)],
        out_specs=[rowb(LANES), rowb(LANES)],
        out_shape=[jax.ShapeDtypeStruct((hk.shape[0], LANES), BF16)] * 2,
        compiler_params=_cparams("parallel"),
        name="compress_kv",
    )(hk, hv, wk, wv, pk, pv, w2k, w2v, kg, ccmp, scmp)


def _compress_weights(pos_emb, w1, w2):
    half = CMP_BLOCK // 2
    eye = jnp.eye(NSA_KV_HEADS, dtype=F32)
    w1r = w1.reshape(2, half, HEAD_DIM, CMP_HIDDEN).astype(BF16)
    zero = jnp.zeros_like(w1r[0])
    wexp = jnp.concatenate(
        [jnp.stack([w1r[s] if p == h else zero for p in range(NSA_KV_HEADS)], axis=1)
         .reshape(half * KV_WIDTH, CMP_HIDDEN) for s in range(2) for h in range(NSA_KV_HEADS)], axis=1)
    pexp = jnp.broadcast_to(pos_emb.reshape(2, half, 1, HEAD_DIM), (2, half, NSA_KV_HEADS, HEAD_DIM))
    pexp = jnp.pad(pexp.reshape(2, half * KV_WIDTH), ((0, SUBLANES - 2), (0, 0)))
    w2e = jnp.einsum("cd,ph->pchd", w2, eye).reshape(NSA_KV_HEADS, CMP_HIDDEN, KV_WIDTH)
    return wexp, pexp, w2e.astype(BF16)


def _attn_kernel(q_ref, ks_ref, vs_ref, kw_ref, vw_ref, kc_ref, vc_ref, g_ref, et_ref, ovt_ref, oh_ref, wb_ref,
                 cb_ref, o_ref, qc_ref, s_ref, p_ref, sw_ref, pw_ref, m_ref, al_ref, oc_ref, as_ref, aw_ref,
                 *, n_cmp, n_slc, top_k):
    tq, tk, wk, rb, nh = ATT_TQ, ATT_TK, ATT_WK, ATT_RB, NSA_HEADS
    rows, grp_rows = nh * tq, NSA_GROUP * tq
    qk_rows = 2 * tq
    q0 = pl.program_id(1) * tq
    for a in range(nh):
        qc_ref[a * tq:(a + 1) * tq, 0:LANES] = q_ref[:, LANES * a:LANES * (a + 1)]
    q8 = qc_ref[:, 0:LANES]
    lane = lax.broadcasted_iota(jnp.int32, (tq, LANES), 1)
    t_row = q0 + lax.broadcasted_iota(jnp.int32, (tq, LANES), 0)

    s_c = _dot_t(q8, kc_ref[...]).reshape(nh, tq, LANES)
    mask_c = ((lane * CMP_STRIDE + (CMP_BLOCK - 1)) <= t_row) & (lane < n_cmp)
    sm = jnp.where(mask_c[None], s_c, NEG_BIG)
    p = jnp.where(mask_c[None], jnp.exp2(sm - jnp.max(sm, axis=-1, keepdims=True)), 0.0)
    l = jnp.sum(p, axis=-1, keepdims=True)
    p_c = p * jnp.where(l > 0.0, 1.0 / l, 0.0)
    oc_ref[...] = _dot(p_c.reshape(rows, LANES).astype(BF16), vc_ref[...])

    blk = lax.broadcasted_iota(jnp.int32, (n_slc, tq), 0)
    cur = (q0 + lax.broadcasted_iota(jnp.int32, (n_slc, tq), 1)) >> int(math.log2(SLC_BLOCK))
    forced = (blk == 0) | (blk == cur) | (blk == cur - 1)
    for h in range(NSA_KV_HEADS):
        ps = p_c[NSA_GROUP * h]
        for g in range(1, NSA_GROUP):
            ps = ps + p_c[NSA_GROUP * h + g]
        hi, lo = _split_bf16(ps)
        imp = (_dot_t(ovt_ref[...], hi) + _dot_t(ovt_ref[...], lo))[0:n_slc]
        imp = jnp.where(forced, jnp.inf, jnp.where(blk > cur, -jnp.inf, imp))
        rank = jnp.zeros((n_slc, tq), F32)
        for i in range(n_slc):
            row = imp[i:i + 1, :]
            rank = rank + jnp.where((row > imp) | ((row == imp) & (blk > i)), 1.0, 0.0)
        bias_t = jnp.where(rank < float(top_k), 0.0, NEG_BIG)
        bias = jnp.concatenate([bias_t, jnp.zeros((LANES - n_slc, tq), F32)], axis=0).T.astype(BF16)
        for g in range(NSA_GROUP):
            a = NSA_GROUP * h + g
            qc_ref[a * tq:(a + 1) * tq, LANES:2 * LANES] = bias

    def softmax_pass(s_buf, p_buf, width, causal_case, online):
        for r in range(rows // rb):
            r0 = r * rb
            s = s_buf[pl.ds(r0, rb), 0:width]
            if causal_case is not None:
                s = s + cb_ref[causal_case, pl.ds(r0 % tq, rb), :]
            m_new = jnp.broadcast_to(jnp.max(s, axis=1, keepdims=True), (rb, LANES))
            if online:
                m_old = m_ref[pl.ds(r0, rb), :]
                m_new = jnp.maximum(m_old, m_new)
                al_ref[pl.ds(r0, rb), :] = jnp.exp2(m_old - m_new)
                m_ref[pl.ds(r0, rb), :] = m_new
            pp = jnp.exp2(s - jnp.concatenate([m_new] * (width // LANES), axis=1))
            p_buf[pl.ds(r0, rb), 0:width] = pp.astype(BF16)

    def pv(acc_ref, p_buf, v, width, online):
        lane_v = lax.broadcasted_iota(jnp.int32, v.shape, 1)
        for h in range(NSA_KV_HEADS):
            rs = slice(h * grp_rows, (h + 1) * grp_rows)
            own = (lane_v < HEAD_DIM) if h == 0 else (lane_v >= HEAD_DIM)
            upd = _dot(p_buf[rs, 0:width], jnp.where(own, v, jnp.ones_like(v)))
            acc_ref[rs, :] = (al_ref[rs, :] * acc_ref[rs, :] + upd) if online else upd

    m_ref[...] = jnp.full(m_ref.shape, 0.1 * NEG_BIG, F32)
    as_ref[...] = jnp.zeros(as_ref.shape, F32)

    def slc_chunk(c, diagonal):
        k0 = c * tk
        s_buf, p_buf = s_ref.at[c % 2], p_ref.at[c % 2]
        kcat = jnp.concatenate([ks_ref[pl.ds(k0, tk), :], et_ref[pl.ds(k0, tk), :]], axis=1)
        for g in range(rows // qk_rows):
            rs = slice(g * qk_rows, (g + 1) * qk_rows)
            s_buf[rs, :] = _dot_t(qc_ref[rs, :], kcat)
        softmax_pass(s_buf, p_buf, tk, ((q0 - k0) >> int(math.log2(tq))) if diagonal else None, True)
        pv(as_ref, p_buf, vs_ref[pl.ds(k0, tk), :], tk, True)

    def window_and_output():
        w0 = pl.multiple_of(jnp.maximum(q0 - WINDOW, 0), LANES)
        w_case = jnp.minimum(q0 >> int(math.log2(tq)), WINDOW // tq)
        kcat_w = jnp.concatenate([kw_ref[pl.ds(w0, wk), :], wb_ref[w_case]], axis=1)
        for g in range(rows // qk_rows):
            rs = slice(g * qk_rows, (g + 1) * qk_rows)
            sw_ref[rs, :] = _dot_t(jnp.concatenate([qc_ref[rs, 0:LANES], oh_ref[rs, :]], axis=1), kcat_w)
        softmax_pass(sw_ref, pw_ref, wk, None, False)
        pv(aw_ref, pw_ref, vw_ref[pl.ds(w0, wk), :], wk, False)

        lo_half = lane < HEAD_DIM
        for j in range(nh // 2):
            comb = []
            for a in (2 * j, 2 * j + 1):
                rs = slice(a * tq, (a + 1) * tq)
                o_s, o_w = as_ref[rs, :], aw_ref[rs, :]
                comb.append(g_ref[:, 3 * a:3 * a + 1] * oc_ref[rs, :]
                            + g_ref[:, 3 * a + 1:3 * a + 2] * (o_s * (1.0 / pltpu.roll(o_s, HEAD_DIM, 1)))
                            + g_ref[:, 3 * a + 2:3 * a + 3] * (o_w * (1.0 / pltpu.roll(o_w, HEAD_DIM, 1))))
            even, odd = comb
            if j // (NSA_GROUP // 2) == 0:
                pair = jnp.where(lo_half, even, pltpu.roll(odd, HEAD_DIM, 1))
            else:
                pair = jnp.where(lo_half, pltpu.roll(even, HEAD_DIM, 1), odd)
            o_ref[:, LANES * j:LANES * (j + 1)] = pair.astype(BF16)

    c_last = q0 >> int(math.log2(tk))
    for n_full in range(ks_ref.shape[0] // tk):
        @pl.when(c_last == n_full)
        def _():
            for c in range(n_full):
                slc_chunk(c, False)
            slc_chunk(n_full, True)
            window_and_output()


def _pack_w_in(w):
    sizes = (NSA_WIDTH,) + (KV_WIDTH,) * 6 + (3 * NSA_HEADS, S5_WIDTH, D_MODEL, D_MODEL)
    o = np.cumsum((0,) + sizes)
    q, kc, vc, ks, vs, kw, vw, g, u, ga, gs = [w[:, o[i]:o[i + 1]] for i in range(len(sizes))]
    g = jnp.pad(g, ((0, 0), (0, LANES - 3 * NSA_HEADS)))
    w1 = jnp.concatenate([q, ks, kw, kc, vc, vs, vw, u, g], axis=1).astype(BF16)
    return w1, ga.astype(BF16), gs.astype(BF16)


def _nsa_branch(x2, gain, w1, ctab, stab, ccmp, scmp, q_gain, k_gain, cmp_k, cmp_v, batch, seq):
    qg = jnp.tile(q_gain, 2)[None, :]
    kg = jnp.tile(k_gain, 2)[None, :]
    q, ks, kw, hk, hv, vs, vw, u_tm, gates = _in_proj(x2, gain, w1, ctab, stab, qg, kg, batch, seq)
    wk, pk, w2k = _compress_weights(*cmp_k)
    wv, pv, w2v = _compress_weights(*cmp_v)
    kc, vc = _compress(hk, hv, wk, wv, pk, pv, w2k, w2v, kg, ccmp, scmp, batch)
    o_a = _attention(q, ks, vs, kw, vw, kc, vc, gates, batch, seq)
    return o_a, u_tm


def _cmp_rope_tables(ctab, stab, batch, seq):
    def pick(t):
        t = t.reshape(batch, seq, LANES)[:, CMP_BLOCK - 1::CMP_STRIDE]
        t = jnp.pad(t, ((0, 0), (0, seq // CMP_STRIDE - t.shape[1]), (0, 0)))
        return t.reshape(batch * (seq // CMP_STRIDE), LANES)
    return pick(ctab), pick(stab)


def _s5_disc_kernel(are_ref, aim_ref, ldt_ref, lre_ref, lim_ref, fre_ref, fim_ref):
    lam_re = jnp.minimum(are_ref[...], -1e-4)
    lam_im = aim_ref[...]
    dt = jnp.exp(ldt_ref[...])
    mag = jnp.exp(lam_re * dt)
    ang = lam_im * dt
    lb_re = mag * jnp.cos(ang)
    lb_im = mag * jnp.sin(ang)
    den = lam_re * lam_re + lam_im * lam_im
    lre_ref[...] = lb_re
    lim_ref[...] = lb_im
    fre_ref[...] = ((lb_re - 1.0) * lam_re + lb_im * lam_im) / den
    fim_ref[...] = (lb_im * lam_re - (lb_re - 1.0) * lam_im) / den


def _s5_weights(a_re, a_im, log_dt, b_re, b_im, c_re, c_im):
    shp = jax.ShapeDtypeStruct((S5_GROUPS, S5_STATE), F32)
    lb_re, lb_im, f_re, f_im = pl.pallas_call(
        _s5_disc_kernel, out_shape=[shp] * 4, name="s5_discretise",
    )(a_re, a_im, log_dt[:, None])
    bb_re = f_re[..., None] * b_re - f_im[..., None] * b_im
    bb_im = f_re[..., None] * b_im + f_im[..., None] * b_re
    eye = jnp.eye(S5_GROUPS, dtype=F32)

    def in_map(bb):
        return jnp.einsum("gpc,gh->gchp", bb, eye).reshape(S5_WIDTH, S5_NSTATE)

    def out_map(cc):
        return jnp.einsum("gcp,gh->hpgc", cc, eye).reshape(S5_NSTATE, S5_WIDTH)

    w_b = jnp.concatenate([in_map(bb_re), in_map(bb_im)], axis=1).astype(BF16)
    w_c = jnp.concatenate([out_map(c_re), -out_map(c_im)], axis=0).astype(BF16)
    return lb_re.reshape(1, S5_NSTATE), lb_im.reshape(1, S5_NSTATE), w_b, w_c


def _s5_kernel(u_ref, wb_ref, lre_ref, lim_ref, wc_ref, d_ref, wg_ref, o_ref, x_ref, st_ref, *, batch):
    ns = S5_NSTATE

    @pl.when(pl.program_id(0) == 0)
    def _():
        st_ref[...] = jnp.zeros(st_ref.shape, F32)

    u = u_ref[...].reshape(u_ref.shape[0] * batch, S5_WIDTH)
    x_ref[...] = _dot(u.astype(BF16), wb_ref[...])
    lre = jnp.broadcast_to(lre_ref[...], (batch, ns))
    lim = jnp.broadcast_to(lim_ref[...], (batch, ns))

    def step(t, carry):
        xr, xi = carry
        r0 = pl.multiple_of(t * batch, batch)
        nr = lre * xr - lim * xi + x_ref[pl.ds(r0, batch), 0:ns]
        ni = lre * xi + lim * xr + x_ref[pl.ds(r0, batch), ns:2 * ns]
        x_ref[pl.ds(r0, batch), 0:ns] = nr
        x_ref[pl.ds(r0, batch), ns:2 * ns] = ni
        return nr, ni

    steps = u.shape[0] // batch
    xr, xi = lax.fori_loop(0, steps, step, (st_ref[:, 0:ns], st_ref[:, ns:2 * ns]), unroll=4)
    st_ref[:, 0:ns] = xr
    st_ref[:, ns:2 * ns] = xi
    y = _dot(x_ref[...].astype(BF16), wc_ref[...]) + d_ref[...] * u
    z = _dot(_gelu(y).astype(BF16), wg_ref[...])
    o = z[:, :S5_WIDTH] * _sigmoid(z[:, S5_WIDTH:])
    for c in range(S5_WIDTH // LANES):
        o_ref[c] = o[:, c * LANES:(c + 1) * LANES]


def _s5(u_tm, lre, lim, w_b, w_c, d_skip, w_glu, batch, seq):
    rows = S5_T * batch
    const = lambda t: (0, 0)
    return pl.pallas_call(
        functools.partial(_s5_kernel, batch=batch),
        grid=(seq // S5_T,),
        in_specs=[pl.BlockSpec((S5_T, batch, S5_WIDTH), lambda t: (t, 0, 0)),
                  pl.BlockSpec(w_b.shape, const), pl.BlockSpec(lre.shape, const), pl.BlockSpec(lim.shape, const),
                  pl.BlockSpec(w_c.shape, const), pl.BlockSpec((1, S5_WIDTH), const),
                  pl.BlockSpec(w_glu.shape, const)],
        out_specs=pl.BlockSpec((S5_WIDTH // LANES, rows, LANES), lambda t: (0, t, 0)),
        out_shape=jax.ShapeDtypeStruct((S5_WIDTH // LANES, seq * batch, LANES), F32),
        scratch_shapes=[pltpu.VMEM((rows, 2 * S5_NSTATE), F32), pltpu.VMEM((batch, 2 * S5_NSTATE), F32)],
        compiler_params=_cparams("arbitrary"),
        name="s5_scan",
    )(u_tm, w_b, lre, lim, w_c, d_skip, w_glu)


RT_GROUP_ROW = 0
RT_EXPERT_ROW = SUBLANES


def _rmsnorm(x, gain):
    return (x * lax.rsqrt(jnp.mean(x * x, axis=-1, keepdims=True) + RMS_EPS)) * gain


def _merge_kernel(x_ref, g1_ref, wga_ref, wgs_ref, oa_ref, ob_ref, pa_ref, ps_ref, wo_ref, g2_ref,
                  wrh_ref, wrl_ref, rb_ref, x1_ref, h2_ref, lg_ref):
    x = x_ref[...]
    h = _rmsnorm(x, g1_ref[...]).astype(BF16)
    gate_a = _sigmoid(_dot(h, wga_ref[...]))
    gate_s = _sigmoid(_dot(h, wgs_ref[...]))
    rows_b = pl.ds(pl.program_id(1), x.shape[0], stride=ob_ref.shape[1] // x.shape[0])
    o_b = jnp.concatenate([ob_ref[c, rows_b, :] for c in range(S5_WIDTH // LANES)], axis=1)
    merged = gate_a * _dot(oa_ref[...], pa_ref[...]) + gate_s * _dot(o_b.astype(BF16), ps_ref[...])
    x1 = x + _dot(merged.astype(BF16), wo_ref[...])
    x1_ref[...] = x1
    h2 = _rmsnorm(x1, g2_ref[...])
    h2_ref[...] = _pack_rows(h2)
    hi, lo = _split_bf16(h2)
    lg_ref[...] = (_dot_t(wrh_ref[...], hi) + _dot_t(wrh_ref[...], lo) + _dot_t(wrl_ref[...], hi)
                   + rb_ref[...])


def _router_weights(w_rg, b_rg, w_re, b_re):
    def rows(group_part, expert_part):
        return jnp.concatenate([
            group_part, jnp.zeros((RT_EXPERT_ROW - N_EXPERT_GROUPS,) + group_part.shape[1:], F32),
            expert_part, jnp.zeros((LANES - RT_EXPERT_ROW - N_EXPERTS,) + group_part.shape[1:], F32)], axis=0)

    hi, lo = _split_bf16(rows(w_rg.T, w_re.T))
    return hi, lo, rows(b_rg[:, None], b_re[:, None])


def _merge(x2, g1, w_ga, w_gs, o_a, ob_tm, p_a, p_s, w_o, g2, wrh, wrl, rb, batch, seq):
    n = x2.shape[0]
    tm = MRG_TM
    nt = seq // tm
    row = lambda i, b: (b * nt + i, 0)
    const = lambda i, b: (0, 0)
    tok = lambda w: pl.BlockSpec((tm, w), row)
    full = lambda a: pl.BlockSpec(a.shape, const)
    return pl.pallas_call(
        _merge_kernel,
        grid=(nt, batch),
        in_specs=[tok(D_MODEL), full(g1), full(w_ga), full(w_gs), tok(NSA_WIDTH),
                  pl.BlockSpec((S5_WIDTH // LANES, tm * batch, LANES), lambda i, b: (0, i, 0)),
                  full(p_a), full(p_s), full(w_o),
                  full(g2), full(wrh), full(wrl), full(rb)],
        out_specs=[tok(D_MODEL), tok(D_MODEL // 2), pl.BlockSpec((LANES, tm), lambda i, b: (0, b * nt + i))],
        out_shape=[jax.ShapeDtypeStruct((n, D_MODEL), F32), jax.ShapeDtypeStruct((n, D_MODEL // 2), jnp.int32),
                   jax.ShapeDtypeStruct((LANES, n), F32)],
        compiler_params=_cparams("parallel", "parallel"),
        name="merge_out_proj",
    )(x2, g1, w_ga, w_gs, o_a, ob_tm, p_a, p_s, w_o, g2, wrh, wrl, rb)


def _first_argmax(v, rows):
    vmax = jnp.max(v, axis=0, keepdims=True)
    idx = jnp.min(jnp.where(v == vmax, rows, SUBLANES), axis=0, keepdims=True)
    return vmax, idx


def _route_kernel(lg_ref, tri_ref, e_ref, w_ref, r_ref, cnt_ref, carry_ref):
    tt = lg_ref.shape[1]

    @pl.when(pl.program_id(0) == 0)
    def _():
        carry_ref[...] = jnp.zeros(carry_ref.shape, F32)

    rows = lax.broadcasted_iota(jnp.int32, (SUBLANES, tt), 0)
    gl = jnp.where(rows < N_EXPERT_GROUPS, lg_ref[RT_GROUP_ROW:RT_GROUP_ROW + SUBLANES, :], -jnp.inf)
    gmax, g_sel = _first_argmax(gl, rows)
    g_w = 1.0 / jnp.sum(jnp.exp(gl - gmax), axis=0, keepdims=True)

    el = jnp.zeros((EXPERTS_PER_GROUP, tt), F32)
    for g in range(N_EXPERT_GROUPS):
        r0 = RT_EXPERT_ROW + EXPERTS_PER_GROUP * g
        el = jnp.where(g_sel == g, lg_ref[r0:r0 + EXPERTS_PER_GROUP, :], el)
    ex = jnp.exp(el - jnp.max(el, axis=0, keepdims=True))
    p = ex / jnp.sum(ex, axis=0, keepdims=True)
    p1, i1 = _first_argmax(p, rows)
    p2, i2 = _first_argmax(jnp.where(rows == i1, -1.0, p), rows)
    e1 = g_sel * EXPERTS_PER_GROUP + i1
    e2 = g_sel * EXPERTS_PER_GROUP + i2
    scale = g_w / (p1 + p2)

    erow = lax.broadcasted_iota(jnp.int32, (N_EXPERTS, tt), 0)
    oh1 = jnp.where(erow == e1, 1.0, 0.0)
    oh2 = jnp.where(erow == e2, 1.0, 0.0)
    both = oh1 + oh2
    before = carry_ref[:, 0:1] + _dot(both.astype(BF16), tri_ref[...])
    carry_ref[...] = carry_ref[...] + jnp.sum(both, axis=1, keepdims=True)
    cnt_ref[...] = carry_ref[...]

    zi = jnp.zeros((SUBLANES - 2, tt), jnp.int32)
    e_ref[...] = jnp.concatenate([e1, e2, zi], axis=0)
    w_ref[...] = jnp.concatenate([p1 * scale, p2 * scale, zi.astype(F32)], axis=0)
    r_ref[...] = jnp.concatenate([jnp.sum(oh1 * before, axis=0, keepdims=True),
                                  jnp.sum(oh2 * before, axis=0, keepdims=True),
                                  zi.astype(F32)], axis=0).astype(jnp.int32)


def _route(logits_t):
    n = logits_t.shape[1]
    tt = RT_TT
    tri = jnp.asarray(np.triu(np.ones((tt, tt), np.float32), 1), BF16)
    blk = pl.BlockSpec((SUBLANES, tt), lambda i: (0, i))
    return pl.pallas_call(
        _route_kernel,
        grid=(n // tt,),
        in_specs=[pl.BlockSpec((LANES, tt), lambda i: (0, i)), pl.BlockSpec((tt, tt), lambda i: (0, 0))],
        out_specs=[blk, blk, blk, pl.BlockSpec((N_EXPERTS, LANES), lambda i: (0, 0))],
        out_shape=[jax.ShapeDtypeStruct((SUBLANES, n), jnp.int32), jax.ShapeDtypeStruct((SUBLANES, n), F32),
                   jax.ShapeDtypeStruct((SUBLANES, n), jnp.int32), jax.ShapeDtypeStruct((N_EXPERTS, LANES), F32)],
        scratch_shapes=[pltpu.VMEM((N_EXPERTS, LANES), F32)],
        compiler_params=_cparams("arbitrary"),
        name="moe_route",
    )(logits_t, tri)


def _expert_kernel(be_ref, nb_ref, x_ref, wg_ref, wu_ref, wd_ref, o_ref, wgb_ref, wub_ref, wdb_ref):
    i = pl.program_id(0)
    half = D_MODEL // 2

    @pl.when(i < nb_ref[0])
    def _():
        @pl.when((i == 0) | (be_ref[i] != be_ref[jnp.maximum(i - 1, 0)]))
        def _():
            wgb_ref[...] = wg_ref[...].astype(BF16)
            wub_ref[...] = wu_ref[...].astype(BF16)
            wdb_ref[...] = wd_ref[...].astype(BF16)

        xa, xb = _unpack_rows(x_ref[...])
        xa, xb = xa.astype(BF16), xb.astype(BF16)
        g = _dot(xa, wgb_ref[0:half, :]) + _dot(xb, wgb_ref[half:, :])
        u = _dot(xa, wub_ref[0:half, :]) + _dot(xb, wub_ref[half:, :])
        o_ref[...] = _pack_rows(_dot(((g * _sigmoid(g)) * u).astype(BF16), wdb_ref[...]))

    @pl.when(i >= nb_ref[0])
    def _():
        o_ref[...] = jnp.zeros(o_ref.shape, o_ref.dtype)


def _experts(blk_expert, n_used, buf, w_gate, w_up, w_down, layer):
    cap, width = buf.shape
    blk = MOE_BLK
    row = lambda i, be, nb: (jnp.minimum(i, nb[0] - 1), 0)
    wmap = lambda i, be, nb: (layer, be[jnp.minimum(i, nb[0] - 1)], 0, 0)
    return pl.pallas_call(
        _expert_kernel,
        grid_spec=pltpu.PrefetchScalarGridSpec(
            num_scalar_prefetch=2,
            grid=(cap // blk,),
            in_specs=[pl.BlockSpec((blk, width), row),
                      pl.BlockSpec((None, None, D_MODEL, EXPERT_FF), wmap),
                      pl.BlockSpec((None, None, D_MODEL, EXPERT_FF), wmap),
                      pl.BlockSpec((None, None, EXPERT_FF, D_MODEL), wmap)],
            out_specs=pl.BlockSpec((blk, width), lambda i, be, nb: (i, 0)),
            scratch_shapes=[pltpu.VMEM((D_MODEL, EXPERT_FF), BF16), pltpu.VMEM((D_MODEL, EXPERT_FF), BF16),
                            pltpu.VMEM((EXPERT_FF, D_MODEL), BF16)],
        ),
        out_shape=jax.ShapeDtypeStruct((cap, width), buf.dtype),
        compiler_params=_cparams("arbitrary"),
        name="moe_experts",
    )(blk_expert, n_used, buf, w_gate, w_up, w_down)


SC_CH = 64


def _sc_mesh():
    return plsc.VectorSubcoreMesh(core_axis_name="c", subcore_axis_name="s")


def _sc_worker(info):
    return lax.axis_index("s") * info.num_cores + lax.axis_index("c")


def _sc_gather_rows(table, idx, rows=None):
    info = plsc.get_sparse_core_info()
    lanes, workers = info.num_lanes, info.num_cores * info.num_subcores
    invert = rows is not None
    rows = rows if invert else idx.shape[0]
    width = table.shape[1]
    per_w = rows // workers
    n_chunks = per_w // SC_CH
    assert per_w % (2 * SC_CH) == 0
    scan = 2048
    n_tok = table.shape[0]

    def body(table_hbm, idx_hbm, out_hbm, idx_v, rows_a, rows_b, sem_a, sem_b, *scan_v):
        base = _sc_worker(info) * per_w
        if invert:
            lane = lax.iota(jnp.int32, lanes)

            @pl.loop(0, per_w // lanes)
            def _(j):
                idx_v[pl.ds(j * lanes, lanes)] = (base + j * lanes + lane) & (n_tok - 1)

            @pl.loop(0, idx.shape[0] // scan)
            def _(c):
                pltpu.sync_copy(idx_hbm.at[pl.ds(c * scan, scan)], scan_v[0])

                @plsc.parallel_loop(0, scan // lanes, unroll=8)
                def _(j):
                    rel = scan_v[0][pl.ds(j * lanes, lanes)] - base
                    mine = (rel >= 0) & (rel < per_w)
                    token = (c * scan + j * lanes + lane) & (n_tok - 1)
                    plsc.store_scatter(idx_v, [jnp.where(mine, rel, 0)], token, mask=mine)
        else:
            pltpu.sync_copy(idx_hbm.at[pl.ds(base, per_w)], idx_v)

        def gather(j, buf, sem):
            return pltpu.make_async_copy(table_hbm.at[idx_v.at[pl.ds(j * SC_CH, SC_CH)]], buf, sem)

        def write(j, buf):
            pltpu.sync_copy(buf, out_hbm.at[pl.ds(base + j * SC_CH, SC_CH)])

        gather(0, rows_a, sem_a).start()

        @pl.loop(0, n_chunks, step=2)
        def _(j):
            gather(j + 1, rows_b, sem_b).start()
            gather(j, rows_a, sem_a).wait()
            write(j, rows_a)

            @pl.when(j + 2 < n_chunks)
            def _():
                gather(j + 2, rows_a, sem_a).start()

            gather(j + 1, rows_b, sem_b).wait()
            write(j + 1, rows_b)

    chunk_buf = pltpu.VMEM((SC_CH, width), table.dtype)
    scratch = [pltpu.VMEM((per_w,), jnp.int32), chunk_buf, chunk_buf,
               pltpu.SemaphoreType.DMA, pltpu.SemaphoreType.DMA]
    return pl.kernel(
        body, out_type=jax.ShapeDtypeStruct((rows, width), table.dtype), mesh=_sc_mesh(),
        scratch_types=scratch + ([pltpu.VMEM((scan,), jnp.int32)] if invert else []),
        compiler_params=pltpu.CompilerParams(needs_layout_passes=False) if invert else None,
        name="sc_dispatch_rows" if invert else "sc_gather_rows",
    )(table, idx)


def _blend_kernel(x1_ref, w_ref, g0_ref, g1_ref, o_ref):
    half = x1_ref.shape[1] // 2
    a0, b0 = _unpack_rows(g0_ref[...])
    a1, b1 = _unpack_rows(g1_ref[...])
    w0, w1 = w_ref[:, 0:1], w_ref[:, 1:2]
    o_ref[:, 0:half] = x1_ref[:, 0:half] + w0 * a0 + w1 * a1
    o_ref[:, half:] = x1_ref[:, half:] + w0 * b0 + w1 * b1


def _blend(x1, w_col, gathered):
    n = x1.shape[0]
    tt = DSP_TT
    nt = n // tt
    width = gathered.shape[1]
    return pl.pallas_call(
        _blend_kernel,
        grid=(nt,),
        in_specs=[pl.BlockSpec((tt, D_MODEL), lambda i: (i, 0)),
                  pl.BlockSpec((tt, EXPERT_TOP_K), lambda i: (i, 0)),
                  pl.BlockSpec((tt, width), lambda i: (i, 0)),
                  pl.BlockSpec((tt, width), lambda i: (nt + i, 0))],
        out_specs=pl.BlockSpec((tt, D_MODEL), lambda i: (i, 0)),
        out_shape=jax.ShapeDtypeStruct((n, D_MODEL), F32),
        compiler_params=_cparams("parallel"),
        name="moe_blend",
    )(x1, w_col, gathered, gathered)


def _moe(x1, h2, logits_t, w_gate, w_up, w_down, layer):
    n = x1.shape[0]
    e_idx, wts, rank, counts = _route(logits_t)
    counts = counts[:, 0].astype(jnp.int32)
    padded = (counts + MOE_BLK - 1) // MOE_BLK * MOE_BLK
    pends = jnp.cumsum(padded)
    pstarts = pends - padded
    experts = jnp.arange(N_EXPERTS, dtype=jnp.int32)
    start_of = jnp.sum(jnp.where(e_idx[None, :EXPERT_TOP_K] == experts[:, None, None],
                                 pstarts[:, None, None], 0), axis=0)
    dest = start_of + rank[:EXPERT_TOP_K]
    cap = n * EXPERT_TOP_K + N_EXPERTS * MOE_BLK
    n_blk = cap // MOE_BLK
    blk_row = jnp.arange(n_blk, dtype=jnp.int32) * MOE_BLK
    blk_expert = jnp.minimum(jnp.sum((pends[None, :] <= blk_row[:, None]).astype(jnp.int32), axis=1),
                             N_EXPERTS - 1)
    n_used = (pends[-1:] // MOE_BLK).astype(jnp.int32)
    dest_flat = dest.reshape(EXPERT_TOP_K * n)
    buf = _sc_gather_rows(h2, dest_flat, rows=cap)
    out_buf = _experts(blk_expert, n_used, buf, w_gate, w_up, w_down, layer)
    return _blend(x1, wts[:EXPERT_TOP_K].T, _sc_gather_rows(out_buf, dest_flat))


def kernel(x, positions, norm_mix, norm_ffn, w_in, q_gain, k_gain, cmp_k_pos, cmp_k_w1, cmp_k_w2, cmp_v_pos,
           cmp_v_w1, cmp_v_w2, s5_a_re, s5_a_im, s5_log_dt, s5_b_re, s5_b_im, s5_c_re, s5_c_im, s5_d, s5_glu,
           proj_nsa, proj_s5, w_out, router_group, router_group_bias, router_expert, router_expert_bias,
           expert_gate, expert_up, expert_down):
    batch, seq, _ = x.shape
    depth = w_in.shape[0]
    n = batch * seq
    inv_freq = ROPE_THETA ** (-jnp.arange(0, ROPE_DIMS, 2, dtype=jnp.float32) / ROPE_DIMS)
    ctab, stab = _rope_tables(positions.reshape(n, 1), inv_freq)
    ccmp, scmp = _cmp_rope_tables(ctab, stab, batch, seq)
    x2 = x.reshape(n, D_MODEL)
    for l in range(depth):
        w1, w_ga, w_gs = _pack_w_in(w_in[l])
        o_a, u_tm = _nsa_branch(x2, norm_mix[l][None], w1, ctab, stab, ccmp, scmp, q_gain[l], k_gain[l],
                                (cmp_k_pos[l], cmp_k_w1[l], cmp_k_w2[l]),
                                (cmp_v_pos[l], cmp_v_w1[l], cmp_v_w2[l]), batch, seq)
        lre, lim, w_b, w_c = _s5_weights(s5_a_re[l], s5_a_im[l], s5_log_dt[l], s5_b_re[l], s5_b_im[l],
                                         s5_c_re[l], s5_c_im[l])
        ob_tm = _s5(u_tm, lre, lim, w_b, w_c, s5_d[l][None],
                    s5_glu[l].astype(BF16), batch, seq)
        wrh, wrl, rb = _router_weights(router_group[l], router_group_bias[l], router_expert[l],
                                       router_expert_bias[l])
        x1, h2, logits_t = _merge(x2, norm_mix[l][None], w_ga, w_gs, o_a, ob_tm, proj_nsa[l].astype(BF16),
                                  proj_s5[l].astype(BF16), w_out[l].astype(BF16), norm_ffn[l][None],
                                  wrh, wrl, rb, batch, seq)
        x2 = _moe(x1, h2, logits_t, expert_gate, expert_up, expert_down, l)
    return x2.reshape(batch, seq, D_MODEL)


def _attn_tables(seq):
    n_cmp = (seq - CMP_BLOCK) // CMP_STRIDE + 1
    n_slc = seq // SLC_BLOCK
    key_blk = np.arange(seq) // SLC_BLOCK
    expand_t = (key_blk[:, None] == np.arange(LANES)[None, :]).astype(np.float32)
    cmp_start = np.arange(LANES) * CMP_STRIDE
    slc_start = np.arange(LANES) * SLC_BLOCK
    overlap_t = ((cmp_start[None, :] < slc_start[:, None] + SLC_BLOCK)
                 & (cmp_start[None, :] + CMP_BLOCK > slc_start[:, None])
                 & (np.arange(LANES)[None, :] < n_cmp) & (np.arange(LANES)[:, None] < n_slc))
    return (jnp.asarray(expand_t, BF16), jnp.asarray(overlap_t.astype(np.float32), BF16), n_cmp, n_slc)


def _attn_mask_tables():
    tq, tk, wk = ATT_TQ, ATT_TK, ATT_WK
    ql = np.arange(tq)
    onehot = np.tile(np.eye(tq, dtype=np.float32), (NSA_HEADS, 1))
    k = np.arange(wk)[:, None]
    band = []
    for c in range(WINDOW // tq + 1):
        if c < WINDOW // tq:
            keep = k <= c * tq + ql[None, :]
        else:
            keep = (k > ql[None, :]) & (k <= ql[None, :] + WINDOW)
        band.append(np.where(keep, 0.0, NEG_BIG))
    kl = np.arange(tk)[None, :]
    causal = [np.where(kl <= c * tq + ql[:, None], 0.0, NEG_BIG) for c in range(tk // tq)]
    return (jnp.asarray(onehot, BF16), jnp.asarray(np.stack(band), BF16),
            jnp.asarray(np.stack(causal), F32))


def _attention(q, ks, vs, kw, vw, kc, vc, gates, batch, seq):
    expand_t, overlap_t, n_cmp, n_slc = _attn_tables(seq)
    onehot, band, causal = _attn_mask_tables()
    tq = ATT_TQ
    nq = seq // tq
    rows = NSA_HEADS * tq
    tile = lambda w: pl.BlockSpec((tq, w), lambda b, i: (b * nq + i, 0))
    per_b = lambda r: pl.BlockSpec((r, LANES), lambda b, i: (b, 0))
    f32_rows = pltpu.VMEM((rows, LANES), F32)
    return pl.pallas_call(
        functools.partial(_attn_kernel, n_cmp=n_cmp, n_slc=n_slc, top_k=min(SLC_TOPK, n_slc)),
        grid=(batch, nq),
        in_specs=[tile(2 * NSA_WIDTH), per_b(seq), per_b(seq), per_b(seq), per_b(seq),
                  per_b(LANES), per_b(LANES), tile(LANES),
                  pl.BlockSpec(expand_t.shape, lambda b, i: (0, 0)),
                  pl.BlockSpec(overlap_t.shape, lambda b, i: (0, 0)),
                  pl.BlockSpec(onehot.shape, lambda b, i: (0, 0)),
                  pl.BlockSpec(band.shape, lambda b, i: (0, 0, 0)),
                  pl.BlockSpec(causal.shape, lambda b, i: (0, 0, 0))],
        out_specs=tile(NSA_WIDTH),
        out_shape=jax.ShapeDtypeStruct((batch * seq, NSA_WIDTH), BF16),
        scratch_shapes=[pltpu.VMEM((rows, 2 * LANES), BF16),
                        pltpu.VMEM((2, rows, ATT_TK), F32), pltpu.VMEM((2, rows, ATT_TK), BF16),
                        pltpu.VMEM((rows, ATT_WK), F32), pltpu.VMEM((rows, ATT_WK), BF16),
                        f32_rows, f32_rows, f32_rows, f32_rows, f32_rows],
        compiler_params=_cparams("parallel", "parallel"),
        name="nsa_attention",
    )(q, ks, vs, kw, vw, kc, vc, gates, expand_t, overlap_t, onehot, band, causal)
```

```python
import functools
import math

import jax
import jax.numpy as jnp
import numpy as np
from jax import lax
from jax.experimental import pallas as pl
from jax.experimental.pallas import tpu as pltpu
from jax.experimental.pallas import tpu_sc as plsc

F32 = jnp.float32
BF16 = jnp.bfloat16

D_MODEL = 1024
NSA_HEADS = 8
NSA_KV_HEADS = 2
NSA_GROUP = NSA_HEADS // NSA_KV_HEADS
HEAD_DIM = 64
ROPE_DIMS = HEAD_DIM // 4
ROPE_THETA = 500000.0
CMP_BLOCK = 32
CMP_STRIDE = 16
CMP_HIDDEN = 2 * HEAD_DIM
SLC_BLOCK = 64
SLC_TOPK = 16
WINDOW = 512
NSA_WIDTH = NSA_HEADS * HEAD_DIM
KV_WIDTH = NSA_KV_HEADS * HEAD_DIM
S5_WIDTH = 256
S5_GROUP_CH = 16
S5_GROUPS = S5_WIDTH // S5_GROUP_CH
S5_STATE = 64
S5_NSTATE = S5_GROUPS * S5_STATE
N_EXPERT_GROUPS = 4
EXPERTS_PER_GROUP = 8
N_EXPERTS = N_EXPERT_GROUPS * EXPERTS_PER_GROUP
EXPERT_TOP_K = 2
EXPERT_FF = 512
RMS_EPS = 1e-6
NEG_BIG = -1e30

LANES = 128
SUBLANES = 8
VMEM_LIMIT = 56 * 1024 * 1024

IN_TM = 512
ATT_TQ = 128
ATT_TK = 512
ATT_WK = WINDOW + ATT_TQ
ATT_RB = 128
S5_T = 128
MRG_TM = 512
RT_TT = 512
MOE_BLK = 512
DSP_TT = 1024

C_Q, C_KS, C_KW, C_KC, C_VC, C_VS, C_VW, C_U, C_G = 0, 512, 640, 768, 896, 1024, 1152, 1280, 1536
IN_W = 1664


def _cparams(*sem):
    return pltpu.CompilerParams(dimension_semantics=sem, vmem_limit_bytes=VMEM_LIMIT)


def _gelu(x):
    return 0.5 * x * (1.0 + jnp.tanh(math.sqrt(2.0 / math.pi) * (x + 0.044715 * (x * x * x))))


def _sigmoid(x):
    return 1.0 / (1.0 + jnp.exp(-x))


def _dot(a, b):
    return jnp.dot(a, b, preferred_element_type=F32)


def _dot_t(a, b):
    return lax.dot_general(a, b, (((1,), (1,)), ((), ())), preferred_element_type=F32)


def _split_bf16(x):
    hi = x.astype(BF16)
    lo = (x - hi.astype(F32)).astype(BF16)
    return hi, lo


def _pack_rows(x):
    k = x.shape[1] // 2
    hi = lax.bitcast_convert_type(x[:, :k].astype(jnp.bfloat16).astype(F32), jnp.int32)
    lo = lax.bitcast_convert_type(x[:, k:].astype(jnp.bfloat16).astype(F32), jnp.int32)
    return hi | lax.shift_right_logical(lo, 16)


def _unpack_rows(w):
    a = lax.bitcast_convert_type(w & jnp.int32(-65536), F32)
    b = lax.bitcast_convert_type(lax.shift_left(w, 16), F32)
    return a, b


def _rope_kernel(pos_ref, freq_ref, sign_ref, c_ref, s_ref):
    ang = pos_ref[...].astype(F32) * freq_ref[...]
    c_ref[...] = jnp.cos(ang)
    s_ref[...] = jnp.sin(ang) * sign_ref[...]


def _rope_tables(pos_col, inv_freq):
    n = pos_col.shape[0]
    lane = np.arange(LANES) % HEAD_DIM
    sel = np.zeros((LANES, ROPE_DIMS // 2), np.float32)
    for l in range(LANES):
        if lane[l] < ROPE_DIMS:
            sel[l, lane[l] % (ROPE_DIMS // 2)] = 1.0
    freq = (jnp.asarray(sel) * inv_freq[None, :]).sum(-1)[None, :]
    sign = np.where(lane < ROPE_DIMS // 2, -1.0, 1.0).astype(np.float32)[None, :]
    tm = 1024
    return pl.pallas_call(
        _rope_kernel,
        grid=(n // tm,),
        in_specs=[pl.BlockSpec((tm, 1), lambda i: (i, 0)),
                  pl.BlockSpec((1, LANES), lambda i: (0, 0)),
                  pl.BlockSpec((1, LANES), lambda i: (0, 0))],
        out_specs=[pl.BlockSpec((tm, LANES), lambda i: (i, 0))] * 2,
        out_shape=[jax.ShapeDtypeStruct((n, LANES), F32)] * 2,
        compiler_params=_cparams("parallel"),
        name="rope_tables",
    )(pos_col, freq, jnp.asarray(sign))


def _headnorm_rope(t, gain, c, s):
    lane = lax.broadcasted_iota(jnp.int32, t.shape, 1)
    lo = lane < HEAD_DIM
    sq = t * t
    s_lo = jnp.sum(jnp.where(lo, sq, 0.0), axis=-1, keepdims=True)
    s_hi = jnp.sum(jnp.where(lo, 0.0, sq), axis=-1, keepdims=True)
    ms = jnp.where(lo, s_lo, s_hi) * (1.0 / HEAD_DIM)
    tn = (t * lax.rsqrt(ms + RMS_EPS)) * gain
    half = ROPE_DIMS // 2
    partner = jnp.where((lane & (HEAD_DIM - 1)) < half,
                        pltpu.roll(tn, LANES - half, 1), pltpu.roll(tn, half, 1))
    return tn * c + partner * s


def _in_proj_kernel(x_ref, gain_ref, w_ref, c_ref, s_ref, qg_ref, kg_ref,
                    q_ref, ks_ref, kw_ref, kc_ref, vc_ref, vs_ref, vw_ref, u_ref, g_ref, raw_ref):
    x = x_ref[...]
    ms = jnp.mean(x * x, axis=-1, keepdims=True)
    h = (x * lax.rsqrt(ms + RMS_EPS)) * gain_ref[...]
    z = _dot(h.astype(BF16), w_ref[...])
    c = c_ref[...]
    s = s_ref[...]
    lane = lax.broadcasted_iota(jnp.int32, c.shape, 1)
    lo = lane < HEAD_DIM
    scale = HEAD_DIM ** -0.5 * math.log2(math.e)
    for j in range(NSA_HEADS // 2):
        r = _headnorm_rope(z[:, C_Q + LANES * j:C_Q + LANES * (j + 1)], qg_ref[...], c, s) * scale
        rr = pltpu.roll(r, HEAD_DIM, 1)
        if j // (NSA_GROUP // 2) == 0:
            even, odd = jnp.where(lo, r, 0.0), jnp.where(lo, rr, 0.0)
        else:
            even, odd = jnp.where(lo, 0.0, rr), jnp.where(lo, 0.0, r)
        q_ref[:, 2 * LANES * j:2 * LANES * j + LANES] = even.astype(BF16)
        q_ref[:, 2 * LANES * j + LANES:2 * LANES * (j + 1)] = odd.astype(BF16)
    ks_ref[...] = _headnorm_rope(z[:, C_KS:C_KS + LANES], kg_ref[...], c, s).astype(BF16)
    kw_ref[...] = _headnorm_rope(z[:, C_KW:C_KW + LANES], kg_ref[...], c, s).astype(BF16)
    half = CMP_BLOCK // 2
    for col, out in ((C_KC, kc_ref), (C_VC, vc_ref)):
        raw_ref[...] = z[:, col:col + LANES]
        for l in range(half):
            piece = raw_ref[pl.ds(l, x.shape[0] // half, stride=half), :]
            out[:, LANES * l:LANES * (l + 1)] = piece.astype(BF16)
    vs_ref[...] = z[:, C_VS:C_VS + LANES].astype(BF16)
    vw_ref[...] = z[:, C_VW:C_VW + LANES].astype(BF16)
    u_ref[:, pl.ds(pl.program_id(1), 1), :] = z[:, C_U:C_U + S5_WIDTH][:, None, :]
    g_ref[...] = _sigmoid(z[:, C_G:C_G + LANES])


def _in_proj(x2, gain, w1, ctab, stab, qg, kg, batch, seq):
    n = x2.shape[0]
    tm = IN_TM
    nt = seq // tm
    row = lambda i, b: (b * nt + i, 0)
    const = lambda i, b: (0, 0)
    tok = lambda w: pl.BlockSpec((tm, w), row)
    half = CMP_BLOCK // 2
    grouped = pl.BlockSpec((tm // half, half * KV_WIDTH), row)
    rows_bf16 = lambda r, w: jax.ShapeDtypeStruct((r, w), BF16)
    return pl.pallas_call(
        _in_proj_kernel,
        grid=(nt, batch),
        in_specs=[tok(D_MODEL), pl.BlockSpec((1, D_MODEL), const), pl.BlockSpec((D_MODEL, IN_W), const),
                  tok(LANES), tok(LANES), pl.BlockSpec((1, LANES), const), pl.BlockSpec((1, LANES), const)],
        out_specs=[tok(2 * NSA_WIDTH), tok(LANES), tok(LANES), grouped, grouped, tok(LANES), tok(LANES),
                   pl.BlockSpec((tm, batch, S5_WIDTH), lambda i, b: (i, 0, 0)), tok(LANES)],
        out_shape=[rows_bf16(n, 2 * NSA_WIDTH), rows_bf16(n, LANES), rows_bf16(n, LANES),
                   rows_bf16(n // half, half * KV_WIDTH), rows_bf16(n // half, half * KV_WIDTH),
                   rows_bf16(n, LANES), rows_bf16(n, LANES),
                   jax.ShapeDtypeStruct((seq, batch, S5_WIDTH), F32), jax.ShapeDtypeStruct((n, LANES), F32)],
        scratch_shapes=[pltpu.VMEM((tm, LANES), F32)],
        compiler_params=_cparams("parallel", "arbitrary"),
        name="in_proj",
    )(x2, gain, w1, ctab, stab, qg, kg)


def _compress_mlp(h_ref, w_ref, p_ref, w2_ref):
    w = w_ref[...]
    pre = _dot(h_ref[...], w)
    phi, plo = _split_bf16(p_ref[...])
    pb = _dot(phi, w) + _dot(plo, w)
    rows = pre.shape[0]
    out = None
    for h in range(NSA_KV_HEADS):
        top = pre[:, LANES * h:LANES * (h + 1)]
        bot = pre[:, LANES * (2 + h):LANES * (3 + h)]
        bias = pb[0:1, LANES * h:LANES * (h + 1)] + pb[1:2, LANES * (2 + h):LANES * (3 + h)]
        a = _gelu(top + pltpu.roll(bot, rows - 1, 0) + bias)
        o = _dot(a.astype(BF16), w2_ref[h])
        out = o if out is None else out + o
    return out


def _compress_kernel(hk_ref, hv_ref, wk_ref, wv_ref, pk_ref, pv_ref, w2k_ref, w2v_ref,
                     kg_ref, c_ref, s_ref, kc_ref, vc_ref):
    k = _compress_mlp(hk_ref, wk_ref, pk_ref, w2k_ref)
    kc_ref[...] = _headnorm_rope(k, kg_ref[...], c_ref[...], s_ref[...]).astype(BF16)
    vc_ref[...] = _compress_mlp(hv_ref, wv_ref, pv_ref, w2v_ref).astype(BF16)


def _compress(hk, hv, wk, wv, pk, pv, w2k, w2v, kg, ccmp, scmp, batch):
    rows = hk.shape[0] // batch
    width = hk.shape[1]
    const2 = lambda b: (0, 0)
    const3 = lambda b: (0, 0, 0)
    rowb = lambda w: pl.BlockSpec((rows, w), lambda b: (b, 0))
    return pl.pallas_call(
        _compress_kernel,
        grid=(batch,),
        in_specs=[rowb(width), rowb(width),
                  pl.BlockSpec(wk.shape, const2), pl.BlockSpec(wv.shape, const2),
                  pl.BlockSpec(pk.shape, const2), pl.BlockSpec(pv.shape, const2),
                  pl.BlockSpec(w2k.shape, const3), pl.BlockSpec(w2v.shape, const3),
                  pl.BlockSpec((1, LANES), const2), rowb(LANES), rowb(LANES)],
        out_specs=[rowb(LANES), rowb(LANES)],
        out_shape=[jax.ShapeDtypeStruct((hk.shape[0], LANES), BF16)] * 2,
        compiler_params=_cparams("parallel"),
        name="compress_kv",
    )(hk, hv, wk, wv, pk, pv, w2k, w2v, kg, ccmp, scmp)


def _compress_weights(pos_emb, w1, w2):
    half = CMP_BLOCK // 2
    eye = jnp.eye(NSA_KV_HEADS, dtype=F32)
    w1r = w1.reshape(2, half, HEAD_DIM, CMP_HIDDEN).astype(BF16)
    zero = jnp.zeros_like(w1r[0])
    wexp = jnp.concatenate(
        [jnp.stack([w1r[s] if p == h else zero for p in range(NSA_KV_HEADS)], axis=1)
         .reshape(half * KV_WIDTH, CMP_HIDDEN) for s in range(2) for h in range(NSA_KV_HEADS)], axis=1)
    pexp = jnp.broadcast_to(pos_emb.reshape(2, half, 1, HEAD_DIM), (2, half, NSA_KV_HEADS, HEAD_DIM))
    pexp = jnp.pad(pexp.reshape(2, half * KV_WIDTH), ((0, SUBLANES - 2), (0, 0)))
    w2e = jnp.einsum("cd,ph->pchd", w2, eye).reshape(NSA_KV_HEADS, CMP_HIDDEN, KV_WIDTH)
    return wexp, pexp, w2e.astype(BF16)


def _attn_kernel(q_ref, ks_ref, vs_ref, kw_ref, vw_ref, kc_ref, vc_ref, g_ref, et_ref, ovt_ref, oh_ref, wb_ref,
                 cb_ref, o_ref, qc_ref, s_ref, p_ref, sw_ref, pw_ref, m_ref, al_ref, oc_ref, as_ref, aw_ref,
                 *, n_cmp, n_slc, top_k):
    tq, tk, wk, rb, nh = ATT_TQ, ATT_TK, ATT_WK, ATT_RB, NSA_HEADS
    rows, grp_rows = nh * tq, NSA_GROUP * tq
    qk_rows = 2 * tq
    q0 = pl.program_id(1) * tq
    for a in range(nh):
        qc_ref[a * tq:(a + 1) * tq, 0:LANES] = q_ref[:, LANES * a:LANES * (a + 1)]
    q8 = qc_ref[:, 0:LANES]
    lane = lax.broadcasted_iota(jnp.int32, (tq, LANES), 1)
    t_row = q0 + lax.broadcasted_iota(jnp.int32, (tq, LANES), 0)

    s_c = _dot_t(q8, kc_ref[...]).reshape(nh, tq, LANES)
    mask_c = ((lane * CMP_STRIDE + (CMP_BLOCK - 1)) <= t_row) & (lane < n_cmp)
    sm = jnp.where(mask_c[None], s_c, NEG_BIG)
    p = jnp.where(mask_c[None], jnp.exp2(sm - jnp.max(sm, axis=-1, keepdims=True)), 0.0)
    l = jnp.sum(p, axis=-1, keepdims=True)
    p_c = p * jnp.where(l > 0.0, 1.0 / l, 0.0)
    oc_ref[...] = _dot(p_c.reshape(rows, LANES).astype(BF16), vc_ref[...])

    blk = lax.broadcasted_iota(jnp.int32, (n_slc, tq), 0)
    cur = (q0 + lax.broadcasted_iota(jnp.int32, (n_slc, tq), 1)) >> int(math.log2(SLC_BLOCK))
    forced = (blk == 0) | (blk == cur) | (blk == cur - 1)
    for h in range(NSA_KV_HEADS):
        ps = p_c[NSA_GROUP * h]
        for g in range(1, NSA_GROUP):
            ps = ps + p_c[NSA_GROUP * h + g]
        hi, lo = _split_bf16(ps)
        imp = (_dot_t(ovt_ref[...], hi) + _dot_t(ovt_ref[...], lo))[0:n_slc]
        imp = jnp.where(forced, jnp.inf, jnp.where(blk > cur, -jnp.inf, imp))
        rank = jnp.zeros((n_slc, tq), F32)
        for i in range(n_slc):
            row = imp[i:i + 1, :]
            rank = rank + jnp.where((row > imp) | ((row == imp) & (blk > i)), 1.0, 0.0)
        bias_t = jnp.where(rank < float(top_k), 0.0, NEG_BIG)
        bias = jnp.concatenate([bias_t, jnp.zeros((LANES - n_slc, tq), F32)], axis=0).T.astype(BF16)
        for g in range(NSA_GROUP):
            a = NSA_GROUP * h + g
            qc_ref[a * tq:(a + 1) * tq, LANES:2 * LANES] = bias

    def softmax_pass(s_buf, p_buf, width, causal_case, online):
        for r in range(rows // rb):
            r0 = r * rb
            s = s_buf[pl.ds(r0, rb), 0:width]
            if causal_case is not None:
                s = s + cb_ref[causal_case, pl.ds(r0 % tq, rb), 0:width]
            m_new = jnp.broadcast_to(jnp.max(s, axis=1, keepdims=True), (rb, LANES))
            if online:
                m_old = m_ref[pl.ds(r0, rb), :]
                m_new = jnp.maximum(m_old, m_new)
                al_ref[pl.ds(r0, rb), :] = jnp.exp2(m_old - m_new)
                m_ref[pl.ds(r0, rb), :] = m_new
            pp = jnp.exp2(s - jnp.concatenate([m_new] * (width // LANES), axis=1))
            p_buf[pl.ds(r0, rb), 0:width] = pp.astype(BF16)

    def pv(acc_ref, p_buf, v, width, online):
        lane_v = lax.broadcasted_iota(jnp.int32, v.shape, 1)
        for h in range(NSA_KV_HEADS):
            rs = slice(h * grp_rows, (h + 1) * grp_rows)
            own = (lane_v < HEAD_DIM) if h == 0 else (lane_v >= HEAD_DIM)
            upd = _dot(p_buf[rs, 0:width], jnp.where(own, v, jnp.ones_like(v)))
            acc_ref[rs, :] = (al_ref[rs, :] * acc_ref[rs, :] + upd) if online else upd

    m_ref[...] = jnp.full(m_ref.shape, 0.1 * NEG_BIG, F32)
    as_ref[...] = jnp.zeros(as_ref.shape, F32)

    def slc_chunk(c, diagonal, width=tk):
        k0 = c * tk
        s_buf, p_buf = s_ref.at[c % 2], p_ref.at[c % 2]
        kcat = jnp.concatenate([ks_ref[pl.ds(k0, width), :], et_ref[pl.ds(k0, width), :]], axis=1)
        for g in range(rows // qk_rows):
            rs = slice(g * qk_rows, (g + 1) * qk_rows)
            s_buf[rs, 0:width] = _dot_t(qc_ref[rs, :], kcat)
        softmax_pass(s_buf, p_buf, width, ((q0 - k0) >> int(math.log2(tq))) if diagonal else None, True)
        pv(as_ref, p_buf, vs_ref[pl.ds(k0, width), :], width, True)

    def window_and_output():
        w0 = pl.multiple_of(jnp.maximum(q0 - WINDOW, 0), LANES)
        w_case = jnp.minimum(q0 >> int(math.log2(tq)), WINDOW // tq)
        kcat_w = jnp.concatenate([kw_ref[pl.ds(w0, wk), :], wb_ref[w_case]], axis=1)
        for g in range(rows // qk_rows):
            rs = slice(g * qk_rows, (g + 1) * qk_rows)
            sw_ref[rs, :] = _dot_t(jnp.concatenate([qc_ref[rs, 0:LANES], oh_ref[rs, :]], axis=1), kcat_w)
        softmax_pass(sw_ref, pw_ref, wk, None, False)
        pv(aw_ref, pw_ref, vw_ref[pl.ds(w0, wk), :], wk, False)

        lo_half = lane < HEAD_DIM
        for j in range(nh // 2):
            comb = []
            for a in (2 * j, 2 * j + 1):
                rs = slice(a * tq, (a + 1) * tq)
                o_s, o_w = as_ref[rs, :], aw_ref[rs, :]
                comb.append(g_ref[:, 3 * a:3 * a + 1] * oc_ref[rs, :]
                            + g_ref[:, 3 * a + 1:3 * a + 2] * (o_s * (1.0 / pltpu.roll(o_s, HEAD_DIM, 1)))
                            + g_ref[:, 3 * a + 2:3 * a + 3] * (o_w * (1.0 / pltpu.roll(o_w, HEAD_DIM, 1))))
            even, odd = comb
            if j // (NSA_GROUP // 2) == 0:
                pair = jnp.where(lo_half, even, pltpu.roll(odd, HEAD_DIM, 1))
            else:
                pair = jnp.where(lo_half, pltpu.roll(even, HEAD_DIM, 1), odd)
            o_ref[:, LANES * j:LANES * (j + 1)] = pair.astype(BF16)

    c_last = q0 >> int(math.log2(tk))
    early = ((q0 >> int(math.log2(tq))) & (tk // tq - 1)) < tk // tq // 2
    for n_full in range(ks_ref.shape[0] // tk):
        for first_half in (True, False):
            @pl.when((c_last == n_full) & (early if first_half else jnp.logical_not(early)))
            def _():
                for c in range(n_full):
                    slc_chunk(c, False)
                slc_chunk(n_full, True, tk // 2 if first_half else tk)
                window_and_output()


def _pack_w_in(w):
    sizes = (NSA_WIDTH,) + (KV_WIDTH,) * 6 + (3 * NSA_HEADS, S5_WIDTH, D_MODEL, D_MODEL)
    o = np.cumsum((0,) + sizes)
    q, kc, vc, ks, vs, kw, vw, g, u, ga, gs = [w[:, o[i]:o[i + 1]] for i in range(len(sizes))]
    g = jnp.pad(g, ((0, 0), (0, LANES - 3 * NSA_HEADS)))
    w1 = jnp.concatenate([q, ks, kw, kc, vc, vs, vw, u, g], axis=1).astype(BF16)
    return w1, ga.astype(BF16), gs.astype(BF16)


def _nsa_branch(x2, gain, w1, ctab, stab, ccmp, scmp, q_gain, k_gain, cmp_k, cmp_v, batch, seq):
    qg = jnp.tile(q_gain, 2)[None, :]
    kg = jnp.tile(k_gain, 2)[None, :]
    q, ks, kw, hk, hv, vs, vw, u_tm, gates = _in_proj(x2, gain, w1, ctab, stab, qg, kg, batch, seq)
    wk, pk, w2k = _compress_weights(*cmp_k)
    wv, pv, w2v = _compress_weights(*cmp_v)
    kc, vc = _compress(hk, hv, wk, wv, pk, pv, w2k, w2v, kg, ccmp, scmp, batch)
    o_a = _attention(q, ks, vs, kw, vw, kc, vc, gates, batch, seq)
    return o_a, u_tm


def _cmp_rope_tables(ctab, stab, batch, seq):
    def pick(t):
        t = t.reshape(batch, seq, LANES)[:, CMP_BLOCK - 1::CMP_STRIDE]
        t = jnp.pad(t, ((0, 0), (0, seq // CMP_STRIDE - t.shape[1]), (0, 0)))
        return t.reshape(batch * (seq // CMP_STRIDE), LANES)
    return pick(ctab), pick(stab)


def _s5_disc_kernel(are_ref, aim_ref, ldt_ref, lre_ref, lim_ref, fre_ref, fim_ref):
    lam_re = jnp.minimum(are_ref[...], -1e-4)
    lam_im = aim_ref[...]
    dt = jnp.exp(ldt_ref[...])
    mag = jnp.exp(lam_re * dt)
    ang = lam_im * dt
    lb_re = mag * jnp.cos(ang)
    lb_im = mag * jnp.sin(ang)
    den = lam_re * lam_re + lam_im * lam_im
    lre_ref[...] = lb_re
    lim_ref[...] = lb_im
    fre_ref[...] = ((lb_re - 1.0) * lam_re + lb_im * lam_im) / den
    fim_ref[...] = (lb_im * lam_re - (lb_re - 1.0) * lam_im) / den


def _s5_weights(a_re, a_im, log_dt, b_re, b_im, c_re, c_im):
    shp = jax.ShapeDtypeStruct((S5_GROUPS, S5_STATE), F32)
    lb_re, lb_im, f_re, f_im = pl.pallas_call(
        _s5_disc_kernel, out_shape=[shp] * 4, name="s5_discretise",
    )(a_re, a_im, log_dt[:, None])
    bb_re = f_re[..., None] * b_re - f_im[..., None] * b_im
    bb_im = f_re[..., None] * b_im + f_im[..., None] * b_re
    eye = jnp.eye(S5_GROUPS, dtype=F32)

    def in_map(bb):
        return jnp.einsum("gpc,gh->gchp", bb, eye).reshape(S5_WIDTH, S5_NSTATE)

    def out_map(cc):
        return jnp.einsum("gcp,gh->hpgc", cc, eye).reshape(S5_NSTATE, S5_WIDTH)

    w_b = jnp.concatenate([in_map(bb_re), in_map(bb_im)], axis=1).astype(BF16)
    w_c = jnp.concatenate([out_map(c_re), -out_map(c_im)], axis=0).astype(BF16)
    return lb_re.reshape(1, S5_NSTATE), lb_im.reshape(1, S5_NSTATE), w_b, w_c


def _s5_kernel(u_ref, wb_ref, lre_ref, lim_ref, wc_ref, d_ref, wg_ref, o_ref, x_ref, st_ref, *, batch):
    ns = S5_NSTATE

    @pl.when(pl.program_id(0) == 0)
    def _():
        st_ref[...] = jnp.zeros(st_ref.shape, F32)

    u = u_ref[...].reshape(u_ref.shape[0] * batch, S5_WIDTH)
    x_ref[...] = _dot(u.astype(BF16), wb_ref[...])
    lre = jnp.broadcast_to(lre_ref[...], (batch, ns))
    lim = jnp.broadcast_to(lim_ref[...], (batch, ns))

    def step(t, carry):
        xr, xi = carry
        r0 = pl.multiple_of(t * batch, batch)
        nr = lre * xr - lim * xi + x_ref[pl.ds(r0, batch), 0:ns]
        ni = lre * xi + lim * xr + x_ref[pl.ds(r0, batch), ns:2 * ns]
        x_ref[pl.ds(r0, batch), 0:ns] = nr
        x_ref[pl.ds(r0, batch), ns:2 * ns] = ni
        return nr, ni

    steps = u.shape[0] // batch
    xr, xi = lax.fori_loop(0, steps, step, (st_ref[:, 0:ns], st_ref[:, ns:2 * ns]), unroll=4)
    st_ref[:, 0:ns] = xr
    st_ref[:, ns:2 * ns] = xi
    y = _dot(x_ref[...].astype(BF16), wc_ref[...]) + d_ref[...] * u
    z = _dot(_gelu(y).astype(BF16), wg_ref[...])
    o = z[:, :S5_WIDTH] * _sigmoid(z[:, S5_WIDTH:])
    for c in range(S5_WIDTH // LANES):
        o_ref[c] = o[:, c * LANES:(c + 1) * LANES]


def _s5(u_tm, lre, lim, w_b, w_c, d_skip, w_glu, batch, seq):
    rows = S5_T * batch
    const = lambda t: (0, 0)
    return pl.pallas_call(
        functools.partial(_s5_kernel, batch=batch),
        grid=(seq // S5_T,),
        in_specs=[pl.BlockSpec((S5_T, batch, S5_WIDTH), lambda t: (t, 0, 0)),
                  pl.BlockSpec(w_b.shape, const), pl.BlockSpec(lre.shape, const), pl.BlockSpec(lim.shape, const),
                  pl.BlockSpec(w_c.shape, const), pl.BlockSpec((1, S5_WIDTH), const),
                  pl.BlockSpec(w_glu.shape, const)],
        out_specs=pl.BlockSpec((S5_WIDTH // LANES, rows, LANES), lambda t: (0, t, 0)),
        out_shape=jax.ShapeDtypeStruct((S5_WIDTH // LANES, seq * batch, LANES), F32),
        scratch_shapes=[pltpu.VMEM((rows, 2 * S5_NSTATE), F32), pltpu.VMEM((batch, 2 * S5_NSTATE), F32)],
        compiler_params=_cparams("arbitrary"),
        name="s5_scan",
    )(u_tm, w_b, lre, lim, w_c, d_skip, w_glu)


RT_GROUP_ROW = 0
RT_EXPERT_ROW = SUBLANES


def _rmsnorm(x, gain):
    return (x * lax.rsqrt(jnp.mean(x * x, axis=-1, keepdims=True) + RMS_EPS)) * gain


def _merge_kernel(x_ref, g1_ref, wga_ref, wgs_ref, oa_ref, ob_ref, pa_ref, ps_ref, wo_ref, g2_ref,
                  wrh_ref, wrl_ref, rb_ref, x1_ref, h2_ref, lg_ref):
    x = x_ref[...]
    h = _rmsnorm(x, g1_ref[...]).astype(BF16)
    gate_a = _sigmoid(_dot(h, wga_ref[...]))
    gate_s = _sigmoid(_dot(h, wgs_ref[...]))
    rows_b = pl.ds(pl.program_id(1), x.shape[0], stride=ob_ref.shape[1] // x.shape[0])
    o_b = jnp.concatenate([ob_ref[c, rows_b, :] for c in range(S5_WIDTH // LANES)], axis=1)
    merged = gate_a * _dot(oa_ref[...], pa_ref[...]) + gate_s * _dot(o_b.astype(BF16), ps_ref[...])
    x1 = x + _dot(merged.astype(BF16), wo_ref[...])
    x1_ref[...] = x1
    h2 = _rmsnorm(x1, g2_ref[...])
    h2_ref[...] = _pack_rows(h2)
    hi, lo = _split_bf16(h2)
    lg_ref[...] = (_dot_t(wrh_ref[...], hi) + _dot_t(wrh_ref[...], lo) + _dot_t(wrl_ref[...], hi)
                   + rb_ref[...])


def _router_weights(w_rg, b_rg, w_re, b_re):
    def rows(group_part, expert_part):
        return jnp.concatenate([
            group_part, jnp.zeros((RT_EXPERT_ROW - N_EXPERT_GROUPS,) + group_part.shape[1:], F32),
            expert_part, jnp.zeros((LANES - RT_EXPERT_ROW - N_EXPERTS,) + group_part.shape[1:], F32)], axis=0)

    hi, lo = _split_bf16(rows(w_rg.T, w_re.T))
    return hi, lo, rows(b_rg[:, None], b_re[:, None])


def _merge(x2, g1, w_ga, w_gs, o_a, ob_tm, p_a, p_s, w_o, g2, wrh, wrl, rb, batch, seq):
    n = x2.shape[0]
    tm = MRG_TM
    nt = seq // tm
    row = lambda i, b: (b * nt + i, 0)
    const = lambda i, b: (0, 0)
    tok = lambda w: pl.BlockSpec((tm, w), row)
    full = lambda a: pl.BlockSpec(a.shape, const)
    return pl.pallas_call(
        _merge_kernel,
        grid=(nt, batch),
        in_specs=[tok(D_MODEL), full(g1), full(w_ga), full(w_gs), tok(NSA_WIDTH),
                  pl.BlockSpec((S5_WIDTH // LANES, tm * batch, LANES), lambda i, b: (0, i, 0)),
                  full(p_a), full(p_s), full(w_o),
                  full(g2), full(wrh), full(wrl), full(rb)],
        out_specs=[tok(D_MODEL), tok(D_MODEL // 2), pl.BlockSpec((LANES, tm), lambda i, b: (0, b * nt + i))],
        out_shape=[jax.ShapeDtypeStruct((n, D_MODEL), F32), jax.ShapeDtypeStruct((n, D_MODEL // 2), jnp.int32),
                   jax.ShapeDtypeStruct((LANES, n), F32)],
        compiler_params=_cparams("parallel", "parallel"),
        name="merge_out_proj",
    )(x2, g1, w_ga, w_gs, o_a, ob_tm, p_a, p_s, w_o, g2, wrh, wrl, rb)


def _first_argmax(v, rows):
    vmax = jnp.max(v, axis=0, keepdims=True)
    idx = jnp.min(jnp.where(v == vmax, rows, SUBLANES), axis=0, keepdims=True)
    return vmax, idx


def _route_kernel(lg_ref, tri_ref, e_ref, w_ref, r_ref, cnt_ref, carry_ref):
    tt = lg_ref.shape[1]

    @pl.when(pl.program_id(0) == 0)
    def _():
        carry_ref[...] = jnp.zeros(carry_ref.shape, F32)

    rows = lax.broadcasted_iota(jnp.int32, (SUBLANES, tt), 0)
    gl = jnp.where(rows < N_EXPERT_GROUPS, lg_ref[RT_GROUP_ROW:RT_GROUP_ROW + SUBLANES, :], -jnp.inf)
    gmax, g_sel = _first_argmax(gl, rows)
    g_w = 1.0 / jnp.sum(jnp.exp(gl - gmax), axis=0, keepdims=True)

    el = jnp.zeros((EXPERTS_PER_GROUP, tt), F32)
    for g in range(N_EXPERT_GROUPS):
        r0 = RT_EXPERT_ROW + EXPERTS_PER_GROUP * g
        el = jnp.where(g_sel == g, lg_ref[r0:r0 + EXPERTS_PER_GROUP, :], el)
    ex = jnp.exp(el - jnp.max(el, axis=0, keepdims=True))
    p = ex / jnp.sum(ex, axis=0, keepdims=True)
    p1, i1 = _first_argmax(p, rows)
    p2, i2 = _first_argmax(jnp.where(rows == i1, -1.0, p), rows)
    e1 = g_sel * EXPERTS_PER_GROUP + i1
    e2 = g_sel * EXPERTS_PER_GROUP + i2
    scale = g_w / (p1 + p2)

    erow = lax.broadcasted_iota(jnp.int32, (N_EXPERTS, tt), 0)
    oh1 = jnp.where(erow == e1, 1.0, 0.0)
    oh2 = jnp.where(erow == e2, 1.0, 0.0)
    both = oh1 + oh2
    before = carry_ref[:, 0:1] + _dot(both.astype(BF16), tri_ref[...])
    carry_ref[...] = carry_ref[...] + jnp.sum(both, axis=1, keepdims=True)
    cnt_ref[...] = carry_ref[...]

    zi = jnp.zeros((SUBLANES - 2, tt), jnp.int32)
    e_ref[...] = jnp.concatenate([e1, e2, zi], axis=0)
    w_ref[...] = jnp.concatenate([p1 * scale, p2 * scale, zi.astype(F32)], axis=0)
    r_ref[...] = jnp.concatenate([jnp.sum(oh1 * before, axis=0, keepdims=True),
                                  jnp.sum(oh2 * before, axis=0, keepdims=True),
                                  zi.astype(F32)], axis=0).astype(jnp.int32)


def _route(logits_t):
    n = logits_t.shape[1]
    tt = RT_TT
    tri = jnp.asarray(np.triu(np.ones((tt, tt), np.float32), 1), BF16)
    blk = pl.BlockSpec((SUBLANES, tt), lambda i: (0, i))
    return pl.pallas_call(
        _route_kernel,
        grid=(n // tt,),
        in_specs=[pl.BlockSpec((LANES, tt), lambda i: (0, i)), pl.BlockSpec((tt, tt), lambda i: (0, 0))],
        out_specs=[blk, blk, blk, pl.BlockSpec((N_EXPERTS, LANES), lambda i: (0, 0))],
        out_shape=[jax.ShapeDtypeStruct((SUBLANES, n), jnp.int32), jax.ShapeDtypeStruct((SUBLANES, n), F32),
                   jax.ShapeDtypeStruct((SUBLANES, n), jnp.int32), jax.ShapeDtypeStruct((N_EXPERTS, LANES), F32)],
        scratch_shapes=[pltpu.VMEM((N_EXPERTS, LANES), F32)],
        compiler_params=_cparams("arbitrary"),
        name="moe_route",
    )(logits_t, tri)


def _expert_kernel(be_ref, nb_ref, x_ref, wg_ref, wu_ref, wd_ref, o_ref, wgb_ref, wub_ref, wdb_ref):
    i = pl.program_id(0)
    half = D_MODEL // 2

    @pl.when(i < nb_ref[0])
    def _():
        @pl.when((i == 0) | (be_ref[i] != be_ref[jnp.maximum(i - 1, 0)]))
        def _():
            wgb_ref[...] = wg_ref[...].astype(BF16)
            wub_ref[...] = wu_ref[...].astype(BF16)
            wdb_ref[...] = wd_ref[...].astype(BF16)

        xa, xb = _unpack_rows(x_ref[...])
        xa, xb = xa.astype(BF16), xb.astype(BF16)
        g = _dot(xa, wgb_ref[0:half, :]) + _dot(xb, wgb_ref[half:, :])
        u = _dot(xa, wub_ref[0:half, :]) + _dot(xb, wub_ref[half:, :])
        o_ref[...] = _pack_rows(_dot(((g * _sigmoid(g)) * u).astype(BF16), wdb_ref[...]))

    @pl.when(i >= nb_ref[0])
    def _():
        o_ref[...] = jnp.zeros(o_ref.shape, o_ref.dtype)


def _experts(blk_expert, n_used, buf, w_gate, w_up, w_down, layer):
    cap, width = buf.shape
    blk = MOE_BLK
    row = lambda i, be, nb: (jnp.minimum(i, nb[0] - 1), 0)
    wmap = lambda i, be, nb: (layer, be[jnp.minimum(i, nb[0] - 1)], 0, 0)
    return pl.pallas_call(
        _expert_kernel,
        grid_spec=pltpu.PrefetchScalarGridSpec(
            num_scalar_prefetch=2,
            grid=(cap // blk,),
            in_specs=[pl.BlockSpec((blk, width), row),
                      pl.BlockSpec((None, None, D_MODEL, EXPERT_FF), wmap),
                      pl.BlockSpec((None, None, D_MODEL, EXPERT_FF), wmap),
                      pl.BlockSpec((None, None, EXPERT_FF, D_MODEL), wmap)],
            out_specs=pl.BlockSpec((blk, width), lambda i, be, nb: (i, 0)),
            scratch_shapes=[pltpu.VMEM((D_MODEL, EXPERT_FF), BF16), pltpu.VMEM((D_MODEL, EXPERT_FF), BF16),
                            pltpu.VMEM((EXPERT_FF, D_MODEL), BF16)],
        ),
        out_shape=jax.ShapeDtypeStruct((cap, width), buf.dtype),
        compiler_params=_cparams("arbitrary"),
        name="moe_experts",
    )(blk_expert, n_used, buf, w_gate, w_up, w_down)


SC_CH = 64


def _sc_mesh():
    return plsc.VectorSubcoreMesh(core_axis_name="c", subcore_axis_name="s")


def _sc_worker(info):
    return lax.axis_index("s") * info.num_cores + lax.axis_index("c")


def _sc_gather_rows(table, idx, rows=None):
    info = plsc.get_sparse_core_info()
    lanes, workers = info.num_lanes, info.num_cores * info.num_subcores
    invert = rows is not None
    rows = rows if invert else idx.shape[0]
    width = table.shape[1]
    per_w = rows // workers
    n_chunks = per_w // SC_CH
    assert per_w % (2 * SC_CH) == 0
    scan = 2048
    n_tok = table.shape[0]

    def body(table_hbm, idx_hbm, out_hbm, idx_v, rows_a, rows_b, sem_a, sem_b, *scan_v):
        base = _sc_worker(info) * per_w
        if invert:
            lane = lax.iota(jnp.int32, lanes)

            @pl.loop(0, per_w // lanes)
            def _(j):
                idx_v[pl.ds(j * lanes, lanes)] = (base + j * lanes + lane) & (n_tok - 1)

            @pl.loop(0, idx.shape[0] // scan)
            def _(c):
                pltpu.sync_copy(idx_hbm.at[pl.ds(c * scan, scan)], scan_v[0])

                @plsc.parallel_loop(0, scan // lanes, unroll=8)
                def _(j):
                    rel = scan_v[0][pl.ds(j * lanes, lanes)] - base
                    mine = (rel >= 0) & (rel < per_w)
                    token = (c * scan + j * lanes + lane) & (n_tok - 1)
                    plsc.store_scatter(idx_v, [jnp.where(mine, rel, 0)], token, mask=mine)
        else:
            pltpu.sync_copy(idx_hbm.at[pl.ds(base, per_w)], idx_v)

        def gather(j, buf, sem):
            return pltpu.make_async_copy(table_hbm.at[idx_v.at[pl.ds(j * SC_CH, SC_CH)]], buf, sem)

        def write(j, buf):
            pltpu.sync_copy(buf, out_hbm.at[pl.ds(base + j * SC_CH, SC_CH)])

        gather(0, rows_a, sem_a).start()

        @pl.loop(0, n_chunks, step=2)
        def _(j):
            gather(j + 1, rows_b, sem_b).start()
            gather(j, rows_a, sem_a).wait()
            write(j, rows_a)

            @pl.when(j + 2 < n_chunks)
            def _():
                gather(j + 2, rows_a, sem_a).start()

            gather(j + 1, rows_b, sem_b).wait()
            write(j + 1, rows_b)

    chunk_buf = pltpu.VMEM((SC_CH, width), table.dtype)
    scratch = [pltpu.VMEM((per_w,), jnp.int32), chunk_buf, chunk_buf,
               pltpu.SemaphoreType.DMA, pltpu.SemaphoreType.DMA]
    return pl.kernel(
        body, out_type=jax.ShapeDtypeStruct((rows, width), table.dtype), mesh=_sc_mesh(),
        scratch_types=scratch + ([pltpu.VMEM((scan,), jnp.int32)] if invert else []),
        compiler_params=pltpu.CompilerParams(needs_layout_passes=False) if invert else None,
        name="sc_dispatch_rows" if invert else "sc_gather_rows",
    )(table, idx)


def _blend_kernel(x1_ref, w_ref, g0_ref, g1_ref, o_ref):
    half = x1_ref.shape[1] // 2
    a0, b0 = _unpack_rows(g0_ref[...])
    a1, b1 = _unpack_rows(g1_ref[...])
    w0, w1 = w_ref[:, 0:1], w_ref[:, 1:2]
    o_ref[:, 0:half] = x1_ref[:, 0:half] + w0 * a0 + w1 * a1
    o_ref[:, half:] = x1_ref[:, half:] + w0 * b0 + w1 * b1


def _blend(x1, w_col, gathered):
    n = x1.shape[0]
    tt = DSP_TT
    nt = n // tt
    width = gathered.shape[1]
    return pl.pallas_call(
        _blend_kernel,
        grid=(nt,),
        in_specs=[pl.BlockSpec((tt, D_MODEL), lambda i: (i, 0)),
                  pl.BlockSpec((tt, EXPERT_TOP_K), lambda i: (i, 0)),
                  pl.BlockSpec((tt, width), lambda i: (i, 0)),
                  pl.BlockSpec((tt, width), lambda i: (nt + i, 0))],
        out_specs=pl.BlockSpec((tt, D_MODEL), lambda i: (i, 0)),
        out_shape=jax.ShapeDtypeStruct((n, D_MODEL), F32),
        compiler_params=_cparams("parallel"),
        name="moe_blend",
    )(x1, w_col, gathered, gathered)


def _moe(x1, h2, logits_t, w_gate, w_up, w_down, layer):
    n = x1.shape[0]
    e_idx, wts, rank, counts = _route(logits_t)
    counts = counts[:, 0].astype(jnp.int32)
    padded = (counts + MOE_BLK - 1) // MOE_BLK * MOE_BLK
    pends = jnp.cumsum(padded)
    pstarts = pends - padded
    experts = jnp.arange(N_EXPERTS, dtype=jnp.int32)
    start_of = jnp.sum(jnp.where(e_idx[None, :EXPERT_TOP_K] == experts[:, None, None],
                                 pstarts[:, None, None], 0), axis=0)
    dest = start_of + rank[:EXPERT_TOP_K]
    cap = n * EXPERT_TOP_K + N_EXPERTS * MOE_BLK
    n_blk = cap // MOE_BLK
    blk_row = jnp.arange(n_blk, dtype=jnp.int32) * MOE_BLK
    blk_expert = jnp.minimum(jnp.sum((pends[None, :] <= blk_row[:, None]).astype(jnp.int32), axis=1),
                             N_EXPERTS - 1)
    n_used = (pends[-1:] // MOE_BLK).astype(jnp.int32)
    dest_flat = dest.reshape(EXPERT_TOP_K * n)
    buf = _sc_gather_rows(h2, dest_flat, rows=cap)
    out_buf = _experts(blk_expert, n_used, buf, w_gate, w_up, w_down, layer)
    return _blend(x1, wts[:EXPERT_TOP_K].T, _sc_gather_rows(out_buf, dest_flat))


def kernel(x, positions, norm_mix, norm_ffn, w_in, q_gain, k_gain, cmp_k_pos, cmp_k_w1, cmp_k_w2, cmp_v_pos,
           cmp_v_w1, cmp_v_w2, s5_a_re, s5_a_im, s5_log_dt, s5_b_re, s5_b_im, s5_c_re, s5_c_im, s5_d, s5_glu,
           proj_nsa, proj_s5, w_out, router_group, router_group_bias, router_expert, router_expert_bias,
           expert_gate, expert_up, expert_down):
    batch, seq, _ = x.shape
    depth = w_in.shape[0]
    n = batch * seq
    inv_freq = ROPE_THETA ** (-jnp.arange(0, ROPE_DIMS, 2, dtype=jnp.float32) / ROPE_DIMS)
    ctab, stab = _rope_tables(positions.reshape(n, 1), inv_freq)
    ccmp, scmp = _cmp_rope_tables(ctab, stab, batch, seq)
    x2 = x.reshape(n, D_MODEL)
    for l in range(depth):
        w1, w_ga, w_gs = _pack_w_in(w_in[l])
        o_a, u_tm = _nsa_branch(x2, norm_mix[l][None], w1, ctab, stab, ccmp, scmp, q_gain[l], k_gain[l],
                                (cmp_k_pos[l], cmp_k_w1[l], cmp_k_w2[l]),
                                (cmp_v_pos[l], cmp_v_w1[l], cmp_v_w2[l]), batch, seq)
        lre, lim, w_b, w_c = _s5_weights(s5_a_re[l], s5_a_im[l], s5_log_dt[l], s5_b_re[l], s5_b_im[l],
                                         s5_c_re[l], s5_c_im[l])
        ob_tm = _s5(u_tm, lre, lim, w_b, w_c, s5_d[l][None],
                    s5_glu[l].astype(BF16), batch, seq)
        wrh, wrl, rb = _router_weights(router_group[l], router_group_bias[l], router_expert[l],
                                       router_expert_bias[l])
        x1, h2, logits_t = _merge(x2, norm_mix[l][None], w_ga, w_gs, o_a, ob_tm, proj_nsa[l].astype(BF16),
                                  proj_s5[l].astype(BF16), w_out[l].astype(BF16), norm_ffn[l][None],
                                  wrh, wrl, rb, batch, seq)
        x2 = _moe(x1, h2, logits_t, expert_gate, expert_up, expert_down, l)
    return x2.reshape(batch, seq, D_MODEL)


def _attn_tables(seq):
    n_cmp = (seq - CMP_BLOCK) // CMP_STRIDE + 1
    n_slc = seq // SLC_BLOCK
    key_blk = np.arange(seq) // SLC_BLOCK
    expand_t = (key_blk[:, None] == np.arange(LANES)[None, :]).astype(np.float32)
    cmp_start = np.arange(LANES) * CMP_STRIDE
    slc_start = np.arange(LANES) * SLC_BLOCK
    overlap_t = ((cmp_start[None, :] < slc_start[:, None] + SLC_BLOCK)
                 & (cmp_start[None, :] + CMP_BLOCK > slc_start[:, None])
                 & (np.arange(LANES)[None, :] < n_cmp) & (np.arange(LANES)[:, None] < n_slc))
    return (jnp.asarray(expand_t, BF16), jnp.asarray(overlap_t.astype(np.float32), BF16), n_cmp, n_slc)


def _attn_mask_tables():
    tq, tk, wk = ATT_TQ, ATT_TK, ATT_WK
    ql = np.arange(tq)
    onehot = np.tile(np.eye(tq, dtype=np.float32), (NSA_HEADS, 1))
    k = np.arange(wk)[:, None]
    band = []
    for c in range(WINDOW // tq + 1):
        if c < WINDOW // tq:
            keep = k <= c * tq + ql[None, :]
        else:
            keep = (k > ql[None, :]) & (k <= ql[None, :] + WINDOW)
        band.append(np.where(keep, 0.0, NEG_BIG))
    kl = np.arange(tk)[None, :]
    causal = [np.where(kl <= c * tq + ql[:, None], 0.0, NEG_BIG) for c in range(tk // tq)]
    return (jnp.asarray(onehot, BF16), jnp.asarray(np.stack(band), BF16),
            jnp.asarray(np.stack(causal), F32))


def _attention(q, ks, vs, kw, vw, kc, vc, gates, batch, seq):
    expand_t, overlap_t, n_cmp, n_slc = _attn_tables(seq)
    onehot, band, causal = _attn_mask_tables()
    tq = ATT_TQ
    nq = seq // tq
    rows = NSA_HEADS * tq
    tile = lambda w: pl.BlockSpec((tq, w), lambda b, i: (b * nq + i, 0))
    per_b = lambda r: pl.BlockSpec((r, LANES), lambda b, i: (b, 0))
    f32_rows = pltpu.VMEM((rows, LANES), F32)
    return pl.pallas_call(
        functools.partial(_attn_kernel, n_cmp=n_cmp, n_slc=n_slc, top_k=min(SLC_TOPK, n_slc)),
        grid=(batch, nq),
        in_specs=[tile(2 * NSA_WIDTH), per_b(seq), per_b(seq), per_b(seq), per_b(seq),
                  per_b(LANES), per_b(LANES), tile(LANES),
                  pl.BlockSpec(expand_t.shape, lambda b, i: (0, 0)),
                  pl.BlockSpec(overlap_t.shape, lambda b, i: (0, 0)),
                  pl.BlockSpec(onehot.shape, lambda b, i: (0, 0)),
                  pl.BlockSpec(band.shape, lambda b, i: (0, 0, 0)),
                  pl.BlockSpec(causal.shape, lambda b, i: (0, 0, 0))],
        out_specs=tile(NSA_WIDTH),
        out_shape=jax.ShapeDtypeStruct((batch * seq, NSA_WIDTH), BF16),
        scratch_shapes=[pltpu.VMEM((rows, 2 * LANES), BF16),
                        pltpu.VMEM((2, rows, ATT_TK), F32), pltpu.VMEM((2, rows, ATT_TK), BF16),
                        pltpu.VMEM((rows, ATT_WK), F32), pltpu.VMEM((rows, ATT_WK), BF16),
                        f32_rows, f32_rows, f32_rows, f32_rows, f32_rows],
        compiler_params=_cparams("parallel", "parallel"),
        name="nsa_attention",
    )(q, ks, vs, kw, vw, kc, vc, gates, expand_t, overlap_t, onehot, band, causal)
```

```python
import functools
import math

import jax
import jax.numpy as jnp
import numpy as np
from jax import lax
from jax.experimental import pallas as pl
from jax.experimental.pallas import tpu as pltpu
from jax.experimental.pallas import tpu_sc as plsc

F32 = jnp.float32
BF16 = jnp.bfloat16

D_MODEL = 1024
NSA_HEADS = 8
NSA_KV_HEADS = 2
NSA_GROUP = NSA_HEADS // NSA_KV_HEADS
HEAD_DIM = 64
ROPE_DIMS = HEAD_DIM // 4
ROPE_THETA = 500000.0
CMP_BLOCK = 32
CMP_STRIDE = 16
CMP_HIDDEN = 2 * HEAD_DIM
SLC_BLOCK = 64
SLC_TOPK = 16
WINDOW = 512
NSA_WIDTH = NSA_HEADS * HEAD_DIM
KV_WIDTH = NSA_KV_HEADS * HEAD_DIM
S5_WIDTH = 256
S5_GROUP_CH = 16
S5_GROUPS = S5_WIDTH // S5_GROUP_CH
S5_STATE = 64
S5_NSTATE = S5_GROUPS * S5_STATE
N_EXPERT_GROUPS = 4
EXPERTS_PER_GROUP = 8
N_EXPERTS = N_EXPERT_GROUPS * EXPERTS_PER_GROUP
EXPERT_TOP_K = 2
EXPERT_FF = 512
RMS_EPS = 1e-6
NEG_BIG = -1e30

LANES = 128
SUBLANES = 8
VMEM_LIMIT = 56 * 1024 * 1024

IN_TM = 512
ATT_TQ = 128
ATT_TK = 512
ATT_WK = WINDOW + ATT_TQ
ATT_RB = 128
S5_T = 128
MRG_TM = 512
RT_TT = 512
MOE_BLK = 512
DSP_TT = 1024

C_Q, C_KS, C_KW, C_KC, C_VC, C_VS, C_VW, C_U, C_G = 0, 512, 640, 768, 896, 1024, 1152, 1280, 1536
IN_W = 1664


def _cparams(*sem):
    return pltpu.CompilerParams(dimension_semantics=sem, vmem_limit_bytes=VMEM_LIMIT)


def _gelu(x):
    return 0.5 * x * (1.0 + jnp.tanh(math.sqrt(2.0 / math.pi) * (x + 0.044715 * (x * x * x))))


def _sigmoid(x):
    return 1.0 / (1.0 + jnp.exp(-x))


def _dot(a, b):
    return jnp.dot(a, b, preferred_element_type=F32)


def _dot_t(a, b):
    return lax.dot_general(a, b, (((1,), (1,)), ((), ())), preferred_element_type=F32)


def _split_bf16(x):
    hi = x.astype(BF16)
    lo = (x - hi.astype(F32)).astype(BF16)
    return hi, lo


def _pack_rows(x):
    k = x.shape[1] // 2
    hi = lax.bitcast_convert_type(x[:, :k].astype(jnp.bfloat16).astype(F32), jnp.int32)
    lo = lax.bitcast_convert_type(x[:, k:].astype(jnp.bfloat16).astype(F32), jnp.int32)
    return hi | lax.shift_right_logical(lo, 16)


def _unpack_rows(w):
    a = lax.bitcast_convert_type(w & jnp.int32(-65536), F32)
    b = lax.bitcast_convert_type(lax.shift_left(w, 16), F32)
    return a, b


def _rope_kernel(pos_ref, freq_ref, sign_ref, c_ref, s_ref):
    ang = pos_ref[...].astype(F32) * freq_ref[...]
    c_ref[...] = jnp.cos(ang)
    s_ref[...] = jnp.sin(ang) * sign_ref[...]


def _rope_tables(pos_col, inv_freq):
    n = pos_col.shape[0]
    lane = np.arange(LANES) % HEAD_DIM
    sel = np.zeros((LANES, ROPE_DIMS // 2), np.float32)
    for l in range(LANES):
        if lane[l] < ROPE_DIMS:
            sel[l, lane[l] % (ROPE_DIMS // 2)] = 1.0
    freq = (jnp.asarray(sel) * inv_freq[None, :]).sum(-1)[None, :]
    sign = np.where(lane < ROPE_DIMS // 2, -1.0, 1.0).astype(np.float32)[None, :]
    tm = 1024
    return pl.pallas_call(
        _rope_kernel,
        grid=(n // tm,),
        in_specs=[pl.BlockSpec((tm, 1), lambda i: (i, 0)),
                  pl.BlockSpec((1, LANES), lambda i: (0, 0)),
                  pl.BlockSpec((1, LANES), lambda i: (0, 0))],
        out_specs=[pl.BlockSpec((tm, LANES), lambda i: (i, 0))] * 2,
        out_shape=[jax.ShapeDtypeStruct((n, LANES), F32)] * 2,
        compiler_params=_cparams("parallel"),
        name="rope_tables",
    )(pos_col, freq, jnp.asarray(sign))


def _headnorm_rope(t, gain, c, s):
    lane = lax.broadcasted_iota(jnp.int32, t.shape, 1)
    lo = lane < HEAD_DIM
    sq = t * t
    s_lo = jnp.sum(jnp.where(lo, sq, 0.0), axis=-1, keepdims=True)
    s_hi = jnp.sum(jnp.where(lo, 0.0, sq), axis=-1, keepdims=True)
    ms = jnp.where(lo, s_lo, s_hi) * (1.0 / HEAD_DIM)
    tn = (t * lax.rsqrt(ms + RMS_EPS)) * gain
    half = ROPE_DIMS // 2
    partner = jnp.where((lane & (HEAD_DIM - 1)) < half,
                        pltpu.roll(tn, LANES - half, 1), pltpu.roll(tn, half, 1))
    return tn * c + partner * s


def _in_proj_kernel(x_ref, gain_ref, w_ref, c_ref, s_ref, qg_ref, kg_ref,
                    q_ref, ks_ref, kw_ref, kc_ref, vc_ref, vs_ref, vw_ref, u_ref, g_ref, raw_ref):
    x = x_ref[...]
    ms = jnp.mean(x * x, axis=-1, keepdims=True)
    h = (x * lax.rsqrt(ms + RMS_EPS)) * gain_ref[...]
    z = _dot(h.astype(BF16), w_ref[...])
    c = c_ref[...]
    s = s_ref[...]
    lane = lax.broadcasted_iota(jnp.int32, c.shape, 1)
    lo = lane < HEAD_DIM
    scale = HEAD_DIM ** -0.5 * math.log2(math.e)
    for j in range(NSA_HEADS // 2):
        r = _headnorm_rope(z[:, C_Q + LANES * j:C_Q + LANES * (j + 1)], qg_ref[...], c, s) * scale
        rr = pltpu.roll(r, HEAD_DIM, 1)
        if j // (NSA_GROUP // 2) == 0:
            even, odd = jnp.where(lo, r, 0.0), jnp.where(lo, rr, 0.0)
        else:
            even, odd = jnp.where(lo, 0.0, rr), jnp.where(lo, 0.0, r)
        q_ref[:, 2 * LANES * j:2 * LANES * j + LANES] = even.astype(BF16)
        q_ref[:, 2 * LANES * j + LANES:2 * LANES * (j + 1)] = odd.astype(BF16)
    ks_ref[...] = _headnorm_rope(z[:, C_KS:C_KS + LANES], kg_ref[...], c, s).astype(BF16)
    kw_ref[...] = _headnorm_rope(z[:, C_KW:C_KW + LANES], kg_ref[...], c, s).astype(BF16)
    half = CMP_BLOCK // 2
    for col, out in ((C_KC, kc_ref), (C_VC, vc_ref)):
        raw_ref[...] = z[:, col:col + LANES]
        for l in range(half):
            piece = raw_ref[pl.ds(l, x.shape[0] // half, stride=half), :]
            out[:, LANES * l:LANES * (l + 1)] = piece.astype(BF16)
    vs_ref[...] = z[:, C_VS:C_VS + LANES].astype(BF16)
    vw_ref[...] = z[:, C_VW:C_VW + LANES].astype(BF16)
    u_ref[:, pl.ds(pl.program_id(1), 1), :] = z[:, C_U:C_U + S5_WIDTH][:, None, :]
    g_ref[...] = _sigmoid(z[:, C_G:C_G + LANES])


def _in_proj(x2, gain, w1, ctab, stab, qg, kg, batch, seq):
    n = x2.shape[0]
    tm = IN_TM
    nt = seq // tm
    row = lambda i, b: (b * nt + i, 0)
    const = lambda i, b: (0, 0)
    tok = lambda w: pl.BlockSpec((tm, w), row)
    half = CMP_BLOCK // 2
    grouped = pl.BlockSpec((tm // half, half * KV_WIDTH), row)
    rows_bf16 = lambda r, w: jax.ShapeDtypeStruct((r, w), BF16)
    return pl.pallas_call(
        _in_proj_kernel,
        grid=(nt, batch),
        in_specs=[tok(D_MODEL), pl.BlockSpec((1, D_MODEL), const), pl.BlockSpec((D_MODEL, IN_W), const),
                  tok(LANES), tok(LANES), pl.BlockSpec((1, LANES), const), pl.BlockSpec((1, LANES), const)],
        out_specs=[tok(2 * NSA_WIDTH), tok(LANES), tok(LANES), grouped, grouped, tok(LANES), tok(LANES),
                   pl.BlockSpec((tm, batch, S5_WIDTH), lambda i, b: (i, 0, 0)), tok(LANES)],
        out_shape=[rows_bf16(n, 2 * NSA_WIDTH), rows_bf16(n, LANES), rows_bf16(n, LANES),
                   rows_bf16(n // half, half * KV_WIDTH), rows_bf16(n // half, half * KV_WIDTH),
                   rows_bf16(n, LANES), rows_bf16(n, LANES),
                   jax.ShapeDtypeStruct((seq, batch, S5_WIDTH), F32), jax.ShapeDtypeStruct((n, LANES), F32)],
        scratch_shapes=[pltpu.VMEM((tm, LANES), F32)],
        compiler_params=_cparams("parallel", "arbitrary"),
        name="in_proj",
    )(x2, gain, w1, ctab, stab, qg, kg)


def _compress_mlp(h_ref, w_ref, p_ref, w2_ref):
    w = w_ref[...]
    pre = _dot(h_ref[...], w)
    phi, plo = _split_bf16(p_ref[...])
    pb = _dot(phi, w) + _dot(plo, w)
    rows = pre.shape[0]
    out = None
    for h in range(NSA_KV_HEADS):
        top = pre[:, LANES * h:LANES * (h + 1)]
        bot = pre[:, LANES * (2 + h):LANES * (3 + h)]
        bias = pb[0:1, LANES * h:LANES * (h + 1)] + pb[1:2, LANES * (2 + h):LANES * (3 + h)]
        a = _gelu(top + pltpu.roll(bot, rows - 1, 0) + bias)
        o = _dot(a.astype(BF16), w2_ref[h])
        out = o if out is None else out + o
    return out


def _compress_kernel(hk_ref, hv_ref, wk_ref, wv_ref, pk_ref, pv_ref, w2k_ref, w2v_ref,
                     kg_ref, c_ref, s_ref, kc_ref, vc_ref):
    k = _compress_mlp(hk_ref, wk_ref, pk_ref, w2k_ref)
    kc_ref[...] = _headnorm_rope(k, kg_ref[...], c_ref[...], s_ref[...]).astype(BF16)
    vc_ref[...] = _compress_mlp(hv_ref, wv_ref, pv_ref, w2v_ref).astype(BF16)


def _compress(hk, hv, wk, wv, pk, pv, w2k, w2v, kg, ccmp, scmp, batch):
    rows = hk.shape[0] // batch
    width = hk.shape[1]
    const2 = lambda b: (0, 0)
    const3 = lambda b: (0, 0, 0)
    rowb = lambda w: pl.BlockSpec((rows, w), lambda b: (b, 0))
    return pl.pallas_call(
        _compress_kernel,
        grid=(batch,),
        in_specs=[rowb(width), rowb(width),
                  pl.BlockSpec(wk.shape, const2), pl.BlockSpec(wv.shape, const2),
                  pl.BlockSpec(pk.shape, const2), pl.BlockSpec(pv.shape, const2),
                  pl.BlockSpec(w2k.shape, const3), pl.BlockSpec(w2v.shape, const3),
                  pl.BlockSpec((1, LANES), const2), rowb(LANES), rowb(LANES)],
        out_specs=[rowb(LANES), rowb(LANES)],
        out_shape=[jax.ShapeDtypeStruct((hk.shape[0], LANES), BF16)] * 2,
        compiler_params=_cparams("parallel"),
        name="compress_kv",
    )(hk, hv, wk, wv, pk, pv, w2k, w2v, kg, ccmp, scmp)


def _compress_weights(pos_emb, w1, w2):
    half = CMP_BLOCK // 2
    eye = jnp.eye(NSA_KV_HEADS, dtype=F32)
    w1r = w1.reshape(2, half, HEAD_DIM, CMP_HIDDEN).astype(BF16)
    zero = jnp.zeros_like(w1r[0])
    wexp = jnp.concatenate(
        [jnp.stack([w1r[s] if p == h else zero for p in range(NSA_KV_HEADS)], axis=1)
         .reshape(half * KV_WIDTH, CMP_HIDDEN) for s in range(2) for h in range(NSA_KV_HEADS)], axis=1)
    pexp = jnp.broadcast_to(pos_emb.reshape(2, half, 1, HEAD_DIM), (2, half, NSA_KV_HEADS, HEAD_DIM))
    pexp = jnp.pad(pexp.reshape(2, half * KV_WIDTH), ((0, SUBLANES - 2), (0, 0)))
    w2e = jnp.einsum("cd,ph->pchd", w2, eye).reshape(NSA_KV_HEADS, CMP_HIDDEN, KV_WIDTH)
    return wexp, pexp, w2e.astype(BF16)


def _attn_kernel(q_ref, ks_ref, vs_ref, kw_ref, vw_ref, kc_ref, vc_ref, g_ref, et_ref, ovt_ref, oh_ref, wb_ref,
                 cb_ref, o_ref, qc_ref, s_ref, p_ref, sw_ref, pw_ref, m_ref, al_ref, oc_ref, as_ref, aw_ref,
                 *, n_cmp, n_slc, top_k):
    tq, tk, wk, rb, nh = ATT_TQ, ATT_TK, ATT_WK, ATT_RB, NSA_HEADS
    rows, grp_rows = nh * tq, NSA_GROUP * tq
    qk_rows = 2 * tq
    q0 = pl.program_id(1) * tq
    for a in range(nh):
        qc_ref[a * tq:(a + 1) * tq, 0:LANES] = q_ref[:, LANES * a:LANES * (a + 1)]
    q8 = qc_ref[:, 0:LANES]
    lane = lax.broadcasted_iota(jnp.int32, (tq, LANES), 1)
    t_row = q0 + lax.broadcasted_iota(jnp.int32, (tq, LANES), 0)

    s_c = _dot_t(q8, kc_ref[...]).reshape(nh, tq, LANES)
    mask_c = ((lane * CMP_STRIDE + (CMP_BLOCK - 1)) <= t_row) & (lane < n_cmp)
    sm = jnp.where(mask_c[None], s_c, NEG_BIG)
    p = jnp.where(mask_c[None], jnp.exp2(sm - jnp.max(sm, axis=-1, keepdims=True)), 0.0)
    l = jnp.sum(p, axis=-1, keepdims=True)
    p_c = p * jnp.where(l > 0.0, 1.0 / l, 0.0)
    oc_ref[...] = _dot(p_c.reshape(rows, LANES).astype(BF16), vc_ref[...])

    blk = lax.broadcasted_iota(jnp.int32, (n_slc, tq), 0)
    cur = (q0 + lax.broadcasted_iota(jnp.int32, (n_slc, tq), 1)) >> int(math.log2(SLC_BLOCK))
    forced = (blk == 0) | (blk == cur) | (blk == cur - 1)
    for h in range(NSA_KV_HEADS):
        ps = p_c[NSA_GROUP * h]
        for g in range(1, NSA_GROUP):
            ps = ps + p_c[NSA_GROUP * h + g]
        hi, lo = _split_bf16(ps)
        imp = (_dot_t(ovt_ref[...], hi) + _dot_t(ovt_ref[...], lo))[0:n_slc]
        imp = jnp.where(forced, jnp.inf, jnp.where(blk > cur, -jnp.inf, imp))
        rank = jnp.zeros((n_slc, tq), F32)
        for i in range(n_slc):
            row = imp[i:i + 1, :]
            rank = rank + jnp.where((row > imp) | ((row == imp) & (blk > i)), 1.0, 0.0)
        bias_t = jnp.where(rank < float(top_k), 0.0, NEG_BIG)
        bias = jnp.concatenate([bias_t, jnp.zeros((LANES - n_slc, tq), F32)], axis=0).T.astype(BF16)
        for g in range(NSA_GROUP):
            a = NSA_GROUP * h + g
            qc_ref[a * tq:(a + 1) * tq, LANES:2 * LANES] = bias

    def softmax_pass(s_buf, p_buf, width, causal_case, online):
        for r in range(rows // rb):
            r0 = r * rb
            s = s_buf[pl.ds(r0, rb), 0:width]
            if causal_case is not None:
                s = s + cb_ref[causal_case, pl.ds(r0 % tq, rb), 0:width]
            m_new = jnp.broadcast_to(jnp.max(s, axis=1, keepdims=True), (rb, LANES))
            if online:
                m_old = m_ref[pl.ds(r0, rb), :]
                m_new = jnp.maximum(m_old, m_new)
                al_ref[pl.ds(r0, rb), :] = jnp.exp2(m_old - m_new)
                m_ref[pl.ds(r0, rb), :] = m_new
            pp = jnp.exp2(s - jnp.concatenate([m_new] * (width // LANES), axis=1))
            p_buf[pl.ds(r0, rb), 0:width] = pp.astype(BF16)

    def pv(acc_ref, p_buf, v, width, online):
        lane_v = lax.broadcasted_iota(jnp.int32, v.shape, 1)
        for h in range(NSA_KV_HEADS):
            rs = slice(h * grp_rows, (h + 1) * grp_rows)
            own = (lane_v < HEAD_DIM) if h == 0 else (lane_v >= HEAD_DIM)
            upd = _dot(p_buf[rs, 0:width], jnp.where(own, v, jnp.ones_like(v)))
            acc_ref[rs, :] = (al_ref[rs, :] * acc_ref[rs, :] + upd) if online else upd

    m_ref[...] = jnp.full(m_ref.shape, 0.1 * NEG_BIG, F32)
    as_ref[...] = jnp.zeros(as_ref.shape, F32)

    def slc_chunk(c, diagonal, width=tk):
        k0 = c * tk
        s_buf, p_buf = s_ref.at[c % 2], p_ref.at[c % 2]
        kcat = jnp.concatenate([ks_ref[pl.ds(k0, width), :], et_ref[pl.ds(k0, width), :]], axis=1)
        for g in range(rows // qk_rows):
            rs = slice(g * qk_rows, (g + 1) * qk_rows)
            s_buf[rs, 0:width] = _dot_t(qc_ref[rs, :], kcat)
        softmax_pass(s_buf, p_buf, width, ((q0 - k0) >> int(math.log2(tq))) if diagonal else None, True)
        pv(as_ref, p_buf, vs_ref[pl.ds(k0, width), :], width, True)

    def window_and_output(width):
        w0 = pl.multiple_of(jnp.maximum(q0 - WINDOW, 0), LANES)
        w_case = jnp.minimum(q0 >> int(math.log2(tq)), WINDOW // tq)
        kcat_w = jnp.concatenate([kw_ref[pl.ds(w0, width), :], wb_ref[w_case, 0:width, :]], axis=1)
        for g in range(rows // qk_rows):
            rs = slice(g * qk_rows, (g + 1) * qk_rows)
            sw_ref[rs, 0:width] = _dot_t(jnp.concatenate([qc_ref[rs, 0:LANES], oh_ref[rs, :]], axis=1), kcat_w)
        softmax_pass(sw_ref, pw_ref, width, None, False)
        pv(aw_ref, pw_ref, vw_ref[pl.ds(w0, width), :], width, False)

        lo_half = lane < HEAD_DIM
        for j in range(nh // 2):
            comb = []
            for a in (2 * j, 2 * j + 1):
                rs = slice(a * tq, (a + 1) * tq)
                o_s, o_w = as_ref[rs, :], aw_ref[rs, :]
                comb.append(g_ref[:, 3 * a:3 * a + 1] * oc_ref[rs, :]
                            + g_ref[:, 3 * a + 1:3 * a + 2] * (o_s * (1.0 / pltpu.roll(o_s, HEAD_DIM, 1)))
                            + g_ref[:, 3 * a + 2:3 * a + 3] * (o_w * (1.0 / pltpu.roll(o_w, HEAD_DIM, 1))))
            even, odd = comb
            if j // (NSA_GROUP // 2) == 0:
                pair = jnp.where(lo_half, even, pltpu.roll(odd, HEAD_DIM, 1))
            else:
                pair = jnp.where(lo_half, pltpu.roll(even, HEAD_DIM, 1), odd)
            o_ref[:, LANES * j:LANES * (j + 1)] = pair.astype(BF16)

    c_last = q0 >> int(math.log2(tk))
    early = ((q0 >> int(math.log2(tq))) & (tk // tq - 1)) < tk // tq // 2
    for n_full in range(ks_ref.shape[0] // tk):
        for first_half in (True, False):
            @pl.when((c_last == n_full) & (early if first_half else jnp.logical_not(early)))
            def _():
                for c in range(n_full):
                    slc_chunk(c, False)
                slc_chunk(n_full, True, tk // 2 if first_half else tk)
                window_and_output((tk // 2 if first_half else tk) if n_full == 0 else wk)


def _pack_w_in(w):
    sizes = (NSA_WIDTH,) + (KV_WIDTH,) * 6 + (3 * NSA_HEADS, S5_WIDTH, D_MODEL, D_MODEL)
    o = np.cumsum((0,) + sizes)
    q, kc, vc, ks, vs, kw, vw, g, u, ga, gs = [w[:, o[i]:o[i + 1]] for i in range(len(sizes))]
    g = jnp.pad(g, ((0, 0), (0, LANES - 3 * NSA_HEADS)))
    w1 = jnp.concatenate([q, ks, kw, kc, vc, vs, vw, u, g], axis=1).astype(BF16)
    return w1, ga.astype(BF16), gs.astype(BF16)


def _nsa_branch(x2, gain, w1, ctab, stab, ccmp, scmp, q_gain, k_gain, cmp_k, cmp_v, batch, seq):
    qg = jnp.tile(q_gain, 2)[None, :]
    kg = jnp.tile(k_gain, 2)[None, :]
    q, ks, kw, hk, hv, vs, vw, u_tm, gates = _in_proj(x2, gain, w1, ctab, stab, qg, kg, batch, seq)
    wk, pk, w2k = _compress_weights(*cmp_k)
    wv, pv, w2v = _compress_weights(*cmp_v)
    kc, vc = _compress(hk, hv, wk, wv, pk, pv, w2k, w2v, kg, ccmp, scmp, batch)
    o_a = _attention(q, ks, vs, kw, vw, kc, vc, gates, batch, seq)
    return o_a, u_tm


def _cmp_rope_tables(ctab, stab, batch, seq):
    def pick(t):
        t = t.reshape(batch, seq, LANES)[:, CMP_BLOCK - 1::CMP_STRIDE]
        t = jnp.pad(t, ((0, 0), (0, seq // CMP_STRIDE - t.shape[1]), (0, 0)))
        return t.reshape(batch * (seq // CMP_STRIDE), LANES)
    return pick(ctab), pick(stab)


def _s5_disc_kernel(are_ref, aim_ref, ldt_ref, lre_ref, lim_ref, fre_ref, fim_ref):
    lam_re = jnp.minimum(are_ref[...], -1e-4)
    lam_im = aim_ref[...]
    dt = jnp.exp(ldt_ref[...])
    mag = jnp.exp(lam_re * dt)
    ang = lam_im * dt
    lb_re = mag * jnp.cos(ang)
    lb_im = mag * jnp.sin(ang)
    den = lam_re * lam_re + lam_im * lam_im
    lre_ref[...] = lb_re
    lim_ref[...] = lb_im
    fre_ref[...] = ((lb_re - 1.0) * lam_re + lb_im * lam_im) / den
    fim_ref[...] = (lb_im * lam_re - (lb_re - 1.0) * lam_im) / den


def _s5_weights(a_re, a_im, log_dt, b_re, b_im, c_re, c_im):
    shp = jax.ShapeDtypeStruct((S5_GROUPS, S5_STATE), F32)
    lb_re, lb_im, f_re, f_im = pl.pallas_call(
        _s5_disc_kernel, out_shape=[shp] * 4, name="s5_discretise",
    )(a_re, a_im, log_dt[:, None])
    bb_re = f_re[..., None] * b_re - f_im[..., None] * b_im
    bb_im = f_re[..., None] * b_im + f_im[..., None] * b_re
    eye = jnp.eye(S5_GROUPS, dtype=F32)

    def in_map(bb):
        return jnp.einsum("gpc,gh->gchp", bb, eye).reshape(S5_WIDTH, S5_NSTATE)

    def out_map(cc):
        return jnp.einsum("gcp,gh->hpgc", cc, eye).reshape(S5_NSTATE, S5_WIDTH)

    w_b = jnp.concatenate([in_map(bb_re), in_map(bb_im)], axis=1).astype(BF16)
    w_c = jnp.concatenate([out_map(c_re), -out_map(c_im)], axis=0).astype(BF16)
    return lb_re.reshape(1, S5_NSTATE), lb_im.reshape(1, S5_NSTATE), w_b, w_c


def _s5_kernel(u_ref, wb_ref, lre_ref, lim_ref, wc_ref, d_ref, wg_ref, o_ref, x_ref, st_ref, *, batch):
    ns = S5_NSTATE

    @pl.when(pl.program_id(0) == 0)
    def _():
        st_ref[...] = jnp.zeros(st_ref.shape, F32)

    u = u_ref[...].reshape(u_ref.shape[0] * batch, S5_WIDTH)
    x_ref[...] = _dot(u.astype(BF16), wb_ref[...])
    lre = jnp.broadcast_to(lre_ref[...], (batch, ns))
    lim = jnp.broadcast_to(lim_ref[...], (batch, ns))

    def step(t, carry):
        xr, xi = carry
        r0 = pl.multiple_of(t * batch, batch)
        nr = lre * xr - lim * xi + x_ref[pl.ds(r0, batch), 0:ns]
        ni = lre * xi + lim * xr + x_ref[pl.ds(r0, batch), ns:2 * ns]
        x_ref[pl.ds(r0, batch), 0:ns] = nr
        x_ref[pl.ds(r0, batch), ns:2 * ns] = ni
        return nr, ni

    steps = u.shape[0] // batch
    xr, xi = lax.fori_loop(0, steps, step, (st_ref[:, 0:ns], st_ref[:, ns:2 * ns]), unroll=4)
    st_ref[:, 0:ns] = xr
    st_ref[:, ns:2 * ns] = xi
    y = _dot(x_ref[...].astype(BF16), wc_ref[...]) + d_ref[...] * u
    z = _dot(_gelu(y).astype(BF16), wg_ref[...])
    o = z[:, :S5_WIDTH] * _sigmoid(z[:, S5_WIDTH:])
    for c in range(S5_WIDTH // LANES):
        o_ref[c] = o[:, c * LANES:(c + 1) * LANES]


def _s5(u_tm, lre, lim, w_b, w_c, d_skip, w_glu, batch, seq):
    rows = S5_T * batch
    const = lambda t: (0, 0)
    return pl.pallas_call(
        functools.partial(_s5_kernel, batch=batch),
        grid=(seq // S5_T,),
        in_specs=[pl.BlockSpec((S5_T, batch, S5_WIDTH), lambda t: (t, 0, 0)),
                  pl.BlockSpec(w_b.shape, const), pl.BlockSpec(lre.shape, const), pl.BlockSpec(lim.shape, const),
                  pl.BlockSpec(w_c.shape, const), pl.BlockSpec((1, S5_WIDTH), const),
                  pl.BlockSpec(w_glu.shape, const)],
        out_specs=pl.BlockSpec((S5_WIDTH // LANES, rows, LANES), lambda t: (0, t, 0)),
        out_shape=jax.ShapeDtypeStruct((S5_WIDTH // LANES, seq * batch, LANES), F32),
        scratch_shapes=[pltpu.VMEM((rows, 2 * S5_NSTATE), F32), pltpu.VMEM((batch, 2 * S5_NSTATE), F32)],
        compiler_params=_cparams("arbitrary"),
        name="s5_scan",
    )(u_tm, w_b, lre, lim, w_c, d_skip, w_glu)


RT_GROUP_ROW = 0
RT_EXPERT_ROW = SUBLANES


def _rmsnorm(x, gain):
    return (x * lax.rsqrt(jnp.mean(x * x, axis=-1, keepdims=True) + RMS_EPS)) * gain


def _merge_kernel(x_ref, g1_ref, wga_ref, wgs_ref, oa_ref, ob_ref, pa_ref, ps_ref, wo_ref, g2_ref,
                  wrh_ref, wrl_ref, rb_ref, x1_ref, h2_ref, lg_ref):
    x = x_ref[...]
    h = _rmsnorm(x, g1_ref[...]).astype(BF16)
    gate_a = _sigmoid(_dot(h, wga_ref[...]))
    gate_s = _sigmoid(_dot(h, wgs_ref[...]))
    rows_b = pl.ds(pl.program_id(1), x.shape[0], stride=ob_ref.shape[1] // x.shape[0])
    o_b = jnp.concatenate([ob_ref[c, rows_b, :] for c in range(S5_WIDTH // LANES)], axis=1)
    merged = gate_a * _dot(oa_ref[...], pa_ref[...]) + gate_s * _dot(o_b.astype(BF16), ps_ref[...])
    x1 = x + _dot(merged.astype(BF16), wo_ref[...])
    x1_ref[...] = x1
    h2 = _rmsnorm(x1, g2_ref[...])
    h2_ref[...] = _pack_rows(h2)
    hi, lo = _split_bf16(h2)
    lg_ref[...] = (_dot_t(wrh_ref[...], hi) + _dot_t(wrh_ref[...], lo) + _dot_t(wrl_ref[...], hi)
                   + rb_ref[...])


def _router_weights(w_rg, b_rg, w_re, b_re):
    def rows(group_part, expert_part):
        return jnp.concatenate([
            group_part, jnp.zeros((RT_EXPERT_ROW - N_EXPERT_GROUPS,) + group_part.shape[1:], F32),
            expert_part, jnp.zeros((LANES - RT_EXPERT_ROW - N_EXPERTS,) + group_part.shape[1:], F32)], axis=0)

    hi, lo = _split_bf16(rows(w_rg.T, w_re.T))
    return hi, lo, rows(b_rg[:, None], b_re[:, None])


def _merge(x2, g1, w_ga, w_gs, o_a, ob_tm, p_a, p_s, w_o, g2, wrh, wrl, rb, batch, seq):
    n = x2.shape[0]
    tm = MRG_TM
    nt = seq // tm
    row = lambda i, b: (b * nt + i, 0)
    const = lambda i, b: (0, 0)
    tok = lambda w: pl.BlockSpec((tm, w), row)
    full = lambda a: pl.BlockSpec(a.shape, const)
    return pl.pallas_call(
        _merge_kernel,
        grid=(nt, batch),
        in_specs=[tok(D_MODEL), full(g1), full(w_ga), full(w_gs), tok(NSA_WIDTH),
                  pl.BlockSpec((S5_WIDTH // LANES, tm * batch, LANES), lambda i, b: (0, i, 0)),
                  full(p_a), full(p_s), full(w_o),
                  full(g2), full(wrh), full(wrl), full(rb)],
        out_specs=[tok(D_MODEL), tok(D_MODEL // 2), pl.BlockSpec((LANES, tm), lambda i, b: (0, b * nt + i))],
        out_shape=[jax.ShapeDtypeStruct((n, D_MODEL), F32), jax.ShapeDtypeStruct((n, D_MODEL // 2), jnp.int32),
                   jax.ShapeDtypeStruct((LANES, n), F32)],
        compiler_params=_cparams("parallel", "parallel"),
        name="merge_out_proj",
    )(x2, g1, w_ga, w_gs, o_a, ob_tm, p_a, p_s, w_o, g2, wrh, wrl, rb)


def _first_argmax(v, rows):
    vmax = jnp.max(v, axis=0, keepdims=True)
    idx = jnp.min(jnp.where(v == vmax, rows, SUBLANES), axis=0, keepdims=True)
    return vmax, idx


def _route_kernel(lg_ref, tri_ref, e_ref, w_ref, r_ref, cnt_ref, carry_ref):
    tt = lg_ref.shape[1]

    @pl.when(pl.program_id(0) == 0)
    def _():
        carry_ref[...] = jnp.zeros(carry_ref.shape, F32)

    rows = lax.broadcasted_iota(jnp.int32, (SUBLANES, tt), 0)
    gl = jnp.where(rows < N_EXPERT_GROUPS, lg_ref[RT_GROUP_ROW:RT_GROUP_ROW + SUBLANES, :], -jnp.inf)
    gmax, g_sel = _first_argmax(gl, rows)
    g_w = 1.0 / jnp.sum(jnp.exp(gl - gmax), axis=0, keepdims=True)

    el = jnp.zeros((EXPERTS_PER_GROUP, tt), F32)
    for g in range(N_EXPERT_GROUPS):
        r0 = RT_EXPERT_ROW + EXPERTS_PER_GROUP * g
        el = jnp.where(g_sel == g, lg_ref[r0:r0 + EXPERTS_PER_GROUP, :], el)
    ex = jnp.exp(el - jnp.max(el, axis=0, keepdims=True))
    p = ex / jnp.sum(ex, axis=0, keepdims=True)
    p1, i1 = _first_argmax(p, rows)
    p2, i2 = _first_argmax(jnp.where(rows == i1, -1.0, p), rows)
    e1 = g_sel * EXPERTS_PER_GROUP + i1
    e2 = g_sel * EXPERTS_PER_GROUP + i2
    scale = g_w / (p1 + p2)

    erow = lax.broadcasted_iota(jnp.int32, (N_EXPERTS, tt), 0)
    oh1 = jnp.where(erow == e1, 1.0, 0.0)
    oh2 = jnp.where(erow == e2, 1.0, 0.0)
    both = oh1 + oh2
    before = carry_ref[:, 0:1] + _dot(both.astype(BF16), tri_ref[...])
    carry_ref[...] = carry_ref[...] + jnp.sum(both, axis=1, keepdims=True)
    cnt_ref[...] = carry_ref[...]

    zi = jnp.zeros((SUBLANES - 2, tt), jnp.int32)
    e_ref[...] = jnp.concatenate([e1, e2, zi], axis=0)
    w_ref[...] = jnp.concatenate([p1 * scale, p2 * scale, zi.astype(F32)], axis=0)
    r_ref[...] = jnp.concatenate([jnp.sum(oh1 * before, axis=0, keepdims=True),
                                  jnp.sum(oh2 * before, axis=0, keepdims=True),
                                  zi.astype(F32)], axis=0).astype(jnp.int32)


def _route(logits_t):
    n = logits_t.shape[1]
    tt = RT_TT
    tri = jnp.asarray(np.triu(np.ones((tt, tt), np.float32), 1), BF16)
    blk = pl.BlockSpec((SUBLANES, tt), lambda i: (0, i))
    return pl.pallas_call(
        _route_kernel,
        grid=(n // tt,),
        in_specs=[pl.BlockSpec((LANES, tt), lambda i: (0, i)), pl.BlockSpec((tt, tt), lambda i: (0, 0))],
        out_specs=[blk, blk, blk, pl.BlockSpec((N_EXPERTS, LANES), lambda i: (0, 0))],
        out_shape=[jax.ShapeDtypeStruct((SUBLANES, n), jnp.int32), jax.ShapeDtypeStruct((SUBLANES, n), F32),
                   jax.ShapeDtypeStruct((SUBLANES, n), jnp.int32), jax.ShapeDtypeStruct((N_EXPERTS, LANES), F32)],
        scratch_shapes=[pltpu.VMEM((N_EXPERTS, LANES), F32)],
        compiler_params=_cparams("arbitrary"),
        name="moe_route",
    )(logits_t, tri)


def _expert_kernel(be_ref, nb_ref, x_ref, wg_ref, wu_ref, wd_ref, o_ref, wgb_ref, wub_ref, wdb_ref):
    i = pl.program_id(0)
    half = D_MODEL // 2

    @pl.when(i < nb_ref[0])
    def _():
        @pl.when((i == 0) | (be_ref[i] != be_ref[jnp.maximum(i - 1, 0)]))
        def _():
            wgb_ref[...] = wg_ref[...].astype(BF16)
            wub_ref[...] = wu_ref[...].astype(BF16)
            wdb_ref[...] = wd_ref[...].astype(BF16)

        xa, xb = _unpack_rows(x_ref[...])
        xa, xb = xa.astype(BF16), xb.astype(BF16)
        g = _dot(xa, wgb_ref[0:half, :]) + _dot(xb, wgb_ref[half:, :])
        u = _dot(xa, wub_ref[0:half, :]) + _dot(xb, wub_ref[half:, :])
        o_ref[...] = _pack_rows(_dot(((g * _sigmoid(g)) * u).astype(BF16), wdb_ref[...]))

    @pl.when(i >= nb_ref[0])
    def _():
        o_ref[...] = jnp.zeros(o_ref.shape, o_ref.dtype)


def _experts(blk_expert, n_used, buf, w_gate, w_up, w_down, layer):
    cap, width = buf.shape
    blk = MOE_BLK
    row = lambda i, be, nb: (jnp.minimum(i, nb[0] - 1), 0)
    wmap = lambda i, be, nb: (layer, be[jnp.minimum(i, nb[0] - 1)], 0, 0)
    return pl.pallas_call(
        _expert_kernel,
        grid_spec=pltpu.PrefetchScalarGridSpec(
            num_scalar_prefetch=2,
            grid=(cap // blk,),
            in_specs=[pl.BlockSpec((blk, width), row),
                      pl.BlockSpec((None, None, D_MODEL, EXPERT_FF), wmap),
                      pl.BlockSpec((None, None, D_MODEL, EXPERT_FF), wmap),
                      pl.BlockSpec((None, None, EXPERT_FF, D_MODEL), wmap)],
            out_specs=pl.BlockSpec((blk, width), lambda i, be, nb: (i, 0)),
            scratch_shapes=[pltpu.VMEM((D_MODEL, EXPERT_FF), BF16), pltpu.VMEM((D_MODEL, EXPERT_FF), BF16),
                            pltpu.VMEM((EXPERT_FF, D_MODEL), BF16)],
        ),
        out_shape=jax.ShapeDtypeStruct((cap, width), buf.dtype),
        compiler_params=_cparams("arbitrary"),
        name="moe_experts",
    )(blk_expert, n_used, buf, w_gate, w_up, w_down)


SC_CH = 64


def _sc_mesh():
    return plsc.VectorSubcoreMesh(core_axis_name="c", subcore_axis_name="s")


def _sc_worker(info):
    return lax.axis_index("s") * info.num_cores + lax.axis_index("c")


def _sc_gather_rows(table, idx, rows=None):
    info = plsc.get_sparse_core_info()
    lanes, workers = info.num_lanes, info.num_cores * info.num_subcores
    invert = rows is not None
    rows = rows if invert else idx.shape[0]
    width = table.shape[1]
    per_w = rows // workers
    n_chunks = per_w // SC_CH
    assert per_w % (2 * SC_CH) == 0
    scan = 2048
    n_tok = table.shape[0]

    def body(table_hbm, idx_hbm, out_hbm, idx_v, rows_a, rows_b, sem_a, sem_b, *scan_v):
        base = _sc_worker(info) * per_w
        if invert:
            lane = lax.iota(jnp.int32, lanes)

            @pl.loop(0, per_w // lanes)
            def _(j):
                idx_v[pl.ds(j * lanes, lanes)] = (base + j * lanes + lane) & (n_tok - 1)

            @pl.loop(0, idx.shape[0] // scan)
            def _(c):
                pltpu.sync_copy(idx_hbm.at[pl.ds(c * scan, scan)], scan_v[0])

                @plsc.parallel_loop(0, scan // lanes, unroll=8)
                def _(j):
                    rel = scan_v[0][pl.ds(j * lanes, lanes)] - base
                    mine = (rel >= 0) & (rel < per_w)
                    token = (c * scan + j * lanes + lane) & (n_tok - 1)
                    plsc.store_scatter(idx_v, [jnp.where(mine, rel, 0)], token, mask=mine)
        else:
            pltpu.sync_copy(idx_hbm.at[pl.ds(base, per_w)], idx_v)

        def gather(j, buf, sem):
            return pltpu.make_async_copy(table_hbm.at[idx_v.at[pl.ds(j * SC_CH, SC_CH)]], buf, sem)

        def write(j, buf):
            pltpu.sync_copy(buf, out_hbm.at[pl.ds(base + j * SC_CH, SC_CH)])

        gather(0, rows_a, sem_a).start()

        @pl.loop(0, n_chunks, step=2)
        def _(j):
            gather(j + 1, rows_b, sem_b).start()
            gather(j, rows_a, sem_a).wait()
            write(j, rows_a)

            @pl.when(j + 2 < n_chunks)
            def _():
                gather(j + 2, rows_a, sem_a).start()

            gather(j + 1, rows_b, sem_b).wait()
            write(j + 1, rows_b)

    chunk_buf = pltpu.VMEM((SC_CH, width), table.dtype)
    scratch = [pltpu.VMEM((per_w,), jnp.int32), chunk_buf, chunk_buf,
               pltpu.SemaphoreType.DMA, pltpu.SemaphoreType.DMA]
    return pl.kernel(
        body, out_type=jax.ShapeDtypeStruct((rows, width), table.dtype), mesh=_sc_mesh(),
        scratch_types=scratch + ([pltpu.VMEM((scan,), jnp.int32)] if invert else []),
        compiler_params=pltpu.CompilerParams(needs_layout_passes=False) if invert else None,
        name="sc_dispatch_rows" if invert else "sc_gather_rows",
    )(table, idx)


def _blend_kernel(x1_ref, w_ref, g0_ref, g1_ref, o_ref):
    half = x1_ref.shape[1] // 2
    a0, b0 = _unpack_rows(g0_ref[...])
    a1, b1 = _unpack_rows(g1_ref[...])
    w0, w1 = w_ref[:, 0:1], w_ref[:, 1:2]
    o_ref[:, 0:half] = x1_ref[:, 0:half] + w0 * a0 + w1 * a1
    o_ref[:, half:] = x1_ref[:, half:] + w0 * b0 + w1 * b1


def _blend(x1, w_col, gathered):
    n = x1.shape[0]
    tt = DSP_TT
    nt = n // tt
    width = gathered.shape[1]
    return pl.pallas_call(
        _blend_kernel,
        grid=(nt,),
        in_specs=[pl.BlockSpec((tt, D_MODEL), lambda i: (i, 0)),
                  pl.BlockSpec((tt, EXPERT_TOP_K), lambda i: (i, 0)),
                  pl.BlockSpec((tt, width), lambda i: (i, 0)),
                  pl.BlockSpec((tt, width), lambda i: (nt + i, 0))],
        out_specs=pl.BlockSpec((tt, D_MODEL), lambda i: (i, 0)),
        out_shape=jax.ShapeDtypeStruct((n, D_MODEL), F32),
        compiler_params=_cparams("parallel"),
        name="moe_blend",
    )(x1, w_col, gathered, gathered)


def _moe(x1, h2, logits_t, w_gate, w_up, w_down, layer):
    n = x1.shape[0]
    e_idx, wts, rank, counts = _route(logits_t)
    counts = counts[:, 0].astype(jnp.int32)
    padded = (counts + MOE_BLK - 1) // MOE_BLK * MOE_BLK
    pends = jnp.cumsum(padded)
    pstarts = pends - padded
    experts = jnp.arange(N_EXPERTS, dtype=jnp.int32)
    start_of = jnp.sum(jnp.where(e_idx[None, :EXPERT_TOP_K] == experts[:, None, None],
                                 pstarts[:, None, None], 0), axis=0)
    dest = start_of + rank[:EXPERT_TOP_K]
    cap = n * EXPERT_TOP_K + N_EXPERTS * MOE_BLK
    n_blk = cap // MOE_BLK
    blk_row = jnp.arange(n_blk, dtype=jnp.int32) * MOE_BLK
    blk_expert = jnp.minimum(jnp.sum((pends[None, :] <= blk_row[:, None]).astype(jnp.int32), axis=1),
                             N_EXPERTS - 1)
    n_used = (pends[-1:] // MOE_BLK).astype(jnp.int32)
    dest_flat = dest.reshape(EXPERT_TOP_K * n)
    buf = _sc_gather_rows(h2, dest_flat, rows=cap)
    out_buf = _experts(blk_expert, n_used, buf, w_gate, w_up, w_down, layer)
    return _blend(x1, wts[:EXPERT_TOP_K].T, _sc_gather_rows(out_buf, dest_flat))


def kernel(x, positions, norm_mix, norm_ffn, w_in, q_gain, k_gain, cmp_k_pos, cmp_k_w1, cmp_k_w2, cmp_v_pos,
           cmp_v_w1, cmp_v_w2, s5_a_re, s5_a_im, s5_log_dt, s5_b_re, s5_b_im, s5_c_re, s5_c_im, s5_d, s5_glu,
           proj_nsa, proj_s5, w_out, router_group, router_group_bias, router_expert, router_expert_bias,
           expert_gate, expert_up, expert_down):
    batch, seq, _ = x.shape
    depth = w_in.shape[0]
    n = batch * seq
    inv_freq = ROPE_THETA ** (-jnp.arange(0, ROPE_DIMS, 2, dtype=jnp.float32) / ROPE_DIMS)
    ctab, stab = _rope_tables(positions.reshape(n, 1), inv_freq)
    ccmp, scmp = _cmp_rope_tables(ctab, stab, batch, seq)
    x2 = x.reshape(n, D_MODEL)
    for l in range(depth):
        w1, w_ga, w_gs = _pack_w_in(w_in[l])
        o_a, u_tm = _nsa_branch(x2, norm_mix[l][None], w1, ctab, stab, ccmp, scmp, q_gain[l], k_gain[l],
                                (cmp_k_pos[l], cmp_k_w1[l], cmp_k_w2[l]),
                                (cmp_v_pos[l], cmp_v_w1[l], cmp_v_w2[l]), batch, seq)
        lre, lim, w_b, w_c = _s5_weights(s5_a_re[l], s5_a_im[l], s5_log_dt[l], s5_b_re[l], s5_b_im[l],
                                         s5_c_re[l], s5_c_im[l])
        ob_tm = _s5(u_tm, lre, lim, w_b, w_c, s5_d[l][None],
                    s5_glu[l].astype(BF16), batch, seq)
        wrh, wrl, rb = _router_weights(router_group[l], router_group_bias[l], router_expert[l],
                                       router_expert_bias[l])
        x1, h2, logits_t = _merge(x2, norm_mix[l][None], w_ga, w_gs, o_a, ob_tm, proj_nsa[l].astype(BF16),
                                  proj_s5[l].astype(BF16), w_out[l].astype(BF16), norm_ffn[l][None],
                                  wrh, wrl, rb, batch, seq)
        x2 = _moe(x1, h2, logits_t, expert_gate, expert_up, expert_down, l)
    return x2.reshape(batch, seq, D_MODEL)


def _attn_tables(seq):
    n_cmp = (seq - CMP_BLOCK) // CMP_STRIDE + 1
    n_slc = seq // SLC_BLOCK
    key_blk = np.arange(seq) // SLC_BLOCK
    expand_t = (key_blk[:, None] == np.arange(LANES)[None, :]).astype(np.float32)
    cmp_start = np.arange(LANES) * CMP_STRIDE
    slc_start = np.arange(LANES) * SLC_BLOCK
    overlap_t = ((cmp_start[None, :] < slc_start[:, None] + SLC_BLOCK)
                 & (cmp_start[None, :] + CMP_BLOCK > slc_start[:, None])
                 & (np.arange(LANES)[None, :] < n_cmp) & (np.arange(LANES)[:, None] < n_slc))
    return (jnp.asarray(expand_t, BF16), jnp.asarray(overlap_t.astype(np.float32), BF16), n_cmp, n_slc)


def _attn_mask_tables():
    tq, tk, wk = ATT_TQ, ATT_TK, ATT_WK
    ql = np.arange(tq)
    onehot = np.tile(np.eye(tq, dtype=np.float32), (NSA_HEADS, 1))
    k = np.arange(wk)[:, None]
    band = []
    for c in range(WINDOW // tq + 1):
        if c < WINDOW // tq:
            keep = k <= c * tq + ql[None, :]
        else:
            keep = (k > ql[None, :]) & (k <= ql[None, :] + WINDOW)
        band.append(np.where(keep, 0.0, NEG_BIG))
    kl = np.arange(tk)[None, :]
    causal = [np.where(kl <= c * tq + ql[:, None], 0.0, NEG_BIG) for c in range(tk // tq)]
    return (jnp.asarray(onehot, BF16), jnp.asarray(np.stack(band), BF16),
            jnp.asarray(np.stack(causal), F32))


def _attention(q, ks, vs, kw, vw, kc, vc, gates, batch, seq):
    expand_t, overlap_t, n_cmp, n_slc = _attn_tables(seq)
    onehot, band, causal = _attn_mask_tables()
    assert ATT_TK == WINDOW
    tq = ATT_TQ
    nq = seq // tq
    rows = NSA_HEADS * tq
    tile = lambda w: pl.BlockSpec((tq, w), lambda b, i: (b * nq + i, 0))
    per_b = lambda r: pl.BlockSpec((r, LANES), lambda b, i: (b, 0))
    f32_rows = pltpu.VMEM((rows, LANES), F32)
    return pl.pallas_call(
        functools.partial(_attn_kernel, n_cmp=n_cmp, n_slc=n_slc, top_k=min(SLC_TOPK, n_slc)),
        grid=(batch, nq),
        in_specs=[tile(2 * NSA_WIDTH), per_b(seq), per_b(seq), per_b(seq), per_b(seq),
                  per_b(LANES), per_b(LANES), tile(LANES),
                  pl.BlockSpec(expand_t.shape, lambda b, i: (0, 0)),
                  pl.BlockSpec(overlap_t.shape, lambda b, i: (0, 0)),
                  pl.BlockSpec(onehot.shape, lambda b, i: (0, 0)),
                  pl.BlockSpec(band.shape, lambda b, i: (0, 0, 0)),
                  pl.BlockSpec(causal.shape, lambda b, i: (0, 0, 0))],
        out_specs=tile(NSA_WIDTH),
        out_shape=jax.ShapeDtypeStruct((batch * seq, NSA_WIDTH), BF16),
        scratch_shapes=[pltpu.VMEM((rows, 2 * LANES), BF16),
                        pltpu.VMEM((2, rows, ATT_TK), F32), pltpu.VMEM((2, rows, ATT_TK), BF16),
                        pltpu.VMEM((rows, ATT_WK), F32), pltpu.VMEM((rows, ATT_WK), BF16),
                        f32_rows, f32_rows, f32_rows, f32_rows, f32_rows],
        compiler_params=_cparams("parallel", "parallel"),
        name="nsa_attention",
    )(q, ks, vs, kw, vw, kc, vc, gates, expand_t, overlap_t, onehot, band, causal)
```

```python
import functools
import math

import jax
import jax.numpy as jnp
import numpy as np
from jax import lax
from jax.experimental import pallas as pl
from jax.experimental.pallas import tpu as pltpu
from jax.experimental.pallas import tpu_sc as plsc

F32 = jnp.float32
BF16 = jnp.bfloat16

D_MODEL = 1024
NSA_HEADS = 8
NSA_KV_HEADS = 2
NSA_GROUP = NSA_HEADS // NSA_KV_HEADS
HEAD_DIM = 64
ROPE_DIMS = HEAD_DIM // 4
ROPE_THETA = 500000.0
CMP_BLOCK = 32
CMP_STRIDE = 16
CMP_HIDDEN = 2 * HEAD_DIM
SLC_BLOCK = 64
SLC_TOPK = 16
WINDOW = 512
NSA_WIDTH = NSA_HEADS * HEAD_DIM
KV_WIDTH = NSA_KV_HEADS * HEAD_DIM
S5_WIDTH = 256
S5_GROUP_CH = 16
S5_GROUPS = S5_WIDTH // S5_GROUP_CH
S5_STATE = 64
S5_NSTATE = S5_GROUPS * S5_STATE
N_EXPERT_GROUPS = 4
EXPERTS_PER_GROUP = 8
N_EXPERTS = N_EXPERT_GROUPS * EXPERTS_PER_GROUP
EXPERT_TOP_K = 2
EXPERT_FF = 512
RMS_EPS = 1e-6
NEG_BIG = -1e30

LANES = 128
SUBLANES = 8
VMEM_LIMIT = 56 * 1024 * 1024

IN_TM = 512
ATT_TQ = 128
ATT_TK = 512
ATT_WK = WINDOW + ATT_TQ
ATT_RB = 128
S5_T = 256
MRG_TM = 512
RT_TT = 2048
MOE_BLK = 512
DSP_TT = 1024

C_Q, C_KS, C_KW, C_KC, C_VC, C_VS, C_VW, C_U, C_G = 0, 512, 640, 768, 896, 1024, 1152, 1280, 1536
IN_W = 1664


def _cparams(*sem):
    return pltpu.CompilerParams(dimension_semantics=sem, vmem_limit_bytes=VMEM_LIMIT)


def _gelu(x):
    return 0.5 * x * (1.0 + jnp.tanh(math.sqrt(2.0 / math.pi) * (x + 0.044715 * (x * x * x))))


def _sigmoid(x):
    return 1.0 / (1.0 + jnp.exp(-x))


def _dot(a, b):
    return jnp.dot(a, b, preferred_element_type=F32)


def _dot_t(a, b):
    return lax.dot_general(a, b, (((1,), (1,)), ((), ())), preferred_element_type=F32)


def _split_bf16(x):
    hi = x.astype(BF16)
    lo = (x - hi.astype(F32)).astype(BF16)
    return hi, lo


def _pack_rows(x):
    k = x.shape[1] // 2
    hi = lax.bitcast_convert_type(x[:, :k].astype(jnp.bfloat16).astype(F32), jnp.int32)
    lo = lax.bitcast_convert_type(x[:, k:].astype(jnp.bfloat16).astype(F32), jnp.int32)
    return hi | lax.shift_right_logical(lo, 16)


def _unpack_rows(w):
    a = lax.bitcast_convert_type(w & jnp.int32(-65536), F32)
    b = lax.bitcast_convert_type(lax.shift_left(w, 16), F32)
    return a, b


def _rope_kernel(pos_ref, freq_ref, sign_ref, c_ref, s_ref):
    ang = pos_ref[...].astype(F32) * freq_ref[...]
    c_ref[...] = jnp.cos(ang)
    s_ref[...] = jnp.sin(ang) * sign_ref[...]


def _rope_tables(pos_col, inv_freq):
    n = pos_col.shape[0]
    lane = np.arange(LANES) % HEAD_DIM
    sel = np.zeros((LANES, ROPE_DIMS // 2), np.float32)
    for l in range(LANES):
        if lane[l] < ROPE_DIMS:
            sel[l, lane[l] % (ROPE_DIMS // 2)] = 1.0
    freq = (jnp.asarray(sel) * inv_freq[None, :]).sum(-1)[None, :]
    sign = np.where(lane < ROPE_DIMS // 2, -1.0, 1.0).astype(np.float32)[None, :]
    tm = 1024
    return pl.pallas_call(
        _rope_kernel,
        grid=(n // tm,),
        in_specs=[pl.BlockSpec((tm, 1), lambda i: (i, 0)),
                  pl.BlockSpec((1, LANES), lambda i: (0, 0)),
                  pl.BlockSpec((1, LANES), lambda i: (0, 0))],
        out_specs=[pl.BlockSpec((tm, LANES), lambda i: (i, 0))] * 2,
        out_shape=[jax.ShapeDtypeStruct((n, LANES), F32)] * 2,
        compiler_params=_cparams("parallel"),
        name="rope_tables",
    )(pos_col, freq, jnp.asarray(sign))


def _headnorm_rope(t, gain, c, s):
    lane = lax.broadcasted_iota(jnp.int32, t.shape, 1)
    lo = lane < HEAD_DIM
    sq = t * t
    s_lo = jnp.sum(jnp.where(lo, sq, 0.0), axis=-1, keepdims=True)
    s_hi = jnp.sum(jnp.where(lo, 0.0, sq), axis=-1, keepdims=True)
    ms = jnp.where(lo, s_lo, s_hi) * (1.0 / HEAD_DIM)
    tn = (t * lax.rsqrt(ms + RMS_EPS)) * gain
    half = ROPE_DIMS // 2
    partner = jnp.where((lane & (HEAD_DIM - 1)) < half,
                        pltpu.roll(tn, LANES - half, 1), pltpu.roll(tn, half, 1))
    return tn * c + partner * s


def _blend_rows(x1, w, g0, g1):
    half = x1.shape[1] // 2
    a0, b0 = _unpack_rows(g0)
    a1, b1 = _unpack_rows(g1)
    w0, w1 = w[:, 0:1], w[:, 1:2]
    return jnp.concatenate([x1[:, 0:half] + w0 * a0 + w1 * a1, x1[:, half:] + w0 * b0 + w1 * b1], axis=1)


def _in_proj_kernel(*refs, blend):
    if blend:
        x1_ref, wc_ref, g0_ref, g1_ref, *refs = refs
    else:
        x_in_ref, *refs = refs
    gain_ref, w_ref, c_ref, s_ref, qg_ref, kg_ref, *refs = refs
    q_ref, ks_ref, kw_ref, kc_ref, vc_ref, vs_ref, vw_ref, u_ref, g_ref, *refs = refs
    if blend:
        x_out_ref, raw_ref = refs
        x = _blend_rows(x1_ref[...], wc_ref[...], g0_ref[...], g1_ref[...])
        x_out_ref[...] = x
    else:
        (raw_ref,) = refs
        x = x_in_ref[...]
    ms = jnp.mean(x * x, axis=-1, keepdims=True)
    h = (x * lax.rsqrt(ms + RMS_EPS)) * gain_ref[...]
    z = _dot(h.astype(BF16), w_ref[...])
    c = c_ref[...]
    s = s_ref[...]
    lane = lax.broadcasted_iota(jnp.int32, c.shape, 1)
    lo = lane < HEAD_DIM
    scale = HEAD_DIM ** -0.5 * math.log2(math.e)
    for j in range(NSA_HEADS // 2):
        r = _headnorm_rope(z[:, C_Q + LANES * j:C_Q + LANES * (j + 1)], qg_ref[...], c, s) * scale
        rr = pltpu.roll(r, HEAD_DIM, 1)
        if j // (NSA_GROUP // 2) == 0:
            even, odd = jnp.where(lo, r, 0.0), jnp.where(lo, rr, 0.0)
        else:
            even, odd = jnp.where(lo, 0.0, rr), jnp.where(lo, 0.0, r)
        q_ref[:, 2 * LANES * j:2 * LANES * j + LANES] = even.astype(BF16)
        q_ref[:, 2 * LANES * j + LANES:2 * LANES * (j + 1)] = odd.astype(BF16)
    ks_ref[...] = _headnorm_rope(z[:, C_KS:C_KS + LANES], kg_ref[...], c, s).astype(BF16)
    kw_ref[...] = _headnorm_rope(z[:, C_KW:C_KW + LANES], kg_ref[...], c, s).astype(BF16)
    half = CMP_BLOCK // 2
    for col, out in ((C_KC, kc_ref), (C_VC, vc_ref)):
        raw_ref[...] = z[:, col:col + LANES]
        for l in range(half):
            piece = raw_ref[pl.ds(l, x.shape[0] // half, stride=half), :]
            out[:, LANES * l:LANES * (l + 1)] = piece.astype(BF16)
    vs_ref[...] = z[:, C_VS:C_VS + LANES].astype(BF16)
    vw_ref[...] = z[:, C_VW:C_VW + LANES].astype(BF16)
    u_ref[:, pl.ds(pl.program_id(1), 1), :] = z[:, C_U:C_U + S5_WIDTH][:, None, :]
    g_ref[...] = _sigmoid(z[:, C_G:C_G + LANES])


def _in_proj(x_in, gain, w1, ctab, stab, qg, kg, batch, seq):
    blend = isinstance(x_in, tuple)
    n = batch * seq
    tm = IN_TM
    nt = seq // tm
    row = lambda i, b: (b * nt + i, 0)
    const = lambda i, b: (0, 0)
    tok = lambda w: pl.BlockSpec((tm, w), row)
    half = CMP_BLOCK // 2
    grouped = pl.BlockSpec((tm // half, half * KV_WIDTH), row)
    rows_bf16 = lambda r, w: jax.ShapeDtypeStruct((r, w), BF16)
    if blend:
        x1, w_col, gathered = x_in
        width = gathered.shape[1]
        x_specs = [tok(D_MODEL), tok(EXPERT_TOP_K), tok(width),
                   pl.BlockSpec((tm, width), lambda i, b: (n // tm + b * nt + i, 0))]
        x_args = (x1, w_col, gathered, gathered)
    else:
        x_specs, x_args = [tok(D_MODEL)], (x_in,)
    return pl.pallas_call(
        functools.partial(_in_proj_kernel, blend=blend),
        grid=(nt, batch),
        in_specs=x_specs + [pl.BlockSpec((1, D_MODEL), const), pl.BlockSpec((D_MODEL, IN_W), const),
                            tok(LANES), tok(LANES), pl.BlockSpec((1, LANES), const),
                            pl.BlockSpec((1, LANES), const)],
        out_specs=[tok(2 * NSA_WIDTH), tok(LANES), tok(LANES), grouped, grouped, tok(LANES), tok(LANES),
                   pl.BlockSpec((tm, batch, S5_WIDTH), lambda i, b: (i, 0, 0)), tok(LANES)]
        + ([tok(D_MODEL)] if blend else []),
        out_shape=[rows_bf16(n, 2 * NSA_WIDTH), rows_bf16(n, LANES), rows_bf16(n, LANES),
                   rows_bf16(n // half, half * KV_WIDTH), rows_bf16(n // half, half * KV_WIDTH),
                   rows_bf16(n, LANES), rows_bf16(n, LANES),
                   jax.ShapeDtypeStruct((seq, batch, S5_WIDTH), F32), jax.ShapeDtypeStruct((n, LANES), F32)]
        + ([jax.ShapeDtypeStruct((n, D_MODEL), F32)] if blend else []),
        scratch_shapes=[pltpu.VMEM((tm, LANES), F32)],
        compiler_params=_cparams("parallel", "arbitrary"),
        name="in_proj",
    )(*x_args, gain, w1, ctab, stab, qg, kg)


def _compress_mlp(h_ref, w_ref, p_ref, w2_ref):
    w = w_ref[...]
    pre = _dot(h_ref[...], w)
    phi, plo = _split_bf16(p_ref[...])
    pb = _dot(phi, w) + _dot(plo, w)
    rows = pre.shape[0]
    out = None
    for h in range(NSA_KV_HEADS):
        top = pre[:, LANES * h:LANES * (h + 1)]
        bot = pre[:, LANES * (2 + h):LANES * (3 + h)]
        bias = pb[0:1, LANES * h:LANES * (h + 1)] + pb[1:2, LANES * (2 + h):LANES * (3 + h)]
        a = _gelu(top + pltpu.roll(bot, rows - 1, 0) + bias)
        o = _dot(a.astype(BF16), w2_ref[h])
        out = o if out is None else out + o
    return out


def _compress_kernel(hk_ref, hv_ref, wk_ref, wv_ref, pk_ref, pv_ref, w2k_ref, w2v_ref,
                     kg_ref, c_ref, s_ref, kc_ref, vc_ref):
    k = _compress_mlp(hk_ref, wk_ref, pk_ref, w2k_ref)
    kc_ref[...] = _headnorm_rope(k, kg_ref[...], c_ref[...], s_ref[...]).astype(BF16)
    vc_ref[...] = _compress_mlp(hv_ref, wv_ref, pv_ref, w2v_ref).astype(BF16)


def _compress(hk, hv, wk, wv, pk, pv, w2k, w2v, kg, ccmp, scmp, batch):
    rows = hk.shape[0] // batch
    width = hk.shape[1]
    const2 = lambda b: (0, 0)
    const3 = lambda b: (0, 0, 0)
    rowb = lambda w: pl.BlockSpec((rows, w), lambda b: (b, 0))
    return pl.pallas_call(
        _compress_kernel,
        grid=(batch,),
        in_specs=[rowb(width), rowb(width),
                  pl.BlockSpec(wk.shape, const2), pl.BlockSpec(wv.shape, const2),
                  pl.BlockSpec(pk.shape, const2), pl.BlockSpec(pv.shape, const2),
                  pl.BlockSpec(w2k.shape, const3), pl.BlockSpec(w2v.shape, const3),
                  pl.BlockSpec((1, LANES), const2), rowb(LANES), rowb(LANES)],
        out_specs=[rowb(LANES), rowb(LANES)],
        out_shape=[jax.ShapeDtypeStruct((hk.shape[0], LANES), BF16)] * 2,
        compiler_params=_cparams("parallel"),
        name="compress_kv",
    )(hk, hv, wk, wv, pk, pv, w2k, w2v, kg, ccmp, scmp)


def _compress_weights(pos_emb, w1, w2):
    half = CMP_BLOCK // 2
    eye = jnp.eye(NSA_KV_HEADS, dtype=F32)
    w1r = w1.reshape(2, half, HEAD_DIM, CMP_HIDDEN).astype(BF16)
    zero = jnp.zeros_like(w1r[0])
    wexp = jnp.concatenate(
        [jnp.stack([w1r[s] if p == h else zero for p in range(NSA_KV_HEADS)], axis=1)
         .reshape(half * KV_WIDTH, CMP_HIDDEN) for s in range(2) for h in range(NSA_KV_HEADS)], axis=1)
    pexp = jnp.broadcast_to(pos_emb.reshape(2, half, 1, HEAD_DIM), (2, half, NSA_KV_HEADS, HEAD_DIM))
    pexp = jnp.pad(pexp.reshape(2, half * KV_WIDTH), ((0, SUBLANES - 2), (0, 0)))
    w2e = jnp.einsum("cd,ph->pchd", w2, eye).reshape(NSA_KV_HEADS, CMP_HIDDEN, KV_WIDTH)
    return wexp, pexp, w2e.astype(BF16)


def _attn_kernel(q_ref, ks_ref, vs_ref, kw_ref, vw_ref, kc_ref, vc_ref, g_ref, et_ref, ovt_ref, oh_ref, wb_ref,
                 cb_ref, o_ref, qc_ref, s_ref, p_ref, sw_ref, pw_ref, m_ref, al_ref, oc_ref, as_ref, aw_ref,
                 *, n_cmp, n_slc, top_k):
    tq, tk, wk, rb, nh = ATT_TQ, ATT_TK, ATT_WK, ATT_RB, NSA_HEADS
    rows, grp_rows = nh * tq, NSA_GROUP * tq
    qk_rows = 2 * tq
    q0 = pl.program_id(1) * tq
    for a in range(nh):
        qc_ref[a * tq:(a + 1) * tq, 0:LANES] = q_ref[:, LANES * a:LANES * (a + 1)]
    q8 = qc_ref[:, 0:LANES]
    lane = lax.broadcasted_iota(jnp.int32, (tq, LANES), 1)
    t_row = q0 + lax.broadcasted_iota(jnp.int32, (tq, LANES), 0)

    s_c = _dot_t(q8, kc_ref[...]).reshape(nh, tq, LANES)
    mask_c = ((lane * CMP_STRIDE + (CMP_BLOCK - 1)) <= t_row) & (lane < n_cmp)
    sm = jnp.where(mask_c[None], s_c, NEG_BIG)
    p = jnp.where(mask_c[None], jnp.exp2(sm - jnp.max(sm, axis=-1, keepdims=True)), 0.0)
    l = jnp.sum(p, axis=-1, keepdims=True)
    p_c = p * jnp.where(l > 0.0, 1.0 / l, 0.0)
    oc_ref[...] = _dot(p_c.reshape(rows, LANES).astype(BF16), vc_ref[...])

    blk = lax.broadcasted_iota(jnp.int32, (n_slc, tq), 0)
    cur = (q0 + lax.broadcasted_iota(jnp.int32, (n_slc, tq), 1)) >> int(math.log2(SLC_BLOCK))
    forced = (blk == 0) | (blk == cur) | (blk == cur - 1)
    for h in range(NSA_KV_HEADS):
        ps = p_c[NSA_GROUP * h]
        for g in range(1, NSA_GROUP):
            ps = ps + p_c[NSA_GROUP * h + g]
        hi, lo = _split_bf16(ps)
        imp = (_dot_t(ovt_ref[...], hi) + _dot_t(ovt_ref[...], lo))[0:n_slc]
        imp = jnp.where(forced, jnp.inf, jnp.where(blk > cur, -jnp.inf, imp))
        rank = jnp.zeros((n_slc, tq), F32)
        for i in range(n_slc):
            row = imp[i:i + 1, :]
            rank = rank + jnp.where((row > imp) | ((row == imp) & (blk > i)), 1.0, 0.0)
        bias_t = jnp.where(rank < float(top_k), 0.0, NEG_BIG)
        bias = jnp.concatenate([bias_t, jnp.zeros((LANES - n_slc, tq), F32)], axis=0).T.astype(BF16)
        for g in range(NSA_GROUP):
            a = NSA_GROUP * h + g
            qc_ref[a * tq:(a + 1) * tq, LANES:2 * LANES] = bias

    def softmax_pass(s_buf, p_buf, width, causal_case, online):
        for r in range(rows // rb):
            r0 = r * rb
            s = s_buf[pl.ds(r0, rb), 0:width]
            if causal_case is not None:
                s = s + cb_ref[causal_case, pl.ds(r0 % tq, rb), 0:width]
            m_new = jnp.broadcast_to(jnp.max(s, axis=1, keepdims=True), (rb, LANES))
            if online:
                m_old = m_ref[pl.ds(r0, rb), :]
                m_new = jnp.maximum(m_old, m_new)
                al_ref[pl.ds(r0, rb), :] = jnp.exp2(m_old - m_new)
                m_ref[pl.ds(r0, rb), :] = m_new
            pp = jnp.exp2(s - jnp.concatenate([m_new] * (width // LANES), axis=1))
            p_buf[pl.ds(r0, rb), 0:width] = pp.astype(BF16)

    def pv(acc_ref, p_buf, v, width, online):
        lane_v = lax.broadcasted_iota(jnp.int32, v.shape, 1)
        for h in range(NSA_KV_HEADS):
            rs = slice(h * grp_rows, (h + 1) * grp_rows)
            own = (lane_v < HEAD_DIM) if h == 0 else (lane_v >= HEAD_DIM)
            upd = _dot(p_buf[rs, 0:width], jnp.where(own, v, jnp.ones_like(v)))
            acc_ref[rs, :] = (al_ref[rs, :] * acc_ref[rs, :] + upd) if online else upd

    m_ref[...] = jnp.full(m_ref.shape, 0.1 * NEG_BIG, F32)
    as_ref[...] = jnp.zeros(as_ref.shape, F32)

    def slc_chunk(c, diagonal, width=tk):
        k0 = c * tk
        s_buf, p_buf = s_ref.at[c % 2], p_ref.at[c % 2]
        kcat = jnp.concatenate([ks_ref[pl.ds(k0, width), :], et_ref[pl.ds(k0, width), :]], axis=1)
        for g in range(rows // qk_rows):
            rs = slice(g * qk_rows, (g + 1) * qk_rows)
            s_buf[rs, 0:width] = _dot_t(qc_ref[rs, :], kcat)
        softmax_pass(s_buf, p_buf, width, ((q0 - k0) >> int(math.log2(tq))) if diagonal else None, True)
        pv(as_ref, p_buf, vs_ref[pl.ds(k0, width), :], width, True)

    def window_and_output():
        w0 = pl.multiple_of(jnp.maximum(q0 - WINDOW, 0), LANES)
        w_case = jnp.minimum(q0 >> int(math.log2(tq)), WINDOW // tq)
        kcat_w = jnp.concatenate([kw_ref[pl.ds(w0, wk), :], wb_ref[w_case]], axis=1)
        for g in range(rows // qk_rows):
            rs = slice(g * qk_rows, (g + 1) * qk_rows)
            sw_ref[rs, :] = _dot_t(jnp.concatenate([qc_ref[rs, 0:LANES], oh_ref[rs, :]], axis=1), kcat_w)
        softmax_pass(sw_ref, pw_ref, wk, None, False)
        pv(aw_ref, pw_ref, vw_ref[pl.ds(w0, wk), :], wk, False)

        lo_half = lane < HEAD_DIM
        for j in range(nh // 2):
            comb = []
            for a in (2 * j, 2 * j + 1):
                rs = slice(a * tq, (a + 1) * tq)
                o_s, o_w = as_ref[rs, :], aw_ref[rs, :]
                comb.append(g_ref[:, 3 * a:3 * a + 1] * oc_ref[rs, :]
                            + g_ref[:, 3 * a + 1:3 * a + 2] * (o_s * (1.0 / pltpu.roll(o_s, HEAD_DIM, 1)))
                            + g_ref[:, 3 * a + 2:3 * a + 3] * (o_w * (1.0 / pltpu.roll(o_w, HEAD_DIM, 1))))
            even, odd = comb
            if j // (NSA_GROUP // 2) == 0:
                pair = jnp.where(lo_half, even, pltpu.roll(odd, HEAD_DIM, 1))
            else:
                pair = jnp.where(lo_half, pltpu.roll(even, HEAD_DIM, 1), odd)
            o_ref[:, LANES * j:LANES * (j + 1)] = pair.astype(BF16)

    c_last = q0 >> int(math.log2(tk))
    early = ((q0 >> int(math.log2(tq))) & (tk // tq - 1)) < tk // tq // 2
    for n_full in range(ks_ref.shape[0] // tk):
        for first_half in (True, False):
            @pl.when((c_last == n_full) & (early if first_half else jnp.logical_not(early)))
            def _():
                for c in range(n_full):
                    slc_chunk(c, False)
                slc_chunk(n_full, True, tk // 2 if first_half else tk)
                window_and_output()


def _pack_w_in(w):
    sizes = (NSA_WIDTH,) + (KV_WIDTH,) * 6 + (3 * NSA_HEADS, S5_WIDTH, D_MODEL, D_MODEL)
    o = np.cumsum((0,) + sizes)
    q, kc, vc, ks, vs, kw, vw, g, u, ga, gs = [w[:, o[i]:o[i + 1]] for i in range(len(sizes))]
    g = jnp.pad(g, ((0, 0), (0, LANES - 3 * NSA_HEADS)))
    w1 = jnp.concatenate([q, ks, kw, kc, vc, vs, vw, u, g], axis=1).astype(BF16)
    return w1, ga.astype(BF16), gs.astype(BF16)


def _nsa_branch(x_in, gain, w1, ctab, stab, ccmp, scmp, q_gain, k_gain, cmp_k, cmp_v, batch, seq):
    qg = jnp.tile(q_gain, 2)[None, :]
    kg = jnp.tile(k_gain, 2)[None, :]
    q, ks, kw, hk, hv, vs, vw, u_tm, gates, *x_out = _in_proj(x_in, gain, w1, ctab, stab, qg, kg, batch, seq)
    wk, pk, w2k = _compress_weights(*cmp_k)
    wv, pv, w2v = _compress_weights(*cmp_v)
    kc, vc = _compress(hk, hv, wk, wv, pk, pv, w2k, w2v, kg, ccmp, scmp, batch)
    o_a = _attention(q, ks, vs, kw, vw, kc, vc, gates, batch, seq)
    return o_a, u_tm, (x_out[0] if x_out else x_in)


def _cmp_rope_tables(ctab, stab, batch, seq):
    def pick(t):
        t = t.reshape(batch, seq, LANES)[:, CMP_BLOCK - 1::CMP_STRIDE]
        t = jnp.pad(t, ((0, 0), (0, seq // CMP_STRIDE - t.shape[1]), (0, 0)))
        return t.reshape(batch * (seq // CMP_STRIDE), LANES)
    return pick(ctab), pick(stab)


def _s5_disc_kernel(are_ref, aim_ref, ldt_ref, lre_ref, lim_ref, fre_ref, fim_ref):
    lam_re = jnp.minimum(are_ref[...], -1e-4)
    lam_im = aim_ref[...]
    dt = jnp.exp(ldt_ref[...])
    mag = jnp.exp(lam_re * dt)
    ang = lam_im * dt
    lb_re = mag * jnp.cos(ang)
    lb_im = mag * jnp.sin(ang)
    den = lam_re * lam_re + lam_im * lam_im
    lre_ref[...] = lb_re
    lim_ref[...] = lb_im
    fre_ref[...] = ((lb_re - 1.0) * lam_re + lb_im * lam_im) / den
    fim_ref[...] = (lb_im * lam_re - (lb_re - 1.0) * lam_im) / den


def _s5_weights(a_re, a_im, log_dt, b_re, b_im, c_re, c_im):
    shp = jax.ShapeDtypeStruct((S5_GROUPS, S5_STATE), F32)
    lb_re, lb_im, f_re, f_im = pl.pallas_call(
        _s5_disc_kernel, out_shape=[shp] * 4, name="s5_discretise",
    )(a_re, a_im, log_dt[:, None])
    bb_re = f_re[..., None] * b_re - f_im[..., None] * b_im
    bb_im = f_re[..., None] * b_im + f_im[..., None] * b_re
    eye = jnp.eye(S5_GROUPS, dtype=F32)

    def in_map(bb):
        return jnp.einsum("gpc,gh->gchp", bb, eye).reshape(S5_WIDTH, S5_NSTATE)

    def out_map(cc):
        return jnp.einsum("gcp,gh->hpgc", cc, eye).reshape(S5_NSTATE, S5_WIDTH)

    w_b = jnp.concatenate([in_map(bb_re), in_map(bb_im)], axis=1).astype(BF16)
    w_c = jnp.concatenate([out_map(c_re), -out_map(c_im)], axis=0).astype(BF16)
    return lb_re.reshape(1, S5_NSTATE), lb_im.reshape(1, S5_NSTATE), w_b, w_c


def _s5_kernel(u_ref, wb_ref, lre_ref, lim_ref, wc_ref, d_ref, wg_ref, o_ref, x_ref, st_ref, *, batch):
    ns = S5_NSTATE

    @pl.when(pl.program_id(0) == 0)
    def _():
        st_ref[...] = jnp.zeros(st_ref.shape, F32)

    u = u_ref[...].reshape(u_ref.shape[0] * batch, S5_WIDTH)
    x_ref[...] = _dot(u.astype(BF16), wb_ref[...])
    lre = jnp.broadcast_to(lre_ref[...], (batch, ns))
    lim = jnp.broadcast_to(lim_ref[...], (batch, ns))

    def step(t, carry):
        xr, xi = carry
        r0 = pl.multiple_of(t * batch, batch)
        nr = lre * xr - lim * xi + x_ref[pl.ds(r0, batch), 0:ns]
        ni = lre * xi + lim * xr + x_ref[pl.ds(r0, batch), ns:2 * ns]
        x_ref[pl.ds(r0, batch), 0:ns] = nr
        x_ref[pl.ds(r0, batch), ns:2 * ns] = ni
        return nr, ni

    steps = u.shape[0] // batch
    xr, xi = lax.fori_loop(0, steps, step, (st_ref[:, 0:ns], st_ref[:, ns:2 * ns]), unroll=4)
    st_ref[:, 0:ns] = xr
    st_ref[:, ns:2 * ns] = xi
    y = _dot(x_ref[...].astype(BF16), wc_ref[...]) + d_ref[...] * u
    z = _dot(_gelu(y).astype(BF16), wg_ref[...])
    o = z[:, :S5_WIDTH] * _sigmoid(z[:, S5_WIDTH:])
    for c in range(S5_WIDTH // LANES):
        o_ref[c] = o[:, c * LANES:(c + 1) * LANES]


def _s5(u_tm, lre, lim, w_b, w_c, d_skip, w_glu, batch, seq):
    rows = S5_T * batch
    const = lambda t: (0, 0)
    return pl.pallas_call(
        functools.partial(_s5_kernel, batch=batch),
        grid=(seq // S5_T,),
        in_specs=[pl.BlockSpec((S5_T, batch, S5_WIDTH), lambda t: (t, 0, 0)),
                  pl.BlockSpec(w_b.shape, const), pl.BlockSpec(lre.shape, const), pl.BlockSpec(lim.shape, const),
                  pl.BlockSpec(w_c.shape, const), pl.BlockSpec((1, S5_WIDTH), const),
                  pl.BlockSpec(w_glu.shape, const)],
        out_specs=pl.BlockSpec((S5_WIDTH // LANES, rows, LANES), lambda t: (0, t, 0)),
        out_shape=jax.ShapeDtypeStruct((S5_WIDTH // LANES, seq * batch, LANES), F32),
        scratch_shapes=[pltpu.VMEM((rows, 2 * S5_NSTATE), F32), pltpu.VMEM((batch, 2 * S5_NSTATE), F32)],
        compiler_params=_cparams("arbitrary"),
        name="s5_scan",
    )(u_tm, w_b, lre, lim, w_c, d_skip, w_glu)


RT_GROUP_ROW = 0
RT_EXPERT_ROW = SUBLANES


def _rmsnorm(x, gain):
    return (x * lax.rsqrt(jnp.mean(x * x, axis=-1, keepdims=True) + RMS_EPS)) * gain


def _merge_kernel(x_ref, g1_ref, wga_ref, wgs_ref, oa_ref, ob_ref, pa_ref, ps_ref, wo_ref, g2_ref,
                  wrh_ref, wrl_ref, rb_ref, x1_ref, h2_ref, lg_ref):
    x = x_ref[...]
    h = _rmsnorm(x, g1_ref[...]).astype(BF16)
    gate_a = _sigmoid(_dot(h, wga_ref[...]))
    gate_s = _sigmoid(_dot(h, wgs_ref[...]))
    rows_b = pl.ds(pl.program_id(1), x.shape[0], stride=ob_ref.shape[1] // x.shape[0])
    o_b = jnp.concatenate([ob_ref[c, rows_b, :] for c in range(S5_WIDTH // LANES)], axis=1)
    merged = gate_a * _dot(oa_ref[...], pa_ref[...]) + gate_s * _dot(o_b.astype(BF16), ps_ref[...])
    x1 = x + _dot(merged.astype(BF16), wo_ref[...])
    x1_ref[...] = x1
    h2 = _rmsnorm(x1, g2_ref[...])
    h2_ref[...] = _pack_rows(h2)
    hi, lo = _split_bf16(h2)
    lg_ref[...] = (_dot_t(wrh_ref[...], hi) + _dot_t(wrh_ref[...], lo) + _dot_t(wrl_ref[...], hi)
                   + rb_ref[...])


def _router_weights(w_rg, b_rg, w_re, b_re):
    def rows(group_part, expert_part):
        return jnp.concatenate([
            group_part, jnp.zeros((RT_EXPERT_ROW - N_EXPERT_GROUPS,) + group_part.shape[1:], F32),
            expert_part, jnp.zeros((LANES - RT_EXPERT_ROW - N_EXPERTS,) + group_part.shape[1:], F32)], axis=0)

    hi, lo = _split_bf16(rows(w_rg.T, w_re.T))
    return hi, lo, rows(b_rg[:, None], b_re[:, None])


def _merge(x2, g1, w_ga, w_gs, o_a, ob_tm, p_a, p_s, w_o, g2, wrh, wrl, rb, batch, seq):
    n = x2.shape[0]
    tm = MRG_TM
    nt = seq // tm
    row = lambda i, b: (b * nt + i, 0)
    const = lambda i, b: (0, 0)
    tok = lambda w: pl.BlockSpec((tm, w), row)
    full = lambda a: pl.BlockSpec(a.shape, const)
    return pl.pallas_call(
        _merge_kernel,
        grid=(nt, batch),
        in_specs=[tok(D_MODEL), full(g1), full(w_ga), full(w_gs), tok(NSA_WIDTH),
                  pl.BlockSpec((S5_WIDTH // LANES, tm * batch, LANES), lambda i, b: (0, i, 0)),
                  full(p_a), full(p_s), full(w_o),
                  full(g2), full(wrh), full(wrl), full(rb)],
        out_specs=[tok(D_MODEL), tok(D_MODEL // 2), pl.BlockSpec((LANES, tm), lambda i, b: (0, b * nt + i))],
        out_shape=[jax.ShapeDtypeStruct((n, D_MODEL), F32), jax.ShapeDtypeStruct((n, D_MODEL // 2), jnp.int32),
                   jax.ShapeDtypeStruct((LANES, n), F32)],
        compiler_params=_cparams("parallel", "parallel"),
        name="merge_out_proj",
    )(x2, g1, w_ga, w_gs, o_a, ob_tm, p_a, p_s, w_o, g2, wrh, wrl, rb)


def _first_argmax(v, rows):
    vmax = jnp.max(v, axis=0, keepdims=True)
    idx = jnp.min(jnp.where(v == vmax, rows, SUBLANES), axis=0, keepdims=True)
    return vmax, idx


def _route_kernel(lg_ref, tri_ref, e_ref, w_ref, r_ref, cnt_ref, carry_ref):
    tt = lg_ref.shape[1]

    @pl.when(pl.program_id(0) == 0)
    def _():
        carry_ref[...] = jnp.zeros(carry_ref.shape, F32)

    rows = lax.broadcasted_iota(jnp.int32, (SUBLANES, tt), 0)
    gl = jnp.where(rows < N_EXPERT_GROUPS, lg_ref[RT_GROUP_ROW:RT_GROUP_ROW + SUBLANES, :], -jnp.inf)
    gmax, g_sel = _first_argmax(gl, rows)
    g_w = 1.0 / jnp.sum(jnp.exp(gl - gmax), axis=0, keepdims=True)

    el = jnp.zeros((EXPERTS_PER_GROUP, tt), F32)
    for g in range(N_EXPERT_GROUPS):
        r0 = RT_EXPERT_ROW + EXPERTS_PER_GROUP * g
        el = jnp.where(g_sel == g, lg_ref[r0:r0 + EXPERTS_PER_GROUP, :], el)
    ex = jnp.exp(el - jnp.max(el, axis=0, keepdims=True))
    p = ex / jnp.sum(ex, axis=0, keepdims=True)
    p1, i1 = _first_argmax(p, rows)
    p2, i2 = _first_argmax(jnp.where(rows == i1, -1.0, p), rows)
    e1 = g_sel * EXPERTS_PER_GROUP + i1
    e2 = g_sel * EXPERTS_PER_GROUP + i2
    scale = g_w / (p1 + p2)

    erow = lax.broadcasted_iota(jnp.int32, (N_EXPERTS, tt), 0)
    oh1 = jnp.where(erow == e1, 1.0, 0.0)
    oh2 = jnp.where(erow == e2, 1.0, 0.0)
    both = oh1 + oh2
    before = carry_ref[:, 0:1] + _dot(both.astype(BF16), tri_ref[...])
    carry_ref[...] = carry_ref[...] + jnp.sum(both, axis=1, keepdims=True)
    cnt_ref[...] = carry_ref[...]

    zi = jnp.zeros((SUBLANES - 2, tt), jnp.int32)
    e_ref[...] = jnp.concatenate([e1, e2, zi], axis=0)
    w_ref[...] = jnp.concatenate([p1 * scale, p2 * scale, zi.astype(F32)], axis=0)
    r_ref[...] = jnp.concatenate([jnp.sum(oh1 * before, axis=0, keepdims=True),
                                  jnp.sum(oh2 * before, axis=0, keepdims=True),
                                  zi.astype(F32)], axis=0).astype(jnp.int32)


def _route(logits_t):
    n = logits_t.shape[1]
    tt = RT_TT
    tri = jnp.asarray(np.triu(np.ones((tt, tt), np.float32), 1), BF16)
    blk = pl.BlockSpec((SUBLANES, tt), lambda i: (0, i))
    return pl.pallas_call(
        _route_kernel,
        grid=(n // tt,),
        in_specs=[pl.BlockSpec((LANES, tt), lambda i: (0, i)), pl.BlockSpec((tt, tt), lambda i: (0, 0))],
        out_specs=[blk, blk, blk, pl.BlockSpec((N_EXPERTS, LANES), lambda i: (0, 0))],
        out_shape=[jax.ShapeDtypeStruct((SUBLANES, n), jnp.int32), jax.ShapeDtypeStruct((SUBLANES, n), F32),
                   jax.ShapeDtypeStruct((SUBLANES, n), jnp.int32), jax.ShapeDtypeStruct((N_EXPERTS, LANES), F32)],
        scratch_shapes=[pltpu.VMEM((N_EXPERTS, LANES), F32)],
        compiler_params=_cparams("arbitrary"),
        name="moe_route",
    )(logits_t, tri)


def _expert_kernel(be_ref, nb_ref, x_ref, wg_ref, wu_ref, wd_ref, o_ref, wgb_ref, wub_ref, wdb_ref):
    i = pl.program_id(0)
    half = D_MODEL // 2

    @pl.when(i < nb_ref[0])
    def _():
        @pl.when((i == 0) | (be_ref[i] != be_ref[jnp.maximum(i - 1, 0)]))
        def _():
            wgb_ref[...] = wg_ref[...].astype(BF16)
            wub_ref[...] = wu_ref[...].astype(BF16)
            wdb_ref[...] = wd_ref[...].astype(BF16)

        xa, xb = _unpack_rows(x_ref[...])
        xa, xb = xa.astype(BF16), xb.astype(BF16)
        g = _dot(xa, wgb_ref[0:half, :]) + _dot(xb, wgb_ref[half:, :])
        u = _dot(xa, wub_ref[0:half, :]) + _dot(xb, wub_ref[half:, :])
        o_ref[...] = _pack_rows(_dot(((g * _sigmoid(g)) * u).astype(BF16), wdb_ref[...]))

    @pl.when(i >= nb_ref[0])
    def _():
        o_ref[...] = jnp.zeros(o_ref.shape, o_ref.dtype)


def _experts(blk_expert, n_used, buf, w_gate, w_up, w_down, layer):
    cap, width = buf.shape
    blk = MOE_BLK
    row = lambda i, be, nb: (jnp.minimum(i, nb[0] - 1), 0)
    wmap = lambda i, be, nb: (layer, be[jnp.minimum(i, nb[0] - 1)], 0, 0)
    return pl.pallas_call(
        _expert_kernel,
        grid_spec=pltpu.PrefetchScalarGridSpec(
            num_scalar_prefetch=2,
            grid=(cap // blk,),
            in_specs=[pl.BlockSpec((blk, width), row),
                      pl.BlockSpec((None, None, D_MODEL, EXPERT_FF), wmap),
                      pl.BlockSpec((None, None, D_MODEL, EXPERT_FF), wmap),
                      pl.BlockSpec((None, None, EXPERT_FF, D_MODEL), wmap)],
            out_specs=pl.BlockSpec((blk, width), lambda i, be, nb: (i, 0)),
            scratch_shapes=[pltpu.VMEM((D_MODEL, EXPERT_FF), BF16), pltpu.VMEM((D_MODEL, EXPERT_FF), BF16),
                            pltpu.VMEM((EXPERT_FF, D_MODEL), BF16)],
        ),
        out_shape=jax.ShapeDtypeStruct((cap, width), buf.dtype),
        compiler_params=_cparams("arbitrary"),
        name="moe_experts",
    )(blk_expert, n_used, buf, w_gate, w_up, w_down)


SC_CH = 64


def _sc_mesh():
    return plsc.VectorSubcoreMesh(core_axis_name="c", subcore_axis_name="s")


def _sc_worker(info):
    return lax.axis_index("s") * info.num_cores + lax.axis_index("c")


def _sc_gather_rows(table, idx, rows=None):
    info = plsc.get_sparse_core_info()
    lanes, workers = info.num_lanes, info.num_cores * info.num_subcores
    invert = rows is not None
    rows = rows if invert else idx.shape[0]
    width = table.shape[1]
    per_w = rows // workers
    n_chunks = per_w // SC_CH
    assert per_w % (2 * SC_CH) == 0
    scan = 2048
    n_tok = table.shape[0]

    def body(table_hbm, idx_hbm, out_hbm, idx_v, rows_a, rows_b, sem_a, sem_b, *scan_v):
        base = _sc_worker(info) * per_w
        if invert:
            lane = lax.iota(jnp.int32, lanes)

            @pl.loop(0, per_w // lanes)
            def _(j):
                idx_v[pl.ds(j * lanes, lanes)] = (base + j * lanes + lane) & (n_tok - 1)

            @pl.loop(0, idx.shape[0] // scan)
            def _(c):
                pltpu.sync_copy(idx_hbm.at[pl.ds(c * scan, scan)], scan_v[0])

                @plsc.parallel_loop(0, scan // lanes, unroll=8)
                def _(j):
                    rel = scan_v[0][pl.ds(j * lanes, lanes)] - base
                    mine = (rel >= 0) & (rel < per_w)
                    token = (c * scan + j * lanes + lane) & (n_tok - 1)
                    plsc.store_scatter(idx_v, [jnp.where(mine, rel, 0)], token, mask=mine)
        else:
            pltpu.sync_copy(idx_hbm.at[pl.ds(base, per_w)], idx_v)

        def gather(j, buf, sem):
            return pltpu.make_async_copy(table_hbm.at[idx_v.at[pl.ds(j * SC_CH, SC_CH)]], buf, sem)

        def write(j, buf):
            pltpu.sync_copy(buf, out_hbm.at[pl.ds(base + j * SC_CH, SC_CH)])

        gather(0, rows_a, sem_a).start()

        @pl.loop(0, n_chunks, step=2)
        def _(j):
            gather(j + 1, rows_b, sem_b).start()
            gather(j, rows_a, sem_a).wait()
            write(j, rows_a)

            @pl.when(j + 2 < n_chunks)
            def _():
                gather(j + 2, rows_a, sem_a).start()

            gather(j + 1, rows_b, sem_b).wait()
            write(j + 1, rows_b)

    chunk_buf = pltpu.VMEM((SC_CH, width), table.dtype)
    scratch = [pltpu.VMEM((per_w,), jnp.int32), chunk_buf, chunk_buf,
               pltpu.SemaphoreType.DMA, pltpu.SemaphoreType.DMA]
    return pl.kernel(
        body, out_type=jax.ShapeDtypeStruct((rows, width), table.dtype), mesh=_sc_mesh(),
        scratch_types=scratch + ([pltpu.VMEM((scan,), jnp.int32)] if invert else []),
        compiler_params=pltpu.CompilerParams(needs_layout_passes=False) if invert else None,
        name="sc_dispatch_rows" if invert else "sc_gather_rows",
    )(table, idx)


def _blend_kernel(x1_ref, w_ref, g0_ref, g1_ref, o_ref):
    o_ref[...] = _blend_rows(x1_ref[...], w_ref[...], g0_ref[...], g1_ref[...])


def _blend(x1, w_col, gathered):
    n = x1.shape[0]
    tt = DSP_TT
    nt = n // tt
    width = gathered.shape[1]
    return pl.pallas_call(
        _blend_kernel,
        grid=(nt,),
        in_specs=[pl.BlockSpec((tt, D_MODEL), lambda i: (i, 0)),
                  pl.BlockSpec((tt, EXPERT_TOP_K), lambda i: (i, 0)),
                  pl.BlockSpec((tt, width), lambda i: (i, 0)),
                  pl.BlockSpec((tt, width), lambda i: (nt + i, 0))],
        out_specs=pl.BlockSpec((tt, D_MODEL), lambda i: (i, 0)),
        out_shape=jax.ShapeDtypeStruct((n, D_MODEL), F32),
        compiler_params=_cparams("parallel"),
        name="moe_blend",
    )(x1, w_col, gathered, gathered)


def _moe(x1, h2, logits_t, w_gate, w_up, w_down, layer):
    n = x1.shape[0]
    e_idx, wts, rank, counts = _route(logits_t)
    counts = counts[:, 0].astype(jnp.int32)
    padded = (counts + MOE_BLK - 1) // MOE_BLK * MOE_BLK
    pends = jnp.cumsum(padded)
    pstarts = pends - padded
    experts = jnp.arange(N_EXPERTS, dtype=jnp.int32)
    start_of = jnp.sum(jnp.where(e_idx[None, :EXPERT_TOP_K] == experts[:, None, None],
                                 pstarts[:, None, None], 0), axis=0)
    dest = start_of + rank[:EXPERT_TOP_K]
    cap = n * EXPERT_TOP_K + N_EXPERTS * MOE_BLK
    n_blk = cap // MOE_BLK
    blk_row = jnp.arange(n_blk, dtype=jnp.int32) * MOE_BLK
    blk_expert = jnp.minimum(jnp.sum((pends[None, :] <= blk_row[:, None]).astype(jnp.int32), axis=1),
                             N_EXPERTS - 1)
    n_used = (pends[-1:] // MOE_BLK).astype(jnp.int32)
    dest_flat = dest.reshape(EXPERT_TOP_K * n)
    buf = _sc_gather_rows(h2, dest_flat, rows=cap)
    out_buf = _experts(blk_expert, n_used, buf, w_gate, w_up, w_down, layer)
    return x1, wts[:EXPERT_TOP_K].T, _sc_gather_rows(out_buf, dest_flat)


def kernel(x, positions, norm_mix, norm_ffn, w_in, q_gain, k_gain, cmp_k_pos, cmp_k_w1, cmp_k_w2, cmp_v_pos,
           cmp_v_w1, cmp_v_w2, s5_a_re, s5_a_im, s5_log_dt, s5_b_re, s5_b_im, s5_c_re, s5_c_im, s5_d, s5_glu,
           proj_nsa, proj_s5, w_out, router_group, router_group_bias, router_expert, router_expert_bias,
           expert_gate, expert_up, expert_down):
    batch, seq, _ = x.shape
    depth = w_in.shape[0]
    n = batch * seq
    inv_freq = ROPE_THETA ** (-jnp.arange(0, ROPE_DIMS, 2, dtype=jnp.float32) / ROPE_DIMS)
    ctab, stab = _rope_tables(positions.reshape(n, 1), inv_freq)
    ccmp, scmp = _cmp_rope_tables(ctab, stab, batch, seq)
    x_in = x.reshape(n, D_MODEL)
    for l in range(depth):
        w1, w_ga, w_gs = _pack_w_in(w_in[l])
        o_a, u_tm, x2 = _nsa_branch(x_in, norm_mix[l][None], w1, ctab, stab, ccmp, scmp, q_gain[l], k_gain[l],
                                    (cmp_k_pos[l], cmp_k_w1[l], cmp_k_w2[l]),
                                    (cmp_v_pos[l], cmp_v_w1[l], cmp_v_w2[l]), batch, seq)
        lre, lim, w_b, w_c = _s5_weights(s5_a_re[l], s5_a_im[l], s5_log_dt[l], s5_b_re[l], s5_b_im[l],
                                         s5_c_re[l], s5_c_im[l])
        ob_tm = _s5(u_tm, lre, lim, w_b, w_c, s5_d[l][None],
                    s5_glu[l].astype(BF16), batch, seq)
        wrh, wrl, rb = _router_weights(router_group[l], router_group_bias[l], router_expert[l],
                                       router_expert_bias[l])
        x1, h2, logits_t = _merge(x2, norm_mix[l][None], w_ga, w_gs, o_a, ob_tm, proj_nsa[l].astype(BF16),
                                  proj_s5[l].astype(BF16), w_out[l].astype(BF16), norm_ffn[l][None],
                                  wrh, wrl, rb, batch, seq)
        x_in = _moe(x1, h2, logits_t, expert_gate, expert_up, expert_down, l)
    return _blend(*x_in).reshape(batch, seq, D_MODEL)


def _attn_tables(seq):
    n_cmp = (seq - CMP_BLOCK) // CMP_STRIDE + 1
    n_slc = seq // SLC_BLOCK
    key_blk = np.arange(seq) // SLC_BLOCK
    expand_t = (key_blk[:, None] == np.arange(LANES)[None, :]).astype(np.float32)
    cmp_start = np.arange(LANES) * CMP_STRIDE
    slc_start = np.arange(LANES) * SLC_BLOCK
    overlap_t = ((cmp_start[None, :] < slc_start[:, None] + SLC_BLOCK)
                 & (cmp_start[None, :] + CMP_BLOCK > slc_start[:, None])
                 & (np.arange(LANES)[None, :] < n_cmp) & (np.arange(LANES)[:, None] < n_slc))
    return (jnp.asarray(expand_t, BF16), jnp.asarray(overlap_t.astype(np.float32), BF16), n_cmp, n_slc)


def _attn_mask_tables():
    tq, tk, wk = ATT_TQ, ATT_TK, ATT_WK
    ql = np.arange(tq)
    onehot = np.tile(np.eye(tq, dtype=np.float32), (NSA_HEADS, 1))
    k = np.arange(wk)[:, None]
    band = []
    for c in range(WINDOW // tq + 1):
        if c < WINDOW // tq:
            keep = k <= c * tq + ql[None, :]
        else:
            keep = (k > ql[None, :]) & (k <= ql[None, :] + WINDOW)
        band.append(np.where(keep, 0.0, NEG_BIG))
    kl = np.arange(tk)[None, :]
    causal = [np.where(kl <= c * tq + ql[:, None], 0.0, NEG_BIG) for c in range(tk // tq)]
    return (jnp.asarray(onehot, BF16), jnp.asarray(np.stack(band), BF16),
            jnp.asarray(np.stack(causal), F32))


def _attention(q, ks, vs, kw, vw, kc, vc, gates, batch, seq):
    expand_t, overlap_t, n_cmp, n_slc = _attn_tables(seq)
    onehot, band, causal = _attn_mask_tables()
    tq = ATT_TQ
    nq = seq // tq
    rows = NSA_HEADS * tq
    tile = lambda w: pl.BlockSpec((tq, w), lambda b, i: (b * nq + i, 0))
    per_b = lambda r: pl.BlockSpec((r, LANES), lambda b, i: (b, 0))
    f32_rows = pltpu.VMEM((rows, LANES), F32)
    return pl.pallas_call(
        functools.partial(_attn_kernel, n_cmp=n_cmp, n_slc=n_slc, top_k=min(SLC_TOPK, n_slc)),
        grid=(batch, nq),
        in_specs=[tile(2 * NSA_WIDTH), per_b(seq), per_b(seq), per_b(seq), per_b(seq),
                  per_b(LANES), per_b(LANES), tile(LANES),
                  pl.BlockSpec(expand_t.shape, lambda b, i: (0, 0)),
                  pl.BlockSpec(overlap_t.shape, lambda b, i: (0, 0)),
                  pl.BlockSpec(onehot.shape, lambda b, i: (0, 0)),
                  pl.BlockSpec(band.shape, lambda b, i: (0, 0, 0)),
                  pl.BlockSpec(causal.shape, lambda b, i: (0, 0, 0))],
        out_specs=tile(NSA_WIDTH),
        out_shape=jax.ShapeDtypeStruct((batch * seq, NSA_WIDTH), BF16),
        scratch_shapes=[pltpu.VMEM((rows, 2 * LANES), BF16),
                        pltpu.VMEM((2, rows, ATT_TK), F32), pltpu.VMEM((2, rows, ATT_TK), BF16),
                        pltpu.VMEM((rows, ATT_WK), F32), pltpu.VMEM((rows, ATT_WK), BF16),
                        f32_rows, f32_rows, f32_rows, f32_rows, f32_rows],
        compiler_params=_cparams("parallel", "parallel"),
        name="nsa_attention",
    )(q, ks, vs, kw, vw, kc, vc, gates, expand_t, overlap_t, onehot, band, causal)
```

```python
import functools
import math

import jax
import jax.numpy as jnp
import numpy as np
from jax import lax
from jax.experimental import pallas as pl
from jax.experimental.pallas import tpu as pltpu
from jax.experimental.pallas import tpu_sc as plsc

F32 = jnp.float32
BF16 = jnp.bfloat16

D_MODEL = 1024
NSA_HEADS = 8
NSA_KV_HEADS = 2
NSA_GROUP = NSA_HEADS // NSA_KV_HEADS
HEAD_DIM = 64
ROPE_DIMS = HEAD_DIM // 4
ROPE_THETA = 500000.0
CMP_BLOCK = 32
CMP_STRIDE = 16
CMP_HIDDEN = 2 * HEAD_DIM
SLC_BLOCK = 64
SLC_TOPK = 16
WINDOW = 512
NSA_WIDTH = NSA_HEADS * HEAD_DIM
KV_WIDTH = NSA_KV_HEADS * HEAD_DIM
S5_WIDTH = 256
S5_GROUP_CH = 16
S5_GROUPS = S5_WIDTH // S5_GROUP_CH
S5_STATE = 64
S5_NSTATE = S5_GROUPS * S5_STATE
N_EXPERT_GROUPS = 4
EXPERTS_PER_GROUP = 8
N_EXPERTS = N_EXPERT_GROUPS * EXPERTS_PER_GROUP
EXPERT_TOP_K = 2
EXPERT_FF = 512
RMS_EPS = 1e-6
NEG_BIG = -1e30

LANES = 128
SUBLANES = 8
VMEM_LIMIT = 56 * 1024 * 1024

IN_TM = 512
ATT_TQ = 128
ATT_TK = 512
ATT_WK = WINDOW + ATT_TQ
ATT_RB = 128
S5_T = 256
MRG_TM = 512
MOE_BLK = 512
DSP_TT = 1024

C_Q, C_KS, C_KW, C_KC, C_VC, C_VS, C_VW, C_U, C_G = 0, 512, 640, 768, 896, 1024, 1152, 1280, 1536
IN_W = 1664


def _cparams(*sem):
    return pltpu.CompilerParams(dimension_semantics=sem, vmem_limit_bytes=VMEM_LIMIT)


def _gelu(x):
    return 0.5 * x * (1.0 + jnp.tanh(math.sqrt(2.0 / math.pi) * (x + 0.044715 * (x * x * x))))


def _sigmoid(x):
    return 1.0 / (1.0 + jnp.exp(-x))


def _dot(a, b):
    return jnp.dot(a, b, preferred_element_type=F32)


def _dot_t(a, b):
    return lax.dot_general(a, b, (((1,), (1,)), ((), ())), preferred_element_type=F32)


def _split_bf16(x):
    hi = x.astype(BF16)
    lo = (x - hi.astype(F32)).astype(BF16)
    return hi, lo


def _pack_rows(x):
    k = x.shape[1] // 2
    hi = lax.bitcast_convert_type(x[:, :k].astype(jnp.bfloat16).astype(F32), jnp.int32)
    lo = lax.bitcast_convert_type(x[:, k:].astype(jnp.bfloat16).astype(F32), jnp.int32)
    return hi | lax.shift_right_logical(lo, 16)


def _unpack_rows(w):
    a = lax.bitcast_convert_type(w & jnp.int32(-65536), F32)
    b = lax.bitcast_convert_type(lax.shift_left(w, 16), F32)
    return a, b


def _rope_kernel(pos_ref, freq_ref, sign_ref, c_ref, s_ref):
    ang = pos_ref[...].astype(F32) * freq_ref[...]
    c_ref[...] = jnp.cos(ang)
    s_ref[...] = jnp.sin(ang) * sign_ref[...]


def _rope_tables(pos_col, inv_freq):
    n = pos_col.shape[0]
    lane = np.arange(LANES) % HEAD_DIM
    sel = np.zeros((LANES, ROPE_DIMS // 2), np.float32)
    for l in range(LANES):
        if lane[l] < ROPE_DIMS:
            sel[l, lane[l] % (ROPE_DIMS // 2)] = 1.0
    freq = (jnp.asarray(sel) * inv_freq[None, :]).sum(-1)[None, :]
    sign = np.where(lane < ROPE_DIMS // 2, -1.0, 1.0).astype(np.float32)[None, :]
    tm = 1024
    return pl.pallas_call(
        _rope_kernel,
        grid=(n // tm,),
        in_specs=[pl.BlockSpec((tm, 1), lambda i: (i, 0)),
                  pl.BlockSpec((1, LANES), lambda i: (0, 0)),
                  pl.BlockSpec((1, LANES), lambda i: (0, 0))],
        out_specs=[pl.BlockSpec((tm, LANES), lambda i: (i, 0))] * 2,
        out_shape=[jax.ShapeDtypeStruct((n, LANES), F32)] * 2,
        compiler_params=_cparams("parallel"),
        name="rope_tables",
    )(pos_col, freq, jnp.asarray(sign))


def _headnorm_rope(t, gain, c, s):
    lane = lax.broadcasted_iota(jnp.int32, t.shape, 1)
    lo = lane < HEAD_DIM
    sq = t * t
    s_lo = jnp.sum(jnp.where(lo, sq, 0.0), axis=-1, keepdims=True)
    s_hi = jnp.sum(jnp.where(lo, 0.0, sq), axis=-1, keepdims=True)
    ms = jnp.where(lo, s_lo, s_hi) * (1.0 / HEAD_DIM)
    tn = (t * lax.rsqrt(ms + RMS_EPS)) * gain
    half = ROPE_DIMS // 2
    partner = jnp.where((lane & (HEAD_DIM - 1)) < half,
                        pltpu.roll(tn, LANES - half, 1), pltpu.roll(tn, half, 1))
    return tn * c + partner * s


def _blend_rows(x1, w, g0, g1):
    half = x1.shape[1] // 2
    a0, b0 = _unpack_rows(g0)
    a1, b1 = _unpack_rows(g1)
    w0, w1 = w[:, 0:1], w[:, 1:2]
    return jnp.concatenate([x1[:, 0:half] + w0 * a0 + w1 * a1, x1[:, half:] + w0 * b0 + w1 * b1], axis=1)


def _in_proj_kernel(*refs, blend):
    if blend:
        x1_ref, wc_ref, g0_ref, g1_ref, *refs = refs
    else:
        x_in_ref, *refs = refs
    gain_ref, w_ref, c_ref, s_ref, qg_ref, kg_ref, *refs = refs
    q_ref, ks_ref, kw_ref, kc_ref, vc_ref, vs_ref, vw_ref, u_ref, g_ref, *refs = refs
    if blend:
        x_out_ref, raw_ref = refs
        x = _blend_rows(x1_ref[...], wc_ref[...], g0_ref[...], g1_ref[...])
        x_out_ref[...] = x
    else:
        (raw_ref,) = refs
        x = x_in_ref[...]
    ms = jnp.mean(x * x, axis=-1, keepdims=True)
    h = (x * lax.rsqrt(ms + RMS_EPS)) * gain_ref[...]
    z = _dot(h.astype(BF16), w_ref[...])
    c = c_ref[...]
    s = s_ref[...]
    lane = lax.broadcasted_iota(jnp.int32, c.shape, 1)
    lo = lane < HEAD_DIM
    scale = HEAD_DIM ** -0.5 * math.log2(math.e)
    for j in range(NSA_HEADS // 2):
        r = _headnorm_rope(z[:, C_Q + LANES * j:C_Q + LANES * (j + 1)], qg_ref[...], c, s) * scale
        rr = pltpu.roll(r, HEAD_DIM, 1)
        if j // (NSA_GROUP // 2) == 0:
            even, odd = jnp.where(lo, r, 0.0), jnp.where(lo, rr, 0.0)
        else:
            even, odd = jnp.where(lo, 0.0, rr), jnp.where(lo, 0.0, r)
        q_ref[:, 2 * LANES * j:2 * LANES * j + LANES] = even.astype(BF16)
        q_ref[:, 2 * LANES * j + LANES:2 * LANES * (j + 1)] = odd.astype(BF16)
    ks_ref[...] = _headnorm_rope(z[:, C_KS:C_KS + LANES], kg_ref[...], c, s).astype(BF16)
    kw_ref[...] = _headnorm_rope(z[:, C_KW:C_KW + LANES], kg_ref[...], c, s).astype(BF16)
    half = CMP_BLOCK // 2
    for col, out in ((C_KC, kc_ref), (C_VC, vc_ref)):
        raw_ref[...] = z[:, col:col + LANES]
        for l in range(half):
            piece = raw_ref[pl.ds(l, x.shape[0] // half, stride=half), :]
            out[:, LANES * l:LANES * (l + 1)] = piece.astype(BF16)
    vs_ref[...] = z[:, C_VS:C_VS + LANES].astype(BF16)
    vw_ref[...] = z[:, C_VW:C_VW + LANES].astype(BF16)
    u_ref[:, pl.ds(pl.program_id(1), 1), :] = z[:, C_U:C_U + S5_WIDTH][:, None, :]
    g_ref[...] = _sigmoid(z[:, C_G:C_G + LANES])


def _in_proj(x_in, gain, w1, ctab, stab, qg, kg, batch, seq):
    blend = isinstance(x_in, tuple)
    n = batch * seq
    tm = IN_TM
    nt = seq // tm
    row = lambda i, b: (b * nt + i, 0)
    const = lambda i, b: (0, 0)
    tok = lambda w: pl.BlockSpec((tm, w), row)
    half = CMP_BLOCK // 2
    grouped = pl.BlockSpec((tm // half, half * KV_WIDTH), row)
    rows_bf16 = lambda r, w: jax.ShapeDtypeStruct((r, w), BF16)
    if blend:
        x1, w_col, gathered = x_in
        width = gathered.shape[1]
        x_specs = [tok(D_MODEL), tok(EXPERT_TOP_K), tok(width),
                   pl.BlockSpec((tm, width), lambda i, b: (n // tm + b * nt + i, 0))]
        x_args = (x1, w_col, gathered, gathered)
    else:
        x_specs, x_args = [tok(D_MODEL)], (x_in,)
    return pl.pallas_call(
        functools.partial(_in_proj_kernel, blend=blend),
        grid=(nt, batch),
        in_specs=x_specs + [pl.BlockSpec((1, D_MODEL), const), pl.BlockSpec((D_MODEL, IN_W), const),
                            tok(LANES), tok(LANES), pl.BlockSpec((1, LANES), const),
                            pl.BlockSpec((1, LANES), const)],
        out_specs=[tok(2 * NSA_WIDTH), tok(LANES), tok(LANES), grouped, grouped, tok(LANES), tok(LANES),
                   pl.BlockSpec((tm, batch, S5_WIDTH), lambda i, b: (i, 0, 0)), tok(LANES)]
        + ([tok(D_MODEL)] if blend else []),
        out_shape=[rows_bf16(n, 2 * NSA_WIDTH), rows_bf16(n, LANES), rows_bf16(n, LANES),
                   rows_bf16(n // half, half * KV_WIDTH), rows_bf16(n // half, half * KV_WIDTH),
                   rows_bf16(n, LANES), rows_bf16(n, LANES),
                   jax.ShapeDtypeStruct((seq, batch, S5_WIDTH), F32), jax.ShapeDtypeStruct((n, LANES), F32)]
        + ([jax.ShapeDtypeStruct((n, D_MODEL), F32)] if blend else []),
        scratch_shapes=[pltpu.VMEM((tm, LANES), F32)],
        compiler_params=_cparams("parallel", "arbitrary"),
        name="in_proj",
    )(*x_args, gain, w1, ctab, stab, qg, kg)


def _compress_mlp(h_ref, w_ref, p_ref, w2_ref):
    w = w_ref[...]
    pre = _dot(h_ref[...], w)
    phi, plo = _split_bf16(p_ref[...])
    pb = _dot(phi, w) + _dot(plo, w)
    rows = pre.shape[0]
    out = None
    for h in range(NSA_KV_HEADS):
        top = pre[:, LANES * h:LANES * (h + 1)]
        bot = pre[:, LANES * (2 + h):LANES * (3 + h)]
        bias = pb[0:1, LANES * h:LANES * (h + 1)] + pb[1:2, LANES * (2 + h):LANES * (3 + h)]
        a = _gelu(top + pltpu.roll(bot, rows - 1, 0) + bias)
        o = _dot(a.astype(BF16), w2_ref[h])
        out = o if out is None else out + o
    return out


def _compress_kernel(hk_ref, hv_ref, wk_ref, wv_ref, pk_ref, pv_ref, w2k_ref, w2v_ref,
                     kg_ref, c_ref, s_ref, kc_ref, vc_ref):
    k = _compress_mlp(hk_ref, wk_ref, pk_ref, w2k_ref)
    kc_ref[...] = _headnorm_rope(k, kg_ref[...], c_ref[...], s_ref[...]).astype(BF16)
    vc_ref[...] = _compress_mlp(hv_ref, wv_ref, pv_ref, w2v_ref).astype(BF16)


def _compress(hk, hv, wk, wv, pk, pv, w2k, w2v, kg, ccmp, scmp, batch):
    rows = hk.shape[0] // batch
    width = hk.shape[1]
    const2 = lambda b: (0, 0)
    const3 = lambda b: (0, 0, 0)
    rowb = lambda w: pl.BlockSpec((rows, w), lambda b: (b, 0))
    return pl.pallas_call(
        _compress_kernel,
        grid=(batch,),
        in_specs=[rowb(width), rowb(width),
                  pl.BlockSpec(wk.shape, const2), pl.BlockSpec(wv.shape, const2),
                  pl.BlockSpec(pk.shape, const2), pl.BlockSpec(pv.shape, const2),
                  pl.BlockSpec(w2k.shape, const3), pl.BlockSpec(w2v.shape, const3),
                  pl.BlockSpec((1, LANES), const2), rowb(LANES), rowb(LANES)],
        out_specs=[rowb(LANES), rowb(LANES)],
        out_shape=[jax.ShapeDtypeStruct((hk.shape[0], LANES), BF16)] * 2,
        compiler_params=_cparams("parallel"),
        name="compress_kv",
    )(hk, hv, wk, wv, pk, pv, w2k, w2v, kg, ccmp, scmp)


def _compress_weights(pos_emb, w1, w2):
    half = CMP_BLOCK // 2
    eye = jnp.eye(NSA_KV_HEADS, dtype=F32)
    w1r = w1.reshape(2, half, HEAD_DIM, CMP_HIDDEN).astype(BF16)
    zero = jnp.zeros_like(w1r[0])
    wexp = jnp.concatenate(
        [jnp.stack([w1r[s] if p == h else zero for p in range(NSA_KV_HEADS)], axis=1)
         .reshape(half * KV_WIDTH, CMP_HIDDEN) for s in range(2) for h in range(NSA_KV_HEADS)], axis=1)
    pexp = jnp.broadcast_to(pos_emb.reshape(2, half, 1, HEAD_DIM), (2, half, NSA_KV_HEADS, HEAD_DIM))
    pexp = jnp.pad(pexp.reshape(2, half * KV_WIDTH), ((0, SUBLANES - 2), (0, 0)))
    w2e = jnp.einsum("cd,ph->pchd", w2, eye).reshape(NSA_KV_HEADS, CMP_HIDDEN, KV_WIDTH)
    return wexp, pexp, w2e.astype(BF16)


def _attn_kernel(q_ref, ks_ref, vs_ref, kw_ref, vw_ref, kc_ref, vc_ref, g_ref, et_ref, ovt_ref, oh_ref, wb_ref,
                 cb_ref, o_ref, qc_ref, s_ref, p_ref, sw_ref, pw_ref, m_ref, al_ref, oc_ref, as_ref, aw_ref,
                 *, n_cmp, n_slc, top_k):
    tq, tk, wk, rb, nh = ATT_TQ, ATT_TK, ATT_WK, ATT_RB, NSA_HEADS
    rows, grp_rows = nh * tq, NSA_GROUP * tq
    qk_rows = 2 * tq
    q0 = pl.program_id(1) * tq
    for a in range(nh):
        qc_ref[a * tq:(a + 1) * tq, 0:LANES] = q_ref[:, LANES * a:LANES * (a + 1)]
    q8 = qc_ref[:, 0:LANES]
    lane = lax.broadcasted_iota(jnp.int32, (tq, LANES), 1)
    t_row = q0 + lax.broadcasted_iota(jnp.int32, (tq, LANES), 0)

    s_c = _dot_t(q8, kc_ref[...]).reshape(nh, tq, LANES)
    mask_c = ((lane * CMP_STRIDE + (CMP_BLOCK - 1)) <= t_row) & (lane < n_cmp)
    sm = jnp.where(mask_c[None], s_c, NEG_BIG)
    p = jnp.where(mask_c[None], jnp.exp2(sm - jnp.max(sm, axis=-1, keepdims=True)), 0.0)
    l = jnp.sum(p, axis=-1, keepdims=True)
    p_c = p * jnp.where(l > 0.0, 1.0 / l, 0.0)
    oc_ref[...] = _dot(p_c.reshape(rows, LANES).astype(BF16), vc_ref[...])

    blk = lax.broadcasted_iota(jnp.int32, (n_slc, tq), 0)
    cur = (q0 + lax.broadcasted_iota(jnp.int32, (n_slc, tq), 1)) >> int(math.log2(SLC_BLOCK))
    forced = (blk == 0) | (blk == cur) | (blk == cur - 1)
    for h in range(NSA_KV_HEADS):
        ps = p_c[NSA_GROUP * h]
        for g in range(1, NSA_GROUP):
            ps = ps + p_c[NSA_GROUP * h + g]
        hi, lo = _split_bf16(ps)
        imp = (_dot_t(ovt_ref[...], hi) + _dot_t(ovt_ref[...], lo))[0:n_slc]
        imp = jnp.where(forced, jnp.inf, jnp.where(blk > cur, -jnp.inf, imp))
        rank = jnp.zeros((n_slc, tq), F32)
        for i in range(n_slc):
            row = imp[i:i + 1, :]
            rank = rank + jnp.where((row > imp) | ((row == imp) & (blk > i)), 1.0, 0.0)
        bias_t = jnp.where(rank < float(top_k), 0.0, NEG_BIG)
        bias = jnp.concatenate([bias_t, jnp.zeros((LANES - n_slc, tq), F32)], axis=0).T.astype(BF16)
        for g in range(NSA_GROUP):
            a = NSA_GROUP * h + g
            qc_ref[a * tq:(a + 1) * tq, LANES:2 * LANES] = bias

    def softmax_pass(s_buf, p_buf, width, causal_case, online):
        for r in range(rows // rb):
            r0 = r * rb
            s = s_buf[pl.ds(r0, rb), 0:width]
            if causal_case is not None:
                s = s + cb_ref[causal_case, pl.ds(r0 % tq, rb), 0:width]
            m_new = jnp.broadcast_to(jnp.max(s, axis=1, keepdims=True), (rb, LANES))
            if online:
                m_old = m_ref[pl.ds(r0, rb), :]
                m_new = jnp.maximum(m_old, m_new)
                al_ref[pl.ds(r0, rb), :] = jnp.exp2(m_old - m_new)
                m_ref[pl.ds(r0, rb), :] = m_new
            pp = jnp.exp2(s - jnp.concatenate([m_new] * (width // LANES), axis=1))
            p_buf[pl.ds(r0, rb), 0:width] = pp.astype(BF16)

    def pv(acc_ref, p_buf, v, width, online):
        lane_v = lax.broadcasted_iota(jnp.int32, v.shape, 1)
        for h in range(NSA_KV_HEADS):
            rs = slice(h * grp_rows, (h + 1) * grp_rows)
            own = (lane_v < HEAD_DIM) if h == 0 else (lane_v >= HEAD_DIM)
            upd = _dot(p_buf[rs, 0:width], jnp.where(own, v, jnp.ones_like(v)))
            acc_ref[rs, :] = (al_ref[rs, :] * acc_ref[rs, :] + upd) if online else upd

    m_ref[...] = jnp.full(m_ref.shape, 0.1 * NEG_BIG, F32)
    as_ref[...] = jnp.zeros(as_ref.shape, F32)

    def slc_chunk(c, diagonal, width=tk):
        k0 = c * tk
        s_buf, p_buf = s_ref.at[c % 2], p_ref.at[c % 2]
        kcat = jnp.concatenate([ks_ref[pl.ds(k0, width), :], et_ref[pl.ds(k0, width), :]], axis=1)
        for g in range(rows // qk_rows):
            rs = slice(g * qk_rows, (g + 1) * qk_rows)
            s_buf[rs, 0:width] = _dot_t(qc_ref[rs, :], kcat)
        softmax_pass(s_buf, p_buf, width, ((q0 - k0) >> int(math.log2(tq))) if diagonal else None, True)
        pv(as_ref, p_buf, vs_ref[pl.ds(k0, width), :], width, True)

    def window_and_output():
        w0 = pl.multiple_of(jnp.maximum(q0 - WINDOW, 0), LANES)
        w_case = jnp.minimum(q0 >> int(math.log2(tq)), WINDOW // tq)
        kcat_w = jnp.concatenate([kw_ref[pl.ds(w0, wk), :], wb_ref[w_case]], axis=1)
        for g in range(rows // qk_rows):
            rs = slice(g * qk_rows, (g + 1) * qk_rows)
            sw_ref[rs, :] = _dot_t(jnp.concatenate([qc_ref[rs, 0:LANES], oh_ref[rs, :]], axis=1), kcat_w)
        softmax_pass(sw_ref, pw_ref, wk, None, False)
        pv(aw_ref, pw_ref, vw_ref[pl.ds(w0, wk), :], wk, False)

        lo_half = lane < HEAD_DIM
        for j in range(nh // 2):
            comb = []
            for a in (2 * j, 2 * j + 1):
                rs = slice(a * tq, (a + 1) * tq)
                o_s, o_w = as_ref[rs, :], aw_ref[rs, :]
                comb.append(g_ref[:, 3 * a:3 * a + 1] * oc_ref[rs, :]
                            + g_ref[:, 3 * a + 1:3 * a + 2] * (o_s * (1.0 / pltpu.roll(o_s, HEAD_DIM, 1)))
                            + g_ref[:, 3 * a + 2:3 * a + 3] * (o_w * (1.0 / pltpu.roll(o_w, HEAD_DIM, 1))))
            even, odd = comb
            if j // (NSA_GROUP // 2) == 0:
                pair = jnp.where(lo_half, even, pltpu.roll(odd, HEAD_DIM, 1))
            else:
                pair = jnp.where(lo_half, pltpu.roll(even, HEAD_DIM, 1), odd)
            o_ref[:, LANES * j:LANES * (j + 1)] = pair.astype(BF16)

    c_last = q0 >> int(math.log2(tk))
    early = ((q0 >> int(math.log2(tq))) & (tk // tq - 1)) < tk // tq // 2
    for n_full in range(ks_ref.shape[0] // tk):
        for first_half in (True, False):
            @pl.when((c_last == n_full) & (early if first_half else jnp.logical_not(early)))
            def _():
                for c in range(n_full):
                    slc_chunk(c, False)
                slc_chunk(n_full, True, tk // 2 if first_half else tk)
                window_and_output()


def _pack_w_in(w):
    sizes = (NSA_WIDTH,) + (KV_WIDTH,) * 6 + (3 * NSA_HEADS, S5_WIDTH, D_MODEL, D_MODEL)
    o = np.cumsum((0,) + sizes)
    q, kc, vc, ks, vs, kw, vw, g, u, ga, gs = [w[:, o[i]:o[i + 1]] for i in range(len(sizes))]
    g = jnp.pad(g, ((0, 0), (0, LANES - 3 * NSA_HEADS)))
    w1 = jnp.concatenate([q, ks, kw, kc, vc, vs, vw, u, g], axis=1).astype(BF16)
    return w1, ga.astype(BF16), gs.astype(BF16)


def _nsa_branch(x_in, gain, w1, ctab, stab, ccmp, scmp, q_gain, k_gain, cmp_k, cmp_v, batch, seq):
    qg = jnp.tile(q_gain, 2)[None, :]
    kg = jnp.tile(k_gain, 2)[None, :]
    q, ks, kw, hk, hv, vs, vw, u_tm, gates, *x_out = _in_proj(x_in, gain, w1, ctab, stab, qg, kg, batch, seq)
    wk, pk, w2k = _compress_weights(*cmp_k)
    wv, pv, w2v = _compress_weights(*cmp_v)
    kc, vc = _compress(hk, hv, wk, wv, pk, pv, w2k, w2v, kg, ccmp, scmp, batch)
    o_a = _attention(q, ks, vs, kw, vw, kc, vc, gates, batch, seq)
    return o_a, u_tm, (x_out[0] if x_out else x_in)


def _cmp_rope_tables(ctab, stab, batch, seq):
    def pick(t):
        t = t.reshape(batch, seq, LANES)[:, CMP_BLOCK - 1::CMP_STRIDE]
        t = jnp.pad(t, ((0, 0), (0, seq // CMP_STRIDE - t.shape[1]), (0, 0)))
        return t.reshape(batch * (seq // CMP_STRIDE), LANES)
    return pick(ctab), pick(stab)


def _s5_disc_kernel(are_ref, aim_ref, ldt_ref, lre_ref, lim_ref, fre_ref, fim_ref):
    lam_re = jnp.minimum(are_ref[...], -1e-4)
    lam_im = aim_ref[...]
    dt = jnp.exp(ldt_ref[...])
    mag = jnp.exp(lam_re * dt)
    ang = lam_im * dt
    lb_re = mag * jnp.cos(ang)
    lb_im = mag * jnp.sin(ang)
    den = lam_re * lam_re + lam_im * lam_im
    lre_ref[...] = lb_re
    lim_ref[...] = lb_im
    fre_ref[...] = ((lb_re - 1.0) * lam_re + lb_im * lam_im) / den
    fim_ref[...] = (lb_im * lam_re - (lb_re - 1.0) * lam_im) / den


def _s5_weights(a_re, a_im, log_dt, b_re, b_im, c_re, c_im):
    shp = jax.ShapeDtypeStruct((S5_GROUPS, S5_STATE), F32)
    lb_re, lb_im, f_re, f_im = pl.pallas_call(
        _s5_disc_kernel, out_shape=[shp] * 4, name="s5_discretise",
    )(a_re, a_im, log_dt[:, None])
    bb_re = f_re[..., None] * b_re - f_im[..., None] * b_im
    bb_im = f_re[..., None] * b_im + f_im[..., None] * b_re
    eye = jnp.eye(S5_GROUPS, dtype=F32)

    def in_map(bb):
        return jnp.einsum("gpc,gh->gchp", bb, eye).reshape(S5_WIDTH, S5_NSTATE)

    def out_map(cc):
        return jnp.einsum("gcp,gh->hpgc", cc, eye).reshape(S5_NSTATE, S5_WIDTH)

    w_b = jnp.concatenate([in_map(bb_re), in_map(bb_im)], axis=1).astype(BF16)
    w_c = jnp.concatenate([out_map(c_re), -out_map(c_im)], axis=0).astype(BF16)
    return lb_re.reshape(1, S5_NSTATE), lb_im.reshape(1, S5_NSTATE), w_b, w_c


def _s5_kernel(u_ref, wb_ref, lre_ref, lim_ref, wc_ref, d_ref, wg_ref, o_ref, x_ref, st_ref, *, batch):
    ns = S5_NSTATE

    @pl.when(pl.program_id(0) == 0)
    def _():
        st_ref[...] = jnp.zeros(st_ref.shape, F32)

    u = u_ref[...].reshape(u_ref.shape[0] * batch, S5_WIDTH)
    x_ref[...] = _dot(u.astype(BF16), wb_ref[...])
    lre = jnp.broadcast_to(lre_ref[...], (batch, ns))
    lim = jnp.broadcast_to(lim_ref[...], (batch, ns))

    def step(t, carry):
        xr, xi = carry
        r0 = pl.multiple_of(t * batch, batch)
        nr = lre * xr - lim * xi + x_ref[pl.ds(r0, batch), 0:ns]
        ni = lre * xi + lim * xr + x_ref[pl.ds(r0, batch), ns:2 * ns]
        x_ref[pl.ds(r0, batch), 0:ns] = nr
        x_ref[pl.ds(r0, batch), ns:2 * ns] = ni
        return nr, ni

    steps = u.shape[0] // batch
    xr, xi = lax.fori_loop(0, steps, step, (st_ref[:, 0:ns], st_ref[:, ns:2 * ns]), unroll=4)
    st_ref[:, 0:ns] = xr
    st_ref[:, ns:2 * ns] = xi
    y = _dot(x_ref[...].astype(BF16), wc_ref[...]) + d_ref[...] * u
    z = _dot(_gelu(y).astype(BF16), wg_ref[...])
    o = z[:, :S5_WIDTH] * _sigmoid(z[:, S5_WIDTH:])
    for c in range(S5_WIDTH // LANES):
        o_ref[c] = o[:, c * LANES:(c + 1) * LANES]


def _s5(u_tm, lre, lim, w_b, w_c, d_skip, w_glu, batch, seq):
    rows = S5_T * batch
    const = lambda t: (0, 0)
    return pl.pallas_call(
        functools.partial(_s5_kernel, batch=batch),
        grid=(seq // S5_T,),
        in_specs=[pl.BlockSpec((S5_T, batch, S5_WIDTH), lambda t: (t, 0, 0)),
                  pl.BlockSpec(w_b.shape, const), pl.BlockSpec(lre.shape, const), pl.BlockSpec(lim.shape, const),
                  pl.BlockSpec(w_c.shape, const), pl.BlockSpec((1, S5_WIDTH), const),
                  pl.BlockSpec(w_glu.shape, const)],
        out_specs=pl.BlockSpec((S5_WIDTH // LANES, rows, LANES), lambda t: (0, t, 0)),
        out_shape=jax.ShapeDtypeStruct((S5_WIDTH // LANES, seq * batch, LANES), F32),
        scratch_shapes=[pltpu.VMEM((rows, 2 * S5_NSTATE), F32), pltpu.VMEM((batch, 2 * S5_NSTATE), F32)],
        compiler_params=_cparams("arbitrary"),
        name="s5_scan",
    )(u_tm, w_b, lre, lim, w_c, d_skip, w_glu)


RT_GROUP_ROW = 0
RT_EXPERT_ROW = SUBLANES


def _rmsnorm(x, gain):
    return (x * lax.rsqrt(jnp.mean(x * x, axis=-1, keepdims=True) + RMS_EPS)) * gain


def _merge_kernel(x_ref, g1_ref, wga_ref, wgs_ref, oa_ref, ob_ref, pa_ref, ps_ref, wo_ref, g2_ref,
                  wrh_ref, wrl_ref, rb_ref, tri_ref, x1_ref, h2_ref, e_ref, w_ref, r_ref, cnt_ref,
                  lg_ref, carry_ref):
    x = x_ref[...]
    h = _rmsnorm(x, g1_ref[...]).astype(BF16)
    gate_a = _sigmoid(_dot(h, wga_ref[...]))
    gate_s = _sigmoid(_dot(h, wgs_ref[...]))
    rows_b = pl.ds(pl.program_id(1), x.shape[0], stride=ob_ref.shape[1] // x.shape[0])
    o_b = jnp.concatenate([ob_ref[c, rows_b, :] for c in range(S5_WIDTH // LANES)], axis=1)
    merged = gate_a * _dot(oa_ref[...], pa_ref[...]) + gate_s * _dot(o_b.astype(BF16), ps_ref[...])
    x1 = x + _dot(merged.astype(BF16), wo_ref[...])
    x1_ref[...] = x1
    h2 = _rmsnorm(x1, g2_ref[...])
    h2_ref[...] = _pack_rows(h2)
    hi, lo = _split_bf16(h2)
    lg_ref[...] = (_dot_t(wrh_ref[...], hi) + _dot_t(wrh_ref[...], lo) + _dot_t(wrl_ref[...], hi)
                   + rb_ref[...])
    _route_tile(lg_ref, tri_ref, e_ref, w_ref, r_ref, cnt_ref, carry_ref,
                (pl.program_id(0) == 0) & (pl.program_id(1) == 0))


def _router_weights(w_rg, b_rg, w_re, b_re):
    def rows(group_part, expert_part):
        return jnp.concatenate([
            group_part, jnp.zeros((RT_EXPERT_ROW - N_EXPERT_GROUPS,) + group_part.shape[1:], F32),
            expert_part, jnp.zeros((LANES - RT_EXPERT_ROW - N_EXPERTS,) + group_part.shape[1:], F32)], axis=0)

    hi, lo = _split_bf16(rows(w_rg.T, w_re.T))
    return hi, lo, rows(b_rg[:, None], b_re[:, None])


def _merge(x2, g1, w_ga, w_gs, o_a, ob_tm, p_a, p_s, w_o, g2, wrh, wrl, rb, batch, seq):
    n = x2.shape[0]
    tm = MRG_TM
    nt = seq // tm
    row = lambda i, b: (b * nt + i, 0)
    const = lambda i, b: (0, 0)
    tok = lambda w: pl.BlockSpec((tm, w), row)
    full = lambda a: pl.BlockSpec(a.shape, const)
    tri = jnp.asarray(np.triu(np.ones((tm, tm), np.float32), 1), BF16)
    lanes_tok = pl.BlockSpec((SUBLANES, tm), lambda i, b: (0, b * nt + i))
    return pl.pallas_call(
        _merge_kernel,
        grid=(nt, batch),
        in_specs=[tok(D_MODEL), full(g1), full(w_ga), full(w_gs), tok(NSA_WIDTH),
                  pl.BlockSpec((S5_WIDTH // LANES, tm * batch, LANES), lambda i, b: (0, i, 0)),
                  full(p_a), full(p_s), full(w_o),
                  full(g2), full(wrh), full(wrl), full(rb), full(tri)],
        out_specs=[tok(D_MODEL), tok(D_MODEL // 2), lanes_tok, lanes_tok, lanes_tok,
                   pl.BlockSpec((N_EXPERTS, LANES), const)],
        out_shape=[jax.ShapeDtypeStruct((n, D_MODEL), F32), jax.ShapeDtypeStruct((n, D_MODEL // 2), jnp.int32),
                   jax.ShapeDtypeStruct((SUBLANES, n), jnp.int32), jax.ShapeDtypeStruct((SUBLANES, n), F32),
                   jax.ShapeDtypeStruct((SUBLANES, n), jnp.int32), jax.ShapeDtypeStruct((N_EXPERTS, LANES), F32)],
        scratch_shapes=[pltpu.VMEM((LANES, tm), F32), pltpu.VMEM((N_EXPERTS, LANES), F32)],
        compiler_params=_cparams("arbitrary", "arbitrary"),
        name="merge_out_proj",
    )(x2, g1, w_ga, w_gs, o_a, ob_tm, p_a, p_s, w_o, g2, wrh, wrl, rb, tri)


def _first_argmax(v, rows):
    vmax = jnp.max(v, axis=0, keepdims=True)
    idx = jnp.min(jnp.where(v == vmax, rows, SUBLANES), axis=0, keepdims=True)
    return vmax, idx


def _route_tile(lg_ref, tri_ref, e_ref, w_ref, r_ref, cnt_ref, carry_ref, first):
    tt = lg_ref.shape[1]

    @pl.when(first)
    def _():
        carry_ref[...] = jnp.zeros(carry_ref.shape, F32)

    rows = lax.broadcasted_iota(jnp.int32, (SUBLANES, tt), 0)
    gl = jnp.where(rows < N_EXPERT_GROUPS, lg_ref[RT_GROUP_ROW:RT_GROUP_ROW + SUBLANES, :], -jnp.inf)
    gmax, g_sel = _first_argmax(gl, rows)
    g_w = 1.0 / jnp.sum(jnp.exp(gl - gmax), axis=0, keepdims=True)

    el = jnp.zeros((EXPERTS_PER_GROUP, tt), F32)
    for g in range(N_EXPERT_GROUPS):
        r0 = RT_EXPERT_ROW + EXPERTS_PER_GROUP * g
        el = jnp.where(g_sel == g, lg_ref[r0:r0 + EXPERTS_PER_GROUP, :], el)
    ex = jnp.exp(el - jnp.max(el, axis=0, keepdims=True))
    p = ex / jnp.sum(ex, axis=0, keepdims=True)
    p1, i1 = _first_argmax(p, rows)
    p2, i2 = _first_argmax(jnp.where(rows == i1, -1.0, p), rows)
    e1 = g_sel * EXPERTS_PER_GROUP + i1
    e2 = g_sel * EXPERTS_PER_GROUP + i2
    scale = g_w / (p1 + p2)

    erow = lax.broadcasted_iota(jnp.int32, (N_EXPERTS, tt), 0)
    oh1 = jnp.where(erow == e1, 1.0, 0.0)
    oh2 = jnp.where(erow == e2, 1.0, 0.0)
    both = oh1 + oh2
    before = carry_ref[:, 0:1] + _dot(both.astype(BF16), tri_ref[...])
    carry_ref[...] = carry_ref[...] + jnp.sum(both, axis=1, keepdims=True)
    cnt_ref[...] = carry_ref[...]

    zi = jnp.zeros((SUBLANES - 2, tt), jnp.int32)
    e_ref[...] = jnp.concatenate([e1, e2, zi], axis=0)
    w_ref[...] = jnp.concatenate([p1 * scale, p2 * scale, zi.astype(F32)], axis=0)
    r_ref[...] = jnp.concatenate([jnp.sum(oh1 * before, axis=0, keepdims=True),
                                  jnp.sum(oh2 * before, axis=0, keepdims=True),
                                  zi.astype(F32)], axis=0).astype(jnp.int32)


def _expert_kernel(be_ref, nb_ref, x_ref, wg_ref, wu_ref, wd_ref, o_ref, wgb_ref, wub_ref, wdb_ref):
    i = pl.program_id(0)
    half = D_MODEL // 2

    @pl.when(i < nb_ref[0])
    def _():
        @pl.when((i == 0) | (be_ref[i] != be_ref[jnp.maximum(i - 1, 0)]))
        def _():
            wgb_ref[...] = wg_ref[...].astype(BF16)
            wub_ref[...] = wu_ref[...].astype(BF16)
            wdb_ref[...] = wd_ref[...].astype(BF16)

        xa, xb = _unpack_rows(x_ref[...])
        xa, xb = xa.astype(BF16), xb.astype(BF16)
        g = _dot(xa, wgb_ref[0:half, :]) + _dot(xb, wgb_ref[half:, :])
        u = _dot(xa, wub_ref[0:half, :]) + _dot(xb, wub_ref[half:, :])
        o_ref[...] = _pack_rows(_dot(((g * _sigmoid(g)) * u).astype(BF16), wdb_ref[...]))

    @pl.when(i >= nb_ref[0])
    def _():
        o_ref[...] = jnp.zeros(o_ref.shape, o_ref.dtype)


def _experts(blk_expert, n_used, buf, w_gate, w_up, w_down, layer):
    cap, width = buf.shape
    blk = MOE_BLK
    row = lambda i, be, nb: (jnp.minimum(i, nb[0] - 1), 0)
    wmap = lambda i, be, nb: (layer, be[jnp.minimum(i, nb[0] - 1)], 0, 0)
    return pl.pallas_call(
        _expert_kernel,
        grid_spec=pltpu.PrefetchScalarGridSpec(
            num_scalar_prefetch=2,
            grid=(cap // blk,),
            in_specs=[pl.BlockSpec((blk, width), row),
                      pl.BlockSpec((None, None, D_MODEL, EXPERT_FF), wmap),
                      pl.BlockSpec((None, None, D_MODEL, EXPERT_FF), wmap),
                      pl.BlockSpec((None, None, EXPERT_FF, D_MODEL), wmap)],
            out_specs=pl.BlockSpec((blk, width), lambda i, be, nb: (i, 0)),
            scratch_shapes=[pltpu.VMEM((D_MODEL, EXPERT_FF), BF16), pltpu.VMEM((D_MODEL, EXPERT_FF), BF16),
                            pltpu.VMEM((EXPERT_FF, D_MODEL), BF16)],
        ),
        out_shape=jax.ShapeDtypeStruct((cap, width), buf.dtype),
        compiler_params=_cparams("arbitrary"),
        name="moe_experts",
    )(blk_expert, n_used, buf, w_gate, w_up, w_down)


SC_CH = 64


def _sc_mesh():
    return plsc.VectorSubcoreMesh(core_axis_name="c", subcore_axis_name="s")


def _sc_worker(info):
    return lax.axis_index("s") * info.num_cores + lax.axis_index("c")


def _sc_gather_rows(table, idx, rows=None):
    info = plsc.get_sparse_core_info()
    lanes, workers = info.num_lanes, info.num_cores * info.num_subcores
    invert = rows is not None
    rows = rows if invert else idx.shape[0]
    width = table.shape[1]
    per_w = rows // workers
    n_chunks = per_w // SC_CH
    assert per_w % (2 * SC_CH) == 0
    scan = 2048
    n_tok = table.shape[0]

    def body(table_hbm, idx_hbm, out_hbm, idx_v, rows_a, rows_b, sem_a, sem_b, *scan_v):
        base = _sc_worker(info) * per_w
        if invert:
            lane = lax.iota(jnp.int32, lanes)

            @pl.loop(0, per_w // lanes)
            def _(j):
                idx_v[pl.ds(j * lanes, lanes)] = (base + j * lanes + lane) & (n_tok - 1)

            @pl.loop(0, idx.shape[0] // scan)
            def _(c):
                pltpu.sync_copy(idx_hbm.at[pl.ds(c * scan, scan)], scan_v[0])

                @plsc.parallel_loop(0, scan // lanes, unroll=8)
                def _(j):
                    rel = scan_v[0][pl.ds(j * lanes, lanes)] - base
                    mine = (rel >= 0) & (rel < per_w)
                    token = (c * scan + j * lanes + lane) & (n_tok - 1)
                    plsc.store_scatter(idx_v, [jnp.where(mine, rel, 0)], token, mask=mine)
        else:
            pltpu.sync_copy(idx_hbm.at[pl.ds(base, per_w)], idx_v)

        def gather(j, buf, sem):
            return pltpu.make_async_copy(table_hbm.at[idx_v.at[pl.ds(j * SC_CH, SC_CH)]], buf, sem)

        def write(j, buf):
            pltpu.sync_copy(buf, out_hbm.at[pl.ds(base + j * SC_CH, SC_CH)])

        gather(0, rows_a, sem_a).start()

        @pl.loop(0, n_chunks, step=2)
        def _(j):
            gather(j + 1, rows_b, sem_b).start()
            gather(j, rows_a, sem_a).wait()
            write(j, rows_a)

            @pl.when(j + 2 < n_chunks)
            def _():
                gather(j + 2, rows_a, sem_a).start()

            gather(j + 1, rows_b, sem_b).wait()
            write(j + 1, rows_b)

    chunk_buf = pltpu.VMEM((SC_CH, width), table.dtype)
    scratch = [pltpu.VMEM((per_w,), jnp.int32), chunk_buf, chunk_buf,
               pltpu.SemaphoreType.DMA, pltpu.SemaphoreType.DMA]
    return pl.kernel(
        body, out_type=jax.ShapeDtypeStruct((rows, width), table.dtype), mesh=_sc_mesh(),
        scratch_types=scratch + ([pltpu.VMEM((scan,), jnp.int32)] if invert else []),
        compiler_params=pltpu.CompilerParams(needs_layout_passes=False) if invert else None,
        name="sc_dispatch_rows" if invert else "sc_gather_rows",
    )(table, idx)


def _blend_kernel(x1_ref, w_ref, g0_ref, g1_ref, o_ref):
    o_ref[...] = _blend_rows(x1_ref[...], w_ref[...], g0_ref[...], g1_ref[...])


def _blend(x1, w_col, gathered):
    n = x1.shape[0]
    tt = DSP_TT
    nt = n // tt
    width = gathered.shape[1]
    return pl.pallas_call(
        _blend_kernel,
        grid=(nt,),
        in_specs=[pl.BlockSpec((tt, D_MODEL), lambda i: (i, 0)),
                  pl.BlockSpec((tt, EXPERT_TOP_K), lambda i: (i, 0)),
                  pl.BlockSpec((tt, width), lambda i: (i, 0)),
                  pl.BlockSpec((tt, width), lambda i: (nt + i, 0))],
        out_specs=pl.BlockSpec((tt, D_MODEL), lambda i: (i, 0)),
        out_shape=jax.ShapeDtypeStruct((n, D_MODEL), F32),
        compiler_params=_cparams("parallel"),
        name="moe_blend",
    )(x1, w_col, gathered, gathered)


def _moe(x1, h2, routing, w_gate, w_up, w_down, layer):
    n = x1.shape[0]
    e_idx, wts, rank, counts = routing
    counts = counts[:, 0].astype(jnp.int32)
    padded = (counts + MOE_BLK - 1) // MOE_BLK * MOE_BLK
    pends = jnp.cumsum(padded)
    pstarts = pends - padded
    experts = jnp.arange(N_EXPERTS, dtype=jnp.int32)
    start_of = jnp.sum(jnp.where(e_idx[None, :EXPERT_TOP_K] == experts[:, None, None],
                                 pstarts[:, None, None], 0), axis=0)
    dest = start_of + rank[:EXPERT_TOP_K]
    cap = n * EXPERT_TOP_K + N_EXPERTS * MOE_BLK
    n_blk = cap // MOE_BLK
    blk_row = jnp.arange(n_blk, dtype=jnp.int32) * MOE_BLK
    blk_expert = jnp.minimum(jnp.sum((pends[None, :] <= blk_row[:, None]).astype(jnp.int32), axis=1),
                             N_EXPERTS - 1)
    n_used = (pends[-1:] // MOE_BLK).astype(jnp.int32)
    dest_flat = dest.reshape(EXPERT_TOP_K * n)
    buf = _sc_gather_rows(h2, dest_flat, rows=cap)
    out_buf = _experts(blk_expert, n_used, buf, w_gate, w_up, w_down, layer)
    return x1, wts[:EXPERT_TOP_K].T, _sc_gather_rows(out_buf, dest_flat)


def kernel(x, positions, norm_mix, norm_ffn, w_in, q_gain, k_gain, cmp_k_pos, cmp_k_w1, cmp_k_w2, cmp_v_pos,
           cmp_v_w1, cmp_v_w2, s5_a_re, s5_a_im, s5_log_dt, s5_b_re, s5_b_im, s5_c_re, s5_c_im, s5_d, s5_glu,
           proj_nsa, proj_s5, w_out, router_group, router_group_bias, router_expert, router_expert_bias,
           expert_gate, expert_up, expert_down):
    batch, seq, _ = x.shape
    depth = w_in.shape[0]
    n = batch * seq
    inv_freq = ROPE_THETA ** (-jnp.arange(0, ROPE_DIMS, 2, dtype=jnp.float32) / ROPE_DIMS)
    ctab, stab = _rope_tables(positions.reshape(n, 1), inv_freq)
    ccmp, scmp = _cmp_rope_tables(ctab, stab, batch, seq)
    x_in = x.reshape(n, D_MODEL)
    for l in range(depth):
        w1, w_ga, w_gs = _pack_w_in(w_in[l])
        o_a, u_tm, x2 = _nsa_branch(x_in, norm_mix[l][None], w1, ctab, stab, ccmp, scmp, q_gain[l], k_gain[l],
                                    (cmp_k_pos[l], cmp_k_w1[l], cmp_k_w2[l]),
                                    (cmp_v_pos[l], cmp_v_w1[l], cmp_v_w2[l]), batch, seq)
        lre, lim, w_b, w_c = _s5_weights(s5_a_re[l], s5_a_im[l], s5_log_dt[l], s5_b_re[l], s5_b_im[l],
                                         s5_c_re[l], s5_c_im[l])
        ob_tm = _s5(u_tm, lre, lim, w_b, w_c, s5_d[l][None],
                    s5_glu[l].astype(BF16), batch, seq)
        wrh, wrl, rb = _router_weights(router_group[l], router_group_bias[l], router_expert[l],
                                       router_expert_bias[l])
        x1, h2, *routing = _merge(x2, norm_mix[l][None], w_ga, w_gs, o_a, ob_tm, proj_nsa[l].astype(BF16),
                                  proj_s5[l].astype(BF16), w_out[l].astype(BF16), norm_ffn[l][None],
                                  wrh, wrl, rb, batch, seq)
        x_in = _moe(x1, h2, routing, expert_gate, expert_up, expert_down, l)
    return _blend(*x_in).reshape(batch, seq, D_MODEL)


def _attn_tables(seq):
    n_cmp = (seq - CMP_BLOCK) // CMP_STRIDE + 1
    n_slc = seq // SLC_BLOCK
    key_blk = np.arange(seq) // SLC_BLOCK
    expand_t = (key_blk[:, None] == np.arange(LANES)[None, :]).astype(np.float32)
    cmp_start = np.arange(LANES) * CMP_STRIDE
    slc_start = np.arange(LANES) * SLC_BLOCK
    overlap_t = ((cmp_start[None, :] < slc_start[:, None] + SLC_BLOCK)
                 & (cmp_start[None, :] + CMP_BLOCK > slc_start[:, None])
                 & (np.arange(LANES)[None, :] < n_cmp) & (np.arange(LANES)[:, None] < n_slc))
    return (jnp.asarray(expand_t, BF16), jnp.asarray(overlap_t.astype(np.float32), BF16), n_cmp, n_slc)


def _attn_mask_tables():
    tq, tk, wk = ATT_TQ, ATT_TK, ATT_WK
    ql = np.arange(tq)
    onehot = np.tile(np.eye(tq, dtype=np.float32), (NSA_HEADS, 1))
    k = np.arange(wk)[:, None]
    band = []
    for c in range(WINDOW // tq + 1):
        if c < WINDOW // tq:
            keep = k <= c * tq + ql[None, :]
        else:
            keep = (k > ql[None, :]) & (k <= ql[None, :] + WINDOW)
        band.append(np.where(keep, 0.0, NEG_BIG))
    kl = np.arange(tk)[None, :]
    causal = [np.where(kl <= c * tq + ql[:, None], 0.0, NEG_BIG) for c in range(tk // tq)]
    return (jnp.asarray(onehot, BF16), jnp.asarray(np.stack(band), BF16),
            jnp.asarray(np.stack(causal), F32))


def _attention(q, ks, vs, kw, vw, kc, vc, gates, batch, seq):
    expand_t, overlap_t, n_cmp, n_slc = _attn_tables(seq)
    onehot, band, causal = _attn_mask_tables()
    tq = ATT_TQ
    nq = seq // tq
    rows = NSA_HEADS * tq
    tile = lambda w: pl.BlockSpec((tq, w), lambda b, i: (b * nq + i, 0))
    per_b = lambda r: pl.BlockSpec((r, LANES), lambda b, i: (b, 0))
    f32_rows = pltpu.VMEM((rows, LANES), F32)
    return pl.pallas_call(
        functools.partial(_attn_kernel, n_cmp=n_cmp, n_slc=n_slc, top_k=min(SLC_TOPK, n_slc)),
        grid=(batch, nq),
        in_specs=[tile(2 * NSA_WIDTH), per_b(seq), per_b(seq), per_b(seq), per_b(seq),
                  per_b(LANES), per_b(LANES), tile(LANES),
                  pl.BlockSpec(expand_t.shape, lambda b, i: (0, 0)),
                  pl.BlockSpec(overlap_t.shape, lambda b, i: (0, 0)),
                  pl.BlockSpec(onehot.shape, lambda b, i: (0, 0)),
                  pl.BlockSpec(band.shape, lambda b, i: (0, 0, 0)),
                  pl.BlockSpec(causal.shape, lambda b, i: (0, 0, 0))],
        out_specs=tile(NSA_WIDTH),
        out_shape=jax.ShapeDtypeStruct((batch * seq, NSA_WIDTH), BF16),
        scratch_shapes=[pltpu.VMEM((rows, 2 * LANES), BF16),
                        pltpu.VMEM((2, rows, ATT_TK), F32), pltpu.VMEM((2, rows, ATT_TK), BF16),
                        pltpu.VMEM((rows, ATT_WK), F32), pltpu.VMEM((rows, ATT_WK), BF16),
                        f32_rows, f32_rows, f32_rows, f32_rows, f32_rows],
        compiler_params=_cparams("parallel", "parallel"),
        name="nsa_attention",
    )(q, ks, vs, kw, vw, kc, vc, gates, expand_t, overlap_t, onehot, band, causal)
```

```python
import functools
import math

import jax
import jax.numpy as jnp
import numpy as np
from jax import lax
from jax.experimental import pallas as pl
from jax.experimental.pallas import tpu as pltpu
from jax.experimental.pallas import tpu_sc as plsc

F32 = jnp.float32
BF16 = jnp.bfloat16

D_MODEL = 1024
NSA_HEADS = 8
NSA_KV_HEADS = 2
NSA_GROUP = NSA_HEADS // NSA_KV_HEADS
HEAD_DIM = 64
ROPE_DIMS = HEAD_DIM // 4
ROPE_THETA = 500000.0
CMP_BLOCK = 32
CMP_STRIDE = 16
CMP_HIDDEN = 2 * HEAD_DIM
SLC_BLOCK = 64
SLC_TOPK = 16
WINDOW = 512
NSA_WIDTH = NSA_HEADS * HEAD_DIM
KV_WIDTH = NSA_KV_HEADS * HEAD_DIM
S5_WIDTH = 256
S5_GROUP_CH = 16
S5_GROUPS = S5_WIDTH // S5_GROUP_CH
S5_STATE = 64
S5_NSTATE = S5_GROUPS * S5_STATE
N_EXPERT_GROUPS = 4
EXPERTS_PER_GROUP = 8
N_EXPERTS = N_EXPERT_GROUPS * EXPERTS_PER_GROUP
EXPERT_TOP_K = 2
EXPERT_FF = 512
RMS_EPS = 1e-6
NEG_BIG = -1e30

LANES = 128
SUBLANES = 8
VMEM_LIMIT = 56 * 1024 * 1024

IN_TM = 512
ATT_TQ = 128
ATT_TK = 512
ATT_WK = WINDOW + ATT_TQ
ATT_RB = 128
S5_T = 256
MRG_TM = 512
MOE_BLK = 512
DSP_TT = 1024

C_Q, C_KC, C_VC, C_KS, C_VS, C_KW, C_VW, C_U, C_G = 0, 512, 640, 768, 896, 1024, 1152, 1280, 1536
IN_W = 1664


def _cparams(*sem):
    return pltpu.CompilerParams(dimension_semantics=sem, vmem_limit_bytes=VMEM_LIMIT)


def _gelu(x):
    return 0.5 * x * (1.0 + jnp.tanh(math.sqrt(2.0 / math.pi) * (x + 0.044715 * (x * x * x))))


def _sigmoid(x):
    return 1.0 / (1.0 + jnp.exp(-x))


def _dot(a, b):
    return jnp.dot(a, b, preferred_element_type=F32)


def _dot_t(a, b):
    return lax.dot_general(a, b, (((1,), (1,)), ((), ())), preferred_element_type=F32)


def _split_bf16(x):
    hi = x.astype(BF16)
    lo = (x - hi.astype(F32)).astype(BF16)
    return hi, lo


def _pack_rows(x):
    k = x.shape[1] // 2
    hi = lax.bitcast_convert_type(x[:, :k].astype(jnp.bfloat16).astype(F32), jnp.int32)
    lo = lax.bitcast_convert_type(x[:, k:].astype(jnp.bfloat16).astype(F32), jnp.int32)
    return hi | lax.shift_right_logical(lo, 16)


def _unpack_rows(w):
    a = lax.bitcast_convert_type(w & jnp.int32(-65536), F32)
    b = lax.bitcast_convert_type(lax.shift_left(w, 16), F32)
    return a, b


def _rope_kernel(pos_ref, freq_ref, sign_ref, c_ref, s_ref):
    ang = pos_ref[...].astype(F32) * freq_ref[...]
    c_ref[...] = jnp.cos(ang)
    s_ref[...] = jnp.sin(ang) * sign_ref[...]


def _rope_tables(pos_col, inv_freq):
    n = pos_col.shape[0]
    lane = np.arange(LANES) % HEAD_DIM
    sel = np.zeros((LANES, ROPE_DIMS // 2), np.float32)
    for l in range(LANES):
        if lane[l] < ROPE_DIMS:
            sel[l, lane[l] % (ROPE_DIMS // 2)] = 1.0
    freq = (jnp.asarray(sel) * inv_freq[None, :]).sum(-1)[None, :]
    sign = np.where(lane < ROPE_DIMS // 2, -1.0, 1.0).astype(np.float32)[None, :]
    tm = 1024
    return pl.pallas_call(
        _rope_kernel,
        grid=(n // tm,),
        in_specs=[pl.BlockSpec((tm, 1), lambda i: (i, 0)),
                  pl.BlockSpec((1, LANES), lambda i: (0, 0)),
                  pl.BlockSpec((1, LANES), lambda i: (0, 0))],
        out_specs=[pl.BlockSpec((tm, LANES), lambda i: (i, 0))] * 2,
        out_shape=[jax.ShapeDtypeStruct((n, LANES), F32)] * 2,
        compiler_params=_cparams("parallel"),
        name="rope_tables",
    )(pos_col, freq, jnp.asarray(sign))


def _headnorm_rope(t, gain, c, s):
    lane = lax.broadcasted_iota(jnp.int32, t.shape, 1)
    lo = lane < HEAD_DIM
    sq = t * t
    s_lo = jnp.sum(jnp.where(lo, sq, 0.0), axis=-1, keepdims=True)
    s_hi = jnp.sum(jnp.where(lo, 0.0, sq), axis=-1, keepdims=True)
    ms = jnp.where(lo, s_lo, s_hi) * (1.0 / HEAD_DIM)
    tn = (t * lax.rsqrt(ms + RMS_EPS)) * gain
    half = ROPE_DIMS // 2
    partner = jnp.where((lane & (HEAD_DIM - 1)) < half,
                        pltpu.roll(tn, LANES - half, 1), pltpu.roll(tn, half, 1))
    return tn * c + partner * s


def _blend_rows(x1, w, g0, g1):
    half = x1.shape[1] // 2
    a0, b0 = _unpack_rows(g0)
    a1, b1 = _unpack_rows(g1)
    w0, w1 = w[:, 0:1], w[:, 1:2]
    return jnp.concatenate([x1[:, 0:half] + w0 * a0 + w1 * a1, x1[:, half:] + w0 * b0 + w1 * b1], axis=1)


def _in_proj_kernel(*refs, blend):
    if blend:
        x1_ref, wc_ref, g0_ref, g1_ref, *refs = refs
    else:
        x_in_ref, *refs = refs
    gain_ref, w_ref, c_ref, s_ref, qg_ref, kg_ref, *refs = refs
    q_ref, ks_ref, kw_ref, kc_ref, vc_ref, vs_ref, vw_ref, u_ref, g_ref, *refs = refs
    if blend:
        x_out_ref, raw_ref = refs
        x = _blend_rows(x1_ref[...], wc_ref[...].T, g0_ref[...], g1_ref[...])
        x_out_ref[...] = x
    else:
        (raw_ref,) = refs
        x = x_in_ref[...]
    ms = jnp.mean(x * x, axis=-1, keepdims=True)
    h = (x * lax.rsqrt(ms + RMS_EPS)) * gain_ref[...]
    z = _dot(h.astype(BF16), w_ref[...])
    c = c_ref[...]
    s = s_ref[...]
    lane = lax.broadcasted_iota(jnp.int32, c.shape, 1)
    lo = lane < HEAD_DIM
    scale = HEAD_DIM ** -0.5 * math.log2(math.e)
    for j in range(NSA_HEADS // 2):
        r = _headnorm_rope(z[:, C_Q + LANES * j:C_Q + LANES * (j + 1)], qg_ref[...], c, s) * scale
        rr = pltpu.roll(r, HEAD_DIM, 1)
        if j // (NSA_GROUP // 2) == 0:
            even, odd = jnp.where(lo, r, 0.0), jnp.where(lo, rr, 0.0)
        else:
            even, odd = jnp.where(lo, 0.0, rr), jnp.where(lo, 0.0, r)
        q_ref[:, 2 * LANES * j:2 * LANES * j + LANES] = even.astype(BF16)
        q_ref[:, 2 * LANES * j + LANES:2 * LANES * (j + 1)] = odd.astype(BF16)
    ks_ref[...] = _headnorm_rope(z[:, C_KS:C_KS + LANES], kg_ref[...], c, s).astype(BF16)
    kw_ref[...] = _headnorm_rope(z[:, C_KW:C_KW + LANES], kg_ref[...], c, s).astype(BF16)
    half = CMP_BLOCK // 2
    for col, out in ((C_KC, kc_ref), (C_VC, vc_ref)):
        raw_ref[...] = z[:, col:col + LANES]
        for l in range(half):
            piece = raw_ref[pl.ds(l, x.shape[0] // half, stride=half), :]
            out[:, LANES * l:LANES * (l + 1)] = piece.astype(BF16)
    vs_ref[...] = z[:, C_VS:C_VS + LANES].astype(BF16)
    vw_ref[...] = z[:, C_VW:C_VW + LANES].astype(BF16)
    u_ref[:, pl.ds(pl.program_id(1), 1), :] = z[:, C_U:C_U + S5_WIDTH][:, None, :]
    g_ref[...] = _sigmoid(z[:, C_G:C_G + LANES])


def _in_proj(x_in, gain, w1, ctab, stab, qg, kg, batch, seq):
    blend = isinstance(x_in, tuple)
    n = batch * seq
    tm = IN_TM
    nt = seq // tm
    row = lambda i, b: (b * nt + i, 0)
    const = lambda i, b: (0, 0)
    tok = lambda w: pl.BlockSpec((tm, w), row)
    half = CMP_BLOCK // 2
    grouped = pl.BlockSpec((tm // half, half * KV_WIDTH), row)
    rows_bf16 = lambda r, w: jax.ShapeDtypeStruct((r, w), BF16)
    if blend:
        x1, w_col, gathered = x_in
        width = gathered.shape[1]
        x_specs = [tok(D_MODEL), pl.BlockSpec((SUBLANES, tm), lambda i, b: (0, b * nt + i)), tok(width),
                   pl.BlockSpec((tm, width), lambda i, b: (n // tm + b * nt + i, 0))]
        x_args = (x1, w_col, gathered, gathered)
    else:
        x_specs, x_args = [tok(D_MODEL)], (x_in,)
    return pl.pallas_call(
        functools.partial(_in_proj_kernel, blend=blend),
        grid=(nt, batch),
        in_specs=x_specs + [pl.BlockSpec((1, D_MODEL), const), pl.BlockSpec((D_MODEL, IN_W), const),
                            tok(LANES), tok(LANES), pl.BlockSpec((1, LANES), const),
                            pl.BlockSpec((1, LANES), const)],
        out_specs=[tok(2 * NSA_WIDTH), tok(LANES), tok(LANES), grouped, grouped, tok(LANES), tok(LANES),
                   pl.BlockSpec((tm, batch, S5_WIDTH), lambda i, b: (i, 0, 0)), tok(LANES)]
        + ([tok(D_MODEL)] if blend else []),
        out_shape=[rows_bf16(n, 2 * NSA_WIDTH), rows_bf16(n, LANES), rows_bf16(n, LANES),
                   rows_bf16(n // half, half * KV_WIDTH), rows_bf16(n // half, half * KV_WIDTH),
                   rows_bf16(n, LANES), rows_bf16(n, LANES),
                   jax.ShapeDtypeStruct((seq, batch, S5_WIDTH), F32), jax.ShapeDtypeStruct((n, LANES), F32)]
        + ([jax.ShapeDtypeStruct((n, D_MODEL), F32)] if blend else []),
        scratch_shapes=[pltpu.VMEM((tm, LANES), F32)],
        compiler_params=_cparams("parallel", "arbitrary"),
        name="in_proj",
    )(*x_args, gain, w1, ctab, stab, qg, kg)


def _compress_mlp(h_ref, w_ref, p_ref, w2_ref):
    w = w_ref[...]
    pre = _dot(h_ref[...], w)
    phi, plo = _split_bf16(p_ref[...])
    pb = _dot(phi, w) + _dot(plo, w)
    rows = pre.shape[0]
    out = None
    for h in range(NSA_KV_HEADS):
        top = pre[:, LANES * h:LANES * (h + 1)]
        bot = pre[:, LANES * (2 + h):LANES * (3 + h)]
        bias = pb[0:1, LANES * h:LANES * (h + 1)] + pb[1:2, LANES * (2 + h):LANES * (3 + h)]
        a = _gelu(top + pltpu.roll(bot, rows - 1, 0) + bias)
        o = _dot(a.astype(BF16), w2_ref[h])
        out = o if out is None else out + o
    return out


def _compress_kernel(hk_ref, hv_ref, wk_ref, wv_ref, pk_ref, pv_ref, w2k_ref, w2v_ref,
                     kg_ref, c_ref, s_ref, kc_ref, vc_ref):
    k = _compress_mlp(hk_ref, wk_ref, pk_ref, w2k_ref)
    kc_ref[...] = _headnorm_rope(k, kg_ref[...], c_ref[...], s_ref[...]).astype(BF16)
    vc_ref[...] = _compress_mlp(hv_ref, wv_ref, pv_ref, w2v_ref).astype(BF16)


def _compress(hk, hv, wk, wv, pk, pv, w2k, w2v, kg, ccmp, scmp, batch):
    rows = hk.shape[0] // batch
    width = hk.shape[1]
    const2 = lambda b: (0, 0)
    const3 = lambda b: (0, 0, 0)
    rowb = lambda w: pl.BlockSpec((rows, w), lambda b: (b, 0))
    return pl.pallas_call(
        _compress_kernel,
        grid=(batch,),
        in_specs=[rowb(width), rowb(width),
                  pl.BlockSpec(wk.shape, const2), pl.BlockSpec(wv.shape, const2),
                  pl.BlockSpec(pk.shape, const2), pl.BlockSpec(pv.shape, const2),
                  pl.BlockSpec(w2k.shape, const3), pl.BlockSpec(w2v.shape, const3),
                  pl.BlockSpec((1, LANES), const2), rowb(LANES), rowb(LANES)],
        out_specs=[rowb(LANES), rowb(LANES)],
        out_shape=[jax.ShapeDtypeStruct((hk.shape[0], LANES), BF16)] * 2,
        compiler_params=_cparams("parallel"),
        name="compress_kv",
    )(hk, hv, wk, wv, pk, pv, w2k, w2v, kg, ccmp, scmp)


def _compress_weights(pos_emb, w1, w2):
    half = CMP_BLOCK // 2
    eye = jnp.eye(NSA_KV_HEADS, dtype=F32)
    w1r = w1.reshape(2, half, HEAD_DIM, CMP_HIDDEN).astype(BF16)
    zero = jnp.zeros_like(w1r[0])
    wexp = jnp.concatenate(
        [jnp.stack([w1r[s] if p == h else zero for p in range(NSA_KV_HEADS)], axis=1)
         .reshape(half * KV_WIDTH, CMP_HIDDEN) for s in range(2) for h in range(NSA_KV_HEADS)], axis=1)
    pexp = jnp.broadcast_to(pos_emb.reshape(2, half, 1, HEAD_DIM), (2, half, NSA_KV_HEADS, HEAD_DIM))
    pexp = jnp.pad(pexp.reshape(2, half * KV_WIDTH), ((0, SUBLANES - 2), (0, 0)))
    w2e = jnp.einsum("cd,ph->pchd", w2, eye).reshape(NSA_KV_HEADS, CMP_HIDDEN, KV_WIDTH)
    return wexp, pexp, w2e.astype(BF16)


def _attn_kernel(q_ref, ks_ref, vs_ref, kw_ref, vw_ref, kc_ref, vc_ref, g_ref, et_ref, ovt_ref, oh_ref, wb_ref,
                 cb_ref, o_ref, qc_ref, s_ref, p_ref, sw_ref, pw_ref, m_ref, al_ref, oc_ref, as_ref, aw_ref,
                 *, n_cmp, n_slc, top_k):
    tq, tk, wk, rb, nh = ATT_TQ, ATT_TK, ATT_WK, ATT_RB, NSA_HEADS
    rows, grp_rows = nh * tq, NSA_GROUP * tq
    qk_rows = 4 * tq
    q0 = pl.program_id(1) * tq
    for a in range(nh):
        qc_ref[a * tq:(a + 1) * tq, 0:LANES] = q_ref[:, LANES * a:LANES * (a + 1)]
    q8 = qc_ref[:, 0:LANES]
    lane = lax.broadcasted_iota(jnp.int32, (tq, LANES), 1)
    t_row = q0 + lax.broadcasted_iota(jnp.int32, (tq, LANES), 0)

    s_c = _dot_t(q8, kc_ref[...]).reshape(nh, tq, LANES)
    mask_c = ((lane * CMP_STRIDE + (CMP_BLOCK - 1)) <= t_row) & (lane < n_cmp)
    sm = jnp.where(mask_c[None], s_c, NEG_BIG)
    p = jnp.where(mask_c[None], jnp.exp2(sm - jnp.max(sm, axis=-1, keepdims=True)), 0.0)
    l = jnp.sum(p, axis=-1, keepdims=True)
    p_c = p * jnp.where(l > 0.0, 1.0 / l, 0.0)
    oc_ref[...] = _dot(p_c.reshape(rows, LANES).astype(BF16), vc_ref[...])

    blk = lax.broadcasted_iota(jnp.int32, (n_slc, tq), 0)
    cur = (q0 + lax.broadcasted_iota(jnp.int32, (n_slc, tq), 1)) >> int(math.log2(SLC_BLOCK))
    forced = (blk == 0) | (blk == cur) | (blk == cur - 1)
    for h in range(NSA_KV_HEADS):
        ps = p_c[NSA_GROUP * h]
        for g in range(1, NSA_GROUP):
            ps = ps + p_c[NSA_GROUP * h + g]
        hi, lo = _split_bf16(ps)
        imp = (_dot_t(ovt_ref[...], hi) + _dot_t(ovt_ref[...], lo))[0:n_slc]
        imp = jnp.where(forced, jnp.inf, jnp.where(blk > cur, -jnp.inf, imp))
        rank = jnp.zeros((n_slc, tq), F32)
        for i in range(n_slc):
            row = imp[i:i + 1, :]
            rank = rank + jnp.where((row > imp) | ((row == imp) & (blk > i)), 1.0, 0.0)
        bias_t = jnp.where(rank < float(top_k), 0.0, NEG_BIG)
        bias = jnp.concatenate([bias_t, jnp.zeros((LANES - n_slc, tq), F32)], axis=0).T.astype(BF16)
        for g in range(NSA_GROUP):
            a = NSA_GROUP * h + g
            qc_ref[a * tq:(a + 1) * tq, LANES:2 * LANES] = bias

    def softmax_pass(s_buf, p_buf, width, causal_case, online):
        for r in range(rows // rb):
            r0 = r * rb
            s = s_buf[pl.ds(r0, rb), 0:width]
            if causal_case is not None:
                s = s + cb_ref[causal_case, pl.ds(r0 % tq, rb), 0:width]
            m_new = jnp.broadcast_to(jnp.max(s, axis=1, keepdims=True), (rb, LANES))
            if online:
                m_old = m_ref[pl.ds(r0, rb), :]
                m_new = jnp.maximum(m_old, m_new)
                al_ref[pl.ds(r0, rb), :] = jnp.exp2(m_old - m_new)
                m_ref[pl.ds(r0, rb), :] = m_new
            pp = jnp.exp2(s - jnp.concatenate([m_new] * (width // LANES), axis=1))
            p_buf[pl.ds(r0, rb), 0:width] = pp.astype(BF16)

    def pv(acc_ref, p_buf, v, width, online):
        lane_v = lax.broadcasted_iota(jnp.int32, v.shape, 1)
        for h in range(NSA_KV_HEADS):
            rs = slice(h * grp_rows, (h + 1) * grp_rows)
            own = (lane_v < HEAD_DIM) if h == 0 else (lane_v >= HEAD_DIM)
            upd = _dot(p_buf[rs, 0:width], jnp.where(own, v, jnp.ones_like(v)))
            acc_ref[rs, :] = (al_ref[rs, :] * acc_ref[rs, :] + upd) if online else upd

    m_ref[...] = jnp.full(m_ref.shape, 0.1 * NEG_BIG, F32)
    as_ref[...] = jnp.zeros(as_ref.shape, F32)

    def slc_chunk(c, diagonal, width=tk):
        k0 = c * tk
        s_buf, p_buf = s_ref.at[c % 2], p_ref.at[c % 2]
        kcat = jnp.concatenate([ks_ref[pl.ds(k0, width), :], et_ref[pl.ds(k0, width), :]], axis=1)
        for g in range(rows // qk_rows):
            rs = slice(g * qk_rows, (g + 1) * qk_rows)
            s_buf[rs, 0:width] = _dot_t(qc_ref[rs, :], kcat)
        softmax_pass(s_buf, p_buf, width, ((q0 - k0) >> int(math.log2(tq))) if diagonal else None, True)
        pv(as_ref, p_buf, vs_ref[pl.ds(k0, width), :], width, True)

    def window_and_output():
        w0 = pl.multiple_of(jnp.maximum(q0 - WINDOW, 0), LANES)
        w_case = jnp.minimum(q0 >> int(math.log2(tq)), WINDOW // tq)
        kcat_w = jnp.concatenate([kw_ref[pl.ds(w0, wk), :], wb_ref[w_case]], axis=1)
        for g in range(rows // qk_rows):
            rs = slice(g * qk_rows, (g + 1) * qk_rows)
            sw_ref[rs, :] = _dot_t(jnp.concatenate([qc_ref[rs, 0:LANES], oh_ref[rs, :]], axis=1), kcat_w)
        softmax_pass(sw_ref, pw_ref, wk, None, False)
        pv(aw_ref, pw_ref, vw_ref[pl.ds(w0, wk), :], wk, False)

        lo_half = lane < HEAD_DIM
        for j in range(nh // 2):
            comb = []
            for a in (2 * j, 2 * j + 1):
                rs = slice(a * tq, (a + 1) * tq)
                o_s, o_w = as_ref[rs, :], aw_ref[rs, :]
                comb.append(g_ref[:, 3 * a:3 * a + 1] * oc_ref[rs, :]
                            + g_ref[:, 3 * a + 1:3 * a + 2] * (o_s * (1.0 / pltpu.roll(o_s, HEAD_DIM, 1)))
                            + g_ref[:, 3 * a + 2:3 * a + 3] * (o_w * (1.0 / pltpu.roll(o_w, HEAD_DIM, 1))))
            even, odd = comb
            if j // (NSA_GROUP // 2) == 0:
                pair = jnp.where(lo_half, even, pltpu.roll(odd, HEAD_DIM, 1))
            else:
                pair = jnp.where(lo_half, pltpu.roll(even, HEAD_DIM, 1), odd)
            o_ref[:, LANES * j:LANES * (j + 1)] = pair.astype(BF16)

    c_last = q0 >> int(math.log2(tk))
    early = ((q0 >> int(math.log2(tq))) & (tk // tq - 1)) < tk // tq // 2
    for n_full in range(ks_ref.shape[0] // tk):
        for first_half in (True, False):
            @pl.when((c_last == n_full) & (early if first_half else jnp.logical_not(early)))
            def _():
                for c in range(n_full):
                    slc_chunk(c, False)
                slc_chunk(n_full, True, tk // 2 if first_half else tk)
                window_and_output()


def _pack_w_in(w):
    sizes = (NSA_WIDTH + 6 * KV_WIDTH, 3 * NSA_HEADS, S5_WIDTH, D_MODEL, D_MODEL)
    o = np.cumsum((0,) + sizes)
    qkv, g, u, ga, gs = [w[:, o[i]:o[i + 1]] for i in range(len(sizes))]
    g = jnp.pad(g, ((0, 0), (0, LANES - 3 * NSA_HEADS)))
    w1 = jnp.concatenate([qkv, u, g], axis=1).astype(BF16)
    return w1, ga.astype(BF16), gs.astype(BF16)


def _nsa_branch(x_in, gain, w1, ctab, stab, ccmp, scmp, q_gain, k_gain, cmp_k, cmp_v, batch, seq):
    qg = jnp.tile(q_gain, 2)[None, :]
    kg = jnp.tile(k_gain, 2)[None, :]
    q, ks, kw, hk, hv, vs, vw, u_tm, gates, *x_out = _in_proj(x_in, gain, w1, ctab, stab, qg, kg, batch, seq)
    wk, pk, w2k = _compress_weights(*cmp_k)
    wv, pv, w2v = _compress_weights(*cmp_v)
    kc, vc = _compress(hk, hv, wk, wv, pk, pv, w2k, w2v, kg, ccmp, scmp, batch)
    o_a = _attention(q, ks, vs, kw, vw, kc, vc, gates, batch, seq)
    return o_a, u_tm, (x_out[0] if x_out else x_in)


def _cmp_rope_tables(ctab, stab, batch, seq):
    def pick(t):
        t = t.reshape(batch, seq, LANES)[:, CMP_BLOCK - 1::CMP_STRIDE]
        t = jnp.pad(t, ((0, 0), (0, seq // CMP_STRIDE - t.shape[1]), (0, 0)))
        return t.reshape(batch * (seq // CMP_STRIDE), LANES)
    return pick(ctab), pick(stab)


def _s5_disc_kernel(are_ref, aim_ref, ldt_ref, lre_ref, lim_ref, fre_ref, fim_ref):
    lam_re = jnp.minimum(are_ref[...], -1e-4)
    lam_im = aim_ref[...]
    dt = jnp.exp(ldt_ref[...])
    mag = jnp.exp(lam_re * dt)
    ang = lam_im * dt
    lb_re = mag * jnp.cos(ang)
    lb_im = mag * jnp.sin(ang)
    den = lam_re * lam_re + lam_im * lam_im
    lre_ref[...] = lb_re
    lim_ref[...] = lb_im
    fre_ref[...] = ((lb_re - 1.0) * lam_re + lb_im * lam_im) / den
    fim_ref[...] = (lb_im * lam_re - (lb_re - 1.0) * lam_im) / den


def _s5_weights(a_re, a_im, log_dt, b_re, b_im, c_re, c_im):
    shp = jax.ShapeDtypeStruct((S5_GROUPS, S5_STATE), F32)
    lb_re, lb_im, f_re, f_im = pl.pallas_call(
        _s5_disc_kernel, out_shape=[shp] * 4, name="s5_discretise",
    )(a_re, a_im, log_dt[:, None])
    bb_re = f_re[..., None] * b_re - f_im[..., None] * b_im
    bb_im = f_re[..., None] * b_im + f_im[..., None] * b_re
    eye = jnp.eye(S5_GROUPS, dtype=F32)

    def in_map(bb):
        return jnp.einsum("gpc,gh->gchp", bb, eye).reshape(S5_WIDTH, S5_NSTATE)

    def out_map(cc):
        return jnp.einsum("gcp,gh->hpgc", cc, eye).reshape(S5_NSTATE, S5_WIDTH)

    w_b = jnp.concatenate([in_map(bb_re), in_map(bb_im)], axis=1).astype(BF16)
    w_c = jnp.concatenate([out_map(c_re), -out_map(c_im)], axis=0).astype(BF16)
    return lb_re.reshape(1, S5_NSTATE), lb_im.reshape(1, S5_NSTATE), w_b, w_c


def _s5_kernel(u_ref, wb_ref, lre_ref, lim_ref, wc_ref, d_ref, wg_ref, o_ref, x_ref, st_ref, *, batch):
    ns = S5_NSTATE

    @pl.when(pl.program_id(0) == 0)
    def _():
        st_ref[...] = jnp.zeros(st_ref.shape, F32)

    u = u_ref[...].reshape(u_ref.shape[0] * batch, S5_WIDTH)
    x_ref[...] = _dot(u.astype(BF16), wb_ref[...])
    lre = jnp.broadcast_to(lre_ref[...], (batch, ns))
    lim = jnp.broadcast_to(lim_ref[...], (batch, ns))

    def step(t, carry):
        xr, xi = carry
        r0 = pl.multiple_of(t * batch, batch)
        nr = lre * xr - lim * xi + x_ref[pl.ds(r0, batch), 0:ns]
        ni = lre * xi + lim * xr + x_ref[pl.ds(r0, batch), ns:2 * ns]
        x_ref[pl.ds(r0, batch), 0:ns] = nr
        x_ref[pl.ds(r0, batch), ns:2 * ns] = ni
        return nr, ni

    steps = u.shape[0] // batch
    xr, xi = lax.fori_loop(0, steps, step, (st_ref[:, 0:ns], st_ref[:, ns:2 * ns]), unroll=4)
    st_ref[:, 0:ns] = xr
    st_ref[:, ns:2 * ns] = xi
    y = _dot(x_ref[...].astype(BF16), wc_ref[...]) + d_ref[...] * u
    z = _dot(_gelu(y).astype(BF16), wg_ref[...])
    o = z[:, :S5_WIDTH] * _sigmoid(z[:, S5_WIDTH:])
    for c in range(S5_WIDTH // LANES):
        o_ref[c] = o[:, c * LANES:(c + 1) * LANES]


def _s5(u_tm, lre, lim, w_b, w_c, d_skip, w_glu, batch, seq):
    rows = S5_T * batch
    const = lambda t: (0, 0)
    return pl.pallas_call(
        functools.partial(_s5_kernel, batch=batch),
        grid=(seq // S5_T,),
        in_specs=[pl.BlockSpec((S5_T, batch, S5_WIDTH), lambda t: (t, 0, 0)),
                  pl.BlockSpec(w_b.shape, const), pl.BlockSpec(lre.shape, const), pl.BlockSpec(lim.shape, const),
                  pl.BlockSpec(w_c.shape, const), pl.BlockSpec((1, S5_WIDTH), const),
                  pl.BlockSpec(w_glu.shape, const)],
        out_specs=pl.BlockSpec((S5_WIDTH // LANES, rows, LANES), lambda t: (0, t, 0)),
        out_shape=jax.ShapeDtypeStruct((S5_WIDTH // LANES, seq * batch, LANES), F32),
        scratch_shapes=[pltpu.VMEM((rows, 2 * S5_NSTATE), F32), pltpu.VMEM((batch, 2 * S5_NSTATE), F32)],
        compiler_params=_cparams("arbitrary"),
        name="s5_scan",
    )(u_tm, w_b, lre, lim, w_c, d_skip, w_glu)


RT_GROUP_ROW = 0
RT_EXPERT_ROW = SUBLANES


def _rmsnorm(x, gain):
    return (x * lax.rsqrt(jnp.mean(x * x, axis=-1, keepdims=True) + RMS_EPS)) * gain


def _merge_kernel(x_ref, g1_ref, wga_ref, wgs_ref, oa_ref, ob_ref, pa_ref, ps_ref, wo_ref, g2_ref,
                  wrh_ref, wrl_ref, rb_ref, tri_ref, x1_ref, h2_ref, e_ref, w_ref, r_ref, cnt_ref,
                  lg_ref, carry_ref):
    x = x_ref[...]
    h = _rmsnorm(x, g1_ref[...]).astype(BF16)
    gate_a = _sigmoid(_dot(h, wga_ref[...]))
    gate_s = _sigmoid(_dot(h, wgs_ref[...]))
    rows_b = pl.ds(pl.program_id(1), x.shape[0], stride=ob_ref.shape[1] // x.shape[0])
    o_b = jnp.concatenate([ob_ref[c, rows_b, :] for c in range(S5_WIDTH // LANES)], axis=1)
    merged = gate_a * _dot(oa_ref[...], pa_ref[...]) + gate_s * _dot(o_b.astype(BF16), ps_ref[...])
    x1 = x + _dot(merged.astype(BF16), wo_ref[...])
    x1_ref[...] = x1
    h2 = _rmsnorm(x1, g2_ref[...])
    h2_ref[...] = _pack_rows(h2)
    hi, lo = _split_bf16(h2)
    lg_ref[...] = (_dot_t(wrh_ref[...], hi) + _dot_t(wrh_ref[...], lo) + _dot_t(wrl_ref[...], hi)
                   + rb_ref[...])
    _route_tile(lg_ref, tri_ref, e_ref, w_ref, r_ref, cnt_ref, carry_ref,
                (pl.program_id(0) == 0) & (pl.program_id(1) == 0))


def _router_weights(w_rg, b_rg, w_re, b_re):
    def rows(group_part, expert_part):
        return jnp.concatenate([
            group_part, jnp.zeros((RT_EXPERT_ROW - N_EXPERT_GROUPS,) + group_part.shape[1:], F32),
            expert_part, jnp.zeros((LANES - RT_EXPERT_ROW - N_EXPERTS,) + group_part.shape[1:], F32)], axis=0)

    hi, lo = _split_bf16(rows(w_rg.T, w_re.T))
    return hi, lo, rows(b_rg[:, None], b_re[:, None])


def _merge(x2, g1, w_ga, w_gs, o_a, ob_tm, p_a, p_s, w_o, g2, wrh, wrl, rb, batch, seq):
    n = x2.shape[0]
    tm = MRG_TM
    nt = seq // tm
    row = lambda i, b: (b * nt + i, 0)
    const = lambda i, b: (0, 0)
    tok = lambda w: pl.BlockSpec((tm, w), row)
    full = lambda a: pl.BlockSpec(a.shape, const)
    tri = jnp.asarray(np.triu(np.ones((tm, tm), np.float32), 1), BF16)
    lanes_tok = pl.BlockSpec((SUBLANES, tm), lambda i, b: (0, b * nt + i))
    return pl.pallas_call(
        _merge_kernel,
        grid=(nt, batch),
        in_specs=[tok(D_MODEL), full(g1), full(w_ga), full(w_gs), tok(NSA_WIDTH),
                  pl.BlockSpec((S5_WIDTH // LANES, tm * batch, LANES), lambda i, b: (0, i, 0)),
                  full(p_a), full(p_s), full(w_o),
                  full(g2), full(wrh), full(wrl), full(rb), full(tri)],
        out_specs=[tok(D_MODEL), tok(D_MODEL // 2), lanes_tok, lanes_tok, lanes_tok,
                   pl.BlockSpec((N_EXPERTS, LANES), const)],
        out_shape=[jax.ShapeDtypeStruct((n, D_MODEL), F32), jax.ShapeDtypeStruct((n, D_MODEL // 2), jnp.int32),
                   jax.ShapeDtypeStruct((SUBLANES, n), jnp.int32), jax.ShapeDtypeStruct((SUBLANES, n), F32),
                   jax.ShapeDtypeStruct((SUBLANES, n), jnp.int32), jax.ShapeDtypeStruct((N_EXPERTS, LANES), F32)],
        scratch_shapes=[pltpu.VMEM((LANES, tm), F32), pltpu.VMEM((N_EXPERTS, LANES), F32)],
        compiler_params=_cparams("arbitrary", "arbitrary"),
        name="merge_out_proj",
    )(x2, g1, w_ga, w_gs, o_a, ob_tm, p_a, p_s, w_o, g2, wrh, wrl, rb, tri)


def _first_argmax(v, rows):
    vmax = jnp.max(v, axis=0, keepdims=True)
    idx = jnp.min(jnp.where(v == vmax, rows, SUBLANES), axis=0, keepdims=True)
    return vmax, idx


def _route_tile(lg_ref, tri_ref, e_ref, w_ref, r_ref, cnt_ref, carry_ref, first):
    tt = lg_ref.shape[1]

    @pl.when(first)
    def _():
        carry_ref[...] = jnp.zeros(carry_ref.shape, F32)

    rows = lax.broadcasted_iota(jnp.int32, (SUBLANES, tt), 0)
    gl = jnp.where(rows < N_EXPERT_GROUPS, lg_ref[RT_GROUP_ROW:RT_GROUP_ROW + SUBLANES, :], -jnp.inf)
    gmax, g_sel = _first_argmax(gl, rows)
    g_w = 1.0 / jnp.sum(jnp.exp(gl - gmax), axis=0, keepdims=True)

    el = jnp.zeros((EXPERTS_PER_GROUP, tt), F32)
    for g in range(N_EXPERT_GROUPS):
        r0 = RT_EXPERT_ROW + EXPERTS_PER_GROUP * g
        el = jnp.where(g_sel == g, lg_ref[r0:r0 + EXPERTS_PER_GROUP, :], el)
    ex = jnp.exp(el - jnp.max(el, axis=0, keepdims=True))
    p = ex / jnp.sum(ex, axis=0, keepdims=True)
    p1, i1 = _first_argmax(p, rows)
    p2, i2 = _first_argmax(jnp.where(rows == i1, -1.0, p), rows)
    e1 = g_sel * EXPERTS_PER_GROUP + i1
    e2 = g_sel * EXPERTS_PER_GROUP + i2
    scale = g_w / (p1 + p2)

    erow = lax.broadcasted_iota(jnp.int32, (N_EXPERTS, tt), 0)
    oh1 = jnp.where(erow == e1, 1.0, 0.0)
    oh2 = jnp.where(erow == e2, 1.0, 0.0)
    both = oh1 + oh2
    before = carry_ref[:, 0:1] + _dot(both.astype(BF16), tri_ref[...])
    carry_ref[...] = carry_ref[...] + jnp.sum(both, axis=1, keepdims=True)
    cnt_ref[...] = carry_ref[...]

    zi = jnp.zeros((SUBLANES - 2, tt), jnp.int32)
    e_ref[...] = jnp.concatenate([e1, e2, zi], axis=0)
    w_ref[...] = jnp.concatenate([p1 * scale, p2 * scale, zi.astype(F32)], axis=0)
    r_ref[...] = jnp.concatenate([jnp.sum(oh1 * before, axis=0, keepdims=True),
                                  jnp.sum(oh2 * before, axis=0, keepdims=True),
                                  zi.astype(F32)], axis=0).astype(jnp.int32)


def _expert_kernel(be_ref, nb_ref, x_ref, wg_ref, wu_ref, wd_ref, o_ref, wgb_ref, wub_ref, wdb_ref):
    i = pl.program_id(0)
    half = D_MODEL // 2

    @pl.when(i < nb_ref[0])
    def _():
        @pl.when((i == 0) | (be_ref[i] != be_ref[jnp.maximum(i - 1, 0)]))
        def _():
            wgb_ref[...] = wg_ref[...].astype(BF16)
            wub_ref[...] = wu_ref[...].astype(BF16)
            wdb_ref[...] = wd_ref[...].astype(BF16)

        xa, xb = _unpack_rows(x_ref[...])
        xa, xb = xa.astype(BF16), xb.astype(BF16)
        g = _dot(xa, wgb_ref[0:half, :]) + _dot(xb, wgb_ref[half:, :])
        u = _dot(xa, wub_ref[0:half, :]) + _dot(xb, wub_ref[half:, :])
        o_ref[...] = _pack_rows(_dot(((g * _sigmoid(g)) * u).astype(BF16), wdb_ref[...]))

    @pl.when(i >= nb_ref[0])
    def _():
        o_ref[...] = jnp.zeros(o_ref.shape, o_ref.dtype)


def _experts(blk_expert, n_used, buf, w_gate, w_up, w_down, layer):
    cap, width = buf.shape
    blk = MOE_BLK
    row = lambda i, be, nb: (jnp.minimum(i, nb[0] - 1), 0)
    wmap = lambda i, be, nb: (layer, be[jnp.minimum(i, nb[0] - 1)], 0, 0)
    return pl.pallas_call(
        _expert_kernel,
        grid_spec=pltpu.PrefetchScalarGridSpec(
            num_scalar_prefetch=2,
            grid=(cap // blk,),
            in_specs=[pl.BlockSpec((blk, width), row),
                      pl.BlockSpec((None, None, D_MODEL, EXPERT_FF), wmap),
                      pl.BlockSpec((None, None, D_MODEL, EXPERT_FF), wmap),
                      pl.BlockSpec((None, None, EXPERT_FF, D_MODEL), wmap)],
            out_specs=pl.BlockSpec((blk, width), lambda i, be, nb: (i, 0)),
            scratch_shapes=[pltpu.VMEM((D_MODEL, EXPERT_FF), BF16), pltpu.VMEM((D_MODEL, EXPERT_FF), BF16),
                            pltpu.VMEM((EXPERT_FF, D_MODEL), BF16)],
        ),
        out_shape=jax.ShapeDtypeStruct((cap, width), buf.dtype),
        compiler_params=_cparams("arbitrary"),
        name="moe_experts",
    )(blk_expert, n_used, buf, w_gate, w_up, w_down)


SC_CH = 64


def _sc_mesh():
    return plsc.VectorSubcoreMesh(core_axis_name="c", subcore_axis_name="s")


def _sc_worker(info):
    return lax.axis_index("s") * info.num_cores + lax.axis_index("c")


def _sc_gather_rows(table, idx, rows=None):
    info = plsc.get_sparse_core_info()
    lanes, workers = info.num_lanes, info.num_cores * info.num_subcores
    invert = rows is not None
    rows = rows if invert else idx.shape[0]
    width = table.shape[1]
    per_w = rows // workers
    n_chunks = per_w // SC_CH
    assert per_w % (2 * SC_CH) == 0
    scan = 2048
    n_tok = table.shape[0]
    assert not invert or (n_tok & (n_tok - 1) == 0 and idx.shape[0] % scan == 0)

    def body(table_hbm, idx_hbm, out_hbm, idx_v, rows_a, rows_b, sem_a, sem_b, *scan_v):
        base = _sc_worker(info) * per_w
        if invert:
            lane = lax.iota(jnp.int32, lanes)

            @pl.loop(0, per_w // lanes)
            def _(j):
                idx_v[pl.ds(j * lanes, lanes)] = (base + j * lanes + lane) & (n_tok - 1)

            @pl.loop(0, idx.shape[0] // scan)
            def _(c):
                pltpu.sync_copy(idx_hbm.at[pl.ds(c * scan, scan)], scan_v[0])

                @plsc.parallel_loop(0, scan // lanes, unroll=8)
                def _(j):
                    rel = scan_v[0][pl.ds(j * lanes, lanes)] - base
                    mine = (rel >= 0) & (rel < per_w)
                    token = (c * scan + j * lanes + lane) & (n_tok - 1)
                    plsc.store_scatter(idx_v, [jnp.where(mine, rel, 0)], token, mask=mine)
        else:
            pltpu.sync_copy(idx_hbm.at[pl.ds(base, per_w)], idx_v)

        def gather(j, buf, sem):
            return pltpu.make_async_copy(table_hbm.at[idx_v.at[pl.ds(j * SC_CH, SC_CH)]], buf, sem)

        def write(j, buf):
            pltpu.sync_copy(buf, out_hbm.at[pl.ds(base + j * SC_CH, SC_CH)])

        gather(0, rows_a, sem_a).start()

        @pl.loop(0, n_chunks, step=2)
        def _(j):
            gather(j + 1, rows_b, sem_b).start()
            gather(j, rows_a, sem_a).wait()
            write(j, rows_a)

            @pl.when(j + 2 < n_chunks)
            def _():
                gather(j + 2, rows_a, sem_a).start()

            gather(j + 1, rows_b, sem_b).wait()
            write(j + 1, rows_b)

    chunk_buf = pltpu.VMEM((SC_CH, width), table.dtype)
    scratch = [pltpu.VMEM((per_w,), jnp.int32), chunk_buf, chunk_buf,
               pltpu.SemaphoreType.DMA, pltpu.SemaphoreType.DMA]
    return pl.kernel(
        body, out_type=jax.ShapeDtypeStruct((rows, width), table.dtype), mesh=_sc_mesh(),
        scratch_types=scratch + ([pltpu.VMEM((scan,), jnp.int32)] if invert else []),
        compiler_params=pltpu.CompilerParams(needs_layout_passes=False) if invert else None,
        name="sc_dispatch_rows" if invert else "sc_gather_rows",
    )(table, idx)


def _blend_kernel(x1_ref, w_ref, g0_ref, g1_ref, o_ref):
    o_ref[...] = _blend_rows(x1_ref[...], w_ref[...].T, g0_ref[...], g1_ref[...])


def _blend(x1, w_col, gathered):
    n = x1.shape[0]
    tt = DSP_TT
    nt = n // tt
    width = gathered.shape[1]
    return pl.pallas_call(
        _blend_kernel,
        grid=(nt,),
        in_specs=[pl.BlockSpec((tt, D_MODEL), lambda i: (i, 0)),
                  pl.BlockSpec((SUBLANES, tt), lambda i: (0, i)),
                  pl.BlockSpec((tt, width), lambda i: (i, 0)),
                  pl.BlockSpec((tt, width), lambda i: (nt + i, 0))],
        out_specs=pl.BlockSpec((tt, D_MODEL), lambda i: (i, 0)),
        out_shape=jax.ShapeDtypeStruct((n, D_MODEL), F32),
        compiler_params=_cparams("parallel"),
        name="moe_blend",
    )(x1, w_col, gathered, gathered)


def _moe(x1, h2, routing, w_gate, w_up, w_down, layer):
    n = x1.shape[0]
    e_idx, wts, rank, counts = routing
    counts = counts[:, 0].astype(jnp.int32)
    padded = (counts + MOE_BLK - 1) // MOE_BLK * MOE_BLK
    pends = jnp.cumsum(padded)
    pstarts = pends - padded
    experts = jnp.arange(N_EXPERTS, dtype=jnp.int32)
    start_of = jnp.sum(jnp.where(e_idx[None, :EXPERT_TOP_K] == experts[:, None, None],
                                 pstarts[:, None, None], 0), axis=0)
    dest = start_of + rank[:EXPERT_TOP_K]
    cap = n * EXPERT_TOP_K + N_EXPERTS * MOE_BLK
    n_blk = cap // MOE_BLK
    blk_row = jnp.arange(n_blk, dtype=jnp.int32) * MOE_BLK
    blk_expert = jnp.minimum(jnp.sum((pends[None, :] <= blk_row[:, None]).astype(jnp.int32), axis=1),
                             N_EXPERTS - 1)
    n_used = (pends[-1:] // MOE_BLK).astype(jnp.int32)
    dest_flat = dest.reshape(EXPERT_TOP_K * n)
    buf = _sc_gather_rows(h2, dest_flat, rows=cap)
    out_buf = _experts(blk_expert, n_used, buf, w_gate, w_up, w_down, layer)
    return x1, wts, _sc_gather_rows(out_buf, dest_flat)


def kernel(x, positions, norm_mix, norm_ffn, w_in, q_gain, k_gain, cmp_k_pos, cmp_k_w1, cmp_k_w2, cmp_v_pos,
           cmp_v_w1, cmp_v_w2, s5_a_re, s5_a_im, s5_log_dt, s5_b_re, s5_b_im, s5_c_re, s5_c_im, s5_d, s5_glu,
           proj_nsa, proj_s5, w_out, router_group, router_group_bias, router_expert, router_expert_bias,
           expert_gate, expert_up, expert_down):
    batch, seq, _ = x.shape
    depth = w_in.shape[0]
    n = batch * seq
    inv_freq = ROPE_THETA ** (-jnp.arange(0, ROPE_DIMS, 2, dtype=jnp.float32) / ROPE_DIMS)
    ctab, stab = _rope_tables(positions.reshape(n, 1), inv_freq)
    ccmp, scmp = _cmp_rope_tables(ctab, stab, batch, seq)
    x_in = x.reshape(n, D_MODEL)
    for l in range(depth):
        w1, w_ga, w_gs = _pack_w_in(w_in[l])
        o_a, u_tm, x2 = _nsa_branch(x_in, norm_mix[l][None], w1, ctab, stab, ccmp, scmp, q_gain[l], k_gain[l],
                                    (cmp_k_pos[l], cmp_k_w1[l], cmp_k_w2[l]),
                                    (cmp_v_pos[l], cmp_v_w1[l], cmp_v_w2[l]), batch, seq)
        lre, lim, w_b, w_c = _s5_weights(s5_a_re[l], s5_a_im[l], s5_log_dt[l], s5_b_re[l], s5_b_im[l],
                                         s5_c_re[l], s5_c_im[l])
        ob_tm = _s5(u_tm, lre, lim, w_b, w_c, s5_d[l][None],
                    s5_glu[l].astype(BF16), batch, seq)
        wrh, wrl, rb = _router_weights(router_group[l], router_group_bias[l], router_expert[l],
                                       router_expert_bias[l])
        x1, h2, *routing = _merge(x2, norm_mix[l][None], w_ga, w_gs, o_a, ob_tm, proj_nsa[l].astype(BF16),
                                  proj_s5[l].astype(BF16), w_out[l].astype(BF16), norm_ffn[l][None],
                                  wrh, wrl, rb, batch, seq)
        x_in = _moe(x1, h2, routing, expert_gate, expert_up, expert_down, l)
    return _blend(*x_in).reshape(batch, seq, D_MODEL)


def _attn_tables(seq):
    n_cmp = (seq - CMP_BLOCK) // CMP_STRIDE + 1
    n_slc = seq // SLC_BLOCK
    key_blk = np.arange(seq) // SLC_BLOCK
    expand_t = (key_blk[:, None] == np.arange(LANES)[None, :]).astype(np.float32)
    cmp_start = np.arange(LANES) * CMP_STRIDE
    slc_start = np.arange(LANES) * SLC_BLOCK
    overlap_t = ((cmp_start[None, :] < slc_start[:, None] + SLC_BLOCK)
                 & (cmp_start[None, :] + CMP_BLOCK > slc_start[:, None])
                 & (np.arange(LANES)[None, :] < n_cmp) & (np.arange(LANES)[:, None] < n_slc))
    return (jnp.asarray(expand_t, BF16), jnp.asarray(overlap_t.astype(np.float32), BF16), n_cmp, n_slc)


def _attn_mask_tables():
    tq, tk, wk = ATT_TQ, ATT_TK, ATT_WK
    ql = np.arange(tq)
    onehot = np.tile(np.eye(tq, dtype=np.float32), (NSA_HEADS, 1))
    k = np.arange(wk)[:, None]
    band = []
    for c in range(WINDOW // tq + 1):
        if c < WINDOW // tq:
            keep = k <= c * tq + ql[None, :]
        else:
            keep = (k > ql[None, :]) & (k <= ql[None, :] + WINDOW)
        band.append(np.where(keep, 0.0, NEG_BIG))
    kl = np.arange(tk)[None, :]
    causal = [np.where(kl <= c * tq + ql[:, None], 0.0, NEG_BIG) for c in range(tk // tq)]
    return (jnp.asarray(onehot, BF16), jnp.asarray(np.stack(band), BF16),
            jnp.asarray(np.stack(causal), F32))


def _attention(q, ks, vs, kw, vw, kc, vc, gates, batch, seq):
    expand_t, overlap_t, n_cmp, n_slc = _attn_tables(seq)
    onehot, band, causal = _attn_mask_tables()
    tq = ATT_TQ
    nq = seq // tq
    rows = NSA_HEADS * tq
    tile = lambda w: pl.BlockSpec((tq, w), lambda b, i: (b * nq + i, 0))
    per_b = lambda r: pl.BlockSpec((r, LANES), lambda b, i: (b, 0))
    f32_rows = pltpu.VMEM((rows, LANES), F32)
    return pl.pallas_call(
        functools.partial(_attn_kernel, n_cmp=n_cmp, n_slc=n_slc, top_k=min(SLC_TOPK, n_slc)),
        grid=(batch, nq),
        in_specs=[tile(2 * NSA_WIDTH), per_b(seq), per_b(seq), per_b(seq), per_b(seq),
                  per_b(LANES), per_b(LANES), tile(LANES),
                  pl.BlockSpec(expand_t.shape, lambda b, i: (0, 0)),
                  pl.BlockSpec(overlap_t.shape, lambda b, i: (0, 0)),
                  pl.BlockSpec(onehot.shape, lambda b, i: (0, 0)),
                  pl.BlockSpec(band.shape, lambda b, i: (0, 0, 0)),
                  pl.BlockSpec(causal.shape, lambda b, i: (0, 0, 0))],
        out_specs=tile(NSA_WIDTH),
        out_shape=jax.ShapeDtypeStruct((batch * seq, NSA_WIDTH), BF16),
        scratch_shapes=[pltpu.VMEM((rows, 2 * LANES), BF16),
                        pltpu.VMEM((2, rows, ATT_TK), F32), pltpu.VMEM((2, rows, ATT_TK), BF16),
                        pltpu.VMEM((rows, ATT_WK), F32), pltpu.VMEM((rows, ATT_WK), BF16),
                        f32_rows, f32_rows, f32_rows, f32_rows, f32_rows],
        compiler_params=_cparams("parallel", "parallel"),
        name="nsa_attention",
    )(q, ks, vs, kw, vw, kc, vc, gates, expand_t, overlap_t, onehot, band, causal)
```

```python
import functools
import math

import jax
import jax.numpy as jnp
import numpy as np
from jax import lax
from jax.experimental import pallas as pl
from jax.experimental.pallas import tpu as pltpu
from jax.experimental.pallas import tpu_sc as plsc

F32 = jnp.float32
BF16 = jnp.bfloat16

D_MODEL = 1024
NSA_HEADS = 8
NSA_KV_HEADS = 2
NSA_GROUP = NSA_HEADS // NSA_KV_HEADS
HEAD_DIM = 64
ROPE_DIMS = HEAD_DIM // 4
ROPE_THETA = 500000.0
CMP_BLOCK = 32
CMP_STRIDE = 16
CMP_HIDDEN = 2 * HEAD_DIM
SLC_BLOCK = 64
SLC_TOPK = 16
WINDOW = 512
NSA_WIDTH = NSA_HEADS * HEAD_DIM
KV_WIDTH = NSA_KV_HEADS * HEAD_DIM
S5_WIDTH = 256
S5_GROUP_CH = 16
S5_GROUPS = S5_WIDTH // S5_GROUP_CH
S5_STATE = 64
S5_NSTATE = S5_GROUPS * S5_STATE
N_EXPERT_GROUPS = 4
EXPERTS_PER_GROUP = 8
N_EXPERTS = N_EXPERT_GROUPS * EXPERTS_PER_GROUP
EXPERT_TOP_K = 2
EXPERT_FF = 512
RMS_EPS = 1e-6
NEG_BIG = -1e30

LANES = 128
SUBLANES = 8
VMEM_LIMIT = 56 * 1024 * 1024

IN_TM = 512
ATT_TQ = 128
ATT_TK = 512
ATT_WK = WINDOW + ATT_TQ
ATT_RB = 64
S5_T = 256
MRG_TM = 512
MOE_BLK = 512
DSP_TT = 1024

C_Q, C_KC, C_VC, C_KS, C_VS, C_KW, C_VW, C_U, C_G = 0, 512, 640, 768, 896, 1024, 1152, 1280, 1536
IN_W = 1664


def _cparams(*sem):
    return pltpu.CompilerParams(dimension_semantics=sem, vmem_limit_bytes=VMEM_LIMIT)


def _gelu(x):
    return 0.5 * x * (1.0 + jnp.tanh(math.sqrt(2.0 / math.pi) * (x + 0.044715 * (x * x * x))))


def _sigmoid(x):
    return 1.0 / (1.0 + jnp.exp(-x))


def _dot(a, b):
    return jnp.dot(a, b, preferred_element_type=F32)


def _dot_t(a, b):
    return lax.dot_general(a, b, (((1,), (1,)), ((), ())), preferred_element_type=F32)


def _split_bf16(x):
    hi = x.astype(BF16)
    lo = (x - hi.astype(F32)).astype(BF16)
    return hi, lo


def _pack_rows(x):
    k = x.shape[1] // 2
    hi = lax.bitcast_convert_type(x[:, :k].astype(jnp.bfloat16).astype(F32), jnp.int32)
    lo = lax.bitcast_convert_type(x[:, k:].astype(jnp.bfloat16).astype(F32), jnp.int32)
    return hi | lax.shift_right_logical(lo, 16)


def _unpack_rows(w):
    a = lax.bitcast_convert_type(w & jnp.int32(-65536), F32)
    b = lax.bitcast_convert_type(lax.shift_left(w, 16), F32)
    return a, b


def _rope_kernel(pos_ref, freq_ref, sign_ref, c_ref, s_ref):
    ang = pos_ref[...].astype(F32) * freq_ref[...]
    c_ref[...] = jnp.cos(ang)
    s_ref[...] = jnp.sin(ang) * sign_ref[...]


def _rope_tables(pos_col, inv_freq):
    n = pos_col.shape[0]
    lane = np.arange(LANES) % HEAD_DIM
    sel = np.zeros((LANES, ROPE_DIMS // 2), np.float32)
    for l in range(LANES):
        if lane[l] < ROPE_DIMS:
            sel[l, lane[l] % (ROPE_DIMS // 2)] = 1.0
    freq = (jnp.asarray(sel) * inv_freq[None, :]).sum(-1)[None, :]
    sign = np.where(lane < ROPE_DIMS // 2, -1.0, 1.0).astype(np.float32)[None, :]
    tm = 1024
    return pl.pallas_call(
        _rope_kernel,
        grid=(n // tm,),
        in_specs=[pl.BlockSpec((tm, 1), lambda i: (i, 0)),
                  pl.BlockSpec((1, LANES), lambda i: (0, 0)),
                  pl.BlockSpec((1, LANES), lambda i: (0, 0))],
        out_specs=[pl.BlockSpec((tm, LANES), lambda i: (i, 0))] * 2,
        out_shape=[jax.ShapeDtypeStruct((n, LANES), F32)] * 2,
        compiler_params=_cparams("parallel"),
        name="rope_tables",
    )(pos_col, freq, jnp.asarray(sign))


def _headnorm_rope(t, gain, c, s):
    lane = lax.broadcasted_iota(jnp.int32, t.shape, 1)
    lo = lane < HEAD_DIM
    sq = t * t
    s_lo = jnp.sum(jnp.where(lo, sq, 0.0), axis=-1, keepdims=True)
    s_hi = jnp.sum(jnp.where(lo, 0.0, sq), axis=-1, keepdims=True)
    ms = jnp.where(lo, s_lo, s_hi) * (1.0 / HEAD_DIM)
    tn = (t * lax.rsqrt(ms + RMS_EPS)) * gain
    half = ROPE_DIMS // 2
    partner = jnp.where((lane & (HEAD_DIM - 1)) < half,
                        pltpu.roll(tn, LANES - half, 1), pltpu.roll(tn, half, 1))
    return tn * c + partner * s


def _blend_rows(x1, w, g0, g1):
    half = x1.shape[1] // 2
    a0, b0 = _unpack_rows(g0)
    a1, b1 = _unpack_rows(g1)
    w0, w1 = w[:, 0:1], w[:, 1:2]
    return jnp.concatenate([x1[:, 0:half] + w0 * a0 + w1 * a1, x1[:, half:] + w0 * b0 + w1 * b1], axis=1)


def _in_proj_kernel(*refs, blend):
    if blend:
        x1_ref, wc_ref, g0_ref, g1_ref, *refs = refs
    else:
        x_in_ref, *refs = refs
    gain_ref, w_ref, c_ref, s_ref, qg_ref, kg_ref, *refs = refs
    q_ref, ks_ref, kw_ref, kc_ref, vc_ref, vs_ref, vw_ref, u_ref, g_ref, *refs = refs
    if blend:
        x_out_ref, raw_ref = refs
        x = _blend_rows(x1_ref[...], wc_ref[...].T, g0_ref[...], g1_ref[...])
        x_out_ref[...] = x
    else:
        (raw_ref,) = refs
        x = x_in_ref[...]
    ms = jnp.mean(x * x, axis=-1, keepdims=True)
    h = (x * lax.rsqrt(ms + RMS_EPS)) * gain_ref[...]
    z = _dot(h.astype(BF16), w_ref[...])
    c = c_ref[...]
    s = s_ref[...]
    lane = lax.broadcasted_iota(jnp.int32, c.shape, 1)
    lo = lane < HEAD_DIM
    scale = HEAD_DIM ** -0.5 * math.log2(math.e)
    for j in range(NSA_HEADS // 2):
        r = _headnorm_rope(z[:, C_Q + LANES * j:C_Q + LANES * (j + 1)], qg_ref[...], c, s) * scale
        rr = pltpu.roll(r, HEAD_DIM, 1)
        if j // (NSA_GROUP // 2) == 0:
            even, odd = jnp.where(lo, r, 0.0), jnp.where(lo, rr, 0.0)
        else:
            even, odd = jnp.where(lo, 0.0, rr), jnp.where(lo, 0.0, r)
        q_ref[:, 2 * LANES * j:2 * LANES * j + LANES] = even.astype(BF16)
        q_ref[:, 2 * LANES * j + LANES:2 * LANES * (j + 1)] = odd.astype(BF16)
    ks_ref[...] = _headnorm_rope(z[:, C_KS:C_KS + LANES], kg_ref[...], c, s).astype(BF16)
    kw_ref[...] = _headnorm_rope(z[:, C_KW:C_KW + LANES], kg_ref[...], c, s).astype(BF16)
    half = CMP_BLOCK // 2
    for col, out in ((C_KC, kc_ref), (C_VC, vc_ref)):
        raw_ref[...] = z[:, col:col + LANES]
        for l in range(half):
            piece = raw_ref[pl.ds(l, x.shape[0] // half, stride=half), :]
            out[:, LANES * l:LANES * (l + 1)] = piece.astype(BF16)
    vs_ref[...] = z[:, C_VS:C_VS + LANES].astype(BF16)
    vw_ref[...] = z[:, C_VW:C_VW + LANES].astype(BF16)
    u_ref[:, pl.ds(pl.program_id(1), 1), :] = z[:, C_U:C_U + S5_WIDTH][:, None, :]
    g_ref[...] = _sigmoid(z[:, C_G:C_G + LANES])


def _in_proj(x_in, gain, w1, ctab, stab, qg, kg, batch, seq):
    blend = isinstance(x_in, tuple)
    n = batch * seq
    tm = IN_TM
    nt = seq // tm
    row = lambda i, b: (b * nt + i, 0)
    const = lambda i, b: (0, 0)
    tok = lambda w: pl.BlockSpec((tm, w), row)
    half = CMP_BLOCK // 2
    grouped = pl.BlockSpec((tm // half, half * KV_WIDTH), row)
    rows_bf16 = lambda r, w: jax.ShapeDtypeStruct((r, w), BF16)
    if blend:
        x1, w_col, gathered = x_in
        width = gathered.shape[1]
        x_specs = [tok(D_MODEL), pl.BlockSpec((SUBLANES, tm), lambda i, b: (0, b * nt + i)), tok(width),
                   pl.BlockSpec((tm, width), lambda i, b: (n // tm + b * nt + i, 0))]
        x_args = (x1, w_col, gathered, gathered)
    else:
        x_specs, x_args = [tok(D_MODEL)], (x_in,)
    return pl.pallas_call(
        functools.partial(_in_proj_kernel, blend=blend),
        grid=(nt, batch),
        in_specs=x_specs + [pl.BlockSpec((1, D_MODEL), const), pl.BlockSpec((D_MODEL, IN_W), const),
                            tok(LANES), tok(LANES), pl.BlockSpec((1, LANES), const),
                            pl.BlockSpec((1, LANES), const)],
        out_specs=[tok(2 * NSA_WIDTH), tok(LANES), tok(LANES), grouped, grouped, tok(LANES), tok(LANES),
                   pl.BlockSpec((tm, batch, S5_WIDTH), lambda i, b: (i, 0, 0)), tok(LANES)]
        + ([tok(D_MODEL)] if blend else []),
        out_shape=[rows_bf16(n, 2 * NSA_WIDTH), rows_bf16(n, LANES), rows_bf16(n, LANES),
                   rows_bf16(n // half, half * KV_WIDTH), rows_bf16(n // half, half * KV_WIDTH),
                   rows_bf16(n, LANES), rows_bf16(n, LANES),
                   jax.ShapeDtypeStruct((seq, batch, S5_WIDTH), F32), jax.ShapeDtypeStruct((n, LANES), F32)]
        + ([jax.ShapeDtypeStruct((n, D_MODEL), F32)] if blend else []),
        scratch_shapes=[pltpu.VMEM((tm, LANES), F32)],
        compiler_params=_cparams("parallel", "arbitrary"),
        name="in_proj",
    )(*x_args, gain, w1, ctab, stab, qg, kg)


def _compress_mlp(h_ref, w_ref, p_ref, w2_ref):
    w = w_ref[...]
    pre = _dot(h_ref[...], w)
    phi, plo = _split_bf16(p_ref[...])
    pb = _dot(phi, w) + _dot(plo, w)
    rows = pre.shape[0]
    out = None
    for h in range(NSA_KV_HEADS):
        top = pre[:, LANES * h:LANES * (h + 1)]
        bot = pre[:, LANES * (2 + h):LANES * (3 + h)]
        bias = pb[0:1, LANES * h:LANES * (h + 1)] + pb[1:2, LANES * (2 + h):LANES * (3 + h)]
        a = _gelu(top + pltpu.roll(bot, rows - 1, 0) + bias)
        o = _dot(a.astype(BF16), w2_ref[h])
        out = o if out is None else out + o
    return out


def _compress_kernel(hk_ref, hv_ref, wk_ref, wv_ref, pk_ref, pv_ref, w2k_ref, w2v_ref,
                     kg_ref, c_ref, s_ref, kc_ref, vc_ref):
    k = _compress_mlp(hk_ref, wk_ref, pk_ref, w2k_ref)
    kc_ref[...] = _headnorm_rope(k, kg_ref[...], c_ref[...], s_ref[...]).astype(BF16)
    vc_ref[...] = _compress_mlp(hv_ref, wv_ref, pv_ref, w2v_ref).astype(BF16)


def _compress(hk, hv, wk, wv, pk, pv, w2k, w2v, kg, ccmp, scmp, batch):
    rows = hk.shape[0] // batch
    width = hk.shape[1]
    const2 = lambda b: (0, 0)
    const3 = lambda b: (0, 0, 0)
    rowb = lambda w: pl.BlockSpec((rows, w), lambda b: (b, 0))
    return pl.pallas_call(
        _compress_kernel,
        grid=(batch,),
        in_specs=[rowb(width), rowb(width),
                  pl.BlockSpec(wk.shape, const2), pl.BlockSpec(wv.shape, const2),
                  pl.BlockSpec(pk.shape, const2), pl.BlockSpec(pv.shape, const2),
                  pl.BlockSpec(w2k.shape, const3), pl.BlockSpec(w2v.shape, const3),
                  pl.BlockSpec((1, LANES), const2), rowb(LANES), rowb(LANES)],
        out_specs=[rowb(LANES), rowb(LANES)],
        out_shape=[jax.ShapeDtypeStruct((hk.shape[0], LANES), BF16)] * 2,
        compiler_params=_cparams("parallel"),
        name="compress_kv",
    )(hk, hv, wk, wv, pk, pv, w2k, w2v, kg, ccmp, scmp)


def _compress_weights(pos_emb, w1, w2):
    half = CMP_BLOCK // 2
    eye = jnp.eye(NSA_KV_HEADS, dtype=F32)
    w1r = w1.reshape(2, half, HEAD_DIM, CMP_HIDDEN).astype(BF16)
    zero = jnp.zeros_like(w1r[0])
    wexp = jnp.concatenate(
        [jnp.stack([w1r[s] if p == h else zero for p in range(NSA_KV_HEADS)], axis=1)
         .reshape(half * KV_WIDTH, CMP_HIDDEN) for s in range(2) for h in range(NSA_KV_HEADS)], axis=1)
    pexp = jnp.broadcast_to(pos_emb.reshape(2, half, 1, HEAD_DIM), (2, half, NSA_KV_HEADS, HEAD_DIM))
    pexp = jnp.pad(pexp.reshape(2, half * KV_WIDTH), ((0, SUBLANES - 2), (0, 0)))
    w2e = jnp.einsum("cd,ph->pchd", w2, eye).reshape(NSA_KV_HEADS, CMP_HIDDEN, KV_WIDTH)
    return wexp, pexp, w2e.astype(BF16)


def _attn_kernel(q_ref, ks_ref, vs_ref, kw_ref, vw_ref, kc_ref, vc_ref, g_ref, et_ref, ovt_ref, oh_ref, wb_ref,
                 cb_ref, o_ref, qc_ref, s_ref, p_ref, sw_ref, pw_ref, m_ref, al_ref, oc_ref, as_ref, aw_ref,
                 *, n_cmp, n_slc, top_k):
    tq, tk, wk, rb, nh = ATT_TQ, ATT_TK, ATT_WK, ATT_RB, NSA_HEADS
    rows, grp_rows = nh * tq, NSA_GROUP * tq
    qk_rows = 4 * tq
    q0 = pl.program_id(1) * tq
    for a in range(nh):
        qc_ref[a * tq:(a + 1) * tq, 0:LANES] = q_ref[:, LANES * a:LANES * (a + 1)]
    q8 = qc_ref[:, 0:LANES]
    lane = lax.broadcasted_iota(jnp.int32, (tq, LANES), 1)
    t_row = q0 + lax.broadcasted_iota(jnp.int32, (tq, LANES), 0)

    s_c = _dot_t(q8, kc_ref[...]).reshape(nh, tq, LANES)
    mask_c = ((lane * CMP_STRIDE + (CMP_BLOCK - 1)) <= t_row) & (lane < n_cmp)
    sm = jnp.where(mask_c[None], s_c, NEG_BIG)
    p = jnp.where(mask_c[None], jnp.exp2(sm - jnp.max(sm, axis=-1, keepdims=True)), 0.0)
    l = jnp.sum(p, axis=-1, keepdims=True)
    p_c = p * jnp.where(l > 0.0, 1.0 / l, 0.0)
    oc_ref[...] = _dot(p_c.reshape(rows, LANES).astype(BF16), vc_ref[...])

    blk = lax.broadcasted_iota(jnp.int32, (n_slc, tq), 0)
    cur = (q0 + lax.broadcasted_iota(jnp.int32, (n_slc, tq), 1)) >> int(math.log2(SLC_BLOCK))
    forced = (blk == 0) | (blk == cur) | (blk == cur - 1)
    for h in range(NSA_KV_HEADS):
        ps = p_c[NSA_GROUP * h]
        for g in range(1, NSA_GROUP):
            ps = ps + p_c[NSA_GROUP * h + g]
        hi, lo = _split_bf16(ps)
        imp = (_dot_t(ovt_ref[...], hi) + _dot_t(ovt_ref[...], lo))[0:n_slc]
        imp = jnp.where(forced, jnp.inf, jnp.where(blk > cur, -jnp.inf, imp))
        rank = jnp.zeros((n_slc, tq), F32)
        for i in range(n_slc):
            row = imp[i:i + 1, :]
            rank = rank + jnp.where((row > imp) | ((row == imp) & (blk > i)), 1.0, 0.0)
        bias_t = jnp.where(rank < float(top_k), 0.0, NEG_BIG)
        bias = jnp.concatenate([bias_t, jnp.zeros((LANES - n_slc, tq), F32)], axis=0).T.astype(BF16)
        for g in range(NSA_GROUP):
            a = NSA_GROUP * h + g
            qc_ref[a * tq:(a + 1) * tq, LANES:2 * LANES] = bias

    def softmax_pass(s_buf, p_buf, width, causal_case, online):
        for r in range(rows // rb):
            r0 = r * rb
            s = s_buf[pl.ds(r0, rb), 0:width]
            if causal_case is not None:
                s = s + cb_ref[causal_case, pl.ds(r0 % tq, rb), 0:width]
            m_new = jnp.broadcast_to(jnp.max(s, axis=1, keepdims=True), (rb, LANES))
            if online:
                m_old = m_ref[pl.ds(r0, rb), :]
                m_new = jnp.maximum(m_old, m_new)
                al_ref[pl.ds(r0, rb), :] = jnp.exp2(m_old - m_new)
                m_ref[pl.ds(r0, rb), :] = m_new
            pp = jnp.exp2(s - jnp.concatenate([m_new] * (width // LANES), axis=1))
            p_buf[pl.ds(r0, rb), 0:width] = pp.astype(BF16)

    def pv(acc_ref, p_buf, v, width, online):
        lane_v = lax.broadcasted_iota(jnp.int32, v.shape, 1)
        for h in range(NSA_KV_HEADS):
            rs = slice(h * grp_rows, (h + 1) * grp_rows)
            own = (lane_v < HEAD_DIM) if h == 0 else (lane_v >= HEAD_DIM)
            upd = _dot(p_buf[rs, 0:width], jnp.where(own, v, jnp.ones_like(v)))
            acc_ref[rs, :] = (al_ref[rs, :] * acc_ref[rs, :] + upd) if online else upd

    m_ref[...] = jnp.full(m_ref.shape, 0.1 * NEG_BIG, F32)
    as_ref[...] = jnp.zeros(as_ref.shape, F32)

    def slc_chunk(c, diagonal, width=tk):
        k0 = c * tk
        s_buf, p_buf = s_ref.at[c % 2], p_ref.at[c % 2]
        kcat = jnp.concatenate([ks_ref[pl.ds(k0, width), :], et_ref[pl.ds(k0, width), :]], axis=1)
        for g in range(rows // qk_rows):
            rs = slice(g * qk_rows, (g + 1) * qk_rows)
            s_buf[rs, 0:width] = _dot_t(qc_ref[rs, :], kcat)
        softmax_pass(s_buf, p_buf, width, ((q0 - k0) >> int(math.log2(tq))) if diagonal else None, True)
        pv(as_ref, p_buf, vs_ref[pl.ds(k0, width), :], width, True)

    def window_and_output():
        w0 = pl.multiple_of(jnp.maximum(q0 - WINDOW, 0), LANES)
        w_case = jnp.minimum(q0 >> int(math.log2(tq)), WINDOW // tq)
        kcat_w = jnp.concatenate([kw_ref[pl.ds(w0, wk), :], wb_ref[w_case]], axis=1)
        for g in range(rows // qk_rows):
            rs = slice(g * qk_rows, (g + 1) * qk_rows)
            sw_ref[rs, :] = _dot_t(jnp.concatenate([qc_ref[rs, 0:LANES], oh_ref[rs, :]], axis=1), kcat_w)
        softmax_pass(sw_ref, pw_ref, wk, None, False)
        pv(aw_ref, pw_ref, vw_ref[pl.ds(w0, wk), :], wk, False)

        lo_half = lane < HEAD_DIM
        for j in range(nh // 2):
            comb = []
            for a in (2 * j, 2 * j + 1):
                rs = slice(a * tq, (a + 1) * tq)
                o_s, o_w = as_ref[rs, :], aw_ref[rs, :]
                comb.append(g_ref[:, 3 * a:3 * a + 1] * oc_ref[rs, :]
                            + g_ref[:, 3 * a + 1:3 * a + 2] * (o_s * (1.0 / pltpu.roll(o_s, HEAD_DIM, 1)))
                            + g_ref[:, 3 * a + 2:3 * a + 3] * (o_w * (1.0 / pltpu.roll(o_w, HEAD_DIM, 1))))
            even, odd = comb
            if j // (NSA_GROUP // 2) == 0:
                pair = jnp.where(lo_half, even, pltpu.roll(odd, HEAD_DIM, 1))
            else:
                pair = jnp.where(lo_half, pltpu.roll(even, HEAD_DIM, 1), odd)
            o_ref[:, LANES * j:LANES * (j + 1)] = pair.astype(BF16)

    c_last = q0 >> int(math.log2(tk))
    early = ((q0 >> int(math.log2(tq))) & (tk // tq - 1)) < tk // tq // 2
    for n_full in range(ks_ref.shape[0] // tk):
        for first_half in (True, False):
            @pl.when((c_last == n_full) & (early if first_half else jnp.logical_not(early)))
            def _():
                for c in range(n_full):
                    slc_chunk(c, False)
                slc_chunk(n_full, True, tk // 2 if first_half else tk)
                window_and_output()


def _pack_w_in(w):
    sizes = (NSA_WIDTH + 6 * KV_WIDTH, 3 * NSA_HEADS, S5_WIDTH, D_MODEL, D_MODEL)
    o = np.cumsum((0,) + sizes)
    qkv, g, u, ga, gs = [w[:, o[i]:o[i + 1]] for i in range(len(sizes))]
    g = jnp.pad(g, ((0, 0), (0, LANES - 3 * NSA_HEADS)))
    w1 = jnp.concatenate([qkv, u, g], axis=1).astype(BF16)
    return w1, ga.astype(BF16), gs.astype(BF16)


def _nsa_branch(x_in, gain, w1, ctab, stab, ccmp, scmp, q_gain, k_gain, cmp_k, cmp_v, batch, seq):
    qg = jnp.tile(q_gain, 2)[None, :]
    kg = jnp.tile(k_gain, 2)[None, :]
    q, ks, kw, hk, hv, vs, vw, u_tm, gates, *x_out = _in_proj(x_in, gain, w1, ctab, stab, qg, kg, batch, seq)
    wk, pk, w2k = _compress_weights(*cmp_k)
    wv, pv, w2v = _compress_weights(*cmp_v)
    kc, vc = _compress(hk, hv, wk, wv, pk, pv, w2k, w2v, kg, ccmp, scmp, batch)
    o_a = _attention(q, ks, vs, kw, vw, kc, vc, gates, batch, seq)
    return o_a, u_tm, (x_out[0] if x_out else x_in)


def _cmp_rope_tables(ctab, stab, batch, seq):
    def pick(t):
        t = t.reshape(batch, seq, LANES)[:, CMP_BLOCK - 1::CMP_STRIDE]
        t = jnp.pad(t, ((0, 0), (0, seq // CMP_STRIDE - t.shape[1]), (0, 0)))
        return t.reshape(batch * (seq // CMP_STRIDE), LANES)
    return pick(ctab), pick(stab)


def _s5_disc_kernel(are_ref, aim_ref, ldt_ref, lre_ref, lim_ref, fre_ref, fim_ref):
    lam_re = jnp.minimum(are_ref[...], -1e-4)
    lam_im = aim_ref[...]
    dt = jnp.exp(ldt_ref[...])
    mag = jnp.exp(lam_re * dt)
    ang = lam_im * dt
    lb_re = mag * jnp.cos(ang)
    lb_im = mag * jnp.sin(ang)
    den = lam_re * lam_re + lam_im * lam_im
    lre_ref[...] = lb_re
    lim_ref[...] = lb_im
    fre_ref[...] = ((lb_re - 1.0) * lam_re + lb_im * lam_im) / den
    fim_ref[...] = (lb_im * lam_re - (lb_re - 1.0) * lam_im) / den


def _s5_weights(a_re, a_im, log_dt, b_re, b_im, c_re, c_im):
    shp = jax.ShapeDtypeStruct((S5_GROUPS, S5_STATE), F32)
    lb_re, lb_im, f_re, f_im = pl.pallas_call(
        _s5_disc_kernel, out_shape=[shp] * 4, name="s5_discretise",
    )(a_re, a_im, log_dt[:, None])
    bb_re = f_re[..., None] * b_re - f_im[..., None] * b_im
    bb_im = f_re[..., None] * b_im + f_im[..., None] * b_re
    eye = jnp.eye(S5_GROUPS, dtype=F32)

    def in_map(bb):
        return jnp.einsum("gpc,gh->gchp", bb, eye).reshape(S5_WIDTH, S5_NSTATE)

    def out_map(cc):
        return jnp.einsum("gcp,gh->hpgc", cc, eye).reshape(S5_NSTATE, S5_WIDTH)

    w_b = jnp.concatenate([in_map(bb_re), in_map(bb_im)], axis=1).astype(BF16)
    w_c = jnp.concatenate([out_map(c_re), -out_map(c_im)], axis=0).astype(BF16)
    return lb_re.reshape(1, S5_NSTATE), lb_im.reshape(1, S5_NSTATE), w_b, w_c


def _s5_kernel(u_ref, wb_ref, lre_ref, lim_ref, wc_ref, d_ref, wg_ref, o_ref, x_ref, st_ref, *, batch):
    ns = S5_NSTATE

    @pl.when(pl.program_id(0) == 0)
    def _():
        st_ref[...] = jnp.zeros(st_ref.shape, F32)

    u = u_ref[...].reshape(u_ref.shape[0] * batch, S5_WIDTH)
    x_ref[...] = _dot(u.astype(BF16), wb_ref[...])
    lre = jnp.broadcast_to(lre_ref[...], (batch, ns))
    lim = jnp.broadcast_to(lim_ref[...], (batch, ns))

    def step(t, carry):
        xr, xi = carry
        r0 = pl.multiple_of(t * batch, batch)
        nr = lre * xr - lim * xi + x_ref[pl.ds(r0, batch), 0:ns]
        ni = lre * xi + lim * xr + x_ref[pl.ds(r0, batch), ns:2 * ns]
        x_ref[pl.ds(r0, batch), 0:ns] = nr
        x_ref[pl.ds(r0, batch), ns:2 * ns] = ni
        return nr, ni

    steps = u.shape[0] // batch
    xr, xi = lax.fori_loop(0, steps, step, (st_ref[:, 0:ns], st_ref[:, ns:2 * ns]), unroll=4)
    st_ref[:, 0:ns] = xr
    st_ref[:, ns:2 * ns] = xi
    y = _dot(x_ref[...].astype(BF16), wc_ref[...]) + d_ref[...] * u
    z = _dot(_gelu(y).astype(BF16), wg_ref[...])
    o = z[:, :S5_WIDTH] * _sigmoid(z[:, S5_WIDTH:])
    for c in range(S5_WIDTH // LANES):
        o_ref[c] = o[:, c * LANES:(c + 1) * LANES]


def _s5(u_tm, lre, lim, w_b, w_c, d_skip, w_glu, batch, seq):
    rows = S5_T * batch
    const = lambda t: (0, 0)
    return pl.pallas_call(
        functools.partial(_s5_kernel, batch=batch),
        grid=(seq // S5_T,),
        in_specs=[pl.BlockSpec((S5_T, batch, S5_WIDTH), lambda t: (t, 0, 0)),
                  pl.BlockSpec(w_b.shape, const), pl.BlockSpec(lre.shape, const), pl.BlockSpec(lim.shape, const),
                  pl.BlockSpec(w_c.shape, const), pl.BlockSpec((1, S5_WIDTH), const),
                  pl.BlockSpec(w_glu.shape, const)],
        out_specs=pl.BlockSpec((S5_WIDTH // LANES, rows, LANES), lambda t: (0, t, 0)),
        out_shape=jax.ShapeDtypeStruct((S5_WIDTH // LANES, seq * batch, LANES), F32),
        scratch_shapes=[pltpu.VMEM((rows, 2 * S5_NSTATE), F32), pltpu.VMEM((batch, 2 * S5_NSTATE), F32)],
        compiler_params=_cparams("arbitrary"),
        name="s5_scan",
    )(u_tm, w_b, lre, lim, w_c, d_skip, w_glu)


RT_GROUP_ROW = 0
RT_EXPERT_ROW = SUBLANES


def _rmsnorm(x, gain):
    return (x * lax.rsqrt(jnp.mean(x * x, axis=-1, keepdims=True) + RMS_EPS)) * gain


def _merge_kernel(x_ref, g1_ref, wga_ref, wgs_ref, oa_ref, ob_ref, pa_ref, ps_ref, wo_ref, g2_ref,
                  wrh_ref, wrl_ref, rb_ref, tri_ref, x1_ref, h2_ref, e_ref, w_ref, r_ref, cnt_ref,
                  lg_ref, carry_ref):
    x = x_ref[...]
    h = _rmsnorm(x, g1_ref[...]).astype(BF16)
    gate_a = _sigmoid(_dot(h, wga_ref[...]))
    gate_s = _sigmoid(_dot(h, wgs_ref[...]))
    rows_b = pl.ds(pl.program_id(1), x.shape[0], stride=ob_ref.shape[1] // x.shape[0])
    o_b = jnp.concatenate([ob_ref[c, rows_b, :] for c in range(S5_WIDTH // LANES)], axis=1)
    merged = gate_a * _dot(oa_ref[...], pa_ref[...]) + gate_s * _dot(o_b.astype(BF16), ps_ref[...])
    x1 = x + _dot(merged.astype(BF16), wo_ref[...])
    x1_ref[...] = x1
    h2 = _rmsnorm(x1, g2_ref[...])
    h2_ref[...] = _pack_rows(h2)
    hi, lo = _split_bf16(h2)
    lg_ref[...] = (_dot_t(wrh_ref[...], hi) + _dot_t(wrh_ref[...], lo) + _dot_t(wrl_ref[...], hi)
                   + rb_ref[...])
    _route_tile(lg_ref, tri_ref, e_ref, w_ref, r_ref, cnt_ref, carry_ref,
                (pl.program_id(0) == 0) & (pl.program_id(1) == 0))


def _router_weights(w_rg, b_rg, w_re, b_re):
    def rows(group_part, expert_part):
        return jnp.concatenate([
            group_part, jnp.zeros((RT_EXPERT_ROW - N_EXPERT_GROUPS,) + group_part.shape[1:], F32),
            expert_part, jnp.zeros((LANES - RT_EXPERT_ROW - N_EXPERTS,) + group_part.shape[1:], F32)], axis=0)

    hi, lo = _split_bf16(rows(w_rg.T, w_re.T))
    return hi, lo, rows(b_rg[:, None], b_re[:, None])


def _merge(x2, g1, w_ga, w_gs, o_a, ob_tm, p_a, p_s, w_o, g2, wrh, wrl, rb, batch, seq):
    n = x2.shape[0]
    tm = MRG_TM
    nt = seq // tm
    row = lambda i, b: (b * nt + i, 0)
    const = lambda i, b: (0, 0)
    tok = lambda w: pl.BlockSpec((tm, w), row)
    full = lambda a: pl.BlockSpec(a.shape, const)
    tri = jnp.asarray(np.triu(np.ones((tm, tm), np.float32), 1), BF16)
    lanes_tok = pl.BlockSpec((SUBLANES, tm), lambda i, b: (0, b * nt + i))
    return pl.pallas_call(
        _merge_kernel,
        grid=(nt, batch),
        in_specs=[tok(D_MODEL), full(g1), full(w_ga), full(w_gs), tok(NSA_WIDTH),
                  pl.BlockSpec((S5_WIDTH // LANES, tm * batch, LANES), lambda i, b: (0, i, 0)),
                  full(p_a), full(p_s), full(w_o),
                  full(g2), full(wrh), full(wrl), full(rb), full(tri)],
        out_specs=[tok(D_MODEL), tok(D_MODEL // 2), lanes_tok, lanes_tok, lanes_tok,
                   pl.BlockSpec((N_EXPERTS, LANES), const)],
        out_shape=[jax.ShapeDtypeStruct((n, D_MODEL), F32), jax.ShapeDtypeStruct((n, D_MODEL // 2), jnp.int32),
                   jax.ShapeDtypeStruct((SUBLANES, n), jnp.int32), jax.ShapeDtypeStruct((SUBLANES, n), F32),
                   jax.ShapeDtypeStruct((SUBLANES, n), jnp.int32), jax.ShapeDtypeStruct((N_EXPERTS, LANES), F32)],
        scratch_shapes=[pltpu.VMEM((LANES, tm), F32), pltpu.VMEM((N_EXPERTS, LANES), F32)],
        compiler_params=_cparams("arbitrary", "arbitrary"),
        name="merge_out_proj",
    )(x2, g1, w_ga, w_gs, o_a, ob_tm, p_a, p_s, w_o, g2, wrh, wrl, rb, tri)


def _first_argmax(v, rows):
    vmax = jnp.max(v, axis=0, keepdims=True)
    idx = jnp.min(jnp.where(v == vmax, rows, SUBLANES), axis=0, keepdims=True)
    return vmax, idx


def _route_tile(lg_ref, tri_ref, e_ref, w_ref, r_ref, cnt_ref, carry_ref, first):
    tt = lg_ref.shape[1]

    @pl.when(first)
    def _():
        carry_ref[...] = jnp.zeros(carry_ref.shape, F32)

    rows = lax.broadcasted_iota(jnp.int32, (SUBLANES, tt), 0)
    gl = jnp.where(rows < N_EXPERT_GROUPS, lg_ref[RT_GROUP_ROW:RT_GROUP_ROW + SUBLANES, :], -jnp.inf)
    gmax, g_sel = _first_argmax(gl, rows)
    g_w = 1.0 / jnp.sum(jnp.exp(gl - gmax), axis=0, keepdims=True)

    el = jnp.zeros((EXPERTS_PER_GROUP, tt), F32)
    for g in range(N_EXPERT_GROUPS):
        r0 = RT_EXPERT_ROW + EXPERTS_PER_GROUP * g
        el = jnp.where(g_sel == g, lg_ref[r0:r0 + EXPERTS_PER_GROUP, :], el)
    ex = jnp.exp(el - jnp.max(el, axis=0, keepdims=True))
    p = ex / jnp.sum(ex, axis=0, keepdims=True)
    p1, i1 = _first_argmax(p, rows)
    p2, i2 = _first_argmax(jnp.where(rows == i1, -1.0, p), rows)
    e1 = g_sel * EXPERTS_PER_GROUP + i1
    e2 = g_sel * EXPERTS_PER_GROUP + i2
    scale = g_w / (p1 + p2)

    erow = lax.broadcasted_iota(jnp.int32, (N_EXPERTS, tt), 0)
    oh1 = jnp.where(erow == e1, 1.0, 0.0)
    oh2 = jnp.where(erow == e2, 1.0, 0.0)
    both = oh1 + oh2
    before = carry_ref[:, 0:1] + _dot(both.astype(BF16), tri_ref[...])
    carry_ref[...] = carry_ref[...] + jnp.sum(both, axis=1, keepdims=True)
    cnt_ref[...] = carry_ref[...]

    zi = jnp.zeros((SUBLANES - 2, tt), jnp.int32)
    e_ref[...] = jnp.concatenate([e1, e2, zi], axis=0)
    w_ref[...] = jnp.concatenate([p1 * scale, p2 * scale, zi.astype(F32)], axis=0)
    r_ref[...] = jnp.concatenate([jnp.sum(oh1 * before, axis=0, keepdims=True),
                                  jnp.sum(oh2 * before, axis=0, keepdims=True),
                                  zi.astype(F32)], axis=0).astype(jnp.int32)


def _expert_kernel(be_ref, nb_ref, x_ref, wg_ref, wu_ref, wd_ref, o_ref, wgb_ref, wub_ref, wdb_ref):
    i = pl.program_id(0)
    half = D_MODEL // 2

    def mlp(wg, wu, wd):
        xa, xb = _unpack_rows(x_ref[...])
        xa, xb = xa.astype(BF16), xb.astype(BF16)
        g = _dot(xa, wg[0:half, :]) + _dot(xb, wg[half:, :])
        u = _dot(xa, wu[0:half, :]) + _dot(xb, wu[half:, :])
        o_ref[...] = _pack_rows(_dot(((g * _sigmoid(g)) * u).astype(BF16), wd))

    new_expert = (i == 0) | (be_ref[i] != be_ref[jnp.maximum(i - 1, 0)])

    @pl.when((i < nb_ref[0]) & new_expert)
    def _():
        wg, wu, wd = wg_ref[...].astype(BF16), wu_ref[...].astype(BF16), wd_ref[...].astype(BF16)
        wgb_ref[...] = wg
        wub_ref[...] = wu
        wdb_ref[...] = wd
        mlp(wg, wu, wd)

    @pl.when((i < nb_ref[0]) & jnp.logical_not(new_expert))
    def _():
        mlp(wgb_ref[...], wub_ref[...], wdb_ref[...])

    @pl.when(i >= nb_ref[0])
    def _():
        o_ref[...] = jnp.zeros(o_ref.shape, o_ref.dtype)


def _experts(blk_expert, n_used, buf, w_gate, w_up, w_down, layer):
    cap, width = buf.shape
    blk = MOE_BLK
    row = lambda i, be, nb: (jnp.minimum(i, nb[0] - 1), 0)
    wmap = lambda i, be, nb: (layer, be[jnp.minimum(i, nb[0] - 1)], 0, 0)
    return pl.pallas_call(
        _expert_kernel,
        grid_spec=pltpu.PrefetchScalarGridSpec(
            num_scalar_prefetch=2,
            grid=(cap // blk,),
            in_specs=[pl.BlockSpec((blk, width), row),
                      pl.BlockSpec((None, None, D_MODEL, EXPERT_FF), wmap),
                      pl.BlockSpec((None, None, D_MODEL, EXPERT_FF), wmap),
                      pl.BlockSpec((None, None, EXPERT_FF, D_MODEL), wmap)],
            out_specs=pl.BlockSpec((blk, width), lambda i, be, nb: (i, 0)),
            scratch_shapes=[pltpu.VMEM((D_MODEL, EXPERT_FF), BF16), pltpu.VMEM((D_MODEL, EXPERT_FF), BF16),
                            pltpu.VMEM((EXPERT_FF, D_MODEL), BF16)],
        ),
        out_shape=jax.ShapeDtypeStruct((cap, width), buf.dtype),
        compiler_params=_cparams("arbitrary"),
        name="moe_experts",
    )(blk_expert, n_used, buf, w_gate, w_up, w_down)


SC_CH = 64


def _sc_mesh():
    return plsc.VectorSubcoreMesh(core_axis_name="c", subcore_axis_name="s")


def _sc_worker(info):
    return lax.axis_index("s") * info.num_cores + lax.axis_index("c")


def _sc_gather_rows(table, idx, rows=None):
    info = plsc.get_sparse_core_info()
    lanes, workers = info.num_lanes, info.num_cores * info.num_subcores
    invert = rows is not None
    rows = rows if invert else idx.shape[0]
    width = table.shape[1]
    per_w = rows // workers
    n_chunks = per_w // SC_CH
    assert per_w % (2 * SC_CH) == 0
    scan = 2048
    n_tok = table.shape[0]
    assert not invert or (n_tok & (n_tok - 1) == 0 and idx.shape[0] % scan == 0)

    def body(table_hbm, idx_hbm, out_hbm, idx_v, rows_a, rows_b, sem_a, sem_b, *scan_v):
        base = _sc_worker(info) * per_w
        if invert:
            lane = lax.iota(jnp.int32, lanes)

            @pl.loop(0, per_w // lanes)
            def _(j):
                idx_v[pl.ds(j * lanes, lanes)] = (base + j * lanes + lane) & (n_tok - 1)

            @pl.loop(0, idx.shape[0] // scan)
            def _(c):
                pltpu.sync_copy(idx_hbm.at[pl.ds(c * scan, scan)], scan_v[0])

                @plsc.parallel_loop(0, scan // lanes, unroll=8)
                def _(j):
                    rel = scan_v[0][pl.ds(j * lanes, lanes)] - base
                    mine = (rel >= 0) & (rel < per_w)
                    token = (c * scan + j * lanes + lane) & (n_tok - 1)
                    plsc.store_scatter(idx_v, [jnp.where(mine, rel, 0)], token, mask=mine)
        else:
            pltpu.sync_copy(idx_hbm.at[pl.ds(base, per_w)], idx_v)

        def gather(j, buf, sem):
            return pltpu.make_async_copy(table_hbm.at[idx_v.at[pl.ds(j * SC_CH, SC_CH)]], buf, sem)

        def write(j, buf):
            pltpu.sync_copy(buf, out_hbm.at[pl.ds(base + j * SC_CH, SC_CH)])

        gather(0, rows_a, sem_a).start()

        @pl.loop(0, n_chunks, step=2)
        def _(j):
            gather(j + 1, rows_b, sem_b).start()
            gather(j, rows_a, sem_a).wait()
            write(j, rows_a)

            @pl.when(j + 2 < n_chunks)
            def _():
                gather(j + 2, rows_a, sem_a).start()

            gather(j + 1, rows_b, sem_b).wait()
            write(j + 1, rows_b)

    chunk_buf = pltpu.VMEM((SC_CH, width), table.dtype)
    scratch = [pltpu.VMEM((per_w,), jnp.int32), chunk_buf, chunk_buf,
               pltpu.SemaphoreType.DMA, pltpu.SemaphoreType.DMA]
    return pl.kernel(
        body, out_type=jax.ShapeDtypeStruct((rows, width), table.dtype), mesh=_sc_mesh(),
        scratch_types=scratch + ([pltpu.VMEM((scan,), jnp.int32)] if invert else []),
        compiler_params=pltpu.CompilerParams(needs_layout_passes=False) if invert else None,
        name="sc_dispatch_rows" if invert else "sc_gather_rows",
    )(table, idx)


def _blend_kernel(x1_ref, w_ref, g0_ref, g1_ref, o_ref):
    o_ref[...] = _blend_rows(x1_ref[...], w_ref[...].T, g0_ref[...], g1_ref[...])


def _blend(x1, w_col, gathered):
    n = x1.shape[0]
    tt = DSP_TT
    nt = n // tt
    width = gathered.shape[1]
    return pl.pallas_call(
        _blend_kernel,
        grid=(nt,),
        in_specs=[pl.BlockSpec((tt, D_MODEL), lambda i: (i, 0)),
                  pl.BlockSpec((SUBLANES, tt), lambda i: (0, i)),
                  pl.BlockSpec((tt, width), lambda i: (i, 0)),
                  pl.BlockSpec((tt, width), lambda i: (nt + i, 0))],
        out_specs=pl.BlockSpec((tt, D_MODEL), lambda i: (i, 0)),
        out_shape=jax.ShapeDtypeStruct((n, D_MODEL), F32),
        compiler_params=_cparams("parallel"),
        name="moe_blend",
    )(x1, w_col, gathered, gathered)


def _moe(x1, h2, routing, w_gate, w_up, w_down, layer):
    n = x1.shape[0]
    e_idx, wts, rank, counts = routing
    counts = counts[:, 0].astype(jnp.int32)
    padded = (counts + MOE_BLK - 1) // MOE_BLK * MOE_BLK
    pends = jnp.cumsum(padded)
    pstarts = pends - padded
    experts = jnp.arange(N_EXPERTS, dtype=jnp.int32)
    start_of = jnp.sum(jnp.where(e_idx[None, :EXPERT_TOP_K] == experts[:, None, None],
                                 pstarts[:, None, None], 0), axis=0)
    dest = start_of + rank[:EXPERT_TOP_K]
    cap = n * EXPERT_TOP_K + N_EXPERTS * MOE_BLK
    n_blk = cap // MOE_BLK
    blk_row = jnp.arange(n_blk, dtype=jnp.int32) * MOE_BLK
    blk_expert = jnp.minimum(jnp.sum((pends[None, :] <= blk_row[:, None]).astype(jnp.int32), axis=1),
                             N_EXPERTS - 1)
    n_used = (pends[-1:] // MOE_BLK).astype(jnp.int32)
    dest_flat = dest.reshape(EXPERT_TOP_K * n)
    buf = _sc_gather_rows(h2, dest_flat, rows=cap)
    out_buf = _experts(blk_expert, n_used, buf, w_gate, w_up, w_down, layer)
    return x1, wts, _sc_gather_rows(out_buf, dest_flat)


def kernel(x, positions, norm_mix, norm_ffn, w_in, q_gain, k_gain, cmp_k_pos, cmp_k_w1, cmp_k_w2, cmp_v_pos,
           cmp_v_w1, cmp_v_w2, s5_a_re, s5_a_im, s5_log_dt, s5_b_re, s5_b_im, s5_c_re, s5_c_im, s5_d, s5_glu,
           proj_nsa, proj_s5, w_out, router_group, router_group_bias, router_expert, router_expert_bias,
           expert_gate, expert_up, expert_down):
    batch, seq, _ = x.shape
    depth = w_in.shape[0]
    n = batch * seq
    inv_freq = ROPE_THETA ** (-jnp.arange(0, ROPE_DIMS, 2, dtype=jnp.float32) / ROPE_DIMS)
    ctab, stab = _rope_tables(positions.reshape(n, 1), inv_freq)
    ccmp, scmp = _cmp_rope_tables(ctab, stab, batch, seq)
    x_in = x.reshape(n, D_MODEL)
    for l in range(depth):
        w1, w_ga, w_gs = _pack_w_in(w_in[l])
        o_a, u_tm, x2 = _nsa_branch(x_in, norm_mix[l][None], w1, ctab, stab, ccmp, scmp, q_gain[l], k_gain[l],
                                    (cmp_k_pos[l], cmp_k_w1[l], cmp_k_w2[l]),
                                    (cmp_v_pos[l], cmp_v_w1[l], cmp_v_w2[l]), batch, seq)
        lre, lim, w_b, w_c = _s5_weights(s5_a_re[l], s5_a_im[l], s5_log_dt[l], s5_b_re[l], s5_b_im[l],
                                         s5_c_re[l], s5_c_im[l])
        ob_tm = _s5(u_tm, lre, lim, w_b, w_c, s5_d[l][None],
                    s5_glu[l].astype(BF16), batch, seq)
        wrh, wrl, rb = _router_weights(router_group[l], router_group_bias[l], router_expert[l],
                                       router_expert_bias[l])
        x1, h2, *routing = _merge(x2, norm_mix[l][None], w_ga, w_gs, o_a, ob_tm, proj_nsa[l].astype(BF16),
                                  proj_s5[l].astype(BF16), w_out[l].astype(BF16), norm_ffn[l][None],
                                  wrh, wrl, rb, batch, seq)
        x_in = _moe(x1, h2, routing, expert_gate, expert_up, expert_down, l)
    return _blend(*x_in).reshape(batch, seq, D_MODEL)


def _attn_tables(seq):
    n_cmp = (seq - CMP_BLOCK) // CMP_STRIDE + 1
    n_slc = seq // SLC_BLOCK
    key_blk = np.arange(seq) // SLC_BLOCK
    expand_t = (key_blk[:, None] == np.arange(LANES)[None, :]).astype(np.float32)
    cmp_start = np.arange(LANES) * CMP_STRIDE
    slc_start = np.arange(LANES) * SLC_BLOCK
    overlap_t = ((cmp_start[None, :] < slc_start[:, None] + SLC_BLOCK)
                 & (cmp_start[None, :] + CMP_BLOCK > slc_start[:, None])
                 & (np.arange(LANES)[None, :] < n_cmp) & (np.arange(LANES)[:, None] < n_slc))
    return (jnp.asarray(expand_t, BF16), jnp.asarray(overlap_t.astype(np.float32), BF16), n_cmp, n_slc)


def _attn_mask_tables():
    tq, tk, wk = ATT_TQ, ATT_TK, ATT_WK
    ql = np.arange(tq)
    onehot = np.tile(np.eye(tq, dtype=np.float32), (NSA_HEADS, 1))
    k = np.arange(wk)[:, None]
    band = []
    for c in range(WINDOW // tq + 1):
        if c < WINDOW // tq:
            keep = k <= c * tq + ql[None, :]
        else:
            keep = (k > ql[None, :]) & (k <= ql[None, :] + WINDOW)
        band.append(np.where(keep, 0.0, NEG_BIG))
    kl = np.arange(tk)[None, :]
    causal = [np.where(kl <= c * tq + ql[:, None], 0.0, NEG_BIG) for c in range(tk // tq)]
    return (jnp.asarray(onehot, BF16), jnp.asarray(np.stack(band), BF16),
            jnp.asarray(np.stack(causal), F32))


def _attention(q, ks, vs, kw, vw, kc, vc, gates, batch, seq):
    expand_t, overlap_t, n_cmp, n_slc = _attn_tables(seq)
    onehot, band, causal = _attn_mask_tables()
    tq = ATT_TQ
    nq = seq // tq
    rows = NSA_HEADS * tq
    tile = lambda w: pl.BlockSpec((tq, w), lambda b, i: (b * nq + i, 0))
    per_b = lambda r: pl.BlockSpec((r, LANES), lambda b, i: (b, 0))
    f32_rows = pltpu.VMEM((rows, LANES), F32)
    return pl.pallas_call(
        functools.partial(_attn_kernel, n_cmp=n_cmp, n_slc=n_slc, top_k=min(SLC_TOPK, n_slc)),
        grid=(batch, nq),
        in_specs=[tile(2 * NSA_WIDTH), per_b(seq), per_b(seq), per_b(seq), per_b(seq),
                  per_b(LANES), per_b(LANES), tile(LANES),
                  pl.BlockSpec(expand_t.shape, lambda b, i: (0, 0)),
                  pl.BlockSpec(overlap_t.shape, lambda b, i: (0, 0)),
                  pl.BlockSpec(onehot.shape, lambda b, i: (0, 0)),
                  pl.BlockSpec(band.shape, lambda b, i: (0, 0, 0)),
                  pl.BlockSpec(causal.shape, lambda b, i: (0, 0, 0))],
        out_specs=tile(NSA_WIDTH),
        out_shape=jax.ShapeDtypeStruct((batch * seq, NSA_WIDTH), BF16),
        scratch_shapes=[pltpu.VMEM((rows, 2 * LANES), BF16),
                        pltpu.VMEM((2, rows, ATT_TK), F32), pltpu.VMEM((2, rows, ATT_TK), BF16),
                        pltpu.VMEM((rows, ATT_WK), F32), pltpu.VMEM((rows, ATT_WK), BF16),
                        f32_rows, f32_rows, f32_rows, f32_rows, f32_rows],
        compiler_params=_cparams("parallel", "parallel"),
        name="nsa_attention",
    )(q, ks, vs, kw, vw, kc, vc, gates, expand_t, overlap_t, onehot, band, causal)
```

```python
import functools
import math

import jax
import jax.numpy as jnp
import numpy as np
from jax import lax
from jax.experimental import pallas as pl
from jax.experimental.pallas import tpu as pltpu
from jax.experimental.pallas import tpu_sc as plsc

F32 = jnp.float32
BF16 = jnp.bfloat16

D_MODEL = 1024
NSA_HEADS = 8
NSA_KV_HEADS = 2
NSA_GROUP = NSA_HEADS // NSA_KV_HEADS
HEAD_DIM = 64
ROPE_DIMS = HEAD_DIM // 4
ROPE_THETA = 500000.0
CMP_BLOCK = 32
CMP_STRIDE = 16
CMP_HIDDEN = 2 * HEAD_DIM
SLC_BLOCK = 64
SLC_TOPK = 16
WINDOW = 512
NSA_WIDTH = NSA_HEADS * HEAD_DIM
KV_WIDTH = NSA_KV_HEADS * HEAD_DIM
S5_WIDTH = 256
S5_GROUP_CH = 16
S5_GROUPS = S5_WIDTH // S5_GROUP_CH
S5_STATE = 64
S5_NSTATE = S5_GROUPS * S5_STATE
N_EXPERT_GROUPS = 4
EXPERTS_PER_GROUP = 8
N_EXPERTS = N_EXPERT_GROUPS * EXPERTS_PER_GROUP
EXPERT_TOP_K = 2
EXPERT_FF = 512
RMS_EPS = 1e-6
NEG_BIG = -1e30

LANES = 128
SUBLANES = 8
VMEM_LIMIT = 56 * 1024 * 1024

IN_TM = 512
ATT_TQ = 128
ATT_TK = 512
ATT_WK = WINDOW + ATT_TQ
ATT_RB = 128
S5_T = 256
MRG_TM = 512
MOE_BLK = 512
DSP_TT = 1024

C_Q, C_KC, C_VC, C_KS, C_VS, C_KW, C_VW, C_U, C_G = 0, 512, 640, 768, 896, 1024, 1152, 1280, 1536
IN_W = 1664


def _cparams(*sem):
    return pltpu.CompilerParams(dimension_semantics=sem, vmem_limit_bytes=VMEM_LIMIT)


def _gelu(x):
    return 0.5 * x * (1.0 + jnp.tanh(math.sqrt(2.0 / math.pi) * (x + 0.044715 * (x * x * x))))


def _sigmoid(x):
    return 1.0 / (1.0 + jnp.exp(-x))


def _dot(a, b):
    return jnp.dot(a, b, preferred_element_type=F32)


def _dot_t(a, b):
    return lax.dot_general(a, b, (((1,), (1,)), ((), ())), preferred_element_type=F32)


def _split_bf16(x):
    hi = x.astype(BF16)
    lo = (x - hi.astype(F32)).astype(BF16)
    return hi, lo


def _pack_rows(x):
    k = x.shape[1] // 2
    hi = lax.bitcast_convert_type(x[:, :k].astype(jnp.bfloat16).astype(F32), jnp.int32)
    lo = lax.bitcast_convert_type(x[:, k:].astype(jnp.bfloat16).astype(F32), jnp.int32)
    return hi | lax.shift_right_logical(lo, 16)


def _unpack_rows(w):
    a = lax.bitcast_convert_type(w & jnp.int32(-65536), F32)
    b = lax.bitcast_convert_type(lax.shift_left(w, 16), F32)
    return a, b


def _rope_kernel(pos_ref, freq_ref, sign_ref, c_ref, s_ref):
    ang = pos_ref[...].astype(F32) * freq_ref[...]
    c_ref[...] = jnp.cos(ang)
    s_ref[...] = jnp.sin(ang) * sign_ref[...]


def _rope_tables(pos_col, inv_freq):
    n = pos_col.shape[0]
    lane = np.arange(LANES) % HEAD_DIM
    sel = np.zeros((LANES, ROPE_DIMS // 2), np.float32)
    for l in range(LANES):
        if lane[l] < ROPE_DIMS:
            sel[l, lane[l] % (ROPE_DIMS // 2)] = 1.0
    freq = (jnp.asarray(sel) * inv_freq[None, :]).sum(-1)[None, :]
    sign = np.where(lane < ROPE_DIMS // 2, -1.0, 1.0).astype(np.float32)[None, :]
    tm = 1024
    return pl.pallas_call(
        _rope_kernel,
        grid=(n // tm,),
        in_specs=[pl.BlockSpec((tm, 1), lambda i: (i, 0)),
                  pl.BlockSpec((1, LANES), lambda i: (0, 0)),
                  pl.BlockSpec((1, LANES), lambda i: (0, 0))],
        out_specs=[pl.BlockSpec((tm, LANES), lambda i: (i, 0))] * 2,
        out_shape=[jax.ShapeDtypeStruct((n, LANES), F32)] * 2,
        compiler_params=_cparams("parallel"),
        name="rope_tables",
    )(pos_col, freq, jnp.asarray(sign))


def _headnorm_rope(t, gain, c, s):
    lane = lax.broadcasted_iota(jnp.int32, t.shape, 1)
    lo = lane < HEAD_DIM
    sq = t * t
    s_lo = jnp.sum(jnp.where(lo, sq, 0.0), axis=-1, keepdims=True)
    s_hi = jnp.sum(jnp.where(lo, 0.0, sq), axis=-1, keepdims=True)
    ms = jnp.where(lo, s_lo, s_hi) * (1.0 / HEAD_DIM)
    tn = (t * lax.rsqrt(ms + RMS_EPS)) * gain
    half = ROPE_DIMS // 2
    partner = jnp.where((lane & (HEAD_DIM - 1)) < half,
                        pltpu.roll(tn, LANES - half, 1), pltpu.roll(tn, half, 1))
    return tn * c + partner * s


def _blend_rows(x1, w, g0, g1):
    half = x1.shape[1] // 2
    a0, b0 = _unpack_rows(g0)
    a1, b1 = _unpack_rows(g1)
    w0, w1 = w[:, 0:1], w[:, 1:2]
    return jnp.concatenate([x1[:, 0:half] + w0 * a0 + w1 * a1, x1[:, half:] + w0 * b0 + w1 * b1], axis=1)


def _in_proj_kernel(*refs, blend):
    if blend:
        x1_ref, wc_ref, g0_ref, g1_ref, *refs = refs
    else:
        x_in_ref, *refs = refs
    gain_ref, w_ref, c_ref, s_ref, qg_ref, kg_ref, *refs = refs
    q_ref, ks_ref, kw_ref, kc_ref, vc_ref, vs_ref, vw_ref, u_ref, g_ref, *refs = refs
    if blend:
        x_out_ref, raw_ref = refs
        x = _blend_rows(x1_ref[...], wc_ref[...].T, g0_ref[...], g1_ref[...])
        x_out_ref[...] = x
    else:
        (raw_ref,) = refs
        x = x_in_ref[...]
    ms = jnp.mean(x * x, axis=-1, keepdims=True)
    h = (x * lax.rsqrt(ms + RMS_EPS)) * gain_ref[...]
    z = _dot(h.astype(BF16), w_ref[...])
    c = c_ref[...]
    s = s_ref[...]
    lane = lax.broadcasted_iota(jnp.int32, c.shape, 1)
    lo = lane < HEAD_DIM
    scale = HEAD_DIM ** -0.5 * math.log2(math.e)
    for j in range(NSA_HEADS // 2):
        r = _headnorm_rope(z[:, C_Q + LANES * j:C_Q + LANES * (j + 1)], qg_ref[...], c, s) * scale
        rr = pltpu.roll(r, HEAD_DIM, 1)
        if j // (NSA_GROUP // 2) == 0:
            even, odd = jnp.where(lo, r, 0.0), jnp.where(lo, rr, 0.0)
        else:
            even, odd = jnp.where(lo, 0.0, rr), jnp.where(lo, 0.0, r)
        q_ref[:, 2 * LANES * j:2 * LANES * j + LANES] = even.astype(BF16)
        q_ref[:, 2 * LANES * j + LANES:2 * LANES * (j + 1)] = odd.astype(BF16)
    ks_ref[...] = _headnorm_rope(z[:, C_KS:C_KS + LANES], kg_ref[...], c, s).astype(BF16)
    kw_ref[...] = _headnorm_rope(z[:, C_KW:C_KW + LANES], kg_ref[...], c, s).astype(BF16)
    half = CMP_BLOCK // 2
    for col, out in ((C_KC, kc_ref), (C_VC, vc_ref)):
        raw_ref[...] = z[:, col:col + LANES]
        for l in range(half):
            piece = raw_ref[pl.ds(l, x.shape[0] // half, stride=half), :]
            out[:, LANES * l:LANES * (l + 1)] = piece.astype(BF16)
    vs_ref[...] = z[:, C_VS:C_VS + LANES].astype(BF16)
    vw_ref[...] = z[:, C_VW:C_VW + LANES].astype(BF16)
    u_ref[:, pl.ds(pl.program_id(1), 1), :] = z[:, C_U:C_U + S5_WIDTH][:, None, :]
    g_ref[...] = _sigmoid(z[:, C_G:C_G + LANES])


def _in_proj(x_in, gain, w1, ctab, stab, qg, kg, batch, seq):
    blend = isinstance(x_in, tuple)
    n = batch * seq
    tm = IN_TM
    nt = seq // tm
    row = lambda i, b: (b * nt + i, 0)
    const = lambda i, b: (0, 0)
    tok = lambda w: pl.BlockSpec((tm, w), row)
    half = CMP_BLOCK // 2
    grouped = pl.BlockSpec((tm // half, half * KV_WIDTH), row)
    rows_bf16 = lambda r, w: jax.ShapeDtypeStruct((r, w), BF16)
    if blend:
        x1, w_col, gathered = x_in
        width = gathered.shape[1]
        x_specs = [tok(D_MODEL), pl.BlockSpec((SUBLANES, tm), lambda i, b: (0, b * nt + i)), tok(width),
                   pl.BlockSpec((tm, width), lambda i, b: (n // tm + b * nt + i, 0))]
        x_args = (x1, w_col, gathered, gathered)
    else:
        x_specs, x_args = [tok(D_MODEL)], (x_in,)
    return pl.pallas_call(
        functools.partial(_in_proj_kernel, blend=blend),
        grid=(nt, batch),
        in_specs=x_specs + [pl.BlockSpec((1, D_MODEL), const), pl.BlockSpec((D_MODEL, IN_W), const),
                            tok(LANES), tok(LANES), pl.BlockSpec((1, LANES), const),
                            pl.BlockSpec((1, LANES), const)],
        out_specs=[tok(2 * NSA_WIDTH), tok(LANES), tok(LANES), grouped, grouped, tok(LANES), tok(LANES),
                   pl.BlockSpec((tm, batch, S5_WIDTH), lambda i, b: (i, 0, 0)), tok(LANES)]
        + ([tok(D_MODEL)] if blend else []),
        out_shape=[rows_bf16(n, 2 * NSA_WIDTH), rows_bf16(n, LANES), rows_bf16(n, LANES),
                   rows_bf16(n // half, half * KV_WIDTH), rows_bf16(n // half, half * KV_WIDTH),
                   rows_bf16(n, LANES), rows_bf16(n, LANES),
                   jax.ShapeDtypeStruct((seq, batch, S5_WIDTH), F32), jax.ShapeDtypeStruct((n, LANES), F32)]
        + ([jax.ShapeDtypeStruct((n, D_MODEL), F32)] if blend else []),
        scratch_shapes=[pltpu.VMEM((tm, LANES), F32)],
        compiler_params=_cparams("parallel", "arbitrary"),
        name="in_proj",
    )(*x_args, gain, w1, ctab, stab, qg, kg)


def _compress_mlp(h_ref, w_ref, p_ref, w2_ref):
    w = w_ref[...]
    pre = _dot(h_ref[...], w)
    phi, plo = _split_bf16(p_ref[...])
    pb = _dot(phi, w) + _dot(plo, w)
    rows = pre.shape[0]
    out = None
    for h in range(NSA_KV_HEADS):
        top = pre[:, LANES * h:LANES * (h + 1)]
        bot = pre[:, LANES * (2 + h):LANES * (3 + h)]
        bias = pb[0:1, LANES * h:LANES * (h + 1)] + pb[1:2, LANES * (2 + h):LANES * (3 + h)]
        a = _gelu(top + pltpu.roll(bot, rows - 1, 0) + bias)
        o = _dot(a.astype(BF16), w2_ref[h])
        out = o if out is None else out + o
    return out


def _compress_kernel(hk_ref, hv_ref, wk_ref, wv_ref, pk_ref, pv_ref, w2k_ref, w2v_ref,
                     kg_ref, c_ref, s_ref, kc_ref, vc_ref):
    k = _compress_mlp(hk_ref, wk_ref, pk_ref, w2k_ref)
    kc_ref[...] = _headnorm_rope(k, kg_ref[...], c_ref[...], s_ref[...]).astype(BF16)
    vc_ref[...] = _compress_mlp(hv_ref, wv_ref, pv_ref, w2v_ref).astype(BF16)


def _compress(hk, hv, wk, wv, pk, pv, w2k, w2v, kg, ccmp, scmp, batch):
    rows = hk.shape[0] // batch
    width = hk.shape[1]
    const2 = lambda b: (0, 0)
    const3 = lambda b: (0, 0, 0)
    rowb = lambda w: pl.BlockSpec((rows, w), lambda b: (b, 0))
    return pl.pallas_call(
        _compress_kernel,
        grid=(batch,),
        in_specs=[rowb(width), rowb(width),
                  pl.BlockSpec(wk.shape, const2), pl.BlockSpec(wv.shape, const2),
                  pl.BlockSpec(pk.shape, const2), pl.BlockSpec(pv.shape, const2),
                  pl.BlockSpec(w2k.shape, const3), pl.BlockSpec(w2v.shape, const3),
                  pl.BlockSpec((1, LANES), const2), rowb(LANES), rowb(LANES)],
        out_specs=[rowb(LANES), rowb(LANES)],
        out_shape=[jax.ShapeDtypeStruct((hk.shape[0], LANES), BF16)] * 2,
        compiler_params=_cparams("parallel"),
        name="compress_kv",
    )(hk, hv, wk, wv, pk, pv, w2k, w2v, kg, ccmp, scmp)


def _compress_weights(pos_emb, w1, w2):
    half = CMP_BLOCK // 2
    eye = jnp.eye(NSA_KV_HEADS, dtype=F32)
    w1r = w1.reshape(2, half, HEAD_DIM, CMP_HIDDEN).astype(BF16)
    zero = jnp.zeros_like(w1r[0])
    wexp = jnp.concatenate(
        [jnp.stack([w1r[s] if p == h else zero for p in range(NSA_KV_HEADS)], axis=1)
         .reshape(half * KV_WIDTH, CMP_HIDDEN) for s in range(2) for h in range(NSA_KV_HEADS)], axis=1)
    pexp = jnp.broadcast_to(pos_emb.reshape(2, half, 1, HEAD_DIM), (2, half, NSA_KV_HEADS, HEAD_DIM))
    pexp = jnp.pad(pexp.reshape(2, half * KV_WIDTH), ((0, SUBLANES - 2), (0, 0)))
    w2e = jnp.einsum("cd,ph->pchd", w2, eye).reshape(NSA_KV_HEADS, CMP_HIDDEN, KV_WIDTH)
    return wexp, pexp, w2e.astype(BF16)


def _attn_kernel(q_ref, ks_ref, vs_ref, kw_ref, vw_ref, kc_ref, vc_ref, g_ref, et_ref, ovt_ref, oh_ref, wb_ref,
                 cb_ref, o_ref, qc_ref, s_ref, p_ref, sw_ref, pw_ref, m_ref, al_ref, oc_ref, as_ref, aw_ref,
                 *, n_cmp, n_slc, top_k):
    tq, tk, wk, rb, nh = ATT_TQ, ATT_TK, ATT_WK, ATT_RB, NSA_HEADS
    rows, grp_rows = nh * tq, NSA_GROUP * tq
    qk_rows = 4 * tq
    q0 = pl.program_id(1) * tq
    for a in range(nh):
        qc_ref[a * tq:(a + 1) * tq, 0:LANES] = q_ref[:, LANES * a:LANES * (a + 1)]
    q8 = qc_ref[:, 0:LANES]
    lane = lax.broadcasted_iota(jnp.int32, (tq, LANES), 1)
    t_row = q0 + lax.broadcasted_iota(jnp.int32, (tq, LANES), 0)

    s_c = _dot_t(q8, kc_ref[...]).reshape(nh, tq, LANES)
    mask_c = ((lane * CMP_STRIDE + (CMP_BLOCK - 1)) <= t_row) & (lane < n_cmp)
    sm = jnp.where(mask_c[None], s_c, NEG_BIG)
    p = jnp.where(mask_c[None], jnp.exp2(sm - jnp.max(sm, axis=-1, keepdims=True)), 0.0)
    l = jnp.sum(p, axis=-1, keepdims=True)
    p_c = p * jnp.where(l > 0.0, 1.0 / l, 0.0)
    oc_ref[...] = _dot(p_c.reshape(rows, LANES).astype(BF16), vc_ref[...])

    blk = lax.broadcasted_iota(jnp.int32, (n_slc, tq), 0)
    cur = (q0 + lax.broadcasted_iota(jnp.int32, (n_slc, tq), 1)) >> int(math.log2(SLC_BLOCK))
    forced = (blk == 0) | (blk == cur) | (blk == cur - 1)
    for h in range(NSA_KV_HEADS):
        ps = p_c[NSA_GROUP * h]
        for g in range(1, NSA_GROUP):
            ps = ps + p_c[NSA_GROUP * h + g]
        hi, lo = _split_bf16(ps)
        imp = (_dot_t(ovt_ref[...], hi) + _dot_t(ovt_ref[...], lo))[0:n_slc]
        imp = jnp.where(forced, jnp.inf, jnp.where(blk > cur, -jnp.inf, imp))
        rank = jnp.zeros((n_slc, tq), F32)
        for i in range(n_slc):
            row = imp[i:i + 1, :]
            rank = rank + jnp.where((row > imp) | ((row == imp) & (blk > i)), 1.0, 0.0)
        bias_t = jnp.where(rank < float(top_k), 0.0, NEG_BIG)
        bias = jnp.concatenate([bias_t, jnp.zeros((LANES - n_slc, tq), F32)], axis=0).T.astype(BF16)
        for g in range(NSA_GROUP):
            a = NSA_GROUP * h + g
            qc_ref[a * tq:(a + 1) * tq, LANES:2 * LANES] = bias

    def softmax_pass(s_buf, p_buf, width, causal_case, online):
        for r in range(rows // rb):
            r0 = r * rb
            s = s_buf[pl.ds(r0, rb), 0:width]
            if causal_case is not None:
                s = s + cb_ref[causal_case, pl.ds(r0 % tq, rb), 0:width]
            m_new = jnp.broadcast_to(jnp.max(s, axis=1, keepdims=True), (rb, LANES))
            if online:
                m_old = m_ref[pl.ds(r0, rb), :]
                m_new = jnp.maximum(m_old, m_new)
                al_ref[pl.ds(r0, rb), :] = jnp.exp2(m_old - m_new)
                m_ref[pl.ds(r0, rb), :] = m_new
            p_buf[pl.ds(r0, rb), 0:width] = jnp.exp2(
                (s - jnp.concatenate([m_new] * (width // LANES), axis=1)).astype(BF16))

    def pv(acc_ref, p_buf, v, width, online):
        lane_v = lax.broadcasted_iota(jnp.int32, v.shape, 1)
        for h in range(NSA_KV_HEADS):
            rs = slice(h * grp_rows, (h + 1) * grp_rows)
            own = (lane_v < HEAD_DIM) if h == 0 else (lane_v >= HEAD_DIM)
            upd = _dot(p_buf[rs, 0:width], jnp.where(own, v, jnp.ones_like(v)))
            acc_ref[rs, :] = (al_ref[rs, :] * acc_ref[rs, :] + upd) if online else upd

    m_ref[...] = jnp.full(m_ref.shape, 0.1 * NEG_BIG, F32)
    as_ref[...] = jnp.zeros(as_ref.shape, F32)

    def slc_chunk(c, diagonal, width=tk):
        k0 = c * tk
        s_buf, p_buf = s_ref.at[c % 2], p_ref.at[c % 2]
        kcat = jnp.concatenate([ks_ref[pl.ds(k0, width), :], et_ref[pl.ds(k0, width), :]], axis=1)
        for g in range(rows // qk_rows):
            rs = slice(g * qk_rows, (g + 1) * qk_rows)
            s_buf[rs, 0:width] = _dot_t(qc_ref[rs, :], kcat)
        softmax_pass(s_buf, p_buf, width, ((q0 - k0) >> int(math.log2(tq))) if diagonal else None, True)
        pv(as_ref, p_buf, vs_ref[pl.ds(k0, width), :], width, True)

    def window_and_output():
        w0 = pl.multiple_of(jnp.maximum(q0 - WINDOW, 0), LANES)
        w_case = jnp.minimum(q0 >> int(math.log2(tq)), WINDOW // tq)
        kcat_w = jnp.concatenate([kw_ref[pl.ds(w0, wk), :], wb_ref[w_case]], axis=1)
        for g in range(rows // qk_rows):
            rs = slice(g * qk_rows, (g + 1) * qk_rows)
            sw_ref[rs, :] = _dot_t(jnp.concatenate([qc_ref[rs, 0:LANES], oh_ref[rs, :]], axis=1), kcat_w)
        softmax_pass(sw_ref, pw_ref, wk, None, False)
        pv(aw_ref, pw_ref, vw_ref[pl.ds(w0, wk), :], wk, False)

        lo_half = lane < HEAD_DIM
        for j in range(nh // 2):
            comb = []
            for a in (2 * j, 2 * j + 1):
                rs = slice(a * tq, (a + 1) * tq)
                o_s, o_w = as_ref[rs, :], aw_ref[rs, :]
                comb.append(g_ref[:, 3 * a:3 * a + 1] * oc_ref[rs, :]
                            + g_ref[:, 3 * a + 1:3 * a + 2] * (o_s * (1.0 / pltpu.roll(o_s, HEAD_DIM, 1)))
                            + g_ref[:, 3 * a + 2:3 * a + 3] * (o_w * (1.0 / pltpu.roll(o_w, HEAD_DIM, 1))))
            even, odd = comb
            if j // (NSA_GROUP // 2) == 0:
                pair = jnp.where(lo_half, even, pltpu.roll(odd, HEAD_DIM, 1))
            else:
                pair = jnp.where(lo_half, pltpu.roll(even, HEAD_DIM, 1), odd)
            o_ref[:, LANES * j:LANES * (j + 1)] = pair.astype(BF16)

    c_last = q0 >> int(math.log2(tk))
    early = ((q0 >> int(math.log2(tq))) & (tk // tq - 1)) < tk // tq // 2
    for n_full in range(ks_ref.shape[0] // tk):
        for first_half in (True, False):
            @pl.when((c_last == n_full) & (early if first_half else jnp.logical_not(early)))
            def _():
                for c in range(n_full):
                    slc_chunk(c, False)
                slc_chunk(n_full, True, tk // 2 if first_half else tk)
                window_and_output()


def _pack_w_in(w):
    sizes = (NSA_WIDTH + 6 * KV_WIDTH, 3 * NSA_HEADS, S5_WIDTH, D_MODEL, D_MODEL)
    o = np.cumsum((0,) + sizes)
    qkv, g, u, ga, gs = [w[:, o[i]:o[i + 1]] for i in range(len(sizes))]
    g = jnp.pad(g, ((0, 0), (0, LANES - 3 * NSA_HEADS)))
    w1 = jnp.concatenate([qkv, u, g], axis=1).astype(BF16)
    return w1, ga.astype(BF16), gs.astype(BF16)


def _nsa_branch(x_in, gain, w1, ctab, stab, ccmp, scmp, q_gain, k_gain, cmp_k, cmp_v, batch, seq):
    qg = jnp.tile(q_gain, 2)[None, :]
    kg = jnp.tile(k_gain, 2)[None, :]
    q, ks, kw, hk, hv, vs, vw, u_tm, gates, *x_out = _in_proj(x_in, gain, w1, ctab, stab, qg, kg, batch, seq)
    wk, pk, w2k = _compress_weights(*cmp_k)
    wv, pv, w2v = _compress_weights(*cmp_v)
    kc, vc = _compress(hk, hv, wk, wv, pk, pv, w2k, w2v, kg, ccmp, scmp, batch)
    o_a = _attention(q, ks, vs, kw, vw, kc, vc, gates, batch, seq)
    return o_a, u_tm, (x_out[0] if x_out else x_in)


def _cmp_rope_tables(ctab, stab, batch, seq):
    def pick(t):
        t = t.reshape(batch, seq, LANES)[:, CMP_BLOCK - 1::CMP_STRIDE]
        t = jnp.pad(t, ((0, 0), (0, seq // CMP_STRIDE - t.shape[1]), (0, 0)))
        return t.reshape(batch * (seq // CMP_STRIDE), LANES)
    return pick(ctab), pick(stab)


def _s5_disc_kernel(are_ref, aim_ref, ldt_ref, lre_ref, lim_ref, fre_ref, fim_ref):
    lam_re = jnp.minimum(are_ref[...], -1e-4)
    lam_im = aim_ref[...]
    dt = jnp.exp(ldt_ref[...])
    mag = jnp.exp(lam_re * dt)
    ang = lam_im * dt
    lb_re = mag * jnp.cos(ang)
    lb_im = mag * jnp.sin(ang)
    den = lam_re * lam_re + lam_im * lam_im
    lre_ref[...] = lb_re
    lim_ref[...] = lb_im
    fre_ref[...] = ((lb_re - 1.0) * lam_re + lb_im * lam_im) / den
    fim_ref[...] = (lb_im * lam_re - (lb_re - 1.0) * lam_im) / den


def _s5_weights(a_re, a_im, log_dt, b_re, b_im, c_re, c_im):
    shp = jax.ShapeDtypeStruct((S5_GROUPS, S5_STATE), F32)
    lb_re, lb_im, f_re, f_im = pl.pallas_call(
        _s5_disc_kernel, out_shape=[shp] * 4, name="s5_discretise",
    )(a_re, a_im, log_dt[:, None])
    bb_re = f_re[..., None] * b_re - f_im[..., None] * b_im
    bb_im = f_re[..., None] * b_im + f_im[..., None] * b_re
    eye = jnp.eye(S5_GROUPS, dtype=F32)

    def in_map(bb):
        return jnp.einsum("gpc,gh->gchp", bb, eye).reshape(S5_WIDTH, S5_NSTATE)

    def out_map(cc):
        return jnp.einsum("gcp,gh->hpgc", cc, eye).reshape(S5_NSTATE, S5_WIDTH)

    w_b = jnp.concatenate([in_map(bb_re), in_map(bb_im)], axis=1).astype(BF16)
    w_c = jnp.concatenate([out_map(c_re), -out_map(c_im)], axis=0).astype(BF16)
    return lb_re.reshape(1, S5_NSTATE), lb_im.reshape(1, S5_NSTATE), w_b, w_c


def _s5_kernel(u_ref, wb_ref, lre_ref, lim_ref, wc_ref, d_ref, wg_ref, o_ref, x_ref, st_ref, *, batch):
    ns = S5_NSTATE

    @pl.when(pl.program_id(0) == 0)
    def _():
        st_ref[...] = jnp.zeros(st_ref.shape, F32)

    u = u_ref[...].reshape(u_ref.shape[0] * batch, S5_WIDTH)
    x_ref[...] = _dot(u.astype(BF16), wb_ref[...])
    lre = jnp.broadcast_to(lre_ref[...], (batch, ns))
    lim = jnp.broadcast_to(lim_ref[...], (batch, ns))

    def step(t, carry):
        xr, xi = carry
        r0 = pl.multiple_of(t * batch, batch)
        nr = lre * xr - lim * xi + x_ref[pl.ds(r0, batch), 0:ns]
        ni = lre * xi + lim * xr + x_ref[pl.ds(r0, batch), ns:2 * ns]
        x_ref[pl.ds(r0, batch), 0:ns] = nr
        x_ref[pl.ds(r0, batch), ns:2 * ns] = ni
        return nr, ni

    steps = u.shape[0] // batch
    xr, xi = lax.fori_loop(0, steps, step, (st_ref[:, 0:ns], st_ref[:, ns:2 * ns]), unroll=4)
    st_ref[:, 0:ns] = xr
    st_ref[:, ns:2 * ns] = xi
    y = _dot(x_ref[...].astype(BF16), wc_ref[...]) + d_ref[...] * u
    z = _dot(_gelu(y).astype(BF16), wg_ref[...])
    o = z[:, :S5_WIDTH] * _sigmoid(z[:, S5_WIDTH:])
    for c in range(S5_WIDTH // LANES):
        o_ref[c] = o[:, c * LANES:(c + 1) * LANES]


def _s5(u_tm, lre, lim, w_b, w_c, d_skip, w_glu, batch, seq):
    rows = S5_T * batch
    const = lambda t: (0, 0)
    return pl.pallas_call(
        functools.partial(_s5_kernel, batch=batch),
        grid=(seq // S5_T,),
        in_specs=[pl.BlockSpec((S5_T, batch, S5_WIDTH), lambda t: (t, 0, 0)),
                  pl.BlockSpec(w_b.shape, const), pl.BlockSpec(lre.shape, const), pl.BlockSpec(lim.shape, const),
                  pl.BlockSpec(w_c.shape, const), pl.BlockSpec((1, S5_WIDTH), const),
                  pl.BlockSpec(w_glu.shape, const)],
        out_specs=pl.BlockSpec((S5_WIDTH // LANES, rows, LANES), lambda t: (0, t, 0)),
        out_shape=jax.ShapeDtypeStruct((S5_WIDTH // LANES, seq * batch, LANES), F32),
        scratch_shapes=[pltpu.VMEM((rows, 2 * S5_NSTATE), F32), pltpu.VMEM((batch, 2 * S5_NSTATE), F32)],
        compiler_params=_cparams("arbitrary"),
        name="s5_scan",
    )(u_tm, w_b, lre, lim, w_c, d_skip, w_glu)


RT_GROUP_ROW = 0
RT_EXPERT_ROW = SUBLANES


def _rmsnorm(x, gain):
    return (x * lax.rsqrt(jnp.mean(x * x, axis=-1, keepdims=True) + RMS_EPS)) * gain


def _merge_kernel(x_ref, g1_ref, wga_ref, wgs_ref, oa_ref, ob_ref, pa_ref, ps_ref, wo_ref, g2_ref,
                  wrh_ref, wrl_ref, rb_ref, tri_ref, x1_ref, h2_ref, e_ref, w_ref, r_ref, cnt_ref,
                  lg_ref, carry_ref):
    x = x_ref[...]
    h = _rmsnorm(x, g1_ref[...]).astype(BF16)
    gate_a = _sigmoid(_dot(h, wga_ref[...]))
    gate_s = _sigmoid(_dot(h, wgs_ref[...]))
    rows_b = pl.ds(pl.program_id(1), x.shape[0], stride=ob_ref.shape[1] // x.shape[0])
    o_b = jnp.concatenate([ob_ref[c, rows_b, :] for c in range(S5_WIDTH // LANES)], axis=1)
    merged = gate_a * _dot(oa_ref[...], pa_ref[...]) + gate_s * _dot(o_b.astype(BF16), ps_ref[...])
    x1 = x + _dot(merged.astype(BF16), wo_ref[...])
    x1_ref[...] = x1
    h2 = _rmsnorm(x1, g2_ref[...])
    h2_ref[...] = _pack_rows(h2)
    hi, lo = _split_bf16(h2)
    lg_ref[...] = (_dot_t(wrh_ref[...], hi) + _dot_t(wrh_ref[...], lo) + _dot_t(wrl_ref[...], hi)
                   + rb_ref[...])
    _route_tile(lg_ref, tri_ref, e_ref, w_ref, r_ref, cnt_ref, carry_ref,
                (pl.program_id(0) == 0) & (pl.program_id(1) == 0))


def _router_weights(w_rg, b_rg, w_re, b_re):
    def rows(group_part, expert_part):
        return jnp.concatenate([
            group_part, jnp.zeros((RT_EXPERT_ROW - N_EXPERT_GROUPS,) + group_part.shape[1:], F32),
            expert_part, jnp.zeros((LANES - RT_EXPERT_ROW - N_EXPERTS,) + group_part.shape[1:], F32)], axis=0)

    hi, lo = _split_bf16(rows(w_rg.T, w_re.T))
    return hi, lo, rows(b_rg[:, None], b_re[:, None])


def _merge(x2, g1, w_ga, w_gs, o_a, ob_tm, p_a, p_s, w_o, g2, wrh, wrl, rb, batch, seq):
    n = x2.shape[0]
    tm = MRG_TM
    nt = seq // tm
    row = lambda i, b: (b * nt + i, 0)
    const = lambda i, b: (0, 0)
    tok = lambda w: pl.BlockSpec((tm, w), row)
    full = lambda a: pl.BlockSpec(a.shape, const)
    tri = jnp.asarray(np.triu(np.ones((tm, tm), np.float32), 1), BF16)
    lanes_tok = pl.BlockSpec((SUBLANES, tm), lambda i, b: (0, b * nt + i))
    return pl.pallas_call(
        _merge_kernel,
        grid=(nt, batch),
        in_specs=[tok(D_MODEL), full(g1), full(w_ga), full(w_gs), tok(NSA_WIDTH),
                  pl.BlockSpec((S5_WIDTH // LANES, tm * batch, LANES), lambda i, b: (0, i, 0)),
                  full(p_a), full(p_s), full(w_o),
                  full(g2), full(wrh), full(wrl), full(rb), full(tri)],
        out_specs=[tok(D_MODEL), tok(D_MODEL // 2), lanes_tok, lanes_tok, lanes_tok,
                   pl.BlockSpec((N_EXPERTS, LANES), const)],
        out_shape=[jax.ShapeDtypeStruct((n, D_MODEL), F32), jax.ShapeDtypeStruct((n, D_MODEL // 2), jnp.int32),
                   jax.ShapeDtypeStruct((SUBLANES, n), jnp.int32), jax.ShapeDtypeStruct((SUBLANES, n), F32),
                   jax.ShapeDtypeStruct((SUBLANES, n), jnp.int32), jax.ShapeDtypeStruct((N_EXPERTS, LANES), F32)],
        scratch_shapes=[pltpu.VMEM((LANES, tm), F32), pltpu.VMEM((N_EXPERTS, LANES), F32)],
        compiler_params=_cparams("arbitrary", "arbitrary"),
        name="merge_out_proj",
    )(x2, g1, w_ga, w_gs, o_a, ob_tm, p_a, p_s, w_o, g2, wrh, wrl, rb, tri)


def _first_argmax(v, rows):
    vmax = jnp.max(v, axis=0, keepdims=True)
    idx = jnp.min(jnp.where(v == vmax, rows, SUBLANES), axis=0, keepdims=True)
    return vmax, idx


def _route_tile(lg_ref, tri_ref, e_ref, w_ref, r_ref, cnt_ref, carry_ref, first):
    tt = lg_ref.shape[1]

    @pl.when(first)
    def _():
        carry_ref[...] = jnp.zeros(carry_ref.shape, F32)

    rows = lax.broadcasted_iota(jnp.int32, (SUBLANES, tt), 0)
    gl = jnp.where(rows < N_EXPERT_GROUPS, lg_ref[RT_GROUP_ROW:RT_GROUP_ROW + SUBLANES, :], -jnp.inf)
    gmax, g_sel = _first_argmax(gl, rows)
    g_w = 1.0 / jnp.sum(jnp.exp(gl - gmax), axis=0, keepdims=True)

    el = jnp.zeros((EXPERTS_PER_GROUP, tt), F32)
    for g in range(N_EXPERT_GROUPS):
        r0 = RT_EXPERT_ROW + EXPERTS_PER_GROUP * g
        el = jnp.where(g_sel == g, lg_ref[r0:r0 + EXPERTS_PER_GROUP, :], el)
    ex = jnp.exp(el - jnp.max(el, axis=0, keepdims=True))
    p = ex / jnp.sum(ex, axis=0, keepdims=True)
    p1, i1 = _first_argmax(p, rows)
    p2, i2 = _first_argmax(jnp.where(rows == i1, -1.0, p), rows)
    e1 = g_sel * EXPERTS_PER_GROUP + i1
    e2 = g_sel * EXPERTS_PER_GROUP + i2
    scale = g_w / (p1 + p2)

    erow = lax.broadcasted_iota(jnp.int32, (N_EXPERTS, tt), 0)
    oh1 = jnp.where(erow == e1, 1.0, 0.0)
    oh2 = jnp.where(erow == e2, 1.0, 0.0)
    both = oh1 + oh2
    before = carry_ref[:, 0:1] + _dot(both.astype(BF16), tri_ref[...])
    carry_ref[...] = carry_ref[...] + jnp.sum(both, axis=1, keepdims=True)
    cnt_ref[...] = carry_ref[...]

    zi = jnp.zeros((SUBLANES - 2, tt), jnp.int32)
    e_ref[...] = jnp.concatenate([e1, e2, zi], axis=0)
    w_ref[...] = jnp.concatenate([p1 * scale, p2 * scale, zi.astype(F32)], axis=0)
    r_ref[...] = jnp.concatenate([jnp.sum(oh1 * before, axis=0, keepdims=True),
                                  jnp.sum(oh2 * before, axis=0, keepdims=True),
                                  zi.astype(F32)], axis=0).astype(jnp.int32)


def _expert_kernel(be_ref, nb_ref, x_ref, wg_ref, wu_ref, wd_ref, o_ref, wgb_ref, wub_ref, wdb_ref):
    i = pl.program_id(0)
    half = D_MODEL // 2

    def mlp(wg, wu, wd):
        xa, xb = _unpack_rows(x_ref[...])
        xa, xb = xa.astype(BF16), xb.astype(BF16)
        g = _dot(xa, wg[0:half, :]) + _dot(xb, wg[half:, :])
        u = _dot(xa, wu[0:half, :]) + _dot(xb, wu[half:, :])
        o_ref[...] = _pack_rows(_dot(((g * _sigmoid(g)) * u).astype(BF16), wd))

    new_expert = (i == 0) | (be_ref[i] != be_ref[jnp.maximum(i - 1, 0)])

    @pl.when((i < nb_ref[0]) & new_expert)
    def _():
        wg, wu, wd = wg_ref[...].astype(BF16), wu_ref[...].astype(BF16), wd_ref[...].astype(BF16)
        wgb_ref[...] = wg
        wub_ref[...] = wu
        wdb_ref[...] = wd
        mlp(wg, wu, wd)

    @pl.when((i < nb_ref[0]) & jnp.logical_not(new_expert))
    def _():
        mlp(wgb_ref[...], wub_ref[...], wdb_ref[...])

    @pl.when(i >= nb_ref[0])
    def _():
        o_ref[...] = jnp.zeros(o_ref.shape, o_ref.dtype)


def _experts(blk_expert, n_used, buf, w_gate, w_up, w_down, layer):
    cap, width = buf.shape
    blk = MOE_BLK
    row = lambda i, be, nb: (jnp.minimum(i, nb[0] - 1), 0)
    wmap = lambda i, be, nb: (layer, be[jnp.minimum(i, nb[0] - 1)], 0, 0)
    return pl.pallas_call(
        _expert_kernel,
        grid_spec=pltpu.PrefetchScalarGridSpec(
            num_scalar_prefetch=2,
            grid=(cap // blk,),
            in_specs=[pl.BlockSpec((blk, width), row),
                      pl.BlockSpec((None, None, D_MODEL, EXPERT_FF), wmap),
                      pl.BlockSpec((None, None, D_MODEL, EXPERT_FF), wmap),
                      pl.BlockSpec((None, None, EXPERT_FF, D_MODEL), wmap)],
            out_specs=pl.BlockSpec((blk, width), lambda i, be, nb: (i, 0)),
            scratch_shapes=[pltpu.VMEM((D_MODEL, EXPERT_FF), BF16), pltpu.VMEM((D_MODEL, EXPERT_FF), BF16),
                            pltpu.VMEM((EXPERT_FF, D_MODEL), BF16)],
        ),
        out_shape=jax.ShapeDtypeStruct((cap, width), buf.dtype),
        compiler_params=_cparams("arbitrary"),
        name="moe_experts",
    )(blk_expert, n_used, buf, w_gate, w_up, w_down)


SC_CH = 64


def _sc_mesh():
    return plsc.VectorSubcoreMesh(core_axis_name="c", subcore_axis_name="s")


def _sc_worker(info):
    return lax.axis_index("s") * info.num_cores + lax.axis_index("c")


def _sc_gather_rows(table, idx, rows=None):
    info = plsc.get_sparse_core_info()
    lanes, workers = info.num_lanes, info.num_cores * info.num_subcores
    invert = rows is not None
    rows = rows if invert else idx.shape[0]
    width = table.shape[1]
    per_w = rows // workers
    n_chunks = per_w // SC_CH
    assert per_w % (2 * SC_CH) == 0
    scan = 2048
    n_tok = table.shape[0]
    assert not invert or (n_tok & (n_tok - 1) == 0 and idx.shape[0] % scan == 0)

    def body(table_hbm, idx_hbm, out_hbm, idx_v, rows_a, rows_b, sem_a, sem_b, *scan_v):
        base = _sc_worker(info) * per_w
        if invert:
            lane = lax.iota(jnp.int32, lanes)

            @pl.loop(0, per_w // lanes)
            def _(j):
                idx_v[pl.ds(j * lanes, lanes)] = (base + j * lanes + lane) & (n_tok - 1)

            @pl.loop(0, idx.shape[0] // scan)
            def _(c):
                pltpu.sync_copy(idx_hbm.at[pl.ds(c * scan, scan)], scan_v[0])

                @plsc.parallel_loop(0, scan // lanes, unroll=8)
                def _(j):
                    rel = scan_v[0][pl.ds(j * lanes, lanes)] - base
                    mine = (rel >= 0) & (rel < per_w)
                    token = (c * scan + j * lanes + lane) & (n_tok - 1)
                    plsc.store_scatter(idx_v, [jnp.where(mine, rel, 0)], token, mask=mine)
        else:
            pltpu.sync_copy(idx_hbm.at[pl.ds(base, per_w)], idx_v)

        def gather(j, buf, sem):
            return pltpu.make_async_copy(table_hbm.at[idx_v.at[pl.ds(j * SC_CH, SC_CH)]], buf, sem)

        def write(j, buf):
            pltpu.sync_copy(buf, out_hbm.at[pl.ds(base + j * SC_CH, SC_CH)])

        gather(0, rows_a, sem_a).start()

        @pl.loop(0, n_chunks, step=2)
        def _(j):
            gather(j + 1, rows_b, sem_b).start()
            gather(j, rows_a, sem_a).wait()
            write(j, rows_a)

            @pl.when(j + 2 < n_chunks)
            def _():
                gather(j + 2, rows_a, sem_a).start()

            gather(j + 1, rows_b, sem_b).wait()
            write(j + 1, rows_b)

    chunk_buf = pltpu.VMEM((SC_CH, width), table.dtype)
    scratch = [pltpu.VMEM((per_w,), jnp.int32), chunk_buf, chunk_buf,
               pltpu.SemaphoreType.DMA, pltpu.SemaphoreType.DMA]
    return pl.kernel(
        body, out_type=jax.ShapeDtypeStruct((rows, width), table.dtype), mesh=_sc_mesh(),
        scratch_types=scratch + ([pltpu.VMEM((scan,), jnp.int32)] if invert else []),
        compiler_params=pltpu.CompilerParams(needs_layout_passes=False) if invert else None,
        name="sc_dispatch_rows" if invert else "sc_gather_rows",
    )(table, idx)


def _blend_kernel(x1_ref, w_ref, g0_ref, g1_ref, o_ref):
    o_ref[...] = _blend_rows(x1_ref[...], w_ref[...].T, g0_ref[...], g1_ref[...])


def _blend(x1, w_col, gathered):
    n = x1.shape[0]
    tt = DSP_TT
    nt = n // tt
    width = gathered.shape[1]
    return pl.pallas_call(
        _blend_kernel,
        grid=(nt,),
        in_specs=[pl.BlockSpec((tt, D_MODEL), lambda i: (i, 0)),
                  pl.BlockSpec((SUBLANES, tt), lambda i: (0, i)),
                  pl.BlockSpec((tt, width), lambda i: (i, 0)),
                  pl.BlockSpec((tt, width), lambda i: (nt + i, 0))],
        out_specs=pl.BlockSpec((tt, D_MODEL), lambda i: (i, 0)),
        out_shape=jax.ShapeDtypeStruct((n, D_MODEL), F32),
        compiler_params=_cparams("parallel"),
        name="moe_blend",
    )(x1, w_col, gathered, gathered)


def _moe(x1, h2, routing, w_gate, w_up, w_down, layer):
    n = x1.shape[0]
    e_idx, wts, rank, counts = routing
    counts = counts[:, 0].astype(jnp.int32)
    padded = (counts + MOE_BLK - 1) // MOE_BLK * MOE_BLK
    pends = jnp.cumsum(padded)
    pstarts = pends - padded
    experts = jnp.arange(N_EXPERTS, dtype=jnp.int32)
    start_of = jnp.sum(jnp.where(e_idx[None, :EXPERT_TOP_K] == experts[:, None, None],
                                 pstarts[:, None, None], 0), axis=0)
    dest = start_of + rank[:EXPERT_TOP_K]
    cap = n * EXPERT_TOP_K + N_EXPERTS * MOE_BLK
    n_blk = cap // MOE_BLK
    blk_row = jnp.arange(n_blk, dtype=jnp.int32) * MOE_BLK
    blk_expert = jnp.minimum(jnp.sum((pends[None, :] <= blk_row[:, None]).astype(jnp.int32), axis=1),
                             N_EXPERTS - 1)
    n_used = (pends[-1:] // MOE_BLK).astype(jnp.int32)
    dest_flat = dest.reshape(EXPERT_TOP_K * n)
    buf = _sc_gather_rows(h2, dest_flat, rows=cap)
    out_buf = _experts(blk_expert, n_used, buf, w_gate, w_up, w_down, layer)
    return x1, wts, _sc_gather_rows(out_buf, dest_flat)


def kernel(x, positions, norm_mix, norm_ffn, w_in, q_gain, k_gain, cmp_k_pos, cmp_k_w1, cmp_k_w2, cmp_v_pos,
           cmp_v_w1, cmp_v_w2, s5_a_re, s5_a_im, s5_log_dt, s5_b_re, s5_b_im, s5_c_re, s5_c_im, s5_d, s5_glu,
           proj_nsa, proj_s5, w_out, router_group, router_group_bias, router_expert, router_expert_bias,
           expert_gate, expert_up, expert_down):
    batch, seq, _ = x.shape
    depth = w_in.shape[0]
    n = batch * seq
    inv_freq = ROPE_THETA ** (-jnp.arange(0, ROPE_DIMS, 2, dtype=jnp.float32) / ROPE_DIMS)
    ctab, stab = _rope_tables(positions.reshape(n, 1), inv_freq)
    ccmp, scmp = _cmp_rope_tables(ctab, stab, batch, seq)
    x_in = x.reshape(n, D_MODEL)
    for l in range(depth):
        w1, w_ga, w_gs = _pack_w_in(w_in[l])
        o_a, u_tm, x2 = _nsa_branch(x_in, norm_mix[l][None], w1, ctab, stab, ccmp, scmp, q_gain[l], k_gain[l],
                                    (cmp_k_pos[l], cmp_k_w1[l], cmp_k_w2[l]),
                                    (cmp_v_pos[l], cmp_v_w1[l], cmp_v_w2[l]), batch, seq)
        lre, lim, w_b, w_c = _s5_weights(s5_a_re[l], s5_a_im[l], s5_log_dt[l], s5_b_re[l], s5_b_im[l],
                                         s5_c_re[l], s5_c_im[l])
        ob_tm = _s5(u_tm, lre, lim, w_b, w_c, s5_d[l][None],
                    s5_glu[l].astype(BF16), batch, seq)
        wrh, wrl, rb = _router_weights(router_group[l], router_group_bias[l], router_expert[l],
                                       router_expert_bias[l])
        x1, h2, *routing = _merge(x2, norm_mix[l][None], w_ga, w_gs, o_a, ob_tm, proj_nsa[l].astype(BF16),
                                  proj_s5[l].astype(BF16), w_out[l].astype(BF16), norm_ffn[l][None],
                                  wrh, wrl, rb, batch, seq)
        x_in = _moe(x1, h2, routing, expert_gate, expert_up, expert_down, l)
    return _blend(*x_in).reshape(batch, seq, D_MODEL)


def _attn_tables(seq):
    n_cmp = (seq - CMP_BLOCK) // CMP_STRIDE + 1
    n_slc = seq // SLC_BLOCK
    key_blk = np.arange(seq) // SLC_BLOCK
    expand_t = (key_blk[:, None] == np.arange(LANES)[None, :]).astype(np.float32)
    cmp_start = np.arange(LANES) * CMP_STRIDE
    slc_start = np.arange(LANES) * SLC_BLOCK
    overlap_t = ((cmp_start[None, :] < slc_start[:, None] + SLC_BLOCK)
                 & (cmp_start[None, :] + CMP_BLOCK > slc_start[:, None])
                 & (np.arange(LANES)[None, :] < n_cmp) & (np.arange(LANES)[:, None] < n_slc))
    return (jnp.asarray(expand_t, BF16), jnp.asarray(overlap_t.astype(np.float32), BF16), n_cmp, n_slc)


def _attn_mask_tables():
    tq, tk, wk = ATT_TQ, ATT_TK, ATT_WK
    ql = np.arange(tq)
    onehot = np.tile(np.eye(tq, dtype=np.float32), (NSA_HEADS, 1))
    k = np.arange(wk)[:, None]
    band = []
    for c in range(WINDOW // tq + 1):
        if c < WINDOW // tq:
            keep = k <= c * tq + ql[None, :]
        else:
            keep = (k > ql[None, :]) & (k <= ql[None, :] + WINDOW)
        band.append(np.where(keep, 0.0, NEG_BIG))
    kl = np.arange(tk)[None, :]
    causal = [np.where(kl <= c * tq + ql[:, None], 0.0, NEG_BIG) for c in range(tk // tq)]
    return (jnp.asarray(onehot, BF16), jnp.asarray(np.stack(band), BF16),
            jnp.asarray(np.stack(causal), F32))


def _attention(q, ks, vs, kw, vw, kc, vc, gates, batch, seq):
    expand_t, overlap_t, n_cmp, n_slc = _attn_tables(seq)
    onehot, band, causal = _attn_mask_tables()
    tq = ATT_TQ
    nq = seq // tq
    rows = NSA_HEADS * tq
    tile = lambda w: pl.BlockSpec((tq, w), lambda b, i: (b * nq + i, 0))
    per_b = lambda r: pl.BlockSpec((r, LANES), lambda b, i: (b, 0))
    f32_rows = pltpu.VMEM((rows, LANES), F32)
    return pl.pallas_call(
        functools.partial(_attn_kernel, n_cmp=n_cmp, n_slc=n_slc, top_k=min(SLC_TOPK, n_slc)),
        grid=(batch, nq),
        in_specs=[tile(2 * NSA_WIDTH), per_b(seq), per_b(seq), per_b(seq), per_b(seq),
                  per_b(LANES), per_b(LANES), tile(LANES),
                  pl.BlockSpec(expand_t.shape, lambda b, i: (0, 0)),
                  pl.BlockSpec(overlap_t.shape, lambda b, i: (0, 0)),
                  pl.BlockSpec(onehot.shape, lambda b, i: (0, 0)),
                  pl.BlockSpec(band.shape, lambda b, i: (0, 0, 0)),
                  pl.BlockSpec(causal.shape, lambda b, i: (0, 0, 0))],
        out_specs=tile(NSA_WIDTH),
        out_shape=jax.ShapeDtypeStruct((batch * seq, NSA_WIDTH), BF16),
        scratch_shapes=[pltpu.VMEM((rows, 2 * LANES), BF16),
                        pltpu.VMEM((2, rows, ATT_TK), F32), pltpu.VMEM((2, rows, ATT_TK), BF16),
                        pltpu.VMEM((rows, ATT_WK), F32), pltpu.VMEM((rows, ATT_WK), BF16),
                        f32_rows, f32_rows, f32_rows, f32_rows, f32_rows],
        compiler_params=_cparams("parallel", "parallel"),
        name="nsa_attention",
    )(q, ks, vs, kw, vw, kc, vc, gates, expand_t, overlap_t, onehot, band, causal)
```

```python
import functools
import math

import jax
import jax.numpy as jnp
import numpy as np
from jax import lax
from jax.experimental import pallas as pl
from jax.experimental.pallas import tpu as pltpu
from jax.experimental.pallas import tpu_sc as plsc

F32 = jnp.float32
BF16 = jnp.bfloat16

D_MODEL = 1024
NSA_HEADS = 8
NSA_KV_HEADS = 2
NSA_GROUP = NSA_HEADS // NSA_KV_HEADS
HEAD_DIM = 64
ROPE_DIMS = HEAD_DIM // 4
ROPE_THETA = 500000.0
CMP_BLOCK = 32
CMP_STRIDE = 16
CMP_HIDDEN = 2 * HEAD_DIM
SLC_BLOCK = 64
SLC_TOPK = 16
WINDOW = 512
NSA_WIDTH = NSA_HEADS * HEAD_DIM
KV_WIDTH = NSA_KV_HEADS * HEAD_DIM
S5_WIDTH = 256
S5_GROUP_CH = 16
S5_GROUPS = S5_WIDTH // S5_GROUP_CH
S5_STATE = 64
S5_NSTATE = S5_GROUPS * S5_STATE
N_EXPERT_GROUPS = 4
EXPERTS_PER_GROUP = 8
N_EXPERTS = N_EXPERT_GROUPS * EXPERTS_PER_GROUP
EXPERT_TOP_K = 2
EXPERT_FF = 512
RMS_EPS = 1e-6
NEG_BIG = -1e30

LANES = 128
SUBLANES = 8
VMEM_LIMIT = 56 * 1024 * 1024

IN_TM = 512
ATT_TQ = 128
ATT_TK = 512
ATT_WK = WINDOW + ATT_TQ
ATT_RB = 128
S5_T = 256
MRG_TM = 512
MOE_BLK = 512
DSP_TT = 1024

C_Q, C_KC, C_VC, C_KS, C_VS, C_KW, C_VW, C_U, C_G = 0, 512, 640, 768, 896, 1024, 1152, 1280, 1536
IN_W = 1664


def _cparams(*sem):
    return pltpu.CompilerParams(dimension_semantics=sem, vmem_limit_bytes=VMEM_LIMIT)


def _gelu(x):
    return 0.5 * x * (1.0 + jnp.tanh(math.sqrt(2.0 / math.pi) * (x + 0.044715 * (x * x * x))))


def _sigmoid(x):
    return 1.0 / (1.0 + jnp.exp(-x))


def _dot(a, b):
    return jnp.dot(a, b, preferred_element_type=F32)


def _dot_t(a, b):
    return lax.dot_general(a, b, (((1,), (1,)), ((), ())), preferred_element_type=F32)


def _split_bf16(x):
    hi = x.astype(BF16)
    lo = (x - hi.astype(F32)).astype(BF16)
    return hi, lo


def _pack_rows(x):
    k = x.shape[1] // 2
    hi = lax.bitcast_convert_type(x[:, :k].astype(jnp.bfloat16).astype(F32), jnp.int32)
    lo = lax.bitcast_convert_type(x[:, k:].astype(jnp.bfloat16).astype(F32), jnp.int32)
    return hi | lax.shift_right_logical(lo, 16)


def _unpack_rows(w):
    a = lax.bitcast_convert_type(w & jnp.int32(-65536), F32)
    b = lax.bitcast_convert_type(lax.shift_left(w, 16), F32)
    return a, b


def _rope_kernel(pos_ref, freq_ref, sign_ref, c_ref, s_ref):
    ang = pos_ref[...].astype(F32) * freq_ref[...]
    c_ref[...] = jnp.cos(ang)
    s_ref[...] = jnp.sin(ang) * sign_ref[...]


def _rope_tables(pos_col, inv_freq):
    n = pos_col.shape[0]
    lane = np.arange(LANES) % HEAD_DIM
    sel = np.zeros((LANES, ROPE_DIMS // 2), np.float32)
    for l in range(LANES):
        if lane[l] < ROPE_DIMS:
            sel[l, lane[l] % (ROPE_DIMS // 2)] = 1.0
    freq = (jnp.asarray(sel) * inv_freq[None, :]).sum(-1)[None, :]
    sign = np.where(lane < ROPE_DIMS // 2, -1.0, 1.0).astype(np.float32)[None, :]
    tm = 1024
    return pl.pallas_call(
        _rope_kernel,
        grid=(n // tm,),
        in_specs=[pl.BlockSpec((tm, 1), lambda i: (i, 0)),
                  pl.BlockSpec((1, LANES), lambda i: (0, 0)),
                  pl.BlockSpec((1, LANES), lambda i: (0, 0))],
        out_specs=[pl.BlockSpec((tm, LANES), lambda i: (i, 0))] * 2,
        out_shape=[jax.ShapeDtypeStruct((n, LANES), F32)] * 2,
        compiler_params=_cparams("parallel"),
        name="rope_tables",
    )(pos_col, freq, jnp.asarray(sign))


def _headnorm_rope(t, gain, c, s):
    lane = lax.broadcasted_iota(jnp.int32, t.shape, 1)
    lo = lane < HEAD_DIM
    sq = t * t
    s_lo = jnp.sum(jnp.where(lo, sq, 0.0), axis=-1, keepdims=True)
    s_hi = jnp.sum(jnp.where(lo, 0.0, sq), axis=-1, keepdims=True)
    ms = jnp.where(lo, s_lo, s_hi) * (1.0 / HEAD_DIM)
    tn = (t * lax.rsqrt(ms + RMS_EPS)) * gain
    half = ROPE_DIMS // 2
    partner = jnp.where((lane & (HEAD_DIM - 1)) < half,
                        pltpu.roll(tn, LANES - half, 1), pltpu.roll(tn, half, 1))
    return tn * c + partner * s


def _blend_rows(x1, w, g0, g1):
    half = x1.shape[1] // 2
    a0, b0 = _unpack_rows(g0)
    a1, b1 = _unpack_rows(g1)
    w0, w1 = w[:, 0:1], w[:, 1:2]
    return jnp.concatenate([x1[:, 0:half] + w0 * a0 + w1 * a1, x1[:, half:] + w0 * b0 + w1 * b1], axis=1)


def _in_proj_kernel(*refs, blend):
    if blend:
        x1_ref, wc_ref, g0_ref, g1_ref, *refs = refs
    else:
        x_in_ref, *refs = refs
    gain_ref, w_ref, c_ref, s_ref, qg_ref, kg_ref, *refs = refs
    q_ref, ks_ref, kw_ref, kc_ref, vc_ref, vs_ref, vw_ref, u_ref, g_ref, *refs = refs
    if blend:
        x_out_ref, raw_ref = refs
        x = _blend_rows(x1_ref[...], wc_ref[...].T, g0_ref[...], g1_ref[...])
        x_out_ref[...] = x
    else:
        (raw_ref,) = refs
        x = x_in_ref[...]
    ms = jnp.mean(x * x, axis=-1, keepdims=True)
    h = (x * lax.rsqrt(ms + RMS_EPS)) * gain_ref[...]
    z = _dot(h.astype(BF16), w_ref[...])
    c = c_ref[...]
    s = s_ref[...]
    lane = lax.broadcasted_iota(jnp.int32, c.shape, 1)
    lo = lane < HEAD_DIM
    scale = HEAD_DIM ** -0.5 * math.log2(math.e)
    for j in range(NSA_HEADS // 2):
        r = _headnorm_rope(z[:, C_Q + LANES * j:C_Q + LANES * (j + 1)], qg_ref[...], c, s) * scale
        rr = pltpu.roll(r, HEAD_DIM, 1)
        if j // (NSA_GROUP // 2) == 0:
            even, odd = jnp.where(lo, r, 0.0), jnp.where(lo, rr, 0.0)
        else:
            even, odd = jnp.where(lo, 0.0, rr), jnp.where(lo, 0.0, r)
        q_ref[:, 2 * LANES * j:2 * LANES * j + LANES] = even.astype(BF16)
        q_ref[:, 2 * LANES * j + LANES:2 * LANES * (j + 1)] = odd.astype(BF16)
    ks_ref[...] = _headnorm_rope(z[:, C_KS:C_KS + LANES], kg_ref[...], c, s).astype(BF16)
    kw_ref[...] = _headnorm_rope(z[:, C_KW:C_KW + LANES], kg_ref[...], c, s).astype(BF16)
    half = CMP_BLOCK // 2
    for col, out in ((C_KC, kc_ref), (C_VC, vc_ref)):
        raw_ref[...] = z[:, col:col + LANES]
        for l in range(half):
            piece = raw_ref[pl.ds(l, x.shape[0] // half, stride=half), :]
            out[:, LANES * l:LANES * (l + 1)] = piece.astype(BF16)
    vs_ref[...] = z[:, C_VS:C_VS + LANES].astype(BF16)
    vw_ref[...] = z[:, C_VW:C_VW + LANES].astype(BF16)
    u_ref[:, pl.ds(pl.program_id(1), 1), :] = z[:, C_U:C_U + S5_WIDTH][:, None, :]
    g_ref[...] = _sigmoid(z[:, C_G:C_G + LANES])


def _in_proj(x_in, gain, w1, ctab, stab, qg, kg, batch, seq):
    blend = isinstance(x_in, tuple)
    n = batch * seq
    tm = IN_TM
    nt = seq // tm
    row = lambda i, b: (b * nt + i, 0)
    const = lambda i, b: (0, 0)
    tok = lambda w: pl.BlockSpec((tm, w), row)
    half = CMP_BLOCK // 2
    grouped = pl.BlockSpec((tm // half, half * KV_WIDTH), row)
    rows_bf16 = lambda r, w: jax.ShapeDtypeStruct((r, w), BF16)
    if blend:
        x1, w_col, gathered = x_in
        width = gathered.shape[1]
        x_specs = [tok(D_MODEL), pl.BlockSpec((SUBLANES, tm), lambda i, b: (0, b * nt + i)), tok(width),
                   pl.BlockSpec((tm, width), lambda i, b: (n // tm + b * nt + i, 0))]
        x_args = (x1, w_col, gathered, gathered)
    else:
        x_specs, x_args = [tok(D_MODEL)], (x_in,)
    return pl.pallas_call(
        functools.partial(_in_proj_kernel, blend=blend),
        grid=(nt, batch),
        in_specs=x_specs + [pl.BlockSpec((1, D_MODEL), const), pl.BlockSpec((D_MODEL, IN_W), const),
                            tok(LANES), tok(LANES), pl.BlockSpec((1, LANES), const),
                            pl.BlockSpec((1, LANES), const)],
        out_specs=[tok(2 * NSA_WIDTH), tok(LANES), tok(LANES), grouped, grouped, tok(LANES), tok(LANES),
                   pl.BlockSpec((tm, batch, S5_WIDTH), lambda i, b: (i, 0, 0)), tok(LANES)]
        + ([tok(D_MODEL)] if blend else []),
        out_shape=[rows_bf16(n, 2 * NSA_WIDTH), rows_bf16(n, LANES), rows_bf16(n, LANES),
                   rows_bf16(n // half, half * KV_WIDTH), rows_bf16(n // half, half * KV_WIDTH),
                   rows_bf16(n, LANES), rows_bf16(n, LANES),
                   jax.ShapeDtypeStruct((seq, batch, S5_WIDTH), F32), jax.ShapeDtypeStruct((n, LANES), F32)]
        + ([jax.ShapeDtypeStruct((n, D_MODEL), F32)] if blend else []),
        scratch_shapes=[pltpu.VMEM((tm, LANES), F32)],
        compiler_params=_cparams("parallel", "arbitrary"),
        name="in_proj",
    )(*x_args, gain, w1, ctab, stab, qg, kg)


def _compress_mlp(h_ref, w_ref, p_ref, w2_ref):
    w = w_ref[...]
    pre = _dot(h_ref[...], w)
    phi, plo = _split_bf16(p_ref[...])
    pb = _dot(phi, w) + _dot(plo, w)
    rows = pre.shape[0]
    out = None
    for h in range(NSA_KV_HEADS):
        top = pre[:, LANES * h:LANES * (h + 1)]
        bot = pre[:, LANES * (2 + h):LANES * (3 + h)]
        bias = pb[0:1, LANES * h:LANES * (h + 1)] + pb[1:2, LANES * (2 + h):LANES * (3 + h)]
        a = _gelu(top + pltpu.roll(bot, rows - 1, 0) + bias)
        o = _dot(a.astype(BF16), w2_ref[h])
        out = o if out is None else out + o
    return out


def _compress_kernel(hk_ref, hv_ref, wk_ref, wv_ref, pk_ref, pv_ref, w2k_ref, w2v_ref,
                     kg_ref, c_ref, s_ref, kc_ref, vc_ref):
    k = _compress_mlp(hk_ref, wk_ref, pk_ref, w2k_ref)
    kc_ref[...] = _headnorm_rope(k, kg_ref[...], c_ref[...], s_ref[...]).astype(BF16)
    vc_ref[...] = _compress_mlp(hv_ref, wv_ref, pv_ref, w2v_ref).astype(BF16)


def _compress(hk, hv, wk, wv, pk, pv, w2k, w2v, kg, ccmp, scmp, batch):
    rows = hk.shape[0] // batch
    width = hk.shape[1]
    const2 = lambda b: (0, 0)
    const3 = lambda b: (0, 0, 0)
    rowb = lambda w: pl.BlockSpec((rows, w), lambda b: (b, 0))
    return pl.pallas_call(
        _compress_kernel,
        grid=(batch,),
        in_specs=[rowb(width), rowb(width),
                  pl.BlockSpec(wk.shape, const2), pl.BlockSpec(wv.shape, const2),
                  pl.BlockSpec(pk.shape, const2), pl.BlockSpec(pv.shape, const2),
                  pl.BlockSpec(w2k.shape, const3), pl.BlockSpec(w2v.shape, const3),
                  pl.BlockSpec((1, LANES), const2), rowb(LANES), rowb(LANES)],
        out_specs=[rowb(LANES), rowb(LANES)],
        out_shape=[jax.ShapeDtypeStruct((hk.shape[0], LANES), BF16)] * 2,
        compiler_params=_cparams("parallel"),
        name="compress_kv",
    )(hk, hv, wk, wv, pk, pv, w2k, w2v, kg, ccmp, scmp)


def _compress_weights(pos_emb, w1, w2):
    half = CMP_BLOCK // 2
    eye = jnp.eye(NSA_KV_HEADS, dtype=F32)
    w1r = w1.reshape(2, half, HEAD_DIM, CMP_HIDDEN).astype(BF16)
    zero = jnp.zeros_like(w1r[0])
    wexp = jnp.concatenate(
        [jnp.stack([w1r[s] if p == h else zero for p in range(NSA_KV_HEADS)], axis=1)
         .reshape(half * KV_WIDTH, CMP_HIDDEN) for s in range(2) for h in range(NSA_KV_HEADS)], axis=1)
    pexp = jnp.broadcast_to(pos_emb.reshape(2, half, 1, HEAD_DIM), (2, half, NSA_KV_HEADS, HEAD_DIM))
    pexp = jnp.pad(pexp.reshape(2, half * KV_WIDTH), ((0, SUBLANES - 2), (0, 0)))
    w2e = jnp.einsum("cd,ph->pchd", w2, eye).reshape(NSA_KV_HEADS, CMP_HIDDEN, KV_WIDTH)
    return wexp, pexp, w2e.astype(BF16)


def _attn_kernel(q_ref, ks_ref, vs_ref, kw_ref, vw_ref, kc_ref, vc_ref, g_ref, et_ref, ovt_ref, oh_ref, wb_ref,
                 cb_ref, o_ref, qc_ref, s_ref, p_ref, sw_ref, pw_ref, m_ref, al_ref, oc_ref, as_ref, aw_ref,
                 *, n_cmp, n_slc, top_k):
    tq, tk, wk, rb, nh = ATT_TQ, ATT_TK, ATT_WK, ATT_RB, NSA_HEADS
    rows, grp_rows = nh * tq, NSA_GROUP * tq
    qk_rows = 4 * tq
    q0 = pl.program_id(1) * tq
    for a in range(nh):
        qc_ref[a * tq:(a + 1) * tq, 0:LANES] = q_ref[:, LANES * a:LANES * (a + 1)]
    q8 = qc_ref[:, 0:LANES]
    lane = lax.broadcasted_iota(jnp.int32, (tq, LANES), 1)
    t_row = q0 + lax.broadcasted_iota(jnp.int32, (tq, LANES), 0)

    s_c = _dot_t(q8, kc_ref[...]).reshape(nh, tq, LANES)
    mask_c = ((lane * CMP_STRIDE + (CMP_BLOCK - 1)) <= t_row) & (lane < n_cmp)
    sm = jnp.where(mask_c[None], s_c, NEG_BIG)
    p = jnp.where(mask_c[None], jnp.exp2(sm - jnp.max(sm, axis=-1, keepdims=True)), 0.0)
    l = jnp.sum(p, axis=-1, keepdims=True)
    p_c = p * jnp.where(l > 0.0, 1.0 / l, 0.0)
    oc_ref[...] = _dot(p_c.reshape(rows, LANES).astype(BF16), vc_ref[...])

    blk = lax.broadcasted_iota(jnp.int32, (n_slc, tq), 0)
    cur = (q0 + lax.broadcasted_iota(jnp.int32, (n_slc, tq), 1)) >> int(math.log2(SLC_BLOCK))
    forced = (blk == 0) | (blk == cur) | (blk == cur - 1)
    for h in range(NSA_KV_HEADS):
        ps = p_c[NSA_GROUP * h]
        for g in range(1, NSA_GROUP):
            ps = ps + p_c[NSA_GROUP * h + g]
        hi, lo = _split_bf16(ps)
        imp = (_dot_t(ovt_ref[...], hi) + _dot_t(ovt_ref[...], lo))[0:n_slc]
        imp = jnp.where(forced, jnp.inf, jnp.where(blk > cur, -jnp.inf, imp))
        rank = jnp.zeros((n_slc, tq), F32)
        for i in range(n_slc):
            row = imp[i:i + 1, :]
            rank = rank + jnp.where((row > imp) | ((row == imp) & (blk > i)), 1.0, 0.0)
        bias_t = jnp.where(rank < float(top_k), 0.0, NEG_BIG)
        bias = jnp.concatenate([bias_t, jnp.zeros((LANES - n_slc, tq), F32)], axis=0).T.astype(BF16)
        for g in range(NSA_GROUP):
            a = NSA_GROUP * h + g
            qc_ref[a * tq:(a + 1) * tq, LANES:2 * LANES] = bias

    def softmax_pass(s_buf, p_buf, width, causal_case, online):
        for r in range(rows // rb):
            r0 = r * rb
            s = s_buf[pl.ds(r0, rb), 0:width]
            if causal_case is not None:
                s = s + cb_ref[causal_case, pl.ds(r0 % tq, rb), 0:width]
            m_new = jnp.broadcast_to(jnp.max(s, axis=1, keepdims=True), (rb, LANES))
            if online:
                m_old = m_ref[pl.ds(r0, rb), :]
                m_new = jnp.maximum(m_old, m_new)
                al_ref[pl.ds(r0, rb), :] = jnp.exp2(m_old - m_new)
                m_ref[pl.ds(r0, rb), :] = m_new
            pp = jnp.exp2(s - jnp.concatenate([m_new] * (width // LANES), axis=1))
            p_buf[pl.ds(r0, rb), 0:width] = pp.astype(BF16)

    def pv(acc_ref, p_buf, v, width, online):
        lane_v = lax.broadcasted_iota(jnp.int32, v.shape, 1)
        for h in range(NSA_KV_HEADS):
            rs = slice(h * grp_rows, (h + 1) * grp_rows)
            own = (lane_v < HEAD_DIM) if h == 0 else (lane_v >= HEAD_DIM)
            upd = _dot(p_buf[rs, 0:width], jnp.where(own, v, jnp.ones_like(v)))
            acc_ref[rs, :] = (al_ref[rs, :] * acc_ref[rs, :] + upd) if online else upd

    m_ref[...] = jnp.full(m_ref.shape, 0.1 * NEG_BIG, F32)
    as_ref[...] = jnp.zeros(as_ref.shape, F32)

    def slc_chunk(c, diagonal, width=tk):
        k0 = c * tk
        s_buf, p_buf = s_ref.at[c % 2], p_ref.at[c % 2]
        kcat = jnp.concatenate([ks_ref[pl.ds(k0, width), :], et_ref[pl.ds(k0, width), :]], axis=1)
        for g in range(rows // qk_rows):
            rs = slice(g * qk_rows, (g + 1) * qk_rows)
            s_buf[rs, 0:width] = _dot_t(qc_ref[rs, :], kcat)
        softmax_pass(s_buf, p_buf, width, ((q0 - k0) >> int(math.log2(tq))) if diagonal else None, True)
        pv(as_ref, p_buf, vs_ref[pl.ds(k0, width), :], width, True)

    def window_and_output():
        w0 = pl.multiple_of(jnp.maximum(q0 - WINDOW, 0), LANES)
        w_case = jnp.minimum(q0 >> int(math.log2(tq)), WINDOW // tq)
        kcat_w = jnp.concatenate([kw_ref[pl.ds(w0, wk), :], wb_ref[w_case]], axis=1)
        for g in range(rows // qk_rows):
            rs = slice(g * qk_rows, (g + 1) * qk_rows)
            sw_ref[rs, :] = _dot_t(jnp.concatenate([qc_ref[rs, 0:LANES], oh_ref[rs, :]], axis=1), kcat_w)
        softmax_pass(sw_ref, pw_ref, wk, None, False)
        pv(aw_ref, pw_ref, vw_ref[pl.ds(w0, wk), :], wk, False)

        lo_half = lane < HEAD_DIM
        for j in range(nh // 2):
            comb = []
            for a in (2 * j, 2 * j + 1):
                rs = slice(a * tq, (a + 1) * tq)
                o_s, o_w = as_ref[rs, :], aw_ref[rs, :]
                comb.append(g_ref[:, 3 * a:3 * a + 1] * oc_ref[rs, :]
                            + g_ref[:, 3 * a + 1:3 * a + 2] * (o_s * (1.0 / pltpu.roll(o_s, HEAD_DIM, 1)))
                            + g_ref[:, 3 * a + 2:3 * a + 3] * (o_w * (1.0 / pltpu.roll(o_w, HEAD_DIM, 1))))
            even, odd = comb
            if j // (NSA_GROUP // 2) == 0:
                pair = jnp.where(lo_half, even, pltpu.roll(odd, HEAD_DIM, 1))
            else:
                pair = jnp.where(lo_half, pltpu.roll(even, HEAD_DIM, 1), odd)
            o_ref[:, LANES * j:LANES * (j + 1)] = pair.astype(BF16)

    c_last = q0 >> int(math.log2(tk))
    early = ((q0 >> int(math.log2(tq))) & (tk // tq - 1)) < tk // tq // 2
    for n_full in range(ks_ref.shape[0] // tk):
        for first_half in (True, False):
            @pl.when((c_last == n_full) & (early if first_half else jnp.logical_not(early)))
            def _():
                for c in range(n_full):
                    slc_chunk(c, False)
                slc_chunk(n_full, True, tk // 2 if first_half else tk)
                window_and_output()


def _pack_w_in(w):
    sizes = (NSA_WIDTH + 6 * KV_WIDTH, 3 * NSA_HEADS, S5_WIDTH, D_MODEL, D_MODEL)
    o = np.cumsum((0,) + sizes)
    qkv, g, u, ga, gs = [w[:, o[i]:o[i + 1]] for i in range(len(sizes))]
    g = jnp.pad(g, ((0, 0), (0, LANES - 3 * NSA_HEADS)))
    w1 = jnp.concatenate([qkv, u, g], axis=1).astype(BF16)
    return w1, ga.astype(BF16), gs.astype(BF16)


def _nsa_branch(x_in, gain, w1, ctab, stab, ccmp, scmp, q_gain, k_gain, cmp_k, cmp_v, batch, seq):
    qg = jnp.tile(q_gain, 2)[None, :]
    kg = jnp.tile(k_gain, 2)[None, :]
    q, ks, kw, hk, hv, vs, vw, u_tm, gates, *x_out = _in_proj(x_in, gain, w1, ctab, stab, qg, kg, batch, seq)
    wk, pk, w2k = _compress_weights(*cmp_k)
    wv, pv, w2v = _compress_weights(*cmp_v)
    kc, vc = _compress(hk, hv, wk, wv, pk, pv, w2k, w2v, kg, ccmp, scmp, batch)
    o_a = _attention(q, ks, vs, kw, vw, kc, vc, gates, batch, seq)
    return o_a, u_tm, (x_out[0] if x_out else x_in)


def _cmp_rope_tables(ctab, stab, batch, seq):
    def pick(t):
        t = t.reshape(batch, seq, LANES)[:, CMP_BLOCK - 1::CMP_STRIDE]
        t = jnp.pad(t, ((0, 0), (0, seq // CMP_STRIDE - t.shape[1]), (0, 0)))
        return t.reshape(batch * (seq // CMP_STRIDE), LANES)
    return pick(ctab), pick(stab)


def _s5_disc_kernel(are_ref, aim_ref, ldt_ref, lre_ref, lim_ref, fre_ref, fim_ref):
    lam_re = jnp.minimum(are_ref[...], -1e-4)
    lam_im = aim_ref[...]
    dt = jnp.exp(ldt_ref[...])
    mag = jnp.exp(lam_re * dt)
    ang = lam_im * dt
    lb_re = mag * jnp.cos(ang)
    lb_im = mag * jnp.sin(ang)
    den = lam_re * lam_re + lam_im * lam_im
    lre_ref[...] = lb_re
    lim_ref[...] = lb_im
    fre_ref[...] = ((lb_re - 1.0) * lam_re + lb_im * lam_im) / den
    fim_ref[...] = (lb_im * lam_re - (lb_re - 1.0) * lam_im) / den


def _s5_weights(a_re, a_im, log_dt, b_re, b_im, c_re, c_im):
    shp = jax.ShapeDtypeStruct((S5_GROUPS, S5_STATE), F32)
    lb_re, lb_im, f_re, f_im = pl.pallas_call(
        _s5_disc_kernel, out_shape=[shp] * 4, name="s5_discretise",
    )(a_re, a_im, log_dt[:, None])
    bb_re = f_re[..., None] * b_re - f_im[..., None] * b_im
    bb_im = f_re[..., None] * b_im + f_im[..., None] * b_re
    eye = jnp.eye(S5_GROUPS, dtype=F32)

    def in_map(bb):
        return jnp.einsum("gpc,gh->gchp", bb, eye).reshape(S5_WIDTH, S5_NSTATE)

    def out_map(cc):
        return jnp.einsum("gcp,gh->hpgc", cc, eye).reshape(S5_NSTATE, S5_WIDTH)

    w_b = jnp.concatenate([in_map(bb_re), in_map(bb_im)], axis=1).astype(BF16)
    w_c = jnp.concatenate([out_map(c_re), -out_map(c_im)], axis=0).astype(BF16)
    return lb_re.reshape(1, S5_NSTATE), lb_im.reshape(1, S5_NSTATE), w_b, w_c


def _s5_kernel(u_ref, wb_ref, lre_ref, lim_ref, wc_ref, d_ref, wg_ref, o_ref, x_ref, st_ref, *, batch):
    ns = S5_NSTATE

    @pl.when(pl.program_id(0) == 0)
    def _():
        st_ref[...] = jnp.zeros(st_ref.shape, F32)

    u = u_ref[...].reshape(u_ref.shape[0] * batch, S5_WIDTH)
    x_ref[...] = _dot(u.astype(BF16), wb_ref[...])
    lre = jnp.broadcast_to(lre_ref[...], (batch, ns))
    lim = jnp.broadcast_to(lim_ref[...], (batch, ns))

    def step(t, carry):
        xr, xi = carry
        r0 = pl.multiple_of(t * batch, batch)
        nr = lre * xr - lim * xi + x_ref[pl.ds(r0, batch), 0:ns]
        ni = lre * xi + lim * xr + x_ref[pl.ds(r0, batch), ns:2 * ns]
        x_ref[pl.ds(r0, batch), 0:ns] = nr
        x_ref[pl.ds(r0, batch), ns:2 * ns] = ni
        return nr, ni

    steps = u.shape[0] // batch
    xr, xi = lax.fori_loop(0, steps, step, (st_ref[:, 0:ns], st_ref[:, ns:2 * ns]), unroll=4)
    st_ref[:, 0:ns] = xr
    st_ref[:, ns:2 * ns] = xi
    y = _dot(x_ref[...].astype(BF16), wc_ref[...]) + d_ref[...] * u
    z = _dot(_gelu(y).astype(BF16), wg_ref[...])
    o = z[:, :S5_WIDTH] * _sigmoid(z[:, S5_WIDTH:])
    for c in range(S5_WIDTH // LANES):
        o_ref[c] = o[:, c * LANES:(c + 1) * LANES]


def _s5(u_tm, lre, lim, w_b, w_c, d_skip, w_glu, batch, seq):
    rows = S5_T * batch
    const = lambda t: (0, 0)
    return pl.pallas_call(
        functools.partial(_s5_kernel, batch=batch),
        grid=(seq // S5_T,),
        in_specs=[pl.BlockSpec((S5_T, batch, S5_WIDTH), lambda t: (t, 0, 0)),
                  pl.BlockSpec(w_b.shape, const), pl.BlockSpec(lre.shape, const), pl.BlockSpec(lim.shape, const),
                  pl.BlockSpec(w_c.shape, const), pl.BlockSpec((1, S5_WIDTH), const),
                  pl.BlockSpec(w_glu.shape, const)],
        out_specs=pl.BlockSpec((S5_WIDTH // LANES, rows, LANES), lambda t: (0, t, 0)),
        out_shape=jax.ShapeDtypeStruct((S5_WIDTH // LANES, seq * batch, LANES), F32),
        scratch_shapes=[pltpu.VMEM((rows, 2 * S5_NSTATE), F32), pltpu.VMEM((batch, 2 * S5_NSTATE), F32)],
        compiler_params=_cparams("arbitrary"),
        name="s5_scan",
    )(u_tm, w_b, lre, lim, w_c, d_skip, w_glu)


RT_GROUP_ROW = 0
RT_EXPERT_ROW = SUBLANES


def _rmsnorm(x, gain):
    return (x * lax.rsqrt(jnp.mean(x * x, axis=-1, keepdims=True) + RMS_EPS)) * gain


def _merge_kernel(x_ref, g1_ref, wga_ref, wgs_ref, oa_ref, ob_ref, pa_ref, ps_ref, wo_ref, g2_ref,
                  wrh_ref, wrl_ref, rb_ref, tri_ref, x1_ref, h2_ref, e_ref, w_ref, r_ref, cnt_ref,
                  lg_ref, carry_ref):
    x = x_ref[...]
    h = _rmsnorm(x, g1_ref[...]).astype(BF16)
    gate_a = _sigmoid(_dot(h, wga_ref[...]))
    gate_s = _sigmoid(_dot(h, wgs_ref[...]))
    rows_b = pl.ds(pl.program_id(1), x.shape[0], stride=ob_ref.shape[1] // x.shape[0])
    o_b = jnp.concatenate([ob_ref[c, rows_b, :] for c in range(S5_WIDTH // LANES)], axis=1)
    merged = gate_a * _dot(oa_ref[...], pa_ref[...]) + gate_s * _dot(o_b.astype(BF16), ps_ref[...])
    x1 = x + _dot(merged.astype(BF16), wo_ref[...])
    x1_ref[...] = x1
    h2 = _rmsnorm(x1, g2_ref[...])
    h2_ref[...] = _pack_rows(h2)
    hi, lo = _split_bf16(h2)
    lg_ref[...] = (_dot_t(wrh_ref[...], hi) + _dot_t(wrh_ref[...], lo) + _dot_t(wrl_ref[...], hi)
                   + rb_ref[...])
    _route_tile(lg_ref, tri_ref, e_ref, w_ref, r_ref, cnt_ref, carry_ref,
                (pl.program_id(0) == 0) & (pl.program_id(1) == 0))


def _router_weights(w_rg, b_rg, w_re, b_re):
    def rows(group_part, expert_part):
        return jnp.concatenate([
            group_part, jnp.zeros((RT_EXPERT_ROW - N_EXPERT_GROUPS,) + group_part.shape[1:], F32),
            expert_part, jnp.zeros((LANES - RT_EXPERT_ROW - N_EXPERTS,) + group_part.shape[1:], F32)], axis=0)

    hi, lo = _split_bf16(rows(w_rg.T, w_re.T))
    return hi, lo, rows(b_rg[:, None], b_re[:, None])


def _merge(x2, g1, w_ga, w_gs, o_a, ob_tm, p_a, p_s, w_o, g2, wrh, wrl, rb, batch, seq):
    n = x2.shape[0]
    tm = MRG_TM
    nt = seq // tm
    row = lambda i, b: (b * nt + i, 0)
    const = lambda i, b: (0, 0)
    tok = lambda w: pl.BlockSpec((tm, w), row)
    full = lambda a: pl.BlockSpec(a.shape, const)
    tri = jnp.asarray(np.triu(np.ones((tm, tm), np.float32), 1), BF16)
    lanes_tok = pl.BlockSpec((SUBLANES, tm), lambda i, b: (0, b * nt + i))
    return pl.pallas_call(
        _merge_kernel,
        grid=(nt, batch),
        in_specs=[tok(D_MODEL), full(g1), full(w_ga), full(w_gs), tok(NSA_WIDTH),
                  pl.BlockSpec((S5_WIDTH // LANES, tm * batch, LANES), lambda i, b: (0, i, 0)),
                  full(p_a), full(p_s), full(w_o),
                  full(g2), full(wrh), full(wrl), full(rb), full(tri)],
        out_specs=[tok(D_MODEL), tok(D_MODEL // 2), lanes_tok, lanes_tok, lanes_tok,
                   pl.BlockSpec((N_EXPERTS, LANES), const)],
        out_shape=[jax.ShapeDtypeStruct((n, D_MODEL), F32), jax.ShapeDtypeStruct((n, D_MODEL // 2), jnp.int32),
                   jax.ShapeDtypeStruct((SUBLANES, n), jnp.int32), jax.ShapeDtypeStruct((SUBLANES, n), F32),
                   jax.ShapeDtypeStruct((SUBLANES, n), jnp.int32), jax.ShapeDtypeStruct((N_EXPERTS, LANES), F32)],
        scratch_shapes=[pltpu.VMEM((LANES, tm), F32), pltpu.VMEM((N_EXPERTS, LANES), F32)],
        compiler_params=_cparams("arbitrary", "arbitrary"),
        name="merge_out_proj",
    )(x2, g1, w_ga, w_gs, o_a, ob_tm, p_a, p_s, w_o, g2, wrh, wrl, rb, tri)


def _first_argmax(v, rows):
    vmax = jnp.max(v, axis=0, keepdims=True)
    idx = jnp.min(jnp.where(v == vmax, rows, SUBLANES), axis=0, keepdims=True)
    return vmax, idx


def _route_tile(lg_ref, tri_ref, e_ref, w_ref, r_ref, cnt_ref, carry_ref, first):
    tt = lg_ref.shape[1]

    @pl.when(first)
    def _():
        carry_ref[...] = jnp.zeros(carry_ref.shape, F32)

    rows = lax.broadcasted_iota(jnp.int32, (SUBLANES, tt), 0)
    gl = jnp.where(rows < N_EXPERT_GROUPS, lg_ref[RT_GROUP_ROW:RT_GROUP_ROW + SUBLANES, :], -jnp.inf)
    gmax, g_sel = _first_argmax(gl, rows)
    g_w = 1.0 / jnp.sum(jnp.exp(gl - gmax), axis=0, keepdims=True)

    el = jnp.zeros((EXPERTS_PER_GROUP, tt), F32)
    for g in range(N_EXPERT_GROUPS):
        r0 = RT_EXPERT_ROW + EXPERTS_PER_GROUP * g
        el = jnp.where(g_sel == g, lg_ref[r0:r0 + EXPERTS_PER_GROUP, :], el)
    ex = jnp.exp(el - jnp.max(el, axis=0, keepdims=True))
    p = ex / jnp.sum(ex, axis=0, keepdims=True)
    p1, i1 = _first_argmax(p, rows)
    p2, i2 = _first_argmax(jnp.where(rows == i1, -1.0, p), rows)
    e1 = g_sel * EXPERTS_PER_GROUP + i1
    e2 = g_sel * EXPERTS_PER_GROUP + i2
    scale = g_w / (p1 + p2)

    erow = lax.broadcasted_iota(jnp.int32, (N_EXPERTS, tt), 0)
    oh1 = jnp.where(erow == e1, 1.0, 0.0)
    oh2 = jnp.where(erow == e2, 1.0, 0.0)
    both = oh1 + oh2
    before = carry_ref[:, 0:1] + _dot(both.astype(BF16), tri_ref[...])
    carry_ref[...] = carry_ref[...] + jnp.sum(both, axis=1, keepdims=True)
    cnt_ref[...] = carry_ref[...]

    zi = jnp.zeros((SUBLANES - 2, tt), jnp.int32)
    e_ref[...] = jnp.concatenate([e1, e2, zi], axis=0)
    w_ref[...] = jnp.concatenate([p1 * scale, p2 * scale, zi.astype(F32)], axis=0)
    r_ref[...] = jnp.concatenate([jnp.sum(oh1 * before, axis=0, keepdims=True),
                                  jnp.sum(oh2 * before, axis=0, keepdims=True),
                                  zi.astype(F32)], axis=0).astype(jnp.int32)


def _expert_kernel(be_ref, nb_ref, x_ref, wg_ref, wu_ref, wd_ref, o_ref, wgb_ref, wub_ref, wdb_ref):
    i = pl.program_id(0)
    half = D_MODEL // 2

    def mlp(wg, wu, wd):
        xa, xb = _unpack_rows(x_ref[...])
        xa, xb = xa.astype(BF16), xb.astype(BF16)
        g = _dot(xa, wg[0:half, :]) + _dot(xb, wg[half:, :])
        u = _dot(xa, wu[0:half, :]) + _dot(xb, wu[half:, :])
        o_ref[...] = _pack_rows(_dot(((g * _sigmoid(g)) * u).astype(BF16), wd))

    new_expert = (i == 0) | (be_ref[i] != be_ref[jnp.maximum(i - 1, 0)])

    @pl.when((i < nb_ref[0]) & new_expert)
    def _():
        wg, wu, wd = wg_ref[...].astype(BF16), wu_ref[...].astype(BF16), wd_ref[...].astype(BF16)
        wgb_ref[...] = wg
        wub_ref[...] = wu
        wdb_ref[...] = wd
        mlp(wg, wu, wd)

    @pl.when((i < nb_ref[0]) & jnp.logical_not(new_expert))
    def _():
        mlp(wgb_ref[...], wub_ref[...], wdb_ref[...])

    @pl.when(i >= nb_ref[0])
    def _():
        o_ref[...] = jnp.zeros(o_ref.shape, o_ref.dtype)


def _experts(blk_expert, n_used, buf, w_gate, w_up, w_down, layer):
    cap, width = buf.shape
    blk = MOE_BLK
    row = lambda i, be, nb: (jnp.minimum(i, nb[0] - 1), 0)
    wmap = lambda i, be, nb: (layer, be[jnp.minimum(i, nb[0] - 1)], 0, 0)
    return pl.pallas_call(
        _expert_kernel,
        grid_spec=pltpu.PrefetchScalarGridSpec(
            num_scalar_prefetch=2,
            grid=(cap // blk,),
            in_specs=[pl.BlockSpec((blk, width), row),
                      pl.BlockSpec((None, None, D_MODEL, EXPERT_FF), wmap),
                      pl.BlockSpec((None, None, D_MODEL, EXPERT_FF), wmap),
                      pl.BlockSpec((None, None, EXPERT_FF, D_MODEL), wmap)],
            out_specs=pl.BlockSpec((blk, width), lambda i, be, nb: (i, 0)),
            scratch_shapes=[pltpu.VMEM((D_MODEL, EXPERT_FF), BF16), pltpu.VMEM((D_MODEL, EXPERT_FF), BF16),
                            pltpu.VMEM((EXPERT_FF, D_MODEL), BF16)],
        ),
        out_shape=jax.ShapeDtypeStruct((cap, width), buf.dtype),
        compiler_params=_cparams("arbitrary"),
        name="moe_experts",
    )(blk_expert, n_used, buf, w_gate, w_up, w_down)


SC_CH = 64


def _sc_mesh():
    return plsc.VectorSubcoreMesh(core_axis_name="c", subcore_axis_name="s")


def _sc_worker(info):
    return lax.axis_index("s") * info.num_cores + lax.axis_index("c")


def _sc_gather_rows(table, idx, rows=None):
    info = plsc.get_sparse_core_info()
    lanes, workers = info.num_lanes, info.num_cores * info.num_subcores
    invert = rows is not None
    rows = rows if invert else idx.shape[0]
    width = table.shape[1]
    per_w = rows // workers
    n_chunks = per_w // SC_CH
    assert per_w % (2 * SC_CH) == 0
    scan = 2048
    n_tok = table.shape[0]
    assert not invert or (n_tok & (n_tok - 1) == 0 and idx.shape[0] % scan == 0)

    def body(table_hbm, idx_hbm, out_hbm, idx_v, rows_a, rows_b, sem_a, sem_b, *scan_v):
        base = _sc_worker(info) * per_w
        if invert:
            lane = lax.iota(jnp.int32, lanes)

            @pl.loop(0, per_w // lanes)
            def _(j):
                idx_v[pl.ds(j * lanes, lanes)] = (base + j * lanes + lane) & (n_tok - 1)

            @pl.loop(0, idx.shape[0] // scan)
            def _(c):
                pltpu.sync_copy(idx_hbm.at[pl.ds(c * scan, scan)], scan_v[0])

                @plsc.parallel_loop(0, scan // lanes, unroll=8)
                def _(j):
                    rel = scan_v[0][pl.ds(j * lanes, lanes)] - base
                    mine = (rel >= 0) & (rel < per_w)
                    token = (c * scan + j * lanes + lane) & (n_tok - 1)
                    plsc.store_scatter(idx_v, [jnp.where(mine, rel, 0)], token, mask=mine)
        else:
            pltpu.sync_copy(idx_hbm.at[pl.ds(base, per_w)], idx_v)

        def gather(j, buf, sem):
            return pltpu.make_async_copy(table_hbm.at[idx_v.at[pl.ds(j * SC_CH, SC_CH)]], buf, sem)

        def write(j, buf):
            pltpu.sync_copy(buf, out_hbm.at[pl.ds(base + j * SC_CH, SC_CH)])

        gather(0, rows_a, sem_a).start()

        @pl.loop(0, n_chunks, step=2)
        def _(j):
            gather(j + 1, rows_b, sem_b).start()
            gather(j, rows_a, sem_a).wait()
            write(j, rows_a)

            @pl.when(j + 2 < n_chunks)
            def _():
                gather(j + 2, rows_a, sem_a).start()

            gather(j + 1, rows_b, sem_b).wait()
            write(j + 1, rows_b)

    chunk_buf = pltpu.VMEM((SC_CH, width), table.dtype)
    scratch = [pltpu.VMEM((per_w,), jnp.int32), chunk_buf, chunk_buf,
               pltpu.SemaphoreType.DMA, pltpu.SemaphoreType.DMA]
    return pl.kernel(
        body, out_type=jax.ShapeDtypeStruct((rows, width), table.dtype), mesh=_sc_mesh(),
        scratch_types=scratch + ([pltpu.VMEM((scan,), jnp.int32)] if invert else []),
        compiler_params=pltpu.CompilerParams(needs_layout_passes=False) if invert else None,
        name="sc_dispatch_rows" if invert else "sc_gather_rows",
    )(table, idx)


def _blend_kernel(x1_ref, w_ref, g0_ref, g1_ref, o_ref):
    o_ref[...] = _blend_rows(x1_ref[...], w_ref[...].T, g0_ref[...], g1_ref[...])


def _blend(x1, w_col, gathered):
    n = x1.shape[0]
    tt = DSP_TT
    nt = n // tt
    width = gathered.shape[1]
    return pl.pallas_call(
        _blend_kernel,
        grid=(nt,),
        in_specs=[pl.BlockSpec((tt, D_MODEL), lambda i: (i, 0)),
                  pl.BlockSpec((SUBLANES, tt), lambda i: (0, i)),
                  pl.BlockSpec((tt, width), lambda i: (i, 0)),
                  pl.BlockSpec((tt, width), lambda i: (nt + i, 0))],
        out_specs=pl.BlockSpec((tt, D_MODEL), lambda i: (i, 0)),
        out_shape=jax.ShapeDtypeStruct((n, D_MODEL), F32),
        compiler_params=_cparams("parallel"),
        name="moe_blend",
    )(x1, w_col, gathered, gathered)


def _moe(x1, h2, routing, w_gate, w_up, w_down, layer, later):
    n = x1.shape[0]
    e_idx, wts, rank, counts = routing
    counts = counts[:, 0].astype(jnp.int32)
    padded = (counts + MOE_BLK - 1) // MOE_BLK * MOE_BLK
    pends = jnp.cumsum(padded)
    pstarts = pends - padded
    experts = jnp.arange(N_EXPERTS, dtype=jnp.int32)
    start_of = jnp.sum(jnp.where(e_idx[None, :EXPERT_TOP_K] == experts[:, None, None],
                                 pstarts[:, None, None], 0), axis=0)
    dest = start_of + rank[:EXPERT_TOP_K]
    cap = n * EXPERT_TOP_K + N_EXPERTS * MOE_BLK
    n_blk = cap // MOE_BLK
    blk_row = jnp.arange(n_blk, dtype=jnp.int32) * MOE_BLK
    blk_expert = jnp.minimum(jnp.sum((pends[None, :] <= blk_row[:, None]).astype(jnp.int32), axis=1),
                             N_EXPERTS - 1)
    n_used = (pends[-1:] // MOE_BLK).astype(jnp.int32)
    dest_flat, later = lax.optimization_barrier((dest.reshape(EXPERT_TOP_K * n), later))
    buf = _sc_gather_rows(h2, dest_flat, rows=cap)
    out_buf = _experts(blk_expert, n_used, buf, w_gate, w_up, w_down, layer)
    return (x1, wts, _sc_gather_rows(out_buf, dest_flat)), later


def kernel(x, positions, norm_mix, norm_ffn, w_in, q_gain, k_gain, cmp_k_pos, cmp_k_w1, cmp_k_w2, cmp_v_pos,
           cmp_v_w1, cmp_v_w2, s5_a_re, s5_a_im, s5_log_dt, s5_b_re, s5_b_im, s5_c_re, s5_c_im, s5_d, s5_glu,
           proj_nsa, proj_s5, w_out, router_group, router_group_bias, router_expert, router_expert_bias,
           expert_gate, expert_up, expert_down):
    batch, seq, _ = x.shape
    depth = w_in.shape[0]
    n = batch * seq
    inv_freq = ROPE_THETA ** (-jnp.arange(0, ROPE_DIMS, 2, dtype=jnp.float32) / ROPE_DIMS)
    ctab, stab = _rope_tables(positions.reshape(n, 1), inv_freq)
    ccmp, scmp = _cmp_rope_tables(ctab, stab, batch, seq)
    x_in = x.reshape(n, D_MODEL)
    stacked = (norm_mix, norm_ffn, w_in, q_gain, k_gain, cmp_k_pos, cmp_k_w1, cmp_k_w2, cmp_v_pos, cmp_v_w1,
               cmp_v_w2, s5_a_re, s5_a_im, s5_log_dt, s5_b_re, s5_b_im, s5_c_re, s5_c_im, s5_d, s5_glu, proj_nsa,
               proj_s5, w_out, router_group, router_group_bias, router_expert, router_expert_bias)
    for l in range(depth):
        (norm_mix, norm_ffn, w_in, q_gain, k_gain, cmp_k_pos, cmp_k_w1, cmp_k_w2, cmp_v_pos, cmp_v_w1,
         cmp_v_w2, s5_a_re, s5_a_im, s5_log_dt, s5_b_re, s5_b_im, s5_c_re, s5_c_im, s5_d, s5_glu, proj_nsa,
         proj_s5, w_out, router_group, router_group_bias, router_expert, router_expert_bias) = stacked
        w1, w_ga, w_gs = _pack_w_in(w_in[l])
        o_a, u_tm, x2 = _nsa_branch(x_in, norm_mix[l][None], w1, ctab, stab, ccmp, scmp, q_gain[l], k_gain[l],
                                    (cmp_k_pos[l], cmp_k_w1[l], cmp_k_w2[l]),
                                    (cmp_v_pos[l], cmp_v_w1[l], cmp_v_w2[l]), batch, seq)
        lre, lim, w_b, w_c = _s5_weights(s5_a_re[l], s5_a_im[l], s5_log_dt[l], s5_b_re[l], s5_b_im[l],
                                         s5_c_re[l], s5_c_im[l])
        ob_tm = _s5(u_tm, lre, lim, w_b, w_c, s5_d[l][None],
                    s5_glu[l].astype(BF16), batch, seq)
        wrh, wrl, rb = _router_weights(router_group[l], router_group_bias[l], router_expert[l],
                                       router_expert_bias[l])
        x1, h2, *routing = _merge(x2, norm_mix[l][None], w_ga, w_gs, o_a, ob_tm, proj_nsa[l].astype(BF16),
                                  proj_s5[l].astype(BF16), w_out[l].astype(BF16), norm_ffn[l][None],
                                  wrh, wrl, rb, batch, seq)
        x_in, stacked = _moe(x1, h2, routing, expert_gate, expert_up, expert_down, l, stacked)
    return _blend(*x_in).reshape(batch, seq, D_MODEL)


def _attn_tables(seq):
    n_cmp = (seq - CMP_BLOCK) // CMP_STRIDE + 1
    n_slc = seq // SLC_BLOCK
    key_blk = np.arange(seq) // SLC_BLOCK
    expand_t = (key_blk[:, None] == np.arange(LANES)[None, :]).astype(np.float32)
    cmp_start = np.arange(LANES) * CMP_STRIDE
    slc_start = np.arange(LANES) * SLC_BLOCK
    overlap_t = ((cmp_start[None, :] < slc_start[:, None] + SLC_BLOCK)
                 & (cmp_start[None, :] + CMP_BLOCK > slc_start[:, None])
                 & (np.arange(LANES)[None, :] < n_cmp) & (np.arange(LANES)[:, None] < n_slc))
    return (jnp.asarray(expand_t, BF16), jnp.asarray(overlap_t.astype(np.float32), BF16), n_cmp, n_slc)


def _attn_mask_tables():
    tq, tk, wk = ATT_TQ, ATT_TK, ATT_WK
    ql = np.arange(tq)
    onehot = np.tile(np.eye(tq, dtype=np.float32), (NSA_HEADS, 1))
    k = np.arange(wk)[:, None]
    band = []
    for c in range(WINDOW // tq + 1):
        if c < WINDOW // tq:
            keep = k <= c * tq + ql[None, :]
        else:
            keep = (k > ql[None, :]) & (k <= ql[None, :] + WINDOW)
        band.append(np.where(keep, 0.0, NEG_BIG))
    kl = np.arange(tk)[None, :]
    causal = [np.where(kl <= c * tq + ql[:, None], 0.0, NEG_BIG) for c in range(tk // tq)]
    return (jnp.asarray(onehot, BF16), jnp.asarray(np.stack(band), BF16),
            jnp.asarray(np.stack(causal), F32))


def _attention(q, ks, vs, kw, vw, kc, vc, gates, batch, seq):
    expand_t, overlap_t, n_cmp, n_slc = _attn_tables(seq)
    onehot, band, causal = _attn_mask_tables()
    tq = ATT_TQ
    nq = seq // tq
    rows = NSA_HEADS * tq
    tile = lambda w: pl.BlockSpec((tq, w), lambda b, i: (b * nq + i, 0))
    per_b = lambda r: pl.BlockSpec((r, LANES), lambda b, i: (b, 0))
    f32_rows = pltpu.VMEM((rows, LANES), F32)
    return pl.pallas_call(
        functools.partial(_attn_kernel, n_cmp=n_cmp, n_slc=n_slc, top_k=min(SLC_TOPK, n_slc)),
        grid=(batch, nq),
        in_specs=[tile(2 * NSA_WIDTH), per_b(seq), per_b(seq), per_b(seq), per_b(seq),
                  per_b(LANES), per_b(LANES), tile(LANES),
                  pl.BlockSpec(expand_t.shape, lambda b, i: (0, 0)),
                  pl.BlockSpec(overlap_t.shape, lambda b, i: (0, 0)),
                  pl.BlockSpec(onehot.shape, lambda b, i: (0, 0)),
                  pl.BlockSpec(band.shape, lambda b, i: (0, 0, 0)),
                  pl.BlockSpec(causal.shape, lambda b, i: (0, 0, 0))],
        out_specs=tile(NSA_WIDTH),
        out_shape=jax.ShapeDtypeStruct((batch * seq, NSA_WIDTH), BF16),
        scratch_shapes=[pltpu.VMEM((rows, 2 * LANES), BF16),
                        pltpu.VMEM((2, rows, ATT_TK), F32), pltpu.VMEM((2, rows, ATT_TK), BF16),
                        pltpu.VMEM((rows, ATT_WK), F32), pltpu.VMEM((rows, ATT_WK), BF16),
                        f32_rows, f32_rows, f32_rows, f32_rows, f32_rows],
        compiler_params=_cparams("parallel", "parallel"),
        name="nsa_attention",
    )(q, ks, vs, kw, vw, kc, vc, gates, expand_t, overlap_t, onehot, band, causal)
```

```python
import functools
import math

import jax
import jax.numpy as jnp
import numpy as np
from jax import lax
from jax.experimental import pallas as pl
from jax.experimental.pallas import tpu as pltpu
from jax.experimental.pallas import tpu_sc as plsc

F32 = jnp.float32
BF16 = jnp.bfloat16

D_MODEL = 1024
NSA_HEADS = 8
NSA_KV_HEADS = 2
NSA_GROUP = NSA_HEADS // NSA_KV_HEADS
HEAD_DIM = 64
ROPE_DIMS = HEAD_DIM // 4
ROPE_THETA = 500000.0
CMP_BLOCK = 32
CMP_STRIDE = 16
CMP_HIDDEN = 2 * HEAD_DIM
SLC_BLOCK = 64
SLC_TOPK = 16
WINDOW = 512
NSA_WIDTH = NSA_HEADS * HEAD_DIM
KV_WIDTH = NSA_KV_HEADS * HEAD_DIM
S5_WIDTH = 256
S5_GROUP_CH = 16
S5_GROUPS = S5_WIDTH // S5_GROUP_CH
S5_STATE = 64
S5_NSTATE = S5_GROUPS * S5_STATE
N_EXPERT_GROUPS = 4
EXPERTS_PER_GROUP = 8
N_EXPERTS = N_EXPERT_GROUPS * EXPERTS_PER_GROUP
EXPERT_TOP_K = 2
EXPERT_FF = 512
RMS_EPS = 1e-6
NEG_BIG = -1e30

LANES = 128
SUBLANES = 8
VMEM_LIMIT = 56 * 1024 * 1024

IN_TM = 512
ATT_TQ = 128
ATT_TK = 512
ATT_WK = WINDOW + ATT_TQ
ATT_RB = 128
S5_T = 256
MRG_TM = 512
MOE_BLK = 512
DSP_TT = 1024

C_Q, C_KC, C_VC, C_KS, C_VS, C_KW, C_VW, C_U, C_G = 0, 512, 640, 768, 896, 1024, 1152, 1280, 1536
IN_W = 1664


def _cparams(*sem):
    return pltpu.CompilerParams(dimension_semantics=sem, vmem_limit_bytes=VMEM_LIMIT)


def _gelu(x):
    return 0.5 * x * (1.0 + jnp.tanh(math.sqrt(2.0 / math.pi) * (x + 0.044715 * (x * x * x))))


def _sigmoid(x):
    return 1.0 / (1.0 + jnp.exp(-x))


def _dot(a, b):
    return jnp.dot(a, b, preferred_element_type=F32)


def _dot_t(a, b):
    return lax.dot_general(a, b, (((1,), (1,)), ((), ())), preferred_element_type=F32)


def _split_bf16(x):
    hi = x.astype(BF16)
    lo = (x - hi.astype(F32)).astype(BF16)
    return hi, lo


def _pack_rows(x):
    k = x.shape[1] // 2
    hi = lax.bitcast_convert_type(x[:, :k].astype(jnp.bfloat16).astype(F32), jnp.int32)
    lo = lax.bitcast_convert_type(x[:, k:].astype(jnp.bfloat16).astype(F32), jnp.int32)
    return hi | lax.shift_right_logical(lo, 16)


def _unpack_rows(w):
    a = lax.bitcast_convert_type(w & jnp.int32(-65536), F32)
    b = lax.bitcast_convert_type(lax.shift_left(w, 16), F32)
    return a, b


def _rope_kernel(pos_ref, freq_ref, sign_ref, c_ref, s_ref):
    ang = pos_ref[...].astype(F32) * freq_ref[...]
    c_ref[...] = jnp.cos(ang)
    s_ref[...] = jnp.sin(ang) * sign_ref[...]


def _rope_tables(pos_col, inv_freq):
    n = pos_col.shape[0]
    lane = np.arange(LANES) % HEAD_DIM
    sel = np.zeros((LANES, ROPE_DIMS // 2), np.float32)
    for l in range(LANES):
        if lane[l] < ROPE_DIMS:
            sel[l, lane[l] % (ROPE_DIMS // 2)] = 1.0
    freq = (jnp.asarray(sel) * inv_freq[None, :]).sum(-1)[None, :]
    sign = np.where(lane < ROPE_DIMS // 2, -1.0, 1.0).astype(np.float32)[None, :]
    tm = 1024
    return pl.pallas_call(
        _rope_kernel,
        grid=(n // tm,),
        in_specs=[pl.BlockSpec((tm, 1), lambda i: (i, 0)),
                  pl.BlockSpec((1, LANES), lambda i: (0, 0)),
                  pl.BlockSpec((1, LANES), lambda i: (0, 0))],
        out_specs=[pl.BlockSpec((tm, LANES), lambda i: (i, 0))] * 2,
        out_shape=[jax.ShapeDtypeStruct((n, LANES), F32)] * 2,
        compiler_params=_cparams("parallel"),
        name="rope_tables",
    )(pos_col, freq, jnp.asarray(sign))


def _headnorm_rope(t, gain, c, s):
    lane = lax.broadcasted_iota(jnp.int32, t.shape, 1)
    lo = lane < HEAD_DIM
    sq = t * t
    s_lo = jnp.sum(jnp.where(lo, sq, 0.0), axis=-1, keepdims=True)
    s_hi = jnp.sum(jnp.where(lo, 0.0, sq), axis=-1, keepdims=True)
    ms = jnp.where(lo, s_lo, s_hi) * (1.0 / HEAD_DIM)
    tn = (t * lax.rsqrt(ms + RMS_EPS)) * gain
    half = ROPE_DIMS // 2
    partner = jnp.where((lane & (HEAD_DIM - 1)) < half,
                        pltpu.roll(tn, LANES - half, 1), pltpu.roll(tn, half, 1))
    return tn * c + partner * s


def _blend_rows(x1, w, g0, g1):
    half = x1.shape[1] // 2
    a0, b0 = _unpack_rows(g0)
    a1, b1 = _unpack_rows(g1)
    w0, w1 = w[:, 0:1], w[:, 1:2]
    return jnp.concatenate([x1[:, 0:half] + w0 * a0 + w1 * a1, x1[:, half:] + w0 * b0 + w1 * b1], axis=1)


def _in_proj_kernel(*refs, blend):
    if blend:
        x1_ref, wc_ref, g0_ref, g1_ref, *refs = refs
    else:
        x_in_ref, *refs = refs
    gain_ref, w_ref, c_ref, s_ref, qg_ref, kg_ref, *refs = refs
    q_ref, ks_ref, kw_ref, kc_ref, vc_ref, vs_ref, vw_ref, u_ref, g_ref, *refs = refs
    if blend:
        x_out_ref, raw_ref = refs
        x = _blend_rows(x1_ref[...], wc_ref[...].T, g0_ref[...], g1_ref[...])
        x_out_ref[...] = x
    else:
        (raw_ref,) = refs
        x = x_in_ref[...]
    ms = jnp.mean(x * x, axis=-1, keepdims=True)
    h = (x * lax.rsqrt(ms + RMS_EPS)) * gain_ref[...]
    z = _dot(h.astype(BF16), w_ref[...])
    c = c_ref[...]
    s = s_ref[...]
    lane = lax.broadcasted_iota(jnp.int32, c.shape, 1)
    lo = lane < HEAD_DIM
    scale = HEAD_DIM ** -0.5 * math.log2(math.e)
    for j in range(NSA_HEADS // 2):
        r = _headnorm_rope(z[:, C_Q + LANES * j:C_Q + LANES * (j + 1)], qg_ref[...], c, s) * scale
        rr = pltpu.roll(r, HEAD_DIM, 1)
        if j // (NSA_GROUP // 2) == 0:
            even, odd = jnp.where(lo, r, 0.0), jnp.where(lo, rr, 0.0)
        else:
            even, odd = jnp.where(lo, 0.0, rr), jnp.where(lo, 0.0, r)
        q_ref[:, 2 * LANES * j:2 * LANES * j + LANES] = even.astype(BF16)
        q_ref[:, 2 * LANES * j + LANES:2 * LANES * (j + 1)] = odd.astype(BF16)
    ks_ref[...] = _headnorm_rope(z[:, C_KS:C_KS + LANES], kg_ref[...], c, s).astype(BF16)
    kw_ref[...] = _headnorm_rope(z[:, C_KW:C_KW + LANES], kg_ref[...], c, s).astype(BF16)
    half = CMP_BLOCK // 2
    for col, out in ((C_KC, kc_ref), (C_VC, vc_ref)):
        raw_ref[...] = z[:, col:col + LANES]
        for l in range(half):
            piece = raw_ref[pl.ds(l, x.shape[0] // half, stride=half), :]
            out[:, LANES * l:LANES * (l + 1)] = piece.astype(BF16)
    vs_ref[...] = z[:, C_VS:C_VS + LANES].astype(BF16)
    vw_ref[...] = z[:, C_VW:C_VW + LANES].astype(BF16)
    u_ref[:, pl.ds(pl.program_id(1), 1), :] = z[:, C_U:C_U + S5_WIDTH][:, None, :]
    g_ref[...] = _sigmoid(z[:, C_G:C_G + LANES])


def _in_proj(x_in, gain, w1, ctab, stab, qg, kg, batch, seq):
    blend = isinstance(x_in, tuple)
    n = batch * seq
    tm = IN_TM
    nt = seq // tm
    row = lambda i, b: (b * nt + i, 0)
    const = lambda i, b: (0, 0)
    tok = lambda w: pl.BlockSpec((tm, w), row)
    half = CMP_BLOCK // 2
    grouped = pl.BlockSpec((tm // half, half * KV_WIDTH), row)
    rows_bf16 = lambda r, w: jax.ShapeDtypeStruct((r, w), BF16)
    if blend:
        x1, w_col, gathered = x_in
        width = gathered.shape[1]
        x_specs = [tok(D_MODEL), pl.BlockSpec((SUBLANES, tm), lambda i, b: (0, b * nt + i)), tok(width),
                   pl.BlockSpec((tm, width), lambda i, b: (n // tm + b * nt + i, 0))]
        x_args = (x1, w_col, gathered, gathered)
    else:
        x_specs, x_args = [tok(D_MODEL)], (x_in,)
    return pl.pallas_call(
        functools.partial(_in_proj_kernel, blend=blend),
        grid=(nt, batch),
        in_specs=x_specs + [pl.BlockSpec((1, D_MODEL), const), pl.BlockSpec((D_MODEL, IN_W), const),
                            tok(LANES), tok(LANES), pl.BlockSpec((1, LANES), const),
                            pl.BlockSpec((1, LANES), const)],
        out_specs=[tok(2 * NSA_WIDTH), tok(LANES), tok(LANES), grouped, grouped, tok(LANES), tok(LANES),
                   pl.BlockSpec((tm, batch, S5_WIDTH), lambda i, b: (i, 0, 0)), tok(LANES)]
        + ([tok(D_MODEL)] if blend else []),
        out_shape=[rows_bf16(n, 2 * NSA_WIDTH), rows_bf16(n, LANES), rows_bf16(n, LANES),
                   rows_bf16(n // half, half * KV_WIDTH), rows_bf16(n // half, half * KV_WIDTH),
                   rows_bf16(n, LANES), rows_bf16(n, LANES),
                   jax.ShapeDtypeStruct((seq, batch, S5_WIDTH), F32), jax.ShapeDtypeStruct((n, LANES), F32)]
        + ([jax.ShapeDtypeStruct((n, D_MODEL), F32)] if blend else []),
        scratch_shapes=[pltpu.VMEM((tm, LANES), F32)],
        compiler_params=_cparams("parallel", "arbitrary"),
        name="in_proj",
    )(*x_args, gain, w1, ctab, stab, qg, kg)


def _compress_mlp(h_ref, w_ref, p_ref, w2_ref):
    w = w_ref[...]
    pre = _dot(h_ref[...], w)
    phi, plo = _split_bf16(p_ref[...])
    pb = _dot(phi, w) + _dot(plo, w)
    rows = pre.shape[0]
    out = None
    for h in range(NSA_KV_HEADS):
        top = pre[:, LANES * h:LANES * (h + 1)]
        bot = pre[:, LANES * (2 + h):LANES * (3 + h)]
        bias = pb[0:1, LANES * h:LANES * (h + 1)] + pb[1:2, LANES * (2 + h):LANES * (3 + h)]
        a = _gelu(top + pltpu.roll(bot, rows - 1, 0) + bias)
        o = _dot(a.astype(BF16), w2_ref[h])
        out = o if out is None else out + o
    return out


def _compress_kernel(hk_ref, hv_ref, wk_ref, wv_ref, pk_ref, pv_ref, w2k_ref, w2v_ref,
                     kg_ref, c_ref, s_ref, kc_ref, vc_ref):
    k = _compress_mlp(hk_ref, wk_ref, pk_ref, w2k_ref)
    kc_ref[...] = _headnorm_rope(k, kg_ref[...], c_ref[...], s_ref[...]).astype(BF16)
    vc_ref[...] = _compress_mlp(hv_ref, wv_ref, pv_ref, w2v_ref).astype(BF16)


def _compress(hk, hv, wk, wv, pk, pv, w2k, w2v, kg, ccmp, scmp, batch):
    rows = hk.shape[0] // batch
    width = hk.shape[1]
    const2 = lambda b: (0, 0)
    const3 = lambda b: (0, 0, 0)
    rowb = lambda w: pl.BlockSpec((rows, w), lambda b: (b, 0))
    return pl.pallas_call(
        _compress_kernel,
        grid=(batch,),
        in_specs=[rowb(width), rowb(width),
                  pl.BlockSpec(wk.shape, const2), pl.BlockSpec(wv.shape, const2),
                  pl.BlockSpec(pk.shape, const2), pl.BlockSpec(pv.shape, const2),
                  pl.BlockSpec(w2k.shape, const3), pl.BlockSpec(w2v.shape, const3),
                  pl.BlockSpec((1, LANES), const2), rowb(LANES), rowb(LANES)],
        out_specs=[rowb(LANES), rowb(LANES)],
        out_shape=[jax.ShapeDtypeStruct((hk.shape[0], LANES), BF16)] * 2,
        compiler_params=_cparams("parallel"),
        name="compress_kv",
    )(hk, hv, wk, wv, pk, pv, w2k, w2v, kg, ccmp, scmp)


def _compress_weights(pos_emb, w1, w2):
    half = CMP_BLOCK // 2
    eye = jnp.eye(NSA_KV_HEADS, dtype=F32)
    w1r = w1.reshape(2, half, HEAD_DIM, CMP_HIDDEN).astype(BF16)
    zero = jnp.zeros_like(w1r[0])
    wexp = jnp.concatenate(
        [jnp.stack([w1r[s] if p == h else zero for p in range(NSA_KV_HEADS)], axis=1)
         .reshape(half * KV_WIDTH, CMP_HIDDEN) for s in range(2) for h in range(NSA_KV_HEADS)], axis=1)
    pexp = jnp.broadcast_to(pos_emb.reshape(2, half, 1, HEAD_DIM), (2, half, NSA_KV_HEADS, HEAD_DIM))
    pexp = jnp.pad(pexp.reshape(2, half * KV_WIDTH), ((0, SUBLANES - 2), (0, 0)))
    w2e = jnp.einsum("cd,ph->pchd", w2, eye).reshape(NSA_KV_HEADS, CMP_HIDDEN, KV_WIDTH)
    return wexp, pexp, w2e.astype(BF16)


def _attn_kernel(q_ref, ks_ref, vs_ref, kw_ref, vw_ref, kc_ref, vc_ref, g_ref, et_ref, ovt_ref, oh_ref, wb_ref,
                 cb_ref, o_ref, qc_ref, s_ref, p_ref, sw_ref, pw_ref, m_ref, al_ref, oc_ref, as_ref, aw_ref,
                 *, n_cmp, n_slc, top_k):
    tq, tk, wk, rb, nh = ATT_TQ, ATT_TK, ATT_WK, ATT_RB, NSA_HEADS
    rows, grp_rows = nh * tq, NSA_GROUP * tq
    qk_rows = 4 * tq
    q0 = pl.program_id(1) * tq
    for a in range(nh):
        qc_ref[a * tq:(a + 1) * tq, 0:LANES] = q_ref[:, LANES * a:LANES * (a + 1)]
    q8 = qc_ref[:, 0:LANES]
    lane = lax.broadcasted_iota(jnp.int32, (tq, LANES), 1)
    t_row = q0 + lax.broadcasted_iota(jnp.int32, (tq, LANES), 0)

    s_c = _dot_t(q8, kc_ref[...]).reshape(nh, tq, LANES)
    mask_c = ((lane * CMP_STRIDE + (CMP_BLOCK - 1)) <= t_row) & (lane < n_cmp)
    sm = jnp.where(mask_c[None], s_c, NEG_BIG)
    p = jnp.where(mask_c[None], jnp.exp2(sm - jnp.max(sm, axis=-1, keepdims=True)), 0.0)
    l = jnp.sum(p, axis=-1, keepdims=True)
    p_c = p * jnp.where(l > 0.0, 1.0 / l, 0.0)
    oc_ref[...] = _dot(p_c.reshape(rows, LANES).astype(BF16), vc_ref[...])

    blk = lax.broadcasted_iota(jnp.int32, (n_slc, tq), 0)
    cur = (q0 + lax.broadcasted_iota(jnp.int32, (n_slc, tq), 1)) >> int(math.log2(SLC_BLOCK))
    forced = (blk == 0) | (blk == cur) | (blk == cur - 1)
    for h in range(NSA_KV_HEADS):
        ps = p_c[NSA_GROUP * h]
        for g in range(1, NSA_GROUP):
            ps = ps + p_c[NSA_GROUP * h + g]
        hi, lo = _split_bf16(ps)
        imp = (_dot_t(ovt_ref[...], hi) + _dot_t(ovt_ref[...], lo))[0:n_slc]
        imp = jnp.where(forced, jnp.inf, jnp.where(blk > cur, -jnp.inf, imp))
        rank = jnp.zeros((n_slc, tq), F32)
        for i in range(n_slc):
            row = imp[i:i + 1, :]
            rank = rank + jnp.where((row > imp) | ((row == imp) & (blk > i)), 1.0, 0.0)
        bias_t = jnp.where(rank < float(top_k), 0.0, NEG_BIG)
        bias = jnp.concatenate([bias_t, jnp.zeros((LANES - n_slc, tq), F32)], axis=0).T.astype(BF16)
        for g in range(NSA_GROUP):
            a = NSA_GROUP * h + g
            qc_ref[a * tq:(a + 1) * tq, LANES:2 * LANES] = bias

    def softmax_pass(s_buf, p_buf, width, causal_case, online):
        for r in range(rows // rb):
            r0 = r * rb
            s = s_buf[pl.ds(r0, rb), 0:width]
            if causal_case is not None:
                s = s + cb_ref[causal_case, pl.ds(r0 % tq, rb), 0:width]
            m_new = jnp.broadcast_to(jnp.max(s, axis=1, keepdims=True), (rb, LANES))
            if online:
                m_old = m_ref[pl.ds(r0, rb), :]
                m_new = jnp.maximum(m_old, m_new)
                al_ref[pl.ds(r0, rb), :] = jnp.exp2(m_old - m_new)
                m_ref[pl.ds(r0, rb), :] = m_new
            pp = jnp.exp2(s - jnp.concatenate([m_new] * (width // LANES), axis=1))
            p_buf[pl.ds(r0, rb), 0:width] = pp.astype(BF16)

    def pv(acc_ref, p_buf, v, width, online):
        lane_v = lax.broadcasted_iota(jnp.int32, v.shape, 1)
        for h in range(NSA_KV_HEADS):
            rs = slice(h * grp_rows, (h + 1) * grp_rows)
            own = (lane_v < HEAD_DIM) if h == 0 else (lane_v >= HEAD_DIM)
            upd = _dot(p_buf[rs, 0:width], jnp.where(own, v, jnp.ones_like(v)))
            acc_ref[rs, :] = (al_ref[rs, :] * acc_ref[rs, :] + upd) if online else upd

    m_ref[...] = jnp.full(m_ref.shape, 0.1 * NEG_BIG, F32)
    as_ref[...] = jnp.zeros(as_ref.shape, F32)

    def slc_chunk(c, diagonal, width=tk):
        k0 = c * tk
        s_buf, p_buf = s_ref.at[c % 2], p_ref.at[c % 2]
        kcat = jnp.concatenate([ks_ref[pl.ds(k0, width), :], et_ref[pl.ds(k0, width), :]], axis=1)
        for g in range(rows // qk_rows):
            rs = slice(g * qk_rows, (g + 1) * qk_rows)
            s_buf[rs, 0:width] = _dot_t(qc_ref[rs, :], kcat)
        softmax_pass(s_buf, p_buf, width, ((q0 - k0) >> int(math.log2(tq))) if diagonal else None, True)
        pv(as_ref, p_buf, vs_ref[pl.ds(k0, width), :], width, True)

    def window_and_output():
        w0 = pl.multiple_of(jnp.maximum(q0 - WINDOW, 0), LANES)
        w_case = jnp.minimum(q0 >> int(math.log2(tq)), WINDOW // tq)
        kcat_w = jnp.concatenate([kw_ref[pl.ds(w0, wk), :], wb_ref[w_case]], axis=1)
        for g in range(rows // qk_rows):
            rs = slice(g * qk_rows, (g + 1) * qk_rows)
            sw_ref[rs, :] = _dot_t(jnp.concatenate([qc_ref[rs, 0:LANES], oh_ref[rs, :]], axis=1), kcat_w)
        softmax_pass(sw_ref, pw_ref, wk, None, False)
        pv(aw_ref, pw_ref, vw_ref[pl.ds(w0, wk), :], wk, False)

        lo_half = lane < HEAD_DIM
        for j in range(nh // 2):
            comb = []
            for a in (2 * j, 2 * j + 1):
                rs = slice(a * tq, (a + 1) * tq)
                o_s, o_w = as_ref[rs, :], aw_ref[rs, :]
                comb.append(g_ref[:, 3 * a:3 * a + 1] * oc_ref[rs, :]
                            + g_ref[:, 3 * a + 1:3 * a + 2] * (o_s * (1.0 / pltpu.roll(o_s, HEAD_DIM, 1)))
                            + g_ref[:, 3 * a + 2:3 * a + 3] * (o_w * (1.0 / pltpu.roll(o_w, HEAD_DIM, 1))))
            even, odd = comb
            if j // (NSA_GROUP // 2) == 0:
                pair = jnp.where(lo_half, even, pltpu.roll(odd, HEAD_DIM, 1))
            else:
                pair = jnp.where(lo_half, pltpu.roll(even, HEAD_DIM, 1), odd)
            o_ref[:, LANES * j:LANES * (j + 1)] = pair.astype(BF16)

    c_last = q0 >> int(math.log2(tk))
    early = ((q0 >> int(math.log2(tq))) & (tk // tq - 1)) < tk // tq // 2
    for n_full in range(ks_ref.shape[0] // tk):
        for first_half in (True, False):
            @pl.when((c_last == n_full) & (early if first_half else jnp.logical_not(early)))
            def _():
                for c in range(n_full):
                    slc_chunk(c, False)
                slc_chunk(n_full, True, tk // 2 if first_half else tk)
                window_and_output()


def _pack_w_in(w):
    sizes = (NSA_WIDTH + 6 * KV_WIDTH, 3 * NSA_HEADS, S5_WIDTH, D_MODEL, D_MODEL)
    o = np.cumsum((0,) + sizes)
    qkv, g, u, ga, gs = [w[:, o[i]:o[i + 1]] for i in range(len(sizes))]
    g = jnp.pad(g, ((0, 0), (0, LANES - 3 * NSA_HEADS)))
    w1 = jnp.concatenate([qkv, u, g], axis=1).astype(BF16)
    return w1, ga.astype(BF16), gs.astype(BF16)


def _nsa_branch(x_in, gain, w1, ctab, stab, ccmp, scmp, q_gain, k_gain, cmp_k, cmp_v, batch, seq):
    qg = jnp.tile(q_gain, 2)[None, :]
    kg = jnp.tile(k_gain, 2)[None, :]
    q, ks, kw, hk, hv, vs, vw, u_tm, gates, *x_out = _in_proj(x_in, gain, w1, ctab, stab, qg, kg, batch, seq)
    wk, pk, w2k = _compress_weights(*cmp_k)
    wv, pv, w2v = _compress_weights(*cmp_v)
    kc, vc = _compress(hk, hv, wk, wv, pk, pv, w2k, w2v, kg, ccmp, scmp, batch)
    o_a = _attention(q, ks, vs, kw, vw, kc, vc, gates, batch, seq)
    return o_a, u_tm, (x_out[0] if x_out else x_in)


def _cmp_rope_tables(ctab, stab, batch, seq):
    def pick(t):
        t = t.reshape(batch, seq, LANES)[:, CMP_BLOCK - 1::CMP_STRIDE]
        t = jnp.pad(t, ((0, 0), (0, seq // CMP_STRIDE - t.shape[1]), (0, 0)))
        return t.reshape(batch * (seq // CMP_STRIDE), LANES)
    return pick(ctab), pick(stab)


def _s5_disc_kernel(are_ref, aim_ref, ldt_ref, lre_ref, lim_ref, fre_ref, fim_ref):
    lam_re = jnp.minimum(are_ref[...], -1e-4)
    lam_im = aim_ref[...]
    dt = jnp.exp(ldt_ref[...])
    mag = jnp.exp(lam_re * dt)
    ang = lam_im * dt
    lb_re = mag * jnp.cos(ang)
    lb_im = mag * jnp.sin(ang)
    den = lam_re * lam_re + lam_im * lam_im
    lre_ref[...] = lb_re
    lim_ref[...] = lb_im
    fre_ref[...] = ((lb_re - 1.0) * lam_re + lb_im * lam_im) / den
    fim_ref[...] = (lb_im * lam_re - (lb_re - 1.0) * lam_im) / den


def _s5_weights(a_re, a_im, log_dt, b_re, b_im, c_re, c_im):
    shp = jax.ShapeDtypeStruct((S5_GROUPS, S5_STATE), F32)
    lb_re, lb_im, f_re, f_im = pl.pallas_call(
        _s5_disc_kernel, out_shape=[shp] * 4, name="s5_discretise",
    )(a_re, a_im, log_dt[:, None])
    bb_re = f_re[..., None] * b_re - f_im[..., None] * b_im
    bb_im = f_re[..., None] * b_im + f_im[..., None] * b_re
    eye = jnp.eye(S5_GROUPS, dtype=F32)

    def in_map(bb):
        return jnp.einsum("gpc,gh->gchp", bb, eye).reshape(S5_WIDTH, S5_NSTATE)

    def out_map(cc):
        return jnp.einsum("gcp,gh->hpgc", cc, eye).reshape(S5_NSTATE, S5_WIDTH)

    w_b = jnp.concatenate([in_map(bb_re), in_map(bb_im)], axis=1).astype(BF16)
    w_c = jnp.concatenate([out_map(c_re), -out_map(c_im)], axis=0).astype(BF16)
    return lb_re.reshape(1, S5_NSTATE), lb_im.reshape(1, S5_NSTATE), w_b, w_c


def _s5_kernel(u_ref, wb_ref, lre_ref, lim_ref, wc_ref, d_ref, wg_ref, o_ref, x_ref, st_ref, *, batch):
    ns = S5_NSTATE

    @pl.when(pl.program_id(0) == 0)
    def _():
        st_ref[...] = jnp.zeros(st_ref.shape, F32)

    u = u_ref[...].reshape(u_ref.shape[0] * batch, S5_WIDTH)
    x_ref[...] = _dot(u.astype(BF16), wb_ref[...])
    lre = jnp.broadcast_to(lre_ref[...], (batch, ns))
    lim = jnp.broadcast_to(lim_ref[...], (batch, ns))

    def step(t, carry):
        xr, xi = carry
        r0 = pl.multiple_of(t * batch, batch)
        nr = lre * xr - lim * xi + x_ref[pl.ds(r0, batch), 0:ns]
        ni = lre * xi + lim * xr + x_ref[pl.ds(r0, batch), ns:2 * ns]
        x_ref[pl.ds(r0, batch), 0:ns] = nr
        x_ref[pl.ds(r0, batch), ns:2 * ns] = ni
        return nr, ni

    steps = u.shape[0] // batch
    xr, xi = lax.fori_loop(0, steps, step, (st_ref[:, 0:ns], st_ref[:, ns:2 * ns]), unroll=4)
    st_ref[:, 0:ns] = xr
    st_ref[:, ns:2 * ns] = xi
    y = _dot(x_ref[...].astype(BF16), wc_ref[...]) + d_ref[...] * u
    z = _dot(_gelu(y).astype(BF16), wg_ref[...])
    o = z[:, :S5_WIDTH] * _sigmoid(z[:, S5_WIDTH:])
    for c in range(S5_WIDTH // LANES):
        o_ref[c] = o[:, c * LANES:(c + 1) * LANES]


def _s5(u_tm, lre, lim, w_b, w_c, d_skip, w_glu, batch, seq):
    rows = S5_T * batch
    const = lambda t: (0, 0)
    return pl.pallas_call(
        functools.partial(_s5_kernel, batch=batch),
        grid=(seq // S5_T,),
        in_specs=[pl.BlockSpec((S5_T, batch, S5_WIDTH), lambda t: (t, 0, 0)),
                  pl.BlockSpec(w_b.shape, const), pl.BlockSpec(lre.shape, const), pl.BlockSpec(lim.shape, const),
                  pl.BlockSpec(w_c.shape, const), pl.BlockSpec((1, S5_WIDTH), const),
                  pl.BlockSpec(w_glu.shape, const)],
        out_specs=pl.BlockSpec((S5_WIDTH // LANES, rows, LANES), lambda t: (0, t, 0)),
        out_shape=jax.ShapeDtypeStruct((S5_WIDTH // LANES, seq * batch, LANES), F32),
        scratch_shapes=[pltpu.VMEM((rows, 2 * S5_NSTATE), F32), pltpu.VMEM((batch, 2 * S5_NSTATE), F32)],
        compiler_params=_cparams("arbitrary"),
        name="s5_scan",
    )(u_tm, w_b, lre, lim, w_c, d_skip, w_glu)


RT_GROUP_ROW = 0
RT_EXPERT_ROW = SUBLANES


def _rmsnorm(x, gain):
    return (x * lax.rsqrt(jnp.mean(x * x, axis=-1, keepdims=True) + RMS_EPS)) * gain


def _merge_kernel(x_ref, g1_ref, wga_ref, wgs_ref, oa_ref, ob_ref, pa_ref, ps_ref, wo_ref, g2_ref,
                  wrh_ref, wrl_ref, rb_ref, tri_ref, x1_ref, h2_ref, e_ref, w_ref, r_ref, cnt_ref,
                  lg_ref, carry_ref):
    x = x_ref[...]
    h = _rmsnorm(x, g1_ref[...]).astype(BF16)
    gate_a = _sigmoid(_dot(h, wga_ref[...]))
    gate_s = _sigmoid(_dot(h, wgs_ref[...]))
    rows_b = pl.ds(pl.program_id(1), x.shape[0], stride=ob_ref.shape[1] // x.shape[0])
    o_b = jnp.concatenate([ob_ref[c, rows_b, :] for c in range(S5_WIDTH // LANES)], axis=1)
    merged = gate_a * _dot(oa_ref[...], pa_ref[...]) + gate_s * _dot(o_b.astype(BF16), ps_ref[...])
    x1 = x + _dot(merged.astype(BF16), wo_ref[...])
    x1_ref[...] = x1
    h2 = _rmsnorm(x1, g2_ref[...])
    h2_ref[...] = _pack_rows(h2)
    hi, lo = _split_bf16(h2)
    lg_ref[...] = (_dot_t(wrh_ref[...], hi) + _dot_t(wrh_ref[...], lo) + _dot_t(wrl_ref[...], hi)
                   + rb_ref[...])
    _route_tile(lg_ref, tri_ref, e_ref, w_ref, r_ref, cnt_ref, carry_ref,
                (pl.program_id(0) == 0) & (pl.program_id(1) == 0))


def _router_weights(w_rg, b_rg, w_re, b_re):
    def rows(group_part, expert_part):
        return jnp.concatenate([
            group_part, jnp.zeros((RT_EXPERT_ROW - N_EXPERT_GROUPS,) + group_part.shape[1:], F32),
            expert_part, jnp.zeros((LANES - RT_EXPERT_ROW - N_EXPERTS,) + group_part.shape[1:], F32)], axis=0)

    hi, lo = _split_bf16(rows(w_rg.T, w_re.T))
    return hi, lo, rows(b_rg[:, None], b_re[:, None])


def _merge(x2, g1, w_ga, w_gs, o_a, ob_tm, p_a, p_s, w_o, g2, wrh, wrl, rb, batch, seq):
    n = x2.shape[0]
    tm = MRG_TM
    nt = seq // tm
    row = lambda i, b: (b * nt + i, 0)
    const = lambda i, b: (0, 0)
    tok = lambda w: pl.BlockSpec((tm, w), row)
    full = lambda a: pl.BlockSpec(a.shape, const)
    tri = jnp.asarray(np.triu(np.ones((tm, tm), np.float32), 1), BF16)
    lanes_tok = pl.BlockSpec((SUBLANES, tm), lambda i, b: (0, b * nt + i))
    return pl.pallas_call(
        _merge_kernel,
        grid=(nt, batch),
        in_specs=[tok(D_MODEL), full(g1), full(w_ga), full(w_gs), tok(NSA_WIDTH),
                  pl.BlockSpec((S5_WIDTH // LANES, tm * batch, LANES), lambda i, b: (0, i, 0)),
                  full(p_a), full(p_s), full(w_o),
                  full(g2), full(wrh), full(wrl), full(rb), full(tri)],
        out_specs=[tok(D_MODEL), tok(D_MODEL // 2), lanes_tok, lanes_tok, lanes_tok,
                   pl.BlockSpec((N_EXPERTS, LANES), const)],
        out_shape=[jax.ShapeDtypeStruct((n, D_MODEL), F32), jax.ShapeDtypeStruct((n, D_MODEL // 2), jnp.int32),
                   jax.ShapeDtypeStruct((SUBLANES, n), jnp.int32), jax.ShapeDtypeStruct((SUBLANES, n), F32),
                   jax.ShapeDtypeStruct((SUBLANES, n), jnp.int32), jax.ShapeDtypeStruct((N_EXPERTS, LANES), F32)],
        scratch_shapes=[pltpu.VMEM((LANES, tm), F32), pltpu.VMEM((N_EXPERTS, LANES), F32)],
        compiler_params=_cparams("arbitrary", "arbitrary"),
        name="merge_out_proj",
    )(x2, g1, w_ga, w_gs, o_a, ob_tm, p_a, p_s, w_o, g2, wrh, wrl, rb, tri)


def _first_argmax(v, rows):
    vmax = jnp.max(v, axis=0, keepdims=True)
    idx = jnp.min(jnp.where(v == vmax, rows, SUBLANES), axis=0, keepdims=True)
    return vmax, idx


def _route_tile(lg_ref, tri_ref, e_ref, w_ref, r_ref, cnt_ref, carry_ref, first):
    tt = lg_ref.shape[1]

    @pl.when(first)
    def _():
        carry_ref[...] = jnp.zeros(carry_ref.shape, F32)

    rows = lax.broadcasted_iota(jnp.int32, (SUBLANES, tt), 0)
    gl = jnp.where(rows < N_EXPERT_GROUPS, lg_ref[RT_GROUP_ROW:RT_GROUP_ROW + SUBLANES, :], -jnp.inf)
    gmax, g_sel = _first_argmax(gl, rows)
    g_w = 1.0 / jnp.sum(jnp.exp(gl - gmax), axis=0, keepdims=True)

    el = jnp.zeros((EXPERTS_PER_GROUP, tt), F32)
    for g in range(N_EXPERT_GROUPS):
        r0 = RT_EXPERT_ROW + EXPERTS_PER_GROUP * g
        el = jnp.where(g_sel == g, lg_ref[r0:r0 + EXPERTS_PER_GROUP, :], el)
    ex = jnp.exp(el - jnp.max(el, axis=0, keepdims=True))
    p = ex / jnp.sum(ex, axis=0, keepdims=True)
    p1, i1 = _first_argmax(p, rows)
    p2, i2 = _first_argmax(jnp.where(rows == i1, -1.0, p), rows)
    e1 = g_sel * EXPERTS_PER_GROUP + i1
    e2 = g_sel * EXPERTS_PER_GROUP + i2
    scale = g_w / (p1 + p2)

    erow = lax.broadcasted_iota(jnp.int32, (N_EXPERTS, tt), 0)
    oh1 = jnp.where(erow == e1, 1.0, 0.0)
    oh2 = jnp.where(erow == e2, 1.0, 0.0)
    both = oh1 + oh2
    before = carry_ref[:, 0:1] + _dot(both.astype(BF16), tri_ref[...])
    carry_ref[...] = carry_ref[...] + jnp.sum(both, axis=1, keepdims=True)
    cnt_ref[...] = carry_ref[...]

    zi = jnp.zeros((SUBLANES - 2, tt), jnp.int32)
    e_ref[...] = jnp.concatenate([e1, e2, zi], axis=0)
    w_ref[...] = jnp.concatenate([p1 * scale, p2 * scale, zi.astype(F32)], axis=0)
    r_ref[...] = jnp.concatenate([jnp.sum(oh1 * before, axis=0, keepdims=True),
                                  jnp.sum(oh2 * before, axis=0, keepdims=True),
                                  zi.astype(F32)], axis=0).astype(jnp.int32)


def _expert_kernel(be_ref, nb_ref, x_ref, wg_ref, wu_ref, wd_ref, o_ref, wgb_ref, wub_ref, wdb_ref):
    i = pl.program_id(0)
    half = D_MODEL // 2

    def mlp(wg, wu, wd):
        xa, xb = _unpack_rows(x_ref[...])
        xa, xb = xa.astype(BF16), xb.astype(BF16)
        top, bottom, every = slice(0, half), slice(half, D_MODEL), slice(None)
        out = None
        for c in range(2):
            cs = slice(c * (EXPERT_FF // 2), (c + 1) * (EXPERT_FF // 2))
            g = _dot(xa, wg(top, cs)) + _dot(xb, wg(bottom, cs))
            u = _dot(xa, wu(top, cs)) + _dot(xb, wu(bottom, cs))
            part = _dot(((g * _sigmoid(g)) * u).astype(BF16), wd(cs, every))
            out = part if out is None else out + part
        o_ref[...] = _pack_rows(out)

    def cast_piece(src_ref, dst_ref):
        def piece(rs, cs):
            w = src_ref[rs, cs].astype(BF16)
            dst_ref[rs, cs] = w
            return w
        return piece

    def kept_piece(ref):
        return lambda rs, cs: ref[rs, cs]

    new_expert = (i == 0) | (be_ref[i] != be_ref[jnp.maximum(i - 1, 0)])

    @pl.when((i < nb_ref[0]) & new_expert)
    def _():
        mlp(cast_piece(wg_ref, wgb_ref), cast_piece(wu_ref, wub_ref), cast_piece(wd_ref, wdb_ref))

    @pl.when((i < nb_ref[0]) & jnp.logical_not(new_expert))
    def _():
        mlp(kept_piece(wgb_ref), kept_piece(wub_ref), kept_piece(wdb_ref))

    @pl.when(i >= nb_ref[0])
    def _():
        o_ref[...] = jnp.zeros(o_ref.shape, o_ref.dtype)


def _experts(blk_expert, n_used, buf, w_gate, w_up, w_down, layer):
    cap, width = buf.shape
    blk = MOE_BLK
    row = lambda i, be, nb: (jnp.minimum(i, nb[0] - 1), 0)
    wmap = lambda i, be, nb: (layer, be[jnp.minimum(i, nb[0] - 1)], 0, 0)
    return pl.pallas_call(
        _expert_kernel,
        grid_spec=pltpu.PrefetchScalarGridSpec(
            num_scalar_prefetch=2,
            grid=(cap // blk,),
            in_specs=[pl.BlockSpec((blk, width), row),
                      pl.BlockSpec((None, None, D_MODEL, EXPERT_FF), wmap),
                      pl.BlockSpec((None, None, D_MODEL, EXPERT_FF), wmap),
                      pl.BlockSpec((None, None, EXPERT_FF, D_MODEL), wmap)],
            out_specs=pl.BlockSpec((blk, width), lambda i, be, nb: (i, 0)),
            scratch_shapes=[pltpu.VMEM((D_MODEL, EXPERT_FF), BF16), pltpu.VMEM((D_MODEL, EXPERT_FF), BF16),
                            pltpu.VMEM((EXPERT_FF, D_MODEL), BF16)],
        ),
        out_shape=jax.ShapeDtypeStruct((cap, width), buf.dtype),
        compiler_params=_cparams("arbitrary"),
        name="moe_experts",
    )(blk_expert, n_used, buf, w_gate, w_up, w_down)


SC_CH = 64


def _sc_mesh():
    return plsc.VectorSubcoreMesh(core_axis_name="c", subcore_axis_name="s")


def _sc_worker(info):
    return lax.axis_index("s") * info.num_cores + lax.axis_index("c")


def _sc_gather_rows(table, idx, rows=None):
    info = plsc.get_sparse_core_info()
    lanes, workers = info.num_lanes, info.num_cores * info.num_subcores
    invert = rows is not None
    rows = rows if invert else idx.shape[0]
    width = table.shape[1]
    per_w = rows // workers
    n_chunks = per_w // SC_CH
    assert per_w % (2 * SC_CH) == 0
    scan = 2048
    n_tok = table.shape[0]
    assert not invert or (n_tok & (n_tok - 1) == 0 and idx.shape[0] % scan == 0)

    def body(table_hbm, idx_hbm, out_hbm, idx_v, rows_a, rows_b, sem_a, sem_b, *scan_v):
        base = _sc_worker(info) * per_w
        if invert:
            lane = lax.iota(jnp.int32, lanes)

            @pl.loop(0, per_w // lanes)
            def _(j):
                idx_v[pl.ds(j * lanes, lanes)] = (base + j * lanes + lane) & (n_tok - 1)

            @pl.loop(0, idx.shape[0] // scan)
            def _(c):
                pltpu.sync_copy(idx_hbm.at[pl.ds(c * scan, scan)], scan_v[0])

                @plsc.parallel_loop(0, scan // lanes, unroll=8)
                def _(j):
                    rel = scan_v[0][pl.ds(j * lanes, lanes)] - base
                    mine = (rel >= 0) & (rel < per_w)
                    token = (c * scan + j * lanes + lane) & (n_tok - 1)
                    plsc.store_scatter(idx_v, [jnp.where(mine, rel, 0)], token, mask=mine)
        else:
            pltpu.sync_copy(idx_hbm.at[pl.ds(base, per_w)], idx_v)

        def gather(j, buf, sem):
            return pltpu.make_async_copy(table_hbm.at[idx_v.at[pl.ds(j * SC_CH, SC_CH)]], buf, sem)

        def write(j, buf):
            pltpu.sync_copy(buf, out_hbm.at[pl.ds(base + j * SC_CH, SC_CH)])

        gather(0, rows_a, sem_a).start()

        @pl.loop(0, n_chunks, step=2)
        def _(j):
            gather(j + 1, rows_b, sem_b).start()
            gather(j, rows_a, sem_a).wait()
            write(j, rows_a)

            @pl.when(j + 2 < n_chunks)
            def _():
                gather(j + 2, rows_a, sem_a).start()

            gather(j + 1, rows_b, sem_b).wait()
            write(j + 1, rows_b)

    chunk_buf = pltpu.VMEM((SC_CH, width), table.dtype)
    scratch = [pltpu.VMEM((per_w,), jnp.int32), chunk_buf, chunk_buf,
               pltpu.SemaphoreType.DMA, pltpu.SemaphoreType.DMA]
    return pl.kernel(
        body, out_type=jax.ShapeDtypeStruct((rows, width), table.dtype), mesh=_sc_mesh(),
        scratch_types=scratch + ([pltpu.VMEM((scan,), jnp.int32)] if invert else []),
        compiler_params=pltpu.CompilerParams(needs_layout_passes=False) if invert else None,
        name="sc_dispatch_rows" if invert else "sc_gather_rows",
    )(table, idx)


def _blend_kernel(x1_ref, w_ref, g0_ref, g1_ref, o_ref):
    o_ref[...] = _blend_rows(x1_ref[...], w_ref[...].T, g0_ref[...], g1_ref[...])


def _blend(x1, w_col, gathered):
    n = x1.shape[0]
    tt = DSP_TT
    nt = n // tt
    width = gathered.shape[1]
    return pl.pallas_call(
        _blend_kernel,
        grid=(nt,),
        in_specs=[pl.BlockSpec((tt, D_MODEL), lambda i: (i, 0)),
                  pl.BlockSpec((SUBLANES, tt), lambda i: (0, i)),
                  pl.BlockSpec((tt, width), lambda i: (i, 0)),
                  pl.BlockSpec((tt, width), lambda i: (nt + i, 0))],
        out_specs=pl.BlockSpec((tt, D_MODEL), lambda i: (i, 0)),
        out_shape=jax.ShapeDtypeStruct((n, D_MODEL), F32),
        compiler_params=_cparams("parallel"),
        name="moe_blend",
    )(x1, w_col, gathered, gathered)


def _moe(x1, h2, routing, w_gate, w_up, w_down, layer):
    n = x1.shape[0]
    e_idx, wts, rank, counts = routing
    counts = counts[:, 0].astype(jnp.int32)
    padded = (counts + MOE_BLK - 1) // MOE_BLK * MOE_BLK
    pends = jnp.cumsum(padded)
    pstarts = pends - padded
    experts = jnp.arange(N_EXPERTS, dtype=jnp.int32)
    start_of = jnp.sum(jnp.where(e_idx[None, :EXPERT_TOP_K] == experts[:, None, None],
                                 pstarts[:, None, None], 0), axis=0)
    dest = start_of + rank[:EXPERT_TOP_K]
    cap = n * EXPERT_TOP_K + N_EXPERTS * MOE_BLK
    n_blk = cap // MOE_BLK
    blk_row = jnp.arange(n_blk, dtype=jnp.int32) * MOE_BLK
    blk_expert = jnp.minimum(jnp.sum((pends[None, :] <= blk_row[:, None]).astype(jnp.int32), axis=1),
                             N_EXPERTS - 1)
    n_used = (pends[-1:] // MOE_BLK).astype(jnp.int32)
    dest_flat = dest.reshape(EXPERT_TOP_K * n)
    buf = _sc_gather_rows(h2, dest_flat, rows=cap)
    out_buf = _experts(blk_expert, n_used, buf, w_gate, w_up, w_down, layer)
    return x1, wts, _sc_gather_rows(out_buf, dest_flat)


def kernel(x, positions, norm_mix, norm_ffn, w_in, q_gain, k_gain, cmp_k_pos, cmp_k_w1, cmp_k_w2, cmp_v_pos,
           cmp_v_w1, cmp_v_w2, s5_a_re, s5_a_im, s5_log_dt, s5_b_re, s5_b_im, s5_c_re, s5_c_im, s5_d, s5_glu,
           proj_nsa, proj_s5, w_out, router_group, router_group_bias, router_expert, router_expert_bias,
           expert_gate, expert_up, expert_down):
    batch, seq, _ = x.shape
    depth = w_in.shape[0]
    n = batch * seq
    inv_freq = ROPE_THETA ** (-jnp.arange(0, ROPE_DIMS, 2, dtype=jnp.float32) / ROPE_DIMS)
    ctab, stab = _rope_tables(positions.reshape(n, 1), inv_freq)
    ccmp, scmp = _cmp_rope_tables(ctab, stab, batch, seq)
    x_in = x.reshape(n, D_MODEL)
    for l in range(depth):
        w1, w_ga, w_gs = _pack_w_in(w_in[l])
        o_a, u_tm, x2 = _nsa_branch(x_in, norm_mix[l][None], w1, ctab, stab, ccmp, scmp, q_gain[l], k_gain[l],
                                    (cmp_k_pos[l], cmp_k_w1[l], cmp_k_w2[l]),
                                    (cmp_v_pos[l], cmp_v_w1[l], cmp_v_w2[l]), batch, seq)
        lre, lim, w_b, w_c = _s5_weights(s5_a_re[l], s5_a_im[l], s5_log_dt[l], s5_b_re[l], s5_b_im[l],
                                         s5_c_re[l], s5_c_im[l])
        ob_tm = _s5(u_tm, lre, lim, w_b, w_c, s5_d[l][None],
                    s5_glu[l].astype(BF16), batch, seq)
        wrh, wrl, rb = _router_weights(router_group[l], router_group_bias[l], router_expert[l],
                                       router_expert_bias[l])
        x1, h2, *routing = _merge(x2, norm_mix[l][None], w_ga, w_gs, o_a, ob_tm, proj_nsa[l].astype(BF16),
                                  proj_s5[l].astype(BF16), w_out[l].astype(BF16), norm_ffn[l][None],
                                  wrh, wrl, rb, batch, seq)
        x_in = _moe(x1, h2, routing, expert_gate, expert_up, expert_down, l)
    return _blend(*x_in).reshape(batch, seq, D_MODEL)


def _attn_tables(seq):
    n_cmp = (seq - CMP_BLOCK) // CMP_STRIDE + 1
    n_slc = seq // SLC_BLOCK
    key_blk = np.arange(seq) // SLC_BLOCK
    expand_t = (key_blk[:, None] == np.arange(LANES)[None, :]).astype(np.float32)
    cmp_start = np.arange(LANES) * CMP_STRIDE
    slc_start = np.arange(LANES) * SLC_BLOCK
    overlap_t = ((cmp_start[None, :] < slc_start[:, None] + SLC_BLOCK)
                 & (cmp_start[None, :] + CMP_BLOCK > slc_start[:, None])
                 & (np.arange(LANES)[None, :] < n_cmp) & (np.arange(LANES)[:, None] < n_slc))
    return (jnp.asarray(expand_t, BF16), jnp.asarray(overlap_t.astype(np.float32), BF16), n_cmp, n_slc)


def _attn_mask_tables():
    tq, tk, wk = ATT_TQ, ATT_TK, ATT_WK
    ql = np.arange(tq)
    onehot = np.tile(np.eye(tq, dtype=np.float32), (NSA_HEADS, 1))
    k = np.arange(wk)[:, None]
    band = []
    for c in range(WINDOW // tq + 1):
        if c < WINDOW // tq:
            keep = k <= c * tq + ql[None, :]
        else:
            keep = (k > ql[None, :]) & (k <= ql[None, :] + WINDOW)
        band.append(np.where(keep, 0.0, NEG_BIG))
    kl = np.arange(tk)[None, :]
    causal = [np.where(kl <= c * tq + ql[:, None], 0.0, NEG_BIG) for c in range(tk // tq)]
    return (jnp.asarray(onehot, BF16), jnp.asarray(np.stack(band), BF16),
            jnp.asarray(np.stack(causal), F32))


def _attention(q, ks, vs, kw, vw, kc, vc, gates, batch, seq):
    expand_t, overlap_t, n_cmp, n_slc = _attn_tables(seq)
    onehot, band, causal = _attn_mask_tables()
    tq = ATT_TQ
    nq = seq // tq
    rows = NSA_HEADS * tq
    tile = lambda w: pl.BlockSpec((tq, w), lambda b, i: (b * nq + i, 0))
    per_b = lambda r: pl.BlockSpec((r, LANES), lambda b, i: (b, 0))
    f32_rows = pltpu.VMEM((rows, LANES), F32)
    return pl.pallas_call(
        functools.partial(_attn_kernel, n_cmp=n_cmp, n_slc=n_slc, top_k=min(SLC_TOPK, n_slc)),
        grid=(batch, nq),
        in_specs=[tile(2 * NSA_WIDTH), per_b(seq), per_b(seq), per_b(seq), per_b(seq),
                  per_b(LANES), per_b(LANES), tile(LANES),
                  pl.BlockSpec(expand_t.shape, lambda b, i: (0, 0)),
                  pl.BlockSpec(overlap_t.shape, lambda b, i: (0, 0)),
                  pl.BlockSpec(onehot.shape, lambda b, i: (0, 0)),
                  pl.BlockSpec(band.shape, lambda b, i: (0, 0, 0)),
                  pl.BlockSpec(causal.shape, lambda b, i: (0, 0, 0))],
        out_specs=tile(NSA_WIDTH),
        out_shape=jax.ShapeDtypeStruct((batch * seq, NSA_WIDTH), BF16),
        scratch_shapes=[pltpu.VMEM((rows, 2 * LANES), BF16),
                        pltpu.VMEM((2, rows, ATT_TK), F32), pltpu.VMEM((2, rows, ATT_TK), BF16),
                        pltpu.VMEM((rows, ATT_WK), F32), pltpu.VMEM((rows, ATT_WK), BF16),
                        f32_rows, f32_rows, f32_rows, f32_rows, f32_rows],
        compiler_params=_cparams("parallel", "parallel"),
        name="nsa_attention",
    )(q, ks, vs, kw, vw, kc, vc, gates, expand_t, overlap_t, onehot, band, causal)
```
